```python
import jax, jax.numpy as jnp
from jax import lax
import numpy as np

D_MODEL = 1024
BATCH = 8
SEQ = 8192
DEPTH = 1

CHUNK = 64
N_PREV_CHUNKS = 8
BAND = (N_PREV_CHUNKS + 1) * CHUNK
ATT_HEADS = 8
ATT_HEAD_DIM = 64
ATT_WIDTH = ATT_HEADS * ATT_HEAD_DIM
REL_CLIP = 128
SG_BLOCK = 128
SG_GROUPS = 8
SG_GROUP_DIM = 64
SG_WIDTH = SG_GROUPS * SG_GROUP_DIM
N_BRANCHES = 2
IN_COLS = 3 * ATT_WIDTH + 2 * SG_WIDTH + N_BRANCHES * D_MODEL
MEM_LEN = 256
XATT_HEADS = 4
XATT_HEAD_DIM = D_MODEL // XATT_HEADS
D_FF = -(-8 * D_MODEL // (3 * 256)) * 256
EPS = 1e-6
NEG_INF = -1e30

kernel_name = "hybrid_chunk_attn_sgu_block"


def rmsnorm(x, g):
    xf = x.astype(jnp.float32)
    y = xf * lax.rsqrt(jnp.mean(xf * xf, axis=-1, keepdims=True) + EPS)
    return (y * g.astype(jnp.float32)).astype(x.dtype)


def layernorm(x, g, b):
    xf = x.astype(jnp.float32)
    mu = jnp.mean(xf, axis=-1, keepdims=True)
    var = jnp.mean(jnp.square(xf - mu), axis=-1, keepdims=True)
    y = (xf - mu) * lax.rsqrt(var + EPS)
    return (y * g.astype(jnp.float32) + b.astype(jnp.float32)).astype(x.dtype)


def chunked_relpos_attention(q, k, v, rel_bias):
    B, S, H, Dh = q.shape
    nC = S // CHUNK
    q = (q * (Dh ** -0.5)).reshape(B, nC, CHUNK, H, Dh)
    k = k.reshape(B, nC, CHUNK, H, Dh)
    v = v.reshape(B, nC, CHUNK, H, Dh)
    pad = ((0, 0), (N_PREV_CHUNKS, 0), (0, 0), (0, 0), (0, 0))
    kp = jnp.pad(k, pad)
    vp = jnp.pad(v, pad)
    kb = jnp.stack([kp[:, j:j + nC] for j in range(N_PREV_CHUNKS + 1)], axis=2).reshape(B, nC, BAND, H, Dh)
    vb = jnp.stack([vp[:, j:j + nC] for j in range(N_PREV_CHUNKS + 1)], axis=2).reshape(B, nC, BAND, H, Dh)
    s = jnp.einsum('bcihd,bcmhd->bhcim', q, kb, preferred_element_type=jnp.float32)
    qi = np.arange(CHUNK)[:, None]
    mi = np.arange(BAND)[None, :]
    dist = qi - mi + N_PREV_CHUNKS * CHUNK
    idx = np.clip(dist, -REL_CLIP, REL_CLIP) + REL_CLIP
    bias = rel_bias[:, idx].astype(jnp.float32)
    s = s + bias[None, :, None]
    valid = (np.arange(nC)[:, None] - N_PREV_CHUNKS + np.arange(BAND)[None, :] // CHUNK) >= 0
    s = jnp.where(valid[None, None, :, None, :], s, NEG_INF)
    p = jax.nn.softmax(s, axis=-1)
    o = jnp.einsum('bhcim,bcmhd->bcihd', p.astype(vb.dtype), vb)
    return o.reshape(B, S, H * Dh)


def spatial_gating(u, v, ln_g, ln_b, w_s, b_s):
    B, S, _ = u.shape
    nB = S // SG_BLOCK
    v = v.reshape(B, nB, SG_BLOCK, SG_GROUPS, SG_GROUP_DIM)
    v = layernorm(v, ln_g, ln_b)
    t = np.arange(SG_BLOCK)
    mask = (t[None, :] // CHUNK) <= (t[:, None] // CHUNK)
    w = jnp.where(mask[None], w_s, 0.0)
    sv = jnp.einsum('gts,bnsgd->bntgd', w, v) + b_s.T[None, None, :, :, None]
    return (u.reshape(B, nB, SG_BLOCK, SG_GROUPS, SG_GROUP_DIM) * sv).reshape(B, S, SG_WIDTH)


def cross_attention(h, m, w_xq, w_xkv, w_xo):
    B, S, _ = h.shape
    M = m.shape[1]
    q = (h @ w_xq).reshape(B, S, XATT_HEADS, XATT_HEAD_DIM) * (XATT_HEAD_DIM ** -0.5)
    k, v = jnp.split(m @ w_xkv, 2, axis=-1)
    k = k.reshape(B, M, XATT_HEADS, XATT_HEAD_DIM)
    v = v.reshape(B, M, XATT_HEADS, XATT_HEAD_DIM)
    s = jnp.einsum('bshd,bmhd->bhsm', q, k, preferred_element_type=jnp.float32)
    p = jax.nn.softmax(s, axis=-1)
    o = jnp.einsum('bhsm,bmhd->bshd', p.astype(v.dtype), v).reshape(B, S, D_MODEL)
    return o @ w_xo


def _fwd_setup_inputs(seed: int = 0) -> dict:
    key = jax.random.key(seed)
    ks = jax.random.split(key, 24)
    f32 = jnp.float32
    nrm = lambda k, shape, scale: jax.random.normal(k, shape, f32) * scale
    L = DEPTH
    return {
        "x": nrm(ks[0], (BATCH, SEQ, D_MODEL), 1.0),
        "mem": nrm(ks[1], (BATCH, MEM_LEN, D_MODEL), 1.0),
        "norm_mix_g": 1.0 + nrm(ks[2], (L, D_MODEL), 0.02),
        "w_in": nrm(ks[3], (L, D_MODEL, IN_COLS), D_MODEL ** -0.5),
        "rel_bias": nrm(ks[4], (L, ATT_HEADS, 2 * REL_CLIP + 1), 0.5),
        "sg_ln_g": 1.0 + nrm(ks[5], (L, SG_GROUPS, SG_GROUP_DIM), 0.02),
        "sg_ln_b": nrm(ks[6], (L, SG_GROUPS, SG_GROUP_DIM), 0.02),
        "sg_w": nrm(ks[7], (L, SG_GROUPS, SG_BLOCK, SG_BLOCK), SG_BLOCK ** -0.5),
        "sg_b": 1.0 + nrm(ks[8], (L, SG_GROUPS, SG_BLOCK), 0.02),
        "w_branch_att": nrm(ks[9], (L, ATT_WIDTH, D_MODEL), ATT_WIDTH ** -0.5),
        "w_branch_sg": nrm(ks[10], (L, SG_WIDTH, D_MODEL), SG_WIDTH ** -0.5),
        "w_out": nrm(ks[11], (L, D_MODEL, D_MODEL), D_MODEL ** -0.5),
        "norm_xattn_g": 1.0 + nrm(ks[12], (L, D_MODEL), 0.02),
        "norm_mem_g": 1.0 + nrm(ks[13], (L, D_MODEL), 0.02),
        "w_xq": nrm(ks[14], (L, D_MODEL, D_MODEL), D_MODEL ** -0.5),
        "w_xkv": nrm(ks[15], (L, D_MODEL, 2 * D_MODEL), D_MODEL ** -0.5),
        "w_xo": nrm(ks[16], (L, D_MODEL, D_MODEL), D_MODEL ** -0.5),
        "norm_ffn_g": 1.0 + nrm(ks[17], (L, D_MODEL), 0.02),
        "w_ffn_in": nrm(ks[18], (L, D_MODEL, 2 * D_FF), D_MODEL ** -0.5),
        "w_ffn_out": nrm(ks[19], (L, D_FF, D_MODEL), D_FF ** -0.5),
        "norm_final_g": 1.0 + nrm(ks[20], (D_MODEL,), 0.02),
    }


def _fwd_reference(x, mem, norm_mix_g, w_in, rel_bias, sg_ln_g, sg_ln_b, sg_w, sg_b,
              w_branch_att, w_branch_sg, w_out, norm_xattn_g, norm_mem_g,
              w_xq, w_xkv, w_xo, norm_ffn_g, w_ffn_in, w_ffn_out, norm_final_g):
    B, S, _ = x.shape
    col = np.cumsum([ATT_WIDTH, ATT_WIDTH, ATT_WIDTH, SG_WIDTH, SG_WIDTH, D_MODEL])
    for l in range(DEPTH):
        h = rmsnorm(x, norm_mix_g[l])
        z = h @ w_in[l]
        q, k, v, u_sg, v_sg, g_a, g_b = jnp.split(z, col, axis=-1)
        q = q.reshape(B, S, ATT_HEADS, ATT_HEAD_DIM)
        k = k.reshape(B, S, ATT_HEADS, ATT_HEAD_DIM)
        v = v.reshape(B, S, ATT_HEADS, ATT_HEAD_DIM)
        y_att = chunked_relpos_attention(q, k, v, rel_bias[l])
        y_sg = spatial_gating(jax.nn.gelu(u_sg), jax.nn.gelu(v_sg),
                              sg_ln_g[l], sg_ln_b[l], sg_w[l], sg_b[l])
        merged = (jax.nn.sigmoid(g_a) * (y_att @ w_branch_att[l])
                  + jax.nn.sigmoid(g_b) * (y_sg @ w_branch_sg[l]))
        x = x + merged @ w_out[l]
        x = x + cross_attention(rmsnorm(x, norm_xattn_g[l]), rmsnorm(mem, norm_mem_g[l]),
                                w_xq[l], w_xkv[l], w_xo[l])
        gate, up = jnp.split(rmsnorm(x, norm_ffn_g[l]) @ w_ffn_in[l], 2, axis=-1)
        x = x + (jax.nn.silu(gate) * up) @ w_ffn_out[l]
    return rmsnorm(x, norm_final_g)


import jax as _jax
import jax.numpy as _jnp

TWIN_FORMAT = 'train_step'
FWD_PARAMS = ['x', 'mem', 'norm_mix_g', 'w_in', 'rel_bias', 'sg_ln_g', 'sg_ln_b', 'sg_w', 'sg_b', 'w_branch_att', 'w_branch_sg', 'w_out', 'norm_xattn_g', 'norm_mem_g', 'w_xq', 'w_xkv', 'w_xo', 'norm_ffn_g', 'w_ffn_in', 'w_ffn_out', 'norm_final_g']
TWIN_WEIGHTS = ['norm_mix_g', 'w_in', 'rel_bias', 'sg_ln_g', 'sg_ln_b', 'sg_w', 'sg_b', 'w_branch_att', 'w_branch_sg', 'w_out', 'norm_xattn_g', 'norm_mem_g', 'w_xq', 'w_xkv', 'w_xo', 'norm_ffn_g', 'w_ffn_in', 'w_ffn_out', 'norm_final_g']
TWIN_DIFF_INPUT = 'x'
TWIN_INPUTS = ['x', 'mem', 'norm_mix_g', 'w_in', 'rel_bias', 'sg_ln_g', 'sg_ln_b', 'sg_w', 'sg_b', 'w_branch_att', 'w_branch_sg', 'w_out', 'norm_xattn_g', 'norm_mem_g', 'w_xq', 'w_xkv', 'w_xo', 'norm_ffn_g', 'w_ffn_in', 'w_ffn_out', 'norm_final_g', 'loss_target', 'm_norm_mix_g', 'm_w_in', 'm_rel_bias', 'm_sg_ln_g', 'm_sg_ln_b', 'm_sg_w', 'm_sg_b', 'm_w_branch_att', 'm_w_branch_sg', 'm_w_out', 'm_norm_xattn_g', 'm_norm_mem_g', 'm_w_xq', 'm_w_xkv', 'm_w_xo', 'm_norm_ffn_g', 'm_w_ffn_in', 'm_w_ffn_out', 'm_norm_final_g', 'v_norm_mix_g', 'v_w_in', 'v_rel_bias', 'v_sg_ln_g', 'v_sg_ln_b', 'v_sg_w', 'v_sg_b', 'v_w_branch_att', 'v_w_branch_sg', 'v_w_out', 'v_norm_xattn_g', 'v_norm_mem_g', 'v_w_xq', 'v_w_xkv', 'v_w_xo', 'v_norm_ffn_g', 'v_w_ffn_in', 'v_w_ffn_out', 'v_norm_final_g']
TWIN_OUTPUTS = ['loss', 'grad_x', 'grad_norm_mix_g', 'grad_w_in', 'grad_rel_bias', 'grad_sg_ln_g', 'grad_sg_ln_b', 'grad_sg_w', 'grad_sg_b', 'grad_w_branch_att', 'grad_w_branch_sg', 'grad_w_out', 'grad_norm_xattn_g', 'grad_norm_mem_g', 'grad_w_xq', 'grad_w_xkv', 'grad_w_xo', 'grad_norm_ffn_g', 'grad_w_ffn_in', 'grad_w_ffn_out', 'grad_norm_final_g', 'delta_norm_mix_g', 'delta_w_in', 'delta_rel_bias', 'delta_sg_ln_g', 'delta_sg_ln_b', 'delta_sg_w', 'delta_sg_b', 'delta_w_branch_att', 'delta_w_branch_sg', 'delta_w_out', 'delta_norm_xattn_g', 'delta_norm_mem_g', 'delta_w_xq', 'delta_w_xkv', 'delta_w_xo', 'delta_norm_ffn_g', 'delta_w_ffn_in', 'delta_w_ffn_out', 'delta_norm_final_g', 'new_m_norm_mix_g', 'new_m_w_in', 'new_m_rel_bias', 'new_m_sg_ln_g', 'new_m_sg_ln_b', 'new_m_sg_w', 'new_m_sg_b', 'new_m_w_branch_att', 'new_m_w_branch_sg', 'new_m_w_out', 'new_m_norm_xattn_g', 'new_m_norm_mem_g', 'new_m_w_xq', 'new_m_w_xkv', 'new_m_w_xo', 'new_m_norm_ffn_g', 'new_m_w_ffn_in', 'new_m_w_ffn_out', 'new_m_norm_final_g', 'new_v_norm_mix_g', 'new_v_w_in', 'new_v_rel_bias', 'new_v_sg_ln_g', 'new_v_sg_ln_b', 'new_v_sg_w', 'new_v_sg_b', 'new_v_w_branch_att', 'new_v_w_branch_sg', 'new_v_w_out', 'new_v_norm_xattn_g', 'new_v_norm_mem_g', 'new_v_w_xq', 'new_v_w_xkv', 'new_v_w_xo', 'new_v_norm_ffn_g', 'new_v_w_ffn_in', 'new_v_w_ffn_out', 'new_v_norm_final_g']
TWIN_LEAF_KINDS = {'loss': 'loss', 'grad_x': 'grad_x', 'grad_norm_mix_g': 'grad_w', 'grad_w_in': 'grad_w', 'grad_rel_bias': 'grad_w', 'grad_sg_ln_g': 'grad_w', 'grad_sg_ln_b': 'grad_w', 'grad_sg_w': 'grad_w', 'grad_sg_b': 'grad_w', 'grad_w_branch_att': 'grad_w', 'grad_w_branch_sg': 'grad_w', 'grad_w_out': 'grad_w', 'grad_norm_xattn_g': 'grad_w', 'grad_norm_mem_g': 'grad_w', 'grad_w_xq': 'grad_w', 'grad_w_xkv': 'grad_w', 'grad_w_xo': 'grad_w', 'grad_norm_ffn_g': 'grad_w', 'grad_w_ffn_in': 'grad_w', 'grad_w_ffn_out': 'grad_w', 'grad_norm_final_g': 'grad_w', 'delta_norm_mix_g': 'delta_w', 'delta_w_in': 'delta_w', 'delta_rel_bias': 'delta_w', 'delta_sg_ln_g': 'delta_w', 'delta_sg_ln_b': 'delta_w', 'delta_sg_w': 'delta_w', 'delta_sg_b': 'delta_w', 'delta_w_branch_att': 'delta_w', 'delta_w_branch_sg': 'delta_w', 'delta_w_out': 'delta_w', 'delta_norm_xattn_g': 'delta_w', 'delta_norm_mem_g': 'delta_w', 'delta_w_xq': 'delta_w', 'delta_w_xkv': 'delta_w', 'delta_w_xo': 'delta_w', 'delta_norm_ffn_g': 'delta_w', 'delta_w_ffn_in': 'delta_w', 'delta_w_ffn_out': 'delta_w', 'delta_norm_final_g': 'delta_w', 'new_m_norm_mix_g': 'new_m', 'new_m_w_in': 'new_m', 'new_m_rel_bias': 'new_m', 'new_m_sg_ln_g': 'new_m', 'new_m_sg_ln_b': 'new_m', 'new_m_sg_w': 'new_m', 'new_m_sg_b': 'new_m', 'new_m_w_branch_att': 'new_m', 'new_m_w_branch_sg': 'new_m', 'new_m_w_out': 'new_m', 'new_m_norm_xattn_g': 'new_m', 'new_m_norm_mem_g': 'new_m', 'new_m_w_xq': 'new_m', 'new_m_w_xkv': 'new_m', 'new_m_w_xo': 'new_m', 'new_m_norm_ffn_g': 'new_m', 'new_m_w_ffn_in': 'new_m', 'new_m_w_ffn_out': 'new_m', 'new_m_norm_final_g': 'new_m', 'new_v_norm_mix_g': 'new_v', 'new_v_w_in': 'new_v', 'new_v_rel_bias': 'new_v', 'new_v_sg_ln_g': 'new_v', 'new_v_sg_ln_b': 'new_v', 'new_v_sg_w': 'new_v', 'new_v_sg_b': 'new_v', 'new_v_w_branch_att': 'new_v', 'new_v_w_branch_sg': 'new_v', 'new_v_w_out': 'new_v', 'new_v_norm_xattn_g': 'new_v', 'new_v_norm_mem_g': 'new_v', 'new_v_w_xq': 'new_v', 'new_v_w_xkv': 'new_v', 'new_v_w_xo': 'new_v', 'new_v_norm_ffn_g': 'new_v', 'new_v_w_ffn_in': 'new_v', 'new_v_w_ffn_out': 'new_v', 'new_v_norm_final_g': 'new_v'}


def _forward(args):
    return _fwd_reference(*[args[k] for k in FWD_PARAMS])


def _output_shape():
    def fwd():
        inp = _fwd_setup_inputs(0)
        return _fwd_reference(*[inp[k] for k in FWD_PARAMS])
    out = _jax.eval_shape(fwd)
    return out.shape, out.dtype

N_MICROBATCH = 1
ADAM_LR = 0.001
ADAM_B1 = 0.9
ADAM_B2 = 0.999
ADAM_EPS = 1e-08
ADAM_WD = 0.01
ADAM_STEP = 10
PER_EXAMPLE_BATCH_AXIS = {'x': 0, 'mem': 0, 'loss_target': 0}
SHARED_INPUTS = []
_WEIGHT_DTYPES = {'norm_mix_g': _jnp.float32, 'w_in': _jnp.float32, 'rel_bias': _jnp.float32, 'sg_ln_g': _jnp.float32, 'sg_ln_b': _jnp.float32, 'sg_w': _jnp.float32, 'sg_b': _jnp.float32, 'w_branch_att': _jnp.float32, 'w_branch_sg': _jnp.float32, 'w_out': _jnp.float32, 'norm_xattn_g': _jnp.float32, 'norm_mem_g': _jnp.float32, 'w_xq': _jnp.float32, 'w_xkv': _jnp.float32, 'w_xo': _jnp.float32, 'norm_ffn_g': _jnp.float32, 'w_ffn_in': _jnp.float32, 'w_ffn_out': _jnp.float32, 'norm_final_g': _jnp.float32}
MOMENT_SCALE = {'norm_mix_g': 1.664618e-01, 'w_in': 7.680339e-02, 'rel_bias': 1.444492e-02, 'sg_ln_g': 1.140827e-01, 'sg_ln_b': 1.184028e-01, 'sg_w': 7.696165e-02, 'sg_b': 9.707237e-02, 'w_branch_att': 2.228427e-02, 'w_branch_sg': 1.217595e-01, 'w_out': 1.198450e-01, 'norm_xattn_g': 2.682160e-02, 'norm_mem_g': 4.059104e-02, 'w_xq': 2.643040e-02, 'w_xkv': 2.661925e-02, 'w_xo': 2.686203e-02, 'norm_ffn_g': 1.718991e-01, 'w_ffn_in': 7.312620e-02, 'w_ffn_out': 1.194713e-01, 'norm_final_g': 6.403402e+01}


def _to_microbatches(a, axis):
    t = _jnp.moveaxis(a, axis, 0)
    t = t.reshape((N_MICROBATCH, t.shape[0] // N_MICROBATCH) + t.shape[1:])
    return _jnp.moveaxis(t, 1, axis + 1)


def setup_inputs(seed: int = 0) -> dict:
    inp = _fwd_setup_inputs(seed)
    key = _jax.random.fold_in(_jax.random.key(seed), 7919)
    shape, _ = _output_shape()
    out = dict(inp)
    out["loss_target"] = _jax.random.normal(_jax.random.fold_in(key, 0), shape, _jnp.float32)
    for i, name in enumerate(TWIN_WEIGHTS):
        w = inp[name].astype(_jnp.float32)
        if MOMENT_SCALE is None:
            s = _jnp.sqrt(_jnp.mean(_jnp.square(w)) + 1e-30)
        else:
            s = MOMENT_SCALE[name]
        km, kv = _jax.random.split(_jax.random.fold_in(key, i + 1))
        out[name] = w
        out["m_" + name] = s * _jax.random.normal(km, w.shape, _jnp.float32)
        out["v_" + name] = (s * s) * _jax.random.uniform(kv, w.shape, _jnp.float32, 0.5, 1.5)
    if N_MICROBATCH > 1:
        for name, axis in PER_EXAMPLE_BATCH_AXIS.items():
            out[name] = _to_microbatches(out[name], axis)
    return {'x': out['x'], 'mem': out['mem'], 'norm_mix_g': out['norm_mix_g'], 'w_in': out['w_in'], 'rel_bias': out['rel_bias'], 'sg_ln_g': out['sg_ln_g'], 'sg_ln_b': out['sg_ln_b'], 'sg_w': out['sg_w'], 'sg_b': out['sg_b'], 'w_branch_att': out['w_branch_att'], 'w_branch_sg': out['w_branch_sg'], 'w_out': out['w_out'], 'norm_xattn_g': out['norm_xattn_g'], 'norm_mem_g': out['norm_mem_g'], 'w_xq': out['w_xq'], 'w_xkv': out['w_xkv'], 'w_xo': out['w_xo'], 'norm_ffn_g': out['norm_ffn_g'], 'w_ffn_in': out['w_ffn_in'], 'w_ffn_out': out['w_ffn_out'], 'norm_final_g': out['norm_final_g'], 'loss_target': out['loss_target'], 'm_norm_mix_g': out['m_norm_mix_g'], 'm_w_in': out['m_w_in'], 'm_rel_bias': out['m_rel_bias'], 'm_sg_ln_g': out['m_sg_ln_g'], 'm_sg_ln_b': out['m_sg_ln_b'], 'm_sg_w': out['m_sg_w'], 'm_sg_b': out['m_sg_b'], 'm_w_branch_att': out['m_w_branch_att'], 'm_w_branch_sg': out['m_w_branch_sg'], 'm_w_out': out['m_w_out'], 'm_norm_xattn_g': out['m_norm_xattn_g'], 'm_norm_mem_g': out['m_norm_mem_g'], 'm_w_xq': out['m_w_xq'], 'm_w_xkv': out['m_w_xkv'], 'm_w_xo': out['m_w_xo'], 'm_norm_ffn_g': out['m_norm_ffn_g'], 'm_w_ffn_in': out['m_w_ffn_in'], 'm_w_ffn_out': out['m_w_ffn_out'], 'm_norm_final_g': out['m_norm_final_g'], 'v_norm_mix_g': out['v_norm_mix_g'], 'v_w_in': out['v_w_in'], 'v_rel_bias': out['v_rel_bias'], 'v_sg_ln_g': out['v_sg_ln_g'], 'v_sg_ln_b': out['v_sg_ln_b'], 'v_sg_w': out['v_sg_w'], 'v_sg_b': out['v_sg_b'], 'v_w_branch_att': out['v_w_branch_att'], 'v_w_branch_sg': out['v_w_branch_sg'], 'v_w_out': out['v_w_out'], 'v_norm_xattn_g': out['v_norm_xattn_g'], 'v_norm_mem_g': out['v_norm_mem_g'], 'v_w_xq': out['v_w_xq'], 'v_w_xkv': out['v_w_xkv'], 'v_w_xo': out['v_w_xo'], 'v_norm_ffn_g': out['v_norm_ffn_g'], 'v_w_ffn_in': out['v_w_ffn_in'], 'v_w_ffn_out': out['v_w_ffn_out'], 'v_norm_final_g': out['v_norm_final_g']}


def _loss(weights, diff, rest, loss_target):
    with _jax.named_scope("forward"):
        args = {**rest, TWIN_DIFF_INPUT: diff, **{k: w.astype(_WEIGHT_DTYPES[k]) for k, w in weights.items()}}
        y = _forward(args)
    with _jax.named_scope("loss_head"):
        err = _jnp.square(y.astype(_jnp.float32) - loss_target)
        return 0.5 * _jnp.sum(_jnp.mean(err, axis=-1)) if err.ndim else 0.5 * err


def _adamw(w, g, m, v):
    m = ADAM_B1 * m + (1.0 - ADAM_B1) * g
    v = ADAM_B2 * v + (1.0 - ADAM_B2) * _jnp.square(g)
    m_hat = m / (1.0 - ADAM_B1 ** ADAM_STEP)
    v_hat = v / (1.0 - ADAM_B2 ** ADAM_STEP)
    delta = -ADAM_LR * (m_hat / (_jnp.sqrt(v_hat) + ADAM_EPS) + ADAM_WD * w)
    return delta, m, v


def reference(x, mem, norm_mix_g, w_in, rel_bias, sg_ln_g, sg_ln_b, sg_w, sg_b, w_branch_att, w_branch_sg, w_out, norm_xattn_g, norm_mem_g, w_xq, w_xkv, w_xo, norm_ffn_g, w_ffn_in, w_ffn_out, norm_final_g, loss_target, m_norm_mix_g, m_w_in, m_rel_bias, m_sg_ln_g, m_sg_ln_b, m_sg_w, m_sg_b, m_w_branch_att, m_w_branch_sg, m_w_out, m_norm_xattn_g, m_norm_mem_g, m_w_xq, m_w_xkv, m_w_xo, m_norm_ffn_g, m_w_ffn_in, m_w_ffn_out, m_norm_final_g, v_norm_mix_g, v_w_in, v_rel_bias, v_sg_ln_g, v_sg_ln_b, v_sg_w, v_sg_b, v_w_branch_att, v_w_branch_sg, v_w_out, v_norm_xattn_g, v_norm_mem_g, v_w_xq, v_w_xkv, v_w_xo, v_norm_ffn_g, v_w_ffn_in, v_w_ffn_out, v_norm_final_g):
    given = dict(x=x, mem=mem, norm_mix_g=norm_mix_g, w_in=w_in, rel_bias=rel_bias, sg_ln_g=sg_ln_g, sg_ln_b=sg_ln_b, sg_w=sg_w, sg_b=sg_b, w_branch_att=w_branch_att, w_branch_sg=w_branch_sg, w_out=w_out, norm_xattn_g=norm_xattn_g, norm_mem_g=norm_mem_g, w_xq=w_xq, w_xkv=w_xkv, w_xo=w_xo, norm_ffn_g=norm_ffn_g, w_ffn_in=w_ffn_in, w_ffn_out=w_ffn_out, norm_final_g=norm_final_g, loss_target=loss_target, m_norm_mix_g=m_norm_mix_g, m_w_in=m_w_in, m_rel_bias=m_rel_bias, m_sg_ln_g=m_sg_ln_g, m_sg_ln_b=m_sg_ln_b, m_sg_w=m_sg_w, m_sg_b=m_sg_b, m_w_branch_att=m_w_branch_att, m_w_branch_sg=m_w_branch_sg, m_w_out=m_w_out, m_norm_xattn_g=m_norm_xattn_g, m_norm_mem_g=m_norm_mem_g, m_w_xq=m_w_xq, m_w_xkv=m_w_xkv, m_w_xo=m_w_xo, m_norm_ffn_g=m_norm_ffn_g, m_w_ffn_in=m_w_ffn_in, m_w_ffn_out=m_w_ffn_out, m_norm_final_g=m_norm_final_g, v_norm_mix_g=v_norm_mix_g, v_w_in=v_w_in, v_rel_bias=v_rel_bias, v_sg_ln_g=v_sg_ln_g, v_sg_ln_b=v_sg_ln_b, v_sg_w=v_sg_w, v_sg_b=v_sg_b, v_w_branch_att=v_w_branch_att, v_w_branch_sg=v_w_branch_sg, v_w_out=v_w_out, v_norm_xattn_g=v_norm_xattn_g, v_norm_mem_g=v_norm_mem_g, v_w_xq=v_w_xq, v_w_xkv=v_w_xkv, v_w_xo=v_w_xo, v_norm_ffn_g=v_norm_ffn_g, v_w_ffn_in=v_w_ffn_in, v_w_ffn_out=v_w_ffn_out, v_norm_final_g=v_norm_final_g)
    weights = {n: given[n] for n in TWIN_WEIGHTS}
    shared = {n: given[n] for n in SHARED_INPUTS}
    per_example = {n: given[n] for n in ['x', 'mem']}
    grad_fn = _jax.value_and_grad(_loss, argnums=(0, 1))

    def one_microbatch(ex, loss_target):
        ex = dict(ex)
        diff = ex.pop(TWIN_DIFF_INPUT)
        return grad_fn(weights, diff, {**shared, **ex}, loss_target)

    if N_MICROBATCH == 1:
        loss, (grad_w, grad_x) = one_microbatch(per_example, given["loss_target"])
    else:
        def body(carry, xs):
            loss_sum, grad_sum = carry
            l_k, (gw_k, gx_k) = one_microbatch(xs[0], xs[1])
            with _jax.named_scope("update"):
                return (loss_sum + l_k, _jax.tree.map(_jnp.add, grad_sum, gw_k)), gx_k

        init = (_jnp.zeros((), _jnp.float32), _jax.tree.map(_jnp.zeros_like, weights))
        (loss, grad_w), grad_x = _jax.lax.scan(body, init, (per_example, given["loss_target"]))
    with _jax.named_scope("update"):
        delta_w, new_m, new_v = {}, {}, {}
        for n in TWIN_WEIGHTS:
            delta_w[n], new_m[n], new_v[n] = _adamw(weights[n], grad_w[n], given["m_" + n], given["v_" + n])
    return (loss, grad_x, *[grad_w[n] for n in TWIN_WEIGHTS], *[delta_w[n] for n in TWIN_WEIGHTS],
            *[new_m[n] for n in TWIN_WEIGHTS], *[new_v[n] for n in TWIN_WEIGHTS])
```

```python
import numpy as np
import jax
import jax.numpy as jnp
from jax import lax
from jax.experimental import pallas as pl
from jax.experimental.pallas import tpu as pltpu

F32, BF16 = jnp.float32, jnp.bfloat16
MESH = pl.DeviceIdType.MESH

EPS = 1e-6
NEG_INF = -1e30
CHUNK = 64
N_PREV_CHUNKS = 8
ATT_HEADS, ATT_HEAD_DIM = 8, 64
REL_CLIP = 128
SG_BLOCK, SG_GROUPS, SG_GROUP_DIM = 128, 8, 64
XATT_HEADS, XATT_HEAD_DIM = 4, 256
ATT_TILE = 128
ATT_WIN = 5
ADAM_LR, ADAM_B1, ADAM_B2, ADAM_EPS, ADAM_WD, ADAM_STEP = 0.001, 0.9, 0.999, 1e-08, 0.01, 10

VMEM_LIMIT_V7X = 56 * 1024 * 1024
EW_BLOCK_BYTES = 1 << 20


def _params(*sem):
    return pltpu.CompilerParams(dimension_semantics=sem, vmem_limit_bytes=VMEM_LIMIT_V7X)


def _full(shape):
    n = len(shape)
    return pl.BlockSpec(shape, lambda *_: (0,) * n)


def _nt(a, b):
    return lax.dot_general(a, b, (((1,), (1,)), ((), ())), preferred_element_type=F32)


def _tn(a, b):
    return lax.dot_general(a, b, (((0,), (0,)), ((), ())), preferred_element_type=F32)


def _nn(a, b):
    return jnp.dot(a, b, preferred_element_type=F32)


def _sigmoid(x):
    return 1.0 / (1.0 + jnp.exp(-x))


_GELU_C = float(np.sqrt(2.0 / np.pi))


def _gelu(x):
    t = jnp.tanh(_GELU_C * (x + 0.044715 * (x * x * x)))
    return x * (0.5 * (1.0 + t))


def _gelu_grad(x):
    t = jnp.tanh(_GELU_C * (x + 0.044715 * (x * x * x)))
    return 0.5 * (1.0 + t) + 0.5 * x * (1.0 - t * t) * (_GELU_C * (1.0 + 3.0 * 0.044715 * x * x))


def _rms_stats(xf):
    rstd = lax.rsqrt(jnp.mean(xf * xf, axis=-1, keepdims=True) + EPS)
    return rstd, xf * rstd


def _rms_bwd(xhat, rstd, g, dh):
    dxh = dh * g
    dx = rstd * (dxh - xhat * jnp.mean(dxh * xhat, axis=-1, keepdims=True))
    return dx, dh * xhat


def norm_mm(x, g, w, *, tm, scale, name):
    S, D = x.shape
    J, _, C = w.shape

    def body(x_ref, g_ref, w_ref, h_ref, z_ref):
        _, xhat = _rms_stats(x_ref[...])
        h = (xhat * g_ref[...]).astype(BF16)
        h_ref[...] = h
        for j in range(J):
            acc = _nn(h, w_ref[j])
            if scale is not None:
                acc = acc * scale
            z_ref[:, j * C:(j + 1) * C] = acc.astype(BF16)

    return pl.pallas_call(
        body, grid=(S // tm,), name=name,
        in_specs=[pl.BlockSpec((tm, D), lambda i: (i, 0)), _full((1, D)), _full((J, D, C))],
        out_specs=[pl.BlockSpec((tm, D), lambda i: (i, 0)), pl.BlockSpec((tm, J * C), lambda i: (i, 0))],
        out_shape=[jax.ShapeDtypeStruct((S, D), BF16), jax.ShapeDtypeStruct((S, J * C), BF16)],
        compiler_params=_params("parallel"),
    )(x, g, w)


def _att_window_specs(nt, col):
    return [pl.BlockSpec((ATT_TILE, 512), lambda t, j=j: (jnp.clip(t - (ATT_WIN - 1) + j, 0, nt - 1), col))
            for j in range(ATT_WIN)]


def _att_scores(q, k, bias, valid):
    s = _nt(q, k) * (ATT_HEAD_DIM ** -0.5) + bias
    return jnp.where(valid, s, NEG_INF)


def _att_valid(t):
    kpos = lax.broadcasted_iota(jnp.int32, (ATT_TILE, ATT_WIN * ATT_TILE), 1) + (t - (ATT_WIN - 1)) * ATT_TILE
    return kpos >= 0


def attn_fwd(z, bias):
    S = z.shape[0]
    nt = S // ATT_TILE

    def body(q_ref, *refs):
        k_refs, v_refs = refs[:ATT_WIN], refs[ATT_WIN:2 * ATT_WIN]
        bias_ref, y_ref, lse_ref = refs[2 * ATT_WIN:]
        valid = _att_valid(pl.program_id(0))
        for h in range(ATT_HEADS):
            sl = slice(h * ATT_HEAD_DIM, (h + 1) * ATT_HEAD_DIM)
            k = jnp.concatenate([r[:, sl] for r in k_refs], axis=0)
            v = jnp.concatenate([r[:, sl] for r in v_refs], axis=0)
            s = _att_scores(q_ref[:, sl], k, bias_ref[h], valid)
            m = jnp.max(s, axis=-1, keepdims=True)
            p = jnp.exp(s - m)
            l = jnp.sum(p, axis=-1, keepdims=True)
            y_ref[:, sl] = (_nn(p.astype(BF16), v) / l).astype(BF16)
            lse_ref[:, h:h + 1] = m + jnp.log(l)

    tile = lambda col: pl.BlockSpec((ATT_TILE, 512), lambda t: (t, col))
    return pl.pallas_call(
        body, grid=(nt,), name="attn_fwd",
        in_specs=[tile(0)] + _att_window_specs(nt, 1) + _att_window_specs(nt, 2) + [_full(bias.shape)],
        out_specs=[tile(0), pl.BlockSpec((ATT_TILE, ATT_HEADS), lambda t: (t, 0))],
        out_shape=[jax.ShapeDtypeStruct((S, 512), BF16), jax.ShapeDtypeStruct((S, ATT_HEADS), F32)],
        compiler_params=_params("parallel"),
    )(z, *([z] * (2 * ATT_WIN)), bias)


def _sgu_group(zu_ref, zv_ref, rows, g, lng_ref, lnb_ref, wm_ref, sgb_ref):
    gsl = slice(g * SG_GROUP_DIM, (g + 1) * SG_GROUP_DIM)
    zu = zu_ref[rows, gsl].astype(F32)
    zv = zv_ref[rows, gsl].astype(F32)
    u, v = _gelu(zu), _gelu(zv)
    mu = jnp.mean(v, axis=-1, keepdims=True)
    vc = v - mu
    rstd = lax.rsqrt(jnp.mean(vc * vc, axis=-1, keepdims=True) + EPS)
    xhat = vc * rstd
    vn = xhat * lng_ref[g] + lnb_ref[g]
    sv = _nn(wm_ref[g], vn.astype(BF16)) + sgb_ref[g]
    return zu, zv, u, xhat, rstd, vn, sv


def sgu_fwd(z, lng, lnb, wm, sgb, *, tm):
    S = z.shape[0]

    def body(zu_ref, zv_ref, lng_ref, lnb_ref, wm_ref, sgb_ref, y_ref):
        for blk in range(tm // SG_BLOCK):
            rows = slice(blk * SG_BLOCK, (blk + 1) * SG_BLOCK)
            for g in range(SG_GROUPS):
                _, _, u, _, _, _, sv = _sgu_group(zu_ref, zv_ref, rows, g, lng_ref, lnb_ref, wm_ref, sgb_ref)
                y_ref[rows, g * SG_GROUP_DIM:(g + 1) * SG_GROUP_DIM] = (u * sv).astype(BF16)

    tile = lambda col: pl.BlockSpec((tm, 512), lambda i: (i, col))
    return pl.pallas_call(
        body, grid=(S // tm,), name="sgu_fwd",
        in_specs=[tile(3), tile(4), _full(lng.shape), _full(lnb.shape), _full(wm.shape), _full(sgb.shape)],
        out_specs=tile(0), out_shape=jax.ShapeDtypeStruct((S, 512), BF16),
        compiler_params=_params("parallel"),
    )(z, z, lng, lnb, wm, sgb)


def _gate_specs(tm):
    return [pl.BlockSpec((tm, 512), lambda i, c=c: (i, c)) for c in (5, 6, 7, 8)]


def merge_fwd(y_att, y_sg, z, x, w_ba, w_bs, w_out, *, tm):
    S, D = x.shape
    J, _, C = w_ba.shape

    def body(ya_ref, ys_ref, g0, g1, g2, g3, x_ref, wba_ref, wbs_ref, wo_ref, a_ref, b_ref, m_ref, x1_ref):
        ya, ys = ya_ref[...], ys_ref[...]
        a = jnp.concatenate([_nn(ya, wba_ref[j]) for j in range(J)], axis=1)
        b = jnp.concatenate([_nn(ys, wbs_ref[j]) for j in range(J)], axis=1)
        ga = jnp.concatenate([g0[...], g1[...]], axis=1).astype(F32)
        gb = jnp.concatenate([g2[...], g3[...]], axis=1).astype(F32)
        a_ref[...] = a.astype(BF16)
        b_ref[...] = b.astype(BF16)
        merged = (_sigmoid(ga) * a + _sigmoid(gb) * b).astype(BF16)
        m_ref[...] = merged
        x1_ref[...] = x_ref[...] + _nn(merged, wo_ref[...])

    row = lambda n: pl.BlockSpec((tm, n), lambda i: (i, 0))
    return pl.pallas_call(
        body, grid=(S // tm,), name="merge_fwd",
        in_specs=[row(512), row(512)] + _gate_specs(tm) + [row(D), _full(w_ba.shape), _full(w_bs.shape), _full(w_out.shape)],
        out_specs=[row(D)] * 4,
        out_shape=[jax.ShapeDtypeStruct((S, D), BF16)] * 3 + [jax.ShapeDtypeStruct((S, D), F32)],
        compiler_params=_params("parallel"),
    )(y_att, y_sg, z, z, z, z, x, w_ba, w_bs, w_out)


def xattn_fwd(xq, kv, w_xo, x1, *, tm):
    S, D = xq.shape
    dh = XATT_HEAD_DIM

    def body(q_ref, kv_ref, wo_ref, x1_ref, o_ref, lse_ref, x2_ref):
        outs = []
        for h in range(XATT_HEADS):
            s = _nt(q_ref[:, h * dh:(h + 1) * dh], kv_ref[:, h * dh:(h + 1) * dh])
            m = jnp.max(s, axis=-1, keepdims=True)
            p = jnp.exp(s - m)
            l = jnp.sum(p, axis=-1, keepdims=True)
            outs.append((_nn(p.astype(BF16), kv_ref[:, D + h * dh:D + (h + 1) * dh]) / l).astype(BF16))
            lse_ref[:, h:h + 1] = m + jnp.log(l)
        o = jnp.concatenate(outs, axis=1)
        o_ref[...] = o
        x2_ref[...] = x1_ref[...] + _nn(o, wo_ref[...])

    row = lambda n: pl.BlockSpec((tm, n), lambda i: (i, 0))
    return pl.pallas_call(
        body, grid=(S // tm,), name="xattn_fwd",
        in_specs=[row(D), _full(kv.shape), _full(w_xo.shape), row(D)],
        out_specs=[row(D), row(XATT_HEADS), row(D)],
        out_shape=[jax.ShapeDtypeStruct((S, D), BF16), jax.ShapeDtypeStruct((S, XATT_HEADS), F32),
                   jax.ShapeDtypeStruct((S, D), F32)],
        compiler_params=_params("parallel"),
    )(xq, kv, w_xo, x1)


def ffn_out_fwd(gu, w, x2, *, tm):
    S, D = x2.shape
    F = w.shape[0]

    def body(gu_ref, w_ref, x2_ref, act_ref, x3_ref):
        gate = gu_ref[:, :F].astype(F32)
        up = gu_ref[:, F:].astype(F32)
        act = (gate * _sigmoid(gate) * up).astype(BF16)
        act_ref[...] = act
        x3_ref[...] = x2_ref[...] + _nn(act, w_ref[...])

    row = lambda n: pl.BlockSpec((tm, n), lambda i: (i, 0))
    return pl.pallas_call(
        body, grid=(S // tm,), name="ffn_out_fwd",
        in_specs=[row(2 * F), _full(w.shape), row(D)], out_specs=[row(F), row(D)],
        out_shape=[jax.ShapeDtypeStruct((S, F), BF16), jax.ShapeDtypeStruct((S, D), F32)],
        compiler_params=_params("parallel"),
    )(gu, w, x2)


def loss_head(x3, g, target, *, tm):
    S, D = x3.shape

    def body(x_ref, g_ref, t_ref, loss_ref, dx_ref, dg_ref):
        @pl.when(pl.program_id(0) == 0)
        def _():
            loss_ref[...] = jnp.zeros_like(loss_ref)
            dg_ref[...] = jnp.zeros_like(dg_ref)

        gv = g_ref[...]
        rstd, xhat = _rms_stats(x_ref[...])
        err = xhat * gv - t_ref[...]
        loss_ref[...] += 0.5 * jnp.sum(jnp.mean(err * err, axis=-1, keepdims=True))
        dx, dgc = _rms_bwd(xhat, rstd, gv, err * (1.0 / D))
        dx_ref[...] = dx
        dg_ref[...] += jnp.sum(dgc, axis=0, keepdims=True)

    row = pl.BlockSpec((tm, D), lambda i: (i, 0))
    return pl.pallas_call(
        body, grid=(S // tm,), name="loss_head",
        in_specs=[row, _full((1, D)), row], out_specs=[_full((1, 128)), row, _full((1, D))],
        out_shape=[jax.ShapeDtypeStruct((1, 128), F32), jax.ShapeDtypeStruct((S, D), F32),
                   jax.ShapeDtypeStruct((1, D), F32)],
        compiler_params=_params("arbitrary"),
    )(x3, g, target)


def ffn_act_bwd(dx3, gu, w, *, tm, nchunk):
    S, D = dx3.shape
    F = w.shape[0]
    cn = F // nchunk

    def body(dx_ref, gu_ref, w_ref, dgu_ref):
        dxb = dx_ref[...].astype(BF16)
        for j in range(nchunk):
            dact = _nt(dxb, w_ref[j * cn:(j + 1) * cn, :])
            gate = gu_ref[:, j * cn:(j + 1) * cn].astype(F32)
            up = gu_ref[:, F + j * cn:F + (j + 1) * cn].astype(F32)
            sg = _sigmoid(gate)
            dgu_ref[:, j * cn:(j + 1) * cn] = (dact * up * (sg * (1.0 + gate * (1.0 - sg)))).astype(BF16)
            dgu_ref[:, F + j * cn:F + (j + 1) * cn] = (dact * (gate * sg)).astype(BF16)

    row = lambda n: pl.BlockSpec((tm, n), lambda i: (i, 0))
    return pl.pallas_call(
        body, grid=(S // tm,), name="ffn_act_bwd",
        in_specs=[row(D), row(2 * F), _full(w.shape)], out_specs=row(2 * F),
        out_shape=jax.ShapeDtypeStruct((S, 2 * F), BF16), compiler_params=_params("parallel"),
    )(dx3, gu, w)


def tn_mm(a, b, *, shards, tk, name):
    K, M = a.shape
    N = b.shape[1]
    tn = N // shards
    nk = K // tk

    def body(a_ref, b_ref, o_ref, acc_ref):
        k = pl.program_id(1)

        @pl.when(k == 0)
        def _():
            acc_ref[...] = jnp.zeros_like(acc_ref)

        acc_ref[...] += _tn(a_ref[...].astype(BF16), b_ref[...].astype(BF16))

        @pl.when(k == nk - 1)
        def _():
            o_ref[...] = acc_ref[...].astype(BF16)

    if shards > 1:
        out_spec = pl.BlockSpec((None, M, tn), lambda n, k: (n, 0, 0))
        out_shape = jax.ShapeDtypeStruct((shards, M, tn), BF16)
    else:
        out_spec = pl.BlockSpec((M, tn), lambda n, k: (0, n))
        out_shape = jax.ShapeDtypeStruct((M, N), BF16)
    return pl.pallas_call(
        body, grid=(shards, nk), name=name,
        in_specs=[pl.BlockSpec((tk, M), lambda n, k: (k, 0)), pl.BlockSpec((tk, tn), lambda n, k: (k, n))],
        out_specs=out_spec, out_shape=out_shape, scratch_shapes=[pltpu.VMEM((M, tn), F32)],
        compiler_params=_params("parallel", "arbitrary"),
    )(a, b)


def mm_nt_norm_bwd(dy, w, x, g, dx_in, *, tm, name):
    S, D = x.shape
    J, _, C = w.shape
    has_in = dx_in is not None

    def body(*refs):
        dy_ref, w_ref, x_ref, g_ref = refs[:4]
        dxin_ref = refs[4] if has_in else None
        dx_ref, dg_ref = refs[-2:]

        @pl.when(pl.program_id(0) == 0)
        def _():
            dg_ref[...] = jnp.zeros_like(dg_ref)

        dh = _nt(dy_ref[:, 0:C], w_ref[0])
        for j in range(1, J):
            dh += _nt(dy_ref[:, j * C:(j + 1) * C], w_ref[j])
        rstd, xhat = _rms_stats(x_ref[...])
        dx, dgc = _rms_bwd(xhat, rstd, g_ref[...], dh)
        dx_ref[...] = dx + dxin_ref[...] if has_in else dx
        dg_ref[...] += jnp.sum(dgc, axis=0, keepdims=True)

    row = lambda n: pl.BlockSpec((tm, n), lambda i: (i, 0))
    ins = [dy, w, x, g] + ([dx_in] if has_in else [])
    return pl.pallas_call(
        body, grid=(S // tm,), name=name,
        in_specs=[row(J * C), _full(w.shape), row(D), _full((1, D))] + ([row(D)] if has_in else []),
        out_specs=[row(D), _full((1, D))],
        out_shape=[jax.ShapeDtypeStruct((S, D), F32), jax.ShapeDtypeStruct((1, D), F32)],
        compiler_params=_params("arbitrary"),
    )(*ins)


def xattn_bwd(dx2, xq, o, lse, kv, w_xo, *, tm):
    S, D = xq.shape
    M = kv.shape[0]
    dh = XATT_HEAD_DIM

    def body(dx_ref, q_ref, o_ref, lse_ref, kv_ref, wo_ref, dq_ref, dkv_ref):
        @pl.when(pl.program_id(0) == 0)
        def _():
            dkv_ref[...] = jnp.zeros_like(dkv_ref)

        do = _nt(dx_ref[...].astype(BF16), wo_ref[...])
        for h in range(XATT_HEADS):
            hs = slice(h * dh, (h + 1) * dh)
            vs = slice(D + h * dh, D + (h + 1) * dh)
            q, k, v = q_ref[:, hs], kv_ref[:, hs], kv_ref[:, vs]
            do_h = do[:, hs]
            do_b = do_h.astype(BF16)
            p = jnp.exp(_nt(q, k) - lse_ref[:, h:h + 1])
            delta = jnp.sum(do_h * o_ref[:, hs].astype(F32), axis=-1, keepdims=True)
            ds = (p * (_nt(do_b, v) - delta)).astype(BF16)
            dq_ref[:, hs] = (_nn(ds, k) * (dh ** -0.5)).astype(BF16)
            dkv_ref[:, hs] += _tn(ds, q)
            dkv_ref[:, vs] += _tn(p.astype(BF16), do_b)

    row = lambda n: pl.BlockSpec((tm, n), lambda i: (i, 0))
    return pl.pallas_call(
        body, grid=(S // tm,), name="xattn_bwd",
        in_specs=[row(D), row(D), row(D), row(XATT_HEADS), _full(kv.shape), _full(w_xo.shape)],
        out_specs=[row(D), _full((M, 2 * D))],
        out_shape=[jax.ShapeDtypeStruct((S, D), BF16), jax.ShapeDtypeStruct((M, 2 * D), F32)],
        compiler_params=_params("arbitrary"),
    )(dx2, xq, o, lse, kv, w_xo)


def merge_bwd(dx1, w_out, a, b, z, w_ba, w_bs, *, tm):
    S, D = dx1.shape
    J, W, C = w_ba.shape

    def body(dx_ref, wo_ref, a_ref, b_ref, g0, g1, g2, g3, wba_ref, wbs_ref, da_ref, db_ref, dg_ref, dya_ref, dys_ref):
        dm = _nt(dx_ref[...].astype(BF16), wo_ref[...])
        sa = _sigmoid(jnp.concatenate([g0[...], g1[...]], axis=1).astype(F32))
        sb = _sigmoid(jnp.concatenate([g2[...], g3[...]], axis=1).astype(F32))
        dg_ref[:, :D] = (dm * a_ref[...].astype(F32) * (sa * (1.0 - sa))).astype(BF16)
        dg_ref[:, D:] = (dm * b_ref[...].astype(F32) * (sb * (1.0 - sb))).astype(BF16)
        da = (dm * sa).astype(BF16)
        db = (dm * sb).astype(BF16)
        da_ref[...] = da
        db_ref[...] = db
        dya = _nt(da[:, 0:C], wba_ref[0])
        dys = _nt(db[:, 0:C], wbs_ref[0])
        for j in range(1, J):
            dya += _nt(da[:, j * C:(j + 1) * C], wba_ref[j])
            dys += _nt(db[:, j * C:(j + 1) * C], wbs_ref[j])
        dya_ref[...] = dya.astype(BF16)
        dys_ref[...] = dys.astype(BF16)

    row = lambda n: pl.BlockSpec((tm, n), lambda i: (i, 0))
    return pl.pallas_call(
        body, grid=(S // tm,), name="merge_bwd",
        in_specs=[row(D), _full(w_out.shape), row(D), row(D)] + _gate_specs(tm) + [_full(w_ba.shape), _full(w_bs.shape)],
        out_specs=[row(D), row(D), row(2 * D), row(W), row(W)],
        out_shape=[jax.ShapeDtypeStruct((S, D), BF16)] * 2 + [jax.ShapeDtypeStruct((S, 2 * D), BF16)]
        + [jax.ShapeDtypeStruct((S, W), BF16)] * 2,
        compiler_params=_params("parallel"),
    )(dx1, w_out, a, b, z, z, z, z, w_ba, w_bs)


def attn_bwd(z, y, dy, lse, bias):
    S = z.shape[0]
    nt = S // ATT_TILE
    back = ATT_WIN - 1

    def body(q_ref, *refs):
        k_refs, v_refs = refs[:ATT_WIN], refs[ATT_WIN:2 * ATT_WIN]
        y_ref, dy_ref, lse_ref, bias_ref, dq_ref, dk_ref, dv_ref, dbias_ref, dk_acc, dv_acc = refs[2 * ATT_WIN:]
        t = pl.program_id(0)

        @pl.when(t == 0)
        def _():
            dbias_ref[...] = jnp.zeros_like(dbias_ref)
            dk_acc[...] = jnp.zeros_like(dk_acc)
            dv_acc[...] = jnp.zeros_like(dv_acc)

        @pl.when(t < nt)
        def _():
            valid = _att_valid(t)
            for h in range(ATT_HEADS):
                sl = slice(h * ATT_HEAD_DIM, (h + 1) * ATT_HEAD_DIM)
                k = jnp.concatenate([r[:, sl] for r in k_refs], axis=0)
                v = jnp.concatenate([r[:, sl] for r in v_refs], axis=0)
                q, do_b = q_ref[:, sl], dy_ref[:, sl]
                p = jnp.exp(_att_scores(q, k, bias_ref[h], valid) - lse_ref[:, h:h + 1])
                delta = jnp.sum(do_b.astype(F32) * y_ref[:, sl].astype(F32), axis=-1, keepdims=True)
                ds = p * (_nt(do_b, v) - delta)
                dbias_ref[h] += ds
                ds_b = ds.astype(BF16)
                dq_ref[:, sl] = (_nn(ds_b, k) * (ATT_HEAD_DIM ** -0.5)).astype(BF16)
                dk_w = _tn(ds_b, q) * (ATT_HEAD_DIM ** -0.5)
                dv_w = _tn(p.astype(BF16), do_b)
                for i in range(ATT_WIN):
                    slot = (t + 1 + i) % ATT_WIN
                    rows = slice(i * ATT_TILE, (i + 1) * ATT_TILE)
                    if i == back:
                        dk_acc[slot, :, sl] = dk_w[rows]
                        dv_acc[slot, :, sl] = dv_w[rows]
                    else:
                        dk_acc[slot, :, sl] += dk_w[rows]
                        dv_acc[slot, :, sl] += dv_w[rows]

        done = (t + 1) % ATT_WIN
        dk_ref[...] = dk_acc[done].astype(BF16)
        dv_ref[...] = dv_acc[done].astype(BF16)

    last = nt - 1
    tile = lambda col, n=512: pl.BlockSpec((ATT_TILE, n), lambda t: (jnp.minimum(t, last), col))
    late = pl.BlockSpec((ATT_TILE, 512), lambda t: (jnp.maximum(t - back, 0), 0))
    return pl.pallas_call(
        body, grid=(nt + back,), name="attn_bwd",
        in_specs=[tile(0)] + _att_window_specs(nt, 1) + _att_window_specs(nt, 2)
        + [tile(0), tile(0), tile(0, ATT_HEADS), _full(bias.shape)],
        out_specs=[tile(0), late, late, _full(bias.shape)],
        out_shape=[jax.ShapeDtypeStruct((S, 512), BF16)] * 3 + [jax.ShapeDtypeStruct(bias.shape, F32)],
        scratch_shapes=[pltpu.VMEM((ATT_WIN, ATT_TILE, 512), F32)] * 2,
        compiler_params=_params("arbitrary"),
    )(z, *([z] * (2 * ATT_WIN)), y, dy, lse, bias)


def sgu_bwd(z, dy, lng, lnb, wm, wmt, sgb, mask, *, tm):
    S = z.shape[0]
    n = S // tm

    def body(zu_ref, zv_ref, dy_ref, lng_ref, lnb_ref, wm_ref, wmt_ref, sgb_ref, mask_ref,
             duv_ref, dwm_ref, dsgb_ref, dlng_ref, dlnb_ref):
        i = pl.program_id(0)

        @pl.when(i == 0)
        def _():
            for r in (dwm_ref, dsgb_ref, dlng_ref, dlnb_ref):
                r[...] = jnp.zeros_like(r)

        for blk in range(tm // SG_BLOCK):
            rows = slice(blk * SG_BLOCK, (blk + 1) * SG_BLOCK)
            for g in range(SG_GROUPS):
                gsl = slice(g * SG_GROUP_DIM, (g + 1) * SG_GROUP_DIM)
                zu, zv, u, xhat, rstd, vn, sv = _sgu_group(zu_ref, zv_ref, rows, g, lng_ref, lnb_ref, wm_ref, sgb_ref)
                dyv = dy_ref[rows, gsl].astype(F32)
                duv_ref[rows, gsl] = (dyv * sv * _gelu_grad(zu)).astype(BF16)
                dsv = dyv * u
                dsv_b = dsv.astype(BF16)
                dsgb_ref[g] += jnp.sum(dsv, axis=-1, keepdims=True)
                dwm_ref[g] += _nt(dsv_b, vn.astype(BF16))
                dvn = _nn(wmt_ref[g], dsv_b)
                dlng_ref[g] += jnp.sum(dvn * xhat, axis=0, keepdims=True)
                dlnb_ref[g] += jnp.sum(dvn, axis=0, keepdims=True)
                dxh = dvn * lng_ref[g]
                dv = rstd * (dxh - jnp.mean(dxh, axis=-1, keepdims=True)
                             - xhat * jnp.mean(dxh * xhat, axis=-1, keepdims=True))
                duv_ref[rows, 512 + g * SG_GROUP_DIM:512 + (g + 1) * SG_GROUP_DIM] = (dv * _gelu_grad(zv)).astype(BF16)

        @pl.when(i == n - 1)
        def _():
            for g in range(SG_GROUPS):
                dwm_ref[g] = dwm_ref[g] * mask_ref[...]

    tile = lambda col: pl.BlockSpec((tm, 512), lambda i: (i, col))
    smalls = [lng, lnb, wm, wmt, sgb, mask]
    return pl.pallas_call(
        body, grid=(n,), name="sgu_bwd",
        in_specs=[tile(3), tile(4), tile(0)] + [_full(s.shape) for s in smalls],
        out_specs=[pl.BlockSpec((tm, 1024), lambda i: (i, 0)), _full(wm.shape), _full(sgb.shape), _full(lng.shape),
                   _full(lnb.shape)],
        out_shape=[jax.ShapeDtypeStruct((S, 1024), BF16), jax.ShapeDtypeStruct(wm.shape, F32),
                   jax.ShapeDtypeStruct(sgb.shape, F32), jax.ShapeDtypeStruct(lng.shape, F32),
                   jax.ShapeDtypeStruct(lnb.shape, F32)],
        compiler_params=_params("arbitrary"),
    )(z, z, dy, *smalls)


def _row_tile(rows, cols, n_arrays):
    best = None
    for tr in range(16, rows + 1, 16):
        if rows % tr == 0 and tr * cols * 4 <= EW_BLOCK_BYTES:
            best = tr
    return best if best is not None else rows


def sum_leading(buf, out_dtype, name):
    n, R, C = buf.shape
    tr = _row_tile(R, C, n + 1)

    def body(*refs):
        acc = refs[0][...].astype(F32)
        for r in refs[1:n]:
            acc = acc + r[...].astype(F32)
        refs[n][...] = acc.astype(out_dtype)

    return pl.pallas_call(
        body, grid=(R // tr,), name=name,
        in_specs=[pl.BlockSpec((None, tr, C), lambda i, k=k: (k, i, 0)) for k in range(n)],
        out_specs=pl.BlockSpec((tr, C), lambda i: (i, 0)), out_shape=jax.ShapeDtypeStruct((R, C), out_dtype),
        compiler_params=_params("parallel"),
    )(*([buf] * n))


def adamw(w, g, m, v, name):
    R, C = w.shape
    tr = _row_tile(R, C, 7)

    def body(w_ref, g_ref, m_ref, v_ref, d_ref, nm_ref, nv_ref):
        gv = g_ref[...]
        nm = ADAM_B1 * m_ref[...] + (1.0 - ADAM_B1) * gv
        nv = ADAM_B2 * v_ref[...] + (1.0 - ADAM_B2) * (gv * gv)
        m_hat = nm / (1.0 - ADAM_B1 ** ADAM_STEP)
        v_hat = nv / (1.0 - ADAM_B2 ** ADAM_STEP)
        d_ref[...] = -ADAM_LR * (m_hat / (jnp.sqrt(v_hat) + ADAM_EPS) + ADAM_WD * w_ref[...])
        nm_ref[...] = nm
        nv_ref[...] = nv

    spec = pl.BlockSpec((tr, C), lambda i: (i, 0))
    return pl.pallas_call(
        body, grid=(R // tr,), name=name, in_specs=[spec] * 4, out_specs=[spec] * 3,
        out_shape=[jax.ShapeDtypeStruct((R, C), F32)] * 3, compiler_params=_params("parallel"),
    )(w, g, m, v)


HBM = pl.BlockSpec(memory_space=pl.ANY)


def _place():
    x, y, c = lax.axis_index("x"), lax.axis_index("y"), lax.axis_index("c")
    chips = [(1 - x, y), (x, 1 - y), (1 - x, 1 - y)]
    return x, y, c, chips


def gather_weights(shards):
    n = len(shards)

    def body(*refs):
        src, dst = refs[:n], refs[n:2 * n]
        send_sems, recv_sems, local_sems = refs[2 * n:]
        x, y, c, chips = _place()
        own = 2 * x + y
        sibling = (x, y, 1 - c)

        def remote(w, k, s, d, to):
            return pltpu.make_async_remote_copy(src_ref=s, dst_ref=d, send_sem=send_sems.at[w, k],
                                                recv_sem=recv_sems.at[w, k], device_id=to, device_id_type=MESH)

        local = [pltpu.make_async_copy(src[w], dst[w].at[own], local_sems.at[w]) for w in range(n)]
        for cp in local:
            cp.start()
        sends = []
        for w in range(n):
            for k, chip in enumerate(chips):
                sends.append(remote(w, k, src[w].at[c], dst[w].at[own, c], (*chip, c)))
                sends[-1].start()
        for w in range(n):
            for k, (px, py) in enumerate(chips):
                theirs = dst[w].at[2 * px + py, c]
                remote(w, k, src[w].at[c], theirs, (px, py, c)).wait_recv()
                sends.append(remote(w, 3 + k, theirs, theirs, sibling))
                sends[-1].start()
        for w in range(n):
            for k, (px, py) in enumerate(chips):
                remote(w, 3 + k, src[w].at[c], dst[w].at[2 * px + py, 1 - c], sibling).wait_recv()
        for cp in sends:
            cp.wait_send()
        for cp in local:
            cp.wait()

    return pl.pallas_call(
        body, name="gather_weights", in_specs=[HBM] * n, out_specs=[HBM] * n,
        out_shape=[jax.ShapeDtypeStruct((4,) + s.shape, s.dtype) for s in shards],
        scratch_shapes=[pltpu.SemaphoreType.DMA((n, 6)), pltpu.SemaphoreType.DMA((n, 6)), pltpu.SemaphoreType.DMA((n,))],
    )(*shards)


def sibling_split(grads):
    n = len(grads)

    def body(*refs):
        src, dst = refs[:n], refs[n:2 * n]
        send_sems, recv_sems, local_sems = refs[2 * n:]
        x, y, c, _ = _place()
        sibling = (x, y, 1 - c)
        local = [pltpu.make_async_copy(src[w].at[:, c], dst[w].at[1], local_sems.at[w]) for w in range(n)]
        for cp in local:
            cp.start()
        sends = [pltpu.make_async_remote_copy(src_ref=src[w].at[:, 1 - c], dst_ref=dst[w].at[0], send_sem=send_sems.at[w],
                                              recv_sem=recv_sems.at[w], device_id=sibling, device_id_type=MESH)
                 for w in range(n)]
        for cp in sends:
            cp.start()
        for cp in sends:
            cp.wait()
        for cp in local:
            cp.wait()

    return pl.pallas_call(
        body, name="sibling_split", in_specs=[HBM] * n, out_specs=[HBM] * n,
        out_shape=[jax.ShapeDtypeStruct((2, 4) + g.shape[2:], g.dtype) for g in grads],
        scratch_shapes=[pltpu.SemaphoreType.DMA((n,))] * 3,
    )(*grads)


def chip_exchange(parts):
    n = len(parts)

    def body(*refs):
        src, dst = refs[:n], refs[n:2 * n]
        send_sems, recv_sems, local_sems = refs[2 * n:]
        x, y, c, chips = _place()
        own = 2 * x + y
        local = [pltpu.make_async_copy(src[w].at[own], dst[w].at[3], local_sems.at[w]) for w in range(n)]
        for cp in local:
            cp.start()
        sends = []
        for w in range(n):
            for k, (px, py) in enumerate(chips):
                sends.append(pltpu.make_async_remote_copy(
                    src_ref=src[w].at[2 * px + py], dst_ref=dst[w].at[k], send_sem=send_sems.at[w, k],
                    recv_sem=recv_sems.at[w, k], device_id=(px, py, c), device_id_type=MESH))
                sends[-1].start()
        for cp in sends:
            cp.wait()
        for cp in local:
            cp.wait()

    return pl.pallas_call(
        body, name="chip_exchange", in_specs=[HBM] * n, out_specs=[HBM] * n,
        out_shape=[jax.ShapeDtypeStruct(p.shape, p.dtype) for p in parts],
        scratch_shapes=[pltpu.SemaphoreType.DMA((n, 3)), pltpu.SemaphoreType.DMA((n, 3)), pltpu.SemaphoreType.DMA((n,))],
    )(*parts)


def sibling_join(halves):
    n = len(halves)

    def body(*refs):
        src, dst = refs[:n], refs[n:2 * n]
        send_sems, recv_sems, local_sems = refs[2 * n:]
        x, y, c, _ = _place()
        sibling = (x, y, 1 - c)
        local = [pltpu.make_async_copy(src[w], dst[w].at[c], local_sems.at[w]) for w in range(n)]
        for cp in local:
            cp.start()
        sends = [pltpu.make_async_remote_copy(src_ref=src[w], dst_ref=dst[w].at[c], send_sem=send_sems.at[w],
                                              recv_sem=recv_sems.at[w], device_id=sibling, device_id_type=MESH)
                 for w in range(n)]
        for cp in sends:
            cp.start()
        for w, cp in enumerate(sends):
            cp.wait_send()
            pltpu.make_async_remote_copy(src_ref=src[w], dst_ref=dst[w].at[1 - c], send_sem=send_sems.at[w],
                                         recv_sem=recv_sems.at[w], device_id=sibling, device_id_type=MESH).wait_recv()
        for cp in local:
            cp.wait()

    return pl.pallas_call(
        body, name="sibling_join", in_specs=[HBM] * n, out_specs=[HBM] * n,
        out_shape=[jax.ShapeDtypeStruct((2,) + h.shape, h.dtype) for h in halves],
        scratch_shapes=[pltpu.SemaphoreType.DMA((n,))] * 3,
    )(*halves)


def allreduce_small(v):
    R, C = v.shape

    def body(v_ref, out_ref, all_ref, send_sems, recv_sems):
        x, y, c, chips = _place()
        me, sibling = (x, y, c), (x, y, 1 - c)

        def slot(px, py, pc):
            return all_ref.at[4 * px + 2 * py + pc]

        def copy(k, block, to, src=None):
            return pltpu.make_async_remote_copy(src_ref=slot(*block) if src is None else src, dst_ref=slot(*block),
                                                send_sem=send_sems.at[k], recv_sem=recv_sems.at[k], device_id=to,
                                                device_id_type=MESH)

        first = [copy(0, me, sibling, src=v_ref)] + [copy(1 + j, me, (*chip, c), src=v_ref) for j, chip in enumerate(chips)]
        for cp in first:
            cp.start()
        slot(*me)[...] = v_ref[...]
        passed = [copy(4 + j, (*chip, c), sibling) for j, chip in enumerate(chips)]
        for j, chip in enumerate(chips):
            copy(1 + j, (*chip, c), me).wait_recv()
            passed[j].start()
        copy(0, sibling, me).wait_recv()
        for j, chip in enumerate(chips):
            copy(4 + j, (*chip, 1 - c), me).wait_recv()
        for cp in first + passed:
            cp.wait_send()
        acc = all_ref[0]
        for k in range(1, 8):
            acc = acc + all_ref[k]
        out_ref[...] = acc

    vmem = pl.BlockSpec(memory_space=pltpu.VMEM)
    return pl.pallas_call(
        body, name="allreduce_small", in_specs=[vmem], out_specs=vmem, out_shape=jax.ShapeDtypeStruct((R, C), F32),
        scratch_shapes=[pltpu.VMEM((8, R, C), F32), pltpu.SemaphoreType.DMA((7,)), pltpu.SemaphoreType.DMA((7,))],
        compiler_params=pltpu.CompilerParams(vmem_limit_bytes=VMEM_LIMIT_V7X),
    )(v)


SMALL_PAD = 1024


def _pack_small(arrs):
    parts = []
    for a in arrs:
        f = a.reshape(-1).astype(F32)
        parts.append(jnp.pad(f, (0, (-f.shape[0]) % SMALL_PAD)))
    return jnp.concatenate(parts).reshape(-1, 128)


def _unpack_small(packed, shapes):
    flat = packed.reshape(-1)
    outs, off = [], 0
    for s in shapes:
        size = int(np.prod(s))
        outs.append(flat[off:off + size].reshape(s))
        off += size + (-size) % SMALL_PAD
    return outs


def _att_bias_table(rel_bias):
    i = np.arange(ATT_TILE)[:, None]
    m = np.arange(ATT_WIN * ATT_TILE)[None, :]
    idx = np.clip(i - m + (ATT_WIN - 1) * ATT_TILE, -REL_CLIP, REL_CLIP) + REL_CLIP
    qc, kc = i // CHUNK, m // CHUNK
    band = (kc >= qc) & (kc <= qc + N_PREV_CHUNKS)
    return jnp.where(band[None], rel_bias[:, idx], NEG_INF), idx


def kernel(x, mem, norm_mix_g, w_in, rel_bias, sg_ln_g, sg_ln_b, sg_w, sg_b, w_branch_att, w_branch_sg, w_out, norm_xattn_g, norm_mem_g, w_xq, w_xkv, w_xo, norm_ffn_g, w_ffn_in, w_ffn_out, norm_final_g, loss_target, m_norm_mix_g, m_w_in, m_rel_bias, m_sg_ln_g, m_sg_ln_b, m_sg_w, m_sg_b, m_w_branch_att, m_w_branch_sg, m_w_out, m_norm_xattn_g, m_norm_mem_g, m_w_xq, m_w_xkv, m_w_xo, m_norm_ffn_g, m_w_ffn_in, m_w_ffn_out, m_norm_final_g, v_norm_mix_g, v_w_in, v_rel_bias, v_sg_ln_g, v_sg_ln_b, v_sg_w, v_sg_b, v_w_branch_att, v_w_branch_sg, v_w_out, v_norm_xattn_g, v_norm_mem_g, v_w_xq, v_w_xkv, v_w_xo, v_norm_ffn_g, v_w_ffn_in, v_w_ffn_out, v_norm_final_g):
    S, D = x.shape[1], x.shape[2]
    x2d, mem2d, tgt = x[0], mem[0], loss_target[0]

    big_names = ["w_in", "w_branch_att", "w_branch_sg", "w_out", "w_xq", "w_xkv", "w_xo", "w_ffn_in", "w_ffn_out"]
    col_sharded = [True, True, True, False, False, True, False, True, False]
    big_w = [a[0] for a in (w_in, w_branch_att, w_branch_sg, w_out, w_xq, w_xkv, w_xo, w_ffn_in, w_ffn_out)]
    big_m = [a[0] for a in (m_w_in, m_w_branch_att, m_w_branch_sg, m_w_out, m_w_xq, m_w_xkv, m_w_xo, m_w_ffn_in, m_w_ffn_out)]
    big_v = [a[0] for a in (v_w_in, v_w_branch_att, v_w_branch_sg, v_w_out, v_w_xq, v_w_xkv, v_w_xo, v_w_ffn_in, v_w_ffn_out)]

    gathered = gather_weights([w.astype(BF16).reshape(2, w.shape[0] // 2, w.shape[1]) for w in big_w])
    full = []
    for g4, w, cs in zip(gathered, big_w, col_sharded):
        R, C = w.shape
        full.append(g4.reshape(4, R, C) if cs else g4.reshape(4 * R, C))
    W_in, W_ba, W_bs, W_out, W_xq, W_xkv, W_xo, W_fi, W_fo = full

    g_mix, g_xat, g_mem, g_ffn = norm_mix_g, norm_xattn_g, norm_mem_g, norm_ffn_g
    g_fin = norm_final_g.reshape(1, D)
    bias, bias_idx = _att_bias_table(rel_bias[0])
    tt = np.arange(SG_BLOCK)
    sg_mask = (tt[None, :] // CHUNK) <= (tt[:, None] // CHUNK)
    wm_f = jnp.where(sg_mask[None], sg_w[0], 0.0)
    wm, wmt = wm_f.astype(BF16), jnp.swapaxes(wm_f, 1, 2).astype(BF16)
    lng, lnb = sg_ln_g[0].reshape(SG_GROUPS, 1, SG_GROUP_DIM), sg_ln_b[0].reshape(SG_GROUPS, 1, SG_GROUP_DIM)
    sgb = sg_b[0].reshape(SG_GROUPS, SG_BLOCK, 1)
    mask_f = jnp.asarray(sg_mask, F32)

    h1, z = norm_mm(x2d, g_mix, W_in, tm=256, scale=None, name="norm_mm_in")
    y_att, lse_att = attn_fwd(z, bias)
    y_sg = sgu_fwd(z, lng, lnb, wm, sgb, tm=256)
    a_br, b_br, merged, x1 = merge_fwd(y_att, y_sg, z, x2d, W_ba, W_bs, W_out, tm=256)
    h2, xq = norm_mm(x1, g_xat, W_xq.reshape(1, D, D), tm=512, scale=XATT_HEAD_DIM ** -0.5, name="norm_mm_xq")
    mn, kv = norm_mm(mem2d, g_mem, W_xkv, tm=mem2d.shape[0], scale=None, name="norm_mm_kv")
    o_x, lse_x, x2 = xattn_fwd(xq, kv, W_xo, x1, tm=512)
    h3, gu = norm_mm(x2, g_ffn, W_fi, tm=256, scale=None, name="norm_mm_ffn")
    act, x3 = ffn_out_fwd(gu, W_fo, x2, tm=256)
    loss_vec, dx3, dg_fin = loss_head(x3, g_fin, tgt, tm=512)

    dgu = ffn_act_bwd(dx3, gu, W_fo, tm=256, nchunk=2)
    gW_fo = tn_mm(act, dx3, shards=1, tk=512, name="dw_ffn_out")
    gW_fi = tn_mm(h3, dgu, shards=4, tk=512, name="dw_ffn_in")
    dx2, dg_ffn = mm_nt_norm_bwd(dgu, W_fi, x2, g_ffn, dx3, tm=256, name="dx_ffn")
    dq_x, dkv = xattn_bwd(dx2, xq, o_x, lse_x, kv, W_xo, tm=512)
    gW_xo = tn_mm(o_x, dx2, shards=1, tk=512, name="dw_xo")
    gW_xq = tn_mm(h2, dq_x, shards=1, tk=512, name="dw_xq")
    dx1, dg_xat = mm_nt_norm_bwd(dq_x, W_xq.reshape(1, D, D), x1, g_xat, dx2, tm=512, name="dx_xq")
    gW_xkv = tn_mm(mn, dkv, shards=4, tk=mem2d.shape[0], name="dw_xkv")
    _, dg_mem = mm_nt_norm_bwd(dkv.astype(BF16), W_xkv, mem2d, g_mem, None, tm=mem2d.shape[0], name="dx_mem")
    da, db, dgab, dy_att, dy_sg = merge_bwd(dx1, W_out, a_br, b_br, z, W_ba, W_bs, tm=256)
    gW_out = tn_mm(merged, dx1, shards=1, tk=512, name="dw_out")
    gW_ba = tn_mm(y_att, da, shards=4, tk=512, name="dw_branch_att")
    gW_bs = tn_mm(y_sg, db, shards=4, tk=512, name="dw_branch_sg")
    dq_a, dk_a, dv_a, dbias = attn_bwd(z, y_att, dy_att, lse_att, bias)
    duv, dwm, dsgb, dlng, dlnb = sgu_bwd(z, dy_sg, lng, lnb, wm, wmt, sgb, mask_f, tm=256)
    dz = jnp.concatenate([dq_a, dk_a, dv_a, duv, dgab], axis=1)
    gW_in = tn_mm(h1, dz, shards=4, tk=512, name="dw_in")
    dx, dg_mix = mm_nt_norm_bwd(dz, W_in, x2d, g_mix, dx1, tm=256, name="dx_in")
    d_rel = jnp.zeros((ATT_HEADS, 2 * REL_CLIP + 1), F32).at[:, bias_idx].add(dbias)

    big_g = [gW_in, gW_ba, gW_bs, gW_out, gW_xq, gW_xkv, gW_xo, gW_fi, gW_fo]
    canon = []
    for g, w in zip(big_g, big_w):
        R, C = w.shape
        canon.append(g.reshape(4, 2, R // 2, C))
    pairs = sibling_split(canon)
    parts = [sum_leading(p.reshape(2, 4 * p.shape[2], p.shape[3]), BF16, "chip_partial_" + nm).reshape(p.shape[1:])
             for p, nm in zip(pairs, big_names)]
    recvd = chip_exchange(parts)
    halves = [sum_leading(r, F32, "shard_sum_" + nm) for r, nm in zip(recvd, big_names)]
    joined = sibling_join(halves)
    big_out = []
    for j, w, m_, v_, nm in zip(joined, big_w, big_m, big_v, big_names):
        g = j.reshape(w.shape)
        big_out.append((g,) + tuple(adamw(w, g, m_, v_, "adamw_" + nm)))

    small_w = [norm_mix_g, rel_bias, sg_ln_g, sg_ln_b, sg_w, sg_b, norm_xattn_g, norm_mem_g, norm_ffn_g, norm_final_g]
    small_m = [m_norm_mix_g, m_rel_bias, m_sg_ln_g, m_sg_ln_b, m_sg_w, m_sg_b, m_norm_xattn_g, m_norm_mem_g, m_norm_ffn_g, m_norm_final_g]
    small_v = [v_norm_mix_g, v_rel_bias, v_sg_ln_g, v_sg_ln_b, v_sg_w, v_sg_b, v_norm_xattn_g, v_norm_mem_g, v_norm_ffn_g, v_norm_final_g]
    small_g = [dg_mix, d_rel, dlng, dlnb, dwm, dsgb, dg_xat, dg_mem, dg_ffn, dg_fin]
    shapes = [w.shape for w in small_w]
    g_sum = allreduce_small(_pack_small(small_g + [loss_vec[0, :1]]))
    zero = jnp.zeros((1,), F32)
    d_s, m_s, v_s = adamw(_pack_small(small_w + [zero]), g_sum, _pack_small(small_m + [zero]), _pack_small(small_v + [zero]),
                          "adamw_small")
    sg_, sd, sm, sv_ = (_unpack_small(p, shapes + [(1,)]) for p in (g_sum, d_s, m_s, v_s))
    loss = sg_[-1][0]

    order = ["norm_mix_g", "w_in", "rel_bias", "sg_ln_g", "sg_ln_b", "sg_w", "sg_b", "w_branch_att", "w_branch_sg", "w_out",
             "norm_xattn_g", "norm_mem_g", "w_xq", "w_xkv", "w_xo", "norm_ffn_g", "w_ffn_in", "w_ffn_out", "norm_final_g"]
    small_names = ["norm_mix_g", "rel_bias", "sg_ln_g", "sg_ln_b", "sg_w", "sg_b", "norm_xattn_g", "norm_mem_g", "norm_ffn_g",
                   "norm_final_g"]
    res = {}
    for i, nm in enumerate(small_names):
        res[nm] = (sg_[i], sd[i], sm[i], sv_[i])
    for nm, outs in zip(big_names, big_out):
        res[nm] = tuple(o[None] for o in outs)
    return (loss, dx[None], *[res[nm][0] for nm in order], *[res[nm][1] for nm in order],
            *[res[nm][2] for nm in order], *[res[nm][3] for nm in order])
```

```python
import numpy as np
import jax
import jax.numpy as jnp
from jax import lax
from jax.experimental import pallas as pl
from jax.experimental.pallas import tpu as pltpu

F32, BF16 = jnp.float32, jnp.bfloat16
MESH = pl.DeviceIdType.MESH

EPS = 1e-6
NEG_INF = -1e30
CHUNK = 64
N_PREV_CHUNKS = 8
ATT_HEADS, ATT_HEAD_DIM = 8, 64
REL_CLIP = 128
SG_BLOCK, SG_GROUPS, SG_GROUP_DIM = 128, 8, 64
XATT_HEADS, XATT_HEAD_DIM = 4, 256
ATT_TILE = 128
ATT_WIN = 5
ADAM_LR, ADAM_B1, ADAM_B2, ADAM_EPS, ADAM_WD, ADAM_STEP = 0.001, 0.9, 0.999, 1e-08, 0.01, 10

VMEM_LIMIT_V7X = 56 * 1024 * 1024
EW_BLOCK_BYTES = 1 << 20


def _params(*sem):
    return pltpu.CompilerParams(dimension_semantics=sem, vmem_limit_bytes=VMEM_LIMIT_V7X)


def _full(shape):
    n = len(shape)
    return pl.BlockSpec(shape, lambda *_: (0,) * n)


def _nt(a, b):
    return lax.dot_general(a, b, (((1,), (1,)), ((), ())), preferred_element_type=F32)


def _tn(a, b):
    return lax.dot_general(a, b, (((0,), (0,)), ((), ())), preferred_element_type=F32)


def _nn(a, b):
    return jnp.dot(a, b, preferred_element_type=F32)


def _sigmoid(x):
    return 1.0 / (1.0 + jnp.exp(-x))


_GELU_C = float(np.sqrt(2.0 / np.pi))


def _gelu(x):
    t = jnp.tanh(_GELU_C * (x + 0.044715 * (x * x * x)))
    return x * (0.5 * (1.0 + t))


def _gelu_grad(x):
    t = jnp.tanh(_GELU_C * (x + 0.044715 * (x * x * x)))
    return 0.5 * (1.0 + t) + 0.5 * x * (1.0 - t * t) * (_GELU_C * (1.0 + 3.0 * 0.044715 * x * x))


def _rms_stats(xf):
    rstd = lax.rsqrt(jnp.mean(xf * xf, axis=-1, keepdims=True) + EPS)
    return rstd, xf * rstd


def _rms_bwd(xhat, rstd, g, dh):
    dxh = dh * g
    dx = rstd * (dxh - xhat * jnp.mean(dxh * xhat, axis=-1, keepdims=True))
    return dx, dh * xhat


def norm_mm(x, g, w, *, tm, scale, name):
    S, D = x.shape
    J, _, C = w.shape

    def body(x_ref, g_ref, w_ref, h_ref, z_ref):
        _, xhat = _rms_stats(x_ref[...])
        h = (xhat * g_ref[...]).astype(BF16)
        h_ref[...] = h
        for j in range(J):
            acc = _nn(h, w_ref[j])
            if scale is not None:
                acc = acc * scale
            z_ref[:, j * C:(j + 1) * C] = acc.astype(BF16)

    return pl.pallas_call(
        body, grid=(S // tm,), name=name,
        in_specs=[pl.BlockSpec((tm, D), lambda i: (i, 0)), _full((1, D)), _full((J, D, C))],
        out_specs=[pl.BlockSpec((tm, D), lambda i: (i, 0)), pl.BlockSpec((tm, J * C), lambda i: (i, 0))],
        out_shape=[jax.ShapeDtypeStruct((S, D), BF16), jax.ShapeDtypeStruct((S, J * C), BF16)],
        compiler_params=_params("parallel"),
    )(x, g, w)


def _att_window_specs(nt, col):
    return [pl.BlockSpec((ATT_TILE, 512), lambda t, j=j: (jnp.clip(t - (ATT_WIN - 1) + j, 0, nt - 1), col))
            for j in range(ATT_WIN)]


def _att_scores(q, k, bias, valid):
    s = _nt(q, k) * (ATT_HEAD_DIM ** -0.5) + bias
    return jnp.where(valid, s, NEG_INF)


def _att_valid(t):
    kpos = lax.broadcasted_iota(jnp.int32, (ATT_TILE, ATT_WIN * ATT_TILE), 1) + (t - (ATT_WIN - 1)) * ATT_TILE
    return kpos >= 0


def attn_fwd(z, bias):
    S = z.shape[0]
    nt = S // ATT_TILE

    def body(q_ref, *refs):
        k_refs, v_refs = refs[:ATT_WIN], refs[ATT_WIN:2 * ATT_WIN]
        bias_ref, y_ref, lse_ref = refs[2 * ATT_WIN:]
        valid = _att_valid(pl.program_id(0))
        for h in range(ATT_HEADS):
            sl = slice(h * ATT_HEAD_DIM, (h + 1) * ATT_HEAD_DIM)
            k = jnp.concatenate([r[:, sl] for r in k_refs], axis=0)
            v = jnp.concatenate([r[:, sl] for r in v_refs], axis=0)
            s = _att_scores(q_ref[:, sl], k, bias_ref[h], valid)
            m = jnp.max(s, axis=-1, keepdims=True)
            p = jnp.exp(s - m)
            l = jnp.sum(p, axis=-1, keepdims=True)
            y_ref[:, sl] = (_nn(p.astype(BF16), v) / l).astype(BF16)
            lse_ref[:, h:h + 1] = m + jnp.log(l)

    tile = lambda col: pl.BlockSpec((ATT_TILE, 512), lambda t: (t, col))
    return pl.pallas_call(
        body, grid=(nt,), name="attn_fwd",
        in_specs=[tile(0)] + _att_window_specs(nt, 1) + _att_window_specs(nt, 2) + [_full(bias.shape)],
        out_specs=[tile(0), pl.BlockSpec((ATT_TILE, ATT_HEADS), lambda t: (t, 0))],
        out_shape=[jax.ShapeDtypeStruct((S, 512), BF16), jax.ShapeDtypeStruct((S, ATT_HEADS), F32)],
        compiler_params=_params("parallel"),
    )(z, *([z] * (2 * ATT_WIN)), bias)


def _group_mean(x, a_ref):
    hi = x.astype(BF16)
    lo = (x - hi.astype(F32)).astype(BF16)
    return _nn(hi, a_ref[...]) + _nn(lo, a_ref[...])


def _block_diag(x, ref):
    group = lax.broadcasted_iota(jnp.int32, x.shape, 1) // SG_GROUP_DIM
    for g in range(SG_GROUPS):
        ref[g * SG_BLOCK:(g + 1) * SG_BLOCK, :] = jnp.where(group == g, x, 0.0).astype(BF16)


def _sgu_block(zu_ref, zv_ref, rows, a_ref, lng_ref, lnb_ref, wcat_ref, bfull_ref, vbd_ref):
    zu = zu_ref[rows, :].astype(F32)
    zv = zv_ref[rows, :].astype(F32)
    u, v = _gelu(zu), _gelu(zv)
    vc = v - _group_mean(v, a_ref)
    rstd = lax.rsqrt(_group_mean(vc * vc, a_ref) + EPS)
    xhat = vc * rstd
    _block_diag(xhat * lng_ref[...] + lnb_ref[...], vbd_ref)
    sv = _nn(wcat_ref[...], vbd_ref[...]) + bfull_ref[...]
    return zu, zv, u, xhat, rstd, sv


def _sgu_rows(b):
    return pl.ds(pl.multiple_of(b * SG_BLOCK, SG_BLOCK), SG_BLOCK)


def sgu_fwd(z, amean, lng, lnb, wcat, bfull, *, tm):
    S = z.shape[0]

    def body(zu_ref, zv_ref, a_ref, lng_ref, lnb_ref, wcat_ref, bfull_ref, y_ref, vbd_ref):
        def block(b, carry):
            rows = _sgu_rows(b)
            _, _, u, _, _, sv = _sgu_block(zu_ref, zv_ref, rows, a_ref, lng_ref, lnb_ref, wcat_ref, bfull_ref, vbd_ref)
            y_ref[rows, :] = (u * sv).astype(BF16)
            return carry

        lax.fori_loop(0, tm // SG_BLOCK, block, 0)

    tile = lambda col: pl.BlockSpec((tm, 512), lambda i: (i, col))
    smalls = [amean, lng, lnb, wcat, bfull]
    return pl.pallas_call(
        body, grid=(S // tm,), name="sgu_fwd",
        in_specs=[tile(3), tile(4)] + [_full(a.shape) for a in smalls],
        out_specs=tile(0), out_shape=jax.ShapeDtypeStruct((S, 512), BF16),
        scratch_shapes=[pltpu.VMEM((SG_GROUPS * SG_BLOCK, 512), BF16)],
        compiler_params=_params("parallel"),
    )(z, z, *smalls)


def _gate_specs(tm):
    return [pl.BlockSpec((tm, 512), lambda i, c=c: (i, c)) for c in (5, 6, 7, 8)]


def merge_fwd(y_att, y_sg, z, x, w_ba, w_bs, w_out, *, tm):
    S, D = x.shape
    J, _, C = w_ba.shape

    def body(ya_ref, ys_ref, g0, g1, g2, g3, x_ref, wba_ref, wbs_ref, wo_ref, a_ref, b_ref, m_ref, x1_ref):
        ya, ys = ya_ref[...], ys_ref[...]
        a = jnp.concatenate([_nn(ya, wba_ref[j]) for j in range(J)], axis=1)
        b = jnp.concatenate([_nn(ys, wbs_ref[j]) for j in range(J)], axis=1)
        ga = jnp.concatenate([g0[...], g1[...]], axis=1).astype(F32)
        gb = jnp.concatenate([g2[...], g3[...]], axis=1).astype(F32)
        a_ref[...] = a.astype(BF16)
        b_ref[...] = b.astype(BF16)
        merged = (_sigmoid(ga) * a + _sigmoid(gb) * b).astype(BF16)
        m_ref[...] = merged
        x1_ref[...] = x_ref[...] + _nn(merged, wo_ref[...])

    row = lambda n: pl.BlockSpec((tm, n), lambda i: (i, 0))
    return pl.pallas_call(
        body, grid=(S // tm,), name="merge_fwd",
        in_specs=[row(512), row(512)] + _gate_specs(tm) + [row(D), _full(w_ba.shape), _full(w_bs.shape), _full(w_out.shape)],
        out_specs=[row(D)] * 4,
        out_shape=[jax.ShapeDtypeStruct((S, D), BF16)] * 3 + [jax.ShapeDtypeStruct((S, D), F32)],
        compiler_params=_params("parallel"),
    )(y_att, y_sg, z, z, z, z, x, w_ba, w_bs, w_out)


def xattn_fwd(xq, kv, w_xo, x1, *, tm):
    S, D = xq.shape
    dh = XATT_HEAD_DIM

    def body(q_ref, kv_ref, wo_ref, x1_ref, o_ref, lse_ref, x2_ref):
        outs = []
        for h in range(XATT_HEADS):
            s = _nt(q_ref[:, h * dh:(h + 1) * dh], kv_ref[:, h * dh:(h + 1) * dh])
            m = jnp.max(s, axis=-1, keepdims=True)
            p = jnp.exp(s - m)
            l = jnp.sum(p, axis=-1, keepdims=True)
            outs.append((_nn(p.astype(BF16), kv_ref[:, D + h * dh:D + (h + 1) * dh]) / l).astype(BF16))
            lse_ref[:, h:h + 1] = m + jnp.log(l)
        o = jnp.concatenate(outs, axis=1)
        o_ref[...] = o
        x2_ref[...] = x1_ref[...] + _nn(o, wo_ref[...])

    row = lambda n: pl.BlockSpec((tm, n), lambda i: (i, 0))
    return pl.pallas_call(
        body, grid=(S // tm,), name="xattn_fwd",
        in_specs=[row(D), _full(kv.shape), _full(w_xo.shape), row(D)],
        out_specs=[row(D), row(XATT_HEADS), row(D)],
        out_shape=[jax.ShapeDtypeStruct((S, D), BF16), jax.ShapeDtypeStruct((S, XATT_HEADS), F32),
                   jax.ShapeDtypeStruct((S, D), F32)],
        compiler_params=_params("parallel"),
    )(xq, kv, w_xo, x1)


def ffn_out_fwd(gu, w, x2, *, tm):
    S, D = x2.shape
    F = w.shape[0]

    def body(gu_ref, w_ref, x2_ref, act_ref, x3_ref):
        gate = gu_ref[:, :F].astype(F32)
        up = gu_ref[:, F:].astype(F32)
        act = (gate * _sigmoid(gate) * up).astype(BF16)
        act_ref[...] = act
        x3_ref[...] = x2_ref[...] + _nn(act, w_ref[...])

    row = lambda n: pl.BlockSpec((tm, n), lambda i: (i, 0))
    return pl.pallas_call(
        body, grid=(S // tm,), name="ffn_out_fwd",
        in_specs=[row(2 * F), _full(w.shape), row(D)], out_specs=[row(F), row(D)],
        out_shape=[jax.ShapeDtypeStruct((S, F), BF16), jax.ShapeDtypeStruct((S, D), F32)],
        compiler_params=_params("parallel"),
    )(gu, w, x2)


def loss_head(x3, g, target, *, tm):
    S, D = x3.shape

    def body(x_ref, g_ref, t_ref, loss_ref, dx_ref, dg_ref):
        @pl.when(pl.program_id(0) == 0)
        def _():
            loss_ref[...] = jnp.zeros_like(loss_ref)
            dg_ref[...] = jnp.zeros_like(dg_ref)

        gv = g_ref[...]
        rstd, xhat = _rms_stats(x_ref[...])
        err = xhat * gv - t_ref[...]
        loss_ref[...] += 0.5 * jnp.sum(jnp.mean(err * err, axis=-1, keepdims=True))
        dx, dgc = _rms_bwd(xhat, rstd, gv, err * (1.0 / D))
        dx_ref[...] = dx
        dg_ref[...] += jnp.sum(dgc, axis=0, keepdims=True)

    row = pl.BlockSpec((tm, D), lambda i: (i, 0))
    return pl.pallas_call(
        body, grid=(S // tm,), name="loss_head",
        in_specs=[row, _full((1, D)), row], out_specs=[_full((1, 128)), row, _full((1, D))],
        out_shape=[jax.ShapeDtypeStruct((1, 128), F32), jax.ShapeDtypeStruct((S, D), F32),
                   jax.ShapeDtypeStruct((1, D), F32)],
        compiler_params=_params("arbitrary"),
    )(x3, g, target)


def ffn_act_bwd(dx3, gu, w, *, tm, nchunk):
    S, D = dx3.shape
    F = w.shape[0]
    cn = F // nchunk

    def body(dx_ref, gu_ref, w_ref, dgu_ref):
        dxb = dx_ref[...].astype(BF16)
        for j in range(nchunk):
            dact = _nt(dxb, w_ref[j * cn:(j + 1) * cn, :])
            gate = gu_ref[:, j * cn:(j + 1) * cn].astype(F32)
            up = gu_ref[:, F + j * cn:F + (j + 1) * cn].astype(F32)
            sg = _sigmoid(gate)
            dgu_ref[:, j * cn:(j + 1) * cn] = (dact * up * (sg * (1.0 + gate * (1.0 - sg)))).astype(BF16)
            dgu_ref[:, F + j * cn:F + (j + 1) * cn] = (dact * (gate * sg)).astype(BF16)

    row = lambda n: pl.BlockSpec((tm, n), lambda i: (i, 0))
    return pl.pallas_call(
        body, grid=(S // tm,), name="ffn_act_bwd",
        in_specs=[row(D), row(2 * F), _full(w.shape)], out_specs=row(2 * F),
        out_shape=jax.ShapeDtypeStruct((S, 2 * F), BF16), compiler_params=_params("parallel"),
    )(dx3, gu, w)


def tn_mm(a, b, *, shards, tk, name):
    K, M = a.shape
    N = b.shape[1]
    tn = N // shards
    nk = K // tk

    def body(a_ref, b_ref, o_ref, acc_ref):
        k = pl.program_id(1)

        @pl.when(k == 0)
        def _():
            acc_ref[...] = jnp.zeros_like(acc_ref)

        acc_ref[...] += _tn(a_ref[...].astype(BF16), b_ref[...].astype(BF16))

        @pl.when(k == nk - 1)
        def _():
            o_ref[...] = acc_ref[...].astype(BF16)

    if shards > 1:
        out_spec = pl.BlockSpec((None, M, tn), lambda n, k: (n, 0, 0))
        out_shape = jax.ShapeDtypeStruct((shards, M, tn), BF16)
    else:
        out_spec = pl.BlockSpec((M, tn), lambda n, k: (0, n))
        out_shape = jax.ShapeDtypeStruct((M, N), BF16)
    return pl.pallas_call(
        body, grid=(shards, nk), name=name,
        in_specs=[pl.BlockSpec((tk, M), lambda n, k: (k, 0)), pl.BlockSpec((tk, tn), lambda n, k: (k, n))],
        out_specs=out_spec, out_shape=out_shape, scratch_shapes=[pltpu.VMEM((M, tn), F32)],
        compiler_params=_params("parallel", "arbitrary"),
    )(a, b)


def mm_nt_norm_bwd(dy, w, x, g, dx_in, *, tm, name):
    S, D = x.shape
    J, _, C = w.shape
    has_in = dx_in is not None

    def body(*refs):
        dy_ref, w_ref, x_ref, g_ref = refs[:4]
        dxin_ref = refs[4] if has_in else None
        dx_ref, dg_ref = refs[-2:]

        @pl.when(pl.program_id(0) == 0)
        def _():
            dg_ref[...] = jnp.zeros_like(dg_ref)

        dh = _nt(dy_ref[:, 0:C], w_ref[0])
        for j in range(1, J):
            dh += _nt(dy_ref[:, j * C:(j + 1) * C], w_ref[j])
        rstd, xhat = _rms_stats(x_ref[...])
        dx, dgc = _rms_bwd(xhat, rstd, g_ref[...], dh)
        dx_ref[...] = dx + dxin_ref[...] if has_in else dx
        dg_ref[...] += jnp.sum(dgc, axis=0, keepdims=True)

    row = lambda n: pl.BlockSpec((tm, n), lambda i: (i, 0))
    ins = [dy, w, x, g] + ([dx_in] if has_in else [])
    return pl.pallas_call(
        body, grid=(S // tm,), name=name,
        in_specs=[row(J * C), _full(w.shape), row(D), _full((1, D))] + ([row(D)] if has_in else []),
        out_specs=[row(D), _full((1, D))],
        out_shape=[jax.ShapeDtypeStruct((S, D), F32), jax.ShapeDtypeStruct((1, D), F32)],
        compiler_params=_params("arbitrary"),
    )(*ins)


def xattn_bwd(dx2, xq, o, lse, kv, w_xo, *, tm):
    S, D = xq.shape
    M = kv.shape[0]
    dh = XATT_HEAD_DIM

    def body(dx_ref, q_ref, o_ref, lse_ref, kv_ref, wo_ref, dq_ref, dkv_ref):
        @pl.when(pl.program_id(0) == 0)
        def _():
            dkv_ref[...] = jnp.zeros_like(dkv_ref)

        do = _nt(dx_ref[...].astype(BF16), wo_ref[...])
        for h in range(XATT_HEADS):
            hs = slice(h * dh, (h + 1) * dh)
            vs = slice(D + h * dh, D + (h + 1) * dh)
            q, k, v = q_ref[:, hs], kv_ref[:, hs], kv_ref[:, vs]
            do_h = do[:, hs]
            do_b = do_h.astype(BF16)
            p = jnp.exp(_nt(q, k) - lse_ref[:, h:h + 1])
            delta = jnp.sum(do_h * o_ref[:, hs].astype(F32), axis=-1, keepdims=True)
            ds = (p * (_nt(do_b, v) - delta)).astype(BF16)
            dq_ref[:, hs] = (_nn(ds, k) * (dh ** -0.5)).astype(BF16)
            dkv_ref[:, hs] += _tn(ds, q)
            dkv_ref[:, vs] += _tn(p.astype(BF16), do_b)

    row = lambda n: pl.BlockSpec((tm, n), lambda i: (i, 0))
    return pl.pallas_call(
        body, grid=(S // tm,), name="xattn_bwd",
        in_specs=[row(D), row(D), row(D), row(XATT_HEADS), _full(kv.shape), _full(w_xo.shape)],
        out_specs=[row(D), _full((M, 2 * D))],
        out_shape=[jax.ShapeDtypeStruct((S, D), BF16), jax.ShapeDtypeStruct((M, 2 * D), F32)],
        compiler_params=_params("arbitrary"),
    )(dx2, xq, o, lse, kv, w_xo)


def merge_bwd(dx1, w_out, a, b, z, w_ba, w_bs, *, tm):
    S, D = dx1.shape
    J, W, C = w_ba.shape

    def body(dx_ref, wo_ref, a_ref, b_ref, g0, g1, g2, g3, wba_ref, wbs_ref, da_ref, db_ref, dg_ref, dya_ref, dys_ref):
        dm = _nt(dx_ref[...].astype(BF16), wo_ref[...])
        sa = _sigmoid(jnp.concatenate([g0[...], g1[...]], axis=1).astype(F32))
        sb = _sigmoid(jnp.concatenate([g2[...], g3[...]], axis=1).astype(F32))
        dg_ref[:, :D] = (dm * a_ref[...].astype(F32) * (sa * (1.0 - sa))).astype(BF16)
        dg_ref[:, D:] = (dm * b_ref[...].astype(F32) * (sb * (1.0 - sb))).astype(BF16)
        da = (dm * sa).astype(BF16)
        db = (dm * sb).astype(BF16)
        da_ref[...] = da
        db_ref[...] = db
        dya = _nt(da[:, 0:C], wba_ref[0])
        dys = _nt(db[:, 0:C], wbs_ref[0])
        for j in range(1, J):
            dya += _nt(da[:, j * C:(j + 1) * C], wba_ref[j])
            dys += _nt(db[:, j * C:(j + 1) * C], wbs_ref[j])
        dya_ref[...] = dya.astype(BF16)
        dys_ref[...] = dys.astype(BF16)

    row = lambda n: pl.BlockSpec((tm, n), lambda i: (i, 0))
    return pl.pallas_call(
        body, grid=(S // tm,), name="merge_bwd",
        in_specs=[row(D), _full(w_out.shape), row(D), row(D)] + _gate_specs(tm) + [_full(w_ba.shape), _full(w_bs.shape)],
        out_specs=[row(D), row(D), row(2 * D), row(W), row(W)],
        out_shape=[jax.ShapeDtypeStruct((S, D), BF16)] * 2 + [jax.ShapeDtypeStruct((S, 2 * D), BF16)]
        + [jax.ShapeDtypeStruct((S, W), BF16)] * 2,
        compiler_params=_params("parallel"),
    )(dx1, w_out, a, b, z, z, z, z, w_ba, w_bs)


def attn_bwd(z, y, dy, lse, bias):
    S = z.shape[0]
    nt = S // ATT_TILE
    back = ATT_WIN - 1

    def body(q_ref, *refs):
        k_refs, v_refs = refs[:ATT_WIN], refs[ATT_WIN:2 * ATT_WIN]
        y_ref, dy_ref, lse_ref, bias_ref, dq_ref, dk_ref, dv_ref, dbias_ref, dk_acc, dv_acc = refs[2 * ATT_WIN:]
        t = pl.program_id(0)

        @pl.when(t == 0)
        def _():
            dbias_ref[...] = jnp.zeros_like(dbias_ref)
            dk_acc[...] = jnp.zeros_like(dk_acc)
            dv_acc[...] = jnp.zeros_like(dv_acc)

        @pl.when(t < nt)
        def _():
            valid = _att_valid(t)
            for h in range(ATT_HEADS):
                sl = slice(h * ATT_HEAD_DIM, (h + 1) * ATT_HEAD_DIM)
                k = jnp.concatenate([r[:, sl] for r in k_refs], axis=0)
                v = jnp.concatenate([r[:, sl] for r in v_refs], axis=0)
                q, do_b = q_ref[:, sl], dy_ref[:, sl]
                p = jnp.exp(_att_scores(q, k, bias_ref[h], valid) - lse_ref[:, h:h + 1])
                delta = jnp.sum(do_b.astype(F32) * y_ref[:, sl].astype(F32), axis=-1, keepdims=True)
                ds = p * (_nt(do_b, v) - delta)
                dbias_ref[h] += ds
                ds_b = ds.astype(BF16)
                dq_ref[:, sl] = (_nn(ds_b, k) * (ATT_HEAD_DIM ** -0.5)).astype(BF16)
                dk_w = _tn(ds_b, q) * (ATT_HEAD_DIM ** -0.5)
                dv_w = _tn(p.astype(BF16), do_b)
                for i in range(ATT_WIN):
                    slot = (t + 1 + i) % ATT_WIN
                    rows = slice(i * ATT_TILE, (i + 1) * ATT_TILE)
                    if i == back:
                        dk_acc[slot, :, sl] = dk_w[rows]
                        dv_acc[slot, :, sl] = dv_w[rows]
                    else:
                        dk_acc[slot, :, sl] += dk_w[rows]
                        dv_acc[slot, :, sl] += dv_w[rows]

        done = (t + 1) % ATT_WIN
        dk_ref[...] = dk_acc[done].astype(BF16)
        dv_ref[...] = dv_acc[done].astype(BF16)

    last = nt - 1
    tile = lambda col, n=512: pl.BlockSpec((ATT_TILE, n), lambda t: (jnp.minimum(t, last), col))
    late = pl.BlockSpec((ATT_TILE, 512), lambda t: (jnp.maximum(t - back, 0), 0))
    return pl.pallas_call(
        body, grid=(nt + back,), name="attn_bwd",
        in_specs=[tile(0)] + _att_window_specs(nt, 1) + _att_window_specs(nt, 2)
        + [tile(0), tile(0), tile(0, ATT_HEADS), _full(bias.shape)],
        out_specs=[tile(0), late, late, _full(bias.shape)],
        out_shape=[jax.ShapeDtypeStruct((S, 512), BF16)] * 3 + [jax.ShapeDtypeStruct(bias.shape, F32)],
        scratch_shapes=[pltpu.VMEM((ATT_WIN, ATT_TILE, 512), F32)] * 2,
        compiler_params=_params("arbitrary"),
    )(z, *([z] * (2 * ATT_WIN)), y, dy, lse, bias)


def sgu_bwd(z, dy, amean, lng, lnb, wcat, wtcat, bfull, maskcat, *, tm):
    S = z.shape[0]
    n = S // tm

    def body(zu_ref, zv_ref, dy_ref, a_ref, lng_ref, lnb_ref, wcat_ref, wtcat_ref, bfull_ref, mask_ref,
             duv_ref, dw_ref, dsgb_ref, dlng_ref, dlnb_ref, vbd_ref, dbd_ref, w_acc, b_acc, g_acc, s_acc):
        i = pl.program_id(0)

        @pl.when(i == 0)
        def _():
            for r in (w_acc, b_acc, g_acc, s_acc):
                r[...] = jnp.zeros_like(r)

        def block(b, carry):
            rows = _sgu_rows(b)
            zu, zv, u, xhat, rstd, sv = _sgu_block(zu_ref, zv_ref, rows, a_ref, lng_ref, lnb_ref, wcat_ref, bfull_ref, vbd_ref)
            dyv = dy_ref[rows, :].astype(F32)
            duv_ref[rows, 0:512] = (dyv * sv * _gelu_grad(zu)).astype(BF16)
            dsv = dyv * u
            b_acc[...] += dsv
            w_acc[...] += _nt(dsv.astype(BF16), vbd_ref[...])
            _block_diag(dsv, dbd_ref)
            dvn = _nn(wtcat_ref[...], dbd_ref[...])
            g_acc[...] += dvn * xhat
            s_acc[...] += dvn
            dxh = dvn * lng_ref[...]
            dv = rstd * (dxh - _group_mean(dxh, a_ref) - xhat * _group_mean(dxh * xhat, a_ref))
            duv_ref[rows, 512:1024] = (dv * _gelu_grad(zv)).astype(BF16)
            return carry

        lax.fori_loop(0, tm // SG_BLOCK, block, 0)

        @pl.when(i == n - 1)
        def _():
            dw_ref[...] = w_acc[...] * mask_ref[...]
            dsgb_ref[...] = _group_mean(b_acc[...], a_ref) * float(SG_GROUP_DIM)
            dlng_ref[...] = jnp.sum(g_acc[...], axis=0, keepdims=True)
            dlnb_ref[...] = jnp.sum(s_acc[...], axis=0, keepdims=True)

    tile = lambda col: pl.BlockSpec((tm, 512), lambda i: (i, col))
    smalls = [amean, lng, lnb, wcat, wtcat, bfull, maskcat]
    wide = SG_GROUPS * SG_BLOCK
    return pl.pallas_call(
        body, grid=(n,), name="sgu_bwd",
        in_specs=[tile(3), tile(4), tile(0)] + [_full(a.shape) for a in smalls],
        out_specs=[pl.BlockSpec((tm, 1024), lambda i: (i, 0)), _full((SG_BLOCK, wide)), _full((SG_BLOCK, 512)),
                   _full((1, 512)), _full((1, 512))],
        out_shape=[jax.ShapeDtypeStruct((S, 1024), BF16), jax.ShapeDtypeStruct((SG_BLOCK, wide), F32),
                   jax.ShapeDtypeStruct((SG_BLOCK, 512), F32), jax.ShapeDtypeStruct((1, 512), F32),
                   jax.ShapeDtypeStruct((1, 512), F32)],
        scratch_shapes=[pltpu.VMEM((wide, 512), BF16), pltpu.VMEM((wide, 512), BF16), pltpu.VMEM((SG_BLOCK, wide), F32),
                        pltpu.VMEM((SG_BLOCK, 512), F32), pltpu.VMEM((SG_BLOCK, 512), F32), pltpu.VMEM((SG_BLOCK, 512), F32)],
        compiler_params=_params("arbitrary"),
    )(z, z, dy, *smalls)


def bias_colsum(p):
    H, _, L = p.shape
    near_lo = (ATT_WIN - 1) * ATT_TILE - REL_CLIP + 1
    near_hi = ATT_WIN * ATT_TILE

    def body(p_ref, gw_ref, far_ref):
        k = lax.broadcasted_iota(jnp.int32, (1, L), 1)
        is_far = (k < near_lo) | (k >= near_hi)
        for h in range(H):
            g = jnp.sum(p_ref[h], axis=0, keepdims=True)
            gw_ref[h:h + 1, :] = g
            far_ref[h:h + 1, :] = jnp.zeros((1, 128), F32) + jnp.sum(jnp.where(is_far, g, 0.0))

    return pl.pallas_call(
        body, name="bias_colsum", in_specs=[_full(p.shape)], out_specs=[_full((H, L)), _full((H, 128))],
        out_shape=[jax.ShapeDtypeStruct((H, L), F32), jax.ShapeDtypeStruct((H, 128), F32)], grid=(1,),
        compiler_params=_params("arbitrary"),
    )(p)


def _row_tile(rows, cols):
    best = None
    for tr in range(16, rows + 1, 16):
        if rows % tr == 0 and tr * cols * 4 <= EW_BLOCK_BYTES:
            best = tr
    return best if best is not None else rows


def place_shard(w, own, name):
    R, C = w.shape
    r = R // 2
    tr = _row_tile(r, C)
    nr = r // tr

    def body(own_ref, w_ref, o_ref):
        o_ref[...] = w_ref[...].astype(BF16)

    return pl.pallas_call(
        body, name=name,
        grid_spec=pltpu.PrefetchScalarGridSpec(
            num_scalar_prefetch=1, grid=(2, nr),
            in_specs=[pl.BlockSpec((tr, C), lambda h, i, own: (h * nr + i, 0))],
            out_specs=pl.BlockSpec((None, None, tr, C), lambda h, i, own: (own[0], h, i, 0))),
        out_shape=jax.ShapeDtypeStruct((4, 2, r, C), BF16), compiler_params=_params("parallel", "parallel"),
    )(own, w)


def chip_partial(grad, recv, core, name):
    _, _, r, C = grad.shape
    tr = _row_tile(r, C)

    def body(core_ref, g_ref, r_ref, o_ref):
        o_ref[...] = (g_ref[...].astype(F32) + r_ref[...].astype(F32)).astype(BF16)

    spec = pl.BlockSpec((None, tr, C), lambda j, i, core: (j, i, 0))
    return pl.pallas_call(
        body, name=name,
        grid_spec=pltpu.PrefetchScalarGridSpec(
            num_scalar_prefetch=1, grid=(4, r // tr),
            in_specs=[pl.BlockSpec((None, None, tr, C), lambda j, i, core: (j, core[0], i, 0)), spec], out_specs=spec),
        out_shape=jax.ShapeDtypeStruct((4, r, C), BF16), compiler_params=_params("parallel", "parallel"),
    )(core, grad, recv)


def shard_sum(part, recv, own, core, name):
    _, r, C = part.shape
    tr = _row_tile(r, C)

    def body(own_ref, core_ref, p_ref, r0, r1, r2, o_ref):
        o_ref[...] = p_ref[...].astype(F32) + r0[...].astype(F32) + r1[...].astype(F32) + r2[...].astype(F32)

    return pl.pallas_call(
        body, name=name,
        grid_spec=pltpu.PrefetchScalarGridSpec(
            num_scalar_prefetch=2, grid=(r // tr,),
            in_specs=[pl.BlockSpec((None, tr, C), lambda i, own, core: (own[0], i, 0))]
            + [pl.BlockSpec((None, tr, C), lambda i, own, core, k=k: (k, i, 0)) for k in range(3)],
            out_specs=pl.BlockSpec((None, tr, C), lambda i, own, core: (core[0], i, 0))),
        out_shape=jax.ShapeDtypeStruct((2, r, C), F32), compiler_params=_params("parallel"),
    )(own, core, part, recv, recv, recv)


def adamw(w, g, m, v, name):
    R, C = w.shape
    tr = _row_tile(R, C)

    def body(w_ref, g_ref, m_ref, v_ref, d_ref, nm_ref, nv_ref):
        gv = g_ref[...]
        nm = ADAM_B1 * m_ref[...] + (1.0 - ADAM_B1) * gv
        nv = ADAM_B2 * v_ref[...] + (1.0 - ADAM_B2) * (gv * gv)
        m_hat = nm / (1.0 - ADAM_B1 ** ADAM_STEP)
        v_hat = nv / (1.0 - ADAM_B2 ** ADAM_STEP)
        d_ref[...] = -ADAM_LR * (m_hat / (jnp.sqrt(v_hat) + ADAM_EPS) + ADAM_WD * w_ref[...])
        nm_ref[...] = nm
        nv_ref[...] = nv

    spec = pl.BlockSpec((tr, C), lambda i: (i, 0))
    return pl.pallas_call(
        body, grid=(R // tr,), name=name, in_specs=[spec] * 4, out_specs=[spec] * 3,
        out_shape=[jax.ShapeDtypeStruct((R, C), F32)] * 3, compiler_params=_params("parallel"),
    )(w, g, m, v)


HBM = pl.BlockSpec(memory_space=pl.ANY)


def _place():
    x, y, c = lax.axis_index("x"), lax.axis_index("y"), lax.axis_index("c")
    chips = [(1 - x, y), (x, 1 - y), (1 - x, 1 - y)]
    return x, y, c, chips


def gather_weights(bufs):
    n = len(bufs)

    def body(*refs):
        buf = refs[n:2 * n]
        send_sems, recv_sems = refs[2 * n:]
        x, y, c, chips = _place()
        own = 2 * x + y
        sibling = (x, y, 1 - c)

        def remote(w, k, slot, to):
            return pltpu.make_async_remote_copy(src_ref=slot, dst_ref=slot, send_sem=send_sems.at[w, k],
                                                recv_sem=recv_sems.at[w, k], device_id=to, device_id_type=MESH)

        sends = []
        for w in range(n):
            for k, chip in enumerate(chips):
                sends.append(remote(w, k, buf[w].at[own, c], (*chip, c)))
                sends[-1].start()
        for w in range(n):
            for k, (px, py) in enumerate(chips):
                theirs = buf[w].at[2 * px + py, c]
                remote(w, k, theirs, (px, py, c)).wait_recv()
                sends.append(remote(w, 3 + k, theirs, sibling))
                sends[-1].start()
        for w in range(n):
            for k, (px, py) in enumerate(chips):
                remote(w, 3 + k, buf[w].at[2 * px + py, 1 - c], sibling).wait_recv()
        for cp in sends:
            cp.wait_send()

    return pl.pallas_call(
        body, name="gather_weights", in_specs=[HBM] * n, out_specs=[HBM] * n,
        out_shape=[jax.ShapeDtypeStruct(b.shape, b.dtype) for b in bufs],
        input_output_aliases={i: i for i in range(n)},
        scratch_shapes=[pltpu.SemaphoreType.DMA((n, 6)), pltpu.SemaphoreType.DMA((n, 6))],
    )(*bufs)


def sibling_split(grads):
    n = len(grads)

    def body(*refs):
        src, dst = refs[:n], refs[n:2 * n]
        send_sems, recv_sems = refs[2 * n:]
        x, y, c, _ = _place()
        sends = [pltpu.make_async_remote_copy(src_ref=src[w].at[:, 1 - c], dst_ref=dst[w], send_sem=send_sems.at[w],
                                              recv_sem=recv_sems.at[w], device_id=(x, y, 1 - c), device_id_type=MESH)
                 for w in range(n)]
        for cp in sends:
            cp.start()
        for cp in sends:
            cp.wait()

    return pl.pallas_call(
        body, name="sibling_split", in_specs=[HBM] * n, out_specs=[HBM] * n,
        out_shape=[jax.ShapeDtypeStruct((4,) + g.shape[2:], g.dtype) for g in grads],
        scratch_shapes=[pltpu.SemaphoreType.DMA((n,))] * 2,
    )(*grads)


def chip_exchange(parts):
    n = len(parts)

    def body(*refs):
        src, dst = refs[:n], refs[n:2 * n]
        send_sems, recv_sems = refs[2 * n:]
        x, y, c, chips = _place()
        sends = []
        for w in range(n):
            for k, (px, py) in enumerate(chips):
                sends.append(pltpu.make_async_remote_copy(
                    src_ref=src[w].at[2 * px + py], dst_ref=dst[w].at[k], send_sem=send_sems.at[w, k],
                    recv_sem=recv_sems.at[w, k], device_id=(px, py, c), device_id_type=MESH))
                sends[-1].start()
        for cp in sends:
            cp.wait()

    return pl.pallas_call(
        body, name="chip_exchange", in_specs=[HBM] * n, out_specs=[HBM] * n,
        out_shape=[jax.ShapeDtypeStruct((3,) + p.shape[1:], p.dtype) for p in parts],
        scratch_shapes=[pltpu.SemaphoreType.DMA((n, 3)), pltpu.SemaphoreType.DMA((n, 3))],
    )(*parts)


def sibling_join(sums):
    n = len(sums)

    def body(*refs):
        buf = refs[n:2 * n]
        send_sems, recv_sems = refs[2 * n:]
        x, y, c, _ = _place()
        sends = [pltpu.make_async_remote_copy(src_ref=buf[w].at[c], dst_ref=buf[w].at[c], send_sem=send_sems.at[w],
                                              recv_sem=recv_sems.at[w], device_id=(x, y, 1 - c), device_id_type=MESH)
                 for w in range(n)]
        for cp in sends:
            cp.start()
        for w, cp in enumerate(sends):
            cp.wait_send()
            pltpu.make_async_remote_copy(src_ref=buf[w].at[c], dst_ref=buf[w].at[1 - c], send_sem=send_sems.at[w],
                                         recv_sem=recv_sems.at[w], device_id=(x, y, 1 - c), device_id_type=MESH).wait_recv()

    return pl.pallas_call(
        body, name="sibling_join", in_specs=[HBM] * n, out_specs=[HBM] * n,
        out_shape=[jax.ShapeDtypeStruct(s.shape, s.dtype) for s in sums],
        input_output_aliases={i: i for i in range(n)},
        scratch_shapes=[pltpu.SemaphoreType.DMA((n,))] * 2,
    )(*sums)


def allreduce_small(v):
    R, C = v.shape

    def body(v_ref, out_ref, all_ref, send_sems, recv_sems):
        x, y, c, chips = _place()
        me, sibling = (x, y, c), (x, y, 1 - c)

        def slot(px, py, pc):
            return all_ref.at[4 * px + 2 * py + pc]

        def copy(k, block, to, src=None):
            return pltpu.make_async_remote_copy(src_ref=slot(*block) if src is None else src, dst_ref=slot(*block),
                                                send_sem=send_sems.at[k], recv_sem=recv_sems.at[k], device_id=to,
                                                device_id_type=MESH)

        first = [copy(0, me, sibling, src=v_ref)] + [copy(1 + j, me, (*chip, c), src=v_ref) for j, chip in enumerate(chips)]
        for cp in first:
            cp.start()
        slot(*me)[...] = v_ref[...]
        passed = [copy(4 + j, (*chip, c), sibling) for j, chip in enumerate(chips)]
        for j, chip in enumerate(chips):
            copy(1 + j, (*chip, c), me).wait_recv()
            passed[j].start()
        copy(0, sibling, me).wait_recv()
        for j, chip in enumerate(chips):
            copy(4 + j, (*chip, 1 - c), me).wait_recv()
        for cp in first + passed:
            cp.wait_send()
        acc = all_ref[0]
        for k in range(1, 8):
            acc = acc + all_ref[k]
        out_ref[...] = acc

    vmem = pl.BlockSpec(memory_space=pltpu.VMEM)
    return pl.pallas_call(
        body, name="allreduce_small", in_specs=[vmem], out_specs=vmem, out_shape=jax.ShapeDtypeStruct((R, C), F32),
        scratch_shapes=[pltpu.VMEM((8, R, C), F32), pltpu.SemaphoreType.DMA((7,)), pltpu.SemaphoreType.DMA((7,))],
        compiler_params=pltpu.CompilerParams(vmem_limit_bytes=VMEM_LIMIT_V7X),
    )(v)


SMALL_PAD = 1024


def _pack_small(arrs):
    parts = []
    for a in arrs:
        f = a.reshape(-1).astype(F32)
        parts.append(jnp.pad(f, (0, (-f.shape[0]) % SMALL_PAD)))
    return jnp.concatenate(parts).reshape(-1, 128)


def _unpack_small(packed, shapes):
    flat = packed.reshape(-1)
    outs, off = [], 0
    for s in shapes:
        size = int(np.prod(s))
        outs.append(flat[off:off + size].reshape(s))
        off += size + (-size) % SMALL_PAD
    return outs


ATT_KEYS = ATT_WIN * ATT_TILE
ATT_NEAR_LO = (ATT_WIN - 1) * ATT_TILE - REL_CLIP + 1
ATT_PERIOD = ATT_KEYS + ATT_TILE + 1


def _att_bias_table(rel_bias):
    far = rel_bias[:, 2 * REL_CLIP:]
    near = rel_bias[:, 2 * REL_CLIP - 1:0:-1]
    w = jnp.concatenate([jnp.broadcast_to(far, (ATT_HEADS, ATT_NEAR_LO)), near,
                         jnp.broadcast_to(far, (ATT_HEADS, ATT_PERIOD - ATT_KEYS))], axis=1)
    rows = jnp.tile(w, (1, ATT_TILE))[:, :ATT_TILE * (ATT_PERIOD - 1)].reshape(ATT_HEADS, ATT_TILE, ATT_PERIOD - 1)
    i = np.arange(ATT_TILE)[:, None]
    m = np.arange(ATT_KEYS)[None, :]
    qc, kc = i // CHUNK, m // CHUNK
    band = (kc >= qc) & (kc <= qc + N_PREV_CHUNKS)
    return jnp.where(band[None], rows[:, :, :ATT_KEYS], NEG_INF)


def _rel_bias_grad(dbias):
    p = jnp.pad(dbias, ((0, 0), (0, 0), (0, ATT_PERIOD - 1 - ATT_KEYS))).reshape(ATT_HEADS, -1)
    p = jnp.pad(p, ((0, 0), (0, ATT_TILE))).reshape(ATT_HEADS, ATT_TILE, ATT_PERIOD)
    gw, far = bias_colsum(p)
    return jnp.concatenate([jnp.zeros((ATT_HEADS, 1), F32), gw[:, ATT_KEYS - 1:ATT_NEAR_LO - 1:-1], far[:, :1]], axis=1)


def kernel(x, mem, norm_mix_g, w_in, rel_bias, sg_ln_g, sg_ln_b, sg_w, sg_b, w_branch_att, w_branch_sg, w_out, norm_xattn_g, norm_mem_g, w_xq, w_xkv, w_xo, norm_ffn_g, w_ffn_in, w_ffn_out, norm_final_g, loss_target, m_norm_mix_g, m_w_in, m_rel_bias, m_sg_ln_g, m_sg_ln_b, m_sg_w, m_sg_b, m_w_branch_att, m_w_branch_sg, m_w_out, m_norm_xattn_g, m_norm_mem_g, m_w_xq, m_w_xkv, m_w_xo, m_norm_ffn_g, m_w_ffn_in, m_w_ffn_out, m_norm_final_g, v_norm_mix_g, v_w_in, v_rel_bias, v_sg_ln_g, v_sg_ln_b, v_sg_w, v_sg_b, v_w_branch_att, v_w_branch_sg, v_w_out, v_norm_xattn_g, v_norm_mem_g, v_w_xq, v_w_xkv, v_w_xo, v_norm_ffn_g, v_w_ffn_in, v_w_ffn_out, v_norm_final_g):
    S, D = x.shape[1], x.shape[2]
    x2d, mem2d, tgt = x[0], mem[0], loss_target[0]

    big_names = ["w_in", "w_branch_att", "w_branch_sg", "w_out", "w_xq", "w_xkv", "w_xo", "w_ffn_in", "w_ffn_out"]
    col_sharded = [True, True, True, False, False, True, False, True, False]
    big_w = [a[0] for a in (w_in, w_branch_att, w_branch_sg, w_out, w_xq, w_xkv, w_xo, w_ffn_in, w_ffn_out)]
    big_m = [a[0] for a in (m_w_in, m_w_branch_att, m_w_branch_sg, m_w_out, m_w_xq, m_w_xkv, m_w_xo, m_w_ffn_in, m_w_ffn_out)]
    big_v = [a[0] for a in (v_w_in, v_w_branch_att, v_w_branch_sg, v_w_out, v_w_xq, v_w_xkv, v_w_xo, v_w_ffn_in, v_w_ffn_out)]

    own = (2 * lax.axis_index("x") + lax.axis_index("y")).astype(jnp.int32).reshape(1)
    core = lax.axis_index("c").astype(jnp.int32).reshape(1)
    gathered = gather_weights([place_shard(w, own, "place_" + nm) for w, nm in zip(big_w, big_names)])
    full = []
    for g4, w, cs in zip(gathered, big_w, col_sharded):
        R, C = w.shape
        full.append(g4.reshape(4, R, C) if cs else g4.reshape(4 * R, C))
    W_in, W_ba, W_bs, W_out, W_xq, W_xkv, W_xo, W_fi, W_fo = full

    g_mix, g_xat, g_mem, g_ffn = norm_mix_g, norm_xattn_g, norm_mem_g, norm_ffn_g
    g_fin = norm_final_g.reshape(1, D)
    bias = _att_bias_table(rel_bias[0])
    tt = np.arange(SG_BLOCK)
    sg_mask = (tt[None, :] // CHUNK) <= (tt[:, None] // CHUNK)
    wm_f = jnp.where(sg_mask[None], sg_w[0], 0.0)
    wide = SG_GROUPS * SG_BLOCK
    wcat = jnp.transpose(wm_f, (1, 0, 2)).reshape(SG_BLOCK, wide).astype(BF16)
    wtcat = jnp.transpose(wm_f, (2, 0, 1)).reshape(SG_BLOCK, wide).astype(BF16)
    maskcat = jnp.asarray(np.tile(sg_mask, (1, SG_GROUPS)), F32)
    bfull = jnp.repeat(sg_b[0].T, SG_GROUP_DIM, axis=1)
    lng, lnb = sg_ln_g[0].reshape(1, 512), sg_ln_b[0].reshape(1, 512)
    gid = np.arange(512) // SG_GROUP_DIM
    amean = jnp.asarray((gid[:, None] == gid[None, :]) / SG_GROUP_DIM, BF16)

    h1, z = norm_mm(x2d, g_mix, W_in, tm=256, scale=None, name="norm_mm_in")
    y_att, lse_att = attn_fwd(z, bias)
    y_sg = sgu_fwd(z, amean, lng, lnb, wcat, bfull, tm=512)
    a_br, b_br, merged, x1 = merge_fwd(y_att, y_sg, z, x2d, W_ba, W_bs, W_out, tm=256)
    h2, xq = norm_mm(x1, g_xat, W_xq.reshape(1, D, D), tm=512, scale=XATT_HEAD_DIM ** -0.5, name="norm_mm_xq")
    mn, kv = norm_mm(mem2d, g_mem, W_xkv, tm=mem2d.shape[0], scale=None, name="norm_mm_kv")
    o_x, lse_x, x2 = xattn_fwd(xq, kv, W_xo, x1, tm=512)
    h3, gu = norm_mm(x2, g_ffn, W_fi, tm=256, scale=None, name="norm_mm_ffn")
    act, x3 = ffn_out_fwd(gu, W_fo, x2, tm=256)
    loss_vec, dx3, dg_fin = loss_head(x3, g_fin, tgt, tm=512)

    dgu = ffn_act_bwd(dx3, gu, W_fo, tm=256, nchunk=2)
    gW_fo = tn_mm(act, dx3, shards=1, tk=512, name="dw_ffn_out")
    gW_fi = tn_mm(h3, dgu, shards=4, tk=512, name="dw_ffn_in")
    dx2, dg_ffn = mm_nt_norm_bwd(dgu, W_fi, x2, g_ffn, dx3, tm=256, name="dx_ffn")
    dq_x, dkv = xattn_bwd(dx2, xq, o_x, lse_x, kv, W_xo, tm=512)
    gW_xo = tn_mm(o_x, dx2, shards=1, tk=512, name="dw_xo")
    gW_xq = tn_mm(h2, dq_x, shards=1, tk=512, name="dw_xq")
    dx1, dg_xat = mm_nt_norm_bwd(dq_x, W_xq.reshape(1, D, D), x1, g_xat, dx2, tm=512, name="dx_xq")
    gW_xkv = tn_mm(mn, dkv, shards=4, tk=mem2d.shape[0], name="dw_xkv")
    _, dg_mem = mm_nt_norm_bwd(dkv.astype(BF16), W_xkv, mem2d, g_mem, None, tm=mem2d.shape[0], name="dx_mem")
    da, db, dgab, dy_att, dy_sg = merge_bwd(dx1, W_out, a_br, b_br, z, W_ba, W_bs, tm=256)
    gW_out = tn_mm(merged, dx1, shards=1, tk=512, name="dw_out")
    gW_ba = tn_mm(y_att, da, shards=4, tk=512, name="dw_branch_att")
    gW_bs = tn_mm(y_sg, db, shards=4, tk=512, name="dw_branch_sg")
    dq_a, dk_a, dv_a, dbias = attn_bwd(z, y_att, dy_att, lse_att, bias)
    duv, dwcat, dsgb_full, dlng, dlnb = sgu_bwd(z, dy_sg, amean, lng, lnb, wcat, wtcat, bfull, maskcat, tm=512)
    dwm = jnp.transpose(dwcat.reshape(SG_BLOCK, SG_GROUPS, SG_BLOCK), (1, 0, 2))
    dsgb = dsgb_full[:, ::SG_GROUP_DIM].T
    dz = jnp.concatenate([dq_a, dk_a, dv_a, duv, dgab], axis=1)
    gW_in = tn_mm(h1, dz, shards=4, tk=512, name="dw_in")
    dx, dg_mix = mm_nt_norm_bwd(dz, W_in, x2d, g_mix, dx1, tm=256, name="dx_in")
    d_rel = _rel_bias_grad(dbias)

    big_g = [gW_in, gW_ba, gW_bs, gW_out, gW_xq, gW_xkv, gW_xo, gW_fi, gW_fo]
    canon = []
    for g, w in zip(big_g, big_w):
        R, C = w.shape
        canon.append(g.reshape(4, 2, R // 2, C))
    from_sibling = sibling_split(canon)
    parts = [chip_partial(g, r, core, "chip_partial_" + nm) for g, r, nm in zip(canon, from_sibling, big_names)]
    from_chips = chip_exchange(parts)
    sums = [shard_sum(p, r, own, core, "shard_sum_" + nm) for p, r, nm in zip(parts, from_chips, big_names)]
    joined = sibling_join(sums)
    big_out = []
    for j, w, m_, v_, nm in zip(joined, big_w, big_m, big_v, big_names):
        g = j.reshape(w.shape)
        big_out.append((g,) + tuple(adamw(w, g, m_, v_, "adamw_" + nm)))

    small_w = [norm_mix_g, rel_bias, sg_ln_g, sg_ln_b, sg_w, sg_b, norm_xattn_g, norm_mem_g, norm_ffn_g, norm_final_g]
    small_m = [m_norm_mix_g, m_rel_bias, m_sg_ln_g, m_sg_ln_b, m_sg_w, m_sg_b, m_norm_xattn_g, m_norm_mem_g, m_norm_ffn_g, m_norm_final_g]
    small_v = [v_norm_mix_g, v_rel_bias, v_sg_ln_g, v_sg_ln_b, v_sg_w, v_sg_b, v_norm_xattn_g, v_norm_mem_g, v_norm_ffn_g, v_norm_final_g]
    small_g = [dg_mix, d_rel, dlng, dlnb, dwm, dsgb, dg_xat, dg_mem, dg_ffn, dg_fin]
    shapes = [w.shape for w in small_w]
    g_sum = allreduce_small(_pack_small(small_g + [loss_vec[0, :1]]))
    zero = jnp.zeros((1,), F32)
    d_s, m_s, v_s = adamw(_pack_small(small_w + [zero]), g_sum, _pack_small(small_m + [zero]), _pack_small(small_v + [zero]),
                          "adamw_small")
    sg_, sd, sm, sv_ = (_unpack_small(p, shapes + [(1,)]) for p in (g_sum, d_s, m_s, v_s))
    loss = sg_[-1][0]

    order = ["norm_mix_g", "w_in", "rel_bias", "sg_ln_g", "sg_ln_b", "sg_w", "sg_b", "w_branch_att", "w_branch_sg", "w_out",
             "norm_xattn_g", "norm_mem_g", "w_xq", "w_xkv", "w_xo", "norm_ffn_g", "w_ffn_in", "w_ffn_out", "norm_final_g"]
    small_names = ["norm_mix_g", "rel_bias", "sg_ln_g", "sg_ln_b", "sg_w", "sg_b", "norm_xattn_g", "norm_mem_g", "norm_ffn_g",
                   "norm_final_g"]
    res = {}
    for i, nm in enumerate(small_names):
        res[nm] = (sg_[i], sd[i], sm[i], sv_[i])
    for nm, outs in zip(big_names, big_out):
        res[nm] = tuple(o[None] for o in outs)
    return (loss, dx[None], *[res[nm][0] for nm in order], *[res[nm][1] for nm in order],
            *[res[nm][2] for nm in order], *[res[nm][3] for nm in order])
```

```python
import functools
from typing import NamedTuple

import numpy as np
import jax
import jax.numpy as jnp
from jax import lax
from jax.experimental import pallas as pl
from jax.experimental.pallas import tpu as pltpu

F32, BF16 = jnp.float32, jnp.bfloat16
MESH = pl.DeviceIdType.MESH

EPS = 1e-6
NEG_INF = -1e30
CHUNK = 64
N_PREV_CHUNKS = 8
ATT_HEADS, ATT_HEAD_DIM = 8, 64
REL_CLIP = 128
SG_BLOCK, SG_GROUPS, SG_GROUP_DIM = 128, 8, 64
XATT_HEADS, XATT_HEAD_DIM = 4, 256
ATT_TILE = 128
ATT_WIN = 5
ADAM_LR, ADAM_B1, ADAM_B2, ADAM_EPS, ADAM_WD, ADAM_STEP = 0.001, 0.9, 0.999, 1e-08, 0.01, 10

VMEM_LIMIT_V7X = 56 * 1024 * 1024
EW_BLOCK_BYTES = 1 << 20
HBM = pl.BlockSpec(memory_space=pl.ANY)


def _params(*sem):
    return pltpu.CompilerParams(dimension_semantics=sem, vmem_limit_bytes=VMEM_LIMIT_V7X)


def _full(shape):
    n = len(shape)
    return pl.BlockSpec(shape, lambda *_: (0,) * n)


class Hook(NamedTuple):
    ins: tuple = ()
    out_shapes: tuple = ()
    aliased: bool = False
    sems: tuple = ()
    steps: tuple = ()


def _run_hook(hook, pos, h_in, h_out, h_sems):
    for step, where, fn in hook.steps:
        if where == pos:
            pl.when(pl.program_id(0) == step)(functools.partial(fn, h_in, h_out, h_sems))


def _hooked_call(body, hook, *, n_in, n_out, in_specs, out_specs, out_shape, scratch_shapes=(), **kw):
    nh = len(hook.ins)
    aliases = {n_in + i: n_out + i for i in range(nh)} if hook.aliased else {}
    return pl.pallas_call(
        body, in_specs=list(in_specs) + [HBM] * nh, out_specs=list(out_specs) + [HBM] * len(hook.out_shapes),
        out_shape=list(out_shape) + list(hook.out_shapes), scratch_shapes=list(scratch_shapes) + list(hook.sems),
        input_output_aliases=aliases, **kw)


def _nt(a, b):
    return lax.dot_general(a, b, (((1,), (1,)), ((), ())), preferred_element_type=F32)


def _tn(a, b):
    return lax.dot_general(a, b, (((0,), (0,)), ((), ())), preferred_element_type=F32)


def _nn(a, b):
    return jnp.dot(a, b, preferred_element_type=F32)


def _sigmoid(x):
    return 1.0 / (1.0 + jnp.exp(-x))


_GELU_C = float(np.sqrt(2.0 / np.pi))


def _gelu(x):
    t = jnp.tanh(_GELU_C * (x + 0.044715 * (x * x * x)))
    return x * (0.5 * (1.0 + t))


def _gelu_grad(x):
    t = jnp.tanh(_GELU_C * (x + 0.044715 * (x * x * x)))
    return 0.5 * (1.0 + t) + 0.5 * x * (1.0 - t * t) * (_GELU_C * (1.0 + 3.0 * 0.044715 * x * x))


def _rms_stats(xf):
    rstd = lax.rsqrt(jnp.mean(xf * xf, axis=-1, keepdims=True) + EPS)
    return rstd, xf * rstd


def _rms_bwd(xhat, rstd, g, dh):
    dxh = dh * g
    dx = rstd * (dxh - xhat * jnp.mean(dxh * xhat, axis=-1, keepdims=True))
    return dx, dh * xhat


def norm_mm(x, g, w, *, tm, scale, name):
    S, D = x.shape
    J, _, C = w.shape

    def body(x_ref, g_ref, w_ref, h_ref, z_ref):
        _, xhat = _rms_stats(x_ref[...])
        h = (xhat * g_ref[...]).astype(BF16)
        h_ref[...] = h
        for j in range(J):
            acc = _nn(h, w_ref[j])
            if scale is not None:
                acc = acc * scale
            z_ref[:, j * C:(j + 1) * C] = acc.astype(BF16)

    return pl.pallas_call(
        body, grid=(S // tm,), name=name,
        in_specs=[pl.BlockSpec((tm, D), lambda i: (i, 0)), _full((1, D)), _full((J, D, C))],
        out_specs=[pl.BlockSpec((tm, D), lambda i: (i, 0)), pl.BlockSpec((tm, J * C), lambda i: (i, 0))],
        out_shape=[jax.ShapeDtypeStruct((S, D), BF16), jax.ShapeDtypeStruct((S, J * C), BF16)],
        compiler_params=_params("parallel"),
    )(x, g, w)


def _att_window_specs(nt, col):
    return [pl.BlockSpec((ATT_TILE, 512), lambda t, j=j: (jnp.clip(t - (ATT_WIN - 1) + j, 0, nt - 1), col))
            for j in range(ATT_WIN)]


def _att_scores(q, k, bias, valid):
    s = _nt(q, k) * (ATT_HEAD_DIM ** -0.5) + bias
    return jnp.where(valid, s, NEG_INF)


def _att_valid(t):
    kpos = lax.broadcasted_iota(jnp.int32, (ATT_TILE, ATT_WIN * ATT_TILE), 1) + (t - (ATT_WIN - 1)) * ATT_TILE
    return kpos >= 0


def attn_fwd(z, bias, hook=Hook()):
    S = z.shape[0]
    nt = S // ATT_TILE
    n_in, nh, nho = 2 + 2 * ATT_WIN, len(hook.ins), len(hook.out_shapes)

    def body(q_ref, *refs):
        k_refs, v_refs = refs[:ATT_WIN], refs[ATT_WIN:2 * ATT_WIN]
        bias_ref = refs[2 * ATT_WIN]
        rest = refs[2 * ATT_WIN + 1:]
        h_in, (y_ref, lse_ref), h_out, h_sems = rest[:nh], rest[nh:nh + 2], rest[nh + 2:nh + 2 + nho], rest[nh + 2 + nho:]
        _run_hook(hook, "before", h_in, h_out, h_sems)
        valid = _att_valid(pl.program_id(0))
        for h in range(ATT_HEADS):
            sl = slice(h * ATT_HEAD_DIM, (h + 1) * ATT_HEAD_DIM)
            k = jnp.concatenate([r[:, sl] for r in k_refs], axis=0)
            v = jnp.concatenate([r[:, sl] for r in v_refs], axis=0)
            s = _att_scores(q_ref[:, sl], k, bias_ref[h], valid)
            m = jnp.max(s, axis=-1, keepdims=True)
            p = jnp.exp(s - m)
            l = jnp.sum(p, axis=-1, keepdims=True)
            y_ref[:, sl] = (_nn(p.astype(BF16), v) / l).astype(BF16)
            lse_ref[:, h:h + 1] = m + jnp.log(l)
        _run_hook(hook, "after", h_in, h_out, h_sems)

    tile = lambda col: pl.BlockSpec((ATT_TILE, 512), lambda t: (t, col))
    return _hooked_call(
        body, hook, n_in=n_in, n_out=2, grid=(nt,), name="attn_fwd",
        in_specs=[tile(0)] + _att_window_specs(nt, 1) + _att_window_specs(nt, 2) + [_full(bias.shape)],
        out_specs=[tile(0), pl.BlockSpec((ATT_TILE, ATT_HEADS), lambda t: (t, 0))],
        out_shape=[jax.ShapeDtypeStruct((S, 512), BF16), jax.ShapeDtypeStruct((S, ATT_HEADS), F32)],
        compiler_params=_params("arbitrary"),
    )(z, *([z] * (2 * ATT_WIN)), bias, *hook.ins)


def _group_mean(x, a_ref):
    hi = x.astype(BF16)
    lo = (x - hi.astype(F32)).astype(BF16)
    return _nn(hi, a_ref[...]) + _nn(lo, a_ref[...])


def _block_diag(x, ref):
    group = lax.broadcasted_iota(jnp.int32, x.shape, 1) // SG_GROUP_DIM
    for g in range(SG_GROUPS):
        ref[g * SG_BLOCK:(g + 1) * SG_BLOCK, :] = jnp.where(group == g, x, 0.0).astype(BF16)


def _sgu_block(zu_ref, zv_ref, rows, a_ref, lng_ref, lnb_ref, wcat_ref, bfull_ref, vbd_ref):
    zu = zu_ref[rows, :].astype(F32)
    zv = zv_ref[rows, :].astype(F32)
    u, v = _gelu(zu), _gelu(zv)
    vc = v - _group_mean(v, a_ref)
    rstd = lax.rsqrt(_group_mean(vc * vc, a_ref) + EPS)
    xhat = vc * rstd
    _block_diag(xhat * lng_ref[...] + lnb_ref[...], vbd_ref)
    sv = _nn(wcat_ref[...], vbd_ref[...]) + bfull_ref[...]
    return zu, zv, u, xhat, rstd, sv


def _sgu_rows(b):
    return pl.ds(pl.multiple_of(b * SG_BLOCK, SG_BLOCK), SG_BLOCK)


def sgu_fwd(z, amean, lng, lnb, wcat, bfull, *, tm):
    S = z.shape[0]

    def body(zu_ref, zv_ref, a_ref, lng_ref, lnb_ref, wcat_ref, bfull_ref, y_ref, vbd_ref):
        def block(b, carry):
            rows = _sgu_rows(b)
            _, _, u, _, _, sv = _sgu_block(zu_ref, zv_ref, rows, a_ref, lng_ref, lnb_ref, wcat_ref, bfull_ref, vbd_ref)
            y_ref[rows, :] = (u * sv).astype(BF16)
            return carry

        lax.fori_loop(0, tm // SG_BLOCK, block, 0)

    tile = lambda col: pl.BlockSpec((tm, 512), lambda i: (i, col))
    smalls = [amean, lng, lnb, wcat, bfull]
    return pl.pallas_call(
        body, grid=(S // tm,), name="sgu_fwd",
        in_specs=[tile(3), tile(4)] + [_full(a.shape) for a in smalls],
        out_specs=tile(0), out_shape=jax.ShapeDtypeStruct((S, 512), BF16),
        scratch_shapes=[pltpu.VMEM((SG_GROUPS * SG_BLOCK, 512), BF16)],
        compiler_params=_params("parallel"),
    )(z, z, *smalls)


def _gate_specs(tm):
    return [pl.BlockSpec((tm, 512), lambda i, c=c: (i, c)) for c in (5, 6, 7, 8)]


def merge_fwd(y_att, y_sg, z, x, w_ba, w_bs, w_out, *, tm):
    S, D = x.shape
    J, _, C = w_ba.shape

    def body(ya_ref, ys_ref, g0, g1, g2, g3, x_ref, wba_ref, wbs_ref, wo_ref, a_ref, b_ref, m_ref, x1_ref):
        ya, ys = ya_ref[...], ys_ref[...]
        a = jnp.concatenate([_nn(ya, wba_ref[j]) for j in range(J)], axis=1)
        b = jnp.concatenate([_nn(ys, wbs_ref[j]) for j in range(J)], axis=1)
        ga = jnp.concatenate([g0[...], g1[...]], axis=1).astype(F32)
        gb = jnp.concatenate([g2[...], g3[...]], axis=1).astype(F32)
        a_ref[...] = a.astype(BF16)
        b_ref[...] = b.astype(BF16)
        merged = (_sigmoid(ga) * a + _sigmoid(gb) * b).astype(BF16)
        m_ref[...] = merged
        x1_ref[...] = x_ref[...] + _nn(merged, wo_ref[...])

    row = lambda n: pl.BlockSpec((tm, n), lambda i: (i, 0))
    return pl.pallas_call(
        body, grid=(S // tm,), name="merge_fwd",
        in_specs=[row(512), row(512)] + _gate_specs(tm) + [row(D), _full(w_ba.shape), _full(w_bs.shape), _full(w_out.shape)],
        out_specs=[row(D)] * 4,
        out_shape=[jax.ShapeDtypeStruct((S, D), BF16)] * 3 + [jax.ShapeDtypeStruct((S, D), F32)],
        compiler_params=_params("parallel"),
    )(y_att, y_sg, z, z, z, z, x, w_ba, w_bs, w_out)


def xattn_fwd(xq, kv, w_xo, x1, *, tm):
    S, D = xq.shape
    dh = XATT_HEAD_DIM

    def body(q_ref, kv_ref, wo_ref, x1_ref, o_ref, lse_ref, x2_ref):
        outs = []
        for h in range(XATT_HEADS):
            s = _nt(q_ref[:, h * dh:(h + 1) * dh], kv_ref[:, h * dh:(h + 1) * dh])
            m = jnp.max(s, axis=-1, keepdims=True)
            p = jnp.exp(s - m)
            l = jnp.sum(p, axis=-1, keepdims=True)
            outs.append((_nn(p.astype(BF16), kv_ref[:, D + h * dh:D + (h + 1) * dh]) / l).astype(BF16))
            lse_ref[:, h:h + 1] = m + jnp.log(l)
        o = jnp.concatenate(outs, axis=1)
        o_ref[...] = o
        x2_ref[...] = x1_ref[...] + _nn(o, wo_ref[...])

    row = lambda n: pl.BlockSpec((tm, n), lambda i: (i, 0))
    return pl.pallas_call(
        body, grid=(S // tm,), name="xattn_fwd",
        in_specs=[row(D), _full(kv.shape), _full(w_xo.shape), row(D)],
        out_specs=[row(D), row(XATT_HEADS), row(D)],
        out_shape=[jax.ShapeDtypeStruct((S, D), BF16), jax.ShapeDtypeStruct((S, XATT_HEADS), F32),
                   jax.ShapeDtypeStruct((S, D), F32)],
        compiler_params=_params("parallel"),
    )(xq, kv, w_xo, x1)


def ffn_out_fwd(gu, w, x2, *, tm):
    S, D = x2.shape
    F = w.shape[0]

    def body(gu_ref, w_ref, x2_ref, act_ref, x3_ref):
        gate = gu_ref[:, :F].astype(F32)
        up = gu_ref[:, F:].astype(F32)
        act = (gate * _sigmoid(gate) * up).astype(BF16)
        act_ref[...] = act
        x3_ref[...] = x2_ref[...] + _nn(act, w_ref[...])

    row = lambda n: pl.BlockSpec((tm, n), lambda i: (i, 0))
    return pl.pallas_call(
        body, grid=(S // tm,), name="ffn_out_fwd",
        in_specs=[row(2 * F), _full(w.shape), row(D)], out_specs=[row(F), row(D)],
        out_shape=[jax.ShapeDtypeStruct((S, F), BF16), jax.ShapeDtypeStruct((S, D), F32)],
        compiler_params=_params("parallel"),
    )(gu, w, x2)


def loss_head(x3, g, target, *, tm):
    S, D = x3.shape

    def body(x_ref, g_ref, t_ref, loss_ref, dx_ref, dg_ref):
        @pl.when(pl.program_id(0) == 0)
        def _():
            loss_ref[...] = jnp.zeros_like(loss_ref)
            dg_ref[...] = jnp.zeros_like(dg_ref)

        gv = g_ref[...]
        rstd, xhat = _rms_stats(x_ref[...])
        err = xhat * gv - t_ref[...]
        loss_ref[...] += 0.5 * jnp.sum(jnp.mean(err * err, axis=-1, keepdims=True))
        dx, dgc = _rms_bwd(xhat, rstd, gv, err * (1.0 / D))
        dx_ref[...] = dx
        dg_ref[...] += jnp.sum(dgc, axis=0, keepdims=True)

    row = pl.BlockSpec((tm, D), lambda i: (i, 0))
    return pl.pallas_call(
        body, grid=(S // tm,), name="loss_head",
        in_specs=[row, _full((1, D)), row], out_specs=[_full((1, 128)), row, _full((1, D))],
        out_shape=[jax.ShapeDtypeStruct((1, 128), F32), jax.ShapeDtypeStruct((S, D), F32),
                   jax.ShapeDtypeStruct((1, D), F32)],
        compiler_params=_params("arbitrary"),
    )(x3, g, target)


def ffn_act_bwd(dx3, gu, w, *, tm, nchunk):
    S, D = dx3.shape
    F = w.shape[0]
    cn = F // nchunk

    def body(dx_ref, gu_ref, w_ref, dgu_ref):
        dxb = dx_ref[...].astype(BF16)
        for j in range(nchunk):
            dact = _nt(dxb, w_ref[j * cn:(j + 1) * cn, :])
            gate = gu_ref[:, j * cn:(j + 1) * cn].astype(F32)
            up = gu_ref[:, F + j * cn:F + (j + 1) * cn].astype(F32)
            sg = _sigmoid(gate)
            dgu_ref[:, j * cn:(j + 1) * cn] = (dact * up * (sg * (1.0 + gate * (1.0 - sg)))).astype(BF16)
            dgu_ref[:, F + j * cn:F + (j + 1) * cn] = (dact * (gate * sg)).astype(BF16)

    row = lambda n: pl.BlockSpec((tm, n), lambda i: (i, 0))
    return pl.pallas_call(
        body, grid=(S // tm,), name="ffn_act_bwd",
        in_specs=[row(D), row(2 * F), _full(w.shape)], out_specs=row(2 * F),
        out_shape=jax.ShapeDtypeStruct((S, 2 * F), BF16), compiler_params=_params("parallel"),
    )(dx3, gu, w)


def tn_mm(a, b, *, shards, tn, tk, name):
    K, M = a.shape
    N = b.shape[1]
    C = N // shards
    per = tn // C
    nk = K // tk

    def body(a_ref, b_ref, o_ref, acc_ref):
        k = pl.program_id(1)

        @pl.when(k == 0)
        def _():
            acc_ref[...] = jnp.zeros_like(acc_ref)

        acc_ref[...] += _tn(a_ref[...].astype(BF16), b_ref[...].astype(BF16))

        @pl.when(k == nk - 1)
        def _():
            if shards > 1:
                for s in range(per):
                    o_ref[s] = acc_ref[:, s * C:(s + 1) * C].astype(BF16)
            else:
                o_ref[...] = acc_ref[...].astype(BF16)

    if shards > 1:
        out_spec = pl.BlockSpec((per, M, C), lambda n, k: (n, 0, 0))
        out_shape = jax.ShapeDtypeStruct((shards, M, C), BF16)
    else:
        out_spec = pl.BlockSpec((M, tn), lambda n, k: (0, n))
        out_shape = jax.ShapeDtypeStruct((M, N), BF16)
    return pl.pallas_call(
        body, grid=(N // tn, nk), name=name,
        in_specs=[pl.BlockSpec((tk, M), lambda n, k: (k, 0)), pl.BlockSpec((tk, tn), lambda n, k: (k, n))],
        out_specs=out_spec, out_shape=out_shape, scratch_shapes=[pltpu.VMEM((M, tn), F32)],
        compiler_params=_params("parallel", "arbitrary"),
    )(a, b)


def mm_nt_norm_bwd(dy, w, x, g, dx_in, *, tm, name, hook=Hook()):
    S, D = x.shape
    J, _, C = w.shape
    has_in = dx_in is not None
    n_in, nh, nho = 4 + has_in, len(hook.ins), len(hook.out_shapes)

    def body(*refs):
        dy_ref, w_ref, x_ref, g_ref = refs[:4]
        dxin_ref = refs[4] if has_in else None
        rest = refs[n_in:]
        h_in, (dx_ref, dg_ref), h_out, h_sems = rest[:nh], rest[nh:nh + 2], rest[nh + 2:nh + 2 + nho], rest[nh + 2 + nho:]
        _run_hook(hook, "before", h_in, h_out, h_sems)

        @pl.when(pl.program_id(0) == 0)
        def _():
            dg_ref[...] = jnp.zeros_like(dg_ref)

        dh = _nt(dy_ref[:, 0:C], w_ref[0])
        for j in range(1, J):
            dh += _nt(dy_ref[:, j * C:(j + 1) * C], w_ref[j])
        rstd, xhat = _rms_stats(x_ref[...])
        dx, dgc = _rms_bwd(xhat, rstd, g_ref[...], dh)
        dx_ref[...] = dx + dxin_ref[...] if has_in else dx
        dg_ref[...] += jnp.sum(dgc, axis=0, keepdims=True)
        _run_hook(hook, "after", h_in, h_out, h_sems)

    row = lambda n: pl.BlockSpec((tm, n), lambda i: (i, 0))
    ins = [dy, w, x, g] + ([dx_in] if has_in else [])
    return _hooked_call(
        body, hook, n_in=n_in, n_out=2, grid=(S // tm,), name=name,
        in_specs=[row(J * C), _full(w.shape), row(D), _full((1, D))] + ([row(D)] if has_in else []),
        out_specs=[row(D), _full((1, D))],
        out_shape=[jax.ShapeDtypeStruct((S, D), F32), jax.ShapeDtypeStruct((1, D), F32)],
        compiler_params=_params("arbitrary"),
    )(*ins, *hook.ins)


def xattn_bwd(dx2, xq, o, lse, kv, w_xo, *, tm):
    S, D = xq.shape
    M = kv.shape[0]
    dh = XATT_HEAD_DIM

    def body(dx_ref, q_ref, o_ref, lse_ref, kv_ref, wo_ref, dq_ref, dkv_ref):
        @pl.when(pl.program_id(0) == 0)
        def _():
            dkv_ref[...] = jnp.zeros_like(dkv_ref)

        do = _nt(dx_ref[...].astype(BF16), wo_ref[...])
        for h in range(XATT_HEADS):
            hs = slice(h * dh, (h + 1) * dh)
            vs = slice(D + h * dh, D + (h + 1) * dh)
            q, k, v = q_ref[:, hs], kv_ref[:, hs], kv_ref[:, vs]
            do_h = do[:, hs]
            do_b = do_h.astype(BF16)
            p = jnp.exp(_nt(q, k) - lse_ref[:, h:h + 1])
            delta = jnp.sum(do_h * o_ref[:, hs].astype(F32), axis=-1, keepdims=True)
            ds = (p * (_nt(do_b, v) - delta)).astype(BF16)
            dq_ref[:, hs] = (_nn(ds, k) * (dh ** -0.5)).astype(BF16)
            dkv_ref[:, hs] += _tn(ds, q)
            dkv_ref[:, vs] += _tn(p.astype(BF16), do_b)

    row = lambda n: pl.BlockSpec((tm, n), lambda i: (i, 0))
    return pl.pallas_call(
        body, grid=(S // tm,), name="xattn_bwd",
        in_specs=[row(D), row(D), row(D), row(XATT_HEADS), _full(kv.shape), _full(w_xo.shape)],
        out_specs=[row(D), _full((M, 2 * D))],
        out_shape=[jax.ShapeDtypeStruct((S, D), BF16), jax.ShapeDtypeStruct((M, 2 * D), F32)],
        compiler_params=_params("arbitrary"),
    )(dx2, xq, o, lse, kv, w_xo)


def merge_bwd(dx1, w_out, a, b, z, w_ba, w_bs, *, tm):
    S, D = dx1.shape
    J, W, C = w_ba.shape

    def body(dx_ref, wo_ref, a_ref, b_ref, g0, g1, g2, g3, wba_ref, wbs_ref, da_ref, db_ref, dg_ref, dya_ref, dys_ref):
        dm = _nt(dx_ref[...].astype(BF16), wo_ref[...])
        sa = _sigmoid(jnp.concatenate([g0[...], g1[...]], axis=1).astype(F32))
        sb = _sigmoid(jnp.concatenate([g2[...], g3[...]], axis=1).astype(F32))
        dg_ref[:, :D] = (dm * a_ref[...].astype(F32) * (sa * (1.0 - sa))).astype(BF16)
        dg_ref[:, D:] = (dm * b_ref[...].astype(F32) * (sb * (1.0 - sb))).astype(BF16)
        da = (dm * sa).astype(BF16)
        db = (dm * sb).astype(BF16)
        da_ref[...] = da
        db_ref[...] = db
        dya = _nt(da[:, 0:C], wba_ref[0])
        dys = _nt(db[:, 0:C], wbs_ref[0])
        for j in range(1, J):
            dya += _nt(da[:, j * C:(j + 1) * C], wba_ref[j])
            dys += _nt(db[:, j * C:(j + 1) * C], wbs_ref[j])
        dya_ref[...] = dya.astype(BF16)
        dys_ref[...] = dys.astype(BF16)

    row = lambda n: pl.BlockSpec((tm, n), lambda i: (i, 0))
    return pl.pallas_call(
        body, grid=(S // tm,), name="merge_bwd",
        in_specs=[row(D), _full(w_out.shape), row(D), row(D)] + _gate_specs(tm) + [_full(w_ba.shape), _full(w_bs.shape)],
        out_specs=[row(D), row(D), row(2 * D), row(W), row(W)],
        out_shape=[jax.ShapeDtypeStruct((S, D), BF16)] * 2 + [jax.ShapeDtypeStruct((S, 2 * D), BF16)]
        + [jax.ShapeDtypeStruct((S, W), BF16)] * 2,
        compiler_params=_params("parallel"),
    )(dx1, w_out, a, b, z, z, z, z, w_ba, w_bs)


def attn_bwd(z, y, dy, lse, bias, hook=Hook()):
    S = z.shape[0]
    nt = S // ATT_TILE
    back = ATT_WIN - 1
    n_in, nh, nho = 5 + 2 * ATT_WIN, len(hook.ins), len(hook.out_shapes)

    def body(q_ref, *refs):
        k_refs, v_refs = refs[:ATT_WIN], refs[ATT_WIN:2 * ATT_WIN]
        y_ref, dy_ref, lse_ref, bias_ref = refs[2 * ATT_WIN:2 * ATT_WIN + 4]
        rest = refs[2 * ATT_WIN + 4:]
        h_in, (dq_ref, dk_ref, dv_ref, dbias_ref), h_out = rest[:nh], rest[nh:nh + 4], rest[nh + 4:nh + 4 + nho]
        dk_acc, dv_acc = rest[nh + 4 + nho:nh + 6 + nho]
        h_sems = rest[nh + 6 + nho:]
        t = pl.program_id(0)
        _run_hook(hook, "before", h_in, h_out, h_sems)

        @pl.when(t == 0)
        def _():
            dbias_ref[...] = jnp.zeros_like(dbias_ref)
            dk_acc[...] = jnp.zeros_like(dk_acc)
            dv_acc[...] = jnp.zeros_like(dv_acc)

        @pl.when(t < nt)
        def _():
            valid = _att_valid(t)
            for h in range(ATT_HEADS):
                sl = slice(h * ATT_HEAD_DIM, (h + 1) * ATT_HEAD_DIM)
                k = jnp.concatenate([r[:, sl] for r in k_refs], axis=0)
                v = jnp.concatenate([r[:, sl] for r in v_refs], axis=0)
                q, do_b = q_ref[:, sl], dy_ref[:, sl]
                p = jnp.exp(_att_scores(q, k, bias_ref[h], valid) - lse_ref[:, h:h + 1])
                delta = jnp.sum(do_b.astype(F32) * y_ref[:, sl].astype(F32), axis=-1, keepdims=True)
                ds = p * (_nt(do_b, v) - delta)
                dbias_ref[h] += ds
                ds_b = ds.astype(BF16)
                dq_ref[:, sl] = (_nn(ds_b, k) * (ATT_HEAD_DIM ** -0.5)).astype(BF16)
                dk_w = _tn(ds_b, q) * (ATT_HEAD_DIM ** -0.5)
                dv_w = _tn(p.astype(BF16), do_b)
                for i in range(ATT_WIN):
                    slot = (t + 1 + i) % ATT_WIN
                    rows = slice(i * ATT_TILE, (i + 1) * ATT_TILE)
                    if i == back:
                        dk_acc[slot, :, sl] = dk_w[rows]
                        dv_acc[slot, :, sl] = dv_w[rows]
                    else:
                        dk_acc[slot, :, sl] += dk_w[rows]
                        dv_acc[slot, :, sl] += dv_w[rows]

        done = (t + 1) % ATT_WIN
        dk_ref[...] = dk_acc[done].astype(BF16)
        dv_ref[...] = dv_acc[done].astype(BF16)
        _run_hook(hook, "after", h_in, h_out, h_sems)

    last = nt - 1
    tile = lambda col, n=512: pl.BlockSpec((ATT_TILE, n), lambda t: (jnp.minimum(t, last), col))
    late = pl.BlockSpec((ATT_TILE, 512), lambda t: (jnp.maximum(t - back, 0), 0))
    return _hooked_call(
        body, hook, n_in=n_in, n_out=4, grid=(nt + back,), name="attn_bwd",
        in_specs=[tile(0)] + _att_window_specs(nt, 1) + _att_window_specs(nt, 2)
        + [tile(0), tile(0), tile(0, ATT_HEADS), _full(bias.shape)],
        out_specs=[tile(0), late, late, _full(bias.shape)],
        out_shape=[jax.ShapeDtypeStruct((S, 512), BF16)] * 3 + [jax.ShapeDtypeStruct(bias.shape, F32)],
        scratch_shapes=[pltpu.VMEM((ATT_WIN, ATT_TILE, 512), F32)] * 2,
        compiler_params=_params("arbitrary"),
    )(z, *([z] * (2 * ATT_WIN)), y, dy, lse, bias, *hook.ins)


def sgu_bwd(z, dy, amean, lng, lnb, wcat, wtcat, bfull, maskcat, *, tm):
    S = z.shape[0]
    n = S // tm

    def body(zu_ref, zv_ref, dy_ref, a_ref, lng_ref, lnb_ref, wcat_ref, wtcat_ref, bfull_ref, mask_ref,
             duv_ref, dw_ref, dsgb_ref, dlng_ref, dlnb_ref, vbd_ref, dbd_ref, w_acc, b_acc, g_acc, s_acc):
        i = pl.program_id(0)

        @pl.when(i == 0)
        def _():
            for r in (w_acc, b_acc, g_acc, s_acc):
                r[...] = jnp.zeros_like(r)

        def block(b, carry):
            rows = _sgu_rows(b)
            zu, zv, u, xhat, rstd, sv = _sgu_block(zu_ref, zv_ref, rows, a_ref, lng_ref, lnb_ref, wcat_ref, bfull_ref, vbd_ref)
            dyv = dy_ref[rows, :].astype(F32)
            duv_ref[rows, 0:512] = (dyv * sv * _gelu_grad(zu)).astype(BF16)
            dsv = dyv * u
            b_acc[...] += dsv
            w_acc[...] += _nt(dsv.astype(BF16), vbd_ref[...])
            _block_diag(dsv, dbd_ref)
            dvn = _nn(wtcat_ref[...], dbd_ref[...])
            g_acc[...] += dvn * xhat
            s_acc[...] += dvn
            dxh = dvn * lng_ref[...]
            dv = rstd * (dxh - _group_mean(dxh, a_ref) - xhat * _group_mean(dxh * xhat, a_ref))
            duv_ref[rows, 512:1024] = (dv * _gelu_grad(zv)).astype(BF16)
            return carry

        lax.fori_loop(0, tm // SG_BLOCK, block, 0)

        @pl.when(i == n - 1)
        def _():
            dw_ref[...] = w_acc[...] * mask_ref[...]
            dsgb_ref[...] = _group_mean(b_acc[...], a_ref) * float(SG_GROUP_DIM)
            dlng_ref[...] = jnp.sum(g_acc[...], axis=0, keepdims=True)
            dlnb_ref[...] = jnp.sum(s_acc[...], axis=0, keepdims=True)

    tile = lambda col: pl.BlockSpec((tm, 512), lambda i: (i, col))
    smalls = [amean, lng, lnb, wcat, wtcat, bfull, maskcat]
    wide = SG_GROUPS * SG_BLOCK
    return pl.pallas_call(
        body, grid=(n,), name="sgu_bwd",
        in_specs=[tile(3), tile(4), tile(0)] + [_full(a.shape) for a in smalls],
        out_specs=[pl.BlockSpec((tm, 1024), lambda i: (i, 0)), _full((SG_BLOCK, wide)), _full((SG_BLOCK, 512)),
                   _full((1, 512)), _full((1, 512))],
        out_shape=[jax.ShapeDtypeStruct((S, 1024), BF16), jax.ShapeDtypeStruct((SG_BLOCK, wide), F32),
                   jax.ShapeDtypeStruct((SG_BLOCK, 512), F32), jax.ShapeDtypeStruct((1, 512), F32),
                   jax.ShapeDtypeStruct((1, 512), F32)],
        scratch_shapes=[pltpu.VMEM((wide, 512), BF16), pltpu.VMEM((wide, 512), BF16), pltpu.VMEM((SG_BLOCK, wide), F32),
                        pltpu.VMEM((SG_BLOCK, 512), F32), pltpu.VMEM((SG_BLOCK, 512), F32), pltpu.VMEM((SG_BLOCK, 512), F32)],
        compiler_params=_params("arbitrary"),
    )(z, z, dy, *smalls)


def bias_colsum(p):
    H, _, L = p.shape
    near_lo = (ATT_WIN - 1) * ATT_TILE - REL_CLIP + 1
    near_hi = ATT_WIN * ATT_TILE

    def body(p_ref, gw_ref, far_ref):
        k = lax.broadcasted_iota(jnp.int32, (1, L), 1)
        is_far = (k < near_lo) | (k >= near_hi)
        for h in range(H):
            g = jnp.sum(p_ref[h], axis=0, keepdims=True)
            gw_ref[h:h + 1, :] = g
            far_ref[h:h + 1, :] = jnp.zeros((1, 128), F32) + jnp.sum(jnp.where(is_far, g, 0.0))

    return pl.pallas_call(
        body, name="bias_colsum", in_specs=[_full(p.shape)], out_specs=[_full((H, L)), _full((H, 128))],
        out_shape=[jax.ShapeDtypeStruct((H, L), F32), jax.ShapeDtypeStruct((H, 128), F32)], grid=(1,),
        compiler_params=_params("arbitrary"),
    )(p)


def _row_tile(rows, cols):
    best = None
    for tr in range(16, rows + 1, 16):
        if rows % tr == 0 and tr * cols * 4 <= EW_BLOCK_BYTES:
            best = tr
    return best if best is not None else rows


def place_shard(w, own, name):
    R, C = w.shape
    r = R // 2
    tr = _row_tile(r, C)
    nr = r // tr

    def body(own_ref, w_ref, o_ref):
        o_ref[...] = w_ref[...].astype(BF16)

    return pl.pallas_call(
        body, name=name,
        grid_spec=pltpu.PrefetchScalarGridSpec(
            num_scalar_prefetch=1, grid=(2, nr),
            in_specs=[pl.BlockSpec((tr, C), lambda h, i, own: (h * nr + i, 0))],
            out_specs=pl.BlockSpec((None, None, tr, C), lambda h, i, own: (own[0], h, i, 0))),
        out_shape=jax.ShapeDtypeStruct((4, 2, r, C), BF16), compiler_params=_params("parallel", "parallel"),
    )(own, w)


def chip_partial(grad, recv, core, name):
    _, _, r, C = grad.shape
    tr = _row_tile(r, C)

    def body(core_ref, g_ref, r_ref, o_ref):
        o_ref[...] = (g_ref[...].astype(F32) + r_ref[...].astype(F32)).astype(BF16)

    spec = pl.BlockSpec((None, tr, C), lambda j, i, core: (j, i, 0))
    return pl.pallas_call(
        body, name=name,
        grid_spec=pltpu.PrefetchScalarGridSpec(
            num_scalar_prefetch=1, grid=(4, r // tr),
            in_specs=[pl.BlockSpec((None, None, tr, C), lambda j, i, core: (j, core[0], i, 0)), spec], out_specs=spec),
        out_shape=jax.ShapeDtypeStruct((4, r, C), BF16), compiler_params=_params("parallel", "parallel"),
    )(core, grad, recv)


def shard_sum(part, recv, own, core, name):
    _, r, C = part.shape
    tr = _row_tile(r, C)

    def body(own_ref, core_ref, p_ref, r0, r1, r2, o_ref):
        o_ref[...] = p_ref[...].astype(F32) + r0[...].astype(F32) + r1[...].astype(F32) + r2[...].astype(F32)

    return pl.pallas_call(
        body, name=name,
        grid_spec=pltpu.PrefetchScalarGridSpec(
            num_scalar_prefetch=2, grid=(r // tr,),
            in_specs=[pl.BlockSpec((None, tr, C), lambda i, own, core: (own[0], i, 0))]
            + [pl.BlockSpec((None, tr, C), lambda i, own, core, k=k: (k, i, 0)) for k in range(3)],
            out_specs=pl.BlockSpec((None, tr, C), lambda i, own, core: (core[0], i, 0))),
        out_shape=jax.ShapeDtypeStruct((2, r, C), F32), compiler_params=_params("parallel"),
    )(own, core, part, recv, recv, recv)


def adamw(w, g, m, v, name):
    R, C = w.shape
    tr = _row_tile(R, C)

    def body(w_ref, g_ref, m_ref, v_ref, d_ref, nm_ref, nv_ref):
        gv = g_ref[...]
        nm = ADAM_B1 * m_ref[...] + (1.0 - ADAM_B1) * gv
        nv = ADAM_B2 * v_ref[...] + (1.0 - ADAM_B2) * (gv * gv)
        m_hat = nm / (1.0 - ADAM_B1 ** ADAM_STEP)
        v_hat = nv / (1.0 - ADAM_B2 ** ADAM_STEP)
        d_ref[...] = -ADAM_LR * (m_hat / (jnp.sqrt(v_hat) + ADAM_EPS) + ADAM_WD * w_ref[...])
        nm_ref[...] = nm
        nv_ref[...] = nv

    spec = pl.BlockSpec((tr, C), lambda i: (i, 0))
    return pl.pallas_call(
        body, grid=(R // tr,), name=name, in_specs=[spec] * 4, out_specs=[spec] * 3,
        out_shape=[jax.ShapeDtypeStruct((R, C), F32)] * 3, compiler_params=_params("parallel"),
    )(w, g, m, v)


HBM = pl.BlockSpec(memory_space=pl.ANY)


def _place():
    x, y, c = lax.axis_index("x"), lax.axis_index("y"), lax.axis_index("c")
    chips = [(1 - x, y), (x, 1 - y), (1 - x, 1 - y)]
    return x, y, c, chips


def _gather_phases(n):
    def copies(buf, sems, kind):
        send_sems, recv_sems = sems
        x, y, c, chips = _place()

        def remote(w, k, slot, to):
            return pltpu.make_async_remote_copy(src_ref=slot, dst_ref=slot, send_sem=send_sems.at[w, k],
                                                recv_sem=recv_sems.at[w, k], device_id=to, device_id_type=MESH)

        def one(w, k, px, py):
            if kind == "mine":
                return remote(w, k, buf[w].at[2 * x + y, c], (px, py, c))
            if kind == "theirs":
                return remote(w, k, buf[w].at[2 * px + py, c], (px, py, c))
            if kind == "onward":
                return remote(w, 3 + k, buf[w].at[2 * px + py, c], (x, y, 1 - c))
            return remote(w, 3 + k, buf[w].at[2 * px + py, 1 - c], (x, y, 1 - c))

        return [one(w, k, px, py) for w in range(n) for k, (px, py) in enumerate(chips)]

    def start(_, buf, sems):
        for cp in copies(buf, sems, "mine"):
            cp.start()

    def relay(_, buf, sems):
        for theirs, onward in zip(copies(buf, sems, "theirs"), copies(buf, sems, "onward")):
            theirs.wait_recv()
            onward.start()

    def finish(_, buf, sems):
        for cp in copies(buf, sems, "relayed"):
            cp.wait_recv()
        for cp in copies(buf, sems, "mine") + copies(buf, sems, "onward"):
            cp.wait_send()

    return start, relay, finish


def _gather_sems(n):
    return (pltpu.SemaphoreType.DMA((n, 6)), pltpu.SemaphoreType.DMA((n, 6)))


def gather_weights(bufs):
    n = len(bufs)
    phases = _gather_phases(n)

    def body(*refs):
        for phase in phases:
            phase(None, refs[n:2 * n], refs[2 * n:])

    return pl.pallas_call(
        body, name="gather_weights", in_specs=[HBM] * n, out_specs=[HBM] * n,
        out_shape=[jax.ShapeDtypeStruct(b.shape, b.dtype) for b in bufs],
        input_output_aliases={i: i for i in range(n)}, scratch_shapes=list(_gather_sems(n)),
    )(*bufs)


def gather_hook(bufs, n_steps):
    start, relay, finish = _gather_phases(len(bufs))
    return Hook(ins=tuple(bufs), out_shapes=tuple(jax.ShapeDtypeStruct(b.shape, b.dtype) for b in bufs), aliased=True,
                sems=_gather_sems(len(bufs)),
                steps=((0, "before", start), ((n_steps * 13) // 16, "before", relay), (n_steps - 1, "after", finish)))


def sibling_split(grads, name):
    n = len(grads)

    def body(*refs):
        src, dst = refs[:n], refs[n:2 * n]
        send_sems, recv_sems = refs[2 * n:]
        x, y, c, _ = _place()
        sends = [pltpu.make_async_remote_copy(src_ref=src[w].at[:, 1 - c], dst_ref=dst[w], send_sem=send_sems.at[w],
                                              recv_sem=recv_sems.at[w], device_id=(x, y, 1 - c), device_id_type=MESH)
                 for w in range(n)]
        for cp in sends:
            cp.start()
        for cp in sends:
            cp.wait()

    return pl.pallas_call(
        body, name=name, in_specs=[HBM] * n, out_specs=[HBM] * n,
        out_shape=[jax.ShapeDtypeStruct((4,) + g.shape[2:], g.dtype) for g in grads],
        scratch_shapes=[pltpu.SemaphoreType.DMA((n,))] * 2,
    )(*grads)


def exchange_hook(parts, n_steps):
    n = len(parts)

    def copies(src, dst, sems):
        send_sems, recv_sems = sems
        _, _, c, chips = _place()
        return [pltpu.make_async_remote_copy(
            src_ref=src[w].at[2 * px + py], dst_ref=dst[w].at[k], send_sem=send_sems.at[w, k],
            recv_sem=recv_sems.at[w, k], device_id=(px, py, c), device_id_type=MESH)
            for w in range(n) for k, (px, py) in enumerate(chips)]

    def start(src, dst, sems):
        for cp in copies(src, dst, sems):
            cp.start()

    def finish(src, dst, sems):
        for cp in copies(src, dst, sems):
            cp.wait()

    return Hook(ins=tuple(parts), out_shapes=tuple(jax.ShapeDtypeStruct((3,) + p.shape[1:], p.dtype) for p in parts),
                sems=(pltpu.SemaphoreType.DMA((n, 3)), pltpu.SemaphoreType.DMA((n, 3))),
                steps=((0, "before", start), (n_steps - 1, "after", finish)))


def sibling_join(sums):
    n = len(sums)

    def body(*refs):
        buf = refs[n:2 * n]
        send_sems, recv_sems = refs[2 * n:]
        x, y, c, _ = _place()
        sends = [pltpu.make_async_remote_copy(src_ref=buf[w].at[c], dst_ref=buf[w].at[c], send_sem=send_sems.at[w],
                                              recv_sem=recv_sems.at[w], device_id=(x, y, 1 - c), device_id_type=MESH)
                 for w in range(n)]
        for cp in sends:
            cp.start()
        for w, cp in enumerate(sends):
            cp.wait_send()
            pltpu.make_async_remote_copy(src_ref=buf[w].at[c], dst_ref=buf[w].at[1 - c], send_sem=send_sems.at[w],
                                         recv_sem=recv_sems.at[w], device_id=(x, y, 1 - c), device_id_type=MESH).wait_recv()

    return pl.pallas_call(
        body, name="sibling_join", in_specs=[HBM] * n, out_specs=[HBM] * n,
        out_shape=[jax.ShapeDtypeStruct(s.shape, s.dtype) for s in sums],
        input_output_aliases={i: i for i in range(n)},
        scratch_shapes=[pltpu.SemaphoreType.DMA((n,))] * 2,
    )(*sums)


def allreduce_small(v):
    R, C = v.shape

    def body(v_ref, out_ref, all_ref, send_sems, recv_sems):
        x, y, c, chips = _place()
        me, sibling = (x, y, c), (x, y, 1 - c)

        def slot(px, py, pc):
            return all_ref.at[4 * px + 2 * py + pc]

        def copy(k, block, to, src=None):
            return pltpu.make_async_remote_copy(src_ref=slot(*block) if src is None else src, dst_ref=slot(*block),
                                                send_sem=send_sems.at[k], recv_sem=recv_sems.at[k], device_id=to,
                                                device_id_type=MESH)

        first = [copy(0, me, sibling, src=v_ref)] + [copy(1 + j, me, (*chip, c), src=v_ref) for j, chip in enumerate(chips)]
        for cp in first:
            cp.start()
        slot(*me)[...] = v_ref[...]
        passed = [copy(4 + j, (*chip, c), sibling) for j, chip in enumerate(chips)]
        for j, chip in enumerate(chips):
            copy(1 + j, (*chip, c), me).wait_recv()
            passed[j].start()
        copy(0, sibling, me).wait_recv()
        for j, chip in enumerate(chips):
            copy(4 + j, (*chip, 1 - c), me).wait_recv()
        for cp in first + passed:
            cp.wait_send()
        acc = all_ref[0]
        for k in range(1, 8):
            acc = acc + all_ref[k]
        out_ref[...] = acc

    vmem = pl.BlockSpec(memory_space=pltpu.VMEM)
    return pl.pallas_call(
        body, name="allreduce_small", in_specs=[vmem], out_specs=vmem, out_shape=jax.ShapeDtypeStruct((R, C), F32),
        scratch_shapes=[pltpu.VMEM((8, R, C), F32), pltpu.SemaphoreType.DMA((7,)), pltpu.SemaphoreType.DMA((7,))],
        compiler_params=pltpu.CompilerParams(vmem_limit_bytes=VMEM_LIMIT_V7X),
    )(v)


SMALL_PAD = 1024


def _pack_small(arrs):
    parts = []
    for a in arrs:
        f = a.reshape(-1).astype(F32)
        parts.append(jnp.pad(f, (0, (-f.shape[0]) % SMALL_PAD)))
    return jnp.concatenate(parts).reshape(-1, 128)


def _unpack_small(packed, shapes):
    flat = packed.reshape(-1)
    outs, off = [], 0
    for s in shapes:
        size = int(np.prod(s))
        outs.append(flat[off:off + size].reshape(s))
        off += size + (-size) % SMALL_PAD
    return outs


ATT_KEYS = ATT_WIN * ATT_TILE
ATT_NEAR_LO = (ATT_WIN - 1) * ATT_TILE - REL_CLIP + 1
ATT_PERIOD = ATT_KEYS + ATT_TILE + 1


def _att_bias_table(rel_bias):
    far = rel_bias[:, 2 * REL_CLIP:]
    near = rel_bias[:, 2 * REL_CLIP - 1:0:-1]
    w = jnp.concatenate([jnp.broadcast_to(far, (ATT_HEADS, ATT_NEAR_LO)), near,
                         jnp.broadcast_to(far, (ATT_HEADS, ATT_PERIOD - ATT_KEYS))], axis=1)
    rows = jnp.tile(w, (1, ATT_TILE))[:, :ATT_TILE * (ATT_PERIOD - 1)].reshape(ATT_HEADS, ATT_TILE, ATT_PERIOD - 1)
    i = np.arange(ATT_TILE)[:, None]
    m = np.arange(ATT_KEYS)[None, :]
    qc, kc = i // CHUNK, m // CHUNK
    band = (kc >= qc) & (kc <= qc + N_PREV_CHUNKS)
    return jnp.where(band[None], rows[:, :, :ATT_KEYS], NEG_INF)


def _rel_bias_grad(dbias):
    p = jnp.pad(dbias, ((0, 0), (0, 0), (0, ATT_PERIOD - 1 - ATT_KEYS))).reshape(ATT_HEADS, -1)
    p = jnp.pad(p, ((0, 0), (0, ATT_TILE))).reshape(ATT_HEADS, ATT_TILE, ATT_PERIOD)
    gw, far = bias_colsum(p)
    return jnp.concatenate([jnp.zeros((ATT_HEADS, 1), F32), gw[:, ATT_KEYS - 1:ATT_NEAR_LO - 1:-1], far[:, :1]], axis=1)


def kernel(x, mem, norm_mix_g, w_in, rel_bias, sg_ln_g, sg_ln_b, sg_w, sg_b, w_branch_att, w_branch_sg, w_out, norm_xattn_g, norm_mem_g, w_xq, w_xkv, w_xo, norm_ffn_g, w_ffn_in, w_ffn_out, norm_final_g, loss_target, m_norm_mix_g, m_w_in, m_rel_bias, m_sg_ln_g, m_sg_ln_b, m_sg_w, m_sg_b, m_w_branch_att, m_w_branch_sg, m_w_out, m_norm_xattn_g, m_norm_mem_g, m_w_xq, m_w_xkv, m_w_xo, m_norm_ffn_g, m_w_ffn_in, m_w_ffn_out, m_norm_final_g, v_norm_mix_g, v_w_in, v_rel_bias, v_sg_ln_g, v_sg_ln_b, v_sg_w, v_sg_b, v_w_branch_att, v_w_branch_sg, v_w_out, v_norm_xattn_g, v_norm_mem_g, v_w_xq, v_w_xkv, v_w_xo, v_norm_ffn_g, v_w_ffn_in, v_w_ffn_out, v_norm_final_g):
    S, D = x.shape[1], x.shape[2]
    x2d, mem2d, tgt = x[0], mem[0], loss_target[0]

    big_names = ["w_in", "w_branch_att", "w_branch_sg", "w_out", "w_xq", "w_xkv", "w_xo", "w_ffn_in", "w_ffn_out"]
    col_sharded = [True, True, True, False, False, True, False, True, False]
    big_w = [a[0] for a in (w_in, w_branch_att, w_branch_sg, w_out, w_xq, w_xkv, w_xo, w_ffn_in, w_ffn_out)]
    big_m = [a[0] for a in (m_w_in, m_w_branch_att, m_w_branch_sg, m_w_out, m_w_xq, m_w_xkv, m_w_xo, m_w_ffn_in, m_w_ffn_out)]
    big_v = [a[0] for a in (v_w_in, v_w_branch_att, v_w_branch_sg, v_w_out, v_w_xq, v_w_xkv, v_w_xo, v_w_ffn_in, v_w_ffn_out)]

    own = (2 * lax.axis_index("x") + lax.axis_index("y")).astype(jnp.int32).reshape(1)
    core = lax.axis_index("c").astype(jnp.int32).reshape(1)
    placed = [place_shard(w, own, "place_" + nm) for w, nm in zip(big_w, big_names)]

    def whole(g4, i):
        R, C = big_w[i].shape
        return g4.reshape(4, R, C) if col_sharded[i] else g4.reshape(4 * R, C)

    W_in = whole(gather_weights(placed[:1])[0], 0)

    g_mix, g_xat, g_mem, g_ffn = norm_mix_g, norm_xattn_g, norm_mem_g, norm_ffn_g
    g_fin = norm_final_g.reshape(1, D)
    bias = _att_bias_table(rel_bias[0])
    tt = np.arange(SG_BLOCK)
    sg_mask = (tt[None, :] // CHUNK) <= (tt[:, None] // CHUNK)
    wm_f = jnp.where(sg_mask[None], sg_w[0], 0.0)
    wide = SG_GROUPS * SG_BLOCK
    wcat = jnp.transpose(wm_f, (1, 0, 2)).reshape(SG_BLOCK, wide).astype(BF16)
    wtcat = jnp.transpose(wm_f, (2, 0, 1)).reshape(SG_BLOCK, wide).astype(BF16)
    maskcat = jnp.asarray(np.tile(sg_mask, (1, SG_GROUPS)), F32)
    bfull = jnp.repeat(sg_b[0].T, SG_GROUP_DIM, axis=1)
    lng, lnb = sg_ln_g[0].reshape(1, 512), sg_ln_b[0].reshape(1, 512)
    gid = np.arange(512) // SG_GROUP_DIM
    amean = jnp.asarray((gid[:, None] == gid[None, :]) / SG_GROUP_DIM, BF16)

    h1, z = norm_mm(x2d, g_mix, W_in, tm=256, scale=None, name="norm_mm_in")
    y_att, lse_att, *gathered = attn_fwd(z, bias, gather_hook(placed[1:], S // ATT_TILE))
    W_ba, W_bs, W_out, W_xq, W_xkv, W_xo, W_fi, W_fo = [whole(g4, i + 1) for i, g4 in enumerate(gathered)]
    y_sg = sgu_fwd(z, amean, lng, lnb, wcat, bfull, tm=512)
    a_br, b_br, merged, x1 = merge_fwd(y_att, y_sg, z, x2d, W_ba, W_bs, W_out, tm=256)
    h2, xq = norm_mm(x1, g_xat, W_xq.reshape(1, D, D), tm=512, scale=XATT_HEAD_DIM ** -0.5, name="norm_mm_xq")
    mn, kv = norm_mm(mem2d, g_mem, W_xkv, tm=mem2d.shape[0], scale=None, name="norm_mm_kv")
    o_x, lse_x, x2 = xattn_fwd(xq, kv, W_xo, x1, tm=512)
    h3, gu = norm_mm(x2, g_ffn, W_fi, tm=256, scale=None, name="norm_mm_ffn")
    act, x3 = ffn_out_fwd(gu, W_fo, x2, tm=256)
    loss_vec, dx3, dg_fin = loss_head(x3, g_fin, tgt, tm=512)

    dgu = ffn_act_bwd(dx3, gu, W_fo, tm=256, nchunk=2)
    gW_fo = tn_mm(act, dx3, shards=1, tn=D, tk=512, name="dw_ffn_out")
    gW_fi = tn_mm(h3, dgu, shards=4, tn=W_fi.shape[2], tk=512, name="dw_ffn_in")
    dx2, dg_ffn = mm_nt_norm_bwd(dgu, W_fi, x2, g_ffn, dx3, tm=256, name="dx_ffn")
    dq_x, dkv = xattn_bwd(dx2, xq, o_x, lse_x, kv, W_xo, tm=512)
    gW_xo = tn_mm(o_x, dx2, shards=1, tn=D, tk=512, name="dw_xo")
    gW_xq = tn_mm(h2, dq_x, shards=1, tn=D, tk=512, name="dw_xq")
    dx1, dg_xat = mm_nt_norm_bwd(dq_x, W_xq.reshape(1, D, D), x1, g_xat, dx2, tm=512, name="dx_xq")
    gW_xkv = tn_mm(mn, dkv, shards=4, tn=2 * D, tk=mem2d.shape[0], name="dw_xkv")
    _, dg_mem = mm_nt_norm_bwd(dkv.astype(BF16), W_xkv, mem2d, g_mem, None, tm=mem2d.shape[0], name="dx_mem")
    da, db, dgab, dy_att, dy_sg = merge_bwd(dx1, W_out, a_br, b_br, z, W_ba, W_bs, tm=256)
    gW_out = tn_mm(merged, dx1, shards=1, tn=D, tk=512, name="dw_out")
    gW_ba = tn_mm(y_att, da, shards=4, tn=D, tk=512, name="dw_branch_att")
    gW_bs = tn_mm(y_sg, db, shards=4, tn=D, tk=512, name="dw_branch_sg")

    def canon(g, i):
        R, C = big_w[i].shape
        return g.reshape(4, 2, R // 2, C)

    def stage1(grads, idx, tag):
        gs = [canon(g, i) for g, i in zip(grads, idx)]
        recv = sibling_split(gs, "sibling_split_" + tag)
        return [chip_partial(g, r, core, "chip_partial_" + big_names[i]) for g, r, i in zip(gs, recv, idx)]

    parts_b = stage1([gW_ba, gW_bs, gW_out, gW_xq, gW_xkv, gW_xo, gW_fi, gW_fo], range(1, 9), "late")
    nt_bwd = S // ATT_TILE + ATT_WIN - 1
    dq_a, dk_a, dv_a, dbias, *from_chips_b = attn_bwd(z, y_att, dy_att, lse_att, bias, exchange_hook(parts_b, nt_bwd))
    duv, dwcat, dsgb_full, dlng, dlnb = sgu_bwd(z, dy_sg, amean, lng, lnb, wcat, wtcat, bfull, maskcat, tm=512)
    dwm = jnp.transpose(dwcat.reshape(SG_BLOCK, SG_GROUPS, SG_BLOCK), (1, 0, 2))
    dsgb = dsgb_full[:, ::SG_GROUP_DIM].T
    dz = jnp.concatenate([dq_a, dk_a, dv_a, duv, dgab], axis=1)
    gW_in = tn_mm(h1, dz, shards=4, tn=W_in.shape[2], tk=512, name="dw_in")
    parts_a = stage1([gW_in], [0], "w_in")
    dx, dg_mix, *from_chips_a = mm_nt_norm_bwd(dz, W_in, x2d, g_mix, dx1, tm=256, name="dx_in",
                                               hook=exchange_hook(parts_a, S // 256))
    d_rel = _rel_bias_grad(dbias)

    parts, from_chips = parts_a + parts_b, from_chips_a + from_chips_b
    sums = [shard_sum(p, r, own, core, "shard_sum_" + nm) for p, r, nm in zip(parts, from_chips, big_names)]
    joined = sibling_join(sums)
    big_out = []
    for j, w, m_, v_, nm in zip(joined, big_w, big_m, big_v, big_names):
        g = j.reshape(w.shape)
        big_out.append((g,) + tuple(adamw(w, g, m_, v_, "adamw_" + nm)))

    small_w = [norm_mix_g, rel_bias, sg_ln_g, sg_ln_b, sg_w, sg_b, norm_xattn_g, norm_mem_g, norm_ffn_g, norm_final_g]
    small_m = [m_norm_mix_g, m_rel_bias, m_sg_ln_g, m_sg_ln_b, m_sg_w, m_sg_b, m_norm_xattn_g, m_norm_mem_g, m_norm_ffn_g, m_norm_final_g]
    small_v = [v_norm_mix_g, v_rel_bias, v_sg_ln_g, v_sg_ln_b, v_sg_w, v_sg_b, v_norm_xattn_g, v_norm_mem_g, v_norm_ffn_g, v_norm_final_g]
    small_g = [dg_mix, d_rel, dlng, dlnb, dwm, dsgb, dg_xat, dg_mem, dg_ffn, dg_fin]
    shapes = [w.shape for w in small_w]
    g_sum = allreduce_small(_pack_small(small_g + [loss_vec[0, :1]]))
    zero = jnp.zeros((1,), F32)
    d_s, m_s, v_s = adamw(_pack_small(small_w + [zero]), g_sum, _pack_small(small_m + [zero]), _pack_small(small_v + [zero]),
                          "adamw_small")
    sg_, sd, sm, sv_ = (_unpack_small(p, shapes + [(1,)]) for p in (g_sum, d_s, m_s, v_s))
    loss = sg_[-1][0]

    order = ["norm_mix_g", "w_in", "rel_bias", "sg_ln_g", "sg_ln_b", "sg_w", "sg_b", "w_branch_att", "w_branch_sg", "w_out",
             "norm_xattn_g", "norm_mem_g", "w_xq", "w_xkv", "w_xo", "norm_ffn_g", "w_ffn_in", "w_ffn_out", "norm_final_g"]
    small_names = ["norm_mix_g", "rel_bias", "sg_ln_g", "sg_ln_b", "sg_w", "sg_b", "norm_xattn_g", "norm_mem_g", "norm_ffn_g",
                   "norm_final_g"]
    res = {}
    for i, nm in enumerate(small_names):
        res[nm] = (sg_[i], sd[i], sm[i], sv_[i])
    for nm, outs in zip(big_names, big_out):
        res[nm] = tuple(o[None] for o in outs)
    return (loss, dx[None], *[res[nm][0] for nm in order], *[res[nm][1] for nm in order],
            *[res[nm][2] for nm in order], *[res[nm][3] for nm in order])
```

```python
import functools
from typing import NamedTuple

import numpy as np
import jax
import jax.numpy as jnp
from jax import lax
from jax.experimental import pallas as pl
from jax.experimental.pallas import tpu as pltpu

F32, BF16 = jnp.float32, jnp.bfloat16
MESH = pl.DeviceIdType.MESH

EPS = 1e-6
NEG_INF = -1e30
CHUNK = 64
N_PREV_CHUNKS = 8
ATT_HEADS, ATT_HEAD_DIM = 8, 64
REL_CLIP = 128
SG_BLOCK, SG_GROUPS, SG_GROUP_DIM = 128, 8, 64
XATT_HEADS, XATT_HEAD_DIM = 4, 256
ATT_TILE = 128
ATT_WIN = 5
ADAM_LR, ADAM_B1, ADAM_B2, ADAM_EPS, ADAM_WD, ADAM_STEP = 0.001, 0.9, 0.999, 1e-08, 0.01, 10

VMEM_LIMIT_V7X = 56 * 1024 * 1024
EW_BLOCK_BYTES = 1 << 20
ROW_TILE = 512
HBM = pl.BlockSpec(memory_space=pl.ANY)


def _params(*sem):
    return pltpu.CompilerParams(dimension_semantics=sem, vmem_limit_bytes=VMEM_LIMIT_V7X)


def _full(shape):
    n = len(shape)
    return pl.BlockSpec(shape, lambda *_: (0,) * n, pipeline_mode=pl.Buffered(1))


def _acc(shape):
    n = len(shape)
    return pl.BlockSpec(shape, lambda *_: (0,) * n)


class Hook(NamedTuple):
    ins: tuple = ()
    out_shapes: tuple = ()
    aliased: bool = False
    sems: tuple = ()
    steps: tuple = ()


def _run_hook(hook, pos, h_in, h_out, h_sems):
    for step, where, fn in hook.steps:
        if where == pos:
            pl.when(pl.program_id(0) == step)(functools.partial(fn, h_in, h_out, h_sems))


def _hooked_call(body, hook, *, n_in, n_out, in_specs, out_specs, out_shape, scratch_shapes=(), **kw):
    nh = len(hook.ins)
    aliases = {n_in + i: n_out + i for i in range(nh)} if hook.aliased else {}
    return pl.pallas_call(
        body, in_specs=list(in_specs) + [HBM] * nh, out_specs=list(out_specs) + [HBM] * len(hook.out_shapes),
        out_shape=list(out_shape) + list(hook.out_shapes), scratch_shapes=list(scratch_shapes) + list(hook.sems),
        input_output_aliases=aliases, **kw)


def _nt(a, b):
    return lax.dot_general(a, b, (((1,), (1,)), ((), ())), preferred_element_type=F32)


def _tn(a, b):
    return lax.dot_general(a, b, (((0,), (0,)), ((), ())), preferred_element_type=F32)


def _nn(a, b):
    return jnp.dot(a, b, preferred_element_type=F32)


def _sigmoid(x):
    return 1.0 / (1.0 + jnp.exp(-x))


_GELU_C = float(np.sqrt(2.0 / np.pi))


def _gelu(x):
    t = jnp.tanh(_GELU_C * (x + 0.044715 * (x * x * x)))
    return x * (0.5 * (1.0 + t))


def _gelu_grad(x):
    t = jnp.tanh(_GELU_C * (x + 0.044715 * (x * x * x)))
    return 0.5 * (1.0 + t) + 0.5 * x * (1.0 - t * t) * (_GELU_C * (1.0 + 3.0 * 0.044715 * x * x))


def _rms_stats(xf):
    rstd = lax.rsqrt(jnp.mean(xf * xf, axis=-1, keepdims=True) + EPS)
    return rstd, xf * rstd


def _rms_bwd(xhat, rstd, g, dh):
    dxh = dh * g
    dx = rstd * (dxh - xhat * jnp.mean(dxh * xhat, axis=-1, keepdims=True))
    return dx, dh * xhat


def norm_mm(x, g, w, *, tm, scale, name):
    S, D = x.shape
    J, _, C = w.shape

    def body(x_ref, g_ref, w_ref, h_ref, z_ref):
        _, xhat = _rms_stats(x_ref[...])
        h = (xhat * g_ref[...]).astype(BF16)
        h_ref[...] = h
        for j in range(J):
            acc = _nn(h, w_ref[j])
            if scale is not None:
                acc = acc * scale
            z_ref[:, j * C:(j + 1) * C] = acc.astype(BF16)

    return pl.pallas_call(
        body, grid=(S // tm,), name=name,
        in_specs=[pl.BlockSpec((tm, D), lambda i: (i, 0)), _full((1, D)), _full((J, D, C))],
        out_specs=[pl.BlockSpec((tm, D), lambda i: (i, 0)), pl.BlockSpec((tm, J * C), lambda i: (i, 0))],
        out_shape=[jax.ShapeDtypeStruct((S, D), BF16), jax.ShapeDtypeStruct((S, J * C), BF16)],
        compiler_params=_params("parallel"),
    )(x, g, w)


def _att_window_specs(nt, col):
    return [pl.BlockSpec((ATT_TILE, 512), lambda t, j=j: (jnp.clip(t - (ATT_WIN - 1) + j, 0, nt - 1), col))
            for j in range(ATT_WIN)]


ATT_SCALE = ATT_HEAD_DIM ** -0.5


def _att_scores(q_scaled, k, bias, valid):
    return jnp.where(valid, _nt(q_scaled, k) + bias, NEG_INF)


def _att_valid(t):
    kpos = lax.broadcasted_iota(jnp.int32, (ATT_TILE, ATT_WIN * ATT_TILE), 1) + (t - (ATT_WIN - 1)) * ATT_TILE
    return kpos >= 0


def attn_fwd(z, bias, hook=Hook()):
    S = z.shape[0]
    nt = S // ATT_TILE
    n_in, nh, nho = 2 + 2 * ATT_WIN, len(hook.ins), len(hook.out_shapes)

    def body(q_ref, *refs):
        k_refs, v_refs = refs[:ATT_WIN], refs[ATT_WIN:2 * ATT_WIN]
        bias_ref = refs[2 * ATT_WIN]
        rest = refs[2 * ATT_WIN + 1:]
        h_in, (y_ref, lse_ref), h_out = rest[:nh], rest[nh:nh + 2], rest[nh + 2:nh + 2 + nho]
        s_scr, p_scr = rest[nh + 2 + nho:nh + 4 + nho]
        h_sems = rest[nh + 4 + nho:]
        _run_hook(hook, "before", h_in, h_out, h_sems)
        valid = _att_valid(pl.program_id(0))
        heads = [slice(h * ATT_HEAD_DIM, (h + 1) * ATT_HEAD_DIM) for h in range(ATT_HEADS)]
        for h, sl in enumerate(heads):
            k = jnp.concatenate([r[:, sl] for r in k_refs], axis=0)
            s_scr[h] = _att_scores(q_ref[:, sl] * ATT_SCALE, k, bias_ref[h], valid)
        stats = []
        for h in range(ATT_HEADS):
            s = s_scr[h]
            m = jnp.max(s, axis=-1, keepdims=True)
            p = jnp.exp(s - m)
            stats.append((m, jnp.sum(p, axis=-1, keepdims=True)))
            p_scr[h] = p.astype(BF16)
        for h, sl in enumerate(heads):
            m, l = stats[h]
            v = jnp.concatenate([r[:, sl] for r in v_refs], axis=0)
            y_ref[:, sl] = (_nn(p_scr[h], v) / l).astype(BF16)
            lse_ref[:, h:h + 1] = m + jnp.log(l)
        _run_hook(hook, "after", h_in, h_out, h_sems)

    tile = lambda col: pl.BlockSpec((ATT_TILE, 512), lambda t: (t, col))
    return _hooked_call(
        body, hook, n_in=n_in, n_out=2, grid=(nt,), name="attn_fwd",
        in_specs=[tile(0)] + _att_window_specs(nt, 1) + _att_window_specs(nt, 2) + [_full(bias.shape)],
        out_specs=[tile(0), pl.BlockSpec((ATT_TILE, ATT_HEADS), lambda t: (t, 0))],
        out_shape=[jax.ShapeDtypeStruct((S, 512), BF16), jax.ShapeDtypeStruct((S, ATT_HEADS), F32)],
        scratch_shapes=[pltpu.VMEM((ATT_HEADS, ATT_TILE, ATT_WIN * ATT_TILE), F32),
                        pltpu.VMEM((ATT_HEADS, ATT_TILE, ATT_WIN * ATT_TILE), BF16)],
        compiler_params=_params("arbitrary"),
    )(z, *([z] * (2 * ATT_WIN)), bias, *hook.ins)


def _group_mean(x, a_ref):
    hi = x.astype(BF16)
    lo = (x - hi.astype(F32)).astype(BF16)
    return _nn(hi, a_ref[...]) + _nn(lo, a_ref[...])


def _block_diag(x, ref):
    group = lax.broadcasted_iota(jnp.int32, x.shape, 1) // SG_GROUP_DIM
    for g in range(SG_GROUPS):
        ref[g * SG_BLOCK:(g + 1) * SG_BLOCK, :] = jnp.where(group == g, x, 0.0).astype(BF16)


def _sgu_block(zu_ref, zv_ref, rows, a_ref, lng_ref, lnb_ref, wcat_ref, bfull_ref, vbd_ref):
    zu = zu_ref[rows, :].astype(F32)
    zv = zv_ref[rows, :].astype(F32)
    u, v = _gelu(zu), _gelu(zv)
    vc = v - _group_mean(v, a_ref)
    rstd = lax.rsqrt(_group_mean(vc * vc, a_ref) + EPS)
    xhat = vc * rstd
    _block_diag(xhat * lng_ref[...] + lnb_ref[...], vbd_ref)
    sv = _nn(wcat_ref[...], vbd_ref[...]) + bfull_ref[...]
    return zu, zv, u, xhat, rstd, sv


def _sgu_rows(b):
    return pl.ds(pl.multiple_of(b * SG_BLOCK, SG_BLOCK), SG_BLOCK)


def sgu_fwd(z, amean, lng, lnb, wcat, bfull, *, tm):
    S = z.shape[0]

    def body(zu_ref, zv_ref, a_ref, lng_ref, lnb_ref, wcat_ref, bfull_ref, y_ref, vbd_ref):
        def block(b, carry):
            rows = _sgu_rows(b)
            _, _, u, _, _, sv = _sgu_block(zu_ref, zv_ref, rows, a_ref, lng_ref, lnb_ref, wcat_ref, bfull_ref, vbd_ref)
            y_ref[rows, :] = (u * sv).astype(BF16)
            return carry

        lax.fori_loop(0, tm // SG_BLOCK, block, 0)

    tile = lambda col: pl.BlockSpec((tm, 512), lambda i: (i, col))
    smalls = [amean, lng, lnb, wcat, bfull]
    return pl.pallas_call(
        body, grid=(S // tm,), name="sgu_fwd",
        in_specs=[tile(3), tile(4)] + [_full(a.shape) for a in smalls],
        out_specs=tile(0), out_shape=jax.ShapeDtypeStruct((S, 512), BF16),
        scratch_shapes=[pltpu.VMEM((SG_GROUPS * SG_BLOCK, 512), BF16)],
        compiler_params=_params("parallel"),
    )(z, z, *smalls)


def _gate_specs(tm):
    return [pl.BlockSpec((tm, 512), lambda i, c=c: (i, c)) for c in (5, 6, 7, 8)]


def merge_fwd(y_att, y_sg, z, x, w_ba, w_bs, w_out, *, tm):
    S, D = x.shape
    J, _, C = w_ba.shape

    def body(ya_ref, ys_ref, g0, g1, g2, g3, x_ref, wba_ref, wbs_ref, wo_ref, a_ref, b_ref, m_ref, x1_ref):
        ya, ys = ya_ref[...], ys_ref[...]
        a = jnp.concatenate([_nn(ya, wba_ref[j]) for j in range(J)], axis=1)
        b = jnp.concatenate([_nn(ys, wbs_ref[j]) for j in range(J)], axis=1)
        ga = jnp.concatenate([g0[...], g1[...]], axis=1).astype(F32)
        gb = jnp.concatenate([g2[...], g3[...]], axis=1).astype(F32)
        a_ref[...] = a.astype(BF16)
        b_ref[...] = b.astype(BF16)
        merged = (_sigmoid(ga) * a + _sigmoid(gb) * b).astype(BF16)
        m_ref[...] = merged
        x1_ref[...] = x_ref[...] + _nn(merged, wo_ref[...])

    row = lambda n: pl.BlockSpec((tm, n), lambda i: (i, 0))
    return pl.pallas_call(
        body, grid=(S // tm,), name="merge_fwd",
        in_specs=[row(512), row(512)] + _gate_specs(tm) + [row(D), _full(w_ba.shape), _full(w_bs.shape), _full(w_out.shape)],
        out_specs=[row(D)] * 4,
        out_shape=[jax.ShapeDtypeStruct((S, D), BF16)] * 3 + [jax.ShapeDtypeStruct((S, D), F32)],
        compiler_params=_params("parallel"),
    )(y_att, y_sg, z, z, z, z, x, w_ba, w_bs, w_out)


def xattn_fwd(xq, kv, w_xo, x1, *, tm):
    S, D = xq.shape
    dh = XATT_HEAD_DIM

    def body(q_ref, kv_ref, wo_ref, x1_ref, o_ref, lse_ref, x2_ref):
        outs = []
        for h in range(XATT_HEADS):
            s = _nt(q_ref[:, h * dh:(h + 1) * dh], kv_ref[:, h * dh:(h + 1) * dh])
            m = jnp.max(s, axis=-1, keepdims=True)
            p = jnp.exp(s - m)
            l = jnp.sum(p, axis=-1, keepdims=True)
            outs.append((_nn(p.astype(BF16), kv_ref[:, D + h * dh:D + (h + 1) * dh]) / l).astype(BF16))
            lse_ref[:, h:h + 1] = m + jnp.log(l)
        o = jnp.concatenate(outs, axis=1)
        o_ref[...] = o
        x2_ref[...] = x1_ref[...] + _nn(o, wo_ref[...])

    row = lambda n: pl.BlockSpec((tm, n), lambda i: (i, 0))
    return pl.pallas_call(
        body, grid=(S // tm,), name="xattn_fwd",
        in_specs=[row(D), _full(kv.shape), _full(w_xo.shape), row(D)],
        out_specs=[row(D), row(XATT_HEADS), row(D)],
        out_shape=[jax.ShapeDtypeStruct((S, D), BF16), jax.ShapeDtypeStruct((S, XATT_HEADS), F32),
                   jax.ShapeDtypeStruct((S, D), F32)],
        compiler_params=_params("parallel"),
    )(xq, kv, w_xo, x1)


def ffn_out_fwd(gu, w, x2, *, tm):
    S, D = x2.shape
    F = w.shape[0]

    def body(gu_ref, w_ref, x2_ref, act_ref, x3_ref):
        gate = gu_ref[:, :F].astype(F32)
        up = gu_ref[:, F:].astype(F32)
        act = (gate * _sigmoid(gate) * up).astype(BF16)
        act_ref[...] = act
        x3_ref[...] = x2_ref[...] + _nn(act, w_ref[...])

    row = lambda n: pl.BlockSpec((tm, n), lambda i: (i, 0))
    return pl.pallas_call(
        body, grid=(S // tm,), name="ffn_out_fwd",
        in_specs=[row(2 * F), _full(w.shape), row(D)], out_specs=[row(F), row(D)],
        out_shape=[jax.ShapeDtypeStruct((S, F), BF16), jax.ShapeDtypeStruct((S, D), F32)],
        compiler_params=_params("parallel"),
    )(gu, w, x2)


def loss_head(x3, g, target, *, tm):
    S, D = x3.shape

    def body(x_ref, g_ref, t_ref, loss_ref, dx_ref, dg_ref):
        @pl.when(pl.program_id(0) == 0)
        def _():
            loss_ref[...] = jnp.zeros_like(loss_ref)
            dg_ref[...] = jnp.zeros_like(dg_ref)

        gv = g_ref[...]
        rstd, xhat = _rms_stats(x_ref[...])
        err = xhat * gv - t_ref[...]
        loss_ref[...] += 0.5 * jnp.sum(jnp.mean(err * err, axis=-1, keepdims=True))
        dx, dgc = _rms_bwd(xhat, rstd, gv, err * (1.0 / D))
        dx_ref[...] = dx
        dg_ref[...] += jnp.sum(dgc, axis=0, keepdims=True)

    row = pl.BlockSpec((tm, D), lambda i: (i, 0))
    return pl.pallas_call(
        body, grid=(S // tm,), name="loss_head",
        in_specs=[row, _full((1, D)), row], out_specs=[_acc((1, 128)), row, _acc((1, D))],
        out_shape=[jax.ShapeDtypeStruct((1, 128), F32), jax.ShapeDtypeStruct((S, D), F32),
                   jax.ShapeDtypeStruct((1, D), F32)],
        compiler_params=_params("arbitrary"),
    )(x3, g, target)


def ffn_act_bwd(dx3, gu, w, *, tm, nchunk):
    S, D = dx3.shape
    F = w.shape[0]
    cn = F // nchunk

    def body(dx_ref, gu_ref, w_ref, dgu_ref):
        dxb = dx_ref[...].astype(BF16)
        for j in range(nchunk):
            dact = _nt(dxb, w_ref[j * cn:(j + 1) * cn, :])
            gate = gu_ref[:, j * cn:(j + 1) * cn].astype(F32)
            up = gu_ref[:, F + j * cn:F + (j + 1) * cn].astype(F32)
            sg = _sigmoid(gate)
            dgu_ref[:, j * cn:(j + 1) * cn] = (dact * up * (sg * (1.0 + gate * (1.0 - sg)))).astype(BF16)
            dgu_ref[:, F + j * cn:F + (j + 1) * cn] = (dact * (gate * sg)).astype(BF16)

    row = lambda n: pl.BlockSpec((tm, n), lambda i: (i, 0))
    return pl.pallas_call(
        body, grid=(S // tm,), name="ffn_act_bwd",
        in_specs=[row(D), row(2 * F), _full(w.shape)], out_specs=row(2 * F),
        out_shape=jax.ShapeDtypeStruct((S, 2 * F), BF16), compiler_params=_params("parallel"),
    )(dx3, gu, w)


def tn_mm(a, b, *, shards, tn, tk, name):
    K, M = a.shape
    N = b.shape[1]
    C = N // shards
    per = tn // C
    nk = K // tk

    def body(a_ref, b_ref, o_ref, acc_ref):
        k = pl.program_id(1)

        @pl.when(k == 0)
        def _():
            acc_ref[...] = jnp.zeros_like(acc_ref)

        acc_ref[...] += _tn(a_ref[...].astype(BF16), b_ref[...].astype(BF16))

        @pl.when(k == nk - 1)
        def _():
            if shards > 1:
                for s in range(per):
                    o_ref[s] = acc_ref[:, s * C:(s + 1) * C].astype(BF16)
            else:
                o_ref[...] = acc_ref[...].astype(BF16)

    if shards > 1:
        out_spec = pl.BlockSpec((per, M, C), lambda n, k: (n, 0, 0))
        out_shape = jax.ShapeDtypeStruct((shards, M, C), BF16)
    else:
        out_spec = pl.BlockSpec((M, tn), lambda n, k: (0, n))
        out_shape = jax.ShapeDtypeStruct((M, N), BF16)
    return pl.pallas_call(
        body, grid=(N // tn, nk), name=name,
        in_specs=[pl.BlockSpec((tk, M), lambda n, k: (k, 0)), pl.BlockSpec((tk, tn), lambda n, k: (k, n))],
        out_specs=out_spec, out_shape=out_shape, scratch_shapes=[pltpu.VMEM((M, tn), F32)],
        compiler_params=_params("parallel", "arbitrary"),
    )(a, b)


def mm_nt_norm_bwd(dy, w, x, g, dx_in, *, tm, name, hook=Hook()):
    S, D = x.shape
    J, _, C = w.shape
    has_in = dx_in is not None
    n_in, nh, nho = 4 + has_in, len(hook.ins), len(hook.out_shapes)

    def body(*refs):
        dy_ref, w_ref, x_ref, g_ref = refs[:4]
        dxin_ref = refs[4] if has_in else None
        rest = refs[n_in:]
        h_in, (dx_ref, dg_ref), h_out, h_sems = rest[:nh], rest[nh:nh + 2], rest[nh + 2:nh + 2 + nho], rest[nh + 2 + nho:]
        _run_hook(hook, "before", h_in, h_out, h_sems)

        @pl.when(pl.program_id(0) == 0)
        def _():
            dg_ref[...] = jnp.zeros_like(dg_ref)

        dh = _nt(dy_ref[:, 0:C], w_ref[0])
        for j in range(1, J):
            dh += _nt(dy_ref[:, j * C:(j + 1) * C], w_ref[j])
        rstd, xhat = _rms_stats(x_ref[...])
        dx, dgc = _rms_bwd(xhat, rstd, g_ref[...], dh)
        dx_ref[...] = dx + dxin_ref[...] if has_in else dx
        dg_ref[...] += jnp.sum(dgc, axis=0, keepdims=True)
        _run_hook(hook, "after", h_in, h_out, h_sems)

    row = lambda n: pl.BlockSpec((tm, n), lambda i: (i, 0))
    ins = [dy, w, x, g] + ([dx_in] if has_in else [])
    return _hooked_call(
        body, hook, n_in=n_in, n_out=2, grid=(S // tm,), name=name,
        in_specs=[row(J * C), _full(w.shape), row(D), _full((1, D))] + ([row(D)] if has_in else []),
        out_specs=[row(D), _acc((1, D))],
        out_shape=[jax.ShapeDtypeStruct((S, D), F32), jax.ShapeDtypeStruct((1, D), F32)],
        compiler_params=_params("arbitrary"),
    )(*ins, *hook.ins)


def xattn_bwd(dx2, xq, o, lse, kv, w_xo, *, tm):
    S, D = xq.shape
    M = kv.shape[0]
    dh = XATT_HEAD_DIM

    def body(dx_ref, q_ref, o_ref, lse_ref, kv_ref, wo_ref, dq_ref, dkv_ref):
        @pl.when(pl.program_id(0) == 0)
        def _():
            dkv_ref[...] = jnp.zeros_like(dkv_ref)

        do = _nt(dx_ref[...].astype(BF16), wo_ref[...])
        for h in range(XATT_HEADS):
            hs = slice(h * dh, (h + 1) * dh)
            vs = slice(D + h * dh, D + (h + 1) * dh)
            q, k, v = q_ref[:, hs], kv_ref[:, hs], kv_ref[:, vs]
            do_h = do[:, hs]
            do_b = do_h.astype(BF16)
            p = jnp.exp(_nt(q, k) - lse_ref[:, h:h + 1])
            delta = jnp.sum(do_h * o_ref[:, hs].astype(F32), axis=-1, keepdims=True)
            ds = (p * (_nt(do_b, v) - delta)).astype(BF16)
            dq_ref[:, hs] = (_nn(ds, k) * (dh ** -0.5)).astype(BF16)
            dkv_ref[:, hs] += _tn(ds, q)
            dkv_ref[:, vs] += _tn(p.astype(BF16), do_b)

    row = lambda n: pl.BlockSpec((tm, n), lambda i: (i, 0))
    return pl.pallas_call(
        body, grid=(S // tm,), name="xattn_bwd",
        in_specs=[row(D), row(D), row(D), row(XATT_HEADS), _full(kv.shape), _full(w_xo.shape)],
        out_specs=[row(D), _acc((M, 2 * D))],
        out_shape=[jax.ShapeDtypeStruct((S, D), BF16), jax.ShapeDtypeStruct((M, 2 * D), F32)],
        compiler_params=_params("arbitrary"),
    )(dx2, xq, o, lse, kv, w_xo)


def merge_bwd(dx1, w_out, a, b, z, w_ba, w_bs, *, tm):
    S, D = dx1.shape
    J, W, C = w_ba.shape

    def body(dx_ref, wo_ref, a_ref, b_ref, g0, g1, g2, g3, wba_ref, wbs_ref, da_ref, db_ref, dg_ref, dya_ref, dys_ref):
        dm = _nt(dx_ref[...].astype(BF16), wo_ref[...])
        sa = _sigmoid(jnp.concatenate([g0[...], g1[...]], axis=1).astype(F32))
        sb = _sigmoid(jnp.concatenate([g2[...], g3[...]], axis=1).astype(F32))
        dg_ref[:, :D] = (dm * a_ref[...].astype(F32) * (sa * (1.0 - sa))).astype(BF16)
        dg_ref[:, D:] = (dm * b_ref[...].astype(F32) * (sb * (1.0 - sb))).astype(BF16)
        da = (dm * sa).astype(BF16)
        db = (dm * sb).astype(BF16)
        da_ref[...] = da
        db_ref[...] = db
        dya = _nt(da[:, 0:C], wba_ref[0])
        dys = _nt(db[:, 0:C], wbs_ref[0])
        for j in range(1, J):
            dya += _nt(da[:, j * C:(j + 1) * C], wba_ref[j])
            dys += _nt(db[:, j * C:(j + 1) * C], wbs_ref[j])
        dya_ref[...] = dya.astype(BF16)
        dys_ref[...] = dys.astype(BF16)

    row = lambda n: pl.BlockSpec((tm, n), lambda i: (i, 0))
    return pl.pallas_call(
        body, grid=(S // tm,), name="merge_bwd",
        in_specs=[row(D), _full(w_out.shape), row(D), row(D)] + _gate_specs(tm) + [_full(w_ba.shape), _full(w_bs.shape)],
        out_specs=[row(D), row(D), row(2 * D), row(W), row(W)],
        out_shape=[jax.ShapeDtypeStruct((S, D), BF16)] * 2 + [jax.ShapeDtypeStruct((S, 2 * D), BF16)]
        + [jax.ShapeDtypeStruct((S, W), BF16)] * 2,
        compiler_params=_params("parallel"),
    )(dx1, w_out, a, b, z, z, z, z, w_ba, w_bs)


def attn_bwd(z, y, dy, lse, bias, hook=Hook()):
    S = z.shape[0]
    nt = S // ATT_TILE
    back = ATT_WIN - 1
    n_in, nh, nho = 5 + 2 * ATT_WIN, len(hook.ins), len(hook.out_shapes)

    def body(q_ref, *refs):
        k_refs, v_refs = refs[:ATT_WIN], refs[ATT_WIN:2 * ATT_WIN]
        y_ref, dy_ref, lse_ref, bias_ref = refs[2 * ATT_WIN:2 * ATT_WIN + 4]
        rest = refs[2 * ATT_WIN + 4:]
        h_in, (dq_ref, dk_ref, dv_ref, dbias_ref), h_out = rest[:nh], rest[nh:nh + 4], rest[nh + 4:nh + 4 + nho]
        dk_acc, dv_acc = rest[nh + 4 + nho:nh + 6 + nho]
        h_sems = rest[nh + 6 + nho:]
        t = pl.program_id(0)
        _run_hook(hook, "before", h_in, h_out, h_sems)

        @pl.when(t == 0)
        def _():
            dbias_ref[...] = jnp.zeros_like(dbias_ref)
            dk_acc[...] = jnp.zeros_like(dk_acc)
            dv_acc[...] = jnp.zeros_like(dv_acc)

        @pl.when(t < nt)
        def _():
            valid = _att_valid(t)
            for h in range(ATT_HEADS):
                sl = slice(h * ATT_HEAD_DIM, (h + 1) * ATT_HEAD_DIM)
                k = jnp.concatenate([r[:, sl] for r in k_refs], axis=0)
                v = jnp.concatenate([r[:, sl] for r in v_refs], axis=0)
                q, do_b = q_ref[:, sl] * ATT_SCALE, dy_ref[:, sl]
                p = jnp.exp(_att_scores(q, k, bias_ref[h], valid) - lse_ref[:, h:h + 1])
                delta = jnp.sum(do_b.astype(F32) * y_ref[:, sl].astype(F32), axis=-1, keepdims=True)
                ds = p * (_nt(do_b, v) - delta)
                dbias_ref[h] += ds
                ds_b = ds.astype(BF16)
                dq_ref[:, sl] = (_nn(ds_b, k) * ATT_SCALE).astype(BF16)
                dk_w = _tn(ds_b, q)
                dv_w = _tn(p.astype(BF16), do_b)
                for i in range(ATT_WIN):
                    slot = (t + 1 + i) % ATT_WIN
                    rows = slice(i * ATT_TILE, (i + 1) * ATT_TILE)
                    if i == back:
                        dk_acc[slot, :, sl] = dk_w[rows]
                        dv_acc[slot, :, sl] = dv_w[rows]
                    else:
                        dk_acc[slot, :, sl] += dk_w[rows]
                        dv_acc[slot, :, sl] += dv_w[rows]

        done = (t + 1) % ATT_WIN
        dk_ref[...] = dk_acc[done].astype(BF16)
        dv_ref[...] = dv_acc[done].astype(BF16)
        _run_hook(hook, "after", h_in, h_out, h_sems)

    last = nt - 1
    tile = lambda col, n=512: pl.BlockSpec((ATT_TILE, n), lambda t: (jnp.minimum(t, last), col))
    late = pl.BlockSpec((ATT_TILE, 512), lambda t: (jnp.maximum(t - back, 0), 0))
    return _hooked_call(
        body, hook, n_in=n_in, n_out=4, grid=(nt + back,), name="attn_bwd",
        in_specs=[tile(0)] + _att_window_specs(nt, 1) + _att_window_specs(nt, 2)
        + [tile(0), tile(0), tile(0, ATT_HEADS), _full(bias.shape)],
        out_specs=[tile(0), late, late, _acc(bias.shape)],
        out_shape=[jax.ShapeDtypeStruct((S, 512), BF16)] * 3 + [jax.ShapeDtypeStruct(bias.shape, F32)],
        scratch_shapes=[pltpu.VMEM((ATT_WIN, ATT_TILE, 512), F32)] * 2,
        compiler_params=_params("arbitrary"),
    )(z, *([z] * (2 * ATT_WIN)), y, dy, lse, bias, *hook.ins)


def sgu_bwd(z, dy, amean, lng, lnb, wcat, wtcat, bfull, maskcat, *, tm):
    S = z.shape[0]
    n = S // tm

    def body(zu_ref, zv_ref, dy_ref, a_ref, lng_ref, lnb_ref, wcat_ref, wtcat_ref, bfull_ref, mask_ref,
             duv_ref, dw_ref, dsgb_ref, dlng_ref, dlnb_ref, vbd_ref, dbd_ref, w_acc, b_acc, g_acc, s_acc):
        i = pl.program_id(0)

        @pl.when(i == 0)
        def _():
            for r in (w_acc, b_acc, g_acc, s_acc):
                r[...] = jnp.zeros_like(r)

        def block(b, carry):
            rows = _sgu_rows(b)
            zu, zv, u, xhat, rstd, sv = _sgu_block(zu_ref, zv_ref, rows, a_ref, lng_ref, lnb_ref, wcat_ref, bfull_ref, vbd_ref)
            dyv = dy_ref[rows, :].astype(F32)
            duv_ref[rows, 0:512] = (dyv * sv * _gelu_grad(zu)).astype(BF16)
            dsv = dyv * u
            b_acc[...] += dsv
            w_acc[...] += _nt(dsv.astype(BF16), vbd_ref[...])
            _block_diag(dsv, dbd_ref)
            dvn = _nn(wtcat_ref[...], dbd_ref[...])
            g_acc[...] += dvn * xhat
            s_acc[...] += dvn
            dxh = dvn * lng_ref[...]
            dv = rstd * (dxh - _group_mean(dxh, a_ref) - xhat * _group_mean(dxh * xhat, a_ref))
            duv_ref[rows, 512:1024] = (dv * _gelu_grad(zv)).astype(BF16)
            return carry

        lax.fori_loop(0, tm // SG_BLOCK, block, 0)

        @pl.when(i == n - 1)
        def _():
            dw_ref[...] = w_acc[...] * mask_ref[...]
            dsgb_ref[...] = _group_mean(b_acc[...], a_ref) * float(SG_GROUP_DIM)
            dlng_ref[...] = jnp.sum(g_acc[...], axis=0, keepdims=True)
            dlnb_ref[...] = jnp.sum(s_acc[...], axis=0, keepdims=True)

    tile = lambda col: pl.BlockSpec((tm, 512), lambda i: (i, col))
    smalls = [amean, lng, lnb, wcat, wtcat, bfull, maskcat]
    wide = SG_GROUPS * SG_BLOCK
    return pl.pallas_call(
        body, grid=(n,), name="sgu_bwd",
        in_specs=[tile(3), tile(4), tile(0)] + [_full(a.shape) for a in smalls],
        out_specs=[pl.BlockSpec((tm, 1024), lambda i: (i, 0)), _acc((SG_BLOCK, wide)), _acc((SG_BLOCK, 512)),
                   _acc((1, 512)), _acc((1, 512))],
        out_shape=[jax.ShapeDtypeStruct((S, 1024), BF16), jax.ShapeDtypeStruct((SG_BLOCK, wide), F32),
                   jax.ShapeDtypeStruct((SG_BLOCK, 512), F32), jax.ShapeDtypeStruct((1, 512), F32),
                   jax.ShapeDtypeStruct((1, 512), F32)],
        scratch_shapes=[pltpu.VMEM((wide, 512), BF16), pltpu.VMEM((wide, 512), BF16), pltpu.VMEM((SG_BLOCK, wide), F32),
                        pltpu.VMEM((SG_BLOCK, 512), F32), pltpu.VMEM((SG_BLOCK, 512), F32), pltpu.VMEM((SG_BLOCK, 512), F32)],
        compiler_params=_params("arbitrary"),
    )(z, z, dy, *smalls)


def bias_colsum(p):
    H, _, L = p.shape
    near_lo = (ATT_WIN - 1) * ATT_TILE - REL_CLIP + 1
    near_hi = ATT_WIN * ATT_TILE

    def body(p_ref, gw_ref, far_ref):
        k = lax.broadcasted_iota(jnp.int32, (1, L), 1)
        is_far = (k < near_lo) | (k >= near_hi)
        for h in range(H):
            g = jnp.sum(p_ref[h], axis=0, keepdims=True)
            gw_ref[h:h + 1, :] = g
            far_ref[h:h + 1, :] = jnp.zeros((1, 128), F32) + jnp.sum(jnp.where(is_far, g, 0.0))

    return pl.pallas_call(
        body, name="bias_colsum", in_specs=[_full(p.shape)], out_specs=[_acc((H, L)), _acc((H, 128))],
        out_shape=[jax.ShapeDtypeStruct((H, L), F32), jax.ShapeDtypeStruct((H, 128), F32)], grid=(1,),
        compiler_params=_params("arbitrary"),
    )(p)


def _row_tile(rows, cols):
    best = None
    for tr in range(16, rows + 1, 16):
        if rows % tr == 0 and tr * cols * 4 <= EW_BLOCK_BYTES:
            best = tr
    return best if best is not None else rows


def place_shard(w, own, name):
    R, C = w.shape
    r = R // 2
    tr = _row_tile(r, C)
    nr = r // tr

    def body(own_ref, w_ref, o_ref):
        o_ref[...] = w_ref[...].astype(BF16)

    return pl.pallas_call(
        body, name=name,
        grid_spec=pltpu.PrefetchScalarGridSpec(
            num_scalar_prefetch=1, grid=(2, nr),
            in_specs=[pl.BlockSpec((tr, C), lambda h, i, own: (h * nr + i, 0))],
            out_specs=pl.BlockSpec((None, None, tr, C), lambda h, i, own: (own[0], h, i, 0))),
        out_shape=jax.ShapeDtypeStruct((4, 2, r, C), BF16), compiler_params=_params("parallel", "parallel"),
    )(own, w)


def chip_partial(grad, recv, core, name):
    _, _, r, C = grad.shape
    tr = _row_tile(r, C)

    def body(core_ref, g_ref, r_ref, o_ref):
        o_ref[...] = (g_ref[...].astype(F32) + r_ref[...].astype(F32)).astype(BF16)

    spec = pl.BlockSpec((None, tr, C), lambda j, i, core: (j, i, 0))
    return pl.pallas_call(
        body, name=name,
        grid_spec=pltpu.PrefetchScalarGridSpec(
            num_scalar_prefetch=1, grid=(4, r // tr),
            in_specs=[pl.BlockSpec((None, None, tr, C), lambda j, i, core: (j, core[0], i, 0)), spec], out_specs=spec),
        out_shape=jax.ShapeDtypeStruct((4, r, C), BF16), compiler_params=_params("parallel", "parallel"),
    )(core, grad, recv)


def shard_sum(part, recv, own, core, name):
    _, r, C = part.shape
    tr = _row_tile(r, C)

    def body(own_ref, core_ref, p_ref, r0, r1, r2, o_ref):
        o_ref[...] = p_ref[...].astype(F32) + r0[...].astype(F32) + r1[...].astype(F32) + r2[...].astype(F32)

    return pl.pallas_call(
        body, name=name,
        grid_spec=pltpu.PrefetchScalarGridSpec(
            num_scalar_prefetch=2, grid=(r // tr,),
            in_specs=[pl.BlockSpec((None, tr, C), lambda i, own, core: (own[0], i, 0))]
            + [pl.BlockSpec((None, tr, C), lambda i, own, core, k=k: (k, i, 0)) for k in range(3)],
            out_specs=pl.BlockSpec((None, tr, C), lambda i, own, core: (core[0], i, 0))),
        out_shape=jax.ShapeDtypeStruct((2, r, C), F32), compiler_params=_params("parallel"),
    )(own, core, part, recv, recv, recv)


def adamw(w, g, m, v, name):
    R, C = w.shape
    tr = _row_tile(R, C)

    def body(w_ref, g_ref, m_ref, v_ref, d_ref, nm_ref, nv_ref):
        gv = g_ref[...]
        nm = ADAM_B1 * m_ref[...] + (1.0 - ADAM_B1) * gv
        nv = ADAM_B2 * v_ref[...] + (1.0 - ADAM_B2) * (gv * gv)
        m_hat = nm / (1.0 - ADAM_B1 ** ADAM_STEP)
        v_hat = nv / (1.0 - ADAM_B2 ** ADAM_STEP)
        d_ref[...] = -ADAM_LR * (m_hat / (jnp.sqrt(v_hat) + ADAM_EPS) + ADAM_WD * w_ref[...])
        nm_ref[...] = nm
        nv_ref[...] = nv

    spec = pl.BlockSpec((tr, C), lambda i: (i, 0))
    return pl.pallas_call(
        body, grid=(R // tr,), name=name, in_specs=[spec] * 4, out_specs=[spec] * 3,
        out_shape=[jax.ShapeDtypeStruct((R, C), F32)] * 3, compiler_params=_params("parallel"),
    )(w, g, m, v)


HBM = pl.BlockSpec(memory_space=pl.ANY)


def _place():
    x, y, c = lax.axis_index("x"), lax.axis_index("y"), lax.axis_index("c")
    chips = [(1 - x, y), (x, 1 - y), (1 - x, 1 - y)]
    return x, y, c, chips


def _gather_phases(n):
    def copies(buf, sems, kind):
        send_sems, recv_sems = sems
        x, y, c, chips = _place()

        def remote(w, k, slot, to):
            return pltpu.make_async_remote_copy(src_ref=slot, dst_ref=slot, send_sem=send_sems.at[w, k],
                                                recv_sem=recv_sems.at[w, k], device_id=to, device_id_type=MESH)

        def one(w, k, px, py):
            if kind == "mine":
                return remote(w, k, buf[w].at[2 * x + y, c], (px, py, c))
            if kind == "theirs":
                return remote(w, k, buf[w].at[2 * px + py, c], (px, py, c))
            if kind == "onward":
                return remote(w, 3 + k, buf[w].at[2 * px + py, c], (x, y, 1 - c))
            return remote(w, 3 + k, buf[w].at[2 * px + py, 1 - c], (x, y, 1 - c))

        return [one(w, k, px, py) for w in range(n) for k, (px, py) in enumerate(chips)]

    def start(_, buf, sems):
        for cp in copies(buf, sems, "mine"):
            cp.start()

    def relay(_, buf, sems):
        for theirs, onward in zip(copies(buf, sems, "theirs"), copies(buf, sems, "onward")):
            theirs.wait_recv()
            onward.start()

    def finish(_, buf, sems):
        for cp in copies(buf, sems, "relayed"):
            cp.wait_recv()
        for cp in copies(buf, sems, "mine") + copies(buf, sems, "onward"):
            cp.wait_send()

    return start, relay, finish


def _gather_sems(n):
    return (pltpu.SemaphoreType.DMA((n, 6)), pltpu.SemaphoreType.DMA((n, 6)))


def gather_weights(bufs):
    n = len(bufs)
    phases = _gather_phases(n)

    def body(*refs):
        for phase in phases:
            phase(None, refs[n:2 * n], refs[2 * n:])

    return pl.pallas_call(
        body, name="gather_weights", in_specs=[HBM] * n, out_specs=[HBM] * n,
        out_shape=[jax.ShapeDtypeStruct(b.shape, b.dtype) for b in bufs],
        input_output_aliases={i: i for i in range(n)}, scratch_shapes=list(_gather_sems(n)),
    )(*bufs)


def gather_hook(bufs, n_steps):
    start, relay, finish = _gather_phases(len(bufs))
    return Hook(ins=tuple(bufs), out_shapes=tuple(jax.ShapeDtypeStruct(b.shape, b.dtype) for b in bufs), aliased=True,
                sems=_gather_sems(len(bufs)),
                steps=((0, "before", start), ((n_steps * 13) // 16, "before", relay), (n_steps - 1, "after", finish)))


def sibling_split(grads, name):
    n = len(grads)

    def body(*refs):
        src, dst = refs[:n], refs[n:2 * n]
        send_sems, recv_sems = refs[2 * n:]
        x, y, c, _ = _place()
        sends = [pltpu.make_async_remote_copy(src_ref=src[w].at[:, 1 - c], dst_ref=dst[w], send_sem=send_sems.at[w],
                                              recv_sem=recv_sems.at[w], device_id=(x, y, 1 - c), device_id_type=MESH)
                 for w in range(n)]
        for cp in sends:
            cp.start()
        for cp in sends:
            cp.wait()

    return pl.pallas_call(
        body, name=name, in_specs=[HBM] * n, out_specs=[HBM] * n,
        out_shape=[jax.ShapeDtypeStruct((4,) + g.shape[2:], g.dtype) for g in grads],
        scratch_shapes=[pltpu.SemaphoreType.DMA((n,))] * 2,
    )(*grads)


def exchange_hook(parts, n_steps):
    n = len(parts)

    def copies(src, dst, sems):
        send_sems, recv_sems = sems
        _, _, c, chips = _place()
        return [pltpu.make_async_remote_copy(
            src_ref=src[w].at[2 * px + py], dst_ref=dst[w].at[k], send_sem=send_sems.at[w, k],
            recv_sem=recv_sems.at[w, k], device_id=(px, py, c), device_id_type=MESH)
            for w in range(n) for k, (px, py) in enumerate(chips)]

    def start(src, dst, sems):
        for cp in copies(src, dst, sems):
            cp.start()

    def finish(src, dst, sems):
        for cp in copies(src, dst, sems):
            cp.wait()

    return Hook(ins=tuple(parts), out_shapes=tuple(jax.ShapeDtypeStruct((3,) + p.shape[1:], p.dtype) for p in parts),
                sems=(pltpu.SemaphoreType.DMA((n, 3)), pltpu.SemaphoreType.DMA((n, 3))),
                steps=((0, "before", start), (n_steps - 1, "after", finish)))


def sibling_join(sums):
    n = len(sums)

    def body(*refs):
        buf = refs[n:2 * n]
        send_sems, recv_sems = refs[2 * n:]
        x, y, c, _ = _place()
        sends = [pltpu.make_async_remote_copy(src_ref=buf[w].at[c], dst_ref=buf[w].at[c], send_sem=send_sems.at[w],
                                              recv_sem=recv_sems.at[w], device_id=(x, y, 1 - c), device_id_type=MESH)
                 for w in range(n)]
        for cp in sends:
            cp.start()
        for w, cp in enumerate(sends):
            cp.wait_send()
            pltpu.make_async_remote_copy(src_ref=buf[w].at[c], dst_ref=buf[w].at[1 - c], send_sem=send_sems.at[w],
                                         recv_sem=recv_sems.at[w], device_id=(x, y, 1 - c), device_id_type=MESH).wait_recv()

    return pl.pallas_call(
        body, name="sibling_join", in_specs=[HBM] * n, out_specs=[HBM] * n,
        out_shape=[jax.ShapeDtypeStruct(s.shape, s.dtype) for s in sums],
        input_output_aliases={i: i for i in range(n)},
        scratch_shapes=[pltpu.SemaphoreType.DMA((n,))] * 2,
    )(*sums)


def allreduce_small(v):
    R, C = v.shape

    def body(v_ref, out_ref, all_ref, send_sems, recv_sems):
        x, y, c, chips = _place()
        me, sibling = (x, y, c), (x, y, 1 - c)

        def slot(px, py, pc):
            return all_ref.at[4 * px + 2 * py + pc]

        def copy(k, block, to, src=None):
            return pltpu.make_async_remote_copy(src_ref=slot(*block) if src is None else src, dst_ref=slot(*block),
                                                send_sem=send_sems.at[k], recv_sem=recv_sems.at[k], device_id=to,
                                                device_id_type=MESH)

        first = [copy(0, me, sibling, src=v_ref)] + [copy(1 + j, me, (*chip, c), src=v_ref) for j, chip in enumerate(chips)]
        for cp in first:
            cp.start()
        slot(*me)[...] = v_ref[...]
        passed = [copy(4 + j, (*chip, c), sibling) for j, chip in enumerate(chips)]
        for j, chip in enumerate(chips):
            copy(1 + j, (*chip, c), me).wait_recv()
            passed[j].start()
        copy(0, sibling, me).wait_recv()
        for j, chip in enumerate(chips):
            copy(4 + j, (*chip, 1 - c), me).wait_recv()
        for cp in first + passed:
            cp.wait_send()
        acc = all_ref[0]
        for k in range(1, 8):
            acc = acc + all_ref[k]
        out_ref[...] = acc

    vmem = pl.BlockSpec(memory_space=pltpu.VMEM)
    return pl.pallas_call(
        body, name="allreduce_small", in_specs=[vmem], out_specs=vmem, out_shape=jax.ShapeDtypeStruct((R, C), F32),
        scratch_shapes=[pltpu.VMEM((8, R, C), F32), pltpu.SemaphoreType.DMA((7,)), pltpu.SemaphoreType.DMA((7,))],
        compiler_params=pltpu.CompilerParams(vmem_limit_bytes=VMEM_LIMIT_V7X),
    )(v)


SMALL_PAD = 1024


def _pack_small(arrs):
    parts = []
    for a in arrs:
        f = a.reshape(-1).astype(F32)
        parts.append(jnp.pad(f, (0, (-f.shape[0]) % SMALL_PAD)))
    return jnp.concatenate(parts).reshape(-1, 128)


def _unpack_small(packed, shapes):
    flat = packed.reshape(-1)
    outs, off = [], 0
    for s in shapes:
        size = int(np.prod(s))
        outs.append(flat[off:off + size].reshape(s))
        off += size + (-size) % SMALL_PAD
    return outs


ATT_KEYS = ATT_WIN * ATT_TILE
ATT_NEAR_LO = (ATT_WIN - 1) * ATT_TILE - REL_CLIP + 1
ATT_PERIOD = ATT_KEYS + ATT_TILE + 1


def _att_bias_table(rel_bias):
    far = rel_bias[:, 2 * REL_CLIP:]
    near = rel_bias[:, 2 * REL_CLIP - 1:0:-1]
    w = jnp.concatenate([jnp.broadcast_to(far, (ATT_HEADS, ATT_NEAR_LO)), near,
                         jnp.broadcast_to(far, (ATT_HEADS, ATT_PERIOD - ATT_KEYS))], axis=1)
    rows = jnp.tile(w, (1, ATT_TILE))[:, :ATT_TILE * (ATT_PERIOD - 1)].reshape(ATT_HEADS, ATT_TILE, ATT_PERIOD - 1)
    i = np.arange(ATT_TILE)[:, None]
    m = np.arange(ATT_KEYS)[None, :]
    qc, kc = i // CHUNK, m // CHUNK
    band = (kc >= qc) & (kc <= qc + N_PREV_CHUNKS)
    return jnp.where(band[None], rows[:, :, :ATT_KEYS], NEG_INF)


def _rel_bias_grad(dbias):
    p = jnp.pad(dbias, ((0, 0), (0, 0), (0, ATT_PERIOD - 1 - ATT_KEYS))).reshape(ATT_HEADS, -1)
    p = jnp.pad(p, ((0, 0), (0, ATT_TILE))).reshape(ATT_HEADS, ATT_TILE, ATT_PERIOD)
    gw, far = bias_colsum(p)
    return jnp.concatenate([jnp.zeros((ATT_HEADS, 1), F32), gw[:, ATT_KEYS - 1:ATT_NEAR_LO - 1:-1], far[:, :1]], axis=1)


def kernel(x, mem, norm_mix_g, w_in, rel_bias, sg_ln_g, sg_ln_b, sg_w, sg_b, w_branch_att, w_branch_sg, w_out, norm_xattn_g, norm_mem_g, w_xq, w_xkv, w_xo, norm_ffn_g, w_ffn_in, w_ffn_out, norm_final_g, loss_target, m_norm_mix_g, m_w_in, m_rel_bias, m_sg_ln_g, m_sg_ln_b, m_sg_w, m_sg_b, m_w_branch_att, m_w_branch_sg, m_w_out, m_norm_xattn_g, m_norm_mem_g, m_w_xq, m_w_xkv, m_w_xo, m_norm_ffn_g, m_w_ffn_in, m_w_ffn_out, m_norm_final_g, v_norm_mix_g, v_w_in, v_rel_bias, v_sg_ln_g, v_sg_ln_b, v_sg_w, v_sg_b, v_w_branch_att, v_w_branch_sg, v_w_out, v_norm_xattn_g, v_norm_mem_g, v_w_xq, v_w_xkv, v_w_xo, v_norm_ffn_g, v_w_ffn_in, v_w_ffn_out, v_norm_final_g):
    S, D = x.shape[1], x.shape[2]
    x2d, mem2d, tgt = x[0], mem[0], loss_target[0]

    big_names = ["w_in", "w_branch_att", "w_branch_sg", "w_out", "w_xq", "w_xkv", "w_xo", "w_ffn_in", "w_ffn_out"]
    col_sharded = [True, True, True, False, False, True, False, True, False]
    big_w = [a[0] for a in (w_in, w_branch_att, w_branch_sg, w_out, w_xq, w_xkv, w_xo, w_ffn_in, w_ffn_out)]
    big_m = [a[0] for a in (m_w_in, m_w_branch_att, m_w_branch_sg, m_w_out, m_w_xq, m_w_xkv, m_w_xo, m_w_ffn_in, m_w_ffn_out)]
    big_v = [a[0] for a in (v_w_in, v_w_branch_att, v_w_branch_sg, v_w_out, v_w_xq, v_w_xkv, v_w_xo, v_w_ffn_in, v_w_ffn_out)]

    own = (2 * lax.axis_index("x") + lax.axis_index("y")).astype(jnp.int32).reshape(1)
    core = lax.axis_index("c").astype(jnp.int32).reshape(1)
    placed = [place_shard(w, own, "place_" + nm) for w, nm in zip(big_w, big_names)]

    def whole(g4, i):
        R, C = big_w[i].shape
        return g4.reshape(4, R, C) if col_sharded[i] else g4.reshape(4 * R, C)

    W_in = whole(gather_weights(placed[:1])[0], 0)

    g_mix, g_xat, g_mem, g_ffn = norm_mix_g, norm_xattn_g, norm_mem_g, norm_ffn_g
    g_fin = norm_final_g.reshape(1, D)
    bias = _att_bias_table(rel_bias[0])
    tt = np.arange(SG_BLOCK)
    sg_mask = (tt[None, :] // CHUNK) <= (tt[:, None] // CHUNK)
    wm_f = jnp.where(sg_mask[None], sg_w[0], 0.0)
    wide = SG_GROUPS * SG_BLOCK
    wcat = jnp.transpose(wm_f, (1, 0, 2)).reshape(SG_BLOCK, wide).astype(BF16)
    wtcat = jnp.transpose(wm_f, (2, 0, 1)).reshape(SG_BLOCK, wide).astype(BF16)
    maskcat = jnp.asarray(np.tile(sg_mask, (1, SG_GROUPS)), F32)
    bfull = jnp.repeat(sg_b[0].T, SG_GROUP_DIM, axis=1)
    lng, lnb = sg_ln_g[0].reshape(1, 512), sg_ln_b[0].reshape(1, 512)
    gid = np.arange(512) // SG_GROUP_DIM
    amean = jnp.asarray((gid[:, None] == gid[None, :]) / SG_GROUP_DIM, BF16)

    h1, z = norm_mm(x2d, g_mix, W_in, tm=ROW_TILE, scale=None, name="norm_mm_in")
    y_att, lse_att, *gathered = attn_fwd(z, bias, gather_hook(placed[1:], S // ATT_TILE))
    W_ba, W_bs, W_out, W_xq, W_xkv, W_xo, W_fi, W_fo = [whole(g4, i + 1) for i, g4 in enumerate(gathered)]
    y_sg = sgu_fwd(z, amean, lng, lnb, wcat, bfull, tm=512)
    a_br, b_br, merged, x1 = merge_fwd(y_att, y_sg, z, x2d, W_ba, W_bs, W_out, tm=ROW_TILE)
    h2, xq = norm_mm(x1, g_xat, W_xq.reshape(1, D, D), tm=512, scale=XATT_HEAD_DIM ** -0.5, name="norm_mm_xq")
    mn, kv = norm_mm(mem2d, g_mem, W_xkv, tm=mem2d.shape[0], scale=None, name="norm_mm_kv")
    o_x, lse_x, x2 = xattn_fwd(xq, kv, W_xo, x1, tm=512)
    h3, gu = norm_mm(x2, g_ffn, W_fi, tm=ROW_TILE, scale=None, name="norm_mm_ffn")
    act, x3 = ffn_out_fwd(gu, W_fo, x2, tm=ROW_TILE)
    loss_vec, dx3, dg_fin = loss_head(x3, g_fin, tgt, tm=512)

    dgu = ffn_act_bwd(dx3, gu, W_fo, tm=ROW_TILE, nchunk=2)
    gW_fo = tn_mm(act, dx3, shards=1, tn=D, tk=512, name="dw_ffn_out")
    gW_fi = tn_mm(h3, dgu, shards=4, tn=W_fi.shape[2], tk=512, name="dw_ffn_in")
    dx2, dg_ffn = mm_nt_norm_bwd(dgu, W_fi, x2, g_ffn, dx3, tm=ROW_TILE, name="dx_ffn")
    dq_x, dkv = xattn_bwd(dx2, xq, o_x, lse_x, kv, W_xo, tm=512)
    gW_xo = tn_mm(o_x, dx2, shards=1, tn=D, tk=512, name="dw_xo")
    gW_xq = tn_mm(h2, dq_x, shards=1, tn=D, tk=512, name="dw_xq")
    dx1, dg_xat = mm_nt_norm_bwd(dq_x, W_xq.reshape(1, D, D), x1, g_xat, dx2, tm=512, name="dx_xq")
    gW_xkv = tn_mm(mn, dkv, shards=4, tn=2 * D, tk=mem2d.shape[0], name="dw_xkv")
    _, dg_mem = mm_nt_norm_bwd(dkv.astype(BF16), W_xkv, mem2d, g_mem, None, tm=mem2d.shape[0], name="dx_mem")
    da, db, dgab, dy_att, dy_sg = merge_bwd(dx1, W_out, a_br, b_br, z, W_ba, W_bs, tm=ROW_TILE)
    gW_out = tn_mm(merged, dx1, shards=1, tn=D, tk=512, name="dw_out")
    gW_ba = tn_mm(y_att, da, shards=4, tn=D, tk=512, name="dw_branch_att")
    gW_bs = tn_mm(y_sg, db, shards=4, tn=D, tk=512, name="dw_branch_sg")

    def canon(g, i):
        R, C = big_w[i].shape
        return g.reshape(4, 2, R // 2, C)

    def stage1(grads, idx, tag):
        gs = [canon(g, i) for g, i in zip(grads, idx)]
        recv = sibling_split(gs, "sibling_split_" + tag)
        return [chip_partial(g, r, core, "chip_partial_" + big_names[i]) for g, r, i in zip(gs, recv, idx)]

    parts_b = stage1([gW_ba, gW_bs, gW_out, gW_xq, gW_xkv, gW_xo, gW_fi, gW_fo], range(1, 9), "late")
    nt_bwd = S // ATT_TILE + ATT_WIN - 1
    dq_a, dk_a, dv_a, dbias, *from_chips_b = attn_bwd(z, y_att, dy_att, lse_att, bias, exchange_hook(parts_b, nt_bwd))
    duv, dwcat, dsgb_full, dlng, dlnb = sgu_bwd(z, dy_sg, amean, lng, lnb, wcat, wtcat, bfull, maskcat, tm=512)
    dwm = jnp.transpose(dwcat.reshape(SG_BLOCK, SG_GROUPS, SG_BLOCK), (1, 0, 2))
    dsgb = dsgb_full[:, ::SG_GROUP_DIM].T
    dz = jnp.concatenate([dq_a, dk_a, dv_a, duv, dgab], axis=1)
    gW_in = tn_mm(h1, dz, shards=4, tn=W_in.shape[2], tk=512, name="dw_in")
    parts_a = stage1([gW_in], [0], "w_in")
    dx, dg_mix, *from_chips_a = mm_nt_norm_bwd(dz, W_in, x2d, g_mix, dx1, tm=ROW_TILE, name="dx_in",
                                               hook=exchange_hook(parts_a, S // ROW_TILE))
    d_rel = _rel_bias_grad(dbias)

    parts, from_chips = parts_a + parts_b, from_chips_a + from_chips_b
    sums = [shard_sum(p, r, own, core, "shard_sum_" + nm) for p, r, nm in zip(parts, from_chips, big_names)]
    joined = sibling_join(sums)
    big_out = []
    for j, w, m_, v_, nm in zip(joined, big_w, big_m, big_v, big_names):
        g = j.reshape(w.shape)
        big_out.append((g,) + tuple(adamw(w, g, m_, v_, "adamw_" + nm)))

    small_w = [norm_mix_g, rel_bias, sg_ln_g, sg_ln_b, sg_w, sg_b, norm_xattn_g, norm_mem_g, norm_ffn_g, norm_final_g]
    small_m = [m_norm_mix_g, m_rel_bias, m_sg_ln_g, m_sg_ln_b, m_sg_w, m_sg_b, m_norm_xattn_g, m_norm_mem_g, m_norm_ffn_g, m_norm_final_g]
    small_v = [v_norm_mix_g, v_rel_bias, v_sg_ln_g, v_sg_ln_b, v_sg_w, v_sg_b, v_norm_xattn_g, v_norm_mem_g, v_norm_ffn_g, v_norm_final_g]
    small_g = [dg_mix, d_rel, dlng, dlnb, dwm, dsgb, dg_xat, dg_mem, dg_ffn, dg_fin]
    shapes = [w.shape for w in small_w]
    g_sum = allreduce_small(_pack_small(small_g + [loss_vec[0, :1]]))
    zero = jnp.zeros((1,), F32)
    d_s, m_s, v_s = adamw(_pack_small(small_w + [zero]), g_sum, _pack_small(small_m + [zero]), _pack_small(small_v + [zero]),
                          "adamw_small")
    sg_, sd, sm, sv_ = (_unpack_small(p, shapes + [(1,)]) for p in (g_sum, d_s, m_s, v_s))
    loss = sg_[-1][0]

    order = ["norm_mix_g", "w_in", "rel_bias", "sg_ln_g", "sg_ln_b", "sg_w", "sg_b", "w_branch_att", "w_branch_sg", "w_out",
             "norm_xattn_g", "norm_mem_g", "w_xq", "w_xkv", "w_xo", "norm_ffn_g", "w_ffn_in", "w_ffn_out", "norm_final_g"]
    small_names = ["norm_mix_g", "rel_bias", "sg_ln_g", "sg_ln_b", "sg_w", "sg_b", "norm_xattn_g", "norm_mem_g", "norm_ffn_g",
                   "norm_final_g"]
    res = {}
    for i, nm in enumerate(small_names):
        res[nm] = (sg_[i], sd[i], sm[i], sv_[i])
    for nm, outs in zip(big_names, big_out):
        res[nm] = tuple(o[None] for o in outs)
    return (loss, dx[None], *[res[nm][0] for nm in order], *[res[nm][1] for nm in order],
            *[res[nm][2] for nm in order], *[res[nm][3] for nm in order])
```

```python
import functools
from typing import NamedTuple

import numpy as np
import jax
import jax.numpy as jnp
from jax import lax
from jax.experimental import pallas as pl
from jax.experimental.pallas import tpu as pltpu

F32, BF16 = jnp.float32, jnp.bfloat16
MESH = pl.DeviceIdType.MESH

EPS = 1e-6
NEG_INF = -1e30
CHUNK = 64
N_PREV_CHUNKS = 8
ATT_HEADS, ATT_HEAD_DIM = 8, 64
REL_CLIP = 128
SG_BLOCK, SG_GROUPS, SG_GROUP_DIM = 128, 8, 64
XATT_HEADS, XATT_HEAD_DIM = 4, 256
ATT_TILE = 128
ATT_WIN = 5
ADAM_LR, ADAM_B1, ADAM_B2, ADAM_EPS, ADAM_WD, ADAM_STEP = 0.001, 0.9, 0.999, 1e-08, 0.01, 10

VMEM_LIMIT_V7X = 56 * 1024 * 1024
EW_BLOCK_BYTES = 1 << 20
ROW_TILE = 512
HBM = pl.BlockSpec(memory_space=pl.ANY)


def _params(*sem):
    return pltpu.CompilerParams(dimension_semantics=sem, vmem_limit_bytes=VMEM_LIMIT_V7X)


def _full(shape):
    n = len(shape)
    return pl.BlockSpec(shape, lambda *_: (0,) * n, pipeline_mode=pl.Buffered(1))


def _acc(shape):
    n = len(shape)
    return pl.BlockSpec(shape, lambda *_: (0,) * n)


class Hook(NamedTuple):
    ins: tuple = ()
    out_shapes: tuple = ()
    aliased: bool = False
    sems: tuple = ()
    steps: tuple = ()


def _run_hook(hook, pos, h_in, h_out, h_sems):
    for step, where, fn in hook.steps:
        if where == pos:
            pl.when(pl.program_id(0) == step)(functools.partial(fn, h_in, h_out, h_sems))


def _hooked_call(body, hook, *, n_in, n_out, in_specs, out_specs, out_shape, scratch_shapes=(), **kw):
    nh = len(hook.ins)
    aliases = {n_in + i: n_out + i for i in range(nh)} if hook.aliased else {}
    return pl.pallas_call(
        body, in_specs=list(in_specs) + [HBM] * nh, out_specs=list(out_specs) + [HBM] * len(hook.out_shapes),
        out_shape=list(out_shape) + list(hook.out_shapes), scratch_shapes=list(scratch_shapes) + list(hook.sems),
        input_output_aliases=aliases, **kw)


def _nt(a, b):
    return lax.dot_general(a, b, (((1,), (1,)), ((), ())), preferred_element_type=F32)


def _tn(a, b):
    return lax.dot_general(a, b, (((0,), (0,)), ((), ())), preferred_element_type=F32)


def _nn(a, b):
    return jnp.dot(a, b, preferred_element_type=F32)


def _sigmoid(x):
    return 1.0 / (1.0 + jnp.exp(-x))


_GELU_C = float(np.sqrt(2.0 / np.pi))


def _gelu(x):
    t = jnp.tanh(_GELU_C * (x + 0.044715 * (x * x * x)))
    return x * (0.5 * (1.0 + t))


def _gelu_grad(x):
    t = jnp.tanh(_GELU_C * (x + 0.044715 * (x * x * x)))
    return 0.5 * (1.0 + t) + 0.5 * x * (1.0 - t * t) * (_GELU_C * (1.0 + 3.0 * 0.044715 * x * x))


def _rms_stats(xf):
    rstd = lax.rsqrt(jnp.mean(xf * xf, axis=-1, keepdims=True) + EPS)
    return rstd, xf * rstd


def _rms_bwd(xhat, rstd, g, dh):
    dxh = dh * g
    dx = rstd * (dxh - xhat * jnp.mean(dxh * xhat, axis=-1, keepdims=True))
    return dx, dh * xhat


def norm_mm(x, g, w, *, tm, scale, name, hook=Hook()):
    S, D = x.shape
    J, _, C = w.shape
    nh, nho = len(hook.ins), len(hook.out_shapes)

    def body(x_ref, g_ref, w_ref, *rest):
        h_in, (h_ref, z_ref), h_out, h_sems = rest[:nh], rest[nh:nh + 2], rest[nh + 2:nh + 2 + nho], rest[nh + 2 + nho:]
        _run_hook(hook, "before", h_in, h_out, h_sems)
        _, xhat = _rms_stats(x_ref[...])
        h = (xhat * g_ref[...]).astype(BF16)
        h_ref[...] = h
        for j in range(J):
            acc = _nn(h, w_ref[j])
            if scale is not None:
                acc = acc * scale
            z_ref[:, j * C:(j + 1) * C] = acc.astype(BF16)
        _run_hook(hook, "after", h_in, h_out, h_sems)

    return _hooked_call(
        body, hook, n_in=3, n_out=2, grid=(S // tm,), name=name,
        in_specs=[pl.BlockSpec((tm, D), lambda i: (i, 0)), _full((1, D)), _full((J, D, C))],
        out_specs=[pl.BlockSpec((tm, D), lambda i: (i, 0)), pl.BlockSpec((tm, J * C), lambda i: (i, 0))],
        out_shape=[jax.ShapeDtypeStruct((S, D), BF16), jax.ShapeDtypeStruct((S, J * C), BF16)],
        compiler_params=_params("arbitrary"),
    )(x, g, w, *hook.ins)


def _att_window_specs(nt, col):
    return [pl.BlockSpec((ATT_TILE, 512), lambda t, j=j: (jnp.clip(t - (ATT_WIN - 1) + j, 0, nt - 1), col))
            for j in range(ATT_WIN)]


ATT_SCALE = ATT_HEAD_DIM ** -0.5


def _att_scores(q_scaled, k, bias, valid):
    return jnp.where(valid, _nt(q_scaled, k) + bias, NEG_INF)


def _att_valid(t):
    kpos = lax.broadcasted_iota(jnp.int32, (ATT_TILE, ATT_WIN * ATT_TILE), 1) + (t - (ATT_WIN - 1)) * ATT_TILE
    return kpos >= 0


def attn_fwd(z, bias, hook=Hook()):
    S = z.shape[0]
    nt = S // ATT_TILE
    n_in, nh, nho = 2 + 2 * ATT_WIN, len(hook.ins), len(hook.out_shapes)

    def body(q_ref, *refs):
        k_refs, v_refs = refs[:ATT_WIN], refs[ATT_WIN:2 * ATT_WIN]
        bias_ref = refs[2 * ATT_WIN]
        rest = refs[2 * ATT_WIN + 1:]
        h_in, (y_ref, lse_ref), h_out = rest[:nh], rest[nh:nh + 2], rest[nh + 2:nh + 2 + nho]
        s_scr, p_scr = rest[nh + 2 + nho:nh + 4 + nho]
        h_sems = rest[nh + 4 + nho:]
        _run_hook(hook, "before", h_in, h_out, h_sems)
        valid = _att_valid(pl.program_id(0))
        heads = [slice(h * ATT_HEAD_DIM, (h + 1) * ATT_HEAD_DIM) for h in range(ATT_HEADS)]
        for h, sl in enumerate(heads):
            k = jnp.concatenate([r[:, sl] for r in k_refs], axis=0)
            s_scr[h] = _att_scores(q_ref[:, sl] * ATT_SCALE, k, bias_ref[h], valid)
        stats = []
        for h in range(ATT_HEADS):
            s = s_scr[h]
            m = jnp.max(s, axis=-1, keepdims=True)
            p = jnp.exp(s - m)
            stats.append((m, jnp.sum(p, axis=-1, keepdims=True)))
            p_scr[h] = p.astype(BF16)
        for h, sl in enumerate(heads):
            m, l = stats[h]
            v = jnp.concatenate([r[:, sl] for r in v_refs], axis=0)
            y_ref[:, sl] = (_nn(p_scr[h], v) / l).astype(BF16)
            lse_ref[:, h:h + 1] = m + jnp.log(l)
        _run_hook(hook, "after", h_in, h_out, h_sems)

    tile = lambda col: pl.BlockSpec((ATT_TILE, 512), lambda t: (t, col))
    return _hooked_call(
        body, hook, n_in=n_in, n_out=2, grid=(nt,), name="attn_fwd",
        in_specs=[tile(0)] + _att_window_specs(nt, 1) + _att_window_specs(nt, 2) + [_full(bias.shape)],
        out_specs=[tile(0), pl.BlockSpec((ATT_TILE, ATT_HEADS), lambda t: (t, 0))],
        out_shape=[jax.ShapeDtypeStruct((S, 512), BF16), jax.ShapeDtypeStruct((S, ATT_HEADS), F32)],
        scratch_shapes=[pltpu.VMEM((ATT_HEADS, ATT_TILE, ATT_WIN * ATT_TILE), F32),
                        pltpu.VMEM((ATT_HEADS, ATT_TILE, ATT_WIN * ATT_TILE), BF16)],
        compiler_params=_params("arbitrary"),
    )(z, *([z] * (2 * ATT_WIN)), bias, *hook.ins)


def _group_mean(x, a_ref):
    hi = x.astype(BF16)
    lo = (x - hi.astype(F32)).astype(BF16)
    return _nn(hi, a_ref[...]) + _nn(lo, a_ref[...])


def _block_diag(x, ref):
    group = lax.broadcasted_iota(jnp.int32, x.shape, 1) // SG_GROUP_DIM
    for g in range(SG_GROUPS):
        ref[g * SG_BLOCK:(g + 1) * SG_BLOCK, :] = jnp.where(group == g, x, 0.0).astype(BF16)


def _sgu_block(zu_ref, zv_ref, rows, a_ref, lng_ref, lnb_ref, wcat_ref, bfull_ref, vbd_ref):
    zu = zu_ref[rows, :].astype(F32)
    zv = zv_ref[rows, :].astype(F32)
    u, v = _gelu(zu), _gelu(zv)
    vc = v - _group_mean(v, a_ref)
    rstd = lax.rsqrt(_group_mean(vc * vc, a_ref) + EPS)
    xhat = vc * rstd
    _block_diag(xhat * lng_ref[...] + lnb_ref[...], vbd_ref)
    sv = _nn(wcat_ref[...], vbd_ref[...]) + bfull_ref[...]
    return zu, zv, u, xhat, rstd, sv


def _sgu_rows(b):
    return pl.ds(pl.multiple_of(b * SG_BLOCK, SG_BLOCK), SG_BLOCK)


def sgu_fwd(z, amean, lng, lnb, wcat, bfull, *, tm):
    S = z.shape[0]

    def body(zu_ref, zv_ref, a_ref, lng_ref, lnb_ref, wcat_ref, bfull_ref, y_ref, vbd_ref):
        def block(b, carry):
            rows = _sgu_rows(b)
            _, _, u, _, _, sv = _sgu_block(zu_ref, zv_ref, rows, a_ref, lng_ref, lnb_ref, wcat_ref, bfull_ref, vbd_ref)
            y_ref[rows, :] = (u * sv).astype(BF16)
            return carry

        lax.fori_loop(0, tm // SG_BLOCK, block, 0)

    tile = lambda col: pl.BlockSpec((tm, 512), lambda i: (i, col))
    smalls = [amean, lng, lnb, wcat, bfull]
    return pl.pallas_call(
        body, grid=(S // tm,), name="sgu_fwd",
        in_specs=[tile(3), tile(4)] + [_full(a.shape) for a in smalls],
        out_specs=tile(0), out_shape=jax.ShapeDtypeStruct((S, 512), BF16),
        scratch_shapes=[pltpu.VMEM((SG_GROUPS * SG_BLOCK, 512), BF16)],
        compiler_params=_params("parallel"),
    )(z, z, *smalls)


def _gate_specs(tm):
    return [pl.BlockSpec((tm, 512), lambda i, c=c: (i, c)) for c in (5, 6, 7, 8)]


def merge_fwd(y_att, y_sg, z, x, w_ba, w_bs, w_out, *, tm):
    S, D = x.shape
    J, _, C = w_ba.shape

    def body(ya_ref, ys_ref, g0, g1, g2, g3, x_ref, wba_ref, wbs_ref, wo_ref, a_ref, b_ref, m_ref, x1_ref):
        ya, ys = ya_ref[...], ys_ref[...]
        a = jnp.concatenate([_nn(ya, wba_ref[j]) for j in range(J)], axis=1)
        b = jnp.concatenate([_nn(ys, wbs_ref[j]) for j in range(J)], axis=1)
        ga = jnp.concatenate([g0[...], g1[...]], axis=1).astype(F32)
        gb = jnp.concatenate([g2[...], g3[...]], axis=1).astype(F32)
        a_ref[...] = a.astype(BF16)
        b_ref[...] = b.astype(BF16)
        merged = (_sigmoid(ga) * a + _sigmoid(gb) * b).astype(BF16)
        m_ref[...] = merged
        x1_ref[...] = x_ref[...] + _nn(merged, wo_ref[...])

    row = lambda n: pl.BlockSpec((tm, n), lambda i: (i, 0))
    return pl.pallas_call(
        body, grid=(S // tm,), name="merge_fwd",
        in_specs=[row(512), row(512)] + _gate_specs(tm) + [row(D), _full(w_ba.shape), _full(w_bs.shape), _full(w_out.shape)],
        out_specs=[row(D)] * 4,
        out_shape=[jax.ShapeDtypeStruct((S, D), BF16)] * 3 + [jax.ShapeDtypeStruct((S, D), F32)],
        compiler_params=_params("parallel"),
    )(y_att, y_sg, z, z, z, z, x, w_ba, w_bs, w_out)


def xattn_fwd(xq, kv, w_xo, x1, *, tm):
    S, D = xq.shape
    dh = XATT_HEAD_DIM

    def body(q_ref, kv_ref, wo_ref, x1_ref, o_ref, lse_ref, x2_ref):
        outs = []
        for h in range(XATT_HEADS):
            s = _nt(q_ref[:, h * dh:(h + 1) * dh], kv_ref[:, h * dh:(h + 1) * dh])
            m = jnp.max(s, axis=-1, keepdims=True)
            p = jnp.exp(s - m)
            l = jnp.sum(p, axis=-1, keepdims=True)
            outs.append((_nn(p.astype(BF16), kv_ref[:, D + h * dh:D + (h + 1) * dh]) / l).astype(BF16))
            lse_ref[:, h:h + 1] = m + jnp.log(l)
        o = jnp.concatenate(outs, axis=1)
        o_ref[...] = o
        x2_ref[...] = x1_ref[...] + _nn(o, wo_ref[...])

    row = lambda n: pl.BlockSpec((tm, n), lambda i: (i, 0))
    return pl.pallas_call(
        body, grid=(S // tm,), name="xattn_fwd",
        in_specs=[row(D), _full(kv.shape), _full(w_xo.shape), row(D)],
        out_specs=[row(D), row(XATT_HEADS), row(D)],
        out_shape=[jax.ShapeDtypeStruct((S, D), BF16), jax.ShapeDtypeStruct((S, XATT_HEADS), F32),
                   jax.ShapeDtypeStruct((S, D), F32)],
        compiler_params=_params("parallel"),
    )(xq, kv, w_xo, x1)


def ffn_out_fwd(gu, w, x2, *, tm):
    S, D = x2.shape
    F = w.shape[0]

    def body(gu_ref, w_ref, x2_ref, act_ref, x3_ref):
        gate = gu_ref[:, :F].astype(F32)
        up = gu_ref[:, F:].astype(F32)
        act = (gate * _sigmoid(gate) * up).astype(BF16)
        act_ref[...] = act
        x3_ref[...] = x2_ref[...] + _nn(act, w_ref[...])

    row = lambda n: pl.BlockSpec((tm, n), lambda i: (i, 0))
    return pl.pallas_call(
        body, grid=(S // tm,), name="ffn_out_fwd",
        in_specs=[row(2 * F), _full(w.shape), row(D)], out_specs=[row(F), row(D)],
        out_shape=[jax.ShapeDtypeStruct((S, F), BF16), jax.ShapeDtypeStruct((S, D), F32)],
        compiler_params=_params("parallel"),
    )(gu, w, x2)


def loss_head(x3, g, target, *, tm):
    S, D = x3.shape

    def body(x_ref, g_ref, t_ref, loss_ref, dx_ref, dg_ref):
        @pl.when(pl.program_id(0) == 0)
        def _():
            loss_ref[...] = jnp.zeros_like(loss_ref)
            dg_ref[...] = jnp.zeros_like(dg_ref)

        gv = g_ref[...]
        rstd, xhat = _rms_stats(x_ref[...])
        err = xhat * gv - t_ref[...]
        loss_ref[...] += 0.5 * jnp.sum(jnp.mean(err * err, axis=-1, keepdims=True))
        dx, dgc = _rms_bwd(xhat, rstd, gv, err * (1.0 / D))
        dx_ref[...] = dx
        dg_ref[...] += jnp.sum(dgc, axis=0, keepdims=True)

    row = pl.BlockSpec((tm, D), lambda i: (i, 0))
    return pl.pallas_call(
        body, grid=(S // tm,), name="loss_head",
        in_specs=[row, _full((1, D)), row], out_specs=[_acc((1, 128)), row, _acc((1, D))],
        out_shape=[jax.ShapeDtypeStruct((1, 128), F32), jax.ShapeDtypeStruct((S, D), F32),
                   jax.ShapeDtypeStruct((1, D), F32)],
        compiler_params=_params("arbitrary"),
    )(x3, g, target)


def ffn_act_bwd(dx3, gu, w, *, tm, nchunk):
    S, D = dx3.shape
    F = w.shape[0]
    cn = F // nchunk

    def body(dx_ref, gu_ref, w_ref, dgu_ref):
        dxb = dx_ref[...].astype(BF16)
        for j in range(nchunk):
            dact = _nt(dxb, w_ref[j * cn:(j + 1) * cn, :])
            gate = gu_ref[:, j * cn:(j + 1) * cn].astype(F32)
            up = gu_ref[:, F + j * cn:F + (j + 1) * cn].astype(F32)
            sg = _sigmoid(gate)
            dgu_ref[:, j * cn:(j + 1) * cn] = (dact * up * (sg * (1.0 + gate * (1.0 - sg)))).astype(BF16)
            dgu_ref[:, F + j * cn:F + (j + 1) * cn] = (dact * (gate * sg)).astype(BF16)

    row = lambda n: pl.BlockSpec((tm, n), lambda i: (i, 0))
    return pl.pallas_call(
        body, grid=(S // tm,), name="ffn_act_bwd",
        in_specs=[row(D), row(2 * F), _full(w.shape)], out_specs=row(2 * F),
        out_shape=jax.ShapeDtypeStruct((S, 2 * F), BF16), compiler_params=_params("parallel"),
    )(dx3, gu, w)


def tn_mm(a, b, *, shards, tn, tk, name):
    K, M = a.shape
    N = b.shape[1]
    C = N // shards
    per = tn // C
    nk = K // tk

    def body(a_ref, b_ref, o_ref, acc_ref):
        k = pl.program_id(1)

        @pl.when(k == 0)
        def _():
            acc_ref[...] = jnp.zeros_like(acc_ref)

        acc_ref[...] += _tn(a_ref[...].astype(BF16), b_ref[...].astype(BF16))

        @pl.when(k == nk - 1)
        def _():
            if shards > 1:
                for s in range(per):
                    o_ref[s] = acc_ref[:, s * C:(s + 1) * C].astype(BF16)
            else:
                o_ref[...] = acc_ref[...].astype(BF16)

    if shards > 1:
        out_spec = pl.BlockSpec((per, M, C), lambda n, k: (n, 0, 0))
        out_shape = jax.ShapeDtypeStruct((shards, M, C), BF16)
    else:
        out_spec = pl.BlockSpec((M, tn), lambda n, k: (0, n))
        out_shape = jax.ShapeDtypeStruct((M, N), BF16)
    return pl.pallas_call(
        body, grid=(N // tn, nk), name=name,
        in_specs=[pl.BlockSpec((tk, M), lambda n, k: (k, 0)), pl.BlockSpec((tk, tn), lambda n, k: (k, n))],
        out_specs=out_spec, out_shape=out_shape, scratch_shapes=[pltpu.VMEM((M, tn), F32)],
        compiler_params=_params("parallel", "arbitrary"),
    )(a, b)


def mm_nt_norm_bwd(dy, w, x, g, dx_in, *, tm, name, hook=Hook()):
    S, D = x.shape
    J, _, C = w.shape
    has_in = dx_in is not None
    n_in, nh, nho = 4 + has_in, len(hook.ins), len(hook.out_shapes)

    def body(*refs):
        dy_ref, w_ref, x_ref, g_ref = refs[:4]
        dxin_ref = refs[4] if has_in else None
        rest = refs[n_in:]
        h_in, (dx_ref, dg_ref), h_out, h_sems = rest[:nh], rest[nh:nh + 2], rest[nh + 2:nh + 2 + nho], rest[nh + 2 + nho:]
        _run_hook(hook, "before", h_in, h_out, h_sems)

        @pl.when(pl.program_id(0) == 0)
        def _():
            dg_ref[...] = jnp.zeros_like(dg_ref)

        dh = _nt(dy_ref[:, 0:C], w_ref[0])
        for j in range(1, J):
            dh += _nt(dy_ref[:, j * C:(j + 1) * C], w_ref[j])
        rstd, xhat = _rms_stats(x_ref[...])
        dx, dgc = _rms_bwd(xhat, rstd, g_ref[...], dh)
        dx_ref[...] = dx + dxin_ref[...] if has_in else dx
        dg_ref[...] += jnp.sum(dgc, axis=0, keepdims=True)
        _run_hook(hook, "after", h_in, h_out, h_sems)

    row = lambda n: pl.BlockSpec((tm, n), lambda i: (i, 0))
    ins = [dy, w, x, g] + ([dx_in] if has_in else [])
    return _hooked_call(
        body, hook, n_in=n_in, n_out=2, grid=(S // tm,), name=name,
        in_specs=[row(J * C), _full(w.shape), row(D), _full((1, D))] + ([row(D)] if has_in else []),
        out_specs=[row(D), _acc((1, D))],
        out_shape=[jax.ShapeDtypeStruct((S, D), F32), jax.ShapeDtypeStruct((1, D), F32)],
        compiler_params=_params("arbitrary"),
    )(*ins, *hook.ins)


def xattn_bwd(dx2, xq, o, lse, kv, w_xo, *, tm):
    S, D = xq.shape
    M = kv.shape[0]
    dh = XATT_HEAD_DIM

    def body(dx_ref, q_ref, o_ref, lse_ref, kv_ref, wo_ref, dq_ref, dkv_ref):
        @pl.when(pl.program_id(0) == 0)
        def _():
            dkv_ref[...] = jnp.zeros_like(dkv_ref)

        do = _nt(dx_ref[...].astype(BF16), wo_ref[...])
        for h in range(XATT_HEADS):
            hs = slice(h * dh, (h + 1) * dh)
            vs = slice(D + h * dh, D + (h + 1) * dh)
            q, k, v = q_ref[:, hs], kv_ref[:, hs], kv_ref[:, vs]
            do_h = do[:, hs]
            do_b = do_h.astype(BF16)
            p = jnp.exp(_nt(q, k) - lse_ref[:, h:h + 1])
            delta = jnp.sum(do_h * o_ref[:, hs].astype(F32), axis=-1, keepdims=True)
            ds = (p * (_nt(do_b, v) - delta)).astype(BF16)
            dq_ref[:, hs] = (_nn(ds, k) * (dh ** -0.5)).astype(BF16)
            dkv_ref[:, hs] += _tn(ds, q)
            dkv_ref[:, vs] += _tn(p.astype(BF16), do_b)

    row = lambda n: pl.BlockSpec((tm, n), lambda i: (i, 0))
    return pl.pallas_call(
        body, grid=(S // tm,), name="xattn_bwd",
        in_specs=[row(D), row(D), row(D), row(XATT_HEADS), _full(kv.shape), _full(w_xo.shape)],
        out_specs=[row(D), _acc((M, 2 * D))],
        out_shape=[jax.ShapeDtypeStruct((S, D), BF16), jax.ShapeDtypeStruct((M, 2 * D), F32)],
        compiler_params=_params("arbitrary"),
    )(dx2, xq, o, lse, kv, w_xo)


def merge_bwd(dx1, w_out, a, b, z, w_ba, w_bs, *, tm):
    S, D = dx1.shape
    J, W, C = w_ba.shape

    def body(dx_ref, wo_ref, a_ref, b_ref, g0, g1, g2, g3, wba_ref, wbs_ref, da_ref, db_ref, dg_ref, dya_ref, dys_ref):
        dm = _nt(dx_ref[...].astype(BF16), wo_ref[...])
        sa = _sigmoid(jnp.concatenate([g0[...], g1[...]], axis=1).astype(F32))
        sb = _sigmoid(jnp.concatenate([g2[...], g3[...]], axis=1).astype(F32))
        dg_ref[:, :D] = (dm * a_ref[...].astype(F32) * (sa * (1.0 - sa))).astype(BF16)
        dg_ref[:, D:] = (dm * b_ref[...].astype(F32) * (sb * (1.0 - sb))).astype(BF16)
        da = (dm * sa).astype(BF16)
        db = (dm * sb).astype(BF16)
        da_ref[...] = da
        db_ref[...] = db
        dya = _nt(da[:, 0:C], wba_ref[0])
        dys = _nt(db[:, 0:C], wbs_ref[0])
        for j in range(1, J):
            dya += _nt(da[:, j * C:(j + 1) * C], wba_ref[j])
            dys += _nt(db[:, j * C:(j + 1) * C], wbs_ref[j])
        dya_ref[...] = dya.astype(BF16)
        dys_ref[...] = dys.astype(BF16)

    row = lambda n: pl.BlockSpec((tm, n), lambda i: (i, 0))
    return pl.pallas_call(
        body, grid=(S // tm,), name="merge_bwd",
        in_specs=[row(D), _full(w_out.shape), row(D), row(D)] + _gate_specs(tm) + [_full(w_ba.shape), _full(w_bs.shape)],
        out_specs=[row(D), row(D), row(2 * D), row(W), row(W)],
        out_shape=[jax.ShapeDtypeStruct((S, D), BF16)] * 2 + [jax.ShapeDtypeStruct((S, 2 * D), BF16)]
        + [jax.ShapeDtypeStruct((S, W), BF16)] * 2,
        compiler_params=_params("parallel"),
    )(dx1, w_out, a, b, z, z, z, z, w_ba, w_bs)


def attn_bwd(z, y, dy, lse, bias, hook=Hook()):
    S = z.shape[0]
    nt = S // ATT_TILE
    back = ATT_WIN - 1
    n_in, nh, nho = 5 + 2 * ATT_WIN, len(hook.ins), len(hook.out_shapes)

    def body(q_ref, *refs):
        k_refs, v_refs = refs[:ATT_WIN], refs[ATT_WIN:2 * ATT_WIN]
        y_ref, dy_ref, lse_ref, bias_ref = refs[2 * ATT_WIN:2 * ATT_WIN + 4]
        rest = refs[2 * ATT_WIN + 4:]
        h_in, (dq_ref, dk_ref, dv_ref, dbias_ref), h_out = rest[:nh], rest[nh:nh + 4], rest[nh + 4:nh + 4 + nho]
        dk_acc, dv_acc = rest[nh + 4 + nho:nh + 6 + nho]
        h_sems = rest[nh + 6 + nho:]
        t = pl.program_id(0)
        _run_hook(hook, "before", h_in, h_out, h_sems)

        @pl.when(t == 0)
        def _():
            dbias_ref[...] = jnp.zeros_like(dbias_ref)
            dk_acc[...] = jnp.zeros_like(dk_acc)
            dv_acc[...] = jnp.zeros_like(dv_acc)

        @pl.when(t < nt)
        def _():
            valid = _att_valid(t)
            for h in range(ATT_HEADS):
                sl = slice(h * ATT_HEAD_DIM, (h + 1) * ATT_HEAD_DIM)
                k = jnp.concatenate([r[:, sl] for r in k_refs], axis=0)
                v = jnp.concatenate([r[:, sl] for r in v_refs], axis=0)
                q, do_b = q_ref[:, sl] * ATT_SCALE, dy_ref[:, sl]
                p = jnp.exp(_att_scores(q, k, bias_ref[h], valid) - lse_ref[:, h:h + 1])
                delta = jnp.sum(do_b.astype(F32) * y_ref[:, sl].astype(F32), axis=-1, keepdims=True)
                ds = p * (_nt(do_b, v) - delta)
                dbias_ref[h] += ds
                ds_b = ds.astype(BF16)
                dq_ref[:, sl] = (_nn(ds_b, k) * ATT_SCALE).astype(BF16)
                dk_w = _tn(ds_b, q)
                dv_w = _tn(p.astype(BF16), do_b)
                for i in range(ATT_WIN):
                    slot = (t + 1 + i) % ATT_WIN
                    rows = slice(i * ATT_TILE, (i + 1) * ATT_TILE)
                    if i == back:
                        dk_acc[slot, :, sl] = dk_w[rows]
                        dv_acc[slot, :, sl] = dv_w[rows]
                    else:
                        dk_acc[slot, :, sl] += dk_w[rows]
                        dv_acc[slot, :, sl] += dv_w[rows]

        done = (t + 1) % ATT_WIN
        dk_ref[...] = dk_acc[done].astype(BF16)
        dv_ref[...] = dv_acc[done].astype(BF16)
        _run_hook(hook, "after", h_in, h_out, h_sems)

    last = nt - 1
    tile = lambda col, n=512: pl.BlockSpec((ATT_TILE, n), lambda t: (jnp.minimum(t, last), col))
    late = pl.BlockSpec((ATT_TILE, 512), lambda t: (jnp.maximum(t - back, 0), 0))
    return _hooked_call(
        body, hook, n_in=n_in, n_out=4, grid=(nt + back,), name="attn_bwd",
        in_specs=[tile(0)] + _att_window_specs(nt, 1) + _att_window_specs(nt, 2)
        + [tile(0), tile(0), tile(0, ATT_HEADS), _full(bias.shape)],
        out_specs=[tile(0), late, late, _acc(bias.shape)],
        out_shape=[jax.ShapeDtypeStruct((S, 512), BF16)] * 3 + [jax.ShapeDtypeStruct(bias.shape, F32)],
        scratch_shapes=[pltpu.VMEM((ATT_WIN, ATT_TILE, 512), F32)] * 2,
        compiler_params=_params("arbitrary"),
    )(z, *([z] * (2 * ATT_WIN)), y, dy, lse, bias, *hook.ins)


def sgu_bwd(z, dy, amean, lng, lnb, wcat, wtcat, bfull, maskcat, *, tm):
    S = z.shape[0]
    n = S // tm

    def body(zu_ref, zv_ref, dy_ref, a_ref, lng_ref, lnb_ref, wcat_ref, wtcat_ref, bfull_ref, mask_ref,
             duv_ref, dw_ref, dsgb_ref, dlng_ref, dlnb_ref, vbd_ref, dbd_ref, w_acc, b_acc, g_acc, s_acc):
        i = pl.program_id(0)

        @pl.when(i == 0)
        def _():
            for r in (w_acc, b_acc, g_acc, s_acc):
                r[...] = jnp.zeros_like(r)

        def block(b, carry):
            rows = _sgu_rows(b)
            zu, zv, u, xhat, rstd, sv = _sgu_block(zu_ref, zv_ref, rows, a_ref, lng_ref, lnb_ref, wcat_ref, bfull_ref, vbd_ref)
            dyv = dy_ref[rows, :].astype(F32)
            duv_ref[rows, 0:512] = (dyv * sv * _gelu_grad(zu)).astype(BF16)
            dsv = dyv * u
            b_acc[...] += dsv
            w_acc[...] += _nt(dsv.astype(BF16), vbd_ref[...])
            _block_diag(dsv, dbd_ref)
            dvn = _nn(wtcat_ref[...], dbd_ref[...])
            g_acc[...] += dvn * xhat
            s_acc[...] += dvn
            dxh = dvn * lng_ref[...]
            dv = rstd * (dxh - _group_mean(dxh, a_ref) - xhat * _group_mean(dxh * xhat, a_ref))
            duv_ref[rows, 512:1024] = (dv * _gelu_grad(zv)).astype(BF16)
            return carry

        lax.fori_loop(0, tm // SG_BLOCK, block, 0)

        @pl.when(i == n - 1)
        def _():
            dw_ref[...] = w_acc[...] * mask_ref[...]
            dsgb_ref[...] = _group_mean(b_acc[...], a_ref) * float(SG_GROUP_DIM)
            dlng_ref[...] = jnp.sum(g_acc[...], axis=0, keepdims=True)
            dlnb_ref[...] = jnp.sum(s_acc[...], axis=0, keepdims=True)

    tile = lambda col: pl.BlockSpec((tm, 512), lambda i: (i, col))
    smalls = [amean, lng, lnb, wcat, wtcat, bfull, maskcat]
    wide = SG_GROUPS * SG_BLOCK
    return pl.pallas_call(
        body, grid=(n,), name="sgu_bwd",
        in_specs=[tile(3), tile(4), tile(0)] + [_full(a.shape) for a in smalls],
        out_specs=[pl.BlockSpec((tm, 1024), lambda i: (i, 0)), _acc((SG_BLOCK, wide)), _acc((SG_BLOCK, 512)),
                   _acc((1, 512)), _acc((1, 512))],
        out_shape=[jax.ShapeDtypeStruct((S, 1024), BF16), jax.ShapeDtypeStruct((SG_BLOCK, wide), F32),
                   jax.ShapeDtypeStruct((SG_BLOCK, 512), F32), jax.ShapeDtypeStruct((1, 512), F32),
                   jax.ShapeDtypeStruct((1, 512), F32)],
        scratch_shapes=[pltpu.VMEM((wide, 512), BF16), pltpu.VMEM((wide, 512), BF16), pltpu.VMEM((SG_BLOCK, wide), F32),
                        pltpu.VMEM((SG_BLOCK, 512), F32), pltpu.VMEM((SG_BLOCK, 512), F32), pltpu.VMEM((SG_BLOCK, 512), F32)],
        compiler_params=_params("arbitrary"),
    )(z, z, dy, *smalls)


def bias_colsum(p):
    H, _, L = p.shape
    near_lo = (ATT_WIN - 1) * ATT_TILE - REL_CLIP + 1
    near_hi = ATT_WIN * ATT_TILE

    def body(p_ref, gw_ref, far_ref):
        k = lax.broadcasted_iota(jnp.int32, (1, L), 1)
        is_far = (k < near_lo) | (k >= near_hi)
        for h in range(H):
            g = jnp.sum(p_ref[h], axis=0, keepdims=True)
            gw_ref[h:h + 1, :] = g
            far_ref[h:h + 1, :] = jnp.zeros((1, 128), F32) + jnp.sum(jnp.where(is_far, g, 0.0))

    return pl.pallas_call(
        body, name="bias_colsum", in_specs=[_full(p.shape)], out_specs=[_acc((H, L)), _acc((H, 128))],
        out_shape=[jax.ShapeDtypeStruct((H, L), F32), jax.ShapeDtypeStruct((H, 128), F32)], grid=(1,),
        compiler_params=_params("arbitrary"),
    )(p)


def _row_tile(rows, cols):
    best = None
    for tr in range(16, rows + 1, 16):
        if rows % tr == 0 and tr * cols * 4 <= EW_BLOCK_BYTES:
            best = tr
    return best if best is not None else rows


def place_shard(w, own, name):
    R, C = w.shape
    r = R // 2
    tr = _row_tile(r, C)
    nr = r // tr

    def body(own_ref, w_ref, o_ref):
        o_ref[...] = w_ref[...].astype(BF16)

    return pl.pallas_call(
        body, name=name,
        grid_spec=pltpu.PrefetchScalarGridSpec(
            num_scalar_prefetch=1, grid=(2, nr),
            in_specs=[pl.BlockSpec((tr, C), lambda h, i, own: (h * nr + i, 0))],
            out_specs=pl.BlockSpec((None, None, tr, C), lambda h, i, own: (own[0], h, i, 0))),
        out_shape=jax.ShapeDtypeStruct((4, 2, r, C), BF16), compiler_params=_params("parallel", "parallel"),
    )(own, w)


def chip_partial(grad, recv, core, name):
    _, _, r, C = grad.shape
    tr = _row_tile(r, C)

    def body(core_ref, g_ref, r_ref, o_ref):
        o_ref[...] = (g_ref[...].astype(F32) + r_ref[...].astype(F32)).astype(BF16)

    spec = pl.BlockSpec((None, tr, C), lambda j, i, core: (j, i, 0))
    return pl.pallas_call(
        body, name=name,
        grid_spec=pltpu.PrefetchScalarGridSpec(
            num_scalar_prefetch=1, grid=(4, r // tr),
            in_specs=[pl.BlockSpec((None, None, tr, C), lambda j, i, core: (j, core[0], i, 0)), spec], out_specs=spec),
        out_shape=jax.ShapeDtypeStruct((4, r, C), BF16), compiler_params=_params("parallel", "parallel"),
    )(core, grad, recv)


def shard_sum(part, recv, own, core, name):
    _, r, C = part.shape
    tr = _row_tile(r, C)

    def body(own_ref, core_ref, p_ref, r0, r1, r2, o_ref):
        o_ref[...] = p_ref[...].astype(F32) + r0[...].astype(F32) + r1[...].astype(F32) + r2[...].astype(F32)

    return pl.pallas_call(
        body, name=name,
        grid_spec=pltpu.PrefetchScalarGridSpec(
            num_scalar_prefetch=2, grid=(r // tr,),
            in_specs=[pl.BlockSpec((None, tr, C), lambda i, own, core: (own[0], i, 0))]
            + [pl.BlockSpec((None, tr, C), lambda i, own, core, k=k: (k, i, 0)) for k in range(3)],
            out_specs=pl.BlockSpec((None, tr, C), lambda i, own, core: (core[0], i, 0))),
        out_shape=jax.ShapeDtypeStruct((2, r, C), F32), compiler_params=_params("parallel"),
    )(own, core, part, recv, recv, recv)


def adamw(w, g, m, v, name):
    R, C = w.shape
    tr = _row_tile(R, C)

    def body(w_ref, g_ref, m_ref, v_ref, d_ref, nm_ref, nv_ref):
        gv = g_ref[...]
        nm = ADAM_B1 * m_ref[...] + (1.0 - ADAM_B1) * gv
        nv = ADAM_B2 * v_ref[...] + (1.0 - ADAM_B2) * (gv * gv)
        m_hat = nm / (1.0 - ADAM_B1 ** ADAM_STEP)
        v_hat = nv / (1.0 - ADAM_B2 ** ADAM_STEP)
        d_ref[...] = -ADAM_LR * (m_hat / (jnp.sqrt(v_hat) + ADAM_EPS) + ADAM_WD * w_ref[...])
        nm_ref[...] = nm
        nv_ref[...] = nv

    spec = pl.BlockSpec((tr, C), lambda i: (i, 0))
    return pl.pallas_call(
        body, grid=(R // tr,), name=name, in_specs=[spec] * 4, out_specs=[spec] * 3,
        out_shape=[jax.ShapeDtypeStruct((R, C), F32)] * 3, compiler_params=_params("parallel"),
    )(w, g, m, v)


HBM = pl.BlockSpec(memory_space=pl.ANY)


def _place():
    x, y, c = lax.axis_index("x"), lax.axis_index("y"), lax.axis_index("c")
    chips = [(1 - x, y), (x, 1 - y), (1 - x, 1 - y)]
    return x, y, c, chips


def _gather_phases(n):
    def copies(buf, sems, kind):
        send_sems, recv_sems = sems
        x, y, c, chips = _place()

        def remote(w, k, slot, to):
            return pltpu.make_async_remote_copy(src_ref=slot, dst_ref=slot, send_sem=send_sems.at[w, k],
                                                recv_sem=recv_sems.at[w, k], device_id=to, device_id_type=MESH)

        def one(w, k, px, py):
            if kind == "mine":
                return remote(w, k, buf[w].at[2 * x + y, c], (px, py, c))
            if kind == "theirs":
                return remote(w, k, buf[w].at[2 * px + py, c], (px, py, c))
            if kind == "onward":
                return remote(w, 3 + k, buf[w].at[2 * px + py, c], (x, y, 1 - c))
            return remote(w, 3 + k, buf[w].at[2 * px + py, 1 - c], (x, y, 1 - c))

        return [one(w, k, px, py) for w in range(n) for k, (px, py) in enumerate(chips)]

    def start(_, buf, sems):
        for cp in copies(buf, sems, "mine"):
            cp.start()

    def relay(_, buf, sems):
        for theirs, onward in zip(copies(buf, sems, "theirs"), copies(buf, sems, "onward")):
            theirs.wait_recv()
            onward.start()

    def finish(_, buf, sems):
        for cp in copies(buf, sems, "relayed"):
            cp.wait_recv()
        for cp in copies(buf, sems, "mine") + copies(buf, sems, "onward"):
            cp.wait_send()

    return start, relay, finish


def _gather_sems(n):
    return (pltpu.SemaphoreType.DMA((n, 6)), pltpu.SemaphoreType.DMA((n, 6)))


def gather_weights(bufs):
    n = len(bufs)
    phases = _gather_phases(n)

    def body(*refs):
        for phase in phases:
            phase(None, refs[n:2 * n], refs[2 * n:])

    return pl.pallas_call(
        body, name="gather_weights", in_specs=[HBM] * n, out_specs=[HBM] * n,
        out_shape=[jax.ShapeDtypeStruct(b.shape, b.dtype) for b in bufs],
        input_output_aliases={i: i for i in range(n)}, scratch_shapes=list(_gather_sems(n)),
    )(*bufs)


def gather_hook(bufs, n_steps):
    start, relay, finish = _gather_phases(len(bufs))
    return Hook(ins=tuple(bufs), out_shapes=tuple(jax.ShapeDtypeStruct(b.shape, b.dtype) for b in bufs), aliased=True,
                sems=_gather_sems(len(bufs)),
                steps=((0, "before", start), ((n_steps * 13) // 16, "before", relay), (n_steps - 1, "after", finish)))


def sibling_split(grads, name):
    n = len(grads)

    def body(*refs):
        src, dst = refs[:n], refs[n:2 * n]
        send_sems, recv_sems = refs[2 * n:]
        x, y, c, _ = _place()
        sends = [pltpu.make_async_remote_copy(src_ref=src[w].at[:, 1 - c], dst_ref=dst[w], send_sem=send_sems.at[w],
                                              recv_sem=recv_sems.at[w], device_id=(x, y, 1 - c), device_id_type=MESH)
                 for w in range(n)]
        for cp in sends:
            cp.start()
        for cp in sends:
            cp.wait()

    return pl.pallas_call(
        body, name=name, in_specs=[HBM] * n, out_specs=[HBM] * n,
        out_shape=[jax.ShapeDtypeStruct((4,) + g.shape[2:], g.dtype) for g in grads],
        scratch_shapes=[pltpu.SemaphoreType.DMA((n,))] * 2,
    )(*grads)


def exchange_hook(parts, n_steps):
    n = len(parts)

    def copies(src, dst, sems):
        send_sems, recv_sems = sems
        _, _, c, chips = _place()
        return [pltpu.make_async_remote_copy(
            src_ref=src[w].at[2 * px + py], dst_ref=dst[w].at[k], send_sem=send_sems.at[w, k],
            recv_sem=recv_sems.at[w, k], device_id=(px, py, c), device_id_type=MESH)
            for w in range(n) for k, (px, py) in enumerate(chips)]

    def start(src, dst, sems):
        for cp in copies(src, dst, sems):
            cp.start()

    def finish(src, dst, sems):
        for cp in copies(src, dst, sems):
            cp.wait()

    return Hook(ins=tuple(parts), out_shapes=tuple(jax.ShapeDtypeStruct((3,) + p.shape[1:], p.dtype) for p in parts),
                sems=(pltpu.SemaphoreType.DMA((n, 3)), pltpu.SemaphoreType.DMA((n, 3))),
                steps=((0, "before", start), (n_steps - 1, "after", finish)))


def sibling_join(sums):
    n = len(sums)

    def body(*refs):
        buf = refs[n:2 * n]
        send_sems, recv_sems = refs[2 * n:]
        x, y, c, _ = _place()
        sends = [pltpu.make_async_remote_copy(src_ref=buf[w].at[c], dst_ref=buf[w].at[c], send_sem=send_sems.at[w],
                                              recv_sem=recv_sems.at[w], device_id=(x, y, 1 - c), device_id_type=MESH)
                 for w in range(n)]
        for cp in sends:
            cp.start()
        for w, cp in enumerate(sends):
            cp.wait_send()
            pltpu.make_async_remote_copy(src_ref=buf[w].at[c], dst_ref=buf[w].at[1 - c], send_sem=send_sems.at[w],
                                         recv_sem=recv_sems.at[w], device_id=(x, y, 1 - c), device_id_type=MESH).wait_recv()

    return pl.pallas_call(
        body, name="sibling_join", in_specs=[HBM] * n, out_specs=[HBM] * n,
        out_shape=[jax.ShapeDtypeStruct(s.shape, s.dtype) for s in sums],
        input_output_aliases={i: i for i in range(n)},
        scratch_shapes=[pltpu.SemaphoreType.DMA((n,))] * 2,
    )(*sums)


def allreduce_small(v):
    R, C = v.shape

    def body(v_ref, out_ref, all_ref, send_sems, recv_sems):
        x, y, c, chips = _place()
        me, sibling = (x, y, c), (x, y, 1 - c)

        def slot(px, py, pc):
            return all_ref.at[4 * px + 2 * py + pc]

        def copy(k, block, to, src=None):
            return pltpu.make_async_remote_copy(src_ref=slot(*block) if src is None else src, dst_ref=slot(*block),
                                                send_sem=send_sems.at[k], recv_sem=recv_sems.at[k], device_id=to,
                                                device_id_type=MESH)

        first = [copy(0, me, sibling, src=v_ref)] + [copy(1 + j, me, (*chip, c), src=v_ref) for j, chip in enumerate(chips)]
        for cp in first:
            cp.start()
        slot(*me)[...] = v_ref[...]
        passed = [copy(4 + j, (*chip, c), sibling) for j, chip in enumerate(chips)]
        for j, chip in enumerate(chips):
            copy(1 + j, (*chip, c), me).wait_recv()
            passed[j].start()
        copy(0, sibling, me).wait_recv()
        for j, chip in enumerate(chips):
            copy(4 + j, (*chip, 1 - c), me).wait_recv()
        for cp in first + passed:
            cp.wait_send()
        acc = all_ref[0]
        for k in range(1, 8):
            acc = acc + all_ref[k]
        out_ref[...] = acc

    vmem = pl.BlockSpec(memory_space=pltpu.VMEM)
    return pl.pallas_call(
        body, name="allreduce_small", in_specs=[vmem], out_specs=vmem, out_shape=jax.ShapeDtypeStruct((R, C), F32),
        scratch_shapes=[pltpu.VMEM((8, R, C), F32), pltpu.SemaphoreType.DMA((7,)), pltpu.SemaphoreType.DMA((7,))],
        compiler_params=pltpu.CompilerParams(vmem_limit_bytes=VMEM_LIMIT_V7X),
    )(v)


SMALL_PAD = 1024


def _pack_small(arrs):
    parts = []
    for a in arrs:
        f = a.reshape(-1).astype(F32)
        parts.append(jnp.pad(f, (0, (-f.shape[0]) % SMALL_PAD)))
    return jnp.concatenate(parts).reshape(-1, 128)


def _unpack_small(packed, shapes):
    flat = packed.reshape(-1)
    outs, off = [], 0
    for s in shapes:
        size = int(np.prod(s))
        outs.append(flat[off:off + size].reshape(s))
        off += size + (-size) % SMALL_PAD
    return outs


ATT_KEYS = ATT_WIN * ATT_TILE
ATT_NEAR_LO = (ATT_WIN - 1) * ATT_TILE - REL_CLIP + 1
ATT_PERIOD = ATT_KEYS + ATT_TILE + 1


def _att_bias_table(rel_bias):
    far = rel_bias[:, 2 * REL_CLIP:]
    near = rel_bias[:, 2 * REL_CLIP - 1:0:-1]
    w = jnp.concatenate([jnp.broadcast_to(far, (ATT_HEADS, ATT_NEAR_LO)), near,
                         jnp.broadcast_to(far, (ATT_HEADS, ATT_PERIOD - ATT_KEYS))], axis=1)
    rows = jnp.tile(w, (1, ATT_TILE))[:, :ATT_TILE * (ATT_PERIOD - 1)].reshape(ATT_HEADS, ATT_TILE, ATT_PERIOD - 1)
    i = np.arange(ATT_TILE)[:, None]
    m = np.arange(ATT_KEYS)[None, :]
    qc, kc = i // CHUNK, m // CHUNK
    band = (kc >= qc) & (kc <= qc + N_PREV_CHUNKS)
    return jnp.where(band[None], rows[:, :, :ATT_KEYS], NEG_INF)


def _rel_bias_grad(dbias):
    p = jnp.pad(dbias, ((0, 0), (0, 0), (0, ATT_PERIOD - 1 - ATT_KEYS))).reshape(ATT_HEADS, -1)
    p = jnp.pad(p, ((0, 0), (0, ATT_TILE))).reshape(ATT_HEADS, ATT_TILE, ATT_PERIOD)
    gw, far = bias_colsum(p)
    return jnp.concatenate([jnp.zeros((ATT_HEADS, 1), F32), gw[:, ATT_KEYS - 1:ATT_NEAR_LO - 1:-1], far[:, :1]], axis=1)


def kernel(x, mem, norm_mix_g, w_in, rel_bias, sg_ln_g, sg_ln_b, sg_w, sg_b, w_branch_att, w_branch_sg, w_out, norm_xattn_g, norm_mem_g, w_xq, w_xkv, w_xo, norm_ffn_g, w_ffn_in, w_ffn_out, norm_final_g, loss_target, m_norm_mix_g, m_w_in, m_rel_bias, m_sg_ln_g, m_sg_ln_b, m_sg_w, m_sg_b, m_w_branch_att, m_w_branch_sg, m_w_out, m_norm_xattn_g, m_norm_mem_g, m_w_xq, m_w_xkv, m_w_xo, m_norm_ffn_g, m_w_ffn_in, m_w_ffn_out, m_norm_final_g, v_norm_mix_g, v_w_in, v_rel_bias, v_sg_ln_g, v_sg_ln_b, v_sg_w, v_sg_b, v_w_branch_att, v_w_branch_sg, v_w_out, v_norm_xattn_g, v_norm_mem_g, v_w_xq, v_w_xkv, v_w_xo, v_norm_ffn_g, v_w_ffn_in, v_w_ffn_out, v_norm_final_g):
    S, D = x.shape[1], x.shape[2]
    x2d, mem2d, tgt = x[0], mem[0], loss_target[0]

    big_names = ["w_in", "w_branch_att", "w_branch_sg", "w_out", "w_xq", "w_xkv", "w_xo", "w_ffn_in", "w_ffn_out"]
    col_sharded = [True, True, True, False, False, True, False, True, False]
    big_w = [a[0] for a in (w_in, w_branch_att, w_branch_sg, w_out, w_xq, w_xkv, w_xo, w_ffn_in, w_ffn_out)]
    big_m = [a[0] for a in (m_w_in, m_w_branch_att, m_w_branch_sg, m_w_out, m_w_xq, m_w_xkv, m_w_xo, m_w_ffn_in, m_w_ffn_out)]
    big_v = [a[0] for a in (v_w_in, v_w_branch_att, v_w_branch_sg, v_w_out, v_w_xq, v_w_xkv, v_w_xo, v_w_ffn_in, v_w_ffn_out)]

    own = (2 * lax.axis_index("x") + lax.axis_index("y")).astype(jnp.int32).reshape(1)
    core = lax.axis_index("c").astype(jnp.int32).reshape(1)
    placed = [place_shard(w, own, "place_" + nm) for w, nm in zip(big_w, big_names)]

    def whole(g4, i):
        R, C = big_w[i].shape
        return g4.reshape(4, R, C) if col_sharded[i] else g4.reshape(4 * R, C)

    W_in = whole(gather_weights(placed[:1])[0], 0)

    g_mix, g_xat, g_mem, g_ffn = norm_mix_g, norm_xattn_g, norm_mem_g, norm_ffn_g
    g_fin = norm_final_g.reshape(1, D)
    bias = _att_bias_table(rel_bias[0])
    tt = np.arange(SG_BLOCK)
    sg_mask = (tt[None, :] // CHUNK) <= (tt[:, None] // CHUNK)
    wm_f = jnp.where(sg_mask[None], sg_w[0], 0.0)
    wide = SG_GROUPS * SG_BLOCK
    wcat = jnp.transpose(wm_f, (1, 0, 2)).reshape(SG_BLOCK, wide).astype(BF16)
    wtcat = jnp.transpose(wm_f, (2, 0, 1)).reshape(SG_BLOCK, wide).astype(BF16)
    maskcat = jnp.asarray(np.tile(sg_mask, (1, SG_GROUPS)), F32)
    bfull = jnp.repeat(sg_b[0].T, SG_GROUP_DIM, axis=1)
    lng, lnb = sg_ln_g[0].reshape(1, 512), sg_ln_b[0].reshape(1, 512)
    gid = np.arange(512) // SG_GROUP_DIM
    amean = jnp.asarray((gid[:, None] == gid[None, :]) / SG_GROUP_DIM, BF16)

    h1, z, *early = norm_mm(x2d, g_mix, W_in, tm=ROW_TILE, scale=None, name="norm_mm_in",
                            hook=gather_hook(placed[1:7], S // ROW_TILE))
    y_att, lse_att, *late = attn_fwd(z, bias, gather_hook(placed[7:], S // ATT_TILE))
    W_ba, W_bs, W_out, W_xq, W_xkv, W_xo, W_fi, W_fo = [whole(g4, i + 1) for i, g4 in enumerate(early + late)]
    y_sg = sgu_fwd(z, amean, lng, lnb, wcat, bfull, tm=512)
    a_br, b_br, merged, x1 = merge_fwd(y_att, y_sg, z, x2d, W_ba, W_bs, W_out, tm=ROW_TILE)
    h2, xq = norm_mm(x1, g_xat, W_xq.reshape(1, D, D), tm=512, scale=XATT_HEAD_DIM ** -0.5, name="norm_mm_xq")
    mn, kv = norm_mm(mem2d, g_mem, W_xkv, tm=mem2d.shape[0], scale=None, name="norm_mm_kv")
    o_x, lse_x, x2 = xattn_fwd(xq, kv, W_xo, x1, tm=512)
    h3, gu = norm_mm(x2, g_ffn, W_fi, tm=ROW_TILE, scale=None, name="norm_mm_ffn")
    act, x3 = ffn_out_fwd(gu, W_fo, x2, tm=ROW_TILE)
    loss_vec, dx3, dg_fin = loss_head(x3, g_fin, tgt, tm=512)

    dgu = ffn_act_bwd(dx3, gu, W_fo, tm=ROW_TILE, nchunk=2)
    gW_fo = tn_mm(act, dx3, shards=1, tn=D, tk=512, name="dw_ffn_out")
    gW_fi = tn_mm(h3, dgu, shards=4, tn=W_fi.shape[2], tk=512, name="dw_ffn_in")
    dx2, dg_ffn = mm_nt_norm_bwd(dgu, W_fi, x2, g_ffn, dx3, tm=ROW_TILE, name="dx_ffn")
    dq_x, dkv = xattn_bwd(dx2, xq, o_x, lse_x, kv, W_xo, tm=512)
    gW_xo = tn_mm(o_x, dx2, shards=1, tn=D, tk=512, name="dw_xo")
    gW_xq = tn_mm(h2, dq_x, shards=1, tn=D, tk=512, name="dw_xq")
    dx1, dg_xat = mm_nt_norm_bwd(dq_x, W_xq.reshape(1, D, D), x1, g_xat, dx2, tm=512, name="dx_xq")
    gW_xkv = tn_mm(mn, dkv, shards=4, tn=2 * D, tk=mem2d.shape[0], name="dw_xkv")
    _, dg_mem = mm_nt_norm_bwd(dkv.astype(BF16), W_xkv, mem2d, g_mem, None, tm=mem2d.shape[0], name="dx_mem")
    da, db, dgab, dy_att, dy_sg = merge_bwd(dx1, W_out, a_br, b_br, z, W_ba, W_bs, tm=ROW_TILE)
    gW_out = tn_mm(merged, dx1, shards=1, tn=D, tk=512, name="dw_out")
    gW_ba = tn_mm(y_att, da, shards=4, tn=D, tk=512, name="dw_branch_att")
    gW_bs = tn_mm(y_sg, db, shards=4, tn=D, tk=512, name="dw_branch_sg")

    def canon(g, i):
        R, C = big_w[i].shape
        return g.reshape(4, 2, R // 2, C)

    def stage1(grads, idx, tag):
        gs = [canon(g, i) for g, i in zip(grads, idx)]
        recv = sibling_split(gs, "sibling_split_" + tag)
        return [chip_partial(g, r, core, "chip_partial_" + big_names[i]) for g, r, i in zip(gs, recv, idx)]

    parts_b = stage1([gW_ba, gW_bs, gW_out, gW_xq, gW_xkv, gW_xo, gW_fi, gW_fo], range(1, 9), "late")
    nt_bwd = S // ATT_TILE + ATT_WIN - 1
    dq_a, dk_a, dv_a, dbias, *from_chips_b = attn_bwd(z, y_att, dy_att, lse_att, bias, exchange_hook(parts_b, nt_bwd))
    duv, dwcat, dsgb_full, dlng, dlnb = sgu_bwd(z, dy_sg, amean, lng, lnb, wcat, wtcat, bfull, maskcat, tm=512)
    dwm = jnp.transpose(dwcat.reshape(SG_BLOCK, SG_GROUPS, SG_BLOCK), (1, 0, 2))
    dsgb = dsgb_full[:, ::SG_GROUP_DIM].T
    dz = jnp.concatenate([dq_a, dk_a, dv_a, duv, dgab], axis=1)
    gW_in = tn_mm(h1, dz, shards=4, tn=W_in.shape[2], tk=512, name="dw_in")
    parts_a = stage1([gW_in], [0], "w_in")
    dx, dg_mix, *from_chips_a = mm_nt_norm_bwd(dz, W_in, x2d, g_mix, dx1, tm=ROW_TILE, name="dx_in",
                                               hook=exchange_hook(parts_a, S // ROW_TILE))
    d_rel = _rel_bias_grad(dbias)

    parts, from_chips = parts_a + parts_b, from_chips_a + from_chips_b
    sums = [shard_sum(p, r, own, core, "shard_sum_" + nm) for p, r, nm in zip(parts, from_chips, big_names)]
    joined = sibling_join(sums)
    big_out = []
    for j, w, m_, v_, nm in zip(joined, big_w, big_m, big_v, big_names):
        g = j.reshape(w.shape)
        big_out.append((g,) + tuple(adamw(w, g, m_, v_, "adamw_" + nm)))

    small_w = [norm_mix_g, rel_bias, sg_ln_g, sg_ln_b, sg_w, sg_b, norm_xattn_g, norm_mem_g, norm_ffn_g, norm_final_g]
    small_m = [m_norm_mix_g, m_rel_bias, m_sg_ln_g, m_sg_ln_b, m_sg_w, m_sg_b, m_norm_xattn_g, m_norm_mem_g, m_norm_ffn_g, m_norm_final_g]
    small_v = [v_norm_mix_g, v_rel_bias, v_sg_ln_g, v_sg_ln_b, v_sg_w, v_sg_b, v_norm_xattn_g, v_norm_mem_g, v_norm_ffn_g, v_norm_final_g]
    small_g = [dg_mix, d_rel, dlng, dlnb, dwm, dsgb, dg_xat, dg_mem, dg_ffn, dg_fin]
    shapes = [w.shape for w in small_w]
    g_sum = allreduce_small(_pack_small(small_g + [loss_vec[0, :1]]))
    zero = jnp.zeros((1,), F32)
    d_s, m_s, v_s = adamw(_pack_small(small_w + [zero]), g_sum, _pack_small(small_m + [zero]), _pack_small(small_v + [zero]),
                          "adamw_small")
    sg_, sd, sm, sv_ = (_unpack_small(p, shapes + [(1,)]) for p in (g_sum, d_s, m_s, v_s))
    loss = sg_[-1][0]

    order = ["norm_mix_g", "w_in", "rel_bias", "sg_ln_g", "sg_ln_b", "sg_w", "sg_b", "w_branch_att", "w_branch_sg", "w_out",
             "norm_xattn_g", "norm_mem_g", "w_xq", "w_xkv", "w_xo", "norm_ffn_g", "w_ffn_in", "w_ffn_out", "norm_final_g"]
    small_names = ["norm_mix_g", "rel_bias", "sg_ln_g", "sg_ln_b", "sg_w", "sg_b", "norm_xattn_g", "norm_mem_g", "norm_ffn_g",
                   "norm_final_g"]
    res = {}
    for i, nm in enumerate(small_names):
        res[nm] = (sg_[i], sd[i], sm[i], sv_[i])
    for nm, outs in zip(big_names, big_out):
        res[nm] = tuple(o[None] for o in outs)
    return (loss, dx[None], *[res[nm][0] for nm in order], *[res[nm][1] for nm in order],
            *[res[nm][2] for nm in order], *[res[nm][3] for nm in order])
```

```python
import functools
from typing import NamedTuple

import numpy as np
import jax
import jax.numpy as jnp
from jax import lax
from jax.experimental import pallas as pl
from jax.experimental.pallas import tpu as pltpu

F32, BF16 = jnp.float32, jnp.bfloat16
MESH = pl.DeviceIdType.MESH

EPS = 1e-6
NEG_INF = -1e30
CHUNK = 64
N_PREV_CHUNKS = 8
ATT_HEADS, ATT_HEAD_DIM = 8, 64
REL_CLIP = 128
SG_BLOCK, SG_GROUPS, SG_GROUP_DIM = 128, 8, 64
XATT_HEADS, XATT_HEAD_DIM = 4, 256
ATT_TILE = 128
ATT_WIN = 5
ADAM_LR, ADAM_B1, ADAM_B2, ADAM_EPS, ADAM_WD, ADAM_STEP = 0.001, 0.9, 0.999, 1e-08, 0.01, 10

VMEM_LIMIT_V7X = 56 * 1024 * 1024
EW_BLOCK_BYTES = 1 << 20
ROW_TILE = 512
DW_TOKENS = 1024
HBM = pl.BlockSpec(memory_space=pl.ANY)


def _params(*sem):
    return pltpu.CompilerParams(dimension_semantics=sem, vmem_limit_bytes=VMEM_LIMIT_V7X)


def _full(shape):
    n = len(shape)
    return pl.BlockSpec(shape, lambda *_: (0,) * n, pipeline_mode=pl.Buffered(1))


def _acc(shape):
    n = len(shape)
    return pl.BlockSpec(shape, lambda *_: (0,) * n)


class Hook(NamedTuple):
    ins: tuple = ()
    out_shapes: tuple = ()
    aliased: bool = False
    sems: tuple = ()
    steps: tuple = ()


def _run_hook(hook, pos, h_in, h_out, h_sems):
    for step, where, fn in hook.steps:
        if where == pos:
            pl.when(pl.program_id(0) == step)(functools.partial(fn, h_in, h_out, h_sems))


def _hooked_call(body, hook, *, n_in, n_out, in_specs, out_specs, out_shape, scratch_shapes=(), **kw):
    nh = len(hook.ins)
    aliases = {n_in + i: n_out + i for i in range(nh)} if hook.aliased else {}
    return pl.pallas_call(
        body, in_specs=list(in_specs) + [HBM] * nh, out_specs=list(out_specs) + [HBM] * len(hook.out_shapes),
        out_shape=list(out_shape) + list(hook.out_shapes), scratch_shapes=list(scratch_shapes) + list(hook.sems),
        input_output_aliases=aliases, **kw)


def _nt(a, b):
    return lax.dot_general(a, b, (((1,), (1,)), ((), ())), preferred_element_type=F32)


def _tn(a, b):
    return lax.dot_general(a, b, (((0,), (0,)), ((), ())), preferred_element_type=F32)


def _nn(a, b):
    return jnp.dot(a, b, preferred_element_type=F32)


def _sigmoid(x):
    return 1.0 / (1.0 + jnp.exp(-x))


_GELU_C = float(np.sqrt(2.0 / np.pi))


def _gelu(x):
    t = jnp.tanh(_GELU_C * (x + 0.044715 * (x * x * x)))
    return x * (0.5 * (1.0 + t))


def _gelu_grad(x):
    t = jnp.tanh(_GELU_C * (x + 0.044715 * (x * x * x)))
    return 0.5 * (1.0 + t) + 0.5 * x * (1.0 - t * t) * (_GELU_C * (1.0 + 3.0 * 0.044715 * x * x))


def _rms_stats(xf):
    rstd = lax.rsqrt(jnp.mean(xf * xf, axis=-1, keepdims=True) + EPS)
    return rstd, xf * rstd


def _rms_bwd(xhat, rstd, g, dh):
    dxh = dh * g
    dx = rstd * (dxh - xhat * jnp.mean(dxh * xhat, axis=-1, keepdims=True))
    return dx, dh * xhat


def norm_mm(x, g, w, *, tm, scale, name, hook=Hook()):
    S, D = x.shape
    J, _, C = w.shape
    nh, nho = len(hook.ins), len(hook.out_shapes)

    def body(x_ref, g_ref, w_ref, *rest):
        h_in, (h_ref, z_ref), h_out, h_sems = rest[:nh], rest[nh:nh + 2], rest[nh + 2:nh + 2 + nho], rest[nh + 2 + nho:]
        _run_hook(hook, "before", h_in, h_out, h_sems)
        _, xhat = _rms_stats(x_ref[...])
        h = (xhat * g_ref[...]).astype(BF16)
        h_ref[...] = h
        for j in range(J):
            acc = _nn(h, w_ref[j])
            if scale is not None:
                acc = acc * scale
            z_ref[:, j * C:(j + 1) * C] = acc.astype(BF16)
        _run_hook(hook, "after", h_in, h_out, h_sems)

    return _hooked_call(
        body, hook, n_in=3, n_out=2, grid=(S // tm,), name=name,
        in_specs=[pl.BlockSpec((tm, D), lambda i: (i, 0)), _full((1, D)), _full((J, D, C))],
        out_specs=[pl.BlockSpec((tm, D), lambda i: (i, 0)), pl.BlockSpec((tm, J * C), lambda i: (i, 0))],
        out_shape=[jax.ShapeDtypeStruct((S, D), BF16), jax.ShapeDtypeStruct((S, J * C), BF16)],
        compiler_params=_params("arbitrary"),
    )(x, g, w, *hook.ins)


def _att_window_specs(nt, col):
    return [pl.BlockSpec((ATT_TILE, 512), lambda t, j=j: (jnp.clip(t - (ATT_WIN - 1) + j, 0, nt - 1), col))
            for j in range(ATT_WIN)]


ATT_SCALE = ATT_HEAD_DIM ** -0.5


def _att_scores(q_scaled, k, bias, valid):
    return jnp.where(valid, _nt(q_scaled, k) + bias, NEG_INF)


def _att_valid(t):
    kpos = lax.broadcasted_iota(jnp.int32, (ATT_TILE, ATT_WIN * ATT_TILE), 1) + (t - (ATT_WIN - 1)) * ATT_TILE
    return kpos >= 0


def attn_fwd(z, bias, hook=Hook()):
    S = z.shape[0]
    nt = S // ATT_TILE
    n_in, nh, nho = 2 + 2 * ATT_WIN, len(hook.ins), len(hook.out_shapes)

    def body(q_ref, *refs):
        k_refs, v_refs = refs[:ATT_WIN], refs[ATT_WIN:2 * ATT_WIN]
        bias_ref = refs[2 * ATT_WIN]
        rest = refs[2 * ATT_WIN + 1:]
        h_in, (y_ref, lse_ref), h_out = rest[:nh], rest[nh:nh + 2], rest[nh + 2:nh + 2 + nho]
        s_scr, p_scr = rest[nh + 2 + nho:nh + 4 + nho]
        h_sems = rest[nh + 4 + nho:]
        _run_hook(hook, "before", h_in, h_out, h_sems)
        valid = _att_valid(pl.program_id(0))
        heads = [slice(h * ATT_HEAD_DIM, (h + 1) * ATT_HEAD_DIM) for h in range(ATT_HEADS)]
        for h, sl in enumerate(heads):
            k = jnp.concatenate([r[:, sl] for r in k_refs], axis=0)
            s_scr[h] = _att_scores(q_ref[:, sl] * ATT_SCALE, k, bias_ref[h], valid)
        stats = []
        for h in range(ATT_HEADS):
            s = s_scr[h]
            m = jnp.max(s, axis=-1, keepdims=True)
            p = jnp.exp(s - m)
            stats.append((m, jnp.sum(p, axis=-1, keepdims=True)))
            p_scr[h] = p.astype(BF16)
        for h, sl in enumerate(heads):
            m, l = stats[h]
            v = jnp.concatenate([r[:, sl] for r in v_refs], axis=0)
            y_ref[:, sl] = (_nn(p_scr[h], v) / l).astype(BF16)
            lse_ref[:, h:h + 1] = m + jnp.log(l)
        _run_hook(hook, "after", h_in, h_out, h_sems)

    tile = lambda col: pl.BlockSpec((ATT_TILE, 512), lambda t: (t, col))
    return _hooked_call(
        body, hook, n_in=n_in, n_out=2, grid=(nt,), name="attn_fwd",
        in_specs=[tile(0)] + _att_window_specs(nt, 1) + _att_window_specs(nt, 2) + [_full(bias.shape)],
        out_specs=[tile(0), pl.BlockSpec((ATT_TILE, ATT_HEADS), lambda t: (t, 0))],
        out_shape=[jax.ShapeDtypeStruct((S, 512), BF16), jax.ShapeDtypeStruct((S, ATT_HEADS), F32)],
        scratch_shapes=[pltpu.VMEM((ATT_HEADS, ATT_TILE, ATT_WIN * ATT_TILE), F32),
                        pltpu.VMEM((ATT_HEADS, ATT_TILE, ATT_WIN * ATT_TILE), BF16)],
        compiler_params=_params("arbitrary"),
    )(z, *([z] * (2 * ATT_WIN)), bias, *hook.ins)


def _group_mean(x, a_ref):
    hi = x.astype(BF16)
    lo = (x - hi.astype(F32)).astype(BF16)
    return _nn(hi, a_ref[...]) + _nn(lo, a_ref[...])


def _block_diag(x, ref):
    group = lax.broadcasted_iota(jnp.int32, x.shape, 1) // SG_GROUP_DIM
    for g in range(SG_GROUPS):
        ref[g * SG_BLOCK:(g + 1) * SG_BLOCK, :] = jnp.where(group == g, x, 0.0).astype(BF16)


def _sgu_block(zu_ref, zv_ref, rows, a_ref, lng_ref, lnb_ref, wcat_ref, bfull_ref, vbd_ref):
    zu = zu_ref[rows, :].astype(F32)
    zv = zv_ref[rows, :].astype(F32)
    u, v = _gelu(zu), _gelu(zv)
    vc = v - _group_mean(v, a_ref)
    rstd = lax.rsqrt(_group_mean(vc * vc, a_ref) + EPS)
    xhat = vc * rstd
    _block_diag(xhat * lng_ref[...] + lnb_ref[...], vbd_ref)
    sv = _nn(wcat_ref[...], vbd_ref[...]) + bfull_ref[...]
    return zu, zv, u, xhat, rstd, sv


def _sgu_rows(b):
    return pl.ds(pl.multiple_of(b * SG_BLOCK, SG_BLOCK), SG_BLOCK)


def sgu_fwd(z, amean, lng, lnb, wcat, bfull, *, tm):
    S = z.shape[0]

    def body(zu_ref, zv_ref, a_ref, lng_ref, lnb_ref, wcat_ref, bfull_ref, y_ref, vbd_ref):
        def block(b, carry):
            rows = _sgu_rows(b)
            _, _, u, _, _, sv = _sgu_block(zu_ref, zv_ref, rows, a_ref, lng_ref, lnb_ref, wcat_ref, bfull_ref, vbd_ref)
            y_ref[rows, :] = (u * sv).astype(BF16)
            return carry

        lax.fori_loop(0, tm // SG_BLOCK, block, 0)

    tile = lambda col: pl.BlockSpec((tm, 512), lambda i: (i, col))
    smalls = [amean, lng, lnb, wcat, bfull]
    return pl.pallas_call(
        body, grid=(S // tm,), name="sgu_fwd",
        in_specs=[tile(3), tile(4)] + [_full(a.shape) for a in smalls],
        out_specs=tile(0), out_shape=jax.ShapeDtypeStruct((S, 512), BF16),
        scratch_shapes=[pltpu.VMEM((SG_GROUPS * SG_BLOCK, 512), BF16)],
        compiler_params=_params("parallel"),
    )(z, z, *smalls)


def _gate_specs(tm):
    return [pl.BlockSpec((tm, 512), lambda i, c=c: (i, c)) for c in (5, 6, 7, 8)]


def merge_fwd(y_att, y_sg, z, x, w_ba, w_bs, w_out, *, tm):
    S, D = x.shape
    J, _, C = w_ba.shape

    def body(ya_ref, ys_ref, g0, g1, g2, g3, x_ref, wba_ref, wbs_ref, wo_ref, a_ref, b_ref, m_ref, x1_ref):
        ya, ys = ya_ref[...], ys_ref[...]
        a = jnp.concatenate([_nn(ya, wba_ref[j]) for j in range(J)], axis=1)
        b = jnp.concatenate([_nn(ys, wbs_ref[j]) for j in range(J)], axis=1)
        ga = jnp.concatenate([g0[...], g1[...]], axis=1).astype(F32)
        gb = jnp.concatenate([g2[...], g3[...]], axis=1).astype(F32)
        a_ref[...] = a.astype(BF16)
        b_ref[...] = b.astype(BF16)
        merged = (_sigmoid(ga) * a + _sigmoid(gb) * b).astype(BF16)
        m_ref[...] = merged
        x1_ref[...] = x_ref[...] + _nn(merged, wo_ref[...])

    row = lambda n: pl.BlockSpec((tm, n), lambda i: (i, 0))
    return pl.pallas_call(
        body, grid=(S // tm,), name="merge_fwd",
        in_specs=[row(512), row(512)] + _gate_specs(tm) + [row(D), _full(w_ba.shape), _full(w_bs.shape), _full(w_out.shape)],
        out_specs=[row(D)] * 4,
        out_shape=[jax.ShapeDtypeStruct((S, D), BF16)] * 3 + [jax.ShapeDtypeStruct((S, D), F32)],
        compiler_params=_params("parallel"),
    )(y_att, y_sg, z, z, z, z, x, w_ba, w_bs, w_out)


def xattn_fwd(xq, kv, w_xo, x1, *, tm):
    S, D = xq.shape
    dh = XATT_HEAD_DIM

    def body(q_ref, kv_ref, wo_ref, x1_ref, o_ref, lse_ref, x2_ref):
        outs = []
        for h in range(XATT_HEADS):
            s = _nt(q_ref[:, h * dh:(h + 1) * dh], kv_ref[:, h * dh:(h + 1) * dh])
            m = jnp.max(s, axis=-1, keepdims=True)
            p = jnp.exp(s - m)
            l = jnp.sum(p, axis=-1, keepdims=True)
            outs.append((_nn(p.astype(BF16), kv_ref[:, D + h * dh:D + (h + 1) * dh]) / l).astype(BF16))
            lse_ref[:, h:h + 1] = m + jnp.log(l)
        o = jnp.concatenate(outs, axis=1)
        o_ref[...] = o
        x2_ref[...] = x1_ref[...] + _nn(o, wo_ref[...])

    row = lambda n: pl.BlockSpec((tm, n), lambda i: (i, 0))
    return pl.pallas_call(
        body, grid=(S // tm,), name="xattn_fwd",
        in_specs=[row(D), _full(kv.shape), _full(w_xo.shape), row(D)],
        out_specs=[row(D), row(XATT_HEADS), row(D)],
        out_shape=[jax.ShapeDtypeStruct((S, D), BF16), jax.ShapeDtypeStruct((S, XATT_HEADS), F32),
                   jax.ShapeDtypeStruct((S, D), F32)],
        compiler_params=_params("parallel"),
    )(xq, kv, w_xo, x1)


def ffn_out_fwd(gu, w, x2, *, tm):
    S, D = x2.shape
    F = w.shape[0]

    def body(gu_ref, w_ref, x2_ref, act_ref, x3_ref):
        gate = gu_ref[:, :F].astype(F32)
        up = gu_ref[:, F:].astype(F32)
        act = (gate * _sigmoid(gate) * up).astype(BF16)
        act_ref[...] = act
        x3_ref[...] = x2_ref[...] + _nn(act, w_ref[...])

    row = lambda n: pl.BlockSpec((tm, n), lambda i: (i, 0))
    return pl.pallas_call(
        body, grid=(S // tm,), name="ffn_out_fwd",
        in_specs=[row(2 * F), _full(w.shape), row(D)], out_specs=[row(F), row(D)],
        out_shape=[jax.ShapeDtypeStruct((S, F), BF16), jax.ShapeDtypeStruct((S, D), F32)],
        compiler_params=_params("parallel"),
    )(gu, w, x2)


def loss_head(x3, g, target, *, tm):
    S, D = x3.shape

    def body(x_ref, g_ref, t_ref, loss_ref, dx_ref, dg_ref):
        @pl.when(pl.program_id(0) == 0)
        def _():
            loss_ref[...] = jnp.zeros_like(loss_ref)
            dg_ref[...] = jnp.zeros_like(dg_ref)

        gv = g_ref[...]
        rstd, xhat = _rms_stats(x_ref[...])
        err = xhat * gv - t_ref[...]
        loss_ref[...] += 0.5 * jnp.sum(jnp.mean(err * err, axis=-1, keepdims=True))
        dx, dgc = _rms_bwd(xhat, rstd, gv, err * (1.0 / D))
        dx_ref[...] = dx
        dg_ref[...] += jnp.sum(dgc, axis=0, keepdims=True)

    row = pl.BlockSpec((tm, D), lambda i: (i, 0))
    return pl.pallas_call(
        body, grid=(S // tm,), name="loss_head",
        in_specs=[row, _full((1, D)), row], out_specs=[_acc((1, 128)), row, _acc((1, D))],
        out_shape=[jax.ShapeDtypeStruct((1, 128), F32), jax.ShapeDtypeStruct((S, D), F32),
                   jax.ShapeDtypeStruct((1, D), F32)],
        compiler_params=_params("arbitrary"),
    )(x3, g, target)


def ffn_act_bwd(dx3, gu, w, *, tm, nchunk):
    S, D = dx3.shape
    F = w.shape[0]
    cn = F // nchunk

    def body(dx_ref, gu_ref, w_ref, dgu_ref):
        dxb = dx_ref[...].astype(BF16)
        for j in range(nchunk):
            dact = _nt(dxb, w_ref[j * cn:(j + 1) * cn, :])
            gate = gu_ref[:, j * cn:(j + 1) * cn].astype(F32)
            up = gu_ref[:, F + j * cn:F + (j + 1) * cn].astype(F32)
            sg = _sigmoid(gate)
            dgu_ref[:, j * cn:(j + 1) * cn] = (dact * up * (sg * (1.0 + gate * (1.0 - sg)))).astype(BF16)
            dgu_ref[:, F + j * cn:F + (j + 1) * cn] = (dact * (gate * sg)).astype(BF16)

    row = lambda n: pl.BlockSpec((tm, n), lambda i: (i, 0))
    return pl.pallas_call(
        body, grid=(S // tm,), name="ffn_act_bwd",
        in_specs=[row(D), row(2 * F), _full(w.shape)], out_specs=row(2 * F),
        out_shape=jax.ShapeDtypeStruct((S, 2 * F), BF16), compiler_params=_params("parallel"),
    )(dx3, gu, w)


def tn_mm(a, b, *, shards, tn, tk, name):
    K, M = a.shape
    N = b.shape[1]
    C = N // shards
    per = tn // C
    nk = K // tk

    def body(a_ref, b_ref, o_ref, acc_ref):
        k = pl.program_id(1)

        @pl.when(k == 0)
        def _():
            acc_ref[...] = jnp.zeros_like(acc_ref)

        acc_ref[...] += _tn(a_ref[...].astype(BF16), b_ref[...].astype(BF16))

        @pl.when(k == nk - 1)
        def _():
            if shards > 1:
                for s in range(per):
                    o_ref[s] = acc_ref[:, s * C:(s + 1) * C].astype(BF16)
            else:
                o_ref[...] = acc_ref[...].astype(BF16)

    if shards > 1:
        out_spec = pl.BlockSpec((per, M, C), lambda n, k: (n, 0, 0))
        out_shape = jax.ShapeDtypeStruct((shards, M, C), BF16)
    else:
        out_spec = pl.BlockSpec((M, tn), lambda n, k: (0, n))
        out_shape = jax.ShapeDtypeStruct((M, N), BF16)
    return pl.pallas_call(
        body, grid=(N // tn, nk), name=name,
        in_specs=[pl.BlockSpec((tk, M), lambda n, k: (k, 0)), pl.BlockSpec((tk, tn), lambda n, k: (k, n))],
        out_specs=out_spec, out_shape=out_shape, scratch_shapes=[pltpu.VMEM((M, tn), F32)],
        compiler_params=_params("parallel", "arbitrary"),
    )(a, b)


def mm_nt_norm_bwd(dy, w, x, g, dx_in, *, tm, name, hook=Hook()):
    S, D = x.shape
    J, _, C = w.shape
    has_in = dx_in is not None
    n_in, nh, nho = 4 + has_in, len(hook.ins), len(hook.out_shapes)

    def body(*refs):
        dy_ref, w_ref, x_ref, g_ref = refs[:4]
        dxin_ref = refs[4] if has_in else None
        rest = refs[n_in:]
        h_in, (dx_ref, dg_ref), h_out, h_sems = rest[:nh], rest[nh:nh + 2], rest[nh + 2:nh + 2 + nho], rest[nh + 2 + nho:]
        _run_hook(hook, "before", h_in, h_out, h_sems)

        @pl.when(pl.program_id(0) == 0)
        def _():
            dg_ref[...] = jnp.zeros_like(dg_ref)

        dh = _nt(dy_ref[:, 0:C], w_ref[0])
        for j in range(1, J):
            dh += _nt(dy_ref[:, j * C:(j + 1) * C], w_ref[j])
        rstd, xhat = _rms_stats(x_ref[...])
        dx, dgc = _rms_bwd(xhat, rstd, g_ref[...], dh)
        dx_ref[...] = dx + dxin_ref[...] if has_in else dx
        dg_ref[...] += jnp.sum(dgc, axis=0, keepdims=True)
        _run_hook(hook, "after", h_in, h_out, h_sems)

    row = lambda n: pl.BlockSpec((tm, n), lambda i: (i, 0))
    ins = [dy, w, x, g] + ([dx_in] if has_in else [])
    return _hooked_call(
        body, hook, n_in=n_in, n_out=2, grid=(S // tm,), name=name,
        in_specs=[row(J * C), _full(w.shape), row(D), _full((1, D))] + ([row(D)] if has_in else []),
        out_specs=[row(D), _acc((1, D))],
        out_shape=[jax.ShapeDtypeStruct((S, D), F32), jax.ShapeDtypeStruct((1, D), F32)],
        compiler_params=_params("arbitrary"),
    )(*ins, *hook.ins)


def xattn_bwd(dx2, xq, o, lse, kv, w_xo, *, tm):
    S, D = xq.shape
    M = kv.shape[0]
    dh = XATT_HEAD_DIM

    def body(dx_ref, q_ref, o_ref, lse_ref, kv_ref, wo_ref, dq_ref, dkv_ref):
        @pl.when(pl.program_id(0) == 0)
        def _():
            dkv_ref[...] = jnp.zeros_like(dkv_ref)

        do = _nt(dx_ref[...].astype(BF16), wo_ref[...])
        for h in range(XATT_HEADS):
            hs = slice(h * dh, (h + 1) * dh)
            vs = slice(D + h * dh, D + (h + 1) * dh)
            q, k, v = q_ref[:, hs], kv_ref[:, hs], kv_ref[:, vs]
            do_h = do[:, hs]
            do_b = do_h.astype(BF16)
            p = jnp.exp(_nt(q, k) - lse_ref[:, h:h + 1])
            delta = jnp.sum(do_h * o_ref[:, hs].astype(F32), axis=-1, keepdims=True)
            ds = (p * (_nt(do_b, v) - delta)).astype(BF16)
            dq_ref[:, hs] = (_nn(ds, k) * (dh ** -0.5)).astype(BF16)
            dkv_ref[:, hs] += _tn(ds, q)
            dkv_ref[:, vs] += _tn(p.astype(BF16), do_b)

    row = lambda n: pl.BlockSpec((tm, n), lambda i: (i, 0))
    return pl.pallas_call(
        body, grid=(S // tm,), name="xattn_bwd",
        in_specs=[row(D), row(D), row(D), row(XATT_HEADS), _full(kv.shape), _full(w_xo.shape)],
        out_specs=[row(D), _acc((M, 2 * D))],
        out_shape=[jax.ShapeDtypeStruct((S, D), BF16), jax.ShapeDtypeStruct((M, 2 * D), F32)],
        compiler_params=_params("arbitrary"),
    )(dx2, xq, o, lse, kv, w_xo)


def merge_bwd(dx1, w_out, a, b, z, w_ba, w_bs, *, tm):
    S, D = dx1.shape
    J, W, C = w_ba.shape

    def body(dx_ref, wo_ref, a_ref, b_ref, g0, g1, g2, g3, wba_ref, wbs_ref, da_ref, db_ref, dg_ref, dya_ref, dys_ref):
        dm = _nt(dx_ref[...].astype(BF16), wo_ref[...])
        sa = _sigmoid(jnp.concatenate([g0[...], g1[...]], axis=1).astype(F32))
        sb = _sigmoid(jnp.concatenate([g2[...], g3[...]], axis=1).astype(F32))
        dg_ref[:, :D] = (dm * a_ref[...].astype(F32) * (sa * (1.0 - sa))).astype(BF16)
        dg_ref[:, D:] = (dm * b_ref[...].astype(F32) * (sb * (1.0 - sb))).astype(BF16)
        da = (dm * sa).astype(BF16)
        db = (dm * sb).astype(BF16)
        da_ref[...] = da
        db_ref[...] = db
        dya = _nt(da[:, 0:C], wba_ref[0])
        dys = _nt(db[:, 0:C], wbs_ref[0])
        for j in range(1, J):
            dya += _nt(da[:, j * C:(j + 1) * C], wba_ref[j])
            dys += _nt(db[:, j * C:(j + 1) * C], wbs_ref[j])
        dya_ref[...] = dya.astype(BF16)
        dys_ref[...] = dys.astype(BF16)

    row = lambda n: pl.BlockSpec((tm, n), lambda i: (i, 0))
    return pl.pallas_call(
        body, grid=(S // tm,), name="merge_bwd",
        in_specs=[row(D), _full(w_out.shape), row(D), row(D)] + _gate_specs(tm) + [_full(w_ba.shape), _full(w_bs.shape)],
        out_specs=[row(D), row(D), row(2 * D), row(W), row(W)],
        out_shape=[jax.ShapeDtypeStruct((S, D), BF16)] * 2 + [jax.ShapeDtypeStruct((S, 2 * D), BF16)]
        + [jax.ShapeDtypeStruct((S, W), BF16)] * 2,
        compiler_params=_params("parallel"),
    )(dx1, w_out, a, b, z, z, z, z, w_ba, w_bs)


def attn_bwd(z, y, dy, lse, bias, hook=Hook()):
    S = z.shape[0]
    nt = S // ATT_TILE
    back = ATT_WIN - 1
    n_in, nh, nho = 5 + 2 * ATT_WIN, len(hook.ins), len(hook.out_shapes)

    def body(q_ref, *refs):
        k_refs, v_refs = refs[:ATT_WIN], refs[ATT_WIN:2 * ATT_WIN]
        y_ref, dy_ref, lse_ref, bias_ref = refs[2 * ATT_WIN:2 * ATT_WIN + 4]
        rest = refs[2 * ATT_WIN + 4:]
        h_in, (dq_ref, dk_ref, dv_ref, dbias_ref), h_out = rest[:nh], rest[nh:nh + 4], rest[nh + 4:nh + 4 + nho]
        dk_acc, dv_acc = rest[nh + 4 + nho:nh + 6 + nho]
        h_sems = rest[nh + 6 + nho:]
        t = pl.program_id(0)
        _run_hook(hook, "before", h_in, h_out, h_sems)

        @pl.when(t == 0)
        def _():
            dbias_ref[...] = jnp.zeros_like(dbias_ref)
            dk_acc[...] = jnp.zeros_like(dk_acc)
            dv_acc[...] = jnp.zeros_like(dv_acc)

        @pl.when(t < nt)
        def _():
            valid = _att_valid(t)
            q_t = (q_ref[...] * ATT_SCALE).astype(F32).T.astype(BF16)
            do_t = dy_ref[...].astype(F32).T.astype(BF16)
            for h in range(ATT_HEADS):
                sl = slice(h * ATT_HEAD_DIM, (h + 1) * ATT_HEAD_DIM)
                k = jnp.concatenate([r[:, sl] for r in k_refs], axis=0)
                v = jnp.concatenate([r[:, sl] for r in v_refs], axis=0)
                q, do_b = q_ref[:, sl] * ATT_SCALE, dy_ref[:, sl]
                p = jnp.exp(_att_scores(q, k, bias_ref[h], valid) - lse_ref[:, h:h + 1])
                delta = jnp.sum(do_b.astype(F32) * y_ref[:, sl].astype(F32), axis=-1, keepdims=True)
                ds = p * (_nt(do_b, v) - delta)
                dbias_ref[h] += ds
                ds_b = ds.astype(BF16)
                dq_ref[:, sl] = (_nn(ds_b, k) * ATT_SCALE).astype(BF16)
                dk_w = _nn(q_t[sl, :], ds_b)
                dv_w = _nn(do_t[sl, :], p.astype(BF16))
                for i in range(ATT_WIN):
                    slot = (t + 1 + i) % ATT_WIN
                    cols = slice(i * ATT_TILE, (i + 1) * ATT_TILE)
                    if i == back:
                        dk_acc[slot, sl, :] = dk_w[:, cols]
                        dv_acc[slot, sl, :] = dv_w[:, cols]
                    else:
                        dk_acc[slot, sl, :] += dk_w[:, cols]
                        dv_acc[slot, sl, :] += dv_w[:, cols]

        done = (t + 1) % ATT_WIN
        dk_ref[...] = dk_acc[done].T.astype(BF16)
        dv_ref[...] = dv_acc[done].T.astype(BF16)
        _run_hook(hook, "after", h_in, h_out, h_sems)

    last = nt - 1
    tile = lambda col, n=512: pl.BlockSpec((ATT_TILE, n), lambda t: (jnp.minimum(t, last), col))
    late = pl.BlockSpec((ATT_TILE, 512), lambda t: (jnp.maximum(t - back, 0), 0))
    return _hooked_call(
        body, hook, n_in=n_in, n_out=4, grid=(nt + back,), name="attn_bwd",
        in_specs=[tile(0)] + _att_window_specs(nt, 1) + _att_window_specs(nt, 2)
        + [tile(0), tile(0), tile(0, ATT_HEADS), _full(bias.shape)],
        out_specs=[tile(0), late, late, _acc(bias.shape)],
        out_shape=[jax.ShapeDtypeStruct((S, 512), BF16)] * 3 + [jax.ShapeDtypeStruct(bias.shape, F32)],
        scratch_shapes=[pltpu.VMEM((ATT_WIN, 512, ATT_TILE), F32)] * 2,
        compiler_params=_params("arbitrary"),
    )(z, *([z] * (2 * ATT_WIN)), y, dy, lse, bias, *hook.ins)


def sgu_bwd(z, dy, amean, lng, lnb, wcat, wtcat, bfull, maskcat, *, tm):
    S = z.shape[0]
    n = S // tm

    def body(zu_ref, zv_ref, dy_ref, a_ref, lng_ref, lnb_ref, wcat_ref, wtcat_ref, bfull_ref, mask_ref,
             duv_ref, dw_ref, dsgb_ref, dlng_ref, dlnb_ref, vbd_ref, dbd_ref, w_acc, b_acc, g_acc, s_acc):
        i = pl.program_id(0)

        @pl.when(i == 0)
        def _():
            for r in (w_acc, b_acc, g_acc, s_acc):
                r[...] = jnp.zeros_like(r)

        def block(b, carry):
            rows = _sgu_rows(b)
            zu, zv, u, xhat, rstd, sv = _sgu_block(zu_ref, zv_ref, rows, a_ref, lng_ref, lnb_ref, wcat_ref, bfull_ref, vbd_ref)
            dyv = dy_ref[rows, :].astype(F32)
            duv_ref[rows, 0:512] = (dyv * sv * _gelu_grad(zu)).astype(BF16)
            dsv = dyv * u
            b_acc[...] += dsv
            w_acc[...] += _nt(dsv.astype(BF16), vbd_ref[...])
            _block_diag(dsv, dbd_ref)
            dvn = _nn(wtcat_ref[...], dbd_ref[...])
            g_acc[...] += dvn * xhat
            s_acc[...] += dvn
            dxh = dvn * lng_ref[...]
            dv = rstd * (dxh - _group_mean(dxh, a_ref) - xhat * _group_mean(dxh * xhat, a_ref))
            duv_ref[rows, 512:1024] = (dv * _gelu_grad(zv)).astype(BF16)
            return carry

        lax.fori_loop(0, tm // SG_BLOCK, block, 0)

        @pl.when(i == n - 1)
        def _():
            dw_ref[...] = w_acc[...] * mask_ref[...]
            dsgb_ref[...] = _group_mean(b_acc[...], a_ref) * float(SG_GROUP_DIM)
            dlng_ref[...] = jnp.sum(g_acc[...], axis=0, keepdims=True)
            dlnb_ref[...] = jnp.sum(s_acc[...], axis=0, keepdims=True)

    tile = lambda col: pl.BlockSpec((tm, 512), lambda i: (i, col))
    smalls = [amean, lng, lnb, wcat, wtcat, bfull, maskcat]
    wide = SG_GROUPS * SG_BLOCK
    return pl.pallas_call(
        body, grid=(n,), name="sgu_bwd",
        in_specs=[tile(3), tile(4), tile(0)] + [_full(a.shape) for a in smalls],
        out_specs=[pl.BlockSpec((tm, 1024), lambda i: (i, 0)), _acc((SG_BLOCK, wide)), _acc((SG_BLOCK, 512)),
                   _acc((1, 512)), _acc((1, 512))],
        out_shape=[jax.ShapeDtypeStruct((S, 1024), BF16), jax.ShapeDtypeStruct((SG_BLOCK, wide), F32),
                   jax.ShapeDtypeStruct((SG_BLOCK, 512), F32), jax.ShapeDtypeStruct((1, 512), F32),
                   jax.ShapeDtypeStruct((1, 512), F32)],
        scratch_shapes=[pltpu.VMEM((wide, 512), BF16), pltpu.VMEM((wide, 512), BF16), pltpu.VMEM((SG_BLOCK, wide), F32),
                        pltpu.VMEM((SG_BLOCK, 512), F32), pltpu.VMEM((SG_BLOCK, 512), F32), pltpu.VMEM((SG_BLOCK, 512), F32)],
        compiler_params=_params("arbitrary"),
    )(z, z, dy, *smalls)


def bias_colsum(p):
    H, _, L = p.shape
    near_lo = (ATT_WIN - 1) * ATT_TILE - REL_CLIP + 1
    near_hi = ATT_WIN * ATT_TILE

    def body(p_ref, gw_ref, far_ref):
        k = lax.broadcasted_iota(jnp.int32, (1, L), 1)
        is_far = (k < near_lo) | (k >= near_hi)
        for h in range(H):
            g = jnp.sum(p_ref[h], axis=0, keepdims=True)
            gw_ref[h:h + 1, :] = g
            far_ref[h:h + 1, :] = jnp.zeros((1, 128), F32) + jnp.sum(jnp.where(is_far, g, 0.0))

    return pl.pallas_call(
        body, name="bias_colsum", in_specs=[_full(p.shape)], out_specs=[_acc((H, L)), _acc((H, 128))],
        out_shape=[jax.ShapeDtypeStruct((H, L), F32), jax.ShapeDtypeStruct((H, 128), F32)], grid=(1,),
        compiler_params=_params("arbitrary"),
    )(p)


def _row_tile(rows, cols):
    best = None
    for tr in range(16, rows + 1, 16):
        if rows % tr == 0 and tr * cols * 4 <= EW_BLOCK_BYTES:
            best = tr
    return best if best is not None else rows


def place_shard(w, own, name):
    R, C = w.shape
    r = R // 2
    tr = _row_tile(r, C)
    nr = r // tr

    def body(own_ref, w_ref, o_ref):
        o_ref[...] = w_ref[...].astype(BF16)

    return pl.pallas_call(
        body, name=name,
        grid_spec=pltpu.PrefetchScalarGridSpec(
            num_scalar_prefetch=1, grid=(2, nr),
            in_specs=[pl.BlockSpec((tr, C), lambda h, i, own: (h * nr + i, 0))],
            out_specs=pl.BlockSpec((None, None, tr, C), lambda h, i, own: (own[0], h, i, 0))),
        out_shape=jax.ShapeDtypeStruct((4, 2, r, C), BF16), compiler_params=_params("parallel", "parallel"),
    )(own, w)


def chip_partial(grad, recv, core, name):
    _, _, r, C = grad.shape
    tr = _row_tile(r, C)

    def body(core_ref, g_ref, r_ref, o_ref):
        o_ref[...] = (g_ref[...].astype(F32) + r_ref[...].astype(F32)).astype(BF16)

    spec = pl.BlockSpec((None, tr, C), lambda j, i, core: (j, i, 0))
    return pl.pallas_call(
        body, name=name,
        grid_spec=pltpu.PrefetchScalarGridSpec(
            num_scalar_prefetch=1, grid=(4, r // tr),
            in_specs=[pl.BlockSpec((None, None, tr, C), lambda j, i, core: (j, core[0], i, 0)), spec], out_specs=spec),
        out_shape=jax.ShapeDtypeStruct((4, r, C), BF16), compiler_params=_params("parallel", "parallel"),
    )(core, grad, recv)


def shard_sum(part, recv, own, core, name):
    _, r, C = part.shape
    tr = _row_tile(r, C)

    def body(own_ref, core_ref, p_ref, r0, r1, r2, o_ref):
        o_ref[...] = p_ref[...].astype(F32) + r0[...].astype(F32) + r1[...].astype(F32) + r2[...].astype(F32)

    return pl.pallas_call(
        body, name=name,
        grid_spec=pltpu.PrefetchScalarGridSpec(
            num_scalar_prefetch=2, grid=(r // tr,),
            in_specs=[pl.BlockSpec((None, tr, C), lambda i, own, core: (own[0], i, 0))]
            + [pl.BlockSpec((None, tr, C), lambda i, own, core, k=k: (k, i, 0)) for k in range(3)],
            out_specs=pl.BlockSpec((None, tr, C), lambda i, own, core: (core[0], i, 0))),
        out_shape=jax.ShapeDtypeStruct((2, r, C), F32), compiler_params=_params("parallel"),
    )(own, core, part, recv, recv, recv)


def adamw(w, g, m, v, name):
    R, C = w.shape
    tr = _row_tile(R, C)

    def body(w_ref, g_ref, m_ref, v_ref, d_ref, nm_ref, nv_ref):
        gv = g_ref[...]
        nm = ADAM_B1 * m_ref[...] + (1.0 - ADAM_B1) * gv
        nv = ADAM_B2 * v_ref[...] + (1.0 - ADAM_B2) * (gv * gv)
        m_hat = nm / (1.0 - ADAM_B1 ** ADAM_STEP)
        v_hat = nv / (1.0 - ADAM_B2 ** ADAM_STEP)
        d_ref[...] = -ADAM_LR * (m_hat / (jnp.sqrt(v_hat) + ADAM_EPS) + ADAM_WD * w_ref[...])
        nm_ref[...] = nm
        nv_ref[...] = nv

    spec = pl.BlockSpec((tr, C), lambda i: (i, 0))
    return pl.pallas_call(
        body, grid=(R // tr,), name=name, in_specs=[spec] * 4, out_specs=[spec] * 3,
        out_shape=[jax.ShapeDtypeStruct((R, C), F32)] * 3, compiler_params=_params("parallel"),
    )(w, g, m, v)


HBM = pl.BlockSpec(memory_space=pl.ANY)


def _place():
    x, y, c = lax.axis_index("x"), lax.axis_index("y"), lax.axis_index("c")
    chips = [(1 - x, y), (x, 1 - y), (1 - x, 1 - y)]
    return x, y, c, chips


def _gather_phases(n):
    def copies(buf, sems, kind):
        send_sems, recv_sems = sems
        x, y, c, chips = _place()

        def remote(w, k, slot, to):
            return pltpu.make_async_remote_copy(src_ref=slot, dst_ref=slot, send_sem=send_sems.at[w, k],
                                                recv_sem=recv_sems.at[w, k], device_id=to, device_id_type=MESH)

        def one(w, k, px, py):
            if kind == "mine":
                return remote(w, k, buf[w].at[2 * x + y, c], (px, py, c))
            if kind == "theirs":
                return remote(w, k, buf[w].at[2 * px + py, c], (px, py, c))
            if kind == "onward":
                return remote(w, 3 + k, buf[w].at[2 * px + py, c], (x, y, 1 - c))
            return remote(w, 3 + k, buf[w].at[2 * px + py, 1 - c], (x, y, 1 - c))

        return [one(w, k, px, py) for w in range(n) for k, (px, py) in enumerate(chips)]

    def start(_, buf, sems):
        for cp in copies(buf, sems, "mine"):
            cp.start()

    def relay(_, buf, sems):
        for theirs, onward in zip(copies(buf, sems, "theirs"), copies(buf, sems, "onward")):
            theirs.wait_recv()
            onward.start()

    def finish(_, buf, sems):
        for cp in copies(buf, sems, "relayed"):
            cp.wait_recv()
        for cp in copies(buf, sems, "mine") + copies(buf, sems, "onward"):
            cp.wait_send()

    return start, relay, finish


def _gather_sems(n):
    return (pltpu.SemaphoreType.DMA((n, 6)), pltpu.SemaphoreType.DMA((n, 6)))


def gather_weights(bufs):
    n = len(bufs)
    phases = _gather_phases(n)

    def body(*refs):
        for phase in phases:
            phase(None, refs[n:2 * n], refs[2 * n:])

    return pl.pallas_call(
        body, name="gather_weights", in_specs=[HBM] * n, out_specs=[HBM] * n,
        out_shape=[jax.ShapeDtypeStruct(b.shape, b.dtype) for b in bufs],
        input_output_aliases={i: i for i in range(n)}, scratch_shapes=list(_gather_sems(n)),
    )(*bufs)


def gather_hook(bufs, n_steps):
    start, relay, finish = _gather_phases(len(bufs))
    return Hook(ins=tuple(bufs), out_shapes=tuple(jax.ShapeDtypeStruct(b.shape, b.dtype) for b in bufs), aliased=True,
                sems=_gather_sems(len(bufs)),
                steps=((0, "before", start), ((n_steps * 13) // 16, "before", relay), (n_steps - 1, "after", finish)))


def sibling_split(grads, name):
    n = len(grads)

    def body(*refs):
        src, dst = refs[:n], refs[n:2 * n]
        send_sems, recv_sems = refs[2 * n:]
        x, y, c, _ = _place()
        sends = [pltpu.make_async_remote_copy(src_ref=src[w].at[:, 1 - c], dst_ref=dst[w], send_sem=send_sems.at[w],
                                              recv_sem=recv_sems.at[w], device_id=(x, y, 1 - c), device_id_type=MESH)
                 for w in range(n)]
        for cp in sends:
            cp.start()
        for cp in sends:
            cp.wait()

    return pl.pallas_call(
        body, name=name, in_specs=[HBM] * n, out_specs=[HBM] * n,
        out_shape=[jax.ShapeDtypeStruct((4,) + g.shape[2:], g.dtype) for g in grads],
        scratch_shapes=[pltpu.SemaphoreType.DMA((n,))] * 2,
    )(*grads)


def exchange_hook(parts, n_steps):
    n = len(parts)

    def copies(src, dst, sems):
        send_sems, recv_sems = sems
        _, _, c, chips = _place()
        return [pltpu.make_async_remote_copy(
            src_ref=src[w].at[2 * px + py], dst_ref=dst[w].at[k], send_sem=send_sems.at[w, k],
            recv_sem=recv_sems.at[w, k], device_id=(px, py, c), device_id_type=MESH)
            for w in range(n) for k, (px, py) in enumerate(chips)]

    def start(src, dst, sems):
        for cp in copies(src, dst, sems):
            cp.start()

    def finish(src, dst, sems):
        for cp in copies(src, dst, sems):
            cp.wait()

    return Hook(ins=tuple(parts), out_shapes=tuple(jax.ShapeDtypeStruct((3,) + p.shape[1:], p.dtype) for p in parts),
                sems=(pltpu.SemaphoreType.DMA((n, 3)), pltpu.SemaphoreType.DMA((n, 3))),
                steps=((0, "before", start), (n_steps - 1, "after", finish)))


def sibling_join(sums):
    n = len(sums)

    def body(*refs):
        buf = refs[n:2 * n]
        send_sems, recv_sems = refs[2 * n:]
        x, y, c, _ = _place()
        sends = [pltpu.make_async_remote_copy(src_ref=buf[w].at[c], dst_ref=buf[w].at[c], send_sem=send_sems.at[w],
                                              recv_sem=recv_sems.at[w], device_id=(x, y, 1 - c), device_id_type=MESH)
                 for w in range(n)]
        for cp in sends:
            cp.start()
        for w, cp in enumerate(sends):
            cp.wait_send()
            pltpu.make_async_remote_copy(src_ref=buf[w].at[c], dst_ref=buf[w].at[1 - c], send_sem=send_sems.at[w],
                                         recv_sem=recv_sems.at[w], device_id=(x, y, 1 - c), device_id_type=MESH).wait_recv()

    return pl.pallas_call(
        body, name="sibling_join", in_specs=[HBM] * n, out_specs=[HBM] * n,
        out_shape=[jax.ShapeDtypeStruct(s.shape, s.dtype) for s in sums],
        input_output_aliases={i: i for i in range(n)},
        scratch_shapes=[pltpu.SemaphoreType.DMA((n,))] * 2,
    )(*sums)


def allreduce_small(v):
    R, C = v.shape

    def body(v_ref, out_ref, all_ref, send_sems, recv_sems):
        x, y, c, chips = _place()
        me, sibling = (x, y, c), (x, y, 1 - c)

        def slot(px, py, pc):
            return all_ref.at[4 * px + 2 * py + pc]

        def copy(k, block, to, src=None):
            return pltpu.make_async_remote_copy(src_ref=slot(*block) if src is None else src, dst_ref=slot(*block),
                                                send_sem=send_sems.at[k], recv_sem=recv_sems.at[k], device_id=to,
                                                device_id_type=MESH)

        first = [copy(0, me, sibling, src=v_ref)] + [copy(1 + j, me, (*chip, c), src=v_ref) for j, chip in enumerate(chips)]
        for cp in first:
            cp.start()
        slot(*me)[...] = v_ref[...]
        passed = [copy(4 + j, (*chip, c), sibling) for j, chip in enumerate(chips)]
        for j, chip in enumerate(chips):
            copy(1 + j, (*chip, c), me).wait_recv()
            passed[j].start()
        copy(0, sibling, me).wait_recv()
        for j, chip in enumerate(chips):
            copy(4 + j, (*chip, 1 - c), me).wait_recv()
        for cp in first + passed:
            cp.wait_send()
        acc = all_ref[0]
        for k in range(1, 8):
            acc = acc + all_ref[k]
        out_ref[...] = acc

    vmem = pl.BlockSpec(memory_space=pltpu.VMEM)
    return pl.pallas_call(
        body, name="allreduce_small", in_specs=[vmem], out_specs=vmem, out_shape=jax.ShapeDtypeStruct((R, C), F32),
        scratch_shapes=[pltpu.VMEM((8, R, C), F32), pltpu.SemaphoreType.DMA((7,)), pltpu.SemaphoreType.DMA((7,))],
        compiler_params=pltpu.CompilerParams(vmem_limit_bytes=VMEM_LIMIT_V7X),
    )(v)


SMALL_PAD = 1024


def _pack_small(arrs):
    parts = []
    for a in arrs:
        f = a.reshape(-1).astype(F32)
        parts.append(jnp.pad(f, (0, (-f.shape[0]) % SMALL_PAD)))
    return jnp.concatenate(parts).reshape(-1, 128)


def _unpack_small(packed, shapes):
    flat = packed.reshape(-1)
    outs, off = [], 0
    for s in shapes:
        size = int(np.prod(s))
        outs.append(flat[off:off + size].reshape(s))
        off += size + (-size) % SMALL_PAD
    return outs


ATT_KEYS = ATT_WIN * ATT_TILE
ATT_NEAR_LO = (ATT_WIN - 1) * ATT_TILE - REL_CLIP + 1
ATT_PERIOD = ATT_KEYS + ATT_TILE + 1


def _att_bias_table(rel_bias):
    far = rel_bias[:, 2 * REL_CLIP:]
    near = rel_bias[:, 2 * REL_CLIP - 1:0:-1]
    w = jnp.concatenate([jnp.broadcast_to(far, (ATT_HEADS, ATT_NEAR_LO)), near,
                         jnp.broadcast_to(far, (ATT_HEADS, ATT_PERIOD - ATT_KEYS))], axis=1)
    rows = jnp.tile(w, (1, ATT_TILE))[:, :ATT_TILE * (ATT_PERIOD - 1)].reshape(ATT_HEADS, ATT_TILE, ATT_PERIOD - 1)
    i = np.arange(ATT_TILE)[:, None]
    m = np.arange(ATT_KEYS)[None, :]
    qc, kc = i // CHUNK, m // CHUNK
    band = (kc >= qc) & (kc <= qc + N_PREV_CHUNKS)
    return jnp.where(band[None], rows[:, :, :ATT_KEYS], NEG_INF)


def _rel_bias_grad(dbias):
    p = jnp.pad(dbias, ((0, 0), (0, 0), (0, ATT_PERIOD - 1 - ATT_KEYS))).reshape(ATT_HEADS, -1)
    p = jnp.pad(p, ((0, 0), (0, ATT_TILE))).reshape(ATT_HEADS, ATT_TILE, ATT_PERIOD)
    gw, far = bias_colsum(p)
    return jnp.concatenate([jnp.zeros((ATT_HEADS, 1), F32), gw[:, ATT_KEYS - 1:ATT_NEAR_LO - 1:-1], far[:, :1]], axis=1)


def kernel(x, mem, norm_mix_g, w_in, rel_bias, sg_ln_g, sg_ln_b, sg_w, sg_b, w_branch_att, w_branch_sg, w_out, norm_xattn_g, norm_mem_g, w_xq, w_xkv, w_xo, norm_ffn_g, w_ffn_in, w_ffn_out, norm_final_g, loss_target, m_norm_mix_g, m_w_in, m_rel_bias, m_sg_ln_g, m_sg_ln_b, m_sg_w, m_sg_b, m_w_branch_att, m_w_branch_sg, m_w_out, m_norm_xattn_g, m_norm_mem_g, m_w_xq, m_w_xkv, m_w_xo, m_norm_ffn_g, m_w_ffn_in, m_w_ffn_out, m_norm_final_g, v_norm_mix_g, v_w_in, v_rel_bias, v_sg_ln_g, v_sg_ln_b, v_sg_w, v_sg_b, v_w_branch_att, v_w_branch_sg, v_w_out, v_norm_xattn_g, v_norm_mem_g, v_w_xq, v_w_xkv, v_w_xo, v_norm_ffn_g, v_w_ffn_in, v_w_ffn_out, v_norm_final_g):
    S, D = x.shape[1], x.shape[2]
    x2d, mem2d, tgt = x[0], mem[0], loss_target[0]

    big_names = ["w_in", "w_branch_att", "w_branch_sg", "w_out", "w_xq", "w_xkv", "w_xo", "w_ffn_in", "w_ffn_out"]
    col_sharded = [True, True, True, False, False, True, False, True, False]
    big_w = [a[0] for a in (w_in, w_branch_att, w_branch_sg, w_out, w_xq, w_xkv, w_xo, w_ffn_in, w_ffn_out)]
    big_m = [a[0] for a in (m_w_in, m_w_branch_att, m_w_branch_sg, m_w_out, m_w_xq, m_w_xkv, m_w_xo, m_w_ffn_in, m_w_ffn_out)]
    big_v = [a[0] for a in (v_w_in, v_w_branch_att, v_w_branch_sg, v_w_out, v_w_xq, v_w_xkv, v_w_xo, v_w_ffn_in, v_w_ffn_out)]

    own = (2 * lax.axis_index("x") + lax.axis_index("y")).astype(jnp.int32).reshape(1)
    core = lax.axis_index("c").astype(jnp.int32).reshape(1)
    placed = [place_shard(w, own, "place_" + nm) for w, nm in zip(big_w, big_names)]

    def whole(g4, i):
        R, C = big_w[i].shape
        return g4.reshape(4, R, C) if col_sharded[i] else g4.reshape(4 * R, C)

    W_in = whole(gather_weights(placed[:1])[0], 0)

    g_mix, g_xat, g_mem, g_ffn = norm_mix_g, norm_xattn_g, norm_mem_g, norm_ffn_g
    g_fin = norm_final_g.reshape(1, D)
    bias = _att_bias_table(rel_bias[0])
    tt = np.arange(SG_BLOCK)
    sg_mask = (tt[None, :] // CHUNK) <= (tt[:, None] // CHUNK)
    wm_f = jnp.where(sg_mask[None], sg_w[0], 0.0)
    wide = SG_GROUPS * SG_BLOCK
    wcat = jnp.transpose(wm_f, (1, 0, 2)).reshape(SG_BLOCK, wide).astype(BF16)
    wtcat = jnp.transpose(wm_f, (2, 0, 1)).reshape(SG_BLOCK, wide).astype(BF16)
    maskcat = jnp.asarray(np.tile(sg_mask, (1, SG_GROUPS)), F32)
    bfull = jnp.repeat(sg_b[0].T, SG_GROUP_DIM, axis=1)
    lng, lnb = sg_ln_g[0].reshape(1, 512), sg_ln_b[0].reshape(1, 512)
    gid = np.arange(512) // SG_GROUP_DIM
    amean = jnp.asarray((gid[:, None] == gid[None, :]) / SG_GROUP_DIM, BF16)

    h1, z, *early = norm_mm(x2d, g_mix, W_in, tm=ROW_TILE, scale=None, name="norm_mm_in",
                            hook=gather_hook(placed[1:7], S // ROW_TILE))
    y_att, lse_att, *late = attn_fwd(z, bias, gather_hook(placed[7:], S // ATT_TILE))
    W_ba, W_bs, W_out, W_xq, W_xkv, W_xo, W_fi, W_fo = [whole(g4, i + 1) for i, g4 in enumerate(early + late)]
    y_sg = sgu_fwd(z, amean, lng, lnb, wcat, bfull, tm=512)
    a_br, b_br, merged, x1 = merge_fwd(y_att, y_sg, z, x2d, W_ba, W_bs, W_out, tm=ROW_TILE)
    h2, xq = norm_mm(x1, g_xat, W_xq.reshape(1, D, D), tm=512, scale=XATT_HEAD_DIM ** -0.5, name="norm_mm_xq")
    mn, kv = norm_mm(mem2d, g_mem, W_xkv, tm=mem2d.shape[0], scale=None, name="norm_mm_kv")
    o_x, lse_x, x2 = xattn_fwd(xq, kv, W_xo, x1, tm=512)
    h3, gu = norm_mm(x2, g_ffn, W_fi, tm=ROW_TILE, scale=None, name="norm_mm_ffn")
    act, x3 = ffn_out_fwd(gu, W_fo, x2, tm=ROW_TILE)
    loss_vec, dx3, dg_fin = loss_head(x3, g_fin, tgt, tm=512)

    dgu = ffn_act_bwd(dx3, gu, W_fo, tm=ROW_TILE, nchunk=2)
    gW_fo = tn_mm(act, dx3, shards=1, tn=D, tk=DW_TOKENS, name="dw_ffn_out")
    gW_fi = tn_mm(h3, dgu, shards=4, tn=2 * W_fi.shape[2], tk=DW_TOKENS, name="dw_ffn_in")
    dx2, dg_ffn = mm_nt_norm_bwd(dgu, W_fi, x2, g_ffn, dx3, tm=ROW_TILE, name="dx_ffn")
    dq_x, dkv = xattn_bwd(dx2, xq, o_x, lse_x, kv, W_xo, tm=512)
    gW_xo = tn_mm(o_x, dx2, shards=1, tn=D, tk=DW_TOKENS, name="dw_xo")
    gW_xq = tn_mm(h2, dq_x, shards=1, tn=D, tk=DW_TOKENS, name="dw_xq")
    dx1, dg_xat = mm_nt_norm_bwd(dq_x, W_xq.reshape(1, D, D), x1, g_xat, dx2, tm=512, name="dx_xq")
    gW_xkv = tn_mm(mn, dkv, shards=4, tn=2 * D, tk=mem2d.shape[0], name="dw_xkv")
    _, dg_mem = mm_nt_norm_bwd(dkv.astype(BF16), W_xkv, mem2d, g_mem, None, tm=mem2d.shape[0], name="dx_mem")
    da, db, dgab, dy_att, dy_sg = merge_bwd(dx1, W_out, a_br, b_br, z, W_ba, W_bs, tm=ROW_TILE)
    gW_out = tn_mm(merged, dx1, shards=1, tn=D, tk=DW_TOKENS, name="dw_out")
    gW_ba = tn_mm(y_att, da, shards=4, tn=D, tk=DW_TOKENS, name="dw_branch_att")
    gW_bs = tn_mm(y_sg, db, shards=4, tn=D, tk=DW_TOKENS, name="dw_branch_sg")

    def canon(g, i):
        R, C = big_w[i].shape
        return g.reshape(4, 2, R // 2, C)

    def stage1(grads, idx, tag):
        gs = [canon(g, i) for g, i in zip(grads, idx)]
        recv = sibling_split(gs, "sibling_split_" + tag)
        return [chip_partial(g, r, core, "chip_partial_" + big_names[i]) for g, r, i in zip(gs, recv, idx)]

    parts_b = stage1([gW_ba, gW_bs, gW_out, gW_xq, gW_xkv, gW_xo, gW_fi, gW_fo], range(1, 9), "late")
    nt_bwd = S // ATT_TILE + ATT_WIN - 1
    dq_a, dk_a, dv_a, dbias, *from_chips_b = attn_bwd(z, y_att, dy_att, lse_att, bias, exchange_hook(parts_b, nt_bwd))
    duv, dwcat, dsgb_full, dlng, dlnb = sgu_bwd(z, dy_sg, amean, lng, lnb, wcat, wtcat, bfull, maskcat, tm=512)
    dwm = jnp.transpose(dwcat.reshape(SG_BLOCK, SG_GROUPS, SG_BLOCK), (1, 0, 2))
    dsgb = dsgb_full[:, ::SG_GROUP_DIM].T
    dz = jnp.concatenate([dq_a, dk_a, dv_a, duv, dgab], axis=1)
    gW_in = tn_mm(h1, dz, shards=4, tn=2 * W_in.shape[2], tk=DW_TOKENS, name="dw_in")
    parts_a = stage1([gW_in], [0], "w_in")
    dx, dg_mix, *from_chips_a = mm_nt_norm_bwd(dz, W_in, x2d, g_mix, dx1, tm=ROW_TILE, name="dx_in",
                                               hook=exchange_hook(parts_a, S // ROW_TILE))
    d_rel = _rel_bias_grad(dbias)

    parts, from_chips = parts_a + parts_b, from_chips_a + from_chips_b
    sums = [shard_sum(p, r, own, core, "shard_sum_" + nm) for p, r, nm in zip(parts, from_chips, big_names)]
    joined = sibling_join(sums)
    big_out = []
    for j, w, m_, v_, nm in zip(joined, big_w, big_m, big_v, big_names):
        g = j.reshape(w.shape)
        big_out.append((g,) + tuple(adamw(w, g, m_, v_, "adamw_" + nm)))

    small_w = [norm_mix_g, rel_bias, sg_ln_g, sg_ln_b, sg_w, sg_b, norm_xattn_g, norm_mem_g, norm_ffn_g, norm_final_g]
    small_m = [m_norm_mix_g, m_rel_bias, m_sg_ln_g, m_sg_ln_b, m_sg_w, m_sg_b, m_norm_xattn_g, m_norm_mem_g, m_norm_ffn_g, m_norm_final_g]
    small_v = [v_norm_mix_g, v_rel_bias, v_sg_ln_g, v_sg_ln_b, v_sg_w, v_sg_b, v_norm_xattn_g, v_norm_mem_g, v_norm_ffn_g, v_norm_final_g]
    small_g = [dg_mix, d_rel, dlng, dlnb, dwm, dsgb, dg_xat, dg_mem, dg_ffn, dg_fin]
    shapes = [w.shape for w in small_w]
    g_sum = allreduce_small(_pack_small(small_g + [loss_vec[0, :1]]))
    zero = jnp.zeros((1,), F32)
    d_s, m_s, v_s = adamw(_pack_small(small_w + [zero]), g_sum, _pack_small(small_m + [zero]), _pack_small(small_v + [zero]),
                          "adamw_small")
    sg_, sd, sm, sv_ = (_unpack_small(p, shapes + [(1,)]) for p in (g_sum, d_s, m_s, v_s))
    loss = sg_[-1][0]

    order = ["norm_mix_g", "w_in", "rel_bias", "sg_ln_g", "sg_ln_b", "sg_w", "sg_b", "w_branch_att", "w_branch_sg", "w_out",
             "norm_xattn_g", "norm_mem_g", "w_xq", "w_xkv", "w_xo", "norm_ffn_g", "w_ffn_in", "w_ffn_out", "norm_final_g"]
    small_names = ["norm_mix_g", "rel_bias", "sg_ln_g", "sg_ln_b", "sg_w", "sg_b", "norm_xattn_g", "norm_mem_g", "norm_ffn_g",
                   "norm_final_g"]
    res = {}
    for i, nm in enumerate(small_names):
        res[nm] = (sg_[i], sd[i], sm[i], sv_[i])
    for nm, outs in zip(big_names, big_out):
        res[nm] = tuple(o[None] for o in outs)
    return (loss, dx[None], *[res[nm][0] for nm in order], *[res[nm][1] for nm in order],
            *[res[nm][2] for nm in order], *[res[nm][3] for nm in order])
```

```python
import functools
from typing import NamedTuple

import numpy as np
import jax
import jax.numpy as jnp
from jax import lax
from jax.experimental import pallas as pl
from jax.experimental.pallas import tpu as pltpu

F32, BF16 = jnp.float32, jnp.bfloat16
MESH = pl.DeviceIdType.MESH

EPS = 1e-6
NEG_INF = -1e30
CHUNK = 64
N_PREV_CHUNKS = 8
ATT_HEADS, ATT_HEAD_DIM = 8, 64
REL_CLIP = 128
SG_BLOCK, SG_GROUPS, SG_GROUP_DIM = 128, 8, 64
XATT_HEADS, XATT_HEAD_DIM = 4, 256
ATT_TILE = 128
ATT_WIN = 5
ADAM_LR, ADAM_B1, ADAM_B2, ADAM_EPS, ADAM_WD, ADAM_STEP = 0.001, 0.9, 0.999, 1e-08, 0.01, 10

VMEM_LIMIT_V7X = 56 * 1024 * 1024
EW_BLOCK_BYTES = 1 << 20
ROW_TILE = 512
DW_TOKENS = 1024
HBM = pl.BlockSpec(memory_space=pl.ANY)


def _params(*sem):
    return pltpu.CompilerParams(dimension_semantics=sem, vmem_limit_bytes=VMEM_LIMIT_V7X)


def _full(shape):
    n = len(shape)
    return pl.BlockSpec(shape, lambda *_: (0,) * n, pipeline_mode=pl.Buffered(1))


def _acc(shape):
    n = len(shape)
    return pl.BlockSpec(shape, lambda *_: (0,) * n)


class Hook(NamedTuple):
    ins: tuple = ()
    out_shapes: tuple = ()
    aliased: bool = False
    sems: tuple = ()
    steps: tuple = ()


def _run_hook(hook, pos, h_in, h_out, h_sems):
    for step, where, fn in hook.steps:
        if where == pos:
            pl.when(pl.program_id(0) == step)(functools.partial(fn, h_in, h_out, h_sems))


def _hooked_call(body, hook, *, n_in, n_out, in_specs, out_specs, out_shape, scratch_shapes=(), **kw):
    nh = len(hook.ins)
    aliases = {n_in + i: n_out + i for i in range(nh)} if hook.aliased else {}
    return pl.pallas_call(
        body, in_specs=list(in_specs) + [HBM] * nh, out_specs=list(out_specs) + [HBM] * len(hook.out_shapes),
        out_shape=list(out_shape) + list(hook.out_shapes), scratch_shapes=list(scratch_shapes) + list(hook.sems),
        input_output_aliases=aliases, **kw)


def _nt(a, b):
    return lax.dot_general(a, b, (((1,), (1,)), ((), ())), preferred_element_type=F32)


def _tn(a, b):
    return lax.dot_general(a, b, (((0,), (0,)), ((), ())), preferred_element_type=F32)


def _nn(a, b):
    return jnp.dot(a, b, preferred_element_type=F32)


def _sigmoid(x):
    return 1.0 / (1.0 + jnp.exp(-x))


_GELU_C = float(np.sqrt(2.0 / np.pi))


def _gelu(x):
    t = jnp.tanh(_GELU_C * (x + 0.044715 * (x * x * x)))
    return x * (0.5 * (1.0 + t))


def _gelu_grad(x):
    t = jnp.tanh(_GELU_C * (x + 0.044715 * (x * x * x)))
    return 0.5 * (1.0 + t) + 0.5 * x * (1.0 - t * t) * (_GELU_C * (1.0 + 3.0 * 0.044715 * x * x))


def _rms_stats(xf):
    rstd = lax.rsqrt(jnp.mean(xf * xf, axis=-1, keepdims=True) + EPS)
    return rstd, xf * rstd


def _rms_bwd(xhat, rstd, g, dh):
    dxh = dh * g
    dx = rstd * (dxh - xhat * jnp.mean(dxh * xhat, axis=-1, keepdims=True))
    return dx, dh * xhat


def norm_mm(x, g, w, *, tm, scale, name, hook=Hook()):
    S, D = x.shape
    J, _, C = w.shape
    nh, nho = len(hook.ins), len(hook.out_shapes)

    def body(x_ref, g_ref, w_ref, *rest):
        h_in, (h_ref, z_ref), h_out, h_sems = rest[:nh], rest[nh:nh + 2], rest[nh + 2:nh + 2 + nho], rest[nh + 2 + nho:]
        _run_hook(hook, "before", h_in, h_out, h_sems)
        _, xhat = _rms_stats(x_ref[...])
        h = (xhat * g_ref[...]).astype(BF16)
        h_ref[...] = h
        for j in range(J):
            acc = _nn(h, w_ref[j])
            if scale is not None:
                acc = acc * scale
            z_ref[:, j * C:(j + 1) * C] = acc.astype(BF16)
        _run_hook(hook, "after", h_in, h_out, h_sems)

    return _hooked_call(
        body, hook, n_in=3, n_out=2, grid=(S // tm,), name=name,
        in_specs=[pl.BlockSpec((tm, D), lambda i: (i, 0)), _full((1, D)), _full((J, D, C))],
        out_specs=[pl.BlockSpec((tm, D), lambda i: (i, 0)), pl.BlockSpec((tm, J * C), lambda i: (i, 0))],
        out_shape=[jax.ShapeDtypeStruct((S, D), BF16), jax.ShapeDtypeStruct((S, J * C), BF16)],
        compiler_params=_params("arbitrary"),
    )(x, g, w, *hook.ins)


def _att_window_specs(nt, col):
    return [pl.BlockSpec((ATT_TILE, 512), lambda t, j=j: (jnp.clip(t - (ATT_WIN - 1) + j, 0, nt - 1), col))
            for j in range(ATT_WIN)]


ATT_SCALE = ATT_HEAD_DIM ** -0.5


def _att_scores(q_scaled, k, bias, valid):
    return jnp.where(valid, _nt(q_scaled, k) + bias, NEG_INF)


def _att_valid(t):
    kpos = lax.broadcasted_iota(jnp.int32, (ATT_TILE, ATT_WIN * ATT_TILE), 1) + (t - (ATT_WIN - 1)) * ATT_TILE
    return kpos >= 0


def attn_fwd(z, bias, hook=Hook()):
    S = z.shape[0]
    nt = S // ATT_TILE
    n_in, nh, nho = 2 + 2 * ATT_WIN, len(hook.ins), len(hook.out_shapes)

    def body(q_ref, *refs):
        k_refs, v_refs = refs[:ATT_WIN], refs[ATT_WIN:2 * ATT_WIN]
        bias_ref = refs[2 * ATT_WIN]
        rest = refs[2 * ATT_WIN + 1:]
        h_in, (y_ref, lse_ref), h_out = rest[:nh], rest[nh:nh + 2], rest[nh + 2:nh + 2 + nho]
        s_scr, p_scr = rest[nh + 2 + nho:nh + 4 + nho]
        h_sems = rest[nh + 4 + nho:]
        _run_hook(hook, "before", h_in, h_out, h_sems)
        valid = _att_valid(pl.program_id(0))
        heads = [slice(h * ATT_HEAD_DIM, (h + 1) * ATT_HEAD_DIM) for h in range(ATT_HEADS)]
        for h, sl in enumerate(heads):
            k = jnp.concatenate([r[:, sl] for r in k_refs], axis=0)
            s_scr[h] = _att_scores(q_ref[:, sl] * ATT_SCALE, k, bias_ref[h], valid)
        stats = []
        for h in range(ATT_HEADS):
            s = s_scr[h]
            m = jnp.max(s, axis=-1, keepdims=True)
            p = jnp.exp(s - m)
            stats.append((m, jnp.sum(p, axis=-1, keepdims=True)))
            p_scr[h] = p.astype(BF16)
        for h, sl in enumerate(heads):
            m, l = stats[h]
            v = jnp.concatenate([r[:, sl] for r in v_refs], axis=0)
            y_ref[:, sl] = (_nn(p_scr[h], v) / l).astype(BF16)
            lse_ref[:, h:h + 1] = m + jnp.log(l)
        _run_hook(hook, "after", h_in, h_out, h_sems)

    tile = lambda col: pl.BlockSpec((ATT_TILE, 512), lambda t: (t, col))
    return _hooked_call(
        body, hook, n_in=n_in, n_out=2, grid=(nt,), name="attn_fwd",
        in_specs=[tile(0)] + _att_window_specs(nt, 1) + _att_window_specs(nt, 2) + [_full(bias.shape)],
        out_specs=[tile(0), pl.BlockSpec((ATT_TILE, ATT_HEADS), lambda t: (t, 0))],
        out_shape=[jax.ShapeDtypeStruct((S, 512), BF16), jax.ShapeDtypeStruct((S, ATT_HEADS), F32)],
        scratch_shapes=[pltpu.VMEM((ATT_HEADS, ATT_TILE, ATT_WIN * ATT_TILE), F32),
                        pltpu.VMEM((ATT_HEADS, ATT_TILE, ATT_WIN * ATT_TILE), BF16)],
        compiler_params=_params("arbitrary"),
    )(z, *([z] * (2 * ATT_WIN)), bias, *hook.ins)


def _group_mean(x, a_ref):
    hi = x.astype(BF16)
    lo = (x - hi.astype(F32)).astype(BF16)
    return _nn(hi, a_ref[...]) + _nn(lo, a_ref[...])


def _block_diag(x, ref):
    group = lax.broadcasted_iota(jnp.int32, x.shape, 1) // SG_GROUP_DIM
    for g in range(SG_GROUPS):
        ref[g * SG_BLOCK:(g + 1) * SG_BLOCK, :] = jnp.where(group == g, x, 0.0).astype(BF16)


def _sgu_block(zu_ref, zv_ref, rows, a_ref, lng_ref, lnb_ref, wcat_ref, bfull_ref, vbd_ref):
    zu = zu_ref[rows, :].astype(F32)
    zv = zv_ref[rows, :].astype(F32)
    u, v = _gelu(zu), _gelu(zv)
    vc = v - _group_mean(v, a_ref)
    rstd = lax.rsqrt(_group_mean(vc * vc, a_ref) + EPS)
    xhat = vc * rstd
    _block_diag(xhat * lng_ref[...] + lnb_ref[...], vbd_ref)
    sv = _nn(wcat_ref[...], vbd_ref[...]) + bfull_ref[...]
    return zu, zv, u, xhat, rstd, sv


def _sgu_rows(b):
    return pl.ds(pl.multiple_of(b * SG_BLOCK, SG_BLOCK), SG_BLOCK)


def sgu_fwd(z, amean, lng, lnb, wcat, bfull, *, tm):
    S = z.shape[0]

    def body(zu_ref, zv_ref, a_ref, lng_ref, lnb_ref, wcat_ref, bfull_ref, y_ref, vbd_ref):
        def block(b, carry):
            rows = _sgu_rows(b)
            _, _, u, _, _, sv = _sgu_block(zu_ref, zv_ref, rows, a_ref, lng_ref, lnb_ref, wcat_ref, bfull_ref, vbd_ref)
            y_ref[rows, :] = (u * sv).astype(BF16)
            return carry

        lax.fori_loop(0, tm // SG_BLOCK, block, 0)

    tile = lambda col: pl.BlockSpec((tm, 512), lambda i: (i, col))
    smalls = [amean, lng, lnb, wcat, bfull]
    return pl.pallas_call(
        body, grid=(S // tm,), name="sgu_fwd",
        in_specs=[tile(3), tile(4)] + [_full(a.shape) for a in smalls],
        out_specs=tile(0), out_shape=jax.ShapeDtypeStruct((S, 512), BF16),
        scratch_shapes=[pltpu.VMEM((SG_GROUPS * SG_BLOCK, 512), BF16)],
        compiler_params=_params("parallel"),
    )(z, z, *smalls)


def _gate_specs(tm):
    return [pl.BlockSpec((tm, 512), lambda i, c=c: (i, c)) for c in (5, 6, 7, 8)]


def merge_fwd(y_att, y_sg, z, x, w_ba, w_bs, w_out, *, tm):
    S, D = x.shape
    J, _, C = w_ba.shape

    def body(ya_ref, ys_ref, g0, g1, g2, g3, x_ref, wba_ref, wbs_ref, wo_ref, a_ref, b_ref, m_ref, x1_ref):
        ya, ys = ya_ref[...], ys_ref[...]
        a = jnp.concatenate([_nn(ya, wba_ref[j]) for j in range(J)], axis=1)
        b = jnp.concatenate([_nn(ys, wbs_ref[j]) for j in range(J)], axis=1)
        ga = jnp.concatenate([g0[...], g1[...]], axis=1).astype(F32)
        gb = jnp.concatenate([g2[...], g3[...]], axis=1).astype(F32)
        a_ref[...] = a.astype(BF16)
        b_ref[...] = b.astype(BF16)
        merged = (_sigmoid(ga) * a + _sigmoid(gb) * b).astype(BF16)
        m_ref[...] = merged
        x1_ref[...] = x_ref[...] + _nn(merged, wo_ref[...])

    row = lambda n: pl.BlockSpec((tm, n), lambda i: (i, 0))
    return pl.pallas_call(
        body, grid=(S // tm,), name="merge_fwd",
        in_specs=[row(512), row(512)] + _gate_specs(tm) + [row(D), _full(w_ba.shape), _full(w_bs.shape), _full(w_out.shape)],
        out_specs=[row(D)] * 4,
        out_shape=[jax.ShapeDtypeStruct((S, D), BF16)] * 3 + [jax.ShapeDtypeStruct((S, D), F32)],
        compiler_params=_params("parallel"),
    )(y_att, y_sg, z, z, z, z, x, w_ba, w_bs, w_out)


def xattn_fwd(xq, kv, w_xo, x1, *, tm):
    S, D = xq.shape
    dh = XATT_HEAD_DIM

    def body(q_ref, kv_ref, wo_ref, x1_ref, o_ref, lse_ref, x2_ref):
        outs = []
        for h in range(XATT_HEADS):
            s = _nt(q_ref[:, h * dh:(h + 1) * dh], kv_ref[:, h * dh:(h + 1) * dh])
            m = jnp.max(s, axis=-1, keepdims=True)
            p = jnp.exp(s - m)
            l = jnp.sum(p, axis=-1, keepdims=True)
            outs.append((_nn(p.astype(BF16), kv_ref[:, D + h * dh:D + (h + 1) * dh]) / l).astype(BF16))
            lse_ref[:, h:h + 1] = m + jnp.log(l)
        o = jnp.concatenate(outs, axis=1)
        o_ref[...] = o
        x2_ref[...] = x1_ref[...] + _nn(o, wo_ref[...])

    row = lambda n: pl.BlockSpec((tm, n), lambda i: (i, 0))
    return pl.pallas_call(
        body, grid=(S // tm,), name="xattn_fwd",
        in_specs=[row(D), _full(kv.shape), _full(w_xo.shape), row(D)],
        out_specs=[row(D), row(XATT_HEADS), row(D)],
        out_shape=[jax.ShapeDtypeStruct((S, D), BF16), jax.ShapeDtypeStruct((S, XATT_HEADS), F32),
                   jax.ShapeDtypeStruct((S, D), F32)],
        compiler_params=_params("parallel"),
    )(xq, kv, w_xo, x1)


def ffn_out_loss(gu, w, x2, g, target, *, tm):
    S, D = x2.shape
    F = w.shape[0]

    def body(gu_ref, w_ref, x2_ref, g_ref, t_ref, act_ref, loss_ref, dx_ref, dg_ref):
        @pl.when(pl.program_id(0) == 0)
        def _():
            loss_ref[...] = jnp.zeros_like(loss_ref)
            dg_ref[...] = jnp.zeros_like(dg_ref)

        gate = gu_ref[:, :F].astype(F32)
        up = gu_ref[:, F:].astype(F32)
        act = (gate * _sigmoid(gate) * up).astype(BF16)
        act_ref[...] = act
        gv = g_ref[...]
        rstd, xhat = _rms_stats(x2_ref[...] + _nn(act, w_ref[...]))
        err = xhat * gv - t_ref[...]
        loss_ref[...] += 0.5 * jnp.sum(jnp.mean(err * err, axis=-1, keepdims=True))
        dx, dgc = _rms_bwd(xhat, rstd, gv, err * (1.0 / D))
        dx_ref[...] = dx
        dg_ref[...] += jnp.sum(dgc, axis=0, keepdims=True)

    row = lambda n: pl.BlockSpec((tm, n), lambda i: (i, 0))
    return pl.pallas_call(
        body, grid=(S // tm,), name="ffn_out_loss",
        in_specs=[row(2 * F), _full(w.shape), row(D), _full((1, D)), row(D)],
        out_specs=[row(F), _acc((1, 128)), row(D), _acc((1, D))],
        out_shape=[jax.ShapeDtypeStruct((S, F), BF16), jax.ShapeDtypeStruct((1, 128), F32),
                   jax.ShapeDtypeStruct((S, D), F32), jax.ShapeDtypeStruct((1, D), F32)],
        compiler_params=_params("arbitrary"),
    )(gu, w, x2, g, target)


def ffn_act_bwd(dx3, gu, w, *, tm, nchunk):
    S, D = dx3.shape
    F = w.shape[0]
    cn = F // nchunk

    def body(dx_ref, gu_ref, w_ref, dgu_ref):
        dxb = dx_ref[...].astype(BF16)
        for j in range(nchunk):
            dact = _nt(dxb, w_ref[j * cn:(j + 1) * cn, :])
            gate = gu_ref[:, j * cn:(j + 1) * cn].astype(F32)
            up = gu_ref[:, F + j * cn:F + (j + 1) * cn].astype(F32)
            sg = _sigmoid(gate)
            dgu_ref[:, j * cn:(j + 1) * cn] = (dact * up * (sg * (1.0 + gate * (1.0 - sg)))).astype(BF16)
            dgu_ref[:, F + j * cn:F + (j + 1) * cn] = (dact * (gate * sg)).astype(BF16)

    row = lambda n: pl.BlockSpec((tm, n), lambda i: (i, 0))
    return pl.pallas_call(
        body, grid=(S // tm,), name="ffn_act_bwd",
        in_specs=[row(D), row(2 * F), _full(w.shape)], out_specs=row(2 * F),
        out_shape=jax.ShapeDtypeStruct((S, 2 * F), BF16), compiler_params=_params("parallel"),
    )(dx3, gu, w)


def tn_mm(a, b, *, shards, tn, tk, name):
    K, M = a.shape
    N = b.shape[1]
    C = N // shards
    per = tn // C
    nk = K // tk

    def body(a_ref, b_ref, o_ref, acc_ref):
        k = pl.program_id(1)

        @pl.when(k == 0)
        def _():
            acc_ref[...] = jnp.zeros_like(acc_ref)

        acc_ref[...] += _tn(a_ref[...].astype(BF16), b_ref[...].astype(BF16))

        @pl.when(k == nk - 1)
        def _():
            if shards > 1:
                for s in range(per):
                    o_ref[s] = acc_ref[:, s * C:(s + 1) * C].astype(BF16)
            else:
                o_ref[...] = acc_ref[...].astype(BF16)

    if shards > 1:
        out_spec = pl.BlockSpec((per, M, C), lambda n, k: (n, 0, 0))
        out_shape = jax.ShapeDtypeStruct((shards, M, C), BF16)
    else:
        out_spec = pl.BlockSpec((M, tn), lambda n, k: (0, n))
        out_shape = jax.ShapeDtypeStruct((M, N), BF16)
    return pl.pallas_call(
        body, grid=(N // tn, nk), name=name,
        in_specs=[pl.BlockSpec((tk, M), lambda n, k: (k, 0)), pl.BlockSpec((tk, tn), lambda n, k: (k, n))],
        out_specs=out_spec, out_shape=out_shape, scratch_shapes=[pltpu.VMEM((M, tn), F32)],
        compiler_params=_params("parallel", "arbitrary"),
    )(a, b)


def mm_nt_norm_bwd(dy, w, x, g, dx_in, *, tm, name, hook=Hook()):
    S, D = x.shape
    J, _, C = w.shape
    has_in = dx_in is not None
    n_in, nh, nho = 4 + has_in, len(hook.ins), len(hook.out_shapes)

    def body(*refs):
        dy_ref, w_ref, x_ref, g_ref = refs[:4]
        dxin_ref = refs[4] if has_in else None
        rest = refs[n_in:]
        h_in, (dx_ref, dg_ref), h_out, h_sems = rest[:nh], rest[nh:nh + 2], rest[nh + 2:nh + 2 + nho], rest[nh + 2 + nho:]
        _run_hook(hook, "before", h_in, h_out, h_sems)

        @pl.when(pl.program_id(0) == 0)
        def _():
            dg_ref[...] = jnp.zeros_like(dg_ref)

        dh = _nt(dy_ref[:, 0:C], w_ref[0])
        for j in range(1, J):
            dh += _nt(dy_ref[:, j * C:(j + 1) * C], w_ref[j])
        rstd, xhat = _rms_stats(x_ref[...])
        dx, dgc = _rms_bwd(xhat, rstd, g_ref[...], dh)
        dx_ref[...] = dx + dxin_ref[...] if has_in else dx
        dg_ref[...] += jnp.sum(dgc, axis=0, keepdims=True)
        _run_hook(hook, "after", h_in, h_out, h_sems)

    row = lambda n: pl.BlockSpec((tm, n), lambda i: (i, 0))
    ins = [dy, w, x, g] + ([dx_in] if has_in else [])
    return _hooked_call(
        body, hook, n_in=n_in, n_out=2, grid=(S // tm,), name=name,
        in_specs=[row(J * C), _full(w.shape), row(D), _full((1, D))] + ([row(D)] if has_in else []),
        out_specs=[row(D), _acc((1, D))],
        out_shape=[jax.ShapeDtypeStruct((S, D), F32), jax.ShapeDtypeStruct((1, D), F32)],
        compiler_params=_params("arbitrary"),
    )(*ins, *hook.ins)


def xattn_bwd(dx2, xq, o, lse, kv, w_xo, *, tm):
    S, D = xq.shape
    M = kv.shape[0]
    dh = XATT_HEAD_DIM

    def body(dx_ref, q_ref, o_ref, lse_ref, kv_ref, wo_ref, dq_ref, dkv_ref):
        @pl.when(pl.program_id(0) == 0)
        def _():
            dkv_ref[...] = jnp.zeros_like(dkv_ref)

        do = _nt(dx_ref[...].astype(BF16), wo_ref[...])
        for h in range(XATT_HEADS):
            hs = slice(h * dh, (h + 1) * dh)
            vs = slice(D + h * dh, D + (h + 1) * dh)
            q, k, v = q_ref[:, hs], kv_ref[:, hs], kv_ref[:, vs]
            do_h = do[:, hs]
            do_b = do_h.astype(BF16)
            p = jnp.exp(_nt(q, k) - lse_ref[:, h:h + 1])
            delta = jnp.sum(do_h * o_ref[:, hs].astype(F32), axis=-1, keepdims=True)
            ds = (p * (_nt(do_b, v) - delta)).astype(BF16)
            dq_ref[:, hs] = (_nn(ds, k) * (dh ** -0.5)).astype(BF16)
            dkv_ref[:, hs] += _tn(ds, q)
            dkv_ref[:, vs] += _tn(p.astype(BF16), do_b)

    row = lambda n: pl.BlockSpec((tm, n), lambda i: (i, 0))
    return pl.pallas_call(
        body, grid=(S // tm,), name="xattn_bwd",
        in_specs=[row(D), row(D), row(D), row(XATT_HEADS), _full(kv.shape), _full(w_xo.shape)],
        out_specs=[row(D), _acc((M, 2 * D))],
        out_shape=[jax.ShapeDtypeStruct((S, D), BF16), jax.ShapeDtypeStruct((M, 2 * D), F32)],
        compiler_params=_params("arbitrary"),
    )(dx2, xq, o, lse, kv, w_xo)


def merge_bwd(dx1, w_out, a, b, z, w_ba, w_bs, *, tm):
    S, D = dx1.shape
    J, W, C = w_ba.shape

    def body(dx_ref, wo_ref, a_ref, b_ref, g0, g1, g2, g3, wba_ref, wbs_ref, da_ref, db_ref, dg_ref, dya_ref, dys_ref):
        dm = _nt(dx_ref[...].astype(BF16), wo_ref[...])
        sa = _sigmoid(jnp.concatenate([g0[...], g1[...]], axis=1).astype(F32))
        sb = _sigmoid(jnp.concatenate([g2[...], g3[...]], axis=1).astype(F32))
        dg_ref[:, :D] = (dm * a_ref[...].astype(F32) * (sa * (1.0 - sa))).astype(BF16)
        dg_ref[:, D:] = (dm * b_ref[...].astype(F32) * (sb * (1.0 - sb))).astype(BF16)
        da = (dm * sa).astype(BF16)
        db = (dm * sb).astype(BF16)
        da_ref[...] = da
        db_ref[...] = db
        dya = _nt(da[:, 0:C], wba_ref[0])
        dys = _nt(db[:, 0:C], wbs_ref[0])
        for j in range(1, J):
            dya += _nt(da[:, j * C:(j + 1) * C], wba_ref[j])
            dys += _nt(db[:, j * C:(j + 1) * C], wbs_ref[j])
        dya_ref[...] = dya.astype(BF16)
        dys_ref[...] = dys.astype(BF16)

    row = lambda n: pl.BlockSpec((tm, n), lambda i: (i, 0))
    return pl.pallas_call(
        body, grid=(S // tm,), name="merge_bwd",
        in_specs=[row(D), _full(w_out.shape), row(D), row(D)] + _gate_specs(tm) + [_full(w_ba.shape), _full(w_bs.shape)],
        out_specs=[row(D), row(D), row(2 * D), row(W), row(W)],
        out_shape=[jax.ShapeDtypeStruct((S, D), BF16)] * 2 + [jax.ShapeDtypeStruct((S, 2 * D), BF16)]
        + [jax.ShapeDtypeStruct((S, W), BF16)] * 2,
        compiler_params=_params("parallel"),
    )(dx1, w_out, a, b, z, z, z, z, w_ba, w_bs)


def attn_bwd(z, y, dy, lse, bias, hook=Hook()):
    S = z.shape[0]
    nt = S // ATT_TILE
    back = ATT_WIN - 1
    n_in, nh, nho = 5 + 2 * ATT_WIN, len(hook.ins), len(hook.out_shapes)

    def body(q_ref, *refs):
        k_refs, v_refs = refs[:ATT_WIN], refs[ATT_WIN:2 * ATT_WIN]
        y_ref, dy_ref, lse_ref, bias_ref = refs[2 * ATT_WIN:2 * ATT_WIN + 4]
        rest = refs[2 * ATT_WIN + 4:]
        h_in, (dq_ref, dk_ref, dv_ref, dbias_ref), h_out = rest[:nh], rest[nh:nh + 4], rest[nh + 4:nh + 4 + nho]
        dk_acc, dv_acc = rest[nh + 4 + nho:nh + 6 + nho]
        h_sems = rest[nh + 6 + nho:]
        t = pl.program_id(0)
        _run_hook(hook, "before", h_in, h_out, h_sems)

        @pl.when(t == 0)
        def _():
            dbias_ref[...] = jnp.zeros_like(dbias_ref)
            dk_acc[...] = jnp.zeros_like(dk_acc)
            dv_acc[...] = jnp.zeros_like(dv_acc)

        @pl.when(t < nt)
        def _():
            valid = _att_valid(t)
            q_t = (q_ref[...] * ATT_SCALE).astype(F32).T.astype(BF16)
            do_t = dy_ref[...].astype(F32).T.astype(BF16)
            for h in range(ATT_HEADS):
                sl = slice(h * ATT_HEAD_DIM, (h + 1) * ATT_HEAD_DIM)
                k = jnp.concatenate([r[:, sl] for r in k_refs], axis=0)
                v = jnp.concatenate([r[:, sl] for r in v_refs], axis=0)
                q, do_b = q_ref[:, sl] * ATT_SCALE, dy_ref[:, sl]
                p = jnp.exp(_att_scores(q, k, bias_ref[h], valid) - lse_ref[:, h:h + 1])
                delta = jnp.sum(do_b.astype(F32) * y_ref[:, sl].astype(F32), axis=-1, keepdims=True)
                ds = p * (_nt(do_b, v) - delta)
                dbias_ref[h] += ds
                ds_b = ds.astype(BF16)
                dq_ref[:, sl] = (_nn(ds_b, k) * ATT_SCALE).astype(BF16)
                dk_w = _nn(q_t[sl, :], ds_b)
                dv_w = _nn(do_t[sl, :], p.astype(BF16))
                for i in range(ATT_WIN):
                    slot = (t + 1 + i) % ATT_WIN
                    cols = slice(i * ATT_TILE, (i + 1) * ATT_TILE)
                    if i == back:
                        dk_acc[slot, sl, :] = dk_w[:, cols]
                        dv_acc[slot, sl, :] = dv_w[:, cols]
                    else:
                        dk_acc[slot, sl, :] += dk_w[:, cols]
                        dv_acc[slot, sl, :] += dv_w[:, cols]

        done = (t + 1) % ATT_WIN
        dk_ref[...] = dk_acc[done].T.astype(BF16)
        dv_ref[...] = dv_acc[done].T.astype(BF16)
        _run_hook(hook, "after", h_in, h_out, h_sems)

    last = nt - 1
    tile = lambda col, n=512: pl.BlockSpec((ATT_TILE, n), lambda t: (jnp.minimum(t, last), col))
    late = pl.BlockSpec((ATT_TILE, 512), lambda t: (jnp.maximum(t - back, 0), 0))
    return _hooked_call(
        body, hook, n_in=n_in, n_out=4, grid=(nt + back,), name="attn_bwd",
        in_specs=[tile(0)] + _att_window_specs(nt, 1) + _att_window_specs(nt, 2)
        + [tile(0), tile(0), tile(0, ATT_HEADS), _full(bias.shape)],
        out_specs=[tile(0), late, late, _acc(bias.shape)],
        out_shape=[jax.ShapeDtypeStruct((S, 512), BF16)] * 3 + [jax.ShapeDtypeStruct(bias.shape, F32)],
        scratch_shapes=[pltpu.VMEM((ATT_WIN, 512, ATT_TILE), F32)] * 2,
        compiler_params=_params("arbitrary"),
    )(z, *([z] * (2 * ATT_WIN)), y, dy, lse, bias, *hook.ins)


def sgu_bwd(z, dy, amean, lng, lnb, wcat, wtcat, bfull, maskcat, *, tm):
    S = z.shape[0]
    n = S // tm

    def body(zu_ref, zv_ref, dy_ref, a_ref, lng_ref, lnb_ref, wcat_ref, wtcat_ref, bfull_ref, mask_ref,
             duv_ref, dw_ref, dsgb_ref, dlng_ref, dlnb_ref, vbd_ref, dbd_ref, w_acc, b_acc, g_acc, s_acc):
        i = pl.program_id(0)

        @pl.when(i == 0)
        def _():
            for r in (w_acc, b_acc, g_acc, s_acc):
                r[...] = jnp.zeros_like(r)

        def block(b, carry):
            rows = _sgu_rows(b)
            zu, zv, u, xhat, rstd, sv = _sgu_block(zu_ref, zv_ref, rows, a_ref, lng_ref, lnb_ref, wcat_ref, bfull_ref, vbd_ref)
            dyv = dy_ref[rows, :].astype(F32)
            duv_ref[rows, 0:512] = (dyv * sv * _gelu_grad(zu)).astype(BF16)
            dsv = dyv * u
            b_acc[...] += dsv
            w_acc[...] += _nt(dsv.astype(BF16), vbd_ref[...])
            _block_diag(dsv, dbd_ref)
            dvn = _nn(wtcat_ref[...], dbd_ref[...])
            g_acc[...] += dvn * xhat
            s_acc[...] += dvn
            dxh = dvn * lng_ref[...]
            dv = rstd * (dxh - _group_mean(dxh, a_ref) - xhat * _group_mean(dxh * xhat, a_ref))
            duv_ref[rows, 512:1024] = (dv * _gelu_grad(zv)).astype(BF16)
            return carry

        lax.fori_loop(0, tm // SG_BLOCK, block, 0)

        @pl.when(i == n - 1)
        def _():
            dw_ref[...] = w_acc[...] * mask_ref[...]
            dsgb_ref[...] = _group_mean(b_acc[...], a_ref) * float(SG_GROUP_DIM)
            dlng_ref[...] = jnp.sum(g_acc[...], axis=0, keepdims=True)
            dlnb_ref[...] = jnp.sum(s_acc[...], axis=0, keepdims=True)

    tile = lambda col: pl.BlockSpec((tm, 512), lambda i: (i, col))
    smalls = [amean, lng, lnb, wcat, wtcat, bfull, maskcat]
    wide = SG_GROUPS * SG_BLOCK
    return pl.pallas_call(
        body, grid=(n,), name="sgu_bwd",
        in_specs=[tile(3), tile(4), tile(0)] + [_full(a.shape) for a in smalls],
        out_specs=[pl.BlockSpec((tm, 1024), lambda i: (i, 0)), _acc((SG_BLOCK, wide)), _acc((SG_BLOCK, 512)),
                   _acc((1, 512)), _acc((1, 512))],
        out_shape=[jax.ShapeDtypeStruct((S, 1024), BF16), jax.ShapeDtypeStruct((SG_BLOCK, wide), F32),
                   jax.ShapeDtypeStruct((SG_BLOCK, 512), F32), jax.ShapeDtypeStruct((1, 512), F32),
                   jax.ShapeDtypeStruct((1, 512), F32)],
        scratch_shapes=[pltpu.VMEM((wide, 512), BF16), pltpu.VMEM((wide, 512), BF16), pltpu.VMEM((SG_BLOCK, wide), F32),
                        pltpu.VMEM((SG_BLOCK, 512), F32), pltpu.VMEM((SG_BLOCK, 512), F32), pltpu.VMEM((SG_BLOCK, 512), F32)],
        compiler_params=_params("arbitrary"),
    )(z, z, dy, *smalls)


def bias_colsum(p):
    H, _, L = p.shape
    near_lo = (ATT_WIN - 1) * ATT_TILE - REL_CLIP + 1
    near_hi = ATT_WIN * ATT_TILE

    def body(p_ref, gw_ref, far_ref):
        k = lax.broadcasted_iota(jnp.int32, (1, L), 1)
        is_far = (k < near_lo) | (k >= near_hi)
        for h in range(H):
            g = jnp.sum(p_ref[h], axis=0, keepdims=True)
            gw_ref[h:h + 1, :] = g
            far_ref[h:h + 1, :] = jnp.zeros((1, 128), F32) + jnp.sum(jnp.where(is_far, g, 0.0))

    return pl.pallas_call(
        body, name="bias_colsum", in_specs=[_full(p.shape)], out_specs=[_acc((H, L)), _acc((H, 128))],
        out_shape=[jax.ShapeDtypeStruct((H, L), F32), jax.ShapeDtypeStruct((H, 128), F32)], grid=(1,),
        compiler_params=_params("arbitrary"),
    )(p)


def _row_tile(rows, cols):
    best = None
    for tr in range(16, rows + 1, 16):
        if rows % tr == 0 and tr * cols * 4 <= EW_BLOCK_BYTES:
            best = tr
    return best if best is not None else rows


def place_shard(w, own, name):
    R, C = w.shape
    r = R // 2
    tr = _row_tile(r, C)
    nr = r // tr

    def body(own_ref, w_ref, o_ref):
        o_ref[...] = w_ref[...].astype(BF16)

    return pl.pallas_call(
        body, name=name,
        grid_spec=pltpu.PrefetchScalarGridSpec(
            num_scalar_prefetch=1, grid=(2, nr),
            in_specs=[pl.BlockSpec((tr, C), lambda h, i, own: (h * nr + i, 0))],
            out_specs=pl.BlockSpec((None, None, tr, C), lambda h, i, own: (own[0], h, i, 0))),
        out_shape=jax.ShapeDtypeStruct((4, 2, r, C), BF16), compiler_params=_params("parallel", "parallel"),
    )(own, w)


def chip_partial(grad, recv, core, name):
    _, _, r, C = grad.shape
    tr = _row_tile(r, C)

    def body(core_ref, g_ref, r_ref, o_ref):
        o_ref[...] = (g_ref[...].astype(F32) + r_ref[...].astype(F32)).astype(BF16)

    spec = pl.BlockSpec((None, tr, C), lambda j, i, core: (j, i, 0))
    return pl.pallas_call(
        body, name=name,
        grid_spec=pltpu.PrefetchScalarGridSpec(
            num_scalar_prefetch=1, grid=(4, r // tr),
            in_specs=[pl.BlockSpec((None, None, tr, C), lambda j, i, core: (j, core[0], i, 0)), spec], out_specs=spec),
        out_shape=jax.ShapeDtypeStruct((4, r, C), BF16), compiler_params=_params("parallel", "parallel"),
    )(core, grad, recv)


def shard_sum(part, recv, own, core, name):
    _, r, C = part.shape
    tr = _row_tile(r, C)

    def body(own_ref, core_ref, p_ref, r0, r1, r2, o_ref):
        o_ref[...] = p_ref[...].astype(F32) + r0[...].astype(F32) + r1[...].astype(F32) + r2[...].astype(F32)

    return pl.pallas_call(
        body, name=name,
        grid_spec=pltpu.PrefetchScalarGridSpec(
            num_scalar_prefetch=2, grid=(r // tr,),
            in_specs=[pl.BlockSpec((None, tr, C), lambda i, own, core: (own[0], i, 0))]
            + [pl.BlockSpec((None, tr, C), lambda i, own, core, k=k: (k, i, 0)) for k in range(3)],
            out_specs=pl.BlockSpec((None, tr, C), lambda i, own, core: (core[0], i, 0))),
        out_shape=jax.ShapeDtypeStruct((2, r, C), F32), compiler_params=_params("parallel"),
    )(own, core, part, recv, recv, recv)


def adamw(w, g, m, v, name):
    R, C = w.shape
    tr = _row_tile(R, C)

    def body(w_ref, g_ref, m_ref, v_ref, d_ref, nm_ref, nv_ref):
        gv = g_ref[...]
        nm = ADAM_B1 * m_ref[...] + (1.0 - ADAM_B1) * gv
        nv = ADAM_B2 * v_ref[...] + (1.0 - ADAM_B2) * (gv * gv)
        m_hat = nm / (1.0 - ADAM_B1 ** ADAM_STEP)
        v_hat = nv / (1.0 - ADAM_B2 ** ADAM_STEP)
        d_ref[...] = -ADAM_LR * (m_hat / (jnp.sqrt(v_hat) + ADAM_EPS) + ADAM_WD * w_ref[...])
        nm_ref[...] = nm
        nv_ref[...] = nv

    spec = pl.BlockSpec((tr, C), lambda i: (i, 0))
    return pl.pallas_call(
        body, grid=(R // tr,), name=name, in_specs=[spec] * 4, out_specs=[spec] * 3,
        out_shape=[jax.ShapeDtypeStruct((R, C), F32)] * 3, compiler_params=_params("parallel"),
    )(w, g, m, v)


HBM = pl.BlockSpec(memory_space=pl.ANY)


def _place():
    x, y, c = lax.axis_index("x"), lax.axis_index("y"), lax.axis_index("c")
    chips = [(1 - x, y), (x, 1 - y), (1 - x, 1 - y)]
    return x, y, c, chips


def _gather_phases(n):
    def copies(buf, sems, kind):
        send_sems, recv_sems = sems
        x, y, c, chips = _place()

        def remote(w, k, slot, to):
            return pltpu.make_async_remote_copy(src_ref=slot, dst_ref=slot, send_sem=send_sems.at[w, k],
                                                recv_sem=recv_sems.at[w, k], device_id=to, device_id_type=MESH)

        def one(w, k, px, py):
            if kind == "mine":
                return remote(w, k, buf[w].at[2 * x + y, c], (px, py, c))
            if kind == "theirs":
                return remote(w, k, buf[w].at[2 * px + py, c], (px, py, c))
            if kind == "onward":
                return remote(w, 3 + k, buf[w].at[2 * px + py, c], (x, y, 1 - c))
            return remote(w, 3 + k, buf[w].at[2 * px + py, 1 - c], (x, y, 1 - c))

        return [one(w, k, px, py) for w in range(n) for k, (px, py) in enumerate(chips)]

    def start(_, buf, sems):
        for cp in copies(buf, sems, "mine"):
            cp.start()

    def relay(_, buf, sems):
        for theirs, onward in zip(copies(buf, sems, "theirs"), copies(buf, sems, "onward")):
            theirs.wait_recv()
            onward.start()

    def finish(_, buf, sems):
        for cp in copies(buf, sems, "relayed"):
            cp.wait_recv()
        for cp in copies(buf, sems, "mine") + copies(buf, sems, "onward"):
            cp.wait_send()

    return start, relay, finish


def _gather_sems(n):
    return (pltpu.SemaphoreType.DMA((n, 6)), pltpu.SemaphoreType.DMA((n, 6)))


def gather_weights(bufs):
    n = len(bufs)
    phases = _gather_phases(n)

    def body(*refs):
        for phase in phases:
            phase(None, refs[n:2 * n], refs[2 * n:])

    return pl.pallas_call(
        body, name="gather_weights", in_specs=[HBM] * n, out_specs=[HBM] * n,
        out_shape=[jax.ShapeDtypeStruct(b.shape, b.dtype) for b in bufs],
        input_output_aliases={i: i for i in range(n)}, scratch_shapes=list(_gather_sems(n)),
    )(*bufs)


def gather_hook(bufs, n_steps):
    start, relay, finish = _gather_phases(len(bufs))
    return Hook(ins=tuple(bufs), out_shapes=tuple(jax.ShapeDtypeStruct(b.shape, b.dtype) for b in bufs), aliased=True,
                sems=_gather_sems(len(bufs)),
                steps=((0, "before", start), ((n_steps * 13) // 16, "before", relay), (n_steps - 1, "after", finish)))


def sibling_split(grads, name):
    n = len(grads)

    def body(*refs):
        src, dst = refs[:n], refs[n:2 * n]
        send_sems, recv_sems = refs[2 * n:]
        x, y, c, _ = _place()
        sends = [pltpu.make_async_remote_copy(src_ref=src[w].at[:, 1 - c], dst_ref=dst[w], send_sem=send_sems.at[w],
                                              recv_sem=recv_sems.at[w], device_id=(x, y, 1 - c), device_id_type=MESH)
                 for w in range(n)]
        for cp in sends:
            cp.start()
        for cp in sends:
            cp.wait()

    return pl.pallas_call(
        body, name=name, in_specs=[HBM] * n, out_specs=[HBM] * n,
        out_shape=[jax.ShapeDtypeStruct((4,) + g.shape[2:], g.dtype) for g in grads],
        scratch_shapes=[pltpu.SemaphoreType.DMA((n,))] * 2,
    )(*grads)


def exchange_hook(parts, n_steps):
    n = len(parts)

    def copies(src, dst, sems):
        send_sems, recv_sems = sems
        _, _, c, chips = _place()
        return [pltpu.make_async_remote_copy(
            src_ref=src[w].at[2 * px + py], dst_ref=dst[w].at[k], send_sem=send_sems.at[w, k],
            recv_sem=recv_sems.at[w, k], device_id=(px, py, c), device_id_type=MESH)
            for w in range(n) for k, (px, py) in enumerate(chips)]

    def start(src, dst, sems):
        for cp in copies(src, dst, sems):
            cp.start()

    def finish(src, dst, sems):
        for cp in copies(src, dst, sems):
            cp.wait()

    return Hook(ins=tuple(parts), out_shapes=tuple(jax.ShapeDtypeStruct((3,) + p.shape[1:], p.dtype) for p in parts),
                sems=(pltpu.SemaphoreType.DMA((n, 3)), pltpu.SemaphoreType.DMA((n, 3))),
                steps=((0, "before", start), (n_steps - 1, "after", finish)))


def sibling_join(sums):
    n = len(sums)

    def body(*refs):
        buf = refs[n:2 * n]
        send_sems, recv_sems = refs[2 * n:]
        x, y, c, _ = _place()
        sends = [pltpu.make_async_remote_copy(src_ref=buf[w].at[c], dst_ref=buf[w].at[c], send_sem=send_sems.at[w],
                                              recv_sem=recv_sems.at[w], device_id=(x, y, 1 - c), device_id_type=MESH)
                 for w in range(n)]
        for cp in sends:
            cp.start()
        for w, cp in enumerate(sends):
            cp.wait_send()
            pltpu.make_async_remote_copy(src_ref=buf[w].at[c], dst_ref=buf[w].at[1 - c], send_sem=send_sems.at[w],
                                         recv_sem=recv_sems.at[w], device_id=(x, y, 1 - c), device_id_type=MESH).wait_recv()

    return pl.pallas_call(
        body, name="sibling_join", in_specs=[HBM] * n, out_specs=[HBM] * n,
        out_shape=[jax.ShapeDtypeStruct(s.shape, s.dtype) for s in sums],
        input_output_aliases={i: i for i in range(n)},
        scratch_shapes=[pltpu.SemaphoreType.DMA((n,))] * 2,
    )(*sums)


def allreduce_small(v):
    R, C = v.shape

    def body(v_ref, out_ref, all_ref, send_sems, recv_sems):
        x, y, c, chips = _place()
        me, sibling = (x, y, c), (x, y, 1 - c)

        def slot(px, py, pc):
            return all_ref.at[4 * px + 2 * py + pc]

        def copy(k, block, to, src=None):
            return pltpu.make_async_remote_copy(src_ref=slot(*block) if src is None else src, dst_ref=slot(*block),
                                                send_sem=send_sems.at[k], recv_sem=recv_sems.at[k], device_id=to,
                                                device_id_type=MESH)

        first = [copy(0, me, sibling, src=v_ref)] + [copy(1 + j, me, (*chip, c), src=v_ref) for j, chip in enumerate(chips)]
        for cp in first:
            cp.start()
        slot(*me)[...] = v_ref[...]
        passed = [copy(4 + j, (*chip, c), sibling) for j, chip in enumerate(chips)]
        for j, chip in enumerate(chips):
            copy(1 + j, (*chip, c), me).wait_recv()
            passed[j].start()
        copy(0, sibling, me).wait_recv()
        for j, chip in enumerate(chips):
            copy(4 + j, (*chip, 1 - c), me).wait_recv()
        for cp in first + passed:
            cp.wait_send()
        acc = all_ref[0]
        for k in range(1, 8):
            acc = acc + all_ref[k]
        out_ref[...] = acc

    vmem = pl.BlockSpec(memory_space=pltpu.VMEM)
    return pl.pallas_call(
        body, name="allreduce_small", in_specs=[vmem], out_specs=vmem, out_shape=jax.ShapeDtypeStruct((R, C), F32),
        scratch_shapes=[pltpu.VMEM((8, R, C), F32), pltpu.SemaphoreType.DMA((7,)), pltpu.SemaphoreType.DMA((7,))],
        compiler_params=pltpu.CompilerParams(vmem_limit_bytes=VMEM_LIMIT_V7X),
    )(v)


def allgather_hook(v, n_steps):
    R, C = v.shape

    def copies(v_ref, out, sems, kind):
        send_sems, recv_sems, _ = sems
        x, y, c, chips = _place()
        me, sibling = (x, y, c), (x, y, 1 - c)

        def slot(px, py, pc):
            return out.at[4 * px + 2 * py + pc]

        def copy(k, block, to, src=None):
            return pltpu.make_async_remote_copy(src_ref=slot(*block) if src is None else src, dst_ref=slot(*block),
                                                send_sem=send_sems.at[k], recv_sem=recv_sems.at[k], device_id=to,
                                                device_id_type=MESH)

        if kind == "mine":
            return [copy(0, me, sibling, src=v_ref)] + [copy(1 + j, me, (*chip, c), src=v_ref) for j, chip in enumerate(chips)]
        if kind == "theirs":
            return [copy(1 + j, (*chip, c), me) for j, chip in enumerate(chips)]
        if kind == "onward":
            return [copy(4 + j, (*chip, c), sibling) for j, chip in enumerate(chips)]
        return [copy(0, sibling, me)] + [copy(4 + j, (*chip, 1 - c), me) for j, chip in enumerate(chips)]

    def own(v_ref, out, sems):
        x, y, c, _ = _place()
        return pltpu.make_async_copy(v_ref, out.at[4 * x + 2 * y + c], sems[2])

    def start(ins, outs, sems):
        own(ins[0], outs[0], sems).start()
        for cp in copies(ins[0], outs[0], sems, "mine"):
            cp.start()

    def relay(ins, outs, sems):
        for theirs, onward in zip(copies(ins[0], outs[0], sems, "theirs"), copies(ins[0], outs[0], sems, "onward")):
            theirs.wait_recv()
            onward.start()

    def finish(ins, outs, sems):
        for cp in copies(ins[0], outs[0], sems, "relayed"):
            cp.wait_recv()
        for cp in copies(ins[0], outs[0], sems, "mine") + copies(ins[0], outs[0], sems, "onward"):
            cp.wait_send()
        own(ins[0], outs[0], sems).wait()

    return Hook(ins=(v,), out_shapes=(jax.ShapeDtypeStruct((8, R, C), F32),),
                sems=(pltpu.SemaphoreType.DMA((7,)), pltpu.SemaphoreType.DMA((7,)), pltpu.SemaphoreType.DMA(())),
                steps=((0, "before", start), (n_steps // 2, "before", relay), (n_steps - 1, "after", finish)))


def merge_hooks(a, b):
    assert not a.aliased and not b.aliased
    cut = (len(a.ins), len(a.out_shapes), len(a.sems))

    def part(fn, second):
        def run(ins, outs, sems):
            if second:
                fn(ins[cut[0]:], outs[cut[1]:], sems[cut[2]:])
            else:
                fn(ins[:cut[0]], outs[:cut[1]], sems[:cut[2]])
        return run

    steps = tuple((s, w, part(fn, False)) for s, w, fn in a.steps) + tuple((s, w, part(fn, True)) for s, w, fn in b.steps)
    return Hook(a.ins + b.ins, a.out_shapes + b.out_shapes, False, a.sems + b.sems, steps)


def sum_blocks(g):
    n, R, C = g.shape

    def body(g_ref, o_ref):
        acc = g_ref[0]
        for k in range(1, n):
            acc = acc + g_ref[k]
        o_ref[...] = acc

    return pl.pallas_call(body, name="sum_blocks", grid=(1,), in_specs=[_full(g.shape)], out_specs=_acc((R, C)),
                          out_shape=jax.ShapeDtypeStruct((R, C), F32), compiler_params=_params("arbitrary"))(g)


SMALL_PAD = 1024


def _pack_small(arrs):
    parts = []
    for a in arrs:
        f = a.reshape(-1).astype(F32)
        parts.append(jnp.pad(f, (0, (-f.shape[0]) % SMALL_PAD)))
    return jnp.concatenate(parts).reshape(-1, 128)


def _unpack_small(packed, shapes):
    flat = packed.reshape(-1)
    outs, off = [], 0
    for s in shapes:
        size = int(np.prod(s))
        outs.append(flat[off:off + size].reshape(s))
        off += size + (-size) % SMALL_PAD
    return outs


ATT_KEYS = ATT_WIN * ATT_TILE
ATT_NEAR_LO = (ATT_WIN - 1) * ATT_TILE - REL_CLIP + 1
ATT_PERIOD = ATT_KEYS + ATT_TILE + 1


def _att_bias_table(rel_bias):
    far = rel_bias[:, 2 * REL_CLIP:]
    near = rel_bias[:, 2 * REL_CLIP - 1:0:-1]
    w = jnp.concatenate([jnp.broadcast_to(far, (ATT_HEADS, ATT_NEAR_LO)), near,
                         jnp.broadcast_to(far, (ATT_HEADS, ATT_PERIOD - ATT_KEYS))], axis=1)
    rows = jnp.tile(w, (1, ATT_TILE))[:, :ATT_TILE * (ATT_PERIOD - 1)].reshape(ATT_HEADS, ATT_TILE, ATT_PERIOD - 1)
    i = np.arange(ATT_TILE)[:, None]
    m = np.arange(ATT_KEYS)[None, :]
    qc, kc = i // CHUNK, m // CHUNK
    band = (kc >= qc) & (kc <= qc + N_PREV_CHUNKS)
    return jnp.where(band[None], rows[:, :, :ATT_KEYS], NEG_INF)


def _rel_bias_grad(dbias):
    p = jnp.pad(dbias, ((0, 0), (0, 0), (0, ATT_PERIOD - 1 - ATT_KEYS))).reshape(ATT_HEADS, -1)
    p = jnp.pad(p, ((0, 0), (0, ATT_TILE))).reshape(ATT_HEADS, ATT_TILE, ATT_PERIOD)
    gw, far = bias_colsum(p)
    return jnp.concatenate([jnp.zeros((ATT_HEADS, 1), F32), gw[:, ATT_KEYS - 1:ATT_NEAR_LO - 1:-1], far[:, :1]], axis=1)


def kernel(x, mem, norm_mix_g, w_in, rel_bias, sg_ln_g, sg_ln_b, sg_w, sg_b, w_branch_att, w_branch_sg, w_out, norm_xattn_g, norm_mem_g, w_xq, w_xkv, w_xo, norm_ffn_g, w_ffn_in, w_ffn_out, norm_final_g, loss_target, m_norm_mix_g, m_w_in, m_rel_bias, m_sg_ln_g, m_sg_ln_b, m_sg_w, m_sg_b, m_w_branch_att, m_w_branch_sg, m_w_out, m_norm_xattn_g, m_norm_mem_g, m_w_xq, m_w_xkv, m_w_xo, m_norm_ffn_g, m_w_ffn_in, m_w_ffn_out, m_norm_final_g, v_norm_mix_g, v_w_in, v_rel_bias, v_sg_ln_g, v_sg_ln_b, v_sg_w, v_sg_b, v_w_branch_att, v_w_branch_sg, v_w_out, v_norm_xattn_g, v_norm_mem_g, v_w_xq, v_w_xkv, v_w_xo, v_norm_ffn_g, v_w_ffn_in, v_w_ffn_out, v_norm_final_g):
    S, D = x.shape[1], x.shape[2]
    x2d, mem2d, tgt = x[0], mem[0], loss_target[0]

    big_names = ["w_in", "w_branch_att", "w_branch_sg", "w_out", "w_xq", "w_xkv", "w_xo", "w_ffn_in", "w_ffn_out"]
    col_sharded = [True, True, True, False, False, True, False, True, False]
    big_w = [a[0] for a in (w_in, w_branch_att, w_branch_sg, w_out, w_xq, w_xkv, w_xo, w_ffn_in, w_ffn_out)]
    big_m = [a[0] for a in (m_w_in, m_w_branch_att, m_w_branch_sg, m_w_out, m_w_xq, m_w_xkv, m_w_xo, m_w_ffn_in, m_w_ffn_out)]
    big_v = [a[0] for a in (v_w_in, v_w_branch_att, v_w_branch_sg, v_w_out, v_w_xq, v_w_xkv, v_w_xo, v_w_ffn_in, v_w_ffn_out)]

    own = (2 * lax.axis_index("x") + lax.axis_index("y")).astype(jnp.int32).reshape(1)
    core = lax.axis_index("c").astype(jnp.int32).reshape(1)
    placed = [place_shard(w, own, "place_" + nm) for w, nm in zip(big_w, big_names)]

    def whole(g4, i):
        R, C = big_w[i].shape
        return g4.reshape(4, R, C) if col_sharded[i] else g4.reshape(4 * R, C)

    W_in = whole(gather_weights(placed[:1])[0], 0)

    g_mix, g_xat, g_mem, g_ffn = norm_mix_g, norm_xattn_g, norm_mem_g, norm_ffn_g
    g_fin = norm_final_g.reshape(1, D)
    bias = _att_bias_table(rel_bias[0])
    tt = np.arange(SG_BLOCK)
    sg_mask = (tt[None, :] // CHUNK) <= (tt[:, None] // CHUNK)
    wm_f = jnp.where(sg_mask[None], sg_w[0], 0.0)
    wide = SG_GROUPS * SG_BLOCK
    wcat = jnp.transpose(wm_f, (1, 0, 2)).reshape(SG_BLOCK, wide).astype(BF16)
    wtcat = jnp.transpose(wm_f, (2, 0, 1)).reshape(SG_BLOCK, wide).astype(BF16)
    maskcat = jnp.asarray(np.tile(sg_mask, (1, SG_GROUPS)), F32)
    bfull = jnp.repeat(sg_b[0].T, SG_GROUP_DIM, axis=1)
    lng, lnb = sg_ln_g[0].reshape(1, 512), sg_ln_b[0].reshape(1, 512)
    gid = np.arange(512) // SG_GROUP_DIM
    amean = jnp.asarray((gid[:, None] == gid[None, :]) / SG_GROUP_DIM, BF16)

    h1, z, *early = norm_mm(x2d, g_mix, W_in, tm=ROW_TILE, scale=None, name="norm_mm_in",
                            hook=gather_hook(placed[1:7], S // ROW_TILE))
    y_att, lse_att, *late = attn_fwd(z, bias, gather_hook(placed[7:], S // ATT_TILE))
    W_ba, W_bs, W_out, W_xq, W_xkv, W_xo, W_fi, W_fo = [whole(g4, i + 1) for i, g4 in enumerate(early + late)]
    y_sg = sgu_fwd(z, amean, lng, lnb, wcat, bfull, tm=512)
    a_br, b_br, merged, x1 = merge_fwd(y_att, y_sg, z, x2d, W_ba, W_bs, W_out, tm=ROW_TILE)
    h2, xq = norm_mm(x1, g_xat, W_xq.reshape(1, D, D), tm=512, scale=XATT_HEAD_DIM ** -0.5, name="norm_mm_xq")
    mn, kv = norm_mm(mem2d, g_mem, W_xkv, tm=mem2d.shape[0], scale=None, name="norm_mm_kv")
    o_x, lse_x, x2 = xattn_fwd(xq, kv, W_xo, x1, tm=512)
    h3, gu = norm_mm(x2, g_ffn, W_fi, tm=ROW_TILE, scale=None, name="norm_mm_ffn")
    act, loss_vec, dx3, dg_fin = ffn_out_loss(gu, W_fo, x2, g_fin, tgt, tm=ROW_TILE)

    dgu = ffn_act_bwd(dx3, gu, W_fo, tm=ROW_TILE, nchunk=2)
    gW_fo = tn_mm(act, dx3, shards=1, tn=D, tk=DW_TOKENS, name="dw_ffn_out")
    gW_fi = tn_mm(h3, dgu, shards=4, tn=2 * W_fi.shape[2], tk=DW_TOKENS, name="dw_ffn_in")
    dx2, dg_ffn = mm_nt_norm_bwd(dgu, W_fi, x2, g_ffn, dx3, tm=ROW_TILE, name="dx_ffn")
    dq_x, dkv = xattn_bwd(dx2, xq, o_x, lse_x, kv, W_xo, tm=512)
    gW_xo = tn_mm(o_x, dx2, shards=1, tn=D, tk=DW_TOKENS, name="dw_xo")
    gW_xq = tn_mm(h2, dq_x, shards=1, tn=D, tk=DW_TOKENS, name="dw_xq")
    dx1, dg_xat = mm_nt_norm_bwd(dq_x, W_xq.reshape(1, D, D), x1, g_xat, dx2, tm=512, name="dx_xq")
    gW_xkv = tn_mm(mn, dkv, shards=4, tn=2 * D, tk=mem2d.shape[0], name="dw_xkv")
    _, dg_mem = mm_nt_norm_bwd(dkv.astype(BF16), W_xkv, mem2d, g_mem, None, tm=mem2d.shape[0], name="dx_mem")
    da, db, dgab, dy_att, dy_sg = merge_bwd(dx1, W_out, a_br, b_br, z, W_ba, W_bs, tm=ROW_TILE)
    gW_out = tn_mm(merged, dx1, shards=1, tn=D, tk=DW_TOKENS, name="dw_out")
    gW_ba = tn_mm(y_att, da, shards=4, tn=D, tk=DW_TOKENS, name="dw_branch_att")
    gW_bs = tn_mm(y_sg, db, shards=4, tn=D, tk=DW_TOKENS, name="dw_branch_sg")

    def canon(g, i):
        R, C = big_w[i].shape
        return g.reshape(4, 2, R // 2, C)

    def stage1(grads, idx, tag):
        gs = [canon(g, i) for g, i in zip(grads, idx)]
        recv = sibling_split(gs, "sibling_split_" + tag)
        return [chip_partial(g, r, core, "chip_partial_" + big_names[i]) for g, r, i in zip(gs, recv, idx)]

    parts_b = stage1([gW_ba, gW_bs, gW_out, gW_xq, gW_xkv, gW_xo, gW_fi, gW_fo], range(1, 9), "late")
    nt_bwd = S // ATT_TILE + ATT_WIN - 1
    dq_a, dk_a, dv_a, dbias, *from_chips_b = attn_bwd(z, y_att, dy_att, lse_att, bias, exchange_hook(parts_b, nt_bwd))
    duv, dwcat, dsgb_full, dlng, dlnb = sgu_bwd(z, dy_sg, amean, lng, lnb, wcat, wtcat, bfull, maskcat, tm=512)
    dwm = jnp.transpose(dwcat.reshape(SG_BLOCK, SG_GROUPS, SG_BLOCK), (1, 0, 2))
    dsgb = dsgb_full[:, ::SG_GROUP_DIM].T
    dz = jnp.concatenate([dq_a, dk_a, dv_a, duv, dgab], axis=1)
    gW_in = tn_mm(h1, dz, shards=4, tn=2 * W_in.shape[2], tk=DW_TOKENS, name="dw_in")
    parts_a = stage1([gW_in], [0], "w_in")
    d_rel = _rel_bias_grad(dbias)
    small_g = [d_rel, dlng, dlnb, dwm, dsgb, dg_xat, dg_mem, dg_ffn, dg_fin, loss_vec[0, :1]]
    n_dx = S // ROW_TILE
    dx, dg_mix, from_chips_a, small_all = mm_nt_norm_bwd(
        dz, W_in, x2d, g_mix, dx1, tm=ROW_TILE, name="dx_in",
        hook=merge_hooks(exchange_hook(parts_a, n_dx), allgather_hook(_pack_small(small_g), n_dx)))
    from_chips_a = [from_chips_a]

    parts, from_chips = parts_a + parts_b, from_chips_a + from_chips_b
    sums = [shard_sum(p, r, own, core, "shard_sum_" + nm) for p, r, nm in zip(parts, from_chips, big_names)]
    joined = sibling_join(sums)
    big_out = []
    for j, w, m_, v_, nm in zip(joined, big_w, big_m, big_v, big_names):
        g = j.reshape(w.shape)
        big_out.append((g,) + tuple(adamw(w, g, m_, v_, "adamw_" + nm)))

    small_w = [norm_mix_g, rel_bias, sg_ln_g, sg_ln_b, sg_w, sg_b, norm_xattn_g, norm_mem_g, norm_ffn_g, norm_final_g]
    small_m = [m_norm_mix_g, m_rel_bias, m_sg_ln_g, m_sg_ln_b, m_sg_w, m_sg_b, m_norm_xattn_g, m_norm_mem_g, m_norm_ffn_g, m_norm_final_g]
    small_v = [v_norm_mix_g, v_rel_bias, v_sg_ln_g, v_sg_ln_b, v_sg_w, v_sg_b, v_norm_xattn_g, v_norm_mem_g, v_norm_ffn_g, v_norm_final_g]
    shapes = [w.shape for w in small_w]
    g_sum = jnp.concatenate([allreduce_small(_pack_small([dg_mix])), sum_blocks(small_all)], axis=0)
    zero = jnp.zeros((1,), F32)
    d_s, m_s, v_s = adamw(_pack_small(small_w + [zero]), g_sum, _pack_small(small_m + [zero]), _pack_small(small_v + [zero]),
                          "adamw_small")
    sg_, sd, sm, sv_ = (_unpack_small(p, shapes + [(1,)]) for p in (g_sum, d_s, m_s, v_s))
    loss = sg_[-1][0]

    order = ["norm_mix_g", "w_in", "rel_bias", "sg_ln_g", "sg_ln_b", "sg_w", "sg_b", "w_branch_att", "w_branch_sg", "w_out",
             "norm_xattn_g", "norm_mem_g", "w_xq", "w_xkv", "w_xo", "norm_ffn_g", "w_ffn_in", "w_ffn_out", "norm_final_g"]
    small_names = ["norm_mix_g", "rel_bias", "sg_ln_g", "sg_ln_b", "sg_w", "sg_b", "norm_xattn_g", "norm_mem_g", "norm_ffn_g",
                   "norm_final_g"]
    res = {}
    for i, nm in enumerate(small_names):
        res[nm] = (sg_[i], sd[i], sm[i], sv_[i])
    for nm, outs in zip(big_names, big_out):
        res[nm] = tuple(o[None] for o in outs)
    return (loss, dx[None], *[res[nm][0] for nm in order], *[res[nm][1] for nm in order],
            *[res[nm][2] for nm in order], *[res[nm][3] for nm in order])
```

```python
import functools
from typing import NamedTuple

import numpy as np
import jax
import jax.numpy as jnp
from jax import lax
from jax.experimental import pallas as pl
from jax.experimental.pallas import tpu as pltpu

F32, BF16 = jnp.float32, jnp.bfloat16
MESH = pl.DeviceIdType.MESH

EPS = 1e-6
NEG_INF = -1e30
CHUNK = 64
N_PREV_CHUNKS = 8
ATT_HEADS, ATT_HEAD_DIM = 8, 64
REL_CLIP = 128
SG_BLOCK, SG_GROUPS, SG_GROUP_DIM = 128, 8, 64
XATT_HEADS, XATT_HEAD_DIM = 4, 256
ATT_TILE = 128
ATT_WIN = 5
ADAM_LR, ADAM_B1, ADAM_B2, ADAM_EPS, ADAM_WD, ADAM_STEP = 0.001, 0.9, 0.999, 1e-08, 0.01, 10

VMEM_LIMIT_V7X = 56 * 1024 * 1024
EW_BLOCK_BYTES = 1 << 20
ROW_TILE = 512
DW_TOKENS = 1024
HBM = pl.BlockSpec(memory_space=pl.ANY)


def _params(*sem):
    return pltpu.CompilerParams(dimension_semantics=sem, vmem_limit_bytes=VMEM_LIMIT_V7X)


def _full(shape):
    n = len(shape)
    return pl.BlockSpec(shape, lambda *_: (0,) * n, pipeline_mode=pl.Buffered(1))


def _acc(shape):
    n = len(shape)
    return pl.BlockSpec(shape, lambda *_: (0,) * n)


class Hook(NamedTuple):
    ins: tuple = ()
    out_shapes: tuple = ()
    aliased: bool = False
    sems: tuple = ()
    steps: tuple = ()


def _run_hook(hook, pos, h_in, h_out, h_sems):
    for step, where, fn in hook.steps:
        if where == pos:
            pl.when(pl.program_id(0) == step)(functools.partial(fn, h_in, h_out, h_sems))


def _hooked_call(body, hook, *, n_in, n_out, in_specs, out_specs, out_shape, scratch_shapes=(), **kw):
    nh = len(hook.ins)
    aliases = {n_in + i: n_out + i for i in range(nh)} if hook.aliased else {}
    return pl.pallas_call(
        body, in_specs=list(in_specs) + [HBM] * nh, out_specs=list(out_specs) + [HBM] * len(hook.out_shapes),
        out_shape=list(out_shape) + list(hook.out_shapes), scratch_shapes=list(scratch_shapes) + list(hook.sems),
        input_output_aliases=aliases, **kw)


def _nt(a, b):
    return lax.dot_general(a, b, (((1,), (1,)), ((), ())), preferred_element_type=F32)


def _tn(a, b):
    return lax.dot_general(a, b, (((0,), (0,)), ((), ())), preferred_element_type=F32)


def _nn(a, b):
    return jnp.dot(a, b, preferred_element_type=F32)


def _sigmoid(x):
    return 1.0 / (1.0 + jnp.exp(-x))


_GELU_C = float(np.sqrt(2.0 / np.pi))


def _gelu(x):
    t = jnp.tanh(_GELU_C * (x + 0.044715 * (x * x * x)))
    return x * (0.5 * (1.0 + t))


def _gelu_grad(x):
    t = jnp.tanh(_GELU_C * (x + 0.044715 * (x * x * x)))
    return 0.5 * (1.0 + t) + 0.5 * x * (1.0 - t * t) * (_GELU_C * (1.0 + 3.0 * 0.044715 * x * x))


def _rms_stats(xf):
    rstd = lax.rsqrt(jnp.mean(xf * xf, axis=-1, keepdims=True) + EPS)
    return rstd, xf * rstd


def _rms_bwd(xhat, rstd, g, dh):
    dxh = dh * g
    dx = rstd * (dxh - xhat * jnp.mean(dxh * xhat, axis=-1, keepdims=True))
    return dx, dh * xhat


def norm_mm(x, g, w, *, tm, scale, name, hook=Hook()):
    S, D = x.shape
    J, _, C = w.shape
    nh, nho = len(hook.ins), len(hook.out_shapes)

    def body(x_ref, g_ref, w_ref, *rest):
        h_in, (h_ref, z_ref), h_out, h_sems = rest[:nh], rest[nh:nh + 2], rest[nh + 2:nh + 2 + nho], rest[nh + 2 + nho:]
        _run_hook(hook, "before", h_in, h_out, h_sems)
        _, xhat = _rms_stats(x_ref[...])
        h = (xhat * g_ref[...]).astype(BF16)
        h_ref[...] = h
        for j in range(J):
            acc = _nn(h, w_ref[j])
            if scale is not None:
                acc = acc * scale
            z_ref[:, j * C:(j + 1) * C] = acc.astype(BF16)
        _run_hook(hook, "after", h_in, h_out, h_sems)

    return _hooked_call(
        body, hook, n_in=3, n_out=2, grid=(S // tm,), name=name,
        in_specs=[pl.BlockSpec((tm, D), lambda i: (i, 0)), _full((1, D)), _full((J, D, C))],
        out_specs=[pl.BlockSpec((tm, D), lambda i: (i, 0)), pl.BlockSpec((tm, J * C), lambda i: (i, 0))],
        out_shape=[jax.ShapeDtypeStruct((S, D), BF16), jax.ShapeDtypeStruct((S, J * C), BF16)],
        compiler_params=_params("arbitrary"),
    )(x, g, w, *hook.ins)


def _att_window_specs(nt, col):
    return [pl.BlockSpec((ATT_TILE, 512), lambda t, j=j: (jnp.clip(t - (ATT_WIN - 1) + j, 0, nt - 1), col))
            for j in range(ATT_WIN)]


ATT_SCALE = ATT_HEAD_DIM ** -0.5


def _att_scores(q_scaled, k, bias, valid):
    return jnp.where(valid, _nt(q_scaled, k) + bias, NEG_INF)


def _att_valid(t):
    kpos = lax.broadcasted_iota(jnp.int32, (ATT_TILE, ATT_WIN * ATT_TILE), 1) + (t - (ATT_WIN - 1)) * ATT_TILE
    return kpos >= 0


def attn_fwd(z, bias, hook=Hook()):
    S = z.shape[0]
    nt = S // ATT_TILE
    n_in, nh, nho = 2 + 2 * ATT_WIN, len(hook.ins), len(hook.out_shapes)

    def body(q_ref, *refs):
        k_refs, v_refs = refs[:ATT_WIN], refs[ATT_WIN:2 * ATT_WIN]
        bias_ref = refs[2 * ATT_WIN]
        rest = refs[2 * ATT_WIN + 1:]
        h_in, (y_ref, lse_ref), h_out = rest[:nh], rest[nh:nh + 2], rest[nh + 2:nh + 2 + nho]
        s_scr, p_scr = rest[nh + 2 + nho:nh + 4 + nho]
        h_sems = rest[nh + 4 + nho:]
        _run_hook(hook, "before", h_in, h_out, h_sems)
        valid = _att_valid(pl.program_id(0))
        heads = [slice(h * ATT_HEAD_DIM, (h + 1) * ATT_HEAD_DIM) for h in range(ATT_HEADS)]
        for h, sl in enumerate(heads):
            k = jnp.concatenate([r[:, sl] for r in k_refs], axis=0)
            s_scr[h] = _att_scores(q_ref[:, sl] * ATT_SCALE, k, bias_ref[h], valid)
        stats = []
        for h in range(ATT_HEADS):
            s = s_scr[h]
            m = jnp.max(s, axis=-1, keepdims=True)
            p = jnp.exp(s - m)
            stats.append((m, jnp.sum(p, axis=-1, keepdims=True)))
            p_scr[h] = p.astype(BF16)
        for h, sl in enumerate(heads):
            m, l = stats[h]
            v = jnp.concatenate([r[:, sl] for r in v_refs], axis=0)
            y_ref[:, sl] = (_nn(p_scr[h], v) / l).astype(BF16)
            lse_ref[:, h:h + 1] = m + jnp.log(l)
        _run_hook(hook, "after", h_in, h_out, h_sems)

    tile = lambda col: pl.BlockSpec((ATT_TILE, 512), lambda t: (t, col))
    return _hooked_call(
        body, hook, n_in=n_in, n_out=2, grid=(nt,), name="attn_fwd",
        in_specs=[tile(0)] + _att_window_specs(nt, 1) + _att_window_specs(nt, 2) + [_full(bias.shape)],
        out_specs=[tile(0), pl.BlockSpec((ATT_TILE, ATT_HEADS), lambda t: (t, 0))],
        out_shape=[jax.ShapeDtypeStruct((S, 512), BF16), jax.ShapeDtypeStruct((S, ATT_HEADS), F32)],
        scratch_shapes=[pltpu.VMEM((ATT_HEADS, ATT_TILE, ATT_WIN * ATT_TILE), F32),
                        pltpu.VMEM((ATT_HEADS, ATT_TILE, ATT_WIN * ATT_TILE), BF16)],
        compiler_params=_params("arbitrary"),
    )(z, *([z] * (2 * ATT_WIN)), bias, *hook.ins)


def _group_mean(x, a_ref):
    hi = x.astype(BF16)
    lo = (x - hi.astype(F32)).astype(BF16)
    return _nn(hi, a_ref[...]) + _nn(lo, a_ref[...])


def _block_diag(x, ref):
    group = lax.broadcasted_iota(jnp.int32, x.shape, 1) // SG_GROUP_DIM
    for g in range(SG_GROUPS):
        ref[g * SG_BLOCK:(g + 1) * SG_BLOCK, :] = jnp.where(group == g, x, 0.0).astype(BF16)


def _sgu_block(zu_ref, zv_ref, rows, a_ref, lng_ref, lnb_ref, wcat_ref, bfull_ref, vbd_ref):
    zu = zu_ref[rows, :].astype(F32)
    zv = zv_ref[rows, :].astype(F32)
    u, v = _gelu(zu), _gelu(zv)
    vc = v - _group_mean(v, a_ref)
    rstd = lax.rsqrt(_group_mean(vc * vc, a_ref) + EPS)
    xhat = vc * rstd
    _block_diag(xhat * lng_ref[...] + lnb_ref[...], vbd_ref)
    sv = _nn(wcat_ref[...], vbd_ref[...]) + bfull_ref[...]
    return zu, zv, u, xhat, rstd, sv


def _sgu_rows(b):
    return pl.ds(pl.multiple_of(b * SG_BLOCK, SG_BLOCK), SG_BLOCK)


def sgu_fwd(z, amean, lng, lnb, wcat, bfull, *, tm):
    S = z.shape[0]

    def body(zu_ref, zv_ref, a_ref, lng_ref, lnb_ref, wcat_ref, bfull_ref, y_ref, vbd_ref):
        def block(b, carry):
            rows = _sgu_rows(b)
            _, _, u, _, _, sv = _sgu_block(zu_ref, zv_ref, rows, a_ref, lng_ref, lnb_ref, wcat_ref, bfull_ref, vbd_ref)
            y_ref[rows, :] = (u * sv).astype(BF16)
            return carry

        lax.fori_loop(0, tm // SG_BLOCK, block, 0)

    tile = lambda col: pl.BlockSpec((tm, 512), lambda i: (i, col))
    smalls = [amean, lng, lnb, wcat, bfull]
    return pl.pallas_call(
        body, grid=(S // tm,), name="sgu_fwd",
        in_specs=[tile(3), tile(4)] + [_full(a.shape) for a in smalls],
        out_specs=tile(0), out_shape=jax.ShapeDtypeStruct((S, 512), BF16),
        scratch_shapes=[pltpu.VMEM((SG_GROUPS * SG_BLOCK, 512), BF16)],
        compiler_params=_params("parallel"),
    )(z, z, *smalls)


def _gate_specs(tm):
    return [pl.BlockSpec((tm, 512), lambda i, c=c: (i, c)) for c in (5, 6, 7, 8)]


def merge_fwd(y_att, y_sg, z, x, w_ba, w_bs, w_out, *, tm):
    S, D = x.shape
    J, _, C = w_ba.shape

    def body(ya_ref, ys_ref, g0, g1, g2, g3, x_ref, wba_ref, wbs_ref, wo_ref, a_ref, b_ref, m_ref, x1_ref):
        ya, ys = ya_ref[...], ys_ref[...]
        a = jnp.concatenate([_nn(ya, wba_ref[j]) for j in range(J)], axis=1)
        b = jnp.concatenate([_nn(ys, wbs_ref[j]) for j in range(J)], axis=1)
        ga = jnp.concatenate([g0[...], g1[...]], axis=1).astype(F32)
        gb = jnp.concatenate([g2[...], g3[...]], axis=1).astype(F32)
        a_ref[...] = a.astype(BF16)
        b_ref[...] = b.astype(BF16)
        merged = (_sigmoid(ga) * a + _sigmoid(gb) * b).astype(BF16)
        m_ref[...] = merged
        x1_ref[...] = x_ref[...] + _nn(merged, wo_ref[...])

    row = lambda n: pl.BlockSpec((tm, n), lambda i: (i, 0))
    return pl.pallas_call(
        body, grid=(S // tm,), name="merge_fwd",
        in_specs=[row(512), row(512)] + _gate_specs(tm) + [row(D), _full(w_ba.shape), _full(w_bs.shape), _full(w_out.shape)],
        out_specs=[row(D)] * 4,
        out_shape=[jax.ShapeDtypeStruct((S, D), BF16)] * 3 + [jax.ShapeDtypeStruct((S, D), F32)],
        compiler_params=_params("parallel"),
    )(y_att, y_sg, z, z, z, z, x, w_ba, w_bs, w_out)


def xattn_fwd(xq, kv, w_xo, x1, *, tm):
    S, D = xq.shape
    dh = XATT_HEAD_DIM

    def body(q_ref, kv_ref, wo_ref, x1_ref, o_ref, lse_ref, x2_ref):
        outs = []
        for h in range(XATT_HEADS):
            s = _nt(q_ref[:, h * dh:(h + 1) * dh], kv_ref[:, h * dh:(h + 1) * dh])
            m = jnp.max(s, axis=-1, keepdims=True)
            p = jnp.exp(s - m)
            l = jnp.sum(p, axis=-1, keepdims=True)
            outs.append((_nn(p.astype(BF16), kv_ref[:, D + h * dh:D + (h + 1) * dh]) / l).astype(BF16))
            lse_ref[:, h:h + 1] = m + jnp.log(l)
        o = jnp.concatenate(outs, axis=1)
        o_ref[...] = o
        x2_ref[...] = x1_ref[...] + _nn(o, wo_ref[...])

    row = lambda n: pl.BlockSpec((tm, n), lambda i: (i, 0))
    return pl.pallas_call(
        body, grid=(S // tm,), name="xattn_fwd",
        in_specs=[row(D), _full(kv.shape), _full(w_xo.shape), row(D)],
        out_specs=[row(D), row(XATT_HEADS), row(D)],
        out_shape=[jax.ShapeDtypeStruct((S, D), BF16), jax.ShapeDtypeStruct((S, XATT_HEADS), F32),
                   jax.ShapeDtypeStruct((S, D), F32)],
        compiler_params=_params("parallel"),
    )(xq, kv, w_xo, x1)


def ffn_out_loss(gu, w, x2, g, target, *, tm):
    S, D = x2.shape
    F = w.shape[0]

    def body(gu_ref, w_ref, x2_ref, g_ref, t_ref, act_ref, loss_ref, dx_ref, dg_ref):
        @pl.when(pl.program_id(0) == 0)
        def _():
            loss_ref[...] = jnp.zeros_like(loss_ref)
            dg_ref[...] = jnp.zeros_like(dg_ref)

        gate = gu_ref[:, :F].astype(F32)
        up = gu_ref[:, F:].astype(F32)
        act = (gate * _sigmoid(gate) * up).astype(BF16)
        act_ref[...] = act
        gv = g_ref[...]
        rstd, xhat = _rms_stats(x2_ref[...] + _nn(act, w_ref[...]))
        err = xhat * gv - t_ref[...]
        loss_ref[...] += 0.5 * jnp.sum(jnp.mean(err * err, axis=-1, keepdims=True))
        dx, dgc = _rms_bwd(xhat, rstd, gv, err * (1.0 / D))
        dx_ref[...] = dx
        dg_ref[...] += jnp.sum(dgc, axis=0, keepdims=True)

    row = lambda n: pl.BlockSpec((tm, n), lambda i: (i, 0))
    return pl.pallas_call(
        body, grid=(S // tm,), name="ffn_out_loss",
        in_specs=[row(2 * F), _full(w.shape), row(D), _full((1, D)), row(D)],
        out_specs=[row(F), _acc((1, 128)), row(D), _acc((1, D))],
        out_shape=[jax.ShapeDtypeStruct((S, F), BF16), jax.ShapeDtypeStruct((1, 128), F32),
                   jax.ShapeDtypeStruct((S, D), F32), jax.ShapeDtypeStruct((1, D), F32)],
        compiler_params=_params("arbitrary"),
    )(gu, w, x2, g, target)


def ffn_act_bwd(dx3, gu, w, *, tm, nchunk):
    S, D = dx3.shape
    F = w.shape[0]
    cn = F // nchunk

    def body(dx_ref, gu_ref, w_ref, dgu_ref):
        dxb = dx_ref[...].astype(BF16)
        for j in range(nchunk):
            dact = _nt(dxb, w_ref[j * cn:(j + 1) * cn, :])
            gate = gu_ref[:, j * cn:(j + 1) * cn].astype(F32)
            up = gu_ref[:, F + j * cn:F + (j + 1) * cn].astype(F32)
            sg = _sigmoid(gate)
            dgu_ref[:, j * cn:(j + 1) * cn] = (dact * up * (sg * (1.0 + gate * (1.0 - sg)))).astype(BF16)
            dgu_ref[:, F + j * cn:F + (j + 1) * cn] = (dact * (gate * sg)).astype(BF16)

    row = lambda n: pl.BlockSpec((tm, n), lambda i: (i, 0))
    return pl.pallas_call(
        body, grid=(S // tm,), name="ffn_act_bwd",
        in_specs=[row(D), row(2 * F), _full(w.shape)], out_specs=row(2 * F),
        out_shape=jax.ShapeDtypeStruct((S, 2 * F), BF16), compiler_params=_params("parallel"),
    )(dx3, gu, w)


def _overlaps(widths, lo, hi):
    out, off = [], 0
    for p, wd in enumerate(widths):
        a, b = max(lo, off), min(hi, off + wd)
        if a < b:
            out.append((p, a - off, b - off, a - lo))
        off += wd
    return out


def tn_mm_pieces(a, pieces, *, shards, per, tk, name):
    K, M = a.shape
    widths = [p.shape[1] for p in pieces]
    C = sum(widths) // shards
    tn = per * C
    nk = K // tk
    n_tiles = shards // per

    def body(a_ref, *refs):
        p_refs, o_ref, acc_ref = refs[:len(pieces)], refs[len(pieces)], refs[len(pieces) + 1]
        n, k = pl.program_id(0), pl.program_id(1)

        @pl.when(k == 0)
        def _():
            acc_ref[...] = jnp.zeros_like(acc_ref)

        av = a_ref[...]
        for tile in range(n_tiles):
            @pl.when(n == tile)
            def _(tile=tile):
                for p, c0, c1, at in _overlaps(widths, tile * tn, (tile + 1) * tn):
                    acc_ref[:, at:at + c1 - c0] += _tn(av, p_refs[p][:, c0:c1])

        @pl.when(k == nk - 1)
        def _():
            for s in range(per):
                o_ref[s] = acc_ref[:, s * C:(s + 1) * C].astype(BF16)

    return pl.pallas_call(
        body, grid=(n_tiles, nk), name=name,
        in_specs=[pl.BlockSpec((tk, M), lambda n, k: (k, 0))] + [pl.BlockSpec((tk, wd), lambda n, k: (k, 0)) for wd in widths],
        out_specs=pl.BlockSpec((per, M, C), lambda n, k: (n, 0, 0)), out_shape=jax.ShapeDtypeStruct((shards, M, C), BF16),
        scratch_shapes=[pltpu.VMEM((M, tn), F32)], compiler_params=_params("parallel", "arbitrary"),
    )(a, *pieces)


def tn_mm(a, b, *, shards, tn, tk, name):
    K, M = a.shape
    N = b.shape[1]
    C = N // shards
    per = tn // C
    nk = K // tk

    def body(a_ref, b_ref, o_ref, acc_ref):
        k = pl.program_id(1)

        @pl.when(k == 0)
        def _():
            acc_ref[...] = jnp.zeros_like(acc_ref)

        acc_ref[...] += _tn(a_ref[...].astype(BF16), b_ref[...].astype(BF16))

        @pl.when(k == nk - 1)
        def _():
            if shards > 1:
                for s in range(per):
                    o_ref[s] = acc_ref[:, s * C:(s + 1) * C].astype(BF16)
            else:
                o_ref[...] = acc_ref[...].astype(BF16)

    if shards > 1:
        out_spec = pl.BlockSpec((per, M, C), lambda n, k: (n, 0, 0))
        out_shape = jax.ShapeDtypeStruct((shards, M, C), BF16)
    else:
        out_spec = pl.BlockSpec((M, tn), lambda n, k: (0, n))
        out_shape = jax.ShapeDtypeStruct((M, N), BF16)
    return pl.pallas_call(
        body, grid=(N // tn, nk), name=name,
        in_specs=[pl.BlockSpec((tk, M), lambda n, k: (k, 0)), pl.BlockSpec((tk, tn), lambda n, k: (k, n))],
        out_specs=out_spec, out_shape=out_shape, scratch_shapes=[pltpu.VMEM((M, tn), F32)],
        compiler_params=_params("parallel", "arbitrary"),
    )(a, b)


def mm_nt_norm_bwd(dy, w, x, g, dx_in, *, tm, name, hook=Hook()):
    S, D = x.shape
    J, _, C = w.shape
    has_in = dx_in is not None
    dys = list(dy) if isinstance(dy, (list, tuple)) else [dy]
    widths = [d.shape[1] for d in dys]
    nd = len(dys)
    n_in, nh, nho = nd + 3 + has_in, len(hook.ins), len(hook.out_shapes)

    def body(*refs):
        dy_refs = refs[:nd]
        w_ref, x_ref, g_ref = refs[nd:nd + 3]
        dxin_ref = refs[nd + 3] if has_in else None
        rest = refs[n_in:]
        h_in, (dx_ref, dg_ref), h_out, h_sems = rest[:nh], rest[nh:nh + 2], rest[nh + 2:nh + 2 + nho], rest[nh + 2 + nho:]
        _run_hook(hook, "before", h_in, h_out, h_sems)

        @pl.when(pl.program_id(0) == 0)
        def _():
            dg_ref[...] = jnp.zeros_like(dg_ref)

        dh = None
        for j in range(J):
            for p, c0, c1, at in _overlaps(widths, j * C, (j + 1) * C):
                part = _nt(dy_refs[p][:, c0:c1], w_ref[j, :, at:at + c1 - c0])
                dh = part if dh is None else dh + part
        rstd, xhat = _rms_stats(x_ref[...])
        dx, dgc = _rms_bwd(xhat, rstd, g_ref[...], dh)
        dx_ref[...] = dx + dxin_ref[...] if has_in else dx
        dg_ref[...] += jnp.sum(dgc, axis=0, keepdims=True)
        _run_hook(hook, "after", h_in, h_out, h_sems)

    row = lambda n: pl.BlockSpec((tm, n), lambda i: (i, 0))
    ins = dys + [w, x, g] + ([dx_in] if has_in else [])
    return _hooked_call(
        body, hook, n_in=n_in, n_out=2, grid=(S // tm,), name=name,
        in_specs=[row(wd) for wd in widths] + [_full(w.shape), row(D), _full((1, D))] + ([row(D)] if has_in else []),
        out_specs=[row(D), _acc((1, D))],
        out_shape=[jax.ShapeDtypeStruct((S, D), F32), jax.ShapeDtypeStruct((1, D), F32)],
        compiler_params=_params("arbitrary"),
    )(*ins, *hook.ins)


def xattn_bwd(dx2, xq, o, lse, kv, w_xo, *, tm):
    S, D = xq.shape
    M = kv.shape[0]
    dh = XATT_HEAD_DIM

    def body(dx_ref, q_ref, o_ref, lse_ref, kv_ref, wo_ref, dq_ref, dkv_ref):
        @pl.when(pl.program_id(0) == 0)
        def _():
            dkv_ref[...] = jnp.zeros_like(dkv_ref)

        do = _nt(dx_ref[...].astype(BF16), wo_ref[...])
        for h in range(XATT_HEADS):
            hs = slice(h * dh, (h + 1) * dh)
            vs = slice(D + h * dh, D + (h + 1) * dh)
            q, k, v = q_ref[:, hs], kv_ref[:, hs], kv_ref[:, vs]
            do_h = do[:, hs]
            do_b = do_h.astype(BF16)
            p = jnp.exp(_nt(q, k) - lse_ref[:, h:h + 1])
            delta = jnp.sum(do_h * o_ref[:, hs].astype(F32), axis=-1, keepdims=True)
            ds = (p * (_nt(do_b, v) - delta)).astype(BF16)
            dq_ref[:, hs] = (_nn(ds, k) * (dh ** -0.5)).astype(BF16)
            dkv_ref[:, hs] += _tn(ds, q)
            dkv_ref[:, vs] += _tn(p.astype(BF16), do_b)

    row = lambda n: pl.BlockSpec((tm, n), lambda i: (i, 0))
    return pl.pallas_call(
        body, grid=(S // tm,), name="xattn_bwd",
        in_specs=[row(D), row(D), row(D), row(XATT_HEADS), _full(kv.shape), _full(w_xo.shape)],
        out_specs=[row(D), _acc((M, 2 * D))],
        out_shape=[jax.ShapeDtypeStruct((S, D), BF16), jax.ShapeDtypeStruct((M, 2 * D), F32)],
        compiler_params=_params("arbitrary"),
    )(dx2, xq, o, lse, kv, w_xo)


def merge_bwd(dx1, w_out, a, b, z, w_ba, w_bs, *, tm):
    S, D = dx1.shape
    J, W, C = w_ba.shape

    def body(dx_ref, wo_ref, a_ref, b_ref, g0, g1, g2, g3, wba_ref, wbs_ref, da_ref, db_ref, dg_ref, dya_ref, dys_ref):
        dm = _nt(dx_ref[...].astype(BF16), wo_ref[...])
        sa = _sigmoid(jnp.concatenate([g0[...], g1[...]], axis=1).astype(F32))
        sb = _sigmoid(jnp.concatenate([g2[...], g3[...]], axis=1).astype(F32))
        dg_ref[:, :D] = (dm * a_ref[...].astype(F32) * (sa * (1.0 - sa))).astype(BF16)
        dg_ref[:, D:] = (dm * b_ref[...].astype(F32) * (sb * (1.0 - sb))).astype(BF16)
        da = (dm * sa).astype(BF16)
        db = (dm * sb).astype(BF16)
        da_ref[...] = da
        db_ref[...] = db
        dya = _nt(da[:, 0:C], wba_ref[0])
        dys = _nt(db[:, 0:C], wbs_ref[0])
        for j in range(1, J):
            dya += _nt(da[:, j * C:(j + 1) * C], wba_ref[j])
            dys += _nt(db[:, j * C:(j + 1) * C], wbs_ref[j])
        dya_ref[...] = dya.astype(BF16)
        dys_ref[...] = dys.astype(BF16)

    row = lambda n: pl.BlockSpec((tm, n), lambda i: (i, 0))
    return pl.pallas_call(
        body, grid=(S // tm,), name="merge_bwd",
        in_specs=[row(D), _full(w_out.shape), row(D), row(D)] + _gate_specs(tm) + [_full(w_ba.shape), _full(w_bs.shape)],
        out_specs=[row(D), row(D), row(2 * D), row(W), row(W)],
        out_shape=[jax.ShapeDtypeStruct((S, D), BF16)] * 2 + [jax.ShapeDtypeStruct((S, 2 * D), BF16)]
        + [jax.ShapeDtypeStruct((S, W), BF16)] * 2,
        compiler_params=_params("parallel"),
    )(dx1, w_out, a, b, z, z, z, z, w_ba, w_bs)


def attn_bwd(z, y, dy, lse, bias, hook=Hook()):
    S = z.shape[0]
    nt = S // ATT_TILE
    back = ATT_WIN - 1
    n_in, nh, nho = 5 + 2 * ATT_WIN, len(hook.ins), len(hook.out_shapes)

    def body(q_ref, *refs):
        k_refs, v_refs = refs[:ATT_WIN], refs[ATT_WIN:2 * ATT_WIN]
        y_ref, dy_ref, lse_ref, bias_ref = refs[2 * ATT_WIN:2 * ATT_WIN + 4]
        rest = refs[2 * ATT_WIN + 4:]
        h_in, (dqkv_ref, dbias_ref), h_out = rest[:nh], rest[nh:nh + 2], rest[nh + 2:nh + 2 + nho]
        dk_acc, dv_acc, dq_buf = rest[nh + 2 + nho:nh + 5 + nho]
        h_sems = rest[nh + 5 + nho:]
        t = pl.program_id(0)
        _run_hook(hook, "before", h_in, h_out, h_sems)

        @pl.when(t == 0)
        def _():
            dbias_ref[...] = jnp.zeros_like(dbias_ref)
            dk_acc[...] = jnp.zeros_like(dk_acc)
            dv_acc[...] = jnp.zeros_like(dv_acc)
            dq_buf[...] = jnp.zeros_like(dq_buf)

        @pl.when(t < nt)
        def _():
            valid = _att_valid(t)
            q_t = (q_ref[...] * ATT_SCALE).astype(F32).T.astype(BF16)
            do_t = dy_ref[...].astype(F32).T.astype(BF16)
            for h in range(ATT_HEADS):
                sl = slice(h * ATT_HEAD_DIM, (h + 1) * ATT_HEAD_DIM)
                k = jnp.concatenate([r[:, sl] for r in k_refs], axis=0)
                v = jnp.concatenate([r[:, sl] for r in v_refs], axis=0)
                q, do_b = q_ref[:, sl] * ATT_SCALE, dy_ref[:, sl]
                p = jnp.exp(_att_scores(q, k, bias_ref[h], valid) - lse_ref[:, h:h + 1])
                delta = jnp.sum(do_b.astype(F32) * y_ref[:, sl].astype(F32), axis=-1, keepdims=True)
                ds = p * (_nt(do_b, v) - delta)
                dbias_ref[h] += ds
                ds_b = ds.astype(BF16)
                dq_buf[t % ATT_WIN, :, sl] = (_nn(ds_b, k) * ATT_SCALE).astype(BF16)
                dk_w = _nn(q_t[sl, :], ds_b)
                dv_w = _nn(do_t[sl, :], p.astype(BF16))
                for i in range(ATT_WIN):
                    slot = (t + 1 + i) % ATT_WIN
                    cols = slice(i * ATT_TILE, (i + 1) * ATT_TILE)
                    if i == back:
                        dk_acc[slot, sl, :] = dk_w[:, cols]
                        dv_acc[slot, sl, :] = dv_w[:, cols]
                    else:
                        dk_acc[slot, sl, :] += dk_w[:, cols]
                        dv_acc[slot, sl, :] += dv_w[:, cols]

        done = (t + 1) % ATT_WIN
        dqkv_ref[:, 0:512] = dq_buf[done]
        dqkv_ref[:, 512:1024] = dk_acc[done].T.astype(BF16)
        dqkv_ref[:, 1024:1536] = dv_acc[done].T.astype(BF16)
        _run_hook(hook, "after", h_in, h_out, h_sems)

    last = nt - 1
    tile = lambda col, n=512: pl.BlockSpec((ATT_TILE, n), lambda t: (jnp.minimum(t, last), col))
    late = pl.BlockSpec((ATT_TILE, 1536), lambda t: (jnp.maximum(t - back, 0), 0))
    return _hooked_call(
        body, hook, n_in=n_in, n_out=2, grid=(nt + back,), name="attn_bwd",
        in_specs=[tile(0)] + _att_window_specs(nt, 1) + _att_window_specs(nt, 2)
        + [tile(0), tile(0), tile(0, ATT_HEADS), _full(bias.shape)],
        out_specs=[late, _acc(bias.shape)],
        out_shape=[jax.ShapeDtypeStruct((S, 1536), BF16), jax.ShapeDtypeStruct(bias.shape, F32)],
        scratch_shapes=[pltpu.VMEM((ATT_WIN, 512, ATT_TILE), F32)] * 2 + [pltpu.VMEM((ATT_WIN, ATT_TILE, 512), BF16)],
        compiler_params=_params("arbitrary"),
    )(z, *([z] * (2 * ATT_WIN)), y, dy, lse, bias, *hook.ins)


def sgu_bwd(z, dy, amean, lng, lnb, wcat, wtcat, bfull, maskcat, *, tm):
    S = z.shape[0]
    n = S // tm

    def body(zu_ref, zv_ref, dy_ref, a_ref, lng_ref, lnb_ref, wcat_ref, wtcat_ref, bfull_ref, mask_ref,
             duv_ref, dw_ref, dsgb_ref, dlng_ref, dlnb_ref, vbd_ref, dbd_ref, w_acc, b_acc, g_acc, s_acc):
        i = pl.program_id(0)

        @pl.when(i == 0)
        def _():
            for r in (w_acc, b_acc, g_acc, s_acc):
                r[...] = jnp.zeros_like(r)

        def block(b, carry):
            rows = _sgu_rows(b)
            zu, zv, u, xhat, rstd, sv = _sgu_block(zu_ref, zv_ref, rows, a_ref, lng_ref, lnb_ref, wcat_ref, bfull_ref, vbd_ref)
            dyv = dy_ref[rows, :].astype(F32)
            duv_ref[rows, 0:512] = (dyv * sv * _gelu_grad(zu)).astype(BF16)
            dsv = dyv * u
            b_acc[...] += dsv
            w_acc[...] += _nt(dsv.astype(BF16), vbd_ref[...])
            _block_diag(dsv, dbd_ref)
            dvn = _nn(wtcat_ref[...], dbd_ref[...])
            g_acc[...] += dvn * xhat
            s_acc[...] += dvn
            dxh = dvn * lng_ref[...]
            dv = rstd * (dxh - _group_mean(dxh, a_ref) - xhat * _group_mean(dxh * xhat, a_ref))
            duv_ref[rows, 512:1024] = (dv * _gelu_grad(zv)).astype(BF16)
            return carry

        lax.fori_loop(0, tm // SG_BLOCK, block, 0)

        @pl.when(i == n - 1)
        def _():
            dw_ref[...] = w_acc[...] * mask_ref[...]
            dsgb_ref[...] = _group_mean(b_acc[...], a_ref) * float(SG_GROUP_DIM)
            dlng_ref[...] = jnp.sum(g_acc[...], axis=0, keepdims=True)
            dlnb_ref[...] = jnp.sum(s_acc[...], axis=0, keepdims=True)

    tile = lambda col: pl.BlockSpec((tm, 512), lambda i: (i, col))
    smalls = [amean, lng, lnb, wcat, wtcat, bfull, maskcat]
    wide = SG_GROUPS * SG_BLOCK
    return pl.pallas_call(
        body, grid=(n,), name="sgu_bwd",
        in_specs=[tile(3), tile(4), tile(0)] + [_full(a.shape) for a in smalls],
        out_specs=[pl.BlockSpec((tm, 1024), lambda i: (i, 0)), _acc((SG_BLOCK, wide)), _acc((SG_BLOCK, 512)),
                   _acc((1, 512)), _acc((1, 512))],
        out_shape=[jax.ShapeDtypeStruct((S, 1024), BF16), jax.ShapeDtypeStruct((SG_BLOCK, wide), F32),
                   jax.ShapeDtypeStruct((SG_BLOCK, 512), F32), jax.ShapeDtypeStruct((1, 512), F32),
                   jax.ShapeDtypeStruct((1, 512), F32)],
        scratch_shapes=[pltpu.VMEM((wide, 512), BF16), pltpu.VMEM((wide, 512), BF16), pltpu.VMEM((SG_BLOCK, wide), F32),
                        pltpu.VMEM((SG_BLOCK, 512), F32), pltpu.VMEM((SG_BLOCK, 512), F32), pltpu.VMEM((SG_BLOCK, 512), F32)],
        compiler_params=_params("arbitrary"),
    )(z, z, dy, *smalls)


def bias_colsum(p):
    H, _, L = p.shape
    near_lo = (ATT_WIN - 1) * ATT_TILE - REL_CLIP + 1
    near_hi = ATT_WIN * ATT_TILE

    def body(p_ref, gw_ref, far_ref):
        k = lax.broadcasted_iota(jnp.int32, (1, L), 1)
        is_far = (k < near_lo) | (k >= near_hi)
        for h in range(H):
            g = jnp.sum(p_ref[h], axis=0, keepdims=True)
            gw_ref[h:h + 1, :] = g
            far_ref[h:h + 1, :] = jnp.zeros((1, 128), F32) + jnp.sum(jnp.where(is_far, g, 0.0))

    return pl.pallas_call(
        body, name="bias_colsum", in_specs=[_full(p.shape)], out_specs=[_acc((H, L)), _acc((H, 128))],
        out_shape=[jax.ShapeDtypeStruct((H, L), F32), jax.ShapeDtypeStruct((H, 128), F32)], grid=(1,),
        compiler_params=_params("arbitrary"),
    )(p)


def _row_tile(rows, cols):
    best = None
    for tr in range(16, rows + 1, 16):
        if rows % tr == 0 and tr * cols * 4 <= EW_BLOCK_BYTES:
            best = tr
    return best if best is not None else rows


def place_shard(w, own, name):
    R, C = w.shape
    r = R // 2
    tr = _row_tile(r, C)
    nr = r // tr

    def body(own_ref, w_ref, o_ref):
        o_ref[...] = w_ref[...].astype(BF16)

    return pl.pallas_call(
        body, name=name,
        grid_spec=pltpu.PrefetchScalarGridSpec(
            num_scalar_prefetch=1, grid=(2, nr),
            in_specs=[pl.BlockSpec((tr, C), lambda h, i, own: (h * nr + i, 0))],
            out_specs=pl.BlockSpec((None, None, tr, C), lambda h, i, own: (own[0], h, i, 0))),
        out_shape=jax.ShapeDtypeStruct((4, 2, r, C), BF16), compiler_params=_params("parallel", "parallel"),
    )(own, w)


def chip_partial(grad, recv, core, name):
    _, _, r, C = grad.shape
    tr = _row_tile(r, C)

    def body(core_ref, g_ref, r_ref, o_ref):
        o_ref[...] = (g_ref[...].astype(F32) + r_ref[...].astype(F32)).astype(BF16)

    spec = pl.BlockSpec((None, tr, C), lambda j, i, core: (j, i, 0))
    return pl.pallas_call(
        body, name=name,
        grid_spec=pltpu.PrefetchScalarGridSpec(
            num_scalar_prefetch=1, grid=(4, r // tr),
            in_specs=[pl.BlockSpec((None, None, tr, C), lambda j, i, core: (j, core[0], i, 0)), spec], out_specs=spec),
        out_shape=jax.ShapeDtypeStruct((4, r, C), BF16), compiler_params=_params("parallel", "parallel"),
    )(core, grad, recv)


def shard_sum(part, recv, own, core, name):
    _, r, C = part.shape
    tr = _row_tile(r, C)

    def body(own_ref, core_ref, p_ref, r0, r1, r2, o_ref):
        o_ref[...] = p_ref[...].astype(F32) + r0[...].astype(F32) + r1[...].astype(F32) + r2[...].astype(F32)

    return pl.pallas_call(
        body, name=name,
        grid_spec=pltpu.PrefetchScalarGridSpec(
            num_scalar_prefetch=2, grid=(r // tr,),
            in_specs=[pl.BlockSpec((None, tr, C), lambda i, own, core: (own[0], i, 0))]
            + [pl.BlockSpec((None, tr, C), lambda i, own, core, k=k: (k, i, 0)) for k in range(3)],
            out_specs=pl.BlockSpec((None, tr, C), lambda i, own, core: (core[0], i, 0))),
        out_shape=jax.ShapeDtypeStruct((2, r, C), F32), compiler_params=_params("parallel"),
    )(own, core, part, recv, recv, recv)


def adamw(w, g, m, v, name):
    R, C = w.shape
    tr = _row_tile(R, C)

    def body(w_ref, g_ref, m_ref, v_ref, d_ref, nm_ref, nv_ref):
        gv = g_ref[...]
        nm = ADAM_B1 * m_ref[...] + (1.0 - ADAM_B1) * gv
        nv = ADAM_B2 * v_ref[...] + (1.0 - ADAM_B2) * (gv * gv)
        m_hat = nm / (1.0 - ADAM_B1 ** ADAM_STEP)
        v_hat = nv / (1.0 - ADAM_B2 ** ADAM_STEP)
        d_ref[...] = -ADAM_LR * (m_hat / (jnp.sqrt(v_hat) + ADAM_EPS) + ADAM_WD * w_ref[...])
        nm_ref[...] = nm
        nv_ref[...] = nv

    spec = pl.BlockSpec((tr, C), lambda i: (i, 0))
    return pl.pallas_call(
        body, grid=(R // tr,), name=name, in_specs=[spec] * 4, out_specs=[spec] * 3,
        out_shape=[jax.ShapeDtypeStruct((R, C), F32)] * 3, compiler_params=_params("parallel"),
    )(w, g, m, v)


HBM = pl.BlockSpec(memory_space=pl.ANY)


def _place():
    x, y, c = lax.axis_index("x"), lax.axis_index("y"), lax.axis_index("c")
    chips = [(1 - x, y), (x, 1 - y), (1 - x, 1 - y)]
    return x, y, c, chips


def _gather_phases(n):
    def copies(buf, sems, kind):
        send_sems, recv_sems = sems
        x, y, c, chips = _place()

        def remote(w, k, slot, to):
            return pltpu.make_async_remote_copy(src_ref=slot, dst_ref=slot, send_sem=send_sems.at[w, k],
                                                recv_sem=recv_sems.at[w, k], device_id=to, device_id_type=MESH)

        def one(w, k, px, py):
            if kind == "mine":
                return remote(w, k, buf[w].at[2 * x + y, c], (px, py, c))
            if kind == "theirs":
                return remote(w, k, buf[w].at[2 * px + py, c], (px, py, c))
            if kind == "onward":
                return remote(w, 3 + k, buf[w].at[2 * px + py, c], (x, y, 1 - c))
            return remote(w, 3 + k, buf[w].at[2 * px + py, 1 - c], (x, y, 1 - c))

        return [one(w, k, px, py) for w in range(n) for k, (px, py) in enumerate(chips)]

    def start(_, buf, sems):
        for cp in copies(buf, sems, "mine"):
            cp.start()

    def relay(_, buf, sems):
        for theirs, onward in zip(copies(buf, sems, "theirs"), copies(buf, sems, "onward")):
            theirs.wait_recv()
            onward.start()

    def finish(_, buf, sems):
        for cp in copies(buf, sems, "relayed"):
            cp.wait_recv()
        for cp in copies(buf, sems, "mine") + copies(buf, sems, "onward"):
            cp.wait_send()

    return start, relay, finish


def _gather_sems(n):
    return (pltpu.SemaphoreType.DMA((n, 6)), pltpu.SemaphoreType.DMA((n, 6)))


def gather_weights(bufs):
    n = len(bufs)
    phases = _gather_phases(n)

    def body(*refs):
        for phase in phases:
            phase(None, refs[n:2 * n], refs[2 * n:])

    return pl.pallas_call(
        body, name="gather_weights", in_specs=[HBM] * n, out_specs=[HBM] * n,
        out_shape=[jax.ShapeDtypeStruct(b.shape, b.dtype) for b in bufs],
        input_output_aliases={i: i for i in range(n)}, scratch_shapes=list(_gather_sems(n)),
    )(*bufs)


def gather_hook(bufs, n_steps):
    start, relay, finish = _gather_phases(len(bufs))
    return Hook(ins=tuple(bufs), out_shapes=tuple(jax.ShapeDtypeStruct(b.shape, b.dtype) for b in bufs), aliased=True,
                sems=_gather_sems(len(bufs)),
                steps=((0, "before", start), ((n_steps * 13) // 16, "before", relay), (n_steps - 1, "after", finish)))


def sibling_split(grads, name):
    n = len(grads)

    def body(*refs):
        src, dst = refs[:n], refs[n:2 * n]
        send_sems, recv_sems = refs[2 * n:]
        x, y, c, _ = _place()
        sends = [pltpu.make_async_remote_copy(src_ref=src[w].at[:, 1 - c], dst_ref=dst[w], send_sem=send_sems.at[w],
                                              recv_sem=recv_sems.at[w], device_id=(x, y, 1 - c), device_id_type=MESH)
                 for w in range(n)]
        for cp in sends:
            cp.start()
        for cp in sends:
            cp.wait()

    return pl.pallas_call(
        body, name=name, in_specs=[HBM] * n, out_specs=[HBM] * n,
        out_shape=[jax.ShapeDtypeStruct((4,) + g.shape[2:], g.dtype) for g in grads],
        scratch_shapes=[pltpu.SemaphoreType.DMA((n,))] * 2,
    )(*grads)


def exchange_hook(parts, n_steps):
    n = len(parts)

    def copies(src, dst, sems):
        send_sems, recv_sems = sems
        _, _, c, chips = _place()
        return [pltpu.make_async_remote_copy(
            src_ref=src[w].at[2 * px + py], dst_ref=dst[w].at[k], send_sem=send_sems.at[w, k],
            recv_sem=recv_sems.at[w, k], device_id=(px, py, c), device_id_type=MESH)
            for w in range(n) for k, (px, py) in enumerate(chips)]

    def start(src, dst, sems):
        for cp in copies(src, dst, sems):
            cp.start()

    def finish(src, dst, sems):
        for cp in copies(src, dst, sems):
            cp.wait()

    return Hook(ins=tuple(parts), out_shapes=tuple(jax.ShapeDtypeStruct((3,) + p.shape[1:], p.dtype) for p in parts),
                sems=(pltpu.SemaphoreType.DMA((n, 3)), pltpu.SemaphoreType.DMA((n, 3))),
                steps=((0, "before", start), (n_steps - 1, "after", finish)))


def sibling_join(sums):
    n = len(sums)

    def body(*refs):
        buf = refs[n:2 * n]
        send_sems, recv_sems = refs[2 * n:]
        x, y, c, _ = _place()
        sends = [pltpu.make_async_remote_copy(src_ref=buf[w].at[c], dst_ref=buf[w].at[c], send_sem=send_sems.at[w],
                                              recv_sem=recv_sems.at[w], device_id=(x, y, 1 - c), device_id_type=MESH)
                 for w in range(n)]
        for cp in sends:
            cp.start()
        for w, cp in enumerate(sends):
            cp.wait_send()
            pltpu.make_async_remote_copy(src_ref=buf[w].at[c], dst_ref=buf[w].at[1 - c], send_sem=send_sems.at[w],
                                         recv_sem=recv_sems.at[w], device_id=(x, y, 1 - c), device_id_type=MESH).wait_recv()

    return pl.pallas_call(
        body, name="sibling_join", in_specs=[HBM] * n, out_specs=[HBM] * n,
        out_shape=[jax.ShapeDtypeStruct(s.shape, s.dtype) for s in sums],
        input_output_aliases={i: i for i in range(n)},
        scratch_shapes=[pltpu.SemaphoreType.DMA((n,))] * 2,
    )(*sums)


def allreduce_small(v):
    R, C = v.shape

    def body(v_ref, out_ref, all_ref, send_sems, recv_sems):
        x, y, c, chips = _place()
        me, sibling = (x, y, c), (x, y, 1 - c)

        def slot(px, py, pc):
            return all_ref.at[4 * px + 2 * py + pc]

        def copy(k, block, to, src=None):
            return pltpu.make_async_remote_copy(src_ref=slot(*block) if src is None else src, dst_ref=slot(*block),
                                                send_sem=send_sems.at[k], recv_sem=recv_sems.at[k], device_id=to,
                                                device_id_type=MESH)

        first = [copy(0, me, sibling, src=v_ref)] + [copy(1 + j, me, (*chip, c), src=v_ref) for j, chip in enumerate(chips)]
        for cp in first:
            cp.start()
        slot(*me)[...] = v_ref[...]
        passed = [copy(4 + j, (*chip, c), sibling) for j, chip in enumerate(chips)]
        for j, chip in enumerate(chips):
            copy(1 + j, (*chip, c), me).wait_recv()
            passed[j].start()
        copy(0, sibling, me).wait_recv()
        for j, chip in enumerate(chips):
            copy(4 + j, (*chip, 1 - c), me).wait_recv()
        for cp in first + passed:
            cp.wait_send()
        acc = all_ref[0]
        for k in range(1, 8):
            acc = acc + all_ref[k]
        out_ref[...] = acc

    vmem = pl.BlockSpec(memory_space=pltpu.VMEM)
    return pl.pallas_call(
        body, name="allreduce_small", in_specs=[vmem], out_specs=vmem, out_shape=jax.ShapeDtypeStruct((R, C), F32),
        scratch_shapes=[pltpu.VMEM((8, R, C), F32), pltpu.SemaphoreType.DMA((7,)), pltpu.SemaphoreType.DMA((7,))],
        compiler_params=pltpu.CompilerParams(vmem_limit_bytes=VMEM_LIMIT_V7X),
    )(v)


def allgather_hook(v, n_steps):
    R, C = v.shape

    def copies(v_ref, out, sems, kind):
        send_sems, recv_sems, _ = sems
        x, y, c, chips = _place()
        me, sibling = (x, y, c), (x, y, 1 - c)

        def slot(px, py, pc):
            return out.at[4 * px + 2 * py + pc]

        def copy(k, block, to, src=None):
            return pltpu.make_async_remote_copy(src_ref=slot(*block) if src is None else src, dst_ref=slot(*block),
                                                send_sem=send_sems.at[k], recv_sem=recv_sems.at[k], device_id=to,
                                                device_id_type=MESH)

        if kind == "mine":
            return [copy(0, me, sibling, src=v_ref)] + [copy(1 + j, me, (*chip, c), src=v_ref) for j, chip in enumerate(chips)]
        if kind == "theirs":
            return [copy(1 + j, (*chip, c), me) for j, chip in enumerate(chips)]
        if kind == "onward":
            return [copy(4 + j, (*chip, c), sibling) for j, chip in enumerate(chips)]
        return [copy(0, sibling, me)] + [copy(4 + j, (*chip, 1 - c), me) for j, chip in enumerate(chips)]

    def own(v_ref, out, sems):
        x, y, c, _ = _place()
        return pltpu.make_async_copy(v_ref, out.at[4 * x + 2 * y + c], sems[2])

    def start(ins, outs, sems):
        own(ins[0], outs[0], sems).start()
        for cp in copies(ins[0], outs[0], sems, "mine"):
            cp.start()

    def relay(ins, outs, sems):
        for theirs, onward in zip(copies(ins[0], outs[0], sems, "theirs"), copies(ins[0], outs[0], sems, "onward")):
            theirs.wait_recv()
            onward.start()

    def finish(ins, outs, sems):
        for cp in copies(ins[0], outs[0], sems, "relayed"):
            cp.wait_recv()
        for cp in copies(ins[0], outs[0], sems, "mine") + copies(ins[0], outs[0], sems, "onward"):
            cp.wait_send()
        own(ins[0], outs[0], sems).wait()

    return Hook(ins=(v,), out_shapes=(jax.ShapeDtypeStruct((8, R, C), F32),),
                sems=(pltpu.SemaphoreType.DMA((7,)), pltpu.SemaphoreType.DMA((7,)), pltpu.SemaphoreType.DMA(())),
                steps=((0, "before", start), (n_steps // 2, "before", relay), (n_steps - 1, "after", finish)))


def merge_hooks(a, b):
    assert not a.aliased and not b.aliased
    cut = (len(a.ins), len(a.out_shapes), len(a.sems))

    def part(fn, second):
        def run(ins, outs, sems):
            if second:
                fn(ins[cut[0]:], outs[cut[1]:], sems[cut[2]:])
            else:
                fn(ins[:cut[0]], outs[:cut[1]], sems[:cut[2]])
        return run

    steps = tuple((s, w, part(fn, False)) for s, w, fn in a.steps) + tuple((s, w, part(fn, True)) for s, w, fn in b.steps)
    return Hook(a.ins + b.ins, a.out_shapes + b.out_shapes, False, a.sems + b.sems, steps)


def sum_blocks(g):
    n, R, C = g.shape

    def body(g_ref, o_ref):
        acc = g_ref[0]
        for k in range(1, n):
            acc = acc + g_ref[k]
        o_ref[...] = acc

    return pl.pallas_call(body, name="sum_blocks", grid=(1,), in_specs=[_full(g.shape)], out_specs=_acc((R, C)),
                          out_shape=jax.ShapeDtypeStruct((R, C), F32), compiler_params=_params("arbitrary"))(g)


SMALL_PAD = 1024


def _pack_small(arrs):
    parts = []
    for a in arrs:
        f = a.reshape(-1).astype(F32)
        parts.append(jnp.pad(f, (0, (-f.shape[0]) % SMALL_PAD)))
    return jnp.concatenate(parts).reshape(-1, 128)


def _unpack_small(packed, shapes):
    flat = packed.reshape(-1)
    outs, off = [], 0
    for s in shapes:
        size = int(np.prod(s))
        outs.append(flat[off:off + size].reshape(s))
        off += size + (-size) % SMALL_PAD
    return outs


ATT_KEYS = ATT_WIN * ATT_TILE
ATT_NEAR_LO = (ATT_WIN - 1) * ATT_TILE - REL_CLIP + 1
ATT_PERIOD = ATT_KEYS + ATT_TILE + 1


def _att_bias_table(rel_bias):
    far = rel_bias[:, 2 * REL_CLIP:]
    near = rel_bias[:, 2 * REL_CLIP - 1:0:-1]
    w = jnp.concatenate([jnp.broadcast_to(far, (ATT_HEADS, ATT_NEAR_LO)), near,
                         jnp.broadcast_to(far, (ATT_HEADS, ATT_PERIOD - ATT_KEYS))], axis=1)
    rows = jnp.tile(w, (1, ATT_TILE))[:, :ATT_TILE * (ATT_PERIOD - 1)].reshape(ATT_HEADS, ATT_TILE, ATT_PERIOD - 1)
    i = np.arange(ATT_TILE)[:, None]
    m = np.arange(ATT_KEYS)[None, :]
    qc, kc = i // CHUNK, m // CHUNK
    band = (kc >= qc) & (kc <= qc + N_PREV_CHUNKS)
    return jnp.where(band[None], rows[:, :, :ATT_KEYS], NEG_INF)


def _rel_bias_grad(dbias):
    p = jnp.pad(dbias, ((0, 0), (0, 0), (0, ATT_PERIOD - 1 - ATT_KEYS))).reshape(ATT_HEADS, -1)
    p = jnp.pad(p, ((0, 0), (0, ATT_TILE))).reshape(ATT_HEADS, ATT_TILE, ATT_PERIOD)
    gw, far = bias_colsum(p)
    return jnp.concatenate([jnp.zeros((ATT_HEADS, 1), F32), gw[:, ATT_KEYS - 1:ATT_NEAR_LO - 1:-1], far[:, :1]], axis=1)


def kernel(x, mem, norm_mix_g, w_in, rel_bias, sg_ln_g, sg_ln_b, sg_w, sg_b, w_branch_att, w_branch_sg, w_out, norm_xattn_g, norm_mem_g, w_xq, w_xkv, w_xo, norm_ffn_g, w_ffn_in, w_ffn_out, norm_final_g, loss_target, m_norm_mix_g, m_w_in, m_rel_bias, m_sg_ln_g, m_sg_ln_b, m_sg_w, m_sg_b, m_w_branch_att, m_w_branch_sg, m_w_out, m_norm_xattn_g, m_norm_mem_g, m_w_xq, m_w_xkv, m_w_xo, m_norm_ffn_g, m_w_ffn_in, m_w_ffn_out, m_norm_final_g, v_norm_mix_g, v_w_in, v_rel_bias, v_sg_ln_g, v_sg_ln_b, v_sg_w, v_sg_b, v_w_branch_att, v_w_branch_sg, v_w_out, v_norm_xattn_g, v_norm_mem_g, v_w_xq, v_w_xkv, v_w_xo, v_norm_ffn_g, v_w_ffn_in, v_w_ffn_out, v_norm_final_g):
    S, D = x.shape[1], x.shape[2]
    x2d, mem2d, tgt = x[0], mem[0], loss_target[0]

    big_names = ["w_in", "w_branch_att", "w_branch_sg", "w_out", "w_xq", "w_xkv", "w_xo", "w_ffn_in", "w_ffn_out"]
    col_sharded = [True, True, True, False, False, True, False, True, False]
    big_w = [a[0] for a in (w_in, w_branch_att, w_branch_sg, w_out, w_xq, w_xkv, w_xo, w_ffn_in, w_ffn_out)]
    big_m = [a[0] for a in (m_w_in, m_w_branch_att, m_w_branch_sg, m_w_out, m_w_xq, m_w_xkv, m_w_xo, m_w_ffn_in, m_w_ffn_out)]
    big_v = [a[0] for a in (v_w_in, v_w_branch_att, v_w_branch_sg, v_w_out, v_w_xq, v_w_xkv, v_w_xo, v_w_ffn_in, v_w_ffn_out)]

    own = (2 * lax.axis_index("x") + lax.axis_index("y")).astype(jnp.int32).reshape(1)
    core = lax.axis_index("c").astype(jnp.int32).reshape(1)
    placed = [place_shard(w, own, "place_" + nm) for w, nm in zip(big_w, big_names)]

    def whole(g4, i):
        R, C = big_w[i].shape
        return g4.reshape(4, R, C) if col_sharded[i] else g4.reshape(4 * R, C)

    W_in = whole(gather_weights(placed[:1])[0], 0)

    g_mix, g_xat, g_mem, g_ffn = norm_mix_g, norm_xattn_g, norm_mem_g, norm_ffn_g
    g_fin = norm_final_g.reshape(1, D)
    bias = _att_bias_table(rel_bias[0])
    tt = np.arange(SG_BLOCK)
    sg_mask = (tt[None, :] // CHUNK) <= (tt[:, None] // CHUNK)
    wm_f = jnp.where(sg_mask[None], sg_w[0], 0.0)
    wide = SG_GROUPS * SG_BLOCK
    wcat = jnp.transpose(wm_f, (1, 0, 2)).reshape(SG_BLOCK, wide).astype(BF16)
    wtcat = jnp.transpose(wm_f, (2, 0, 1)).reshape(SG_BLOCK, wide).astype(BF16)
    maskcat = jnp.asarray(np.tile(sg_mask, (1, SG_GROUPS)), F32)
    bfull = jnp.repeat(sg_b[0].T, SG_GROUP_DIM, axis=1)
    lng, lnb = sg_ln_g[0].reshape(1, 512), sg_ln_b[0].reshape(1, 512)
    gid = np.arange(512) // SG_GROUP_DIM
    amean = jnp.asarray((gid[:, None] == gid[None, :]) / SG_GROUP_DIM, BF16)

    h1, z, *early = norm_mm(x2d, g_mix, W_in, tm=ROW_TILE, scale=None, name="norm_mm_in",
                            hook=gather_hook(placed[1:7], S // ROW_TILE))
    y_att, lse_att, *late = attn_fwd(z, bias, gather_hook(placed[7:], S // ATT_TILE))
    W_ba, W_bs, W_out, W_xq, W_xkv, W_xo, W_fi, W_fo = [whole(g4, i + 1) for i, g4 in enumerate(early + late)]
    y_sg = sgu_fwd(z, amean, lng, lnb, wcat, bfull, tm=512)
    a_br, b_br, merged, x1 = merge_fwd(y_att, y_sg, z, x2d, W_ba, W_bs, W_out, tm=ROW_TILE)
    h2, xq = norm_mm(x1, g_xat, W_xq.reshape(1, D, D), tm=512, scale=XATT_HEAD_DIM ** -0.5, name="norm_mm_xq")
    mn, kv = norm_mm(mem2d, g_mem, W_xkv, tm=mem2d.shape[0], scale=None, name="norm_mm_kv")
    o_x, lse_x, x2 = xattn_fwd(xq, kv, W_xo, x1, tm=512)
    h3, gu = norm_mm(x2, g_ffn, W_fi, tm=ROW_TILE, scale=None, name="norm_mm_ffn")
    act, loss_vec, dx3, dg_fin = ffn_out_loss(gu, W_fo, x2, g_fin, tgt, tm=ROW_TILE)

    dgu = ffn_act_bwd(dx3, gu, W_fo, tm=ROW_TILE, nchunk=2)
    gW_fo = tn_mm(act, dx3, shards=1, tn=D, tk=DW_TOKENS, name="dw_ffn_out")
    gW_fi = tn_mm(h3, dgu, shards=4, tn=2 * W_fi.shape[2], tk=DW_TOKENS, name="dw_ffn_in")
    dx2, dg_ffn = mm_nt_norm_bwd(dgu, W_fi, x2, g_ffn, dx3, tm=ROW_TILE, name="dx_ffn")
    dq_x, dkv = xattn_bwd(dx2, xq, o_x, lse_x, kv, W_xo, tm=512)
    gW_xo = tn_mm(o_x, dx2, shards=1, tn=D, tk=DW_TOKENS, name="dw_xo")
    gW_xq = tn_mm(h2, dq_x, shards=1, tn=D, tk=DW_TOKENS, name="dw_xq")
    dx1, dg_xat = mm_nt_norm_bwd(dq_x, W_xq.reshape(1, D, D), x1, g_xat, dx2, tm=512, name="dx_xq")
    gW_xkv = tn_mm(mn, dkv, shards=4, tn=2 * D, tk=mem2d.shape[0], name="dw_xkv")
    _, dg_mem = mm_nt_norm_bwd(dkv.astype(BF16), W_xkv, mem2d, g_mem, None, tm=mem2d.shape[0], name="dx_mem")
    da, db, dgab, dy_att, dy_sg = merge_bwd(dx1, W_out, a_br, b_br, z, W_ba, W_bs, tm=ROW_TILE)
    gW_out = tn_mm(merged, dx1, shards=1, tn=D, tk=DW_TOKENS, name="dw_out")
    gW_ba = tn_mm(y_att, da, shards=4, tn=D, tk=DW_TOKENS, name="dw_branch_att")
    gW_bs = tn_mm(y_sg, db, shards=4, tn=D, tk=DW_TOKENS, name="dw_branch_sg")

    def canon(g, i):
        R, C = big_w[i].shape
        return g.reshape(4, 2, R // 2, C)

    def stage1(grads, idx, tag):
        gs = [canon(g, i) for g, i in zip(grads, idx)]
        recv = sibling_split(gs, "sibling_split_" + tag)
        return [chip_partial(g, r, core, "chip_partial_" + big_names[i]) for g, r, i in zip(gs, recv, idx)]

    parts_b = stage1([gW_ba, gW_bs, gW_out, gW_xq, gW_xkv, gW_xo, gW_fi, gW_fo], range(1, 9), "late")
    nt_bwd = S // ATT_TILE + ATT_WIN - 1
    dqkv, dbias, *from_chips_b = attn_bwd(z, y_att, dy_att, lse_att, bias, exchange_hook(parts_b, nt_bwd))
    duv, dwcat, dsgb_full, dlng, dlnb = sgu_bwd(z, dy_sg, amean, lng, lnb, wcat, wtcat, bfull, maskcat, tm=512)
    dwm = jnp.transpose(dwcat.reshape(SG_BLOCK, SG_GROUPS, SG_BLOCK), (1, 0, 2))
    dsgb = dsgb_full[:, ::SG_GROUP_DIM].T
    dz = [dqkv, duv, dgab]
    gW_in = tn_mm_pieces(h1, dz, shards=4, per=2, tk=DW_TOKENS, name="dw_in")
    parts_a = stage1([gW_in], [0], "w_in")
    d_rel = _rel_bias_grad(dbias)
    small_g = [d_rel, dlng, dlnb, dwm, dsgb, dg_xat, dg_mem, dg_ffn, dg_fin, loss_vec[0, :1]]
    n_dx = S // ROW_TILE
    dx, dg_mix, from_chips_a, small_all = mm_nt_norm_bwd(
        dz, W_in, x2d, g_mix, dx1, tm=ROW_TILE, name="dx_in",
        hook=merge_hooks(exchange_hook(parts_a, n_dx), allgather_hook(_pack_small(small_g), n_dx)))
    from_chips_a = [from_chips_a]

    parts, from_chips = parts_a + parts_b, from_chips_a + from_chips_b
    sums = [shard_sum(p, r, own, core, "shard_sum_" + nm) for p, r, nm in zip(parts, from_chips, big_names)]
    joined = sibling_join(sums)
    big_out = []
    for j, w, m_, v_, nm in zip(joined, big_w, big_m, big_v, big_names):
        g = j.reshape(w.shape)
        big_out.append((g,) + tuple(adamw(w, g, m_, v_, "adamw_" + nm)))

    small_w = [norm_mix_g, rel_bias, sg_ln_g, sg_ln_b, sg_w, sg_b, norm_xattn_g, norm_mem_g, norm_ffn_g, norm_final_g]
    small_m = [m_norm_mix_g, m_rel_bias, m_sg_ln_g, m_sg_ln_b, m_sg_w, m_sg_b, m_norm_xattn_g, m_norm_mem_g, m_norm_ffn_g, m_norm_final_g]
    small_v = [v_norm_mix_g, v_rel_bias, v_sg_ln_g, v_sg_ln_b, v_sg_w, v_sg_b, v_norm_xattn_g, v_norm_mem_g, v_norm_ffn_g, v_norm_final_g]
    shapes = [w.shape for w in small_w]
    g_sum = jnp.concatenate([allreduce_small(_pack_small([dg_mix])), sum_blocks(small_all)], axis=0)
    zero = jnp.zeros((1,), F32)
    d_s, m_s, v_s = adamw(_pack_small(small_w + [zero]), g_sum, _pack_small(small_m + [zero]), _pack_small(small_v + [zero]),
                          "adamw_small")
    sg_, sd, sm, sv_ = (_unpack_small(p, shapes + [(1,)]) for p in (g_sum, d_s, m_s, v_s))
    loss = sg_[-1][0]

    order = ["norm_mix_g", "w_in", "rel_bias", "sg_ln_g", "sg_ln_b", "sg_w", "sg_b", "w_branch_att", "w_branch_sg", "w_out",
             "norm_xattn_g", "norm_mem_g", "w_xq", "w_xkv", "w_xo", "norm_ffn_g", "w_ffn_in", "w_ffn_out", "norm_final_g"]
    small_names = ["norm_mix_g", "rel_bias", "sg_ln_g", "sg_ln_b", "sg_w", "sg_b", "norm_xattn_g", "norm_mem_g", "norm_ffn_g",
                   "norm_final_g"]
    res = {}
    for i, nm in enumerate(small_names):
        res[nm] = (sg_[i], sd[i], sm[i], sv_[i])
    for nm, outs in zip(big_names, big_out):
        res[nm] = tuple(o[None] for o in outs)
    return (loss, dx[None], *[res[nm][0] for nm in order], *[res[nm][1] for nm in order],
            *[res[nm][2] for nm in order], *[res[nm][3] for nm in order])
```

```python
import functools
from typing import NamedTuple

import numpy as np
import jax
import jax.numpy as jnp
from jax import lax
from jax.experimental import pallas as pl
from jax.experimental.pallas import tpu as pltpu

F32, BF16 = jnp.float32, jnp.bfloat16
MESH = pl.DeviceIdType.MESH

EPS = 1e-6
NEG_INF = -1e30
CHUNK = 64
N_PREV_CHUNKS = 8
ATT_HEADS, ATT_HEAD_DIM = 8, 64
REL_CLIP = 128
SG_BLOCK, SG_GROUPS, SG_GROUP_DIM = 128, 8, 64
XATT_HEADS, XATT_HEAD_DIM = 4, 256
ATT_TILE = 128
ATT_WIN = 5
ADAM_LR, ADAM_B1, ADAM_B2, ADAM_EPS, ADAM_WD, ADAM_STEP = 0.001, 0.9, 0.999, 1e-08, 0.01, 10

VMEM_LIMIT_V7X = 56 * 1024 * 1024
EW_BLOCK_BYTES = 1 << 20
ROW_TILE = 512
DW_TOKENS = 1024
HBM = pl.BlockSpec(memory_space=pl.ANY)


def _params(*sem):
    return pltpu.CompilerParams(dimension_semantics=sem, vmem_limit_bytes=VMEM_LIMIT_V7X)


def _full(shape):
    n = len(shape)
    return pl.BlockSpec(shape, lambda *_: (0,) * n, pipeline_mode=pl.Buffered(1))


def _acc(shape):
    n = len(shape)
    return pl.BlockSpec(shape, lambda *_: (0,) * n)


class Hook(NamedTuple):
    ins: tuple = ()
    out_shapes: tuple = ()
    aliased: bool = False
    sems: tuple = ()
    steps: tuple = ()


def _run_hook(hook, pos, h_in, h_out, h_sems):
    for step, where, fn in hook.steps:
        if where == pos:
            pl.when(pl.program_id(0) == step)(functools.partial(fn, h_in, h_out, h_sems))


def _hooked_call(body, hook, *, n_in, n_out, in_specs, out_specs, out_shape, scratch_shapes=(), **kw):
    nh = len(hook.ins)
    aliases = {n_in + i: n_out + i for i in range(nh)} if hook.aliased else {}
    return pl.pallas_call(
        body, in_specs=list(in_specs) + [HBM] * nh, out_specs=list(out_specs) + [HBM] * len(hook.out_shapes),
        out_shape=list(out_shape) + list(hook.out_shapes), scratch_shapes=list(scratch_shapes) + list(hook.sems),
        input_output_aliases=aliases, **kw)


def _nt(a, b):
    return lax.dot_general(a, b, (((1,), (1,)), ((), ())), preferred_element_type=F32)


def _tn(a, b):
    return lax.dot_general(a, b, (((0,), (0,)), ((), ())), preferred_element_type=F32)


def _nn(a, b):
    return jnp.dot(a, b, preferred_element_type=F32)


def _sigmoid(x):
    return 1.0 / (1.0 + jnp.exp(-x))


_GELU_C = float(np.sqrt(2.0 / np.pi))


def _gelu(x):
    t = jnp.tanh(_GELU_C * (x + 0.044715 * (x * x * x)))
    return x * (0.5 * (1.0 + t))


def _gelu_grad(x):
    t = jnp.tanh(_GELU_C * (x + 0.044715 * (x * x * x)))
    return 0.5 * (1.0 + t) + 0.5 * x * (1.0 - t * t) * (_GELU_C * (1.0 + 3.0 * 0.044715 * x * x))


def _rms_stats(xf):
    rstd = lax.rsqrt(jnp.mean(xf * xf, axis=-1, keepdims=True) + EPS)
    return rstd, xf * rstd


def _rms_bwd(xhat, rstd, g, dh):
    dxh = dh * g
    dx = rstd * (dxh - xhat * jnp.mean(dxh * xhat, axis=-1, keepdims=True))
    return dx, dh * xhat


def norm_mm(x, g, w, *, tm, scale, name, hook=Hook()):
    S, D = x.shape
    J, _, C = w.shape
    nh, nho = len(hook.ins), len(hook.out_shapes)

    def body(x_ref, g_ref, w_ref, *rest):
        h_in, (h_ref, z_ref), h_out, h_sems = rest[:nh], rest[nh:nh + 2], rest[nh + 2:nh + 2 + nho], rest[nh + 2 + nho:]
        _run_hook(hook, "before", h_in, h_out, h_sems)
        _, xhat = _rms_stats(x_ref[...])
        h = (xhat * g_ref[...]).astype(BF16)
        h_ref[...] = h
        for j in range(J):
            acc = _nn(h, w_ref[j])
            if scale is not None:
                acc = acc * scale
            z_ref[:, j * C:(j + 1) * C] = acc.astype(BF16)
        _run_hook(hook, "after", h_in, h_out, h_sems)

    return _hooked_call(
        body, hook, n_in=3, n_out=2, grid=(S // tm,), name=name,
        in_specs=[pl.BlockSpec((tm, D), lambda i: (i, 0)), _full((1, D)), _full((J, D, C))],
        out_specs=[pl.BlockSpec((tm, D), lambda i: (i, 0)), pl.BlockSpec((tm, J * C), lambda i: (i, 0))],
        out_shape=[jax.ShapeDtypeStruct((S, D), BF16), jax.ShapeDtypeStruct((S, J * C), BF16)],
        compiler_params=_params("arbitrary"),
    )(x, g, w, *hook.ins)


def _att_window_specs(nt, col):
    return [pl.BlockSpec((ATT_TILE, 512), lambda t, j=j: (jnp.clip(t - (ATT_WIN - 1) + j, 0, nt - 1), col))
            for j in range(ATT_WIN)]


ATT_SCALE = ATT_HEAD_DIM ** -0.5


def _att_scores(q_scaled, k, bias, valid):
    return jnp.where(valid, _nt(q_scaled, k) + bias, NEG_INF)


def _att_valid(t):
    kpos = lax.broadcasted_iota(jnp.int32, (ATT_TILE, ATT_WIN * ATT_TILE), 1) + (t - (ATT_WIN - 1)) * ATT_TILE
    return kpos >= 0


def attn_fwd(z, bias, hook=Hook()):
    S = z.shape[0]
    nt = S // ATT_TILE
    n_in, nh, nho = 2 + 2 * ATT_WIN, len(hook.ins), len(hook.out_shapes)

    def body(q_ref, *refs):
        k_refs, v_refs = refs[:ATT_WIN], refs[ATT_WIN:2 * ATT_WIN]
        bias_ref = refs[2 * ATT_WIN]
        rest = refs[2 * ATT_WIN + 1:]
        h_in, (y_ref, lse_ref), h_out = rest[:nh], rest[nh:nh + 2], rest[nh + 2:nh + 2 + nho]
        s_scr, p_scr = rest[nh + 2 + nho:nh + 4 + nho]
        h_sems = rest[nh + 4 + nho:]
        _run_hook(hook, "before", h_in, h_out, h_sems)
        valid = _att_valid(pl.program_id(0))
        heads = [slice(h * ATT_HEAD_DIM, (h + 1) * ATT_HEAD_DIM) for h in range(ATT_HEADS)]
        for h, sl in enumerate(heads):
            k = jnp.concatenate([r[:, sl] for r in k_refs], axis=0)
            s_scr[h] = _att_scores(q_ref[:, sl] * ATT_SCALE, k, bias_ref[h], valid)
        stats = []
        for h in range(ATT_HEADS):
            s = s_scr[h]
            m = jnp.max(s, axis=-1, keepdims=True)
            p = jnp.exp(s - m)
            stats.append((m, jnp.sum(p, axis=-1, keepdims=True)))
            p_scr[h] = p.astype(BF16)
        for h, sl in enumerate(heads):
            m, l = stats[h]
            v = jnp.concatenate([r[:, sl] for r in v_refs], axis=0)
            y_ref[:, sl] = (_nn(p_scr[h], v) / l).astype(BF16)
            lse_ref[:, h:h + 1] = m + jnp.log(l)
        _run_hook(hook, "after", h_in, h_out, h_sems)

    tile = lambda col: pl.BlockSpec((ATT_TILE, 512), lambda t: (t, col))
    return _hooked_call(
        body, hook, n_in=n_in, n_out=2, grid=(nt,), name="attn_fwd",
        in_specs=[tile(0)] + _att_window_specs(nt, 1) + _att_window_specs(nt, 2) + [_full(bias.shape)],
        out_specs=[tile(0), pl.BlockSpec((ATT_TILE, ATT_HEADS), lambda t: (t, 0))],
        out_shape=[jax.ShapeDtypeStruct((S, 512), BF16), jax.ShapeDtypeStruct((S, ATT_HEADS), F32)],
        scratch_shapes=[pltpu.VMEM((ATT_HEADS, ATT_TILE, ATT_WIN * ATT_TILE), F32),
                        pltpu.VMEM((ATT_HEADS, ATT_TILE, ATT_WIN * ATT_TILE), BF16)],
        compiler_params=_params("arbitrary"),
    )(z, *([z] * (2 * ATT_WIN)), bias, *hook.ins)


SG_PAIRS = SG_GROUPS // 2


def _group_mean(x, a_ref):
    rows = x.shape[0]
    hi = x.astype(BF16)
    lo = (x - hi.astype(F32)).astype(BF16)
    both = _nn(jnp.concatenate([hi, lo], axis=0), a_ref[...])
    return both[:rows] + both[rows:]


def _pair_diag(x, ref):
    first = lax.broadcasted_iota(jnp.int32, x.shape, 1) < SG_GROUP_DIM
    ref[0:SG_BLOCK, :] = jnp.where(first, x, 0.0).astype(BF16)
    ref[SG_BLOCK:2 * SG_BLOCK, :] = jnp.where(first, 0.0, x).astype(BF16)


def _sgu_norm(zv_ref, a_ref, lng_ref, lnb_ref):
    v = _gelu(zv_ref[...].astype(F32))
    vc = v - _group_mean(v, a_ref)
    rstd = lax.rsqrt(_group_mean(vc * vc, a_ref) + EPS)
    xhat = vc * rstd
    return xhat, rstd, xhat * lng_ref[...] + lnb_ref[...]


def _sgu_rows(b):
    return pl.ds(pl.multiple_of(b * SG_BLOCK, SG_BLOCK), SG_BLOCK)


def sgu_fwd(z, amean, lng, lnb, wpair, bfull, *, tm, hook=Hook()):
    S = z.shape[0]
    nh, nho = len(hook.ins), len(hook.out_shapes)

    def body(zu_ref, zv_ref, a_ref, lng_ref, lnb_ref, wpair_ref, bfull_ref, *rest):
        h_in, y_ref, h_out = rest[:nh], rest[nh], rest[nh + 1:nh + 1 + nho]
        vn_scr, vbd = rest[nh + 1 + nho:nh + 3 + nho]
        h_sems = rest[nh + 3 + nho:]
        _run_hook(hook, "before", h_in, h_out, h_sems)
        vn_scr[...] = _sgu_norm(zv_ref, a_ref, lng_ref, lnb_ref)[2]

        def block(b, carry):
            rows = _sgu_rows(b)
            u = _gelu(zu_ref[rows, :].astype(F32))
            for p in range(SG_PAIRS):
                lanes = slice(p * 128, (p + 1) * 128)
                _pair_diag(vn_scr[rows, lanes], vbd.at[p])
                sv = _nn(wpair_ref[p], vbd[p]) + bfull_ref[:, lanes]
                y_ref[rows, lanes] = (u[:, lanes] * sv).astype(BF16)
            return carry

        lax.fori_loop(0, tm // SG_BLOCK, block, 0)
        _run_hook(hook, "after", h_in, h_out, h_sems)

    tile = lambda col: pl.BlockSpec((tm, 512), lambda i: (i, col))
    smalls = [amean, lng, lnb, wpair, bfull]
    return _hooked_call(
        body, hook, n_in=2 + len(smalls), n_out=1, grid=(S // tm,), name="sgu_fwd",
        in_specs=[tile(3), tile(4)] + [_full(a.shape) for a in smalls],
        out_specs=[tile(0)], out_shape=[jax.ShapeDtypeStruct((S, 512), BF16)],
        scratch_shapes=[pltpu.VMEM((tm, 512), F32), pltpu.VMEM((SG_PAIRS, 2 * SG_BLOCK, 128), BF16)],
        compiler_params=_params("arbitrary"),
    )(z, z, *smalls, *hook.ins)


def _gate_specs(tm):
    return [pl.BlockSpec((tm, 512), lambda i, c=c: (i, c)) for c in (5, 6, 7, 8)]


def merge_fwd(y_att, y_sg, z, x, w_ba, w_bs, w_out, *, tm):
    S, D = x.shape
    J, _, C = w_ba.shape

    def body(ya_ref, ys_ref, g0, g1, g2, g3, x_ref, wba_ref, wbs_ref, wo_ref, a_ref, b_ref, m_ref, x1_ref):
        ya, ys = ya_ref[...], ys_ref[...]
        a = jnp.concatenate([_nn(ya, wba_ref[j]) for j in range(J)], axis=1)
        b = jnp.concatenate([_nn(ys, wbs_ref[j]) for j in range(J)], axis=1)
        ga = jnp.concatenate([g0[...], g1[...]], axis=1).astype(F32)
        gb = jnp.concatenate([g2[...], g3[...]], axis=1).astype(F32)
        a_ref[...] = a.astype(BF16)
        b_ref[...] = b.astype(BF16)
        merged = (_sigmoid(ga) * a + _sigmoid(gb) * b).astype(BF16)
        m_ref[...] = merged
        x1_ref[...] = x_ref[...] + _nn(merged, wo_ref[...])

    row = lambda n: pl.BlockSpec((tm, n), lambda i: (i, 0))
    return pl.pallas_call(
        body, grid=(S // tm,), name="merge_fwd",
        in_specs=[row(512), row(512)] + _gate_specs(tm) + [row(D), _full(w_ba.shape), _full(w_bs.shape), _full(w_out.shape)],
        out_specs=[row(D)] * 4,
        out_shape=[jax.ShapeDtypeStruct((S, D), BF16)] * 3 + [jax.ShapeDtypeStruct((S, D), F32)],
        compiler_params=_params("parallel"),
    )(y_att, y_sg, z, z, z, z, x, w_ba, w_bs, w_out)


def xattn_fwd(xq, kv, w_xo, x1, *, tm):
    S, D = xq.shape
    dh = XATT_HEAD_DIM

    def body(q_ref, kv_ref, wo_ref, x1_ref, o_ref, lse_ref, x2_ref):
        outs = []
        for h in range(XATT_HEADS):
            s = _nt(q_ref[:, h * dh:(h + 1) * dh], kv_ref[:, h * dh:(h + 1) * dh])
            m = jnp.max(s, axis=-1, keepdims=True)
            p = jnp.exp(s - m)
            l = jnp.sum(p, axis=-1, keepdims=True)
            outs.append((_nn(p.astype(BF16), kv_ref[:, D + h * dh:D + (h + 1) * dh]) / l).astype(BF16))
            lse_ref[:, h:h + 1] = m + jnp.log(l)
        o = jnp.concatenate(outs, axis=1)
        o_ref[...] = o
        x2_ref[...] = x1_ref[...] + _nn(o, wo_ref[...])

    row = lambda n: pl.BlockSpec((tm, n), lambda i: (i, 0))
    return pl.pallas_call(
        body, grid=(S // tm,), name="xattn_fwd",
        in_specs=[row(D), _full(kv.shape), _full(w_xo.shape), row(D)],
        out_specs=[row(D), row(XATT_HEADS), row(D)],
        out_shape=[jax.ShapeDtypeStruct((S, D), BF16), jax.ShapeDtypeStruct((S, XATT_HEADS), F32),
                   jax.ShapeDtypeStruct((S, D), F32)],
        compiler_params=_params("parallel"),
    )(xq, kv, w_xo, x1)


def ffn_out_loss(gu, w, x2, g, target, *, tm):
    S, D = x2.shape
    F = w.shape[0]

    def body(gu_ref, w_ref, x2_ref, g_ref, t_ref, act_ref, loss_ref, dx_ref, dg_ref):
        @pl.when(pl.program_id(0) == 0)
        def _():
            loss_ref[...] = jnp.zeros_like(loss_ref)
            dg_ref[...] = jnp.zeros_like(dg_ref)

        gate = gu_ref[:, :F].astype(F32)
        up = gu_ref[:, F:].astype(F32)
        act = (gate * _sigmoid(gate) * up).astype(BF16)
        act_ref[...] = act
        gv = g_ref[...]
        rstd, xhat = _rms_stats(x2_ref[...] + _nn(act, w_ref[...]))
        err = xhat * gv - t_ref[...]
        loss_ref[...] += 0.5 * jnp.sum(jnp.mean(err * err, axis=-1, keepdims=True))
        dx, dgc = _rms_bwd(xhat, rstd, gv, err * (1.0 / D))
        dx_ref[...] = dx
        dg_ref[...] += jnp.sum(dgc, axis=0, keepdims=True)

    row = lambda n: pl.BlockSpec((tm, n), lambda i: (i, 0))
    return pl.pallas_call(
        body, grid=(S // tm,), name="ffn_out_loss",
        in_specs=[row(2 * F), _full(w.shape), row(D), _full((1, D)), row(D)],
        out_specs=[row(F), _acc((1, 128)), row(D), _acc((1, D))],
        out_shape=[jax.ShapeDtypeStruct((S, F), BF16), jax.ShapeDtypeStruct((1, 128), F32),
                   jax.ShapeDtypeStruct((S, D), F32), jax.ShapeDtypeStruct((1, D), F32)],
        compiler_params=_params("arbitrary"),
    )(gu, w, x2, g, target)


def ffn_act_bwd(dx3, gu, w, *, tm, nchunk):
    S, D = dx3.shape
    F = w.shape[0]
    cn = F // nchunk

    def body(dx_ref, gu_ref, w_ref, dgu_ref):
        dxb = dx_ref[...].astype(BF16)
        for j in range(nchunk):
            dact = _nt(dxb, w_ref[j * cn:(j + 1) * cn, :])
            gate = gu_ref[:, j * cn:(j + 1) * cn].astype(F32)
            up = gu_ref[:, F + j * cn:F + (j + 1) * cn].astype(F32)
            sg = _sigmoid(gate)
            dgu_ref[:, j * cn:(j + 1) * cn] = (dact * up * (sg * (1.0 + gate * (1.0 - sg)))).astype(BF16)
            dgu_ref[:, F + j * cn:F + (j + 1) * cn] = (dact * (gate * sg)).astype(BF16)

    row = lambda n: pl.BlockSpec((tm, n), lambda i: (i, 0))
    return pl.pallas_call(
        body, grid=(S // tm,), name="ffn_act_bwd",
        in_specs=[row(D), row(2 * F), _full(w.shape)], out_specs=row(2 * F),
        out_shape=jax.ShapeDtypeStruct((S, 2 * F), BF16), compiler_params=_params("parallel"),
    )(dx3, gu, w)


def _overlaps(widths, lo, hi):
    out, off = [], 0
    for p, wd in enumerate(widths):
        a, b = max(lo, off), min(hi, off + wd)
        if a < b:
            out.append((p, a - off, b - off, a - lo))
        off += wd
    return out


def tn_mm_pieces(a, pieces, *, shards, per, tk, name):
    K, M = a.shape
    widths = [p.shape[1] for p in pieces]
    C = sum(widths) // shards
    tn = per * C
    nk = K // tk
    n_tiles = shards // per

    def body(a_ref, *refs):
        p_refs, o_ref, acc_ref = refs[:len(pieces)], refs[len(pieces)], refs[len(pieces) + 1]
        n, k = pl.program_id(0), pl.program_id(1)

        @pl.when(k == 0)
        def _():
            acc_ref[...] = jnp.zeros_like(acc_ref)

        av = a_ref[...]
        for tile in range(n_tiles):
            @pl.when(n == tile)
            def _(tile=tile):
                for p, c0, c1, at in _overlaps(widths, tile * tn, (tile + 1) * tn):
                    acc_ref[:, at:at + c1 - c0] += _tn(av, p_refs[p][:, c0:c1])

        @pl.when(k == nk - 1)
        def _():
            for s in range(per):
                o_ref[s] = acc_ref[:, s * C:(s + 1) * C].astype(BF16)

    return pl.pallas_call(
        body, grid=(n_tiles, nk), name=name,
        in_specs=[pl.BlockSpec((tk, M), lambda n, k: (k, 0))] + [pl.BlockSpec((tk, wd), lambda n, k: (k, 0)) for wd in widths],
        out_specs=pl.BlockSpec((per, M, C), lambda n, k: (n, 0, 0)), out_shape=jax.ShapeDtypeStruct((shards, M, C), BF16),
        scratch_shapes=[pltpu.VMEM((M, tn), F32)], compiler_params=_params("parallel", "arbitrary"),
    )(a, *pieces)


def tn_mm(a, b, *, shards, tn, tk, name):
    K, M = a.shape
    N = b.shape[1]
    C = N // shards
    per = tn // C
    nk = K // tk

    def body(a_ref, b_ref, o_ref, acc_ref):
        k = pl.program_id(1)

        @pl.when(k == 0)
        def _():
            acc_ref[...] = jnp.zeros_like(acc_ref)

        acc_ref[...] += _tn(a_ref[...].astype(BF16), b_ref[...].astype(BF16))

        @pl.when(k == nk - 1)
        def _():
            if shards > 1:
                for s in range(per):
                    o_ref[s] = acc_ref[:, s * C:(s + 1) * C].astype(BF16)
            else:
                o_ref[...] = acc_ref[...].astype(BF16)

    if shards > 1:
        out_spec = pl.BlockSpec((per, M, C), lambda n, k: (n, 0, 0))
        out_shape = jax.ShapeDtypeStruct((shards, M, C), BF16)
    else:
        out_spec = pl.BlockSpec((M, tn), lambda n, k: (0, n))
        out_shape = jax.ShapeDtypeStruct((M, N), BF16)
    return pl.pallas_call(
        body, grid=(N // tn, nk), name=name,
        in_specs=[pl.BlockSpec((tk, M), lambda n, k: (k, 0)), pl.BlockSpec((tk, tn), lambda n, k: (k, n))],
        out_specs=out_spec, out_shape=out_shape, scratch_shapes=[pltpu.VMEM((M, tn), F32)],
        compiler_params=_params("parallel", "arbitrary"),
    )(a, b)


def mm_nt_norm_bwd(dy, w, x, g, dx_in, *, tm, name, hook=Hook()):
    S, D = x.shape
    J, _, C = w.shape
    has_in = dx_in is not None
    dys = list(dy) if isinstance(dy, (list, tuple)) else [dy]
    widths = [d.shape[1] for d in dys]
    nd = len(dys)
    n_in, nh, nho = nd + 3 + has_in, len(hook.ins), len(hook.out_shapes)

    def body(*refs):
        dy_refs = refs[:nd]
        w_ref, x_ref, g_ref = refs[nd:nd + 3]
        dxin_ref = refs[nd + 3] if has_in else None
        rest = refs[n_in:]
        h_in, (dx_ref, dg_ref), h_out, h_sems = rest[:nh], rest[nh:nh + 2], rest[nh + 2:nh + 2 + nho], rest[nh + 2 + nho:]
        _run_hook(hook, "before", h_in, h_out, h_sems)

        @pl.when(pl.program_id(0) == 0)
        def _():
            dg_ref[...] = jnp.zeros_like(dg_ref)

        dh = None
        for j in range(J):
            for p, c0, c1, at in _overlaps(widths, j * C, (j + 1) * C):
                part = _nt(dy_refs[p][:, c0:c1], w_ref[j, :, at:at + c1 - c0])
                dh = part if dh is None else dh + part
        rstd, xhat = _rms_stats(x_ref[...])
        dx, dgc = _rms_bwd(xhat, rstd, g_ref[...], dh)
        dx_ref[...] = dx + dxin_ref[...] if has_in else dx
        dg_ref[...] += jnp.sum(dgc, axis=0, keepdims=True)
        _run_hook(hook, "after", h_in, h_out, h_sems)

    row = lambda n: pl.BlockSpec((tm, n), lambda i: (i, 0))
    ins = dys + [w, x, g] + ([dx_in] if has_in else [])
    return _hooked_call(
        body, hook, n_in=n_in, n_out=2, grid=(S // tm,), name=name,
        in_specs=[row(wd) for wd in widths] + [_full(w.shape), row(D), _full((1, D))] + ([row(D)] if has_in else []),
        out_specs=[row(D), _acc((1, D))],
        out_shape=[jax.ShapeDtypeStruct((S, D), F32), jax.ShapeDtypeStruct((1, D), F32)],
        compiler_params=_params("arbitrary"),
    )(*ins, *hook.ins)


def xattn_bwd(dx2, xq, o, lse, kv, w_xo, *, tm):
    S, D = xq.shape
    M = kv.shape[0]
    dh = XATT_HEAD_DIM

    def body(dx_ref, q_ref, o_ref, lse_ref, kv_ref, wo_ref, dq_ref, dkv_ref):
        @pl.when(pl.program_id(0) == 0)
        def _():
            dkv_ref[...] = jnp.zeros_like(dkv_ref)

        do = _nt(dx_ref[...].astype(BF16), wo_ref[...])
        for h in range(XATT_HEADS):
            hs = slice(h * dh, (h + 1) * dh)
            vs = slice(D + h * dh, D + (h + 1) * dh)
            q, k, v = q_ref[:, hs], kv_ref[:, hs], kv_ref[:, vs]
            do_h = do[:, hs]
            do_b = do_h.astype(BF16)
            p = jnp.exp(_nt(q, k) - lse_ref[:, h:h + 1])
            delta = jnp.sum(do_h * o_ref[:, hs].astype(F32), axis=-1, keepdims=True)
            ds = (p * (_nt(do_b, v) - delta)).astype(BF16)
            dq_ref[:, hs] = (_nn(ds, k) * (dh ** -0.5)).astype(BF16)
            dkv_ref[:, hs] += _tn(ds, q)
            dkv_ref[:, vs] += _tn(p.astype(BF16), do_b)

    row = lambda n: pl.BlockSpec((tm, n), lambda i: (i, 0))
    return pl.pallas_call(
        body, grid=(S // tm,), name="xattn_bwd",
        in_specs=[row(D), row(D), row(D), row(XATT_HEADS), _full(kv.shape), _full(w_xo.shape)],
        out_specs=[row(D), _acc((M, 2 * D))],
        out_shape=[jax.ShapeDtypeStruct((S, D), BF16), jax.ShapeDtypeStruct((M, 2 * D), F32)],
        compiler_params=_params("arbitrary"),
    )(dx2, xq, o, lse, kv, w_xo)


def merge_bwd(dx1, w_out, a, b, z, w_ba, w_bs, *, tm):
    S, D = dx1.shape
    J, W, C = w_ba.shape

    def body(dx_ref, wo_ref, a_ref, b_ref, g0, g1, g2, g3, wba_ref, wbs_ref, da_ref, db_ref, dg_ref, dya_ref, dys_ref):
        dm = _nt(dx_ref[...].astype(BF16), wo_ref[...])
        sa = _sigmoid(jnp.concatenate([g0[...], g1[...]], axis=1).astype(F32))
        sb = _sigmoid(jnp.concatenate([g2[...], g3[...]], axis=1).astype(F32))
        dg_ref[:, :D] = (dm * a_ref[...].astype(F32) * (sa * (1.0 - sa))).astype(BF16)
        dg_ref[:, D:] = (dm * b_ref[...].astype(F32) * (sb * (1.0 - sb))).astype(BF16)
        da = (dm * sa).astype(BF16)
        db = (dm * sb).astype(BF16)
        da_ref[...] = da
        db_ref[...] = db
        dya = _nt(da[:, 0:C], wba_ref[0])
        dys = _nt(db[:, 0:C], wbs_ref[0])
        for j in range(1, J):
            dya += _nt(da[:, j * C:(j + 1) * C], wba_ref[j])
            dys += _nt(db[:, j * C:(j + 1) * C], wbs_ref[j])
        dya_ref[...] = dya.astype(BF16)
        dys_ref[...] = dys.astype(BF16)

    row = lambda n: pl.BlockSpec((tm, n), lambda i: (i, 0))
    return pl.pallas_call(
        body, grid=(S // tm,), name="merge_bwd",
        in_specs=[row(D), _full(w_out.shape), row(D), row(D)] + _gate_specs(tm) + [_full(w_ba.shape), _full(w_bs.shape)],
        out_specs=[row(D), row(D), row(2 * D), row(W), row(W)],
        out_shape=[jax.ShapeDtypeStruct((S, D), BF16)] * 2 + [jax.ShapeDtypeStruct((S, 2 * D), BF16)]
        + [jax.ShapeDtypeStruct((S, W), BF16)] * 2,
        compiler_params=_params("parallel"),
    )(dx1, w_out, a, b, z, z, z, z, w_ba, w_bs)


def attn_bwd(z, y, dy, lse, bias, hook=Hook()):
    S = z.shape[0]
    nt = S // ATT_TILE
    back = ATT_WIN - 1
    n_in, nh, nho = 5 + 2 * ATT_WIN, len(hook.ins), len(hook.out_shapes)

    def body(q_ref, *refs):
        k_refs, v_refs = refs[:ATT_WIN], refs[ATT_WIN:2 * ATT_WIN]
        y_ref, dy_ref, lse_ref, bias_ref = refs[2 * ATT_WIN:2 * ATT_WIN + 4]
        rest = refs[2 * ATT_WIN + 4:]
        h_in, (dqkv_ref, dbias_ref), h_out = rest[:nh], rest[nh:nh + 2], rest[nh + 2:nh + 2 + nho]
        dk_acc, dv_acc, dq_buf = rest[nh + 2 + nho:nh + 5 + nho]
        h_sems = rest[nh + 5 + nho:]
        t = pl.program_id(0)
        _run_hook(hook, "before", h_in, h_out, h_sems)

        @pl.when(t == 0)
        def _():
            dbias_ref[...] = jnp.zeros_like(dbias_ref)
            dk_acc[...] = jnp.zeros_like(dk_acc)
            dv_acc[...] = jnp.zeros_like(dv_acc)
            dq_buf[...] = jnp.zeros_like(dq_buf)

        @pl.when(t < nt)
        def _():
            valid = _att_valid(t)
            q_t = (q_ref[...] * ATT_SCALE).astype(F32).T.astype(BF16)
            do_t = dy_ref[...].astype(F32).T.astype(BF16)
            for h in range(ATT_HEADS):
                sl = slice(h * ATT_HEAD_DIM, (h + 1) * ATT_HEAD_DIM)
                k = jnp.concatenate([r[:, sl] for r in k_refs], axis=0)
                v = jnp.concatenate([r[:, sl] for r in v_refs], axis=0)
                q, do_b = q_ref[:, sl] * ATT_SCALE, dy_ref[:, sl]
                p = jnp.exp(_att_scores(q, k, bias_ref[h], valid) - lse_ref[:, h:h + 1])
                delta = jnp.sum(do_b.astype(F32) * y_ref[:, sl].astype(F32), axis=-1, keepdims=True)
                ds = p * (_nt(do_b, v) - delta)
                dbias_ref[h] += ds
                ds_b = ds.astype(BF16)
                dq_buf[t % ATT_WIN, :, sl] = (_nn(ds_b, k) * ATT_SCALE).astype(BF16)
                dk_w = _nn(q_t[sl, :], ds_b)
                dv_w = _nn(do_t[sl, :], p.astype(BF16))
                for i in range(ATT_WIN):
                    slot = (t + 1 + i) % ATT_WIN
                    cols = slice(i * ATT_TILE, (i + 1) * ATT_TILE)
                    if i == back:
                        dk_acc[slot, sl, :] = dk_w[:, cols]
                        dv_acc[slot, sl, :] = dv_w[:, cols]
                    else:
                        dk_acc[slot, sl, :] += dk_w[:, cols]
                        dv_acc[slot, sl, :] += dv_w[:, cols]

        done = (t + 1) % ATT_WIN
        dqkv_ref[:, 0:512] = dq_buf[done]
        dqkv_ref[:, 512:1024] = dk_acc[done].T.astype(BF16)
        dqkv_ref[:, 1024:1536] = dv_acc[done].T.astype(BF16)
        _run_hook(hook, "after", h_in, h_out, h_sems)

    last = nt - 1
    tile = lambda col, n=512: pl.BlockSpec((ATT_TILE, n), lambda t: (jnp.minimum(t, last), col))
    late = pl.BlockSpec((ATT_TILE, 1536), lambda t: (jnp.maximum(t - back, 0), 0))
    return _hooked_call(
        body, hook, n_in=n_in, n_out=2, grid=(nt + back,), name="attn_bwd",
        in_specs=[tile(0)] + _att_window_specs(nt, 1) + _att_window_specs(nt, 2)
        + [tile(0), tile(0), tile(0, ATT_HEADS), _full(bias.shape)],
        out_specs=[late, _acc(bias.shape)],
        out_shape=[jax.ShapeDtypeStruct((S, 1536), BF16), jax.ShapeDtypeStruct(bias.shape, F32)],
        scratch_shapes=[pltpu.VMEM((ATT_WIN, 512, ATT_TILE), F32)] * 2 + [pltpu.VMEM((ATT_WIN, ATT_TILE, 512), BF16)],
        compiler_params=_params("arbitrary"),
    )(z, *([z] * (2 * ATT_WIN)), y, dy, lse, bias, *hook.ins)


def sgu_bwd(z, dy, amean, lng, lnb, wpair, wtpair, bfull, maskpair, *, tm):
    S = z.shape[0]
    n = S // tm

    def body(zu_ref, zv_ref, dy_ref, a_ref, lng_ref, lnb_ref, wpair_ref, wtpair_ref, bfull_ref, mask_ref,
             duv_ref, dw_ref, dsgb_ref, dlng_ref, dlnb_ref, xhat_scr, rstd_scr, vn_scr, dxh_scr, vbd, dbd,
             w_acc, b_acc, g_acc, s_acc):
        i = pl.program_id(0)

        @pl.when(i == 0)
        def _():
            for r in (w_acc, b_acc, g_acc, s_acc):
                r[...] = jnp.zeros_like(r)

        xhat_scr[...], rstd_scr[...], vn_scr[...] = _sgu_norm(zv_ref, a_ref, lng_ref, lnb_ref)

        def block(b, carry):
            rows = _sgu_rows(b)
            zu = zu_ref[rows, :].astype(F32)
            u, du = _gelu(zu), _gelu_grad(zu)
            dyv = dy_ref[rows, :].astype(F32)
            for p in range(SG_PAIRS):
                lanes = slice(p * 128, (p + 1) * 128)
                _pair_diag(vn_scr[rows, lanes], vbd.at[p])
                sv = _nn(wpair_ref[p], vbd[p]) + bfull_ref[:, lanes]
                duv_ref[rows, lanes] = (dyv[:, lanes] * sv * du[:, lanes]).astype(BF16)
                dsv = dyv[:, lanes] * u[:, lanes]
                b_acc[:, lanes] += dsv
                w_acc[p] += _nt(dsv.astype(BF16), vbd[p])
                _pair_diag(dsv, dbd.at[p])
                dvn = _nn(wtpair_ref[p], dbd[p])
                g_acc[:, lanes] += dvn * xhat_scr[rows, lanes]
                s_acc[:, lanes] += dvn
                dxh_scr[rows, lanes] = dvn * lng_ref[:, lanes]
            return carry

        lax.fori_loop(0, tm // SG_BLOCK, block, 0)
        dxh, xhat = dxh_scr[...], xhat_scr[...]
        dv = rstd_scr[...] * (dxh - _group_mean(dxh, a_ref) - xhat * _group_mean(dxh * xhat, a_ref))
        duv_ref[:, 512:1024] = (dv * _gelu_grad(zv_ref[...].astype(F32))).astype(BF16)

        @pl.when(i == n - 1)
        def _():
            dw_ref[...] = w_acc[...] * mask_ref[...]
            dsgb_ref[...] = _group_mean(b_acc[...], a_ref) * float(SG_GROUP_DIM)
            dlng_ref[...] = jnp.sum(g_acc[...], axis=0, keepdims=True)
            dlnb_ref[...] = jnp.sum(s_acc[...], axis=0, keepdims=True)

    tile = lambda col: pl.BlockSpec((tm, 512), lambda i: (i, col))
    smalls = [amean, lng, lnb, wpair, wtpair, bfull, maskpair]
    pair_shape = (SG_PAIRS, SG_BLOCK, 2 * SG_BLOCK)
    return pl.pallas_call(
        body, grid=(n,), name="sgu_bwd",
        in_specs=[tile(3), tile(4), tile(0)] + [_full(a.shape) for a in smalls],
        out_specs=[pl.BlockSpec((tm, 1024), lambda i: (i, 0)), _acc(pair_shape), _acc((SG_BLOCK, 512)),
                   _acc((1, 512)), _acc((1, 512))],
        out_shape=[jax.ShapeDtypeStruct((S, 1024), BF16), jax.ShapeDtypeStruct(pair_shape, F32),
                   jax.ShapeDtypeStruct((SG_BLOCK, 512), F32), jax.ShapeDtypeStruct((1, 512), F32),
                   jax.ShapeDtypeStruct((1, 512), F32)],
        scratch_shapes=[pltpu.VMEM((tm, 512), F32)] * 4
        + [pltpu.VMEM((SG_PAIRS, 2 * SG_BLOCK, 128), BF16)] * 2
        + [pltpu.VMEM(pair_shape, F32)] + [pltpu.VMEM((SG_BLOCK, 512), F32)] * 3,
        compiler_params=_params("arbitrary"),
    )(z, z, dy, *smalls)


def bias_colsum(p):
    H, _, L = p.shape
    near_lo = (ATT_WIN - 1) * ATT_TILE - REL_CLIP + 1
    near_hi = ATT_WIN * ATT_TILE

    def body(p_ref, gw_ref, far_ref):
        k = lax.broadcasted_iota(jnp.int32, (1, L), 1)
        is_far = (k < near_lo) | (k >= near_hi)
        for h in range(H):
            g = jnp.sum(p_ref[h], axis=0, keepdims=True)
            gw_ref[h:h + 1, :] = g
            far_ref[h:h + 1, :] = jnp.zeros((1, 128), F32) + jnp.sum(jnp.where(is_far, g, 0.0))

    return pl.pallas_call(
        body, name="bias_colsum", in_specs=[_full(p.shape)], out_specs=[_acc((H, L)), _acc((H, 128))],
        out_shape=[jax.ShapeDtypeStruct((H, L), F32), jax.ShapeDtypeStruct((H, 128), F32)], grid=(1,),
        compiler_params=_params("arbitrary"),
    )(p)


def _row_tile(rows, cols):
    best = None
    for tr in range(16, rows + 1, 16):
        if rows % tr == 0 and tr * cols * 4 <= EW_BLOCK_BYTES:
            best = tr
    return best if best is not None else rows


def place_shard(w, own, name):
    R, C = w.shape
    r = R // 2
    tr = _row_tile(r, C)
    nr = r // tr

    def body(own_ref, w_ref, o_ref):
        o_ref[...] = w_ref[...].astype(BF16)

    return pl.pallas_call(
        body, name=name,
        grid_spec=pltpu.PrefetchScalarGridSpec(
            num_scalar_prefetch=1, grid=(2, nr),
            in_specs=[pl.BlockSpec((tr, C), lambda h, i, own: (h * nr + i, 0))],
            out_specs=pl.BlockSpec((None, None, tr, C), lambda h, i, own: (own[0], h, i, 0))),
        out_shape=jax.ShapeDtypeStruct((4, 2, r, C), BF16), compiler_params=_params("parallel", "parallel"),
    )(own, w)


def chip_partial(grad, recv, core, name):
    _, _, r, C = grad.shape
    tr = _row_tile(r, C)

    def body(core_ref, g_ref, r_ref, o_ref):
        o_ref[...] = (g_ref[...].astype(F32) + r_ref[...].astype(F32)).astype(BF16)

    spec = pl.BlockSpec((None, tr, C), lambda j, i, core: (j, i, 0))
    return pl.pallas_call(
        body, name=name,
        grid_spec=pltpu.PrefetchScalarGridSpec(
            num_scalar_prefetch=1, grid=(4, r // tr),
            in_specs=[pl.BlockSpec((None, None, tr, C), lambda j, i, core: (j, core[0], i, 0)), spec], out_specs=spec),
        out_shape=jax.ShapeDtypeStruct((4, r, C), BF16), compiler_params=_params("parallel", "parallel"),
    )(core, grad, recv)


def shard_sum(part, recv, own, core, name):
    _, r, C = part.shape
    tr = _row_tile(r, C)

    def body(own_ref, core_ref, p_ref, r0, r1, r2, o_ref):
        o_ref[...] = p_ref[...].astype(F32) + r0[...].astype(F32) + r1[...].astype(F32) + r2[...].astype(F32)

    return pl.pallas_call(
        body, name=name,
        grid_spec=pltpu.PrefetchScalarGridSpec(
            num_scalar_prefetch=2, grid=(r // tr,),
            in_specs=[pl.BlockSpec((None, tr, C), lambda i, own, core: (own[0], i, 0))]
            + [pl.BlockSpec((None, tr, C), lambda i, own, core, k=k: (k, i, 0)) for k in range(3)],
            out_specs=pl.BlockSpec((None, tr, C), lambda i, own, core: (core[0], i, 0))),
        out_shape=jax.ShapeDtypeStruct((2, r, C), F32), compiler_params=_params("parallel"),
    )(own, core, part, recv, recv, recv)


def adamw(w, g, m, v, name):
    R, C = w.shape
    tr = _row_tile(R, C)

    def body(w_ref, g_ref, m_ref, v_ref, d_ref, nm_ref, nv_ref):
        gv = g_ref[...]
        nm = ADAM_B1 * m_ref[...] + (1.0 - ADAM_B1) * gv
        nv = ADAM_B2 * v_ref[...] + (1.0 - ADAM_B2) * (gv * gv)
        m_hat = nm / (1.0 - ADAM_B1 ** ADAM_STEP)
        v_hat = nv / (1.0 - ADAM_B2 ** ADAM_STEP)
        d_ref[...] = -ADAM_LR * (m_hat / (jnp.sqrt(v_hat) + ADAM_EPS) + ADAM_WD * w_ref[...])
        nm_ref[...] = nm
        nv_ref[...] = nv

    spec = pl.BlockSpec((tr, C), lambda i: (i, 0))
    return pl.pallas_call(
        body, grid=(R // tr,), name=name, in_specs=[spec] * 4, out_specs=[spec] * 3,
        out_shape=[jax.ShapeDtypeStruct((R, C), F32)] * 3, compiler_params=_params("parallel"),
    )(w, g, m, v)


HBM = pl.BlockSpec(memory_space=pl.ANY)


def _place():
    x, y, c = lax.axis_index("x"), lax.axis_index("y"), lax.axis_index("c")
    chips = [(1 - x, y), (x, 1 - y), (1 - x, 1 - y)]
    return x, y, c, chips


def _gather_phases(n):
    def copies(buf, sems, kind):
        send_sems, recv_sems = sems
        x, y, c, chips = _place()

        def remote(w, k, slot, to):
            return pltpu.make_async_remote_copy(src_ref=slot, dst_ref=slot, send_sem=send_sems.at[w, k],
                                                recv_sem=recv_sems.at[w, k], device_id=to, device_id_type=MESH)

        def one(w, k, px, py):
            if kind == "mine":
                return remote(w, k, buf[w].at[2 * x + y, c], (px, py, c))
            if kind == "theirs":
                return remote(w, k, buf[w].at[2 * px + py, c], (px, py, c))
            if kind == "onward":
                return remote(w, 3 + k, buf[w].at[2 * px + py, c], (x, y, 1 - c))
            return remote(w, 3 + k, buf[w].at[2 * px + py, 1 - c], (x, y, 1 - c))

        return [one(w, k, px, py) for w in range(n) for k, (px, py) in enumerate(chips)]

    def start(_, buf, sems):
        for cp in copies(buf, sems, "mine"):
            cp.start()

    def relay(_, buf, sems):
        for theirs, onward in zip(copies(buf, sems, "theirs"), copies(buf, sems, "onward")):
            theirs.wait_recv()
            onward.start()

    def finish(_, buf, sems):
        for cp in copies(buf, sems, "relayed"):
            cp.wait_recv()
        for cp in copies(buf, sems, "mine") + copies(buf, sems, "onward"):
            cp.wait_send()

    return start, relay, finish


def _gather_sems(n):
    return (pltpu.SemaphoreType.DMA((n, 6)), pltpu.SemaphoreType.DMA((n, 6)))


def gather_weights(bufs):
    n = len(bufs)
    phases = _gather_phases(n)

    def body(*refs):
        for phase in phases:
            phase(None, refs[n:2 * n], refs[2 * n:])

    return pl.pallas_call(
        body, name="gather_weights", in_specs=[HBM] * n, out_specs=[HBM] * n,
        out_shape=[jax.ShapeDtypeStruct(b.shape, b.dtype) for b in bufs],
        input_output_aliases={i: i for i in range(n)}, scratch_shapes=list(_gather_sems(n)),
    )(*bufs)


def gather_hook(bufs, n_steps):
    start, relay, finish = _gather_phases(len(bufs))
    return Hook(ins=tuple(bufs), out_shapes=tuple(jax.ShapeDtypeStruct(b.shape, b.dtype) for b in bufs), aliased=True,
                sems=_gather_sems(len(bufs)),
                steps=((0, "before", start), ((n_steps * 13) // 16, "before", relay), (n_steps - 1, "after", finish)))


def sibling_split(grads, name):
    n = len(grads)

    def body(*refs):
        src, dst = refs[:n], refs[n:2 * n]
        send_sems, recv_sems = refs[2 * n:]
        x, y, c, _ = _place()
        sends = [pltpu.make_async_remote_copy(src_ref=src[w].at[:, 1 - c], dst_ref=dst[w], send_sem=send_sems.at[w],
                                              recv_sem=recv_sems.at[w], device_id=(x, y, 1 - c), device_id_type=MESH)
                 for w in range(n)]
        for cp in sends:
            cp.start()
        for cp in sends:
            cp.wait()

    return pl.pallas_call(
        body, name=name, in_specs=[HBM] * n, out_specs=[HBM] * n,
        out_shape=[jax.ShapeDtypeStruct((4,) + g.shape[2:], g.dtype) for g in grads],
        scratch_shapes=[pltpu.SemaphoreType.DMA((n,))] * 2,
    )(*grads)


def exchange_hook(parts, n_steps):
    n = len(parts)

    def copies(src, dst, sems):
        send_sems, recv_sems = sems
        _, _, c, chips = _place()
        return [pltpu.make_async_remote_copy(
            src_ref=src[w].at[2 * px + py], dst_ref=dst[w].at[k], send_sem=send_sems.at[w, k],
            recv_sem=recv_sems.at[w, k], device_id=(px, py, c), device_id_type=MESH)
            for w in range(n) for k, (px, py) in enumerate(chips)]

    def start(src, dst, sems):
        for cp in copies(src, dst, sems):
            cp.start()

    def finish(src, dst, sems):
        for cp in copies(src, dst, sems):
            cp.wait()

    return Hook(ins=tuple(parts), out_shapes=tuple(jax.ShapeDtypeStruct((3,) + p.shape[1:], p.dtype) for p in parts),
                sems=(pltpu.SemaphoreType.DMA((n, 3)), pltpu.SemaphoreType.DMA((n, 3))),
                steps=((0, "before", start), (n_steps - 1, "after", finish)))


def sibling_join(sums):
    n = len(sums)

    def body(*refs):
        buf = refs[n:2 * n]
        send_sems, recv_sems = refs[2 * n:]
        x, y, c, _ = _place()
        sends = [pltpu.make_async_remote_copy(src_ref=buf[w].at[c], dst_ref=buf[w].at[c], send_sem=send_sems.at[w],
                                              recv_sem=recv_sems.at[w], device_id=(x, y, 1 - c), device_id_type=MESH)
                 for w in range(n)]
        for cp in sends:
            cp.start()
        for w, cp in enumerate(sends):
            cp.wait_send()
            pltpu.make_async_remote_copy(src_ref=buf[w].at[c], dst_ref=buf[w].at[1 - c], send_sem=send_sems.at[w],
                                         recv_sem=recv_sems.at[w], device_id=(x, y, 1 - c), device_id_type=MESH).wait_recv()

    return pl.pallas_call(
        body, name="sibling_join", in_specs=[HBM] * n, out_specs=[HBM] * n,
        out_shape=[jax.ShapeDtypeStruct(s.shape, s.dtype) for s in sums],
        input_output_aliases={i: i for i in range(n)},
        scratch_shapes=[pltpu.SemaphoreType.DMA((n,))] * 2,
    )(*sums)


def allreduce_small(v):
    R, C = v.shape

    def body(v_ref, out_ref, all_ref, send_sems, recv_sems):
        x, y, c, chips = _place()
        me, sibling = (x, y, c), (x, y, 1 - c)

        def slot(px, py, pc):
            return all_ref.at[4 * px + 2 * py + pc]

        def copy(k, block, to, src=None):
            return pltpu.make_async_remote_copy(src_ref=slot(*block) if src is None else src, dst_ref=slot(*block),
                                                send_sem=send_sems.at[k], recv_sem=recv_sems.at[k], device_id=to,
                                                device_id_type=MESH)

        first = [copy(0, me, sibling, src=v_ref)] + [copy(1 + j, me, (*chip, c), src=v_ref) for j, chip in enumerate(chips)]
        for cp in first:
            cp.start()
        slot(*me)[...] = v_ref[...]
        passed = [copy(4 + j, (*chip, c), sibling) for j, chip in enumerate(chips)]
        for j, chip in enumerate(chips):
            copy(1 + j, (*chip, c), me).wait_recv()
            passed[j].start()
        copy(0, sibling, me).wait_recv()
        for j, chip in enumerate(chips):
            copy(4 + j, (*chip, 1 - c), me).wait_recv()
        for cp in first + passed:
            cp.wait_send()
        acc = all_ref[0]
        for k in range(1, 8):
            acc = acc + all_ref[k]
        out_ref[...] = acc

    vmem = pl.BlockSpec(memory_space=pltpu.VMEM)
    return pl.pallas_call(
        body, name="allreduce_small", in_specs=[vmem], out_specs=vmem, out_shape=jax.ShapeDtypeStruct((R, C), F32),
        scratch_shapes=[pltpu.VMEM((8, R, C), F32), pltpu.SemaphoreType.DMA((7,)), pltpu.SemaphoreType.DMA((7,))],
        compiler_params=pltpu.CompilerParams(vmem_limit_bytes=VMEM_LIMIT_V7X),
    )(v)


def allgather_hook(v, n_steps):
    R, C = v.shape

    def copies(v_ref, out, sems, kind):
        send_sems, recv_sems, _ = sems
        x, y, c, chips = _place()
        me, sibling = (x, y, c), (x, y, 1 - c)

        def slot(px, py, pc):
            return out.at[4 * px + 2 * py + pc]

        def copy(k, block, to, src=None):
            return pltpu.make_async_remote_copy(src_ref=slot(*block) if src is None else src, dst_ref=slot(*block),
                                                send_sem=send_sems.at[k], recv_sem=recv_sems.at[k], device_id=to,
                                                device_id_type=MESH)

        if kind == "mine":
            return [copy(0, me, sibling, src=v_ref)] + [copy(1 + j, me, (*chip, c), src=v_ref) for j, chip in enumerate(chips)]
        if kind == "theirs":
            return [copy(1 + j, (*chip, c), me) for j, chip in enumerate(chips)]
        if kind == "onward":
            return [copy(4 + j, (*chip, c), sibling) for j, chip in enumerate(chips)]
        return [copy(0, sibling, me)] + [copy(4 + j, (*chip, 1 - c), me) for j, chip in enumerate(chips)]

    def own(v_ref, out, sems):
        x, y, c, _ = _place()
        return pltpu.make_async_copy(v_ref, out.at[4 * x + 2 * y + c], sems[2])

    def start(ins, outs, sems):
        own(ins[0], outs[0], sems).start()
        for cp in copies(ins[0], outs[0], sems, "mine"):
            cp.start()

    def relay(ins, outs, sems):
        for theirs, onward in zip(copies(ins[0], outs[0], sems, "theirs"), copies(ins[0], outs[0], sems, "onward")):
            theirs.wait_recv()
            onward.start()

    def finish(ins, outs, sems):
        for cp in copies(ins[0], outs[0], sems, "relayed"):
            cp.wait_recv()
        for cp in copies(ins[0], outs[0], sems, "mine") + copies(ins[0], outs[0], sems, "onward"):
            cp.wait_send()
        own(ins[0], outs[0], sems).wait()

    return Hook(ins=(v,), out_shapes=(jax.ShapeDtypeStruct((8, R, C), F32),),
                sems=(pltpu.SemaphoreType.DMA((7,)), pltpu.SemaphoreType.DMA((7,)), pltpu.SemaphoreType.DMA(())),
                steps=((0, "before", start), ((n_steps * 13) // 16, "before", relay), (n_steps - 1, "after", finish)))


def merge_hooks(a, b):
    assert not a.aliased and not b.aliased
    cut = (len(a.ins), len(a.out_shapes), len(a.sems))

    def part(fn, second):
        def run(ins, outs, sems):
            if second:
                fn(ins[cut[0]:], outs[cut[1]:], sems[cut[2]:])
            else:
                fn(ins[:cut[0]], outs[:cut[1]], sems[:cut[2]])
        return run

    steps = tuple((s, w, part(fn, False)) for s, w, fn in a.steps) + tuple((s, w, part(fn, True)) for s, w, fn in b.steps)
    return Hook(a.ins + b.ins, a.out_shapes + b.out_shapes, False, a.sems + b.sems, steps)


def sum_blocks(g):
    n, R, C = g.shape

    def body(g_ref, o_ref):
        acc = g_ref[0]
        for k in range(1, n):
            acc = acc + g_ref[k]
        o_ref[...] = acc

    return pl.pallas_call(body, name="sum_blocks", grid=(1,), in_specs=[_full(g.shape)], out_specs=_acc((R, C)),
                          out_shape=jax.ShapeDtypeStruct((R, C), F32), compiler_params=_params("arbitrary"))(g)


SMALL_PAD = 1024


def _pack_small(arrs):
    parts = []
    for a in arrs:
        f = a.reshape(-1).astype(F32)
        parts.append(jnp.pad(f, (0, (-f.shape[0]) % SMALL_PAD)))
    return jnp.concatenate(parts).reshape(-1, 128)


def _unpack_small(packed, shapes):
    flat = packed.reshape(-1)
    outs, off = [], 0
    for s in shapes:
        size = int(np.prod(s))
        outs.append(flat[off:off + size].reshape(s))
        off += size + (-size) % SMALL_PAD
    return outs


ATT_KEYS = ATT_WIN * ATT_TILE
ATT_NEAR_LO = (ATT_WIN - 1) * ATT_TILE - REL_CLIP + 1
ATT_PERIOD = ATT_KEYS + ATT_TILE + 1


def _att_bias_table(rel_bias):
    far = rel_bias[:, 2 * REL_CLIP:]
    near = rel_bias[:, 2 * REL_CLIP - 1:0:-1]
    w = jnp.concatenate([jnp.broadcast_to(far, (ATT_HEADS, ATT_NEAR_LO)), near,
                         jnp.broadcast_to(far, (ATT_HEADS, ATT_PERIOD - ATT_KEYS))], axis=1)
    rows = jnp.tile(w, (1, ATT_TILE))[:, :ATT_TILE * (ATT_PERIOD - 1)].reshape(ATT_HEADS, ATT_TILE, ATT_PERIOD - 1)
    i = np.arange(ATT_TILE)[:, None]
    m = np.arange(ATT_KEYS)[None, :]
    qc, kc = i // CHUNK, m // CHUNK
    band = (kc >= qc) & (kc <= qc + N_PREV_CHUNKS)
    return jnp.where(band[None], rows[:, :, :ATT_KEYS], NEG_INF)


def _rel_bias_grad(dbias):
    p = jnp.pad(dbias, ((0, 0), (0, 0), (0, ATT_PERIOD - 1 - ATT_KEYS))).reshape(ATT_HEADS, -1)
    p = jnp.pad(p, ((0, 0), (0, ATT_TILE))).reshape(ATT_HEADS, ATT_TILE, ATT_PERIOD)
    gw, far = bias_colsum(p)
    return jnp.concatenate([jnp.zeros((ATT_HEADS, 1), F32), gw[:, ATT_KEYS - 1:ATT_NEAR_LO - 1:-1], far[:, :1]], axis=1)


def kernel(x, mem, norm_mix_g, w_in, rel_bias, sg_ln_g, sg_ln_b, sg_w, sg_b, w_branch_att, w_branch_sg, w_out, norm_xattn_g, norm_mem_g, w_xq, w_xkv, w_xo, norm_ffn_g, w_ffn_in, w_ffn_out, norm_final_g, loss_target, m_norm_mix_g, m_w_in, m_rel_bias, m_sg_ln_g, m_sg_ln_b, m_sg_w, m_sg_b, m_w_branch_att, m_w_branch_sg, m_w_out, m_norm_xattn_g, m_norm_mem_g, m_w_xq, m_w_xkv, m_w_xo, m_norm_ffn_g, m_w_ffn_in, m_w_ffn_out, m_norm_final_g, v_norm_mix_g, v_w_in, v_rel_bias, v_sg_ln_g, v_sg_ln_b, v_sg_w, v_sg_b, v_w_branch_att, v_w_branch_sg, v_w_out, v_norm_xattn_g, v_norm_mem_g, v_w_xq, v_w_xkv, v_w_xo, v_norm_ffn_g, v_w_ffn_in, v_w_ffn_out, v_norm_final_g):
    S, D = x.shape[1], x.shape[2]
    x2d, mem2d, tgt = x[0], mem[0], loss_target[0]

    big_names = ["w_in", "w_branch_att", "w_branch_sg", "w_out", "w_xq", "w_xkv", "w_xo", "w_ffn_in", "w_ffn_out"]
    col_sharded = [True, True, True, False, False, True, False, True, False]
    big_w = [a[0] for a in (w_in, w_branch_att, w_branch_sg, w_out, w_xq, w_xkv, w_xo, w_ffn_in, w_ffn_out)]
    big_m = [a[0] for a in (m_w_in, m_w_branch_att, m_w_branch_sg, m_w_out, m_w_xq, m_w_xkv, m_w_xo, m_w_ffn_in, m_w_ffn_out)]
    big_v = [a[0] for a in (v_w_in, v_w_branch_att, v_w_branch_sg, v_w_out, v_w_xq, v_w_xkv, v_w_xo, v_w_ffn_in, v_w_ffn_out)]

    own = (2 * lax.axis_index("x") + lax.axis_index("y")).astype(jnp.int32).reshape(1)
    core = lax.axis_index("c").astype(jnp.int32).reshape(1)
    placed = [place_shard(w, own, "place_" + nm) for w, nm in zip(big_w, big_names)]

    def whole(g4, i):
        R, C = big_w[i].shape
        return g4.reshape(4, R, C) if col_sharded[i] else g4.reshape(4 * R, C)

    W_in = whole(gather_weights(placed[:1])[0], 0)

    g_mix, g_xat, g_mem, g_ffn = norm_mix_g, norm_xattn_g, norm_mem_g, norm_ffn_g
    g_fin = norm_final_g.reshape(1, D)
    bias = _att_bias_table(rel_bias[0])
    tt = np.arange(SG_BLOCK)
    sg_mask = (tt[None, :] // CHUNK) <= (tt[:, None] // CHUNK)
    wm_f = jnp.where(sg_mask[None], sg_w[0], 0.0)
    pairs = wm_f.reshape(SG_PAIRS, 2, SG_BLOCK, SG_BLOCK)
    wpair = jnp.transpose(pairs, (0, 2, 1, 3)).reshape(SG_PAIRS, SG_BLOCK, 2 * SG_BLOCK).astype(BF16)
    wtpair = jnp.transpose(pairs, (0, 3, 1, 2)).reshape(SG_PAIRS, SG_BLOCK, 2 * SG_BLOCK).astype(BF16)
    maskpair = jnp.asarray(np.tile(sg_mask, (SG_PAIRS, 1, 2)), F32)
    bfull = jnp.repeat(sg_b[0].T, SG_GROUP_DIM, axis=1)
    lng, lnb = sg_ln_g[0].reshape(1, 512), sg_ln_b[0].reshape(1, 512)
    gid = np.arange(512) // SG_GROUP_DIM
    amean = jnp.asarray((gid[:, None] == gid[None, :]) / SG_GROUP_DIM, BF16)

    h1, z, *early = norm_mm(x2d, g_mix, W_in, tm=ROW_TILE, scale=None, name="norm_mm_in",
                            hook=gather_hook(placed[1:7], S // ROW_TILE))
    y_att, lse_att, *late = attn_fwd(z, bias, gather_hook(placed[7:], S // ATT_TILE))
    W_ba, W_bs, W_out, W_xq, W_xkv, W_xo, W_fi, W_fo = [whole(g4, i + 1) for i, g4 in enumerate(early + late)]
    y_sg, = sgu_fwd(z, amean, lng, lnb, wpair, bfull, tm=512)
    a_br, b_br, merged, x1 = merge_fwd(y_att, y_sg, z, x2d, W_ba, W_bs, W_out, tm=ROW_TILE)
    h2, xq = norm_mm(x1, g_xat, W_xq.reshape(1, D, D), tm=512, scale=XATT_HEAD_DIM ** -0.5, name="norm_mm_xq")
    mn, kv = norm_mm(mem2d, g_mem, W_xkv, tm=mem2d.shape[0], scale=None, name="norm_mm_kv")
    o_x, lse_x, x2 = xattn_fwd(xq, kv, W_xo, x1, tm=512)
    h3, gu = norm_mm(x2, g_ffn, W_fi, tm=ROW_TILE, scale=None, name="norm_mm_ffn")
    act, loss_vec, dx3, dg_fin = ffn_out_loss(gu, W_fo, x2, g_fin, tgt, tm=ROW_TILE)

    dgu = ffn_act_bwd(dx3, gu, W_fo, tm=ROW_TILE, nchunk=2)
    gW_fo = tn_mm(act, dx3, shards=1, tn=D, tk=DW_TOKENS, name="dw_ffn_out")
    gW_fi = tn_mm(h3, dgu, shards=4, tn=2 * W_fi.shape[2], tk=DW_TOKENS, name="dw_ffn_in")
    dx2, dg_ffn = mm_nt_norm_bwd(dgu, W_fi, x2, g_ffn, dx3, tm=ROW_TILE, name="dx_ffn")
    dq_x, dkv = xattn_bwd(dx2, xq, o_x, lse_x, kv, W_xo, tm=512)
    gW_xo = tn_mm(o_x, dx2, shards=1, tn=D, tk=DW_TOKENS, name="dw_xo")
    gW_xq = tn_mm(h2, dq_x, shards=1, tn=D, tk=DW_TOKENS, name="dw_xq")
    dx1, dg_xat = mm_nt_norm_bwd(dq_x, W_xq.reshape(1, D, D), x1, g_xat, dx2, tm=512, name="dx_xq")
    gW_xkv = tn_mm(mn, dkv, shards=4, tn=2 * D, tk=mem2d.shape[0], name="dw_xkv")
    _, dg_mem = mm_nt_norm_bwd(dkv.astype(BF16), W_xkv, mem2d, g_mem, None, tm=mem2d.shape[0], name="dx_mem")
    da, db, dgab, dy_att, dy_sg = merge_bwd(dx1, W_out, a_br, b_br, z, W_ba, W_bs, tm=ROW_TILE)
    gW_out = tn_mm(merged, dx1, shards=1, tn=D, tk=DW_TOKENS, name="dw_out")
    gW_ba = tn_mm(y_att, da, shards=4, tn=D, tk=DW_TOKENS, name="dw_branch_att")
    gW_bs = tn_mm(y_sg, db, shards=4, tn=D, tk=DW_TOKENS, name="dw_branch_sg")

    def canon(g, i):
        R, C = big_w[i].shape
        return g.reshape(4, 2, R // 2, C)

    def stage1(grads, idx, tag):
        gs = [canon(g, i) for g, i in zip(grads, idx)]
        recv = sibling_split(gs, "sibling_split_" + tag)
        return [chip_partial(g, r, core, "chip_partial_" + big_names[i]) for g, r, i in zip(gs, recv, idx)]

    parts_b = stage1([gW_ba, gW_bs, gW_out, gW_xq, gW_xkv, gW_xo, gW_fi, gW_fo], range(1, 9), "late")
    nt_bwd = S // ATT_TILE + ATT_WIN - 1
    dqkv, dbias, *from_chips_b = attn_bwd(z, y_att, dy_att, lse_att, bias, exchange_hook(parts_b, nt_bwd))
    duv, dwpair, dsgb_full, dlng, dlnb = sgu_bwd(z, dy_sg, amean, lng, lnb, wpair, wtpair, bfull, maskpair, tm=512)
    dwm = jnp.transpose(dwpair.reshape(SG_PAIRS, SG_BLOCK, 2, SG_BLOCK), (0, 2, 1, 3))
    dsgb = dsgb_full[:, ::SG_GROUP_DIM].T
    dz = [dqkv, duv, dgab]
    gW_in = tn_mm_pieces(h1, dz, shards=4, per=2, tk=DW_TOKENS, name="dw_in")
    parts_a = stage1([gW_in], [0], "w_in")
    d_rel = _rel_bias_grad(dbias)
    small_g = [d_rel, dlng, dlnb, dwm, dsgb, dg_xat, dg_mem, dg_ffn, dg_fin, loss_vec[0, :1]]
    n_dx = S // ROW_TILE
    dx, dg_mix, from_chips_a, small_all = mm_nt_norm_bwd(
        dz, W_in, x2d, g_mix, dx1, tm=ROW_TILE, name="dx_in",
        hook=merge_hooks(exchange_hook(parts_a, n_dx), allgather_hook(_pack_small(small_g), n_dx)))
    from_chips_a = [from_chips_a]

    parts, from_chips = parts_a + parts_b, from_chips_a + from_chips_b
    sums = [shard_sum(p, r, own, core, "shard_sum_" + nm) for p, r, nm in zip(parts, from_chips, big_names)]
    joined = sibling_join(sums)
    big_out = []
    for j, w, m_, v_, nm in zip(joined, big_w, big_m, big_v, big_names):
        g = j.reshape(w.shape)
        big_out.append((g,) + tuple(adamw(w, g, m_, v_, "adamw_" + nm)))

    small_w = [norm_mix_g, rel_bias, sg_ln_g, sg_ln_b, sg_w, sg_b, norm_xattn_g, norm_mem_g, norm_ffn_g, norm_final_g]
    small_m = [m_norm_mix_g, m_rel_bias, m_sg_ln_g, m_sg_ln_b, m_sg_w, m_sg_b, m_norm_xattn_g, m_norm_mem_g, m_norm_ffn_g, m_norm_final_g]
    small_v = [v_norm_mix_g, v_rel_bias, v_sg_ln_g, v_sg_ln_b, v_sg_w, v_sg_b, v_norm_xattn_g, v_norm_mem_g, v_norm_ffn_g, v_norm_final_g]
    shapes = [w.shape for w in small_w]
    g_sum = jnp.concatenate([allreduce_small(_pack_small([dg_mix])), sum_blocks(small_all)], axis=0)
    zero = jnp.zeros((1,), F32)
    d_s, m_s, v_s = adamw(_pack_small(small_w + [zero]), g_sum, _pack_small(small_m + [zero]), _pack_small(small_v + [zero]),
                          "adamw_small")
    sg_, sd, sm, sv_ = (_unpack_small(p, shapes + [(1,)]) for p in (g_sum, d_s, m_s, v_s))
    loss = sg_[-1][0]

    order = ["norm_mix_g", "w_in", "rel_bias", "sg_ln_g", "sg_ln_b", "sg_w", "sg_b", "w_branch_att", "w_branch_sg", "w_out",
             "norm_xattn_g", "norm_mem_g", "w_xq", "w_xkv", "w_xo", "norm_ffn_g", "w_ffn_in", "w_ffn_out", "norm_final_g"]
    small_names = ["norm_mix_g", "rel_bias", "sg_ln_g", "sg_ln_b", "sg_w", "sg_b", "norm_xattn_g", "norm_mem_g", "norm_ffn_g",
                   "norm_final_g"]
    res = {}
    for i, nm in enumerate(small_names):
        res[nm] = (sg_[i], sd[i], sm[i], sv_[i])
    for nm, outs in zip(big_names, big_out):
        res[nm] = tuple(o[None] for o in outs)
    return (loss, dx[None], *[res[nm][0] for nm in order], *[res[nm][1] for nm in order],
            *[res[nm][2] for nm in order], *[res[nm][3] for nm in order])
```

```python
import functools
from typing import NamedTuple

import numpy as np
import jax
import jax.numpy as jnp
from jax import lax
from jax.experimental import pallas as pl
from jax.experimental.pallas import tpu as pltpu

F32, BF16 = jnp.float32, jnp.bfloat16
MESH = pl.DeviceIdType.MESH

EPS = 1e-6
NEG_INF = -1e30
CHUNK = 64
N_PREV_CHUNKS = 8
ATT_HEADS, ATT_HEAD_DIM = 8, 64
REL_CLIP = 128
SG_BLOCK, SG_GROUPS, SG_GROUP_DIM = 128, 8, 64
XATT_HEADS, XATT_HEAD_DIM = 4, 256
ATT_TILE = 128
ATT_WIN = 5
ADAM_LR, ADAM_B1, ADAM_B2, ADAM_EPS, ADAM_WD, ADAM_STEP = 0.001, 0.9, 0.999, 1e-08, 0.01, 10

VMEM_LIMIT_V7X = 56 * 1024 * 1024
EW_BLOCK_BYTES = 2 << 20
ROW_TILE = 512
DW_TOKENS = 1024
HBM = pl.BlockSpec(memory_space=pl.ANY)


def _params(*sem):
    return pltpu.CompilerParams(dimension_semantics=sem, vmem_limit_bytes=VMEM_LIMIT_V7X)


def _full(shape):
    n = len(shape)
    return pl.BlockSpec(shape, lambda *_: (0,) * n, pipeline_mode=pl.Buffered(1))


def _acc(shape):
    n = len(shape)
    return pl.BlockSpec(shape, lambda *_: (0,) * n)


class Hook(NamedTuple):
    ins: tuple = ()
    out_shapes: tuple = ()
    aliased: bool = False
    sems: tuple = ()
    steps: tuple = ()


def _run_hook(hook, pos, h_in, h_out, h_sems):
    for step, where, fn in hook.steps:
        if where == pos:
            pl.when(pl.program_id(0) == step)(functools.partial(fn, h_in, h_out, h_sems))


def _hooked_call(body, hook, *, n_in, n_out, in_specs, out_specs, out_shape, scratch_shapes=(), **kw):
    nh = len(hook.ins)
    aliases = {n_in + i: n_out + i for i in range(nh)} if hook.aliased else {}
    return pl.pallas_call(
        body, in_specs=list(in_specs) + [HBM] * nh, out_specs=list(out_specs) + [HBM] * len(hook.out_shapes),
        out_shape=list(out_shape) + list(hook.out_shapes), scratch_shapes=list(scratch_shapes) + list(hook.sems),
        input_output_aliases=aliases, **kw)


def _nt(a, b):
    return lax.dot_general(a, b, (((1,), (1,)), ((), ())), preferred_element_type=F32)


def _tn(a, b):
    return lax.dot_general(a, b, (((0,), (0,)), ((), ())), preferred_element_type=F32)


def _nn(a, b):
    return jnp.dot(a, b, preferred_element_type=F32)


def _sigmoid(x):
    return 1.0 / (1.0 + jnp.exp(-x))


_GELU_C = float(np.sqrt(2.0 / np.pi))


def _gelu(x):
    t = jnp.tanh(_GELU_C * (x + 0.044715 * (x * x * x)))
    return x * (0.5 * (1.0 + t))


def _gelu_grad(x):
    t = jnp.tanh(_GELU_C * (x + 0.044715 * (x * x * x)))
    return 0.5 * (1.0 + t) + 0.5 * x * (1.0 - t * t) * (_GELU_C * (1.0 + 3.0 * 0.044715 * x * x))


def _rms_stats(xf):
    rstd = lax.rsqrt(jnp.mean(xf * xf, axis=-1, keepdims=True) + EPS)
    return rstd, xf * rstd


def _rms_bwd(xhat, rstd, g, dh):
    dxh = dh * g
    dx = rstd * (dxh - xhat * jnp.mean(dxh * xhat, axis=-1, keepdims=True))
    return dx, dh * xhat


def norm_mm(x, g, w, *, tm, scale, name, hook=Hook()):
    S, D = x.shape
    J, _, C = w.shape
    nh, nho = len(hook.ins), len(hook.out_shapes)

    def body(x_ref, g_ref, w_ref, *rest):
        h_in, (h_ref, z_ref), h_out, h_sems = rest[:nh], rest[nh:nh + 2], rest[nh + 2:nh + 2 + nho], rest[nh + 2 + nho:]
        _run_hook(hook, "before", h_in, h_out, h_sems)
        _, xhat = _rms_stats(x_ref[...])
        h = (xhat * g_ref[...]).astype(BF16)
        h_ref[...] = h
        for j in range(J):
            acc = _nn(h, w_ref[j])
            if scale is not None:
                acc = acc * scale
            z_ref[:, j * C:(j + 1) * C] = acc.astype(BF16)
        _run_hook(hook, "after", h_in, h_out, h_sems)

    return _hooked_call(
        body, hook, n_in=3, n_out=2, grid=(S // tm,), name=name,
        in_specs=[pl.BlockSpec((tm, D), lambda i: (i, 0)), _full((1, D)), _full((J, D, C))],
        out_specs=[pl.BlockSpec((tm, D), lambda i: (i, 0)), pl.BlockSpec((tm, J * C), lambda i: (i, 0))],
        out_shape=[jax.ShapeDtypeStruct((S, D), BF16), jax.ShapeDtypeStruct((S, J * C), BF16)],
        compiler_params=_params("arbitrary"),
    )(x, g, w, *hook.ins)


def _att_window_specs(nt, col):
    return [pl.BlockSpec((ATT_TILE, 512), lambda t, j=j: (jnp.clip(t - (ATT_WIN - 1) + j, 0, nt - 1), col))
            for j in range(ATT_WIN)]


ATT_SCALE = ATT_HEAD_DIM ** -0.5


def _att_scores(q_scaled, k, bias, valid):
    return jnp.where(valid, _nt(q_scaled, k) + bias, NEG_INF)


def _att_valid(t):
    kpos = lax.broadcasted_iota(jnp.int32, (ATT_TILE, ATT_WIN * ATT_TILE), 1) + (t - (ATT_WIN - 1)) * ATT_TILE
    return kpos >= 0


def attn_fwd(z, bias, hook=Hook()):
    S = z.shape[0]
    nt = S // ATT_TILE
    n_in, nh, nho = 2 + 2 * ATT_WIN, len(hook.ins), len(hook.out_shapes)

    def body(q_ref, *refs):
        k_refs, v_refs = refs[:ATT_WIN], refs[ATT_WIN:2 * ATT_WIN]
        bias_ref = refs[2 * ATT_WIN]
        rest = refs[2 * ATT_WIN + 1:]
        h_in, (y_ref, lse_ref), h_out = rest[:nh], rest[nh:nh + 2], rest[nh + 2:nh + 2 + nho]
        s_scr, p_scr = rest[nh + 2 + nho:nh + 4 + nho]
        h_sems = rest[nh + 4 + nho:]
        _run_hook(hook, "before", h_in, h_out, h_sems)
        valid = _att_valid(pl.program_id(0))
        heads = [slice(h * ATT_HEAD_DIM, (h + 1) * ATT_HEAD_DIM) for h in range(ATT_HEADS)]
        for h, sl in enumerate(heads):
            k = jnp.concatenate([r[:, sl] for r in k_refs], axis=0)
            s_scr[h] = _att_scores(q_ref[:, sl] * ATT_SCALE, k, bias_ref[h], valid)
        stats = []
        for h in range(ATT_HEADS):
            s = s_scr[h]
            m = jnp.max(s, axis=-1, keepdims=True)
            p = jnp.exp(s - m)
            stats.append((m, jnp.sum(p, axis=-1, keepdims=True)))
            p_scr[h] = p.astype(BF16)
        for h, sl in enumerate(heads):
            m, l = stats[h]
            v = jnp.concatenate([r[:, sl] for r in v_refs], axis=0)
            y_ref[:, sl] = (_nn(p_scr[h], v) / l).astype(BF16)
            lse_ref[:, h:h + 1] = m + jnp.log(l)
        _run_hook(hook, "after", h_in, h_out, h_sems)

    tile = lambda col: pl.BlockSpec((ATT_TILE, 512), lambda t: (t, col))
    return _hooked_call(
        body, hook, n_in=n_in, n_out=2, grid=(nt,), name="attn_fwd",
        in_specs=[tile(0)] + _att_window_specs(nt, 1) + _att_window_specs(nt, 2) + [_full(bias.shape)],
        out_specs=[tile(0), pl.BlockSpec((ATT_TILE, ATT_HEADS), lambda t: (t, 0))],
        out_shape=[jax.ShapeDtypeStruct((S, 512), BF16), jax.ShapeDtypeStruct((S, ATT_HEADS), F32)],
        scratch_shapes=[pltpu.VMEM((ATT_HEADS, ATT_TILE, ATT_WIN * ATT_TILE), F32),
                        pltpu.VMEM((ATT_HEADS, ATT_TILE, ATT_WIN * ATT_TILE), BF16)],
        compiler_params=_params("arbitrary"),
    )(z, *([z] * (2 * ATT_WIN)), bias, *hook.ins)


SG_PAIRS = SG_GROUPS // 2


def _group_mean(x, a_ref):
    rows = x.shape[0]
    hi = x.astype(BF16)
    lo = (x - hi.astype(F32)).astype(BF16)
    both = _nn(jnp.concatenate([hi, lo], axis=0), a_ref[...])
    return both[:rows] + both[rows:]


def _pair_diag(x, ref):
    first = lax.broadcasted_iota(jnp.int32, x.shape, 1) < SG_GROUP_DIM
    ref[0:SG_BLOCK, :] = jnp.where(first, x, 0.0).astype(BF16)
    ref[SG_BLOCK:2 * SG_BLOCK, :] = jnp.where(first, 0.0, x).astype(BF16)


def _sgu_norm(zv_ref, a_ref, lng_ref, lnb_ref):
    v = _gelu(zv_ref[...].astype(F32))
    vc = v - _group_mean(v, a_ref)
    rstd = lax.rsqrt(_group_mean(vc * vc, a_ref) + EPS)
    xhat = vc * rstd
    return xhat, rstd, xhat * lng_ref[...] + lnb_ref[...]


def _sgu_rows(b):
    return pl.ds(pl.multiple_of(b * SG_BLOCK, SG_BLOCK), SG_BLOCK)


def sgu_fwd(z, amean, lng, lnb, wpair, bfull, *, tm, hook=Hook()):
    S = z.shape[0]
    nh, nho = len(hook.ins), len(hook.out_shapes)

    def body(zu_ref, zv_ref, a_ref, lng_ref, lnb_ref, wpair_ref, bfull_ref, *rest):
        h_in, y_ref, h_out = rest[:nh], rest[nh], rest[nh + 1:nh + 1 + nho]
        vn_scr, vbd = rest[nh + 1 + nho:nh + 3 + nho]
        h_sems = rest[nh + 3 + nho:]
        _run_hook(hook, "before", h_in, h_out, h_sems)
        vn_scr[...] = _sgu_norm(zv_ref, a_ref, lng_ref, lnb_ref)[2]

        def block(b, carry):
            rows = _sgu_rows(b)
            u = _gelu(zu_ref[rows, :].astype(F32))
            for p in range(SG_PAIRS):
                lanes = slice(p * 128, (p + 1) * 128)
                _pair_diag(vn_scr[rows, lanes], vbd.at[p])
                sv = _nn(wpair_ref[p], vbd[p]) + bfull_ref[:, lanes]
                y_ref[rows, lanes] = (u[:, lanes] * sv).astype(BF16)
            return carry

        lax.fori_loop(0, tm // SG_BLOCK, block, 0)
        _run_hook(hook, "after", h_in, h_out, h_sems)

    tile = lambda col: pl.BlockSpec((tm, 512), lambda i: (i, col))
    smalls = [amean, lng, lnb, wpair, bfull]
    return _hooked_call(
        body, hook, n_in=2 + len(smalls), n_out=1, grid=(S // tm,), name="sgu_fwd",
        in_specs=[tile(3), tile(4)] + [_full(a.shape) for a in smalls],
        out_specs=[tile(0)], out_shape=[jax.ShapeDtypeStruct((S, 512), BF16)],
        scratch_shapes=[pltpu.VMEM((tm, 512), F32), pltpu.VMEM((SG_PAIRS, 2 * SG_BLOCK, 128), BF16)],
        compiler_params=_params("arbitrary"),
    )(z, z, *smalls, *hook.ins)


def _gate_specs(tm):
    return [pl.BlockSpec((tm, 512), lambda i, c=c: (i, c)) for c in (5, 6, 7, 8)]


def merge_fwd(y_att, y_sg, z, x, w_ba, w_bs, w_out, *, tm):
    S, D = x.shape
    J, _, C = w_ba.shape

    def body(ya_ref, ys_ref, g0, g1, g2, g3, x_ref, wba_ref, wbs_ref, wo_ref, a_ref, b_ref, m_ref, x1_ref):
        ya, ys = ya_ref[...], ys_ref[...]
        a = jnp.concatenate([_nn(ya, wba_ref[j]) for j in range(J)], axis=1)
        b = jnp.concatenate([_nn(ys, wbs_ref[j]) for j in range(J)], axis=1)
        ga = jnp.concatenate([g0[...], g1[...]], axis=1).astype(F32)
        gb = jnp.concatenate([g2[...], g3[...]], axis=1).astype(F32)
        a_ref[...] = a.astype(BF16)
        b_ref[...] = b.astype(BF16)
        merged = (_sigmoid(ga) * a + _sigmoid(gb) * b).astype(BF16)
        m_ref[...] = merged
        x1_ref[...] = x_ref[...] + _nn(merged, wo_ref[...])

    row = lambda n: pl.BlockSpec((tm, n), lambda i: (i, 0))
    return pl.pallas_call(
        body, grid=(S // tm,), name="merge_fwd",
        in_specs=[row(512), row(512)] + _gate_specs(tm) + [row(D), _full(w_ba.shape), _full(w_bs.shape), _full(w_out.shape)],
        out_specs=[row(D)] * 4,
        out_shape=[jax.ShapeDtypeStruct((S, D), BF16)] * 3 + [jax.ShapeDtypeStruct((S, D), F32)],
        compiler_params=_params("parallel"),
    )(y_att, y_sg, z, z, z, z, x, w_ba, w_bs, w_out)


def xattn_fwd(xq, kv, w_xo, x1, *, tm):
    S, D = xq.shape
    dh = XATT_HEAD_DIM

    def body(q_ref, kv_ref, wo_ref, x1_ref, o_ref, lse_ref, x2_ref):
        outs = []
        for h in range(XATT_HEADS):
            s = _nt(q_ref[:, h * dh:(h + 1) * dh], kv_ref[:, h * dh:(h + 1) * dh])
            m = jnp.max(s, axis=-1, keepdims=True)
            p = jnp.exp(s - m)
            l = jnp.sum(p, axis=-1, keepdims=True)
            outs.append((_nn(p.astype(BF16), kv_ref[:, D + h * dh:D + (h + 1) * dh]) / l).astype(BF16))
            lse_ref[:, h:h + 1] = m + jnp.log(l)
        o = jnp.concatenate(outs, axis=1)
        o_ref[...] = o
        x2_ref[...] = x1_ref[...] + _nn(o, wo_ref[...])

    row = lambda n: pl.BlockSpec((tm, n), lambda i: (i, 0))
    return pl.pallas_call(
        body, grid=(S // tm,), name="xattn_fwd",
        in_specs=[row(D), _full(kv.shape), _full(w_xo.shape), row(D)],
        out_specs=[row(D), row(XATT_HEADS), row(D)],
        out_shape=[jax.ShapeDtypeStruct((S, D), BF16), jax.ShapeDtypeStruct((S, XATT_HEADS), F32),
                   jax.ShapeDtypeStruct((S, D), F32)],
        compiler_params=_params("parallel"),
    )(xq, kv, w_xo, x1)


def ffn_out_loss(gu, w, x2, g, target, *, tm):
    S, D = x2.shape
    F = w.shape[0]

    def body(gu_ref, w_ref, x2_ref, g_ref, t_ref, act_ref, loss_ref, dx_ref, dg_ref):
        @pl.when(pl.program_id(0) == 0)
        def _():
            loss_ref[...] = jnp.zeros_like(loss_ref)
            dg_ref[...] = jnp.zeros_like(dg_ref)

        gate = gu_ref[:, :F].astype(F32)
        up = gu_ref[:, F:].astype(F32)
        act = (gate * _sigmoid(gate) * up).astype(BF16)
        act_ref[...] = act
        gv = g_ref[...]
        rstd, xhat = _rms_stats(x2_ref[...] + _nn(act, w_ref[...]))
        err = xhat * gv - t_ref[...]
        loss_ref[...] += 0.5 * jnp.sum(jnp.mean(err * err, axis=-1, keepdims=True))
        dx, dgc = _rms_bwd(xhat, rstd, gv, err * (1.0 / D))
        dx_ref[...] = dx
        dg_ref[...] += jnp.sum(dgc, axis=0, keepdims=True)

    row = lambda n: pl.BlockSpec((tm, n), lambda i: (i, 0))
    return pl.pallas_call(
        body, grid=(S // tm,), name="ffn_out_loss",
        in_specs=[row(2 * F), _full(w.shape), row(D), _full((1, D)), row(D)],
        out_specs=[row(F), _acc((1, 128)), row(D), _acc((1, D))],
        out_shape=[jax.ShapeDtypeStruct((S, F), BF16), jax.ShapeDtypeStruct((1, 128), F32),
                   jax.ShapeDtypeStruct((S, D), F32), jax.ShapeDtypeStruct((1, D), F32)],
        compiler_params=_params("arbitrary"),
    )(gu, w, x2, g, target)


def ffn_act_bwd(dx3, gu, w, *, tm, nchunk):
    S, D = dx3.shape
    F = w.shape[0]
    cn = F // nchunk

    def body(dx_ref, gu_ref, w_ref, dgu_ref):
        dxb = dx_ref[...].astype(BF16)
        for j in range(nchunk):
            dact = _nt(dxb, w_ref[j * cn:(j + 1) * cn, :])
            gate = gu_ref[:, j * cn:(j + 1) * cn].astype(F32)
            up = gu_ref[:, F + j * cn:F + (j + 1) * cn].astype(F32)
            sg = _sigmoid(gate)
            dgu_ref[:, j * cn:(j + 1) * cn] = (dact * up * (sg * (1.0 + gate * (1.0 - sg)))).astype(BF16)
            dgu_ref[:, F + j * cn:F + (j + 1) * cn] = (dact * (gate * sg)).astype(BF16)

    row = lambda n: pl.BlockSpec((tm, n), lambda i: (i, 0))
    return pl.pallas_call(
        body, grid=(S // tm,), name="ffn_act_bwd",
        in_specs=[row(D), row(2 * F), _full(w.shape)], out_specs=row(2 * F),
        out_shape=jax.ShapeDtypeStruct((S, 2 * F), BF16), compiler_params=_params("parallel"),
    )(dx3, gu, w)


def _overlaps(widths, lo, hi):
    out, off = [], 0
    for p, wd in enumerate(widths):
        a, b = max(lo, off), min(hi, off + wd)
        if a < b:
            out.append((p, a - off, b - off, a - lo))
        off += wd
    return out


def tn_mm_pieces(a, pieces, *, shards, per, tk, name):
    K, M = a.shape
    widths = [p.shape[1] for p in pieces]
    C = sum(widths) // shards
    tn = per * C
    nk = K // tk
    n_tiles = shards // per

    def body(a_ref, *refs):
        p_refs, o_ref, acc_ref = refs[:len(pieces)], refs[len(pieces)], refs[len(pieces) + 1]
        n, k = pl.program_id(0), pl.program_id(1)

        @pl.when(k == 0)
        def _():
            acc_ref[...] = jnp.zeros_like(acc_ref)

        av = a_ref[...]
        for tile in range(n_tiles):
            @pl.when(n == tile)
            def _(tile=tile):
                for p, c0, c1, at in _overlaps(widths, tile * tn, (tile + 1) * tn):
                    acc_ref[:, at:at + c1 - c0] += _tn(av, p_refs[p][:, c0:c1])

        @pl.when(k == nk - 1)
        def _():
            for s in range(per):
                o_ref[s] = acc_ref[:, s * C:(s + 1) * C].astype(BF16)

    return pl.pallas_call(
        body, grid=(n_tiles, nk), name=name,
        in_specs=[pl.BlockSpec((tk, M), lambda n, k: (k, 0))] + [pl.BlockSpec((tk, wd), lambda n, k: (k, 0)) for wd in widths],
        out_specs=pl.BlockSpec((per, M, C), lambda n, k: (n, 0, 0)), out_shape=jax.ShapeDtypeStruct((shards, M, C), BF16),
        scratch_shapes=[pltpu.VMEM((M, tn), F32)], compiler_params=_params("parallel", "arbitrary"),
    )(a, *pieces)


def tn_mm(a, b, *, shards, tn, tk, name):
    K, M = a.shape
    N = b.shape[1]
    C = N // shards
    per = tn // C
    nk = K // tk

    def body(a_ref, b_ref, o_ref, acc_ref):
        k = pl.program_id(1)

        @pl.when(k == 0)
        def _():
            acc_ref[...] = jnp.zeros_like(acc_ref)

        acc_ref[...] += _tn(a_ref[...].astype(BF16), b_ref[...].astype(BF16))

        @pl.when(k == nk - 1)
        def _():
            if shards > 1:
                for s in range(per):
                    o_ref[s] = acc_ref[:, s * C:(s + 1) * C].astype(BF16)
            else:
                o_ref[...] = acc_ref[...].astype(BF16)

    if shards > 1:
        out_spec = pl.BlockSpec((per, M, C), lambda n, k: (n, 0, 0))
        out_shape = jax.ShapeDtypeStruct((shards, M, C), BF16)
    else:
        out_spec = pl.BlockSpec((M, tn), lambda n, k: (0, n))
        out_shape = jax.ShapeDtypeStruct((M, N), BF16)
    return pl.pallas_call(
        body, grid=(N // tn, nk), name=name,
        in_specs=[pl.BlockSpec((tk, M), lambda n, k: (k, 0)), pl.BlockSpec((tk, tn), lambda n, k: (k, n))],
        out_specs=out_spec, out_shape=out_shape, scratch_shapes=[pltpu.VMEM((M, tn), F32)],
        compiler_params=_params("parallel", "arbitrary"),
    )(a, b)


def mm_nt_norm_bwd(dy, w, x, g, dx_in, *, tm, name, hook=Hook()):
    S, D = x.shape
    J, _, C = w.shape
    has_in = dx_in is not None
    dys = list(dy) if isinstance(dy, (list, tuple)) else [dy]
    widths = [d.shape[1] for d in dys]
    nd = len(dys)
    n_in, nh, nho = nd + 3 + has_in, len(hook.ins), len(hook.out_shapes)

    def body(*refs):
        dy_refs = refs[:nd]
        w_ref, x_ref, g_ref = refs[nd:nd + 3]
        dxin_ref = refs[nd + 3] if has_in else None
        rest = refs[n_in:]
        h_in, (dx_ref, dg_ref), h_out, h_sems = rest[:nh], rest[nh:nh + 2], rest[nh + 2:nh + 2 + nho], rest[nh + 2 + nho:]
        _run_hook(hook, "before", h_in, h_out, h_sems)

        @pl.when(pl.program_id(0) == 0)
        def _():
            dg_ref[...] = jnp.zeros_like(dg_ref)

        dh = None
        for j in range(J):
            for p, c0, c1, at in _overlaps(widths, j * C, (j + 1) * C):
                part = _nt(dy_refs[p][:, c0:c1], w_ref[j, :, at:at + c1 - c0])
                dh = part if dh is None else dh + part
        rstd, xhat = _rms_stats(x_ref[...])
        dx, dgc = _rms_bwd(xhat, rstd, g_ref[...], dh)
        dx_ref[...] = dx + dxin_ref[...] if has_in else dx
        dg_ref[...] += jnp.sum(dgc, axis=0, keepdims=True)
        _run_hook(hook, "after", h_in, h_out, h_sems)

    row = lambda n: pl.BlockSpec((tm, n), lambda i: (i, 0))
    ins = dys + [w, x, g] + ([dx_in] if has_in else [])
    return _hooked_call(
        body, hook, n_in=n_in, n_out=2, grid=(S // tm,), name=name,
        in_specs=[row(wd) for wd in widths] + [_full(w.shape), row(D), _full((1, D))] + ([row(D)] if has_in else []),
        out_specs=[row(D), _acc((1, D))],
        out_shape=[jax.ShapeDtypeStruct((S, D), F32), jax.ShapeDtypeStruct((1, D), F32)],
        compiler_params=_params("arbitrary"),
    )(*ins, *hook.ins)


def xattn_bwd(dx2, xq, o, lse, kv, w_xo, *, tm):
    S, D = xq.shape
    M = kv.shape[0]
    dh = XATT_HEAD_DIM

    def body(dx_ref, q_ref, o_ref, lse_ref, kv_ref, wo_ref, dq_ref, dkv_ref):
        @pl.when(pl.program_id(0) == 0)
        def _():
            dkv_ref[...] = jnp.zeros_like(dkv_ref)

        do = _nt(dx_ref[...].astype(BF16), wo_ref[...])
        for h in range(XATT_HEADS):
            hs = slice(h * dh, (h + 1) * dh)
            vs = slice(D + h * dh, D + (h + 1) * dh)
            q, k, v = q_ref[:, hs], kv_ref[:, hs], kv_ref[:, vs]
            do_h = do[:, hs]
            do_b = do_h.astype(BF16)
            p = jnp.exp(_nt(q, k) - lse_ref[:, h:h + 1])
            delta = jnp.sum(do_h * o_ref[:, hs].astype(F32), axis=-1, keepdims=True)
            ds = (p * (_nt(do_b, v) - delta)).astype(BF16)
            dq_ref[:, hs] = (_nn(ds, k) * (dh ** -0.5)).astype(BF16)
            dkv_ref[:, hs] += _tn(ds, q)
            dkv_ref[:, vs] += _tn(p.astype(BF16), do_b)

    row = lambda n: pl.BlockSpec((tm, n), lambda i: (i, 0))
    return pl.pallas_call(
        body, grid=(S // tm,), name="xattn_bwd",
        in_specs=[row(D), row(D), row(D), row(XATT_HEADS), _full(kv.shape), _full(w_xo.shape)],
        out_specs=[row(D), _acc((M, 2 * D))],
        out_shape=[jax.ShapeDtypeStruct((S, D), BF16), jax.ShapeDtypeStruct((M, 2 * D), F32)],
        compiler_params=_params("arbitrary"),
    )(dx2, xq, o, lse, kv, w_xo)


def merge_bwd(dx1, w_out, a, b, z, w_ba, w_bs, *, tm, hook=Hook()):
    S, D = dx1.shape
    J, W, C = w_ba.shape
    nh, nho = len(hook.ins), len(hook.out_shapes)

    def body(dx_ref, wo_ref, a_ref, b_ref, g0, g1, g2, g3, wba_ref, wbs_ref, *rest):
        h_in, (da_ref, db_ref, dg_ref, dya_ref, dys_ref) = rest[:nh], rest[nh:nh + 5]
        h_out, h_sems = rest[nh + 5:nh + 5 + nho], rest[nh + 5 + nho:]
        _run_hook(hook, "before", h_in, h_out, h_sems)
        dm = _nt(dx_ref[...].astype(BF16), wo_ref[...])
        sa = _sigmoid(jnp.concatenate([g0[...], g1[...]], axis=1).astype(F32))
        sb = _sigmoid(jnp.concatenate([g2[...], g3[...]], axis=1).astype(F32))
        dg_ref[:, :D] = (dm * a_ref[...].astype(F32) * (sa * (1.0 - sa))).astype(BF16)
        dg_ref[:, D:] = (dm * b_ref[...].astype(F32) * (sb * (1.0 - sb))).astype(BF16)
        da = (dm * sa).astype(BF16)
        db = (dm * sb).astype(BF16)
        da_ref[...] = da
        db_ref[...] = db
        dya = _nt(da[:, 0:C], wba_ref[0])
        dys = _nt(db[:, 0:C], wbs_ref[0])
        for j in range(1, J):
            dya += _nt(da[:, j * C:(j + 1) * C], wba_ref[j])
            dys += _nt(db[:, j * C:(j + 1) * C], wbs_ref[j])
        dya_ref[...] = dya.astype(BF16)
        dys_ref[...] = dys.astype(BF16)
        _run_hook(hook, "after", h_in, h_out, h_sems)

    row = lambda n: pl.BlockSpec((tm, n), lambda i: (i, 0))
    return _hooked_call(
        body, hook, n_in=10, n_out=5, grid=(S // tm,), name="merge_bwd",
        in_specs=[row(D), _full(w_out.shape), row(D), row(D)] + _gate_specs(tm) + [_full(w_ba.shape), _full(w_bs.shape)],
        out_specs=[row(D), row(D), row(2 * D), row(W), row(W)],
        out_shape=[jax.ShapeDtypeStruct((S, D), BF16)] * 2 + [jax.ShapeDtypeStruct((S, 2 * D), BF16)]
        + [jax.ShapeDtypeStruct((S, W), BF16)] * 2,
        compiler_params=_params("arbitrary"),
    )(dx1, w_out, a, b, z, z, z, z, w_ba, w_bs, *hook.ins)


def attn_bwd(z, y, dy, lse, bias, hook=Hook()):
    S = z.shape[0]
    nt = S // ATT_TILE
    back = ATT_WIN - 1
    n_in, nh, nho = 5 + 2 * ATT_WIN, len(hook.ins), len(hook.out_shapes)

    def body(q_ref, *refs):
        k_refs, v_refs = refs[:ATT_WIN], refs[ATT_WIN:2 * ATT_WIN]
        y_ref, dy_ref, lse_ref, bias_ref = refs[2 * ATT_WIN:2 * ATT_WIN + 4]
        rest = refs[2 * ATT_WIN + 4:]
        h_in, (dqkv_ref, dbias_ref), h_out = rest[:nh], rest[nh:nh + 2], rest[nh + 2:nh + 2 + nho]
        dk_acc, dv_acc, dq_buf = rest[nh + 2 + nho:nh + 5 + nho]
        h_sems = rest[nh + 5 + nho:]
        t = pl.program_id(0)
        _run_hook(hook, "before", h_in, h_out, h_sems)

        @pl.when(t == 0)
        def _():
            dbias_ref[...] = jnp.zeros_like(dbias_ref)
            dk_acc[...] = jnp.zeros_like(dk_acc)
            dv_acc[...] = jnp.zeros_like(dv_acc)
            dq_buf[...] = jnp.zeros_like(dq_buf)

        @pl.when(t < nt)
        def _():
            valid = _att_valid(t)
            q_t = (q_ref[...] * ATT_SCALE).astype(F32).T.astype(BF16)
            do_t = dy_ref[...].astype(F32).T.astype(BF16)
            for h in range(ATT_HEADS):
                sl = slice(h * ATT_HEAD_DIM, (h + 1) * ATT_HEAD_DIM)
                k = jnp.concatenate([r[:, sl] for r in k_refs], axis=0)
                v = jnp.concatenate([r[:, sl] for r in v_refs], axis=0)
                q, do_b = q_ref[:, sl] * ATT_SCALE, dy_ref[:, sl]
                p = jnp.exp(_att_scores(q, k, bias_ref[h], valid) - lse_ref[:, h:h + 1])
                delta = jnp.sum(do_b.astype(F32) * y_ref[:, sl].astype(F32), axis=-1, keepdims=True)
                ds = p * (_nt(do_b, v) - delta)
                dbias_ref[h] += ds
                ds_b = ds.astype(BF16)
                dq_buf[t % ATT_WIN, :, sl] = (_nn(ds_b, k) * ATT_SCALE).astype(BF16)
                dk_w = _nn(q_t[sl, :], ds_b)
                dv_w = _nn(do_t[sl, :], p.astype(BF16))
                for i in range(ATT_WIN):
                    slot = (t + 1 + i) % ATT_WIN
                    cols = slice(i * ATT_TILE, (i + 1) * ATT_TILE)
                    if i == back:
                        dk_acc[slot, sl, :] = dk_w[:, cols]
                        dv_acc[slot, sl, :] = dv_w[:, cols]
                    else:
                        dk_acc[slot, sl, :] += dk_w[:, cols]
                        dv_acc[slot, sl, :] += dv_w[:, cols]

        done = (t + 1) % ATT_WIN
        dqkv_ref[:, 0:512] = dq_buf[done]
        dqkv_ref[:, 512:1024] = dk_acc[done].T.astype(BF16)
        dqkv_ref[:, 1024:1536] = dv_acc[done].T.astype(BF16)
        _run_hook(hook, "after", h_in, h_out, h_sems)

    last = nt - 1
    tile = lambda col, n=512: pl.BlockSpec((ATT_TILE, n), lambda t: (jnp.minimum(t, last), col))
    late = pl.BlockSpec((ATT_TILE, 1536), lambda t: (jnp.maximum(t - back, 0), 0))
    return _hooked_call(
        body, hook, n_in=n_in, n_out=2, grid=(nt + back,), name="attn_bwd",
        in_specs=[tile(0)] + _att_window_specs(nt, 1) + _att_window_specs(nt, 2)
        + [tile(0), tile(0), tile(0, ATT_HEADS), _full(bias.shape)],
        out_specs=[late, _acc(bias.shape)],
        out_shape=[jax.ShapeDtypeStruct((S, 1536), BF16), jax.ShapeDtypeStruct(bias.shape, F32)],
        scratch_shapes=[pltpu.VMEM((ATT_WIN, 512, ATT_TILE), F32)] * 2 + [pltpu.VMEM((ATT_WIN, ATT_TILE, 512), BF16)],
        compiler_params=_params("arbitrary"),
    )(z, *([z] * (2 * ATT_WIN)), y, dy, lse, bias, *hook.ins)


def sgu_bwd(z, dy, amean, lng, lnb, wpair, wtpair, bfull, maskpair, *, tm):
    S = z.shape[0]
    n = S // tm

    def body(zu_ref, zv_ref, dy_ref, a_ref, lng_ref, lnb_ref, wpair_ref, wtpair_ref, bfull_ref, mask_ref,
             duv_ref, dw_ref, dsgb_ref, dlng_ref, dlnb_ref, xhat_scr, rstd_scr, vn_scr, dxh_scr, vbd, dbd,
             w_acc, b_acc, g_acc, s_acc):
        i = pl.program_id(0)

        @pl.when(i == 0)
        def _():
            for r in (w_acc, b_acc, g_acc, s_acc):
                r[...] = jnp.zeros_like(r)

        xhat_scr[...], rstd_scr[...], vn_scr[...] = _sgu_norm(zv_ref, a_ref, lng_ref, lnb_ref)

        def block(b, carry):
            rows = _sgu_rows(b)
            zu = zu_ref[rows, :].astype(F32)
            u, du = _gelu(zu), _gelu_grad(zu)
            dyv = dy_ref[rows, :].astype(F32)
            for p in range(SG_PAIRS):
                lanes = slice(p * 128, (p + 1) * 128)
                _pair_diag(vn_scr[rows, lanes], vbd.at[p])
                sv = _nn(wpair_ref[p], vbd[p]) + bfull_ref[:, lanes]
                duv_ref[rows, lanes] = (dyv[:, lanes] * sv * du[:, lanes]).astype(BF16)
                dsv = dyv[:, lanes] * u[:, lanes]
                b_acc[:, lanes] += dsv
                w_acc[p] += _nt(dsv.astype(BF16), vbd[p])
                _pair_diag(dsv, dbd.at[p])
                dvn = _nn(wtpair_ref[p], dbd[p])
                g_acc[:, lanes] += dvn * xhat_scr[rows, lanes]
                s_acc[:, lanes] += dvn
                dxh_scr[rows, lanes] = dvn * lng_ref[:, lanes]
            return carry

        lax.fori_loop(0, tm // SG_BLOCK, block, 0)
        dxh, xhat = dxh_scr[...], xhat_scr[...]
        dv = rstd_scr[...] * (dxh - _group_mean(dxh, a_ref) - xhat * _group_mean(dxh * xhat, a_ref))
        duv_ref[:, 512:1024] = (dv * _gelu_grad(zv_ref[...].astype(F32))).astype(BF16)

        @pl.when(i == n - 1)
        def _():
            dw_ref[...] = w_acc[...] * mask_ref[...]
            dsgb_ref[...] = _group_mean(b_acc[...], a_ref) * float(SG_GROUP_DIM)
            dlng_ref[...] = jnp.sum(g_acc[...], axis=0, keepdims=True)
            dlnb_ref[...] = jnp.sum(s_acc[...], axis=0, keepdims=True)

    tile = lambda col: pl.BlockSpec((tm, 512), lambda i: (i, col))
    smalls = [amean, lng, lnb, wpair, wtpair, bfull, maskpair]
    pair_shape = (SG_PAIRS, SG_BLOCK, 2 * SG_BLOCK)
    return pl.pallas_call(
        body, grid=(n,), name="sgu_bwd",
        in_specs=[tile(3), tile(4), tile(0)] + [_full(a.shape) for a in smalls],
        out_specs=[pl.BlockSpec((tm, 1024), lambda i: (i, 0)), _acc(pair_shape), _acc((SG_BLOCK, 512)),
                   _acc((1, 512)), _acc((1, 512))],
        out_shape=[jax.ShapeDtypeStruct((S, 1024), BF16), jax.ShapeDtypeStruct(pair_shape, F32),
                   jax.ShapeDtypeStruct((SG_BLOCK, 512), F32), jax.ShapeDtypeStruct((1, 512), F32),
                   jax.ShapeDtypeStruct((1, 512), F32)],
        scratch_shapes=[pltpu.VMEM((tm, 512), F32)] * 4
        + [pltpu.VMEM((SG_PAIRS, 2 * SG_BLOCK, 128), BF16)] * 2
        + [pltpu.VMEM(pair_shape, F32)] + [pltpu.VMEM((SG_BLOCK, 512), F32)] * 3,
        compiler_params=_params("arbitrary"),
    )(z, z, dy, *smalls)


def bias_colsum(p):
    H, _, L = p.shape
    near_lo = (ATT_WIN - 1) * ATT_TILE - REL_CLIP + 1
    near_hi = ATT_WIN * ATT_TILE

    def body(p_ref, gw_ref, far_ref):
        k = lax.broadcasted_iota(jnp.int32, (1, L), 1)
        is_far = (k < near_lo) | (k >= near_hi)
        for h in range(H):
            g = jnp.sum(p_ref[h], axis=0, keepdims=True)
            gw_ref[h:h + 1, :] = g
            far_ref[h:h + 1, :] = jnp.zeros((1, 128), F32) + jnp.sum(jnp.where(is_far, g, 0.0))

    return pl.pallas_call(
        body, name="bias_colsum", in_specs=[_full(p.shape)], out_specs=[_acc((H, L)), _acc((H, 128))],
        out_shape=[jax.ShapeDtypeStruct((H, L), F32), jax.ShapeDtypeStruct((H, 128), F32)], grid=(1,),
        compiler_params=_params("arbitrary"),
    )(p)


def _row_tile(rows, cols):
    best = None
    for tr in range(16, rows + 1, 16):
        if rows % tr == 0 and tr * cols * 4 <= EW_BLOCK_BYTES:
            best = tr
    return best if best is not None else rows


def place_shard(w, own, name):
    R, C = w.shape
    r = R // 2
    tr = _row_tile(r, C)
    nr = r // tr

    def body(own_ref, w_ref, o_ref):
        o_ref[...] = w_ref[...].astype(BF16)

    return pl.pallas_call(
        body, name=name,
        grid_spec=pltpu.PrefetchScalarGridSpec(
            num_scalar_prefetch=1, grid=(2, nr),
            in_specs=[pl.BlockSpec((tr, C), lambda h, i, own: (h * nr + i, 0))],
            out_specs=pl.BlockSpec((None, None, tr, C), lambda h, i, own: (own[0], h, i, 0))),
        out_shape=jax.ShapeDtypeStruct((4, 2, r, C), BF16), compiler_params=_params("parallel", "parallel"),
    )(own, w)


def chip_partial(grad, recv, core, name):
    _, _, r, C = grad.shape
    tr = _row_tile(r, C)

    def body(core_ref, g_ref, r_ref, o_ref):
        o_ref[...] = (g_ref[...].astype(F32) + r_ref[...].astype(F32)).astype(BF16)

    spec = pl.BlockSpec((None, tr, C), lambda j, i, core: (j, i, 0))
    return pl.pallas_call(
        body, name=name,
        grid_spec=pltpu.PrefetchScalarGridSpec(
            num_scalar_prefetch=1, grid=(4, r // tr),
            in_specs=[pl.BlockSpec((None, None, tr, C), lambda j, i, core: (j, core[0], i, 0)), spec], out_specs=spec),
        out_shape=jax.ShapeDtypeStruct((4, r, C), BF16), compiler_params=_params("parallel", "parallel"),
    )(core, grad, recv)


def shard_sum(part, recv, own, core, name):
    _, r, C = part.shape
    tr = _row_tile(r, C)

    def body(own_ref, core_ref, p_ref, r0, r1, r2, o_ref):
        o_ref[...] = p_ref[...].astype(F32) + r0[...].astype(F32) + r1[...].astype(F32) + r2[...].astype(F32)

    return pl.pallas_call(
        body, name=name,
        grid_spec=pltpu.PrefetchScalarGridSpec(
            num_scalar_prefetch=2, grid=(r // tr,),
            in_specs=[pl.BlockSpec((None, tr, C), lambda i, own, core: (own[0], i, 0))]
            + [pl.BlockSpec((None, tr, C), lambda i, own, core, k=k: (k, i, 0)) for k in range(3)],
            out_specs=pl.BlockSpec((None, tr, C), lambda i, own, core: (core[0], i, 0))),
        out_shape=jax.ShapeDtypeStruct((2, r, C), F32), compiler_params=_params("parallel"),
    )(own, core, part, recv, recv, recv)


def adamw(w, g, m, v, name):
    R, C = w.shape
    tr = _row_tile(R, C)

    def body(w_ref, g_ref, m_ref, v_ref, d_ref, nm_ref, nv_ref):
        gv = g_ref[...]
        nm = ADAM_B1 * m_ref[...] + (1.0 - ADAM_B1) * gv
        nv = ADAM_B2 * v_ref[...] + (1.0 - ADAM_B2) * (gv * gv)
        m_hat = nm / (1.0 - ADAM_B1 ** ADAM_STEP)
        v_hat = nv / (1.0 - ADAM_B2 ** ADAM_STEP)
        d_ref[...] = -ADAM_LR * (m_hat / (jnp.sqrt(v_hat) + ADAM_EPS) + ADAM_WD * w_ref[...])
        nm_ref[...] = nm
        nv_ref[...] = nv

    spec = pl.BlockSpec((tr, C), lambda i: (i, 0))
    return pl.pallas_call(
        body, grid=(R // tr,), name=name, in_specs=[spec] * 4, out_specs=[spec] * 3,
        out_shape=[jax.ShapeDtypeStruct((R, C), F32)] * 3, compiler_params=_params("parallel"),
    )(w, g, m, v)


HBM = pl.BlockSpec(memory_space=pl.ANY)


def _place():
    x, y, c = lax.axis_index("x"), lax.axis_index("y"), lax.axis_index("c")
    chips = [(1 - x, y), (x, 1 - y), (1 - x, 1 - y)]
    return x, y, c, chips


def _gather_phases(n):
    def copies(buf, sems, kind):
        send_sems, recv_sems = sems
        x, y, c, chips = _place()

        def remote(w, k, slot, to):
            return pltpu.make_async_remote_copy(src_ref=slot, dst_ref=slot, send_sem=send_sems.at[w, k],
                                                recv_sem=recv_sems.at[w, k], device_id=to, device_id_type=MESH)

        def one(w, k, px, py):
            if kind == "mine":
                return remote(w, k, buf[w].at[2 * x + y, c], (px, py, c))
            if kind == "theirs":
                return remote(w, k, buf[w].at[2 * px + py, c], (px, py, c))
            if kind == "onward":
                return remote(w, 3 + k, buf[w].at[2 * px + py, c], (x, y, 1 - c))
            return remote(w, 3 + k, buf[w].at[2 * px + py, 1 - c], (x, y, 1 - c))

        return [one(w, k, px, py) for w in range(n) for k, (px, py) in enumerate(chips)]

    def start(_, buf, sems):
        for cp in copies(buf, sems, "mine"):
            cp.start()

    def relay(_, buf, sems):
        for theirs, onward in zip(copies(buf, sems, "theirs"), copies(buf, sems, "onward")):
            theirs.wait_recv()
            onward.start()

    def finish(_, buf, sems):
        for cp in copies(buf, sems, "relayed"):
            cp.wait_recv()
        for cp in copies(buf, sems, "mine") + copies(buf, sems, "onward"):
            cp.wait_send()

    return start, relay, finish


def _gather_sems(n):
    return (pltpu.SemaphoreType.DMA((n, 6)), pltpu.SemaphoreType.DMA((n, 6)))


def gather_weights(bufs):
    n = len(bufs)
    phases = _gather_phases(n)

    def body(*refs):
        for phase in phases:
            phase(None, refs[n:2 * n], refs[2 * n:])

    return pl.pallas_call(
        body, name="gather_weights", in_specs=[HBM] * n, out_specs=[HBM] * n,
        out_shape=[jax.ShapeDtypeStruct(b.shape, b.dtype) for b in bufs],
        input_output_aliases={i: i for i in range(n)}, scratch_shapes=list(_gather_sems(n)),
    )(*bufs)


def gather_hook(bufs, n_steps):
    start, relay, finish = _gather_phases(len(bufs))
    return Hook(ins=tuple(bufs), out_shapes=tuple(jax.ShapeDtypeStruct(b.shape, b.dtype) for b in bufs), aliased=True,
                sems=_gather_sems(len(bufs)),
                steps=((0, "before", start), ((n_steps * 13) // 16, "before", relay), (n_steps - 1, "after", finish)))


def sibling_split(grads, name):
    n = len(grads)

    def body(*refs):
        src, dst = refs[:n], refs[n:2 * n]
        send_sems, recv_sems = refs[2 * n:]
        x, y, c, _ = _place()
        sends = [pltpu.make_async_remote_copy(src_ref=src[w].at[:, 1 - c], dst_ref=dst[w], send_sem=send_sems.at[w],
                                              recv_sem=recv_sems.at[w], device_id=(x, y, 1 - c), device_id_type=MESH)
                 for w in range(n)]
        for cp in sends:
            cp.start()
        for cp in sends:
            cp.wait()

    return pl.pallas_call(
        body, name=name, in_specs=[HBM] * n, out_specs=[HBM] * n,
        out_shape=[jax.ShapeDtypeStruct((4,) + g.shape[2:], g.dtype) for g in grads],
        scratch_shapes=[pltpu.SemaphoreType.DMA((n,))] * 2,
    )(*grads)


def split_hook(grads, n_steps):
    n = len(grads)

    def copies(src, dst, sems):
        send_sems, recv_sems = sems
        x, y, c, _ = _place()
        return [pltpu.make_async_remote_copy(src_ref=src[w].at[:, 1 - c], dst_ref=dst[w], send_sem=send_sems.at[w],
                                             recv_sem=recv_sems.at[w], device_id=(x, y, 1 - c), device_id_type=MESH)
                for w in range(n)]

    def start(src, dst, sems):
        for cp in copies(src, dst, sems):
            cp.start()

    def finish(src, dst, sems):
        for cp in copies(src, dst, sems):
            cp.wait()

    return Hook(ins=tuple(grads), out_shapes=tuple(jax.ShapeDtypeStruct((4,) + g.shape[2:], g.dtype) for g in grads),
                sems=(pltpu.SemaphoreType.DMA((n,)), pltpu.SemaphoreType.DMA((n,))),
                steps=((0, "before", start), (n_steps - 1, "after", finish)))


def exchange_hook(parts, n_steps):
    n = len(parts)

    def copies(src, dst, sems):
        send_sems, recv_sems = sems
        _, _, c, chips = _place()
        return [pltpu.make_async_remote_copy(
            src_ref=src[w].at[2 * px + py], dst_ref=dst[w].at[k], send_sem=send_sems.at[w, k],
            recv_sem=recv_sems.at[w, k], device_id=(px, py, c), device_id_type=MESH)
            for w in range(n) for k, (px, py) in enumerate(chips)]

    def start(src, dst, sems):
        for cp in copies(src, dst, sems):
            cp.start()

    def finish(src, dst, sems):
        for cp in copies(src, dst, sems):
            cp.wait()

    return Hook(ins=tuple(parts), out_shapes=tuple(jax.ShapeDtypeStruct((3,) + p.shape[1:], p.dtype) for p in parts),
                sems=(pltpu.SemaphoreType.DMA((n, 3)), pltpu.SemaphoreType.DMA((n, 3))),
                steps=((0, "before", start), (n_steps - 1, "after", finish)))


def sibling_join(sums):
    n = len(sums)

    def body(*refs):
        buf = refs[n:2 * n]
        send_sems, recv_sems = refs[2 * n:]
        x, y, c, _ = _place()
        sends = [pltpu.make_async_remote_copy(src_ref=buf[w].at[c], dst_ref=buf[w].at[c], send_sem=send_sems.at[w],
                                              recv_sem=recv_sems.at[w], device_id=(x, y, 1 - c), device_id_type=MESH)
                 for w in range(n)]
        for cp in sends:
            cp.start()
        for w, cp in enumerate(sends):
            cp.wait_send()
            pltpu.make_async_remote_copy(src_ref=buf[w].at[c], dst_ref=buf[w].at[1 - c], send_sem=send_sems.at[w],
                                         recv_sem=recv_sems.at[w], device_id=(x, y, 1 - c), device_id_type=MESH).wait_recv()

    return pl.pallas_call(
        body, name="sibling_join", in_specs=[HBM] * n, out_specs=[HBM] * n,
        out_shape=[jax.ShapeDtypeStruct(s.shape, s.dtype) for s in sums],
        input_output_aliases={i: i for i in range(n)},
        scratch_shapes=[pltpu.SemaphoreType.DMA((n,))] * 2,
    )(*sums)


def allreduce_small(v):
    R, C = v.shape

    def body(v_ref, out_ref, all_ref, send_sems, recv_sems):
        x, y, c, chips = _place()
        me, sibling = (x, y, c), (x, y, 1 - c)

        def slot(px, py, pc):
            return all_ref.at[4 * px + 2 * py + pc]

        def copy(k, block, to, src=None):
            return pltpu.make_async_remote_copy(src_ref=slot(*block) if src is None else src, dst_ref=slot(*block),
                                                send_sem=send_sems.at[k], recv_sem=recv_sems.at[k], device_id=to,
                                                device_id_type=MESH)

        first = [copy(0, me, sibling, src=v_ref)] + [copy(1 + j, me, (*chip, c), src=v_ref) for j, chip in enumerate(chips)]
        for cp in first:
            cp.start()
        slot(*me)[...] = v_ref[...]
        passed = [copy(4 + j, (*chip, c), sibling) for j, chip in enumerate(chips)]
        for j, chip in enumerate(chips):
            copy(1 + j, (*chip, c), me).wait_recv()
            passed[j].start()
        copy(0, sibling, me).wait_recv()
        for j, chip in enumerate(chips):
            copy(4 + j, (*chip, 1 - c), me).wait_recv()
        for cp in first + passed:
            cp.wait_send()
        acc = all_ref[0]
        for k in range(1, 8):
            acc = acc + all_ref[k]
        out_ref[...] = acc

    vmem = pl.BlockSpec(memory_space=pltpu.VMEM)
    return pl.pallas_call(
        body, name="allreduce_small", in_specs=[vmem], out_specs=vmem, out_shape=jax.ShapeDtypeStruct((R, C), F32),
        scratch_shapes=[pltpu.VMEM((8, R, C), F32), pltpu.SemaphoreType.DMA((7,)), pltpu.SemaphoreType.DMA((7,))],
        compiler_params=pltpu.CompilerParams(vmem_limit_bytes=VMEM_LIMIT_V7X),
    )(v)


def allgather_hook(v, n_steps):
    R, C = v.shape

    def copies(v_ref, out, sems, kind):
        send_sems, recv_sems, _ = sems
        x, y, c, chips = _place()
        me, sibling = (x, y, c), (x, y, 1 - c)

        def slot(px, py, pc):
            return out.at[4 * px + 2 * py + pc]

        def copy(k, block, to, src=None):
            return pltpu.make_async_remote_copy(src_ref=slot(*block) if src is None else src, dst_ref=slot(*block),
                                                send_sem=send_sems.at[k], recv_sem=recv_sems.at[k], device_id=to,
                                                device_id_type=MESH)

        if kind == "mine":
            return [copy(0, me, sibling, src=v_ref)] + [copy(1 + j, me, (*chip, c), src=v_ref) for j, chip in enumerate(chips)]
        if kind == "theirs":
            return [copy(1 + j, (*chip, c), me) for j, chip in enumerate(chips)]
        if kind == "onward":
            return [copy(4 + j, (*chip, c), sibling) for j, chip in enumerate(chips)]
        return [copy(0, sibling, me)] + [copy(4 + j, (*chip, 1 - c), me) for j, chip in enumerate(chips)]

    def own(v_ref, out, sems):
        x, y, c, _ = _place()
        return pltpu.make_async_copy(v_ref, out.at[4 * x + 2 * y + c], sems[2])

    def start(ins, outs, sems):
        own(ins[0], outs[0], sems).start()
        for cp in copies(ins[0], outs[0], sems, "mine"):
            cp.start()

    def relay(ins, outs, sems):
        for theirs, onward in zip(copies(ins[0], outs[0], sems, "theirs"), copies(ins[0], outs[0], sems, "onward")):
            theirs.wait_recv()
            onward.start()

    def finish(ins, outs, sems):
        for cp in copies(ins[0], outs[0], sems, "relayed"):
            cp.wait_recv()
        for cp in copies(ins[0], outs[0], sems, "mine") + copies(ins[0], outs[0], sems, "onward"):
            cp.wait_send()
        own(ins[0], outs[0], sems).wait()

    return Hook(ins=(v,), out_shapes=(jax.ShapeDtypeStruct((8, R, C), F32),),
                sems=(pltpu.SemaphoreType.DMA((7,)), pltpu.SemaphoreType.DMA((7,)), pltpu.SemaphoreType.DMA(())),
                steps=((0, "before", start), ((n_steps * 13) // 16, "before", relay), (n_steps - 1, "after", finish)))


def merge_hooks(a, b):
    assert not a.aliased and not b.aliased
    cut = (len(a.ins), len(a.out_shapes), len(a.sems))

    def part(fn, second):
        def run(ins, outs, sems):
            if second:
                fn(ins[cut[0]:], outs[cut[1]:], sems[cut[2]:])
            else:
                fn(ins[:cut[0]], outs[:cut[1]], sems[:cut[2]])
        return run

    steps = tuple((s, w, part(fn, False)) for s, w, fn in a.steps) + tuple((s, w, part(fn, True)) for s, w, fn in b.steps)
    return Hook(a.ins + b.ins, a.out_shapes + b.out_shapes, False, a.sems + b.sems, steps)


def sum_blocks(g):
    n, R, C = g.shape

    def body(g_ref, o_ref):
        acc = g_ref[0]
        for k in range(1, n):
            acc = acc + g_ref[k]
        o_ref[...] = acc

    return pl.pallas_call(body, name="sum_blocks", grid=(1,), in_specs=[_full(g.shape)], out_specs=_acc((R, C)),
                          out_shape=jax.ShapeDtypeStruct((R, C), F32), compiler_params=_params("arbitrary"))(g)


SMALL_PAD = 1024


def _pack_small(arrs):
    parts = []
    for a in arrs:
        f = a.reshape(-1).astype(F32)
        parts.append(jnp.pad(f, (0, (-f.shape[0]) % SMALL_PAD)))
    return jnp.concatenate(parts).reshape(-1, 128)


def _unpack_small(packed, shapes):
    flat = packed.reshape(-1)
    outs, off = [], 0
    for s in shapes:
        size = int(np.prod(s))
        outs.append(flat[off:off + size].reshape(s))
        off += size + (-size) % SMALL_PAD
    return outs


ATT_KEYS = ATT_WIN * ATT_TILE
ATT_NEAR_LO = (ATT_WIN - 1) * ATT_TILE - REL_CLIP + 1
ATT_PERIOD = ATT_KEYS + ATT_TILE + 1


def _att_bias_table(rel_bias):
    far = rel_bias[:, 2 * REL_CLIP:]
    near = rel_bias[:, 2 * REL_CLIP - 1:0:-1]
    w = jnp.concatenate([jnp.broadcast_to(far, (ATT_HEADS, ATT_NEAR_LO)), near,
                         jnp.broadcast_to(far, (ATT_HEADS, ATT_PERIOD - ATT_KEYS))], axis=1)
    rows = jnp.tile(w, (1, ATT_TILE))[:, :ATT_TILE * (ATT_PERIOD - 1)].reshape(ATT_HEADS, ATT_TILE, ATT_PERIOD - 1)
    i = np.arange(ATT_TILE)[:, None]
    m = np.arange(ATT_KEYS)[None, :]
    qc, kc = i // CHUNK, m // CHUNK
    band = (kc >= qc) & (kc <= qc + N_PREV_CHUNKS)
    return jnp.where(band[None], rows[:, :, :ATT_KEYS], NEG_INF)


def _rel_bias_grad(dbias):
    p = jnp.pad(dbias, ((0, 0), (0, 0), (0, ATT_PERIOD - 1 - ATT_KEYS))).reshape(ATT_HEADS, -1)
    p = jnp.pad(p, ((0, 0), (0, ATT_TILE))).reshape(ATT_HEADS, ATT_TILE, ATT_PERIOD)
    gw, far = bias_colsum(p)
    return jnp.concatenate([jnp.zeros((ATT_HEADS, 1), F32), gw[:, ATT_KEYS - 1:ATT_NEAR_LO - 1:-1], far[:, :1]], axis=1)


def kernel(x, mem, norm_mix_g, w_in, rel_bias, sg_ln_g, sg_ln_b, sg_w, sg_b, w_branch_att, w_branch_sg, w_out, norm_xattn_g, norm_mem_g, w_xq, w_xkv, w_xo, norm_ffn_g, w_ffn_in, w_ffn_out, norm_final_g, loss_target, m_norm_mix_g, m_w_in, m_rel_bias, m_sg_ln_g, m_sg_ln_b, m_sg_w, m_sg_b, m_w_branch_att, m_w_branch_sg, m_w_out, m_norm_xattn_g, m_norm_mem_g, m_w_xq, m_w_xkv, m_w_xo, m_norm_ffn_g, m_w_ffn_in, m_w_ffn_out, m_norm_final_g, v_norm_mix_g, v_w_in, v_rel_bias, v_sg_ln_g, v_sg_ln_b, v_sg_w, v_sg_b, v_w_branch_att, v_w_branch_sg, v_w_out, v_norm_xattn_g, v_norm_mem_g, v_w_xq, v_w_xkv, v_w_xo, v_norm_ffn_g, v_w_ffn_in, v_w_ffn_out, v_norm_final_g):
    S, D = x.shape[1], x.shape[2]
    x2d, mem2d, tgt = x[0], mem[0], loss_target[0]

    big_names = ["w_in", "w_branch_att", "w_branch_sg", "w_out", "w_xq", "w_xkv", "w_xo", "w_ffn_in", "w_ffn_out"]
    col_sharded = [True, True, True, False, False, True, False, True, False]
    big_w = [a[0] for a in (w_in, w_branch_att, w_branch_sg, w_out, w_xq, w_xkv, w_xo, w_ffn_in, w_ffn_out)]
    big_m = [a[0] for a in (m_w_in, m_w_branch_att, m_w_branch_sg, m_w_out, m_w_xq, m_w_xkv, m_w_xo, m_w_ffn_in, m_w_ffn_out)]
    big_v = [a[0] for a in (v_w_in, v_w_branch_att, v_w_branch_sg, v_w_out, v_w_xq, v_w_xkv, v_w_xo, v_w_ffn_in, v_w_ffn_out)]

    own = (2 * lax.axis_index("x") + lax.axis_index("y")).astype(jnp.int32).reshape(1)
    core = lax.axis_index("c").astype(jnp.int32).reshape(1)
    placed = [place_shard(w, own, "place_" + nm) for w, nm in zip(big_w, big_names)]

    def whole(g4, i):
        R, C = big_w[i].shape
        return g4.reshape(4, R, C) if col_sharded[i] else g4.reshape(4 * R, C)

    W_in = whole(gather_weights(placed[:1])[0], 0)

    g_mix, g_xat, g_mem, g_ffn = norm_mix_g, norm_xattn_g, norm_mem_g, norm_ffn_g
    g_fin = norm_final_g.reshape(1, D)
    bias = _att_bias_table(rel_bias[0])
    tt = np.arange(SG_BLOCK)
    sg_mask = (tt[None, :] // CHUNK) <= (tt[:, None] // CHUNK)
    wm_f = jnp.where(sg_mask[None], sg_w[0], 0.0)
    pairs = wm_f.reshape(SG_PAIRS, 2, SG_BLOCK, SG_BLOCK)
    wpair = jnp.transpose(pairs, (0, 2, 1, 3)).reshape(SG_PAIRS, SG_BLOCK, 2 * SG_BLOCK).astype(BF16)
    wtpair = jnp.transpose(pairs, (0, 3, 1, 2)).reshape(SG_PAIRS, SG_BLOCK, 2 * SG_BLOCK).astype(BF16)
    maskpair = jnp.asarray(np.tile(sg_mask, (SG_PAIRS, 1, 2)), F32)
    bfull = jnp.repeat(sg_b[0].T, SG_GROUP_DIM, axis=1)
    lng, lnb = sg_ln_g[0].reshape(1, 512), sg_ln_b[0].reshape(1, 512)
    gid = np.arange(512) // SG_GROUP_DIM
    amean = jnp.asarray((gid[:, None] == gid[None, :]) / SG_GROUP_DIM, BF16)

    h1, z, *early = norm_mm(x2d, g_mix, W_in, tm=ROW_TILE, scale=None, name="norm_mm_in",
                            hook=gather_hook(placed[1:7], S // ROW_TILE))
    y_att, lse_att, *late = attn_fwd(z, bias, gather_hook(placed[7:], S // ATT_TILE))
    W_ba, W_bs, W_out, W_xq, W_xkv, W_xo, W_fi, W_fo = [whole(g4, i + 1) for i, g4 in enumerate(early + late)]
    y_sg, = sgu_fwd(z, amean, lng, lnb, wpair, bfull, tm=512)
    a_br, b_br, merged, x1 = merge_fwd(y_att, y_sg, z, x2d, W_ba, W_bs, W_out, tm=ROW_TILE)
    h2, xq = norm_mm(x1, g_xat, W_xq.reshape(1, D, D), tm=512, scale=XATT_HEAD_DIM ** -0.5, name="norm_mm_xq")
    mn, kv = norm_mm(mem2d, g_mem, W_xkv, tm=mem2d.shape[0], scale=None, name="norm_mm_kv")
    o_x, lse_x, x2 = xattn_fwd(xq, kv, W_xo, x1, tm=512)
    h3, gu = norm_mm(x2, g_ffn, W_fi, tm=ROW_TILE, scale=None, name="norm_mm_ffn")
    act, loss_vec, dx3, dg_fin = ffn_out_loss(gu, W_fo, x2, g_fin, tgt, tm=ROW_TILE)

    dgu = ffn_act_bwd(dx3, gu, W_fo, tm=ROW_TILE, nchunk=2)
    gW_fo = tn_mm(act, dx3, shards=1, tn=D, tk=DW_TOKENS, name="dw_ffn_out")
    gW_fi = tn_mm(h3, dgu, shards=4, tn=2 * W_fi.shape[2], tk=DW_TOKENS, name="dw_ffn_in")
    dx2, dg_ffn = mm_nt_norm_bwd(dgu, W_fi, x2, g_ffn, dx3, tm=ROW_TILE, name="dx_ffn")
    dq_x, dkv = xattn_bwd(dx2, xq, o_x, lse_x, kv, W_xo, tm=512)
    gW_xo = tn_mm(o_x, dx2, shards=1, tn=D, tk=DW_TOKENS, name="dw_xo")
    gW_xq = tn_mm(h2, dq_x, shards=1, tn=D, tk=DW_TOKENS, name="dw_xq")
    dx1, dg_xat = mm_nt_norm_bwd(dq_x, W_xq.reshape(1, D, D), x1, g_xat, dx2, tm=512, name="dx_xq")
    gW_xkv = tn_mm(mn, dkv, shards=4, tn=2 * D, tk=mem2d.shape[0], name="dw_xkv")
    _, dg_mem = mm_nt_norm_bwd(dkv.astype(BF16), W_xkv, mem2d, g_mem, None, tm=mem2d.shape[0], name="dx_mem")
    def canon(g, i):
        R, C = big_w[i].shape
        return g.reshape(4, 2, R // 2, C)

    def partials(gs, recv, idx):
        return [chip_partial(g, r, core, "chip_partial_" + big_names[i]) for g, r, i in zip(gs, recv, idx)]

    early_g = [canon(g, i) for g, i in zip([gW_xq, gW_xkv, gW_xo, gW_fi, gW_fo], range(4, 9))]
    da, db, dgab, dy_att, dy_sg, *early_r = merge_bwd(dx1, W_out, a_br, b_br, z, W_ba, W_bs, tm=ROW_TILE,
                                                      hook=split_hook(early_g, S // ROW_TILE))
    gW_out = tn_mm(merged, dx1, shards=1, tn=D, tk=DW_TOKENS, name="dw_out")
    gW_ba = tn_mm(y_att, da, shards=4, tn=D, tk=DW_TOKENS, name="dw_branch_att")
    gW_bs = tn_mm(y_sg, db, shards=4, tn=D, tk=DW_TOKENS, name="dw_branch_sg")
    mid_g = [canon(g, i) for g, i in zip([gW_ba, gW_bs, gW_out], range(1, 4))]
    parts_b = (partials(mid_g, sibling_split(mid_g, "sibling_split_mid"), range(1, 4))
               + partials(early_g, early_r, range(4, 9)))
    nt_bwd = S // ATT_TILE + ATT_WIN - 1
    dqkv, dbias, *from_chips_b = attn_bwd(z, y_att, dy_att, lse_att, bias, exchange_hook(parts_b, nt_bwd))
    duv, dwpair, dsgb_full, dlng, dlnb = sgu_bwd(z, dy_sg, amean, lng, lnb, wpair, wtpair, bfull, maskpair, tm=512)
    dwm = jnp.transpose(dwpair.reshape(SG_PAIRS, SG_BLOCK, 2, SG_BLOCK), (0, 2, 1, 3))
    dsgb = dsgb_full[:, ::SG_GROUP_DIM].T
    dz = [dqkv, duv, dgab]
    gW_in = tn_mm_pieces(h1, dz, shards=4, per=2, tk=DW_TOKENS, name="dw_in")
    in_g = [canon(gW_in, 0)]
    parts_a = partials(in_g, sibling_split(in_g, "sibling_split_w_in"), [0])
    d_rel = _rel_bias_grad(dbias)
    small_g = [d_rel, dlng, dlnb, dwm, dsgb, dg_xat, dg_mem, dg_ffn, dg_fin, loss_vec[0, :1]]
    n_dx = S // ROW_TILE
    dx, dg_mix, from_chips_a, small_all = mm_nt_norm_bwd(
        dz, W_in, x2d, g_mix, dx1, tm=ROW_TILE, name="dx_in",
        hook=merge_hooks(exchange_hook(parts_a, n_dx), allgather_hook(_pack_small(small_g), n_dx)))
    from_chips_a = [from_chips_a]

    parts, from_chips = parts_a + parts_b, from_chips_a + from_chips_b
    sums = [shard_sum(p, r, own, core, "shard_sum_" + nm) for p, r, nm in zip(parts, from_chips, big_names)]
    joined = sibling_join(sums)
    big_out = []
    for j, w, m_, v_, nm in zip(joined, big_w, big_m, big_v, big_names):
        g = j.reshape(w.shape)
        big_out.append((g,) + tuple(adamw(w, g, m_, v_, "adamw_" + nm)))

    small_w = [norm_mix_g, rel_bias, sg_ln_g, sg_ln_b, sg_w, sg_b, norm_xattn_g, norm_mem_g, norm_ffn_g, norm_final_g]
    small_m = [m_norm_mix_g, m_rel_bias, m_sg_ln_g, m_sg_ln_b, m_sg_w, m_sg_b, m_norm_xattn_g, m_norm_mem_g, m_norm_ffn_g, m_norm_final_g]
    small_v = [v_norm_mix_g, v_rel_bias, v_sg_ln_g, v_sg_ln_b, v_sg_w, v_sg_b, v_norm_xattn_g, v_norm_mem_g, v_norm_ffn_g, v_norm_final_g]
    shapes = [w.shape for w in small_w]
    g_sum = jnp.concatenate([allreduce_small(_pack_small([dg_mix])), sum_blocks(small_all)], axis=0)
    zero = jnp.zeros((1,), F32)
    d_s, m_s, v_s = adamw(_pack_small(small_w + [zero]), g_sum, _pack_small(small_m + [zero]), _pack_small(small_v + [zero]),
                          "adamw_small")
    sg_, sd, sm, sv_ = (_unpack_small(p, shapes + [(1,)]) for p in (g_sum, d_s, m_s, v_s))
    loss = sg_[-1][0]

    order = ["norm_mix_g", "w_in", "rel_bias", "sg_ln_g", "sg_ln_b", "sg_w", "sg_b", "w_branch_att", "w_branch_sg", "w_out",
             "norm_xattn_g", "norm_mem_g", "w_xq", "w_xkv", "w_xo", "norm_ffn_g", "w_ffn_in", "w_ffn_out", "norm_final_g"]
    small_names = ["norm_mix_g", "rel_bias", "sg_ln_g", "sg_ln_b", "sg_w", "sg_b", "norm_xattn_g", "norm_mem_g", "norm_ffn_g",
                   "norm_final_g"]
    res = {}
    for i, nm in enumerate(small_names):
        res[nm] = (sg_[i], sd[i], sm[i], sv_[i])
    for nm, outs in zip(big_names, big_out):
        res[nm] = tuple(o[None] for o in outs)
    return (loss, dx[None], *[res[nm][0] for nm in order], *[res[nm][1] for nm in order],
            *[res[nm][2] for nm in order], *[res[nm][3] for nm in order])
```

```python
import functools
from typing import NamedTuple

import numpy as np
import jax
import jax.numpy as jnp
from jax import lax
from jax.experimental import pallas as pl
from jax.experimental.pallas import tpu as pltpu

F32, BF16 = jnp.float32, jnp.bfloat16
MESH = pl.DeviceIdType.MESH

EPS = 1e-6
NEG_INF = -1e30
CHUNK = 64
N_PREV_CHUNKS = 8
ATT_HEADS, ATT_HEAD_DIM = 8, 64
REL_CLIP = 128
SG_BLOCK, SG_GROUPS, SG_GROUP_DIM = 128, 8, 64
XATT_HEADS, XATT_HEAD_DIM = 4, 256
ATT_TILE = 128
ATT_WIN = 5
ADAM_LR, ADAM_B1, ADAM_B2, ADAM_EPS, ADAM_WD, ADAM_STEP = 0.001, 0.9, 0.999, 1e-08, 0.01, 10

VMEM_LIMIT_V7X = 56 * 1024 * 1024
EW_BLOCK_BYTES = 2 << 20
ROW_TILE = 512
DW_TOKENS = 1024
HBM = pl.BlockSpec(memory_space=pl.ANY)


def _params(*sem):
    return pltpu.CompilerParams(dimension_semantics=sem, vmem_limit_bytes=VMEM_LIMIT_V7X)


def _full(shape):
    n = len(shape)
    return pl.BlockSpec(shape, lambda *_: (0,) * n, pipeline_mode=pl.Buffered(1))


def _acc(shape):
    n = len(shape)
    return pl.BlockSpec(shape, lambda *_: (0,) * n)


class Hook(NamedTuple):
    ins: tuple = ()
    out_shapes: tuple = ()
    alias: tuple = ()
    sems: tuple = ()
    steps: tuple = ()


def _run_hook(hook, pos, h_in, h_out, h_sems, step=None):
    now = pl.program_id(0) if step is None else step
    for at, where, fn in hook.steps:
        if where == pos:
            pl.when(now == at)(functools.partial(fn, h_in, h_out, h_sems))


def _hooked_call(body, hook, *, n_in, n_out, in_specs, out_specs, out_shape, scratch_shapes=(), aliases=None, **kw):
    nh = len(hook.ins)
    aliases = {**(aliases or {}), **{n_in + i: n_out + o for i, o in hook.alias}}
    if "grid_spec" in kw:
        gs = kw.pop("grid_spec")
        spec = pltpu.PrefetchScalarGridSpec(
            num_scalar_prefetch=gs["prefetch"], grid=gs["grid"], in_specs=list(in_specs) + [HBM] * nh,
            out_specs=list(out_specs) + [HBM] * len(hook.out_shapes), scratch_shapes=list(scratch_shapes) + list(hook.sems))
        return pl.pallas_call(body, grid_spec=spec, out_shape=list(out_shape) + list(hook.out_shapes),
                              input_output_aliases=aliases, **kw)
    return pl.pallas_call(
        body, in_specs=list(in_specs) + [HBM] * nh, out_specs=list(out_specs) + [HBM] * len(hook.out_shapes),
        out_shape=list(out_shape) + list(hook.out_shapes), scratch_shapes=list(scratch_shapes) + list(hook.sems),
        input_output_aliases=aliases, **kw)


def _nt(a, b):
    return lax.dot_general(a, b, (((1,), (1,)), ((), ())), preferred_element_type=F32)


def _tn(a, b):
    return lax.dot_general(a, b, (((0,), (0,)), ((), ())), preferred_element_type=F32)


def _nn(a, b):
    return jnp.dot(a, b, preferred_element_type=F32)


def _sigmoid(x):
    return 1.0 / (1.0 + jnp.exp(-x))


_GELU_C = float(np.sqrt(2.0 / np.pi))


def _gelu(x):
    t = jnp.tanh(_GELU_C * (x + 0.044715 * (x * x * x)))
    return x * (0.5 * (1.0 + t))


def _gelu_grad(x):
    t = jnp.tanh(_GELU_C * (x + 0.044715 * (x * x * x)))
    return 0.5 * (1.0 + t) + 0.5 * x * (1.0 - t * t) * (_GELU_C * (1.0 + 3.0 * 0.044715 * x * x))


def _rms_stats(xf):
    rstd = lax.rsqrt(jnp.mean(xf * xf, axis=-1, keepdims=True) + EPS)
    return rstd, xf * rstd


def _rms_bwd(xhat, rstd, g, dh):
    dxh = dh * g
    dx = rstd * (dxh - xhat * jnp.mean(dxh * xhat, axis=-1, keepdims=True))
    return dx, dh * xhat


def norm_mm(x, g, w, *, tm, scale, name, hook=Hook()):
    S, D = x.shape
    J, _, C = w.shape
    nh, nho = len(hook.ins), len(hook.out_shapes)

    def body(x_ref, g_ref, w_ref, *rest):
        h_in, (h_ref, z_ref), h_out, h_sems = rest[:nh], rest[nh:nh + 2], rest[nh + 2:nh + 2 + nho], rest[nh + 2 + nho:]
        _run_hook(hook, "before", h_in, h_out, h_sems)
        _, xhat = _rms_stats(x_ref[...])
        h = (xhat * g_ref[...]).astype(BF16)
        h_ref[...] = h
        for j in range(J):
            acc = _nn(h, w_ref[j])
            if scale is not None:
                acc = acc * scale
            z_ref[:, j * C:(j + 1) * C] = acc.astype(BF16)
        _run_hook(hook, "after", h_in, h_out, h_sems)

    return _hooked_call(
        body, hook, n_in=3, n_out=2, grid=(S // tm,), name=name,
        in_specs=[pl.BlockSpec((tm, D), lambda i: (i, 0)), _full((1, D)), _full((J, D, C))],
        out_specs=[pl.BlockSpec((tm, D), lambda i: (i, 0)), pl.BlockSpec((tm, J * C), lambda i: (i, 0))],
        out_shape=[jax.ShapeDtypeStruct((S, D), BF16), jax.ShapeDtypeStruct((S, J * C), BF16)],
        compiler_params=_params("arbitrary"),
    )(x, g, w, *hook.ins)


def in_proj_main(x, g, w, idx, *, tm, hook=Hook()):
    S, D = x.shape
    _, _, C = w.shape
    ns = 3
    nh, nho = len(hook.ins), len(hook.out_shapes)

    def body(idx_ref, x_ref, g_ref, w_ref, *rest):
        h_in, (h_ref, z_ref), h_out, h_sems = rest[:nh], rest[nh:nh + 2], rest[nh + 2:nh + 2 + nho], rest[nh + 2 + nho:]
        s = pl.program_id(1)
        step = pl.program_id(0) * ns + s
        _run_hook(hook, "before", h_in, h_out, h_sems, step)

        @pl.when(s == 0)
        def _():
            _, xhat = _rms_stats(x_ref[...])
            h_ref[...] = (xhat * g_ref[...]).astype(BF16)

        z_ref[...] = _nn(h_ref[...], w_ref[...]).astype(BF16)
        _run_hook(hook, "after", h_in, h_out, h_sems, step)

    return _hooked_call(
        body, hook, n_in=4, n_out=2, name="in_proj_main", grid_spec=dict(prefetch=1, grid=(S // tm, ns)),
        in_specs=[pl.BlockSpec((tm, D), lambda i, s, idx: (i, 0)), pl.BlockSpec((1, D), lambda i, s, idx: (0, 0)),
                  pl.BlockSpec((None, D, C), lambda i, s, idx: (idx[s], 0, 0))],
        out_specs=[pl.BlockSpec((tm, D), lambda i, s, idx: (i, 0)), pl.BlockSpec((tm, C), lambda i, s, idx: (i, idx[s]))],
        out_shape=[jax.ShapeDtypeStruct((S, D), BF16), jax.ShapeDtypeStruct((S, 4 * C), BF16)],
        compiler_params=_params("arbitrary", "arbitrary"),
    )(idx, x, g, w, *hook.ins)


def in_proj_last(h, w2, z, idx, *, tm):
    S, D = h.shape
    _, r, C = w2.shape

    def body(idx_ref, h_ref, w_ref, z_in, z_ref):
        z_ref[...] = (_nn(h_ref[:, :r], w_ref[0]) + _nn(h_ref[:, r:], w_ref[1])).astype(BF16)

    return pl.pallas_call(
        body, name="in_proj_last",
        grid_spec=pltpu.PrefetchScalarGridSpec(
            num_scalar_prefetch=1, grid=(S // tm,),
            in_specs=[pl.BlockSpec((tm, D), lambda i, idx: (i, 0)), pl.BlockSpec((2, r, C), lambda i, idx: (0, 0, 0)), HBM],
            out_specs=pl.BlockSpec((tm, C), lambda i, idx: (i, idx[3]))),
        out_shape=jax.ShapeDtypeStruct(z.shape, z.dtype), input_output_aliases={3: 0},
        compiler_params=_params("parallel"),
    )(idx, h, w2, z)


def _att_window_specs(nt, col):
    return [pl.BlockSpec((ATT_TILE, 512), lambda t, j=j: (jnp.clip(t - (ATT_WIN - 1) + j, 0, nt - 1), col))
            for j in range(ATT_WIN)]


ATT_SCALE = ATT_HEAD_DIM ** -0.5


def _att_scores(q_scaled, k, bias, valid):
    return jnp.where(valid, _nt(q_scaled, k) + bias, NEG_INF)


def _att_valid(t):
    kpos = lax.broadcasted_iota(jnp.int32, (ATT_TILE, ATT_WIN * ATT_TILE), 1) + (t - (ATT_WIN - 1)) * ATT_TILE
    return kpos >= 0


def attn_fwd(z, bias, hook=Hook()):
    S = z.shape[0]
    nt = S // ATT_TILE
    n_in, nh, nho = 2 + 2 * ATT_WIN, len(hook.ins), len(hook.out_shapes)

    def body(q_ref, *refs):
        k_refs, v_refs = refs[:ATT_WIN], refs[ATT_WIN:2 * ATT_WIN]
        bias_ref = refs[2 * ATT_WIN]
        rest = refs[2 * ATT_WIN + 1:]
        h_in, (y_ref, lse_ref), h_out = rest[:nh], rest[nh:nh + 2], rest[nh + 2:nh + 2 + nho]
        s_scr, p_scr = rest[nh + 2 + nho:nh + 4 + nho]
        h_sems = rest[nh + 4 + nho:]
        _run_hook(hook, "before", h_in, h_out, h_sems)
        valid = _att_valid(pl.program_id(0))
        heads = [slice(h * ATT_HEAD_DIM, (h + 1) * ATT_HEAD_DIM) for h in range(ATT_HEADS)]
        for h, sl in enumerate(heads):
            k = jnp.concatenate([r[:, sl] for r in k_refs], axis=0)
            s_scr[h] = _att_scores(q_ref[:, sl] * ATT_SCALE, k, bias_ref[h], valid)
        stats = []
        for h in range(ATT_HEADS):
            s = s_scr[h]
            m = jnp.max(s, axis=-1, keepdims=True)
            p = jnp.exp(s - m)
            stats.append((m, jnp.sum(p, axis=-1, keepdims=True)))
            p_scr[h] = p.astype(BF16)
        for h, sl in enumerate(heads):
            m, l = stats[h]
            v = jnp.concatenate([r[:, sl] for r in v_refs], axis=0)
            y_ref[:, sl] = (_nn(p_scr[h], v) / l).astype(BF16)
            lse_ref[:, h:h + 1] = m + jnp.log(l)
        _run_hook(hook, "after", h_in, h_out, h_sems)

    tile = lambda col: pl.BlockSpec((ATT_TILE, 512), lambda t: (t, col))
    return _hooked_call(
        body, hook, n_in=n_in, n_out=2, grid=(nt,), name="attn_fwd",
        in_specs=[tile(0)] + _att_window_specs(nt, 1) + _att_window_specs(nt, 2) + [_full(bias.shape)],
        out_specs=[tile(0), pl.BlockSpec((ATT_TILE, ATT_HEADS), lambda t: (t, 0))],
        out_shape=[jax.ShapeDtypeStruct((S, 512), BF16), jax.ShapeDtypeStruct((S, ATT_HEADS), F32)],
        scratch_shapes=[pltpu.VMEM((ATT_HEADS, ATT_TILE, ATT_WIN * ATT_TILE), F32),
                        pltpu.VMEM((ATT_HEADS, ATT_TILE, ATT_WIN * ATT_TILE), BF16)],
        compiler_params=_params("arbitrary"),
    )(z, *([z] * (2 * ATT_WIN)), bias, *hook.ins)


SG_PAIRS = SG_GROUPS // 2


def _group_mean(x, a_ref):
    rows = x.shape[0]
    hi = x.astype(BF16)
    lo = (x - hi.astype(F32)).astype(BF16)
    both = _nn(jnp.concatenate([hi, lo], axis=0), a_ref[...])
    return both[:rows] + both[rows:]


def _pair_diag(x, ref):
    first = lax.broadcasted_iota(jnp.int32, x.shape, 1) < SG_GROUP_DIM
    ref[0:SG_BLOCK, :] = jnp.where(first, x, 0.0).astype(BF16)
    ref[SG_BLOCK:2 * SG_BLOCK, :] = jnp.where(first, 0.0, x).astype(BF16)


def _sgu_norm(zv_ref, a_ref, lng_ref, lnb_ref):
    v = _gelu(zv_ref[...].astype(F32))
    vc = v - _group_mean(v, a_ref)
    rstd = lax.rsqrt(_group_mean(vc * vc, a_ref) + EPS)
    xhat = vc * rstd
    return xhat, rstd, xhat * lng_ref[...] + lnb_ref[...]


def _sgu_rows(b):
    return pl.ds(pl.multiple_of(b * SG_BLOCK, SG_BLOCK), SG_BLOCK)


def sgu_fwd(z, amean, lng, lnb, wpair, bfull, *, tm, hook=Hook()):
    S = z.shape[0]
    nh, nho = len(hook.ins), len(hook.out_shapes)

    def body(zu_ref, zv_ref, a_ref, lng_ref, lnb_ref, wpair_ref, bfull_ref, *rest):
        h_in, y_ref, h_out = rest[:nh], rest[nh], rest[nh + 1:nh + 1 + nho]
        vn_scr, vbd = rest[nh + 1 + nho:nh + 3 + nho]
        h_sems = rest[nh + 3 + nho:]
        _run_hook(hook, "before", h_in, h_out, h_sems)
        vn_scr[...] = _sgu_norm(zv_ref, a_ref, lng_ref, lnb_ref)[2]

        def block(b, carry):
            rows = _sgu_rows(b)
            u = _gelu(zu_ref[rows, :].astype(F32))
            for p in range(SG_PAIRS):
                lanes = slice(p * 128, (p + 1) * 128)
                _pair_diag(vn_scr[rows, lanes], vbd.at[p])
                sv = _nn(wpair_ref[p], vbd[p]) + bfull_ref[:, lanes]
                y_ref[rows, lanes] = (u[:, lanes] * sv).astype(BF16)
            return carry

        lax.fori_loop(0, tm // SG_BLOCK, block, 0)
        _run_hook(hook, "after", h_in, h_out, h_sems)

    tile = lambda col: pl.BlockSpec((tm, 512), lambda i: (i, col))
    smalls = [amean, lng, lnb, wpair, bfull]
    return _hooked_call(
        body, hook, n_in=2 + len(smalls), n_out=1, grid=(S // tm,), name="sgu_fwd",
        in_specs=[tile(3), tile(4)] + [_full(a.shape) for a in smalls],
        out_specs=[tile(0)], out_shape=[jax.ShapeDtypeStruct((S, 512), BF16)],
        scratch_shapes=[pltpu.VMEM((tm, 512), F32), pltpu.VMEM((SG_PAIRS, 2 * SG_BLOCK, 128), BF16)],
        compiler_params=_params("arbitrary"),
    )(z, z, *smalls, *hook.ins)


def _gate_specs(tm):
    return [pl.BlockSpec((tm, 512), lambda i, c=c: (i, c)) for c in (5, 6, 7, 8)]


def merge_fwd(y_att, y_sg, z, x, w_ba, w_bs, w_out, *, tm):
    S, D = x.shape
    J, _, C = w_ba.shape

    def body(ya_ref, ys_ref, g0, g1, g2, g3, x_ref, wba_ref, wbs_ref, wo_ref, a_ref, b_ref, m_ref, x1_ref):
        ya, ys = ya_ref[...], ys_ref[...]
        a = jnp.concatenate([_nn(ya, wba_ref[j]) for j in range(J)], axis=1)
        b = jnp.concatenate([_nn(ys, wbs_ref[j]) for j in range(J)], axis=1)
        ga = jnp.concatenate([g0[...], g1[...]], axis=1).astype(F32)
        gb = jnp.concatenate([g2[...], g3[...]], axis=1).astype(F32)
        a_ref[...] = a.astype(BF16)
        b_ref[...] = b.astype(BF16)
        merged = (_sigmoid(ga) * a + _sigmoid(gb) * b).astype(BF16)
        m_ref[...] = merged
        x1_ref[...] = x_ref[...] + _nn(merged, wo_ref[...])

    row = lambda n: pl.BlockSpec((tm, n), lambda i: (i, 0))
    return pl.pallas_call(
        body, grid=(S // tm,), name="merge_fwd",
        in_specs=[row(512), row(512)] + _gate_specs(tm) + [row(D), _full(w_ba.shape), _full(w_bs.shape), _full(w_out.shape)],
        out_specs=[row(D)] * 4,
        out_shape=[jax.ShapeDtypeStruct((S, D), BF16)] * 3 + [jax.ShapeDtypeStruct((S, D), F32)],
        compiler_params=_params("parallel"),
    )(y_att, y_sg, z, z, z, z, x, w_ba, w_bs, w_out)


def xattn_fwd(xq, kv, w_xo, x1, *, tm):
    S, D = xq.shape
    dh = XATT_HEAD_DIM

    def body(q_ref, kv_ref, wo_ref, x1_ref, o_ref, lse_ref, x2_ref):
        outs = []
        for h in range(XATT_HEADS):
            s = _nt(q_ref[:, h * dh:(h + 1) * dh], kv_ref[:, h * dh:(h + 1) * dh])
            m = jnp.max(s, axis=-1, keepdims=True)
            p = jnp.exp(s - m)
            l = jnp.sum(p, axis=-1, keepdims=True)
            outs.append((_nn(p.astype(BF16), kv_ref[:, D + h * dh:D + (h + 1) * dh]) / l).astype(BF16))
            lse_ref[:, h:h + 1] = m + jnp.log(l)
        o = jnp.concatenate(outs, axis=1)
        o_ref[...] = o
        x2_ref[...] = x1_ref[...] + _nn(o, wo_ref[...])

    row = lambda n: pl.BlockSpec((tm, n), lambda i: (i, 0))
    return pl.pallas_call(
        body, grid=(S // tm,), name="xattn_fwd",
        in_specs=[row(D), _full(kv.shape), _full(w_xo.shape), row(D)],
        out_specs=[row(D), row(XATT_HEADS), row(D)],
        out_shape=[jax.ShapeDtypeStruct((S, D), BF16), jax.ShapeDtypeStruct((S, XATT_HEADS), F32),
                   jax.ShapeDtypeStruct((S, D), F32)],
        compiler_params=_params("parallel"),
    )(xq, kv, w_xo, x1)


def ffn_out_loss(gu, w, x2, g, target, *, tm):
    S, D = x2.shape
    F = w.shape[0]

    def body(gu_ref, w_ref, x2_ref, g_ref, t_ref, act_ref, loss_ref, dx_ref, dg_ref):
        @pl.when(pl.program_id(0) == 0)
        def _():
            loss_ref[...] = jnp.zeros_like(loss_ref)
            dg_ref[...] = jnp.zeros_like(dg_ref)

        gate = gu_ref[:, :F].astype(F32)
        up = gu_ref[:, F:].astype(F32)
        act = (gate * _sigmoid(gate) * up).astype(BF16)
        act_ref[...] = act
        gv = g_ref[...]
        rstd, xhat = _rms_stats(x2_ref[...] + _nn(act, w_ref[...]))
        err = xhat * gv - t_ref[...]
        loss_ref[...] += 0.5 * jnp.sum(jnp.mean(err * err, axis=-1, keepdims=True))
        dx, dgc = _rms_bwd(xhat, rstd, gv, err * (1.0 / D))
        dx_ref[...] = dx
        dg_ref[...] += jnp.sum(dgc, axis=0, keepdims=True)

    row = lambda n: pl.BlockSpec((tm, n), lambda i: (i, 0))
    return pl.pallas_call(
        body, grid=(S // tm,), name="ffn_out_loss",
        in_specs=[row(2 * F), _full(w.shape), row(D), _full((1, D)), row(D)],
        out_specs=[row(F), _acc((1, 128)), row(D), _acc((1, D))],
        out_shape=[jax.ShapeDtypeStruct((S, F), BF16), jax.ShapeDtypeStruct((1, 128), F32),
                   jax.ShapeDtypeStruct((S, D), F32), jax.ShapeDtypeStruct((1, D), F32)],
        compiler_params=_params("arbitrary"),
    )(gu, w, x2, g, target)


def ffn_act_bwd(dx3, gu, w, *, tm, nchunk):
    S, D = dx3.shape
    F = w.shape[0]
    cn = F // nchunk

    def body(dx_ref, gu_ref, w_ref, dgu_ref):
        dxb = dx_ref[...].astype(BF16)
        for j in range(nchunk):
            dact = _nt(dxb, w_ref[j * cn:(j + 1) * cn, :])
            gate = gu_ref[:, j * cn:(j + 1) * cn].astype(F32)
            up = gu_ref[:, F + j * cn:F + (j + 1) * cn].astype(F32)
            sg = _sigmoid(gate)
            dgu_ref[:, j * cn:(j + 1) * cn] = (dact * up * (sg * (1.0 + gate * (1.0 - sg)))).astype(BF16)
            dgu_ref[:, F + j * cn:F + (j + 1) * cn] = (dact * (gate * sg)).astype(BF16)

    row = lambda n: pl.BlockSpec((tm, n), lambda i: (i, 0))
    return pl.pallas_call(
        body, grid=(S // tm,), name="ffn_act_bwd",
        in_specs=[row(D), row(2 * F), _full(w.shape)], out_specs=row(2 * F),
        out_shape=jax.ShapeDtypeStruct((S, 2 * F), BF16), compiler_params=_params("parallel"),
    )(dx3, gu, w)


def _overlaps(widths, lo, hi):
    out, off = [], 0
    for p, wd in enumerate(widths):
        a, b = max(lo, off), min(hi, off + wd)
        if a < b:
            out.append((p, a - off, b - off, a - lo))
        off += wd
    return out


def tn_mm_pieces(a, pieces, *, shards, per, tk, name):
    K, M = a.shape
    widths = [p.shape[1] for p in pieces]
    C = sum(widths) // shards
    tn = per * C
    nk = K // tk
    n_tiles = shards // per

    def body(a_ref, *refs):
        p_refs, o_ref, acc_ref = refs[:len(pieces)], refs[len(pieces)], refs[len(pieces) + 1]
        n, k = pl.program_id(0), pl.program_id(1)

        @pl.when(k == 0)
        def _():
            acc_ref[...] = jnp.zeros_like(acc_ref)

        av = a_ref[...]
        for tile in range(n_tiles):
            @pl.when(n == tile)
            def _(tile=tile):
                for p, c0, c1, at in _overlaps(widths, tile * tn, (tile + 1) * tn):
                    acc_ref[:, at:at + c1 - c0] += _tn(av, p_refs[p][:, c0:c1])

        @pl.when(k == nk - 1)
        def _():
            for s in range(per):
                o_ref[s] = acc_ref[:, s * C:(s + 1) * C].astype(BF16)

    return pl.pallas_call(
        body, grid=(n_tiles, nk), name=name,
        in_specs=[pl.BlockSpec((tk, M), lambda n, k: (k, 0))] + [pl.BlockSpec((tk, wd), lambda n, k: (k, 0)) for wd in widths],
        out_specs=pl.BlockSpec((per, M, C), lambda n, k: (n, 0, 0)), out_shape=jax.ShapeDtypeStruct((shards, M, C), BF16),
        scratch_shapes=[pltpu.VMEM((M, tn), F32)], compiler_params=_params("parallel", "arbitrary"),
    )(a, *pieces)


def tn_mm(a, b, *, shards, tn, tk, name):
    K, M = a.shape
    N = b.shape[1]
    C = N // shards
    per = tn // C
    nk = K // tk

    def body(a_ref, b_ref, o_ref, acc_ref):
        k = pl.program_id(1)

        @pl.when(k == 0)
        def _():
            acc_ref[...] = jnp.zeros_like(acc_ref)

        acc_ref[...] += _tn(a_ref[...].astype(BF16), b_ref[...].astype(BF16))

        @pl.when(k == nk - 1)
        def _():
            if shards > 1:
                for s in range(per):
                    o_ref[s] = acc_ref[:, s * C:(s + 1) * C].astype(BF16)
            else:
                o_ref[...] = acc_ref[...].astype(BF16)

    if shards > 1:
        out_spec = pl.BlockSpec((per, M, C), lambda n, k: (n, 0, 0))
        out_shape = jax.ShapeDtypeStruct((shards, M, C), BF16)
    else:
        out_spec = pl.BlockSpec((M, tn), lambda n, k: (0, n))
        out_shape = jax.ShapeDtypeStruct((M, N), BF16)
    return pl.pallas_call(
        body, grid=(N // tn, nk), name=name,
        in_specs=[pl.BlockSpec((tk, M), lambda n, k: (k, 0)), pl.BlockSpec((tk, tn), lambda n, k: (k, n))],
        out_specs=out_spec, out_shape=out_shape, scratch_shapes=[pltpu.VMEM((M, tn), F32)],
        compiler_params=_params("parallel", "arbitrary"),
    )(a, b)


def mm_nt_norm_bwd(dy, w, x, g, dx_in, *, tm, name, hook=Hook()):
    S, D = x.shape
    J, _, C = w.shape
    has_in = dx_in is not None
    dys = list(dy) if isinstance(dy, (list, tuple)) else [dy]
    widths = [d.shape[1] for d in dys]
    nd = len(dys)
    n_in, nh, nho = nd + 3 + has_in, len(hook.ins), len(hook.out_shapes)

    def body(*refs):
        dy_refs = refs[:nd]
        w_ref, x_ref, g_ref = refs[nd:nd + 3]
        dxin_ref = refs[nd + 3] if has_in else None
        rest = refs[n_in:]
        h_in, (dx_ref, dg_ref), h_out, h_sems = rest[:nh], rest[nh:nh + 2], rest[nh + 2:nh + 2 + nho], rest[nh + 2 + nho:]
        _run_hook(hook, "before", h_in, h_out, h_sems)

        @pl.when(pl.program_id(0) == 0)
        def _():
            dg_ref[...] = jnp.zeros_like(dg_ref)

        dh = None
        for j in range(J):
            for p, c0, c1, at in _overlaps(widths, j * C, (j + 1) * C):
                part = _nt(dy_refs[p][:, c0:c1], w_ref[j, :, at:at + c1 - c0])
                dh = part if dh is None else dh + part
        rstd, xhat = _rms_stats(x_ref[...])
        dx, dgc = _rms_bwd(xhat, rstd, g_ref[...], dh)
        dx_ref[...] = dx + dxin_ref[...] if has_in else dx
        dg_ref[...] += jnp.sum(dgc, axis=0, keepdims=True)
        _run_hook(hook, "after", h_in, h_out, h_sems)

    row = lambda n: pl.BlockSpec((tm, n), lambda i: (i, 0))
    ins = dys + [w, x, g] + ([dx_in] if has_in else [])
    return _hooked_call(
        body, hook, n_in=n_in, n_out=2, grid=(S // tm,), name=name,
        in_specs=[row(wd) for wd in widths] + [_full(w.shape), row(D), _full((1, D))] + ([row(D)] if has_in else []),
        out_specs=[row(D), _acc((1, D))],
        out_shape=[jax.ShapeDtypeStruct((S, D), F32), jax.ShapeDtypeStruct((1, D), F32)],
        compiler_params=_params("arbitrary"),
    )(*ins, *hook.ins)


def xattn_bwd(dx2, xq, o, lse, kv, w_xo, *, tm):
    S, D = xq.shape
    M = kv.shape[0]
    dh = XATT_HEAD_DIM

    def body(dx_ref, q_ref, o_ref, lse_ref, kv_ref, wo_ref, dq_ref, dkv_ref):
        @pl.when(pl.program_id(0) == 0)
        def _():
            dkv_ref[...] = jnp.zeros_like(dkv_ref)

        do = _nt(dx_ref[...].astype(BF16), wo_ref[...])
        for h in range(XATT_HEADS):
            hs = slice(h * dh, (h + 1) * dh)
            vs = slice(D + h * dh, D + (h + 1) * dh)
            q, k, v = q_ref[:, hs], kv_ref[:, hs], kv_ref[:, vs]
            do_h = do[:, hs]
            do_b = do_h.astype(BF16)
            p = jnp.exp(_nt(q, k) - lse_ref[:, h:h + 1])
            delta = jnp.sum(do_h * o_ref[:, hs].astype(F32), axis=-1, keepdims=True)
            ds = (p * (_nt(do_b, v) - delta)).astype(BF16)
            dq_ref[:, hs] = (_nn(ds, k) * (dh ** -0.5)).astype(BF16)
            dkv_ref[:, hs] += _tn(ds, q)
            dkv_ref[:, vs] += _tn(p.astype(BF16), do_b)

    row = lambda n: pl.BlockSpec((tm, n), lambda i: (i, 0))
    return pl.pallas_call(
        body, grid=(S // tm,), name="xattn_bwd",
        in_specs=[row(D), row(D), row(D), row(XATT_HEADS), _full(kv.shape), _full(w_xo.shape)],
        out_specs=[row(D), _acc((M, 2 * D))],
        out_shape=[jax.ShapeDtypeStruct((S, D), BF16), jax.ShapeDtypeStruct((M, 2 * D), F32)],
        compiler_params=_params("arbitrary"),
    )(dx2, xq, o, lse, kv, w_xo)


def merge_bwd(dx1, w_out, a, b, z, w_ba, w_bs, *, tm, hook=Hook()):
    S, D = dx1.shape
    J, W, C = w_ba.shape
    nh, nho = len(hook.ins), len(hook.out_shapes)

    def body(dx_ref, wo_ref, a_ref, b_ref, g0, g1, g2, g3, wba_ref, wbs_ref, *rest):
        h_in, (da_ref, db_ref, dg_ref, dya_ref, dys_ref) = rest[:nh], rest[nh:nh + 5]
        h_out, h_sems = rest[nh + 5:nh + 5 + nho], rest[nh + 5 + nho:]
        _run_hook(hook, "before", h_in, h_out, h_sems)
        dm = _nt(dx_ref[...].astype(BF16), wo_ref[...])
        sa = _sigmoid(jnp.concatenate([g0[...], g1[...]], axis=1).astype(F32))
        sb = _sigmoid(jnp.concatenate([g2[...], g3[...]], axis=1).astype(F32))
        dg_ref[:, :D] = (dm * a_ref[...].astype(F32) * (sa * (1.0 - sa))).astype(BF16)
        dg_ref[:, D:] = (dm * b_ref[...].astype(F32) * (sb * (1.0 - sb))).astype(BF16)
        da = (dm * sa).astype(BF16)
        db = (dm * sb).astype(BF16)
        da_ref[...] = da
        db_ref[...] = db
        dya = _nt(da[:, 0:C], wba_ref[0])
        dys = _nt(db[:, 0:C], wbs_ref[0])
        for j in range(1, J):
            dya += _nt(da[:, j * C:(j + 1) * C], wba_ref[j])
            dys += _nt(db[:, j * C:(j + 1) * C], wbs_ref[j])
        dya_ref[...] = dya.astype(BF16)
        dys_ref[...] = dys.astype(BF16)
        _run_hook(hook, "after", h_in, h_out, h_sems)

    row = lambda n: pl.BlockSpec((tm, n), lambda i: (i, 0))
    return _hooked_call(
        body, hook, n_in=10, n_out=5, grid=(S // tm,), name="merge_bwd",
        in_specs=[row(D), _full(w_out.shape), row(D), row(D)] + _gate_specs(tm) + [_full(w_ba.shape), _full(w_bs.shape)],
        out_specs=[row(D), row(D), row(2 * D), row(W), row(W)],
        out_shape=[jax.ShapeDtypeStruct((S, D), BF16)] * 2 + [jax.ShapeDtypeStruct((S, 2 * D), BF16)]
        + [jax.ShapeDtypeStruct((S, W), BF16)] * 2,
        compiler_params=_params("arbitrary"),
    )(dx1, w_out, a, b, z, z, z, z, w_ba, w_bs, *hook.ins)


def attn_bwd(z, y, dy, lse, bias, hook=Hook()):
    S = z.shape[0]
    nt = S // ATT_TILE
    back = ATT_WIN - 1
    n_in, nh, nho = 5 + 2 * ATT_WIN, len(hook.ins), len(hook.out_shapes)

    def body(q_ref, *refs):
        k_refs, v_refs = refs[:ATT_WIN], refs[ATT_WIN:2 * ATT_WIN]
        y_ref, dy_ref, lse_ref, bias_ref = refs[2 * ATT_WIN:2 * ATT_WIN + 4]
        rest = refs[2 * ATT_WIN + 4:]
        h_in, (dqkv_ref, dbias_ref), h_out = rest[:nh], rest[nh:nh + 2], rest[nh + 2:nh + 2 + nho]
        dk_acc, dv_acc, dq_buf = rest[nh + 2 + nho:nh + 5 + nho]
        h_sems = rest[nh + 5 + nho:]
        t = pl.program_id(0)
        _run_hook(hook, "before", h_in, h_out, h_sems)

        @pl.when(t == 0)
        def _():
            dbias_ref[...] = jnp.zeros_like(dbias_ref)
            dk_acc[...] = jnp.zeros_like(dk_acc)
            dv_acc[...] = jnp.zeros_like(dv_acc)
            dq_buf[...] = jnp.zeros_like(dq_buf)

        @pl.when(t < nt)
        def _():
            valid = _att_valid(t)
            q_t = (q_ref[...] * ATT_SCALE).astype(F32).T.astype(BF16)
            do_t = dy_ref[...].astype(F32).T.astype(BF16)
            for h in range(ATT_HEADS):
                sl = slice(h * ATT_HEAD_DIM, (h + 1) * ATT_HEAD_DIM)
                k = jnp.concatenate([r[:, sl] for r in k_refs], axis=0)
                v = jnp.concatenate([r[:, sl] for r in v_refs], axis=0)
                q, do_b = q_ref[:, sl] * ATT_SCALE, dy_ref[:, sl]
                p = jnp.exp(_att_scores(q, k, bias_ref[h], valid) - lse_ref[:, h:h + 1])
                delta = jnp.sum(do_b.astype(F32) * y_ref[:, sl].astype(F32), axis=-1, keepdims=True)
                ds = p * (_nt(do_b, v) - delta)
                dbias_ref[h] += ds
                ds_b = ds.astype(BF16)
                dq_buf[t % ATT_WIN, :, sl] = (_nn(ds_b, k) * ATT_SCALE).astype(BF16)
                dk_w = _nn(q_t[sl, :], ds_b)
                dv_w = _nn(do_t[sl, :], p.astype(BF16))
                for i in range(ATT_WIN):
                    slot = (t + 1 + i) % ATT_WIN
                    cols = slice(i * ATT_TILE, (i + 1) * ATT_TILE)
                    if i == back:
                        dk_acc[slot, sl, :] = dk_w[:, cols]
                        dv_acc[slot, sl, :] = dv_w[:, cols]
                    else:
                        dk_acc[slot, sl, :] += dk_w[:, cols]
                        dv_acc[slot, sl, :] += dv_w[:, cols]

        done = (t + 1) % ATT_WIN
        dqkv_ref[:, 0:512] = dq_buf[done]
        dqkv_ref[:, 512:1024] = dk_acc[done].T.astype(BF16)
        dqkv_ref[:, 1024:1536] = dv_acc[done].T.astype(BF16)
        _run_hook(hook, "after", h_in, h_out, h_sems)

    last = nt - 1
    tile = lambda col, n=512: pl.BlockSpec((ATT_TILE, n), lambda t: (jnp.minimum(t, last), col))
    late = pl.BlockSpec((ATT_TILE, 1536), lambda t: (jnp.maximum(t - back, 0), 0))
    return _hooked_call(
        body, hook, n_in=n_in, n_out=2, grid=(nt + back,), name="attn_bwd",
        in_specs=[tile(0)] + _att_window_specs(nt, 1) + _att_window_specs(nt, 2)
        + [tile(0), tile(0), tile(0, ATT_HEADS), _full(bias.shape)],
        out_specs=[late, _acc(bias.shape)],
        out_shape=[jax.ShapeDtypeStruct((S, 1536), BF16), jax.ShapeDtypeStruct(bias.shape, F32)],
        scratch_shapes=[pltpu.VMEM((ATT_WIN, 512, ATT_TILE), F32)] * 2 + [pltpu.VMEM((ATT_WIN, ATT_TILE, 512), BF16)],
        compiler_params=_params("arbitrary"),
    )(z, *([z] * (2 * ATT_WIN)), y, dy, lse, bias, *hook.ins)


def sgu_bwd(z, dy, amean, lng, lnb, wpair, wtpair, bfull, maskpair, *, tm):
    S = z.shape[0]
    n = S // tm

    def body(zu_ref, zv_ref, dy_ref, a_ref, lng_ref, lnb_ref, wpair_ref, wtpair_ref, bfull_ref, mask_ref,
             duv_ref, dw_ref, dsgb_ref, dlng_ref, dlnb_ref, xhat_scr, rstd_scr, vn_scr, dxh_scr, vbd, dbd,
             w_acc, b_acc, g_acc, s_acc):
        i = pl.program_id(0)

        @pl.when(i == 0)
        def _():
            for r in (w_acc, b_acc, g_acc, s_acc):
                r[...] = jnp.zeros_like(r)

        xhat_scr[...], rstd_scr[...], vn_scr[...] = _sgu_norm(zv_ref, a_ref, lng_ref, lnb_ref)

        def block(b, carry):
            rows = _sgu_rows(b)
            zu = zu_ref[rows, :].astype(F32)
            u, du = _gelu(zu), _gelu_grad(zu)
            dyv = dy_ref[rows, :].astype(F32)
            for p in range(SG_PAIRS):
                lanes = slice(p * 128, (p + 1) * 128)
                _pair_diag(vn_scr[rows, lanes], vbd.at[p])
                sv = _nn(wpair_ref[p], vbd[p]) + bfull_ref[:, lanes]
                duv_ref[rows, lanes] = (dyv[:, lanes] * sv * du[:, lanes]).astype(BF16)
                dsv = dyv[:, lanes] * u[:, lanes]
                b_acc[:, lanes] += dsv
                w_acc[p] += _nt(dsv.astype(BF16), vbd[p])
                _pair_diag(dsv, dbd.at[p])
                dvn = _nn(wtpair_ref[p], dbd[p])
                g_acc[:, lanes] += dvn * xhat_scr[rows, lanes]
                s_acc[:, lanes] += dvn
                dxh_scr[rows, lanes] = dvn * lng_ref[:, lanes]
            return carry

        lax.fori_loop(0, tm // SG_BLOCK, block, 0)
        dxh, xhat = dxh_scr[...], xhat_scr[...]
        dv = rstd_scr[...] * (dxh - _group_mean(dxh, a_ref) - xhat * _group_mean(dxh * xhat, a_ref))
        duv_ref[:, 512:1024] = (dv * _gelu_grad(zv_ref[...].astype(F32))).astype(BF16)

        @pl.when(i == n - 1)
        def _():
            dw_ref[...] = w_acc[...] * mask_ref[...]
            dsgb_ref[...] = _group_mean(b_acc[...], a_ref) * float(SG_GROUP_DIM)
            dlng_ref[...] = jnp.sum(g_acc[...], axis=0, keepdims=True)
            dlnb_ref[...] = jnp.sum(s_acc[...], axis=0, keepdims=True)

    tile = lambda col: pl.BlockSpec((tm, 512), lambda i: (i, col))
    smalls = [amean, lng, lnb, wpair, wtpair, bfull, maskpair]
    pair_shape = (SG_PAIRS, SG_BLOCK, 2 * SG_BLOCK)
    return pl.pallas_call(
        body, grid=(n,), name="sgu_bwd",
        in_specs=[tile(3), tile(4), tile(0)] + [_full(a.shape) for a in smalls],
        out_specs=[pl.BlockSpec((tm, 1024), lambda i: (i, 0)), _acc(pair_shape), _acc((SG_BLOCK, 512)),
                   _acc((1, 512)), _acc((1, 512))],
        out_shape=[jax.ShapeDtypeStruct((S, 1024), BF16), jax.ShapeDtypeStruct(pair_shape, F32),
                   jax.ShapeDtypeStruct((SG_BLOCK, 512), F32), jax.ShapeDtypeStruct((1, 512), F32),
                   jax.ShapeDtypeStruct((1, 512), F32)],
        scratch_shapes=[pltpu.VMEM((tm, 512), F32)] * 4
        + [pltpu.VMEM((SG_PAIRS, 2 * SG_BLOCK, 128), BF16)] * 2
        + [pltpu.VMEM(pair_shape, F32)] + [pltpu.VMEM((SG_BLOCK, 512), F32)] * 3,
        compiler_params=_params("arbitrary"),
    )(z, z, dy, *smalls)


def bias_colsum(p):
    H, _, L = p.shape
    near_lo = (ATT_WIN - 1) * ATT_TILE - REL_CLIP + 1
    near_hi = ATT_WIN * ATT_TILE

    def body(p_ref, gw_ref, far_ref):
        k = lax.broadcasted_iota(jnp.int32, (1, L), 1)
        is_far = (k < near_lo) | (k >= near_hi)
        for h in range(H):
            g = jnp.sum(p_ref[h], axis=0, keepdims=True)
            gw_ref[h:h + 1, :] = g
            far_ref[h:h + 1, :] = jnp.zeros((1, 128), F32) + jnp.sum(jnp.where(is_far, g, 0.0))

    return pl.pallas_call(
        body, name="bias_colsum", in_specs=[_full(p.shape)], out_specs=[_acc((H, L)), _acc((H, 128))],
        out_shape=[jax.ShapeDtypeStruct((H, L), F32), jax.ShapeDtypeStruct((H, 128), F32)], grid=(1,),
        compiler_params=_params("arbitrary"),
    )(p)


def _row_tile(rows, cols):
    best = None
    for tr in range(16, rows + 1, 16):
        if rows % tr == 0 and tr * cols * 4 <= EW_BLOCK_BYTES:
            best = tr
    return best if best is not None else rows


def place_shard(w, own, name):
    R, C = w.shape
    r = R // 2
    tr = _row_tile(r, C)
    nr = r // tr

    def body(own_ref, w_ref, o_ref):
        o_ref[...] = w_ref[...].astype(BF16)

    return pl.pallas_call(
        body, name=name,
        grid_spec=pltpu.PrefetchScalarGridSpec(
            num_scalar_prefetch=1, grid=(2, nr),
            in_specs=[pl.BlockSpec((tr, C), lambda h, i, own: (h * nr + i, 0))],
            out_specs=pl.BlockSpec((None, None, tr, C), lambda h, i, own: (own[0], h, i, 0))),
        out_shape=jax.ShapeDtypeStruct((4, 2, r, C), BF16), compiler_params=_params("parallel", "parallel"),
    )(own, w)


def chip_partial(grad, recv, core, name):
    _, _, r, C = grad.shape
    tr = _row_tile(r, C)

    def body(core_ref, g_ref, r_ref, o_ref):
        o_ref[...] = (g_ref[...].astype(F32) + r_ref[...].astype(F32)).astype(BF16)

    spec = pl.BlockSpec((None, tr, C), lambda j, i, core: (j, i, 0))
    return pl.pallas_call(
        body, name=name,
        grid_spec=pltpu.PrefetchScalarGridSpec(
            num_scalar_prefetch=1, grid=(4, r // tr),
            in_specs=[pl.BlockSpec((None, None, tr, C), lambda j, i, core: (j, core[0], i, 0)), spec], out_specs=spec),
        out_shape=jax.ShapeDtypeStruct((4, r, C), BF16), compiler_params=_params("parallel", "parallel"),
    )(core, grad, recv)


def shard_sum(part, recv, own, core, name):
    _, r, C = part.shape
    tr = _row_tile(r, C)

    def body(own_ref, core_ref, p_ref, r0, r1, r2, o_ref):
        o_ref[...] = p_ref[...].astype(F32) + r0[...].astype(F32) + r1[...].astype(F32) + r2[...].astype(F32)

    return pl.pallas_call(
        body, name=name,
        grid_spec=pltpu.PrefetchScalarGridSpec(
            num_scalar_prefetch=2, grid=(r // tr,),
            in_specs=[pl.BlockSpec((None, tr, C), lambda i, own, core: (own[0], i, 0))]
            + [pl.BlockSpec((None, tr, C), lambda i, own, core, k=k: (k, i, 0)) for k in range(3)],
            out_specs=pl.BlockSpec((None, tr, C), lambda i, own, core: (core[0], i, 0))),
        out_shape=jax.ShapeDtypeStruct((2, r, C), F32), compiler_params=_params("parallel"),
    )(own, core, part, recv, recv, recv)


def adamw(w, g, m, v, name):
    R, C = w.shape
    tr = _row_tile(R, C)

    def body(w_ref, g_ref, m_ref, v_ref, d_ref, nm_ref, nv_ref):
        gv = g_ref[...]
        nm = ADAM_B1 * m_ref[...] + (1.0 - ADAM_B1) * gv
        nv = ADAM_B2 * v_ref[...] + (1.0 - ADAM_B2) * (gv * gv)
        m_hat = nm / (1.0 - ADAM_B1 ** ADAM_STEP)
        v_hat = nv / (1.0 - ADAM_B2 ** ADAM_STEP)
        d_ref[...] = -ADAM_LR * (m_hat / (jnp.sqrt(v_hat) + ADAM_EPS) + ADAM_WD * w_ref[...])
        nm_ref[...] = nm
        nv_ref[...] = nv

    spec = pl.BlockSpec((tr, C), lambda i: (i, 0))
    return pl.pallas_call(
        body, grid=(R // tr,), name=name, in_specs=[spec] * 4, out_specs=[spec] * 3,
        out_shape=[jax.ShapeDtypeStruct((R, C), F32)] * 3, compiler_params=_params("parallel"),
    )(w, g, m, v)


HBM = pl.BlockSpec(memory_space=pl.ANY)


def _place():
    x, y, c = lax.axis_index("x"), lax.axis_index("y"), lax.axis_index("c")
    chips = [(1 - x, y), (x, 1 - y), (1 - x, 1 - y)]
    return x, y, c, chips


def _gather_phases(n, rels=(0, 1, 2)):
    def copies(buf, sems, kind):
        send_sems, recv_sems = sems
        x, y, c, chips = _place()

        def remote(w, k, slot, to):
            return pltpu.make_async_remote_copy(src_ref=slot, dst_ref=slot, send_sem=send_sems.at[w, k],
                                                recv_sem=recv_sems.at[w, k], device_id=to, device_id_type=MESH)

        def one(w, k, px, py):
            if kind == "mine":
                return remote(w, k, buf[w].at[2 * x + y, c], (px, py, c))
            if kind == "theirs":
                return remote(w, k, buf[w].at[2 * px + py, c], (px, py, c))
            if kind == "onward":
                return remote(w, 3 + k, buf[w].at[2 * px + py, c], (x, y, 1 - c))
            return remote(w, 3 + k, buf[w].at[2 * px + py, 1 - c], (x, y, 1 - c))

        return [one(w, k, *chips[k]) for w in range(n) for k in rels]

    def start(_, buf, sems):
        for cp in copies(buf, sems, "mine"):
            cp.start()

    def relay(_, buf, sems):
        for theirs, onward in zip(copies(buf, sems, "theirs"), copies(buf, sems, "onward")):
            theirs.wait_recv()
            onward.start()

    def finish(_, buf, sems):
        for cp in copies(buf, sems, "relayed"):
            cp.wait_recv()
        for cp in copies(buf, sems, "mine") + copies(buf, sems, "onward"):
            cp.wait_send()

    return start, relay, finish


def _gather_sems(n):
    return (pltpu.SemaphoreType.DMA((n, 6)), pltpu.SemaphoreType.DMA((n, 6)))


def gather_weights(bufs, rels=(0, 1, 2)):
    n = len(bufs)
    phases = _gather_phases(n, rels)

    def body(*refs):
        for phase in phases:
            phase(None, refs[n:2 * n], refs[2 * n:])

    return pl.pallas_call(
        body, name="gather_weights", in_specs=[HBM] * n, out_specs=[HBM] * n,
        out_shape=[jax.ShapeDtypeStruct(b.shape, b.dtype) for b in bufs],
        input_output_aliases={i: i for i in range(n)}, scratch_shapes=list(_gather_sems(n)),
    )(*bufs)


def gather_hook(bufs, n_steps):
    start, relay, finish = _gather_phases(len(bufs))
    return Hook(ins=tuple(bufs), out_shapes=tuple(jax.ShapeDtypeStruct(b.shape, b.dtype) for b in bufs),
                alias=tuple((i, i) for i in range(len(bufs))),
                sems=_gather_sems(len(bufs)),
                steps=((0, "before", start), ((n_steps * 13) // 16, "before", relay), (n_steps - 1, "after", finish)))


def diagonal_hook(buf, n_steps):
    _, _, r, C = buf.shape

    def copies(src, dst, sems):
        send_sems, recv_sems = sems
        x, y, c, chips = _place()
        far, sibling = (*chips[2], c), (x, y, 1 - c)

        def remote(k, s, d, to):
            return pltpu.make_async_remote_copy(src_ref=s, dst_ref=d, send_sem=send_sems.at[k], recv_sem=recv_sems.at[k],
                                                device_id=to, device_id_type=MESH)

        return {"mine": remote(0, src.at[2 * x + y, c], dst.at[c], far),
                "onward": remote(1, dst.at[c], dst.at[c], sibling),
                "relayed": remote(1, dst.at[1 - c], dst.at[1 - c], sibling)}

    def start(ins, outs, sems):
        copies(ins[0], outs[0], sems)["mine"].start()

    def relay(ins, outs, sems):
        cps = copies(ins[0], outs[0], sems)
        cps["mine"].wait_recv()
        cps["onward"].start()

    def finish(ins, outs, sems):
        cps = copies(ins[0], outs[0], sems)
        cps["relayed"].wait_recv()
        cps["mine"].wait_send()
        cps["onward"].wait_send()

    return Hook(ins=(buf,), out_shapes=(jax.ShapeDtypeStruct((2, r, C), buf.dtype),),
                sems=(pltpu.SemaphoreType.DMA((2,)), pltpu.SemaphoreType.DMA((2,))),
                steps=((0, "before", start), (n_steps // 2, "before", relay), (n_steps - 1, "after", finish)))


def sibling_split(grads, name):
    n = len(grads)

    def body(*refs):
        src, dst = refs[:n], refs[n:2 * n]
        send_sems, recv_sems = refs[2 * n:]
        x, y, c, _ = _place()
        sends = [pltpu.make_async_remote_copy(src_ref=src[w].at[:, 1 - c], dst_ref=dst[w], send_sem=send_sems.at[w],
                                              recv_sem=recv_sems.at[w], device_id=(x, y, 1 - c), device_id_type=MESH)
                 for w in range(n)]
        for cp in sends:
            cp.start()
        for cp in sends:
            cp.wait()

    return pl.pallas_call(
        body, name=name, in_specs=[HBM] * n, out_specs=[HBM] * n,
        out_shape=[jax.ShapeDtypeStruct((4,) + g.shape[2:], g.dtype) for g in grads],
        scratch_shapes=[pltpu.SemaphoreType.DMA((n,))] * 2,
    )(*grads)


def split_hook(grads, n_steps):
    n = len(grads)

    def copies(src, dst, sems):
        send_sems, recv_sems = sems
        x, y, c, _ = _place()
        return [pltpu.make_async_remote_copy(src_ref=src[w].at[:, 1 - c], dst_ref=dst[w], send_sem=send_sems.at[w],
                                             recv_sem=recv_sems.at[w], device_id=(x, y, 1 - c), device_id_type=MESH)
                for w in range(n)]

    def start(src, dst, sems):
        for cp in copies(src, dst, sems):
            cp.start()

    def finish(src, dst, sems):
        for cp in copies(src, dst, sems):
            cp.wait()

    return Hook(ins=tuple(grads), out_shapes=tuple(jax.ShapeDtypeStruct((4,) + g.shape[2:], g.dtype) for g in grads),
                sems=(pltpu.SemaphoreType.DMA((n,)), pltpu.SemaphoreType.DMA((n,))),
                steps=((0, "before", start), (n_steps - 1, "after", finish)))


def exchange_hook(parts, n_steps):
    n = len(parts)

    def copies(src, dst, sems):
        send_sems, recv_sems = sems
        _, _, c, chips = _place()
        return [pltpu.make_async_remote_copy(
            src_ref=src[w].at[2 * px + py], dst_ref=dst[w].at[k], send_sem=send_sems.at[w, k],
            recv_sem=recv_sems.at[w, k], device_id=(px, py, c), device_id_type=MESH)
            for w in range(n) for k, (px, py) in enumerate(chips)]

    def start(src, dst, sems):
        for cp in copies(src, dst, sems):
            cp.start()

    def finish(src, dst, sems):
        for cp in copies(src, dst, sems):
            cp.wait()

    return Hook(ins=tuple(parts), out_shapes=tuple(jax.ShapeDtypeStruct((3,) + p.shape[1:], p.dtype) for p in parts),
                sems=(pltpu.SemaphoreType.DMA((n, 3)), pltpu.SemaphoreType.DMA((n, 3))),
                steps=((0, "before", start), (n_steps - 1, "after", finish)))


def sibling_join(sums):
    n = len(sums)

    def body(*refs):
        buf = refs[n:2 * n]
        send_sems, recv_sems = refs[2 * n:]
        x, y, c, _ = _place()
        sends = [pltpu.make_async_remote_copy(src_ref=buf[w].at[c], dst_ref=buf[w].at[c], send_sem=send_sems.at[w],
                                              recv_sem=recv_sems.at[w], device_id=(x, y, 1 - c), device_id_type=MESH)
                 for w in range(n)]
        for cp in sends:
            cp.start()
        for w, cp in enumerate(sends):
            cp.wait_send()
            pltpu.make_async_remote_copy(src_ref=buf[w].at[c], dst_ref=buf[w].at[1 - c], send_sem=send_sems.at[w],
                                         recv_sem=recv_sems.at[w], device_id=(x, y, 1 - c), device_id_type=MESH).wait_recv()

    return pl.pallas_call(
        body, name="sibling_join", in_specs=[HBM] * n, out_specs=[HBM] * n,
        out_shape=[jax.ShapeDtypeStruct(s.shape, s.dtype) for s in sums],
        input_output_aliases={i: i for i in range(n)},
        scratch_shapes=[pltpu.SemaphoreType.DMA((n,))] * 2,
    )(*sums)


def allreduce_small(v):
    R, C = v.shape

    def body(v_ref, out_ref, all_ref, send_sems, recv_sems):
        x, y, c, chips = _place()
        me, sibling = (x, y, c), (x, y, 1 - c)

        def slot(px, py, pc):
            return all_ref.at[4 * px + 2 * py + pc]

        def copy(k, block, to, src=None):
            return pltpu.make_async_remote_copy(src_ref=slot(*block) if src is None else src, dst_ref=slot(*block),
                                                send_sem=send_sems.at[k], recv_sem=recv_sems.at[k], device_id=to,
                                                device_id_type=MESH)

        first = [copy(0, me, sibling, src=v_ref)] + [copy(1 + j, me, (*chip, c), src=v_ref) for j, chip in enumerate(chips)]
        for cp in first:
            cp.start()
        slot(*me)[...] = v_ref[...]
        passed = [copy(4 + j, (*chip, c), sibling) for j, chip in enumerate(chips)]
        for j, chip in enumerate(chips):
            copy(1 + j, (*chip, c), me).wait_recv()
            passed[j].start()
        copy(0, sibling, me).wait_recv()
        for j, chip in enumerate(chips):
            copy(4 + j, (*chip, 1 - c), me).wait_recv()
        for cp in first + passed:
            cp.wait_send()
        acc = all_ref[0]
        for k in range(1, 8):
            acc = acc + all_ref[k]
        out_ref[...] = acc

    vmem = pl.BlockSpec(memory_space=pltpu.VMEM)
    return pl.pallas_call(
        body, name="allreduce_small", in_specs=[vmem], out_specs=vmem, out_shape=jax.ShapeDtypeStruct((R, C), F32),
        scratch_shapes=[pltpu.VMEM((8, R, C), F32), pltpu.SemaphoreType.DMA((7,)), pltpu.SemaphoreType.DMA((7,))],
        compiler_params=pltpu.CompilerParams(vmem_limit_bytes=VMEM_LIMIT_V7X),
    )(v)


def allgather_hook(v, n_steps):
    R, C = v.shape

    def copies(v_ref, out, sems, kind):
        send_sems, recv_sems, _ = sems
        x, y, c, chips = _place()
        me, sibling = (x, y, c), (x, y, 1 - c)

        def slot(px, py, pc):
            return out.at[4 * px + 2 * py + pc]

        def copy(k, block, to, src=None):
            return pltpu.make_async_remote_copy(src_ref=slot(*block) if src is None else src, dst_ref=slot(*block),
                                                send_sem=send_sems.at[k], recv_sem=recv_sems.at[k], device_id=to,
                                                device_id_type=MESH)

        if kind == "mine":
            return [copy(0, me, sibling, src=v_ref)] + [copy(1 + j, me, (*chip, c), src=v_ref) for j, chip in enumerate(chips)]
        if kind == "theirs":
            return [copy(1 + j, (*chip, c), me) for j, chip in enumerate(chips)]
        if kind == "onward":
            return [copy(4 + j, (*chip, c), sibling) for j, chip in enumerate(chips)]
        return [copy(0, sibling, me)] + [copy(4 + j, (*chip, 1 - c), me) for j, chip in enumerate(chips)]

    def own(v_ref, out, sems):
        x, y, c, _ = _place()
        return pltpu.make_async_copy(v_ref, out.at[4 * x + 2 * y + c], sems[2])

    def start(ins, outs, sems):
        own(ins[0], outs[0], sems).start()
        for cp in copies(ins[0], outs[0], sems, "mine"):
            cp.start()

    def relay(ins, outs, sems):
        for theirs, onward in zip(copies(ins[0], outs[0], sems, "theirs"), copies(ins[0], outs[0], sems, "onward")):
            theirs.wait_recv()
            onward.start()

    def finish(ins, outs, sems):
        for cp in copies(ins[0], outs[0], sems, "relayed"):
            cp.wait_recv()
        for cp in copies(ins[0], outs[0], sems, "mine") + copies(ins[0], outs[0], sems, "onward"):
            cp.wait_send()
        own(ins[0], outs[0], sems).wait()

    return Hook(ins=(v,), out_shapes=(jax.ShapeDtypeStruct((8, R, C), F32),),
                sems=(pltpu.SemaphoreType.DMA((7,)), pltpu.SemaphoreType.DMA((7,)), pltpu.SemaphoreType.DMA(())),
                steps=((0, "before", start), ((n_steps * 13) // 16, "before", relay), (n_steps - 1, "after", finish)))


def merge_hooks(a, b):
    cut = (len(a.ins), len(a.out_shapes), len(a.sems))

    def part(fn, second):
        def run(ins, outs, sems):
            if second:
                fn(ins[cut[0]:], outs[cut[1]:], sems[cut[2]:])
            else:
                fn(ins[:cut[0]], outs[:cut[1]], sems[:cut[2]])
        return run

    steps = tuple((s, w, part(fn, False)) for s, w, fn in a.steps) + tuple((s, w, part(fn, True)) for s, w, fn in b.steps)
    alias = a.alias + tuple((i + cut[0], o + cut[1]) for i, o in b.alias)
    return Hook(a.ins + b.ins, a.out_shapes + b.out_shapes, alias, a.sems + b.sems, steps)


def sum_blocks(g):
    n, R, C = g.shape

    def body(g_ref, o_ref):
        acc = g_ref[0]
        for k in range(1, n):
            acc = acc + g_ref[k]
        o_ref[...] = acc

    return pl.pallas_call(body, name="sum_blocks", grid=(1,), in_specs=[_full(g.shape)], out_specs=_acc((R, C)),
                          out_shape=jax.ShapeDtypeStruct((R, C), F32), compiler_params=_params("arbitrary"))(g)


SMALL_PAD = 1024


def _pack_small(arrs):
    parts = []
    for a in arrs:
        f = a.reshape(-1).astype(F32)
        parts.append(jnp.pad(f, (0, (-f.shape[0]) % SMALL_PAD)))
    return jnp.concatenate(parts).reshape(-1, 128)


def _unpack_small(packed, shapes):
    flat = packed.reshape(-1)
    outs, off = [], 0
    for s in shapes:
        size = int(np.prod(s))
        outs.append(flat[off:off + size].reshape(s))
        off += size + (-size) % SMALL_PAD
    return outs


ATT_KEYS = ATT_WIN * ATT_TILE
ATT_NEAR_LO = (ATT_WIN - 1) * ATT_TILE - REL_CLIP + 1
ATT_PERIOD = ATT_KEYS + ATT_TILE + 1


def _att_bias_table(rel_bias):
    far = rel_bias[:, 2 * REL_CLIP:]
    near = rel_bias[:, 2 * REL_CLIP - 1:0:-1]
    w = jnp.concatenate([jnp.broadcast_to(far, (ATT_HEADS, ATT_NEAR_LO)), near,
                         jnp.broadcast_to(far, (ATT_HEADS, ATT_PERIOD - ATT_KEYS))], axis=1)
    rows = jnp.tile(w, (1, ATT_TILE))[:, :ATT_TILE * (ATT_PERIOD - 1)].reshape(ATT_HEADS, ATT_TILE, ATT_PERIOD - 1)
    i = np.arange(ATT_TILE)[:, None]
    m = np.arange(ATT_KEYS)[None, :]
    qc, kc = i // CHUNK, m // CHUNK
    band = (kc >= qc) & (kc <= qc + N_PREV_CHUNKS)
    return jnp.where(band[None], rows[:, :, :ATT_KEYS], NEG_INF)


def _rel_bias_grad(dbias):
    p = jnp.pad(dbias, ((0, 0), (0, 0), (0, ATT_PERIOD - 1 - ATT_KEYS))).reshape(ATT_HEADS, -1)
    p = jnp.pad(p, ((0, 0), (0, ATT_TILE))).reshape(ATT_HEADS, ATT_TILE, ATT_PERIOD)
    gw, far = bias_colsum(p)
    return jnp.concatenate([jnp.zeros((ATT_HEADS, 1), F32), gw[:, ATT_KEYS - 1:ATT_NEAR_LO - 1:-1], far[:, :1]], axis=1)


def kernel(x, mem, norm_mix_g, w_in, rel_bias, sg_ln_g, sg_ln_b, sg_w, sg_b, w_branch_att, w_branch_sg, w_out, norm_xattn_g, norm_mem_g, w_xq, w_xkv, w_xo, norm_ffn_g, w_ffn_in, w_ffn_out, norm_final_g, loss_target, m_norm_mix_g, m_w_in, m_rel_bias, m_sg_ln_g, m_sg_ln_b, m_sg_w, m_sg_b, m_w_branch_att, m_w_branch_sg, m_w_out, m_norm_xattn_g, m_norm_mem_g, m_w_xq, m_w_xkv, m_w_xo, m_norm_ffn_g, m_w_ffn_in, m_w_ffn_out, m_norm_final_g, v_norm_mix_g, v_w_in, v_rel_bias, v_sg_ln_g, v_sg_ln_b, v_sg_w, v_sg_b, v_w_branch_att, v_w_branch_sg, v_w_out, v_norm_xattn_g, v_norm_mem_g, v_w_xq, v_w_xkv, v_w_xo, v_norm_ffn_g, v_w_ffn_in, v_w_ffn_out, v_norm_final_g):
    S, D = x.shape[1], x.shape[2]
    x2d, mem2d, tgt = x[0], mem[0], loss_target[0]

    big_names = ["w_in", "w_branch_att", "w_branch_sg", "w_out", "w_xq", "w_xkv", "w_xo", "w_ffn_in", "w_ffn_out"]
    col_sharded = [True, True, True, False, False, True, False, True, False]
    big_w = [a[0] for a in (w_in, w_branch_att, w_branch_sg, w_out, w_xq, w_xkv, w_xo, w_ffn_in, w_ffn_out)]
    big_m = [a[0] for a in (m_w_in, m_w_branch_att, m_w_branch_sg, m_w_out, m_w_xq, m_w_xkv, m_w_xo, m_w_ffn_in, m_w_ffn_out)]
    big_v = [a[0] for a in (v_w_in, v_w_branch_att, v_w_branch_sg, v_w_out, v_w_xq, v_w_xkv, v_w_xo, v_w_ffn_in, v_w_ffn_out)]

    own = (2 * lax.axis_index("x") + lax.axis_index("y")).astype(jnp.int32).reshape(1)
    core = lax.axis_index("c").astype(jnp.int32).reshape(1)
    placed = [place_shard(w, own, "place_" + nm) for w, nm in zip(big_w, big_names)]

    def whole(g4, i):
        R, C = big_w[i].shape
        return g4.reshape(4, R, C) if col_sharded[i] else g4.reshape(4 * R, C)

    shard_order = jnp.concatenate([own, own ^ 2, own ^ 1, own ^ 3])
    w_in_near = gather_weights(placed[:1], rels=(0, 1))[0]

    g_mix, g_xat, g_mem, g_ffn = norm_mix_g, norm_xattn_g, norm_mem_g, norm_ffn_g
    g_fin = norm_final_g.reshape(1, D)
    bias = _att_bias_table(rel_bias[0])
    tt = np.arange(SG_BLOCK)
    sg_mask = (tt[None, :] // CHUNK) <= (tt[:, None] // CHUNK)
    wm_f = jnp.where(sg_mask[None], sg_w[0], 0.0)
    pairs = wm_f.reshape(SG_PAIRS, 2, SG_BLOCK, SG_BLOCK)
    wpair = jnp.transpose(pairs, (0, 2, 1, 3)).reshape(SG_PAIRS, SG_BLOCK, 2 * SG_BLOCK).astype(BF16)
    wtpair = jnp.transpose(pairs, (0, 3, 1, 2)).reshape(SG_PAIRS, SG_BLOCK, 2 * SG_BLOCK).astype(BF16)
    maskpair = jnp.asarray(np.tile(sg_mask, (SG_PAIRS, 1, 2)), F32)
    bfull = jnp.repeat(sg_b[0].T, SG_GROUP_DIM, axis=1)
    lng, lnb = sg_ln_g[0].reshape(1, 512), sg_ln_b[0].reshape(1, 512)
    gid = np.arange(512) // SG_GROUP_DIM
    amean = jnp.asarray((gid[:, None] == gid[None, :]) / SG_GROUP_DIM, BF16)

    n_main = 3 * (S // ROW_TILE)
    h1, z_most, w_in_far, *g1 = in_proj_main(
        x2d, g_mix, whole(w_in_near, 0), shard_order, tm=ROW_TILE,
        hook=merge_hooks(diagonal_hook(w_in_near, n_main), gather_hook(placed[1:4], n_main)))
    z = in_proj_last(h1, w_in_far, z_most, shard_order, tm=ROW_TILE)
    W_in = whole(lax.dynamic_update_slice(w_in_near, w_in_far[None], (shard_order[3], 0, 0, 0)), 0)
    y_att, lse_att, *g2 = attn_fwd(z, bias, gather_hook([placed[4]] + placed[7:], S // ATT_TILE))
    y_sg, *g3 = sgu_fwd(z, amean, lng, lnb, wpair, bfull, tm=512, hook=gather_hook(placed[5:7], S // 512))
    gathered = g1 + [g2[0]] + g3 + g2[1:]
    W_ba, W_bs, W_out, W_xq, W_xkv, W_xo, W_fi, W_fo = [whole(g4, i + 1) for i, g4 in enumerate(gathered)]
    a_br, b_br, merged, x1 = merge_fwd(y_att, y_sg, z, x2d, W_ba, W_bs, W_out, tm=ROW_TILE)
    h2, xq = norm_mm(x1, g_xat, W_xq.reshape(1, D, D), tm=512, scale=XATT_HEAD_DIM ** -0.5, name="norm_mm_xq")
    mn, kv = norm_mm(mem2d, g_mem, W_xkv, tm=mem2d.shape[0], scale=None, name="norm_mm_kv")
    o_x, lse_x, x2 = xattn_fwd(xq, kv, W_xo, x1, tm=512)
    h3, gu = norm_mm(x2, g_ffn, W_fi, tm=ROW_TILE, scale=None, name="norm_mm_ffn")
    act, loss_vec, dx3, dg_fin = ffn_out_loss(gu, W_fo, x2, g_fin, tgt, tm=ROW_TILE)

    dgu = ffn_act_bwd(dx3, gu, W_fo, tm=ROW_TILE, nchunk=2)
    gW_fo = tn_mm(act, dx3, shards=1, tn=D, tk=DW_TOKENS, name="dw_ffn_out")
    gW_fi = tn_mm(h3, dgu, shards=4, tn=2 * W_fi.shape[2], tk=DW_TOKENS, name="dw_ffn_in")
    dx2, dg_ffn = mm_nt_norm_bwd(dgu, W_fi, x2, g_ffn, dx3, tm=ROW_TILE, name="dx_ffn")
    dq_x, dkv = xattn_bwd(dx2, xq, o_x, lse_x, kv, W_xo, tm=512)
    gW_xo = tn_mm(o_x, dx2, shards=1, tn=D, tk=DW_TOKENS, name="dw_xo")
    gW_xq = tn_mm(h2, dq_x, shards=1, tn=D, tk=DW_TOKENS, name="dw_xq")
    dx1, dg_xat = mm_nt_norm_bwd(dq_x, W_xq.reshape(1, D, D), x1, g_xat, dx2, tm=512, name="dx_xq")
    gW_xkv = tn_mm(mn, dkv, shards=4, tn=2 * D, tk=mem2d.shape[0], name="dw_xkv")
    _, dg_mem = mm_nt_norm_bwd(dkv.astype(BF16), W_xkv, mem2d, g_mem, None, tm=mem2d.shape[0], name="dx_mem")
    def canon(g, i):
        R, C = big_w[i].shape
        return g.reshape(4, 2, R // 2, C)

    def partials(gs, recv, idx):
        return [chip_partial(g, r, core, "chip_partial_" + big_names[i]) for g, r, i in zip(gs, recv, idx)]

    early_g = [canon(g, i) for g, i in zip([gW_xq, gW_xkv, gW_xo, gW_fi, gW_fo], range(4, 9))]
    da, db, dgab, dy_att, dy_sg, *early_r = merge_bwd(dx1, W_out, a_br, b_br, z, W_ba, W_bs, tm=ROW_TILE,
                                                      hook=split_hook(early_g, S // ROW_TILE))
    gW_out = tn_mm(merged, dx1, shards=1, tn=D, tk=DW_TOKENS, name="dw_out")
    gW_ba = tn_mm(y_att, da, shards=4, tn=D, tk=DW_TOKENS, name="dw_branch_att")
    gW_bs = tn_mm(y_sg, db, shards=4, tn=D, tk=DW_TOKENS, name="dw_branch_sg")
    mid_g = [canon(g, i) for g, i in zip([gW_ba, gW_bs, gW_out], range(1, 4))]
    parts_b = (partials(mid_g, sibling_split(mid_g, "sibling_split_mid"), range(1, 4))
               + partials(early_g, early_r, range(4, 9)))
    nt_bwd = S // ATT_TILE + ATT_WIN - 1
    dqkv, dbias, *from_chips_b = attn_bwd(z, y_att, dy_att, lse_att, bias, exchange_hook(parts_b, nt_bwd))
    duv, dwpair, dsgb_full, dlng, dlnb = sgu_bwd(z, dy_sg, amean, lng, lnb, wpair, wtpair, bfull, maskpair, tm=512)
    dwm = jnp.transpose(dwpair.reshape(SG_PAIRS, SG_BLOCK, 2, SG_BLOCK), (0, 2, 1, 3))
    dsgb = dsgb_full[:, ::SG_GROUP_DIM].T
    dz = [dqkv, duv, dgab]
    gW_in = tn_mm_pieces(h1, dz, shards=4, per=2, tk=DW_TOKENS, name="dw_in")
    in_g = [canon(gW_in, 0)]
    parts_a = partials(in_g, sibling_split(in_g, "sibling_split_w_in"), [0])
    d_rel = _rel_bias_grad(dbias)
    small_g = [d_rel, dlng, dlnb, dwm, dsgb, dg_xat, dg_mem, dg_ffn, dg_fin, loss_vec[0, :1]]
    n_dx = S // ROW_TILE
    dx, dg_mix, from_chips_a, small_all = mm_nt_norm_bwd(
        dz, W_in, x2d, g_mix, dx1, tm=ROW_TILE, name="dx_in",
        hook=merge_hooks(exchange_hook(parts_a, n_dx), allgather_hook(_pack_small(small_g), n_dx)))
    from_chips_a = [from_chips_a]

    parts, from_chips = parts_a + parts_b, from_chips_a + from_chips_b
    sums = [shard_sum(p, r, own, core, "shard_sum_" + nm) for p, r, nm in zip(parts, from_chips, big_names)]
    joined = sibling_join(sums)
    big_out = []
    for j, w, m_, v_, nm in zip(joined, big_w, big_m, big_v, big_names):
        g = j.reshape(w.shape)
        big_out.append((g,) + tuple(adamw(w, g, m_, v_, "adamw_" + nm)))

    small_w = [norm_mix_g, rel_bias, sg_ln_g, sg_ln_b, sg_w, sg_b, norm_xattn_g, norm_mem_g, norm_ffn_g, norm_final_g]
    small_m = [m_norm_mix_g, m_rel_bias, m_sg_ln_g, m_sg_ln_b, m_sg_w, m_sg_b, m_norm_xattn_g, m_norm_mem_g, m_norm_ffn_g, m_norm_final_g]
    small_v = [v_norm_mix_g, v_rel_bias, v_sg_ln_g, v_sg_ln_b, v_sg_w, v_sg_b, v_norm_xattn_g, v_norm_mem_g, v_norm_ffn_g, v_norm_final_g]
    shapes = [w.shape for w in small_w]
    g_sum = jnp.concatenate([allreduce_small(_pack_small([dg_mix])), sum_blocks(small_all)], axis=0)
    zero = jnp.zeros((1,), F32)
    d_s, m_s, v_s = adamw(_pack_small(small_w + [zero]), g_sum, _pack_small(small_m + [zero]), _pack_small(small_v + [zero]),
                          "adamw_small")
    sg_, sd, sm, sv_ = (_unpack_small(p, shapes + [(1,)]) for p in (g_sum, d_s, m_s, v_s))
    loss = sg_[-1][0]

    order = ["norm_mix_g", "w_in", "rel_bias", "sg_ln_g", "sg_ln_b", "sg_w", "sg_b", "w_branch_att", "w_branch_sg", "w_out",
             "norm_xattn_g", "norm_mem_g", "w_xq", "w_xkv", "w_xo", "norm_ffn_g", "w_ffn_in", "w_ffn_out", "norm_final_g"]
    small_names = ["norm_mix_g", "rel_bias", "sg_ln_g", "sg_ln_b", "sg_w", "sg_b", "norm_xattn_g", "norm_mem_g", "norm_ffn_g",
                   "norm_final_g"]
    res = {}
    for i, nm in enumerate(small_names):
        res[nm] = (sg_[i], sd[i], sm[i], sv_[i])
    for nm, outs in zip(big_names, big_out):
        res[nm] = tuple(o[None] for o in outs)
    return (loss, dx[None], *[res[nm][0] for nm in order], *[res[nm][1] for nm in order],
            *[res[nm][2] for nm in order], *[res[nm][3] for nm in order])
```

```python
import functools
from typing import NamedTuple

import numpy as np
import jax
import jax.numpy as jnp
from jax import lax
from jax.experimental import pallas as pl
from jax.experimental.pallas import tpu as pltpu

F32, BF16 = jnp.float32, jnp.bfloat16
MESH = pl.DeviceIdType.MESH

EPS = 1e-6
NEG_INF = -1e30
CHUNK = 64
N_PREV_CHUNKS = 8
ATT_HEADS, ATT_HEAD_DIM = 8, 64
REL_CLIP = 128
SG_BLOCK, SG_GROUPS, SG_GROUP_DIM = 128, 8, 64
XATT_HEADS, XATT_HEAD_DIM = 4, 256
ATT_TILE = 128
ATT_WIN = 5
ADAM_LR, ADAM_B1, ADAM_B2, ADAM_EPS, ADAM_WD, ADAM_STEP = 0.001, 0.9, 0.999, 1e-08, 0.01, 10

VMEM_LIMIT_V7X = 56 * 1024 * 1024
EW_BLOCK_BYTES = 2 << 20
ROW_TILE = 512
DW_TOKENS = 1024
HBM = pl.BlockSpec(memory_space=pl.ANY)


def _params(*sem):
    return pltpu.CompilerParams(dimension_semantics=sem, vmem_limit_bytes=VMEM_LIMIT_V7X)


def _full(shape):
    n = len(shape)
    return pl.BlockSpec(shape, lambda *_: (0,) * n, pipeline_mode=pl.Buffered(1))


def _acc(shape):
    n = len(shape)
    return pl.BlockSpec(shape, lambda *_: (0,) * n)


class Hook(NamedTuple):
    ins: tuple = ()
    out_shapes: tuple = ()
    alias: tuple = ()
    sems: tuple = ()
    steps: tuple = ()


def _run_hook(hook, pos, h_in, h_out, h_sems):
    for at, where, fn in hook.steps:
        if where == pos:
            pl.when(pl.program_id(0) == at)(functools.partial(fn, h_in, h_out, h_sems))


def _hooked_call(body, hook, *, n_in, n_out, in_specs, out_specs, out_shape, scratch_shapes=(), **kw):
    nh = len(hook.ins)
    aliases = {n_in + i: n_out + o for i, o in hook.alias}
    return pl.pallas_call(
        body, in_specs=list(in_specs) + [HBM] * nh, out_specs=list(out_specs) + [HBM] * len(hook.out_shapes),
        out_shape=list(out_shape) + list(hook.out_shapes), scratch_shapes=list(scratch_shapes) + list(hook.sems),
        input_output_aliases=aliases, **kw)


def _nt(a, b):
    return lax.dot_general(a, b, (((1,), (1,)), ((), ())), preferred_element_type=F32)


def _tn(a, b):
    return lax.dot_general(a, b, (((0,), (0,)), ((), ())), preferred_element_type=F32)


def _nn(a, b):
    return jnp.dot(a, b, preferred_element_type=F32)


def _sigmoid(x):
    return 0.5 * jnp.tanh(0.5 * x) + 0.5


_GELU_C = float(np.sqrt(2.0 / np.pi))


def _gelu(x):
    t = jnp.tanh(_GELU_C * (x + 0.044715 * (x * x * x)))
    return x * (0.5 * (1.0 + t))


def _gelu_grad(x):
    t = jnp.tanh(_GELU_C * (x + 0.044715 * (x * x * x)))
    return 0.5 * (1.0 + t) + 0.5 * x * (1.0 - t * t) * (_GELU_C * (1.0 + 3.0 * 0.044715 * x * x))


def _rms_stats(xf):
    rstd = lax.rsqrt(jnp.mean(xf * xf, axis=-1, keepdims=True) + EPS)
    return rstd, xf * rstd


def _rms_bwd(xhat, rstd, g, dh):
    dxh = dh * g
    dx = rstd * (dxh - xhat * jnp.mean(dxh * xhat, axis=-1, keepdims=True))
    return dx, dh * xhat


def norm_mm(x, g, w, *, tm, scale, name, hook=Hook()):
    S, D = x.shape
    J, _, C = w.shape
    nh, nho = len(hook.ins), len(hook.out_shapes)

    def body(x_ref, g_ref, w_ref, *rest):
        h_in, (h_ref, z_ref), h_out, h_sems = rest[:nh], rest[nh:nh + 2], rest[nh + 2:nh + 2 + nho], rest[nh + 2 + nho:]
        _run_hook(hook, "before", h_in, h_out, h_sems)
        _, xhat = _rms_stats(x_ref[...])
        h = (xhat * g_ref[...]).astype(BF16)
        h_ref[...] = h
        for j in range(J):
            acc = _nn(h, w_ref[j])
            if scale is not None:
                acc = acc * scale
            z_ref[:, j * C:(j + 1) * C] = acc.astype(BF16)
        _run_hook(hook, "after", h_in, h_out, h_sems)

    return _hooked_call(
        body, hook, n_in=3, n_out=2, grid=(S // tm,), name=name,
        in_specs=[pl.BlockSpec((tm, D), lambda i: (i, 0)), _full((1, D)), _full((J, D, C))],
        out_specs=[pl.BlockSpec((tm, D), lambda i: (i, 0)), pl.BlockSpec((tm, J * C), lambda i: (i, 0))],
        out_shape=[jax.ShapeDtypeStruct((S, D), BF16), jax.ShapeDtypeStruct((S, J * C), BF16)],
        compiler_params=_params("arbitrary"),
    )(x, g, w, *hook.ins)


def _att_window_specs(nt, col):
    return [pl.BlockSpec((ATT_TILE, 512), lambda t, j=j: (jnp.clip(t - (ATT_WIN - 1) + j, 0, nt - 1), col))
            for j in range(ATT_WIN)]


ATT_SCALE = ATT_HEAD_DIM ** -0.5


def _att_scores(q_scaled, k, bias, valid):
    return jnp.where(valid, _nt(q_scaled, k) + bias, NEG_INF)


def _att_valid(t):
    kpos = lax.broadcasted_iota(jnp.int32, (ATT_TILE, ATT_WIN * ATT_TILE), 1) + (t - (ATT_WIN - 1)) * ATT_TILE
    return kpos >= 0


def attn_fwd(z, bias, hook=Hook()):
    S = z.shape[0]
    nt = S // ATT_TILE
    n_in, nh, nho = 2 + 2 * ATT_WIN, len(hook.ins), len(hook.out_shapes)

    def body(q_ref, *refs):
        k_refs, v_refs = refs[:ATT_WIN], refs[ATT_WIN:2 * ATT_WIN]
        bias_ref = refs[2 * ATT_WIN]
        rest = refs[2 * ATT_WIN + 1:]
        h_in, (y_ref, lse_ref), h_out = rest[:nh], rest[nh:nh + 2], rest[nh + 2:nh + 2 + nho]
        s_scr, p_scr = rest[nh + 2 + nho:nh + 4 + nho]
        h_sems = rest[nh + 4 + nho:]
        _run_hook(hook, "before", h_in, h_out, h_sems)
        valid = _att_valid(pl.program_id(0))
        heads = [slice(h * ATT_HEAD_DIM, (h + 1) * ATT_HEAD_DIM) for h in range(ATT_HEADS)]
        for h, sl in enumerate(heads):
            k = jnp.concatenate([r[:, sl] for r in k_refs], axis=0)
            s_scr[h] = _att_scores(q_ref[:, sl] * ATT_SCALE, k, bias_ref[h], valid)
        stats = []
        for h in range(ATT_HEADS):
            s = s_scr[h]
            m = jnp.max(s, axis=-1, keepdims=True)
            p = jnp.exp(s - m)
            stats.append((m, jnp.sum(p, axis=-1, keepdims=True)))
            p_scr[h] = p.astype(BF16)
        for h, sl in enumerate(heads):
            m, l = stats[h]
            v = jnp.concatenate([r[:, sl] for r in v_refs], axis=0)
            y_ref[:, sl] = (_nn(p_scr[h], v) / l).astype(BF16)
            lse_ref[:, h:h + 1] = m + jnp.log(l)
        _run_hook(hook, "after", h_in, h_out, h_sems)

    tile = lambda col: pl.BlockSpec((ATT_TILE, 512), lambda t: (t, col))
    return _hooked_call(
        body, hook, n_in=n_in, n_out=2, grid=(nt,), name="attn_fwd",
        in_specs=[tile(0)] + _att_window_specs(nt, 1) + _att_window_specs(nt, 2) + [_full(bias.shape)],
        out_specs=[tile(0), pl.BlockSpec((ATT_TILE, ATT_HEADS), lambda t: (t, 0))],
        out_shape=[jax.ShapeDtypeStruct((S, 512), BF16), jax.ShapeDtypeStruct((S, ATT_HEADS), F32)],
        scratch_shapes=[pltpu.VMEM((ATT_HEADS, ATT_TILE, ATT_WIN * ATT_TILE), F32),
                        pltpu.VMEM((ATT_HEADS, ATT_TILE, ATT_WIN * ATT_TILE), BF16)],
        compiler_params=_params("arbitrary"),
    )(z, *([z] * (2 * ATT_WIN)), bias, *hook.ins)


SG_PAIRS = SG_GROUPS // 2


def _group_mean(x, a_ref):
    rows = x.shape[0]
    hi = x.astype(BF16)
    lo = (x - hi.astype(F32)).astype(BF16)
    both = _nn(jnp.concatenate([hi, lo], axis=0), a_ref[...])
    return both[:rows] + both[rows:]


def _pair_diag(x, ref):
    first = lax.broadcasted_iota(jnp.int32, x.shape, 1) < SG_GROUP_DIM
    ref[0:SG_BLOCK, :] = jnp.where(first, x, 0.0).astype(BF16)
    ref[SG_BLOCK:2 * SG_BLOCK, :] = jnp.where(first, 0.0, x).astype(BF16)


def _sgu_norm(zv_ref, a_ref, lng_ref, lnb_ref):
    v = _gelu(zv_ref[...].astype(F32))
    vc = v - _group_mean(v, a_ref)
    rstd = lax.rsqrt(_group_mean(vc * vc, a_ref) + EPS)
    xhat = vc * rstd
    return xhat, rstd, xhat * lng_ref[...] + lnb_ref[...]


def _sgu_rows(b):
    return pl.ds(pl.multiple_of(b * SG_BLOCK, SG_BLOCK), SG_BLOCK)


def sgu_fwd(z, amean, lng, lnb, wpair, bfull, *, tm, hook=Hook()):
    S = z.shape[0]
    nh, nho = len(hook.ins), len(hook.out_shapes)

    def body(zu_ref, zv_ref, a_ref, lng_ref, lnb_ref, wpair_ref, bfull_ref, *rest):
        h_in, y_ref, h_out = rest[:nh], rest[nh], rest[nh + 1:nh + 1 + nho]
        vn_scr, vbd = rest[nh + 1 + nho:nh + 3 + nho]
        h_sems = rest[nh + 3 + nho:]
        _run_hook(hook, "before", h_in, h_out, h_sems)
        vn_scr[...] = _sgu_norm(zv_ref, a_ref, lng_ref, lnb_ref)[2]

        def block(b, carry):
            rows = _sgu_rows(b)
            u = _gelu(zu_ref[rows, :].astype(F32))
            for p in range(SG_PAIRS):
                lanes = slice(p * 128, (p + 1) * 128)
                _pair_diag(vn_scr[rows, lanes], vbd.at[p])
                sv = _nn(wpair_ref[p], vbd[p]) + bfull_ref[:, lanes]
                y_ref[rows, lanes] = (u[:, lanes] * sv).astype(BF16)
            return carry

        lax.fori_loop(0, tm // SG_BLOCK, block, 0)
        _run_hook(hook, "after", h_in, h_out, h_sems)

    tile = lambda col: pl.BlockSpec((tm, 512), lambda i: (i, col))
    smalls = [amean, lng, lnb, wpair, bfull]
    return _hooked_call(
        body, hook, n_in=2 + len(smalls), n_out=1, grid=(S // tm,), name="sgu_fwd",
        in_specs=[tile(3), tile(4)] + [_full(a.shape) for a in smalls],
        out_specs=[tile(0)], out_shape=[jax.ShapeDtypeStruct((S, 512), BF16)],
        scratch_shapes=[pltpu.VMEM((tm, 512), F32), pltpu.VMEM((SG_PAIRS, 2 * SG_BLOCK, 128), BF16)],
        compiler_params=_params("arbitrary"),
    )(z, z, *smalls, *hook.ins)


def _gate_specs(tm):
    return [pl.BlockSpec((tm, 512), lambda i, c=c: (i, c)) for c in (5, 6, 7, 8)]


def merge_fwd(y_att, y_sg, z, x, w_ba, w_bs, w_out, *, tm):
    S, D = x.shape
    J, _, C = w_ba.shape

    def body(ya_ref, ys_ref, g0, g1, g2, g3, x_ref, wba_ref, wbs_ref, wo_ref, a_ref, b_ref, m_ref, x1_ref):
        ya, ys = ya_ref[...], ys_ref[...]
        a = jnp.concatenate([_nn(ya, wba_ref[j]) for j in range(J)], axis=1)
        b = jnp.concatenate([_nn(ys, wbs_ref[j]) for j in range(J)], axis=1)
        sa = _sigmoid(jnp.concatenate([g0[...], g1[...]], axis=1).astype(F32))
        sb = _sigmoid(jnp.concatenate([g2[...], g3[...]], axis=1).astype(F32))
        a_ref[...] = a.astype(BF16)
        b_ref[...] = b.astype(BF16)
        merged = (sa * a + sb * b).astype(BF16)
        m_ref[...] = merged
        x1_ref[...] = x_ref[...] + _nn(merged, wo_ref[...])

    row = lambda n: pl.BlockSpec((tm, n), lambda i: (i, 0))
    return pl.pallas_call(
        body, grid=(S // tm,), name="merge_fwd",
        in_specs=[row(512), row(512)] + _gate_specs(tm) + [row(D), _full(w_ba.shape), _full(w_bs.shape), _full(w_out.shape)],
        out_specs=[row(D)] * 4,
        out_shape=[jax.ShapeDtypeStruct((S, D), BF16)] * 3 + [jax.ShapeDtypeStruct((S, D), F32)],
        compiler_params=_params("parallel"),
    )(y_att, y_sg, z, z, z, z, x, w_ba, w_bs, w_out)


def xattn_fwd(xq, kv, w_xo, x1, *, tm):
    S, D = xq.shape
    dh = XATT_HEAD_DIM

    def body(q_ref, kv_ref, wo_ref, x1_ref, o_ref, lse_ref, x2_ref):
        outs = []
        for h in range(XATT_HEADS):
            s = _nt(q_ref[:, h * dh:(h + 1) * dh], kv_ref[:, h * dh:(h + 1) * dh])
            m = jnp.max(s, axis=-1, keepdims=True)
            p = jnp.exp(s - m)
            l = jnp.sum(p, axis=-1, keepdims=True)
            outs.append((_nn(p.astype(BF16), kv_ref[:, D + h * dh:D + (h + 1) * dh]) / l).astype(BF16))
            lse_ref[:, h:h + 1] = m + jnp.log(l)
        o = jnp.concatenate(outs, axis=1)
        o_ref[...] = o
        x2_ref[...] = x1_ref[...] + _nn(o, wo_ref[...])

    row = lambda n: pl.BlockSpec((tm, n), lambda i: (i, 0))
    return pl.pallas_call(
        body, grid=(S // tm,), name="xattn_fwd",
        in_specs=[row(D), _full(kv.shape), _full(w_xo.shape), row(D)],
        out_specs=[row(D), row(XATT_HEADS), row(D)],
        out_shape=[jax.ShapeDtypeStruct((S, D), BF16), jax.ShapeDtypeStruct((S, XATT_HEADS), F32),
                   jax.ShapeDtypeStruct((S, D), F32)],
        compiler_params=_params("parallel"),
    )(xq, kv, w_xo, x1)


def ffn_out_loss(gu, w, x2, g, target, *, tm):
    S, D = x2.shape
    F = w.shape[0]

    def body(gu_ref, w_ref, x2_ref, g_ref, t_ref, act_ref, loss_ref, dx_ref, dg_ref):
        @pl.when(pl.program_id(0) == 0)
        def _():
            loss_ref[...] = jnp.zeros_like(loss_ref)
            dg_ref[...] = jnp.zeros_like(dg_ref)

        half = gu_ref[:, :F] * 0.5
        act = (half * gu_ref[:, F:]) * (jnp.tanh(half) + 1.0)
        act_ref[...] = act
        gv = g_ref[...]
        rstd, xhat = _rms_stats(x2_ref[...] + _nn(act, w_ref[...]))
        err = xhat * gv - t_ref[...]
        loss_ref[...] += 0.5 * jnp.sum(jnp.mean(err * err, axis=-1, keepdims=True))
        dx, dgc = _rms_bwd(xhat, rstd, gv, err * (1.0 / D))
        dx_ref[...] = dx
        dg_ref[...] += jnp.sum(dgc, axis=0, keepdims=True)

    row = lambda n: pl.BlockSpec((tm, n), lambda i: (i, 0))
    return pl.pallas_call(
        body, grid=(S // tm,), name="ffn_out_loss",
        in_specs=[row(2 * F), _full(w.shape), row(D), _full((1, D)), row(D)],
        out_specs=[row(F), _acc((1, 128)), row(D), _acc((1, D))],
        out_shape=[jax.ShapeDtypeStruct((S, F), BF16), jax.ShapeDtypeStruct((1, 128), F32),
                   jax.ShapeDtypeStruct((S, D), F32), jax.ShapeDtypeStruct((1, D), F32)],
        compiler_params=_params("arbitrary"),
    )(gu, w, x2, g, target)


def ffn_act_bwd(dx3, gu, w, *, tm, nchunk):
    S, D = dx3.shape
    F = w.shape[0]
    cn = F // nchunk

    def body(dx_ref, gu_ref, w_ref, dgu_ref):
        dxb = dx_ref[...].astype(BF16)
        for j in range(nchunk):
            dact = _nt(dxb, w_ref[j * cn:(j + 1) * cn, :])
            gate = gu_ref[:, j * cn:(j + 1) * cn]
            up = gu_ref[:, F + j * cn:F + (j + 1) * cn]
            sg = _sigmoid(gate)
            silu = gate * sg
            dsilu = sg + silu * (1.0 - sg)
            dgu_ref[:, j * cn:(j + 1) * cn] = (dact * (up * dsilu).astype(F32)).astype(BF16)
            dgu_ref[:, F + j * cn:F + (j + 1) * cn] = (dact * silu.astype(F32)).astype(BF16)

    row = lambda n: pl.BlockSpec((tm, n), lambda i: (i, 0))
    return pl.pallas_call(
        body, grid=(S // tm,), name="ffn_act_bwd",
        in_specs=[row(D), row(2 * F), _full(w.shape)], out_specs=row(2 * F),
        out_shape=jax.ShapeDtypeStruct((S, 2 * F), BF16), compiler_params=_params("parallel"),
    )(dx3, gu, w)


def _overlaps(widths, lo, hi):
    out, off = [], 0
    for p, wd in enumerate(widths):
        a, b = max(lo, off), min(hi, off + wd)
        if a < b:
            out.append((p, a - off, b - off, a - lo))
        off += wd
    return out


def tn_mm_pieces(a, pieces, *, shards, per, tk, name):
    K, M = a.shape
    widths = [p.shape[1] for p in pieces]
    C = sum(widths) // shards
    tn = per * C
    nk = K // tk
    n_tiles = shards // per

    def body(a_ref, *refs):
        p_refs, o_ref, acc_ref = refs[:len(pieces)], refs[len(pieces)], refs[len(pieces) + 1]
        n, k = pl.program_id(0), pl.program_id(1)

        @pl.when(k == 0)
        def _():
            acc_ref[...] = jnp.zeros_like(acc_ref)

        av = a_ref[...]
        for tile in range(n_tiles):
            @pl.when(n == tile)
            def _(tile=tile):
                for p, c0, c1, at in _overlaps(widths, tile * tn, (tile + 1) * tn):
                    acc_ref[:, at:at + c1 - c0] += _tn(av, p_refs[p][:, c0:c1])

        @pl.when(k == nk - 1)
        def _():
            for s in range(per):
                o_ref[s] = acc_ref[:, s * C:(s + 1) * C].astype(BF16)

    return pl.pallas_call(
        body, grid=(n_tiles, nk), name=name,
        in_specs=[pl.BlockSpec((tk, M), lambda n, k: (k, 0))] + [pl.BlockSpec((tk, wd), lambda n, k: (k, 0)) for wd in widths],
        out_specs=pl.BlockSpec((per, M, C), lambda n, k: (n, 0, 0)), out_shape=jax.ShapeDtypeStruct((shards, M, C), BF16),
        scratch_shapes=[pltpu.VMEM((M, tn), F32)], compiler_params=_params("parallel", "arbitrary"),
    )(a, *pieces)


def tn_mm(a, b, *, shards, tn, tk, name):
    K, M = a.shape
    N = b.shape[1]
    C = N // shards
    per = tn // C
    nk = K // tk

    def body(a_ref, b_ref, o_ref, acc_ref):
        k = pl.program_id(1)

        @pl.when(k == 0)
        def _():
            acc_ref[...] = jnp.zeros_like(acc_ref)

        acc_ref[...] += _tn(a_ref[...].astype(BF16), b_ref[...].astype(BF16))

        @pl.when(k == nk - 1)
        def _():
            if shards > 1:
                for s in range(per):
                    o_ref[s] = acc_ref[:, s * C:(s + 1) * C].astype(BF16)
            else:
                o_ref[...] = acc_ref[...].astype(BF16)

    if shards > 1:
        out_spec = pl.BlockSpec((per, M, C), lambda n, k: (n, 0, 0))
        out_shape = jax.ShapeDtypeStruct((shards, M, C), BF16)
    else:
        out_spec = pl.BlockSpec((M, tn), lambda n, k: (0, n))
        out_shape = jax.ShapeDtypeStruct((M, N), BF16)
    return pl.pallas_call(
        body, grid=(N // tn, nk), name=name,
        in_specs=[pl.BlockSpec((tk, M), lambda n, k: (k, 0)), pl.BlockSpec((tk, tn), lambda n, k: (k, n))],
        out_specs=out_spec, out_shape=out_shape, scratch_shapes=[pltpu.VMEM((M, tn), F32)],
        compiler_params=_params("parallel", "arbitrary"),
    )(a, b)


def mm_nt_norm_bwd(dy, w, x, g, dx_in, *, tm, name, hook=Hook()):
    S, D = x.shape
    J, _, C = w.shape
    has_in = dx_in is not None
    dys = list(dy) if isinstance(dy, (list, tuple)) else [dy]
    widths = [d.shape[1] for d in dys]
    nd = len(dys)
    n_in, nh, nho = nd + 3 + has_in, len(hook.ins), len(hook.out_shapes)

    def body(*refs):
        dy_refs = refs[:nd]
        w_ref, x_ref, g_ref = refs[nd:nd + 3]
        dxin_ref = refs[nd + 3] if has_in else None
        rest = refs[n_in:]
        h_in, (dx_ref, dg_ref), h_out, h_sems = rest[:nh], rest[nh:nh + 2], rest[nh + 2:nh + 2 + nho], rest[nh + 2 + nho:]
        _run_hook(hook, "before", h_in, h_out, h_sems)

        @pl.when(pl.program_id(0) == 0)
        def _():
            dg_ref[...] = jnp.zeros_like(dg_ref)

        dh = None
        for j in range(J):
            for p, c0, c1, at in _overlaps(widths, j * C, (j + 1) * C):
                part = _nt(dy_refs[p][:, c0:c1], w_ref[j, :, at:at + c1 - c0])
                dh = part if dh is None else dh + part
        rstd, xhat = _rms_stats(x_ref[...])
        dx, dgc = _rms_bwd(xhat, rstd, g_ref[...], dh)
        dx_ref[...] = dx + dxin_ref[...] if has_in else dx
        dg_ref[...] += jnp.sum(dgc, axis=0, keepdims=True)
        _run_hook(hook, "after", h_in, h_out, h_sems)

    row = lambda n: pl.BlockSpec((tm, n), lambda i: (i, 0))
    ins = dys + [w, x, g] + ([dx_in] if has_in else [])
    return _hooked_call(
        body, hook, n_in=n_in, n_out=2, grid=(S // tm,), name=name,
        in_specs=[row(wd) for wd in widths] + [_full(w.shape), row(D), _full((1, D))] + ([row(D)] if has_in else []),
        out_specs=[row(D), _acc((1, D))],
        out_shape=[jax.ShapeDtypeStruct((S, D), F32), jax.ShapeDtypeStruct((1, D), F32)],
        compiler_params=_params("arbitrary"),
    )(*ins, *hook.ins)


def xattn_bwd(dx2, xq, o, lse, kv, w_xo, *, tm):
    S, D = xq.shape
    M = kv.shape[0]
    dh = XATT_HEAD_DIM

    def body(dx_ref, q_ref, o_ref, lse_ref, kv_ref, wo_ref, dq_ref, dkv_ref):
        @pl.when(pl.program_id(0) == 0)
        def _():
            dkv_ref[...] = jnp.zeros_like(dkv_ref)

        do = _nt(dx_ref[...].astype(BF16), wo_ref[...])
        for h in range(XATT_HEADS):
            hs = slice(h * dh, (h + 1) * dh)
            vs = slice(D + h * dh, D + (h + 1) * dh)
            q, k, v = q_ref[:, hs], kv_ref[:, hs], kv_ref[:, vs]
            do_h = do[:, hs]
            do_b = do_h.astype(BF16)
            p = jnp.exp(_nt(q, k) - lse_ref[:, h:h + 1])
            delta = jnp.sum(do_h * o_ref[:, hs].astype(F32), axis=-1, keepdims=True)
            ds = (p * (_nt(do_b, v) - delta)).astype(BF16)
            dq_ref[:, hs] = (_nn(ds, k) * (dh ** -0.5)).astype(BF16)
            dkv_ref[:, hs] += _tn(ds, q)
            dkv_ref[:, vs] += _tn(p.astype(BF16), do_b)

    row = lambda n: pl.BlockSpec((tm, n), lambda i: (i, 0))
    return pl.pallas_call(
        body, grid=(S // tm,), name="xattn_bwd",
        in_specs=[row(D), row(D), row(D), row(XATT_HEADS), _full(kv.shape), _full(w_xo.shape)],
        out_specs=[row(D), _acc((M, 2 * D))],
        out_shape=[jax.ShapeDtypeStruct((S, D), BF16), jax.ShapeDtypeStruct((M, 2 * D), F32)],
        compiler_params=_params("arbitrary"),
    )(dx2, xq, o, lse, kv, w_xo)


def merge_bwd(dx1, w_out, a, b, z, w_ba, w_bs, *, tm, hook=Hook()):
    S, D = dx1.shape
    J, W, C = w_ba.shape
    nh, nho = len(hook.ins), len(hook.out_shapes)

    def body(dx_ref, wo_ref, a_ref, b_ref, g0, g1, g2, g3, wba_ref, wbs_ref, *rest):
        h_in, (da_ref, db_ref, dg_ref, dya_ref, dys_ref) = rest[:nh], rest[nh:nh + 5]
        h_out, h_sems = rest[nh + 5:nh + 5 + nho], rest[nh + 5 + nho:]
        _run_hook(hook, "before", h_in, h_out, h_sems)
        dm = _nt(dx_ref[...].astype(BF16), wo_ref[...])
        sa = _sigmoid(jnp.concatenate([g0[...], g1[...]], axis=1))
        sb = _sigmoid(jnp.concatenate([g2[...], g3[...]], axis=1))
        dg_ref[:, :D] = (dm * (a_ref[...] * (sa * (1.0 - sa))).astype(F32)).astype(BF16)
        dg_ref[:, D:] = (dm * (b_ref[...] * (sb * (1.0 - sb))).astype(F32)).astype(BF16)
        da = (dm * sa.astype(F32)).astype(BF16)
        db = (dm * sb.astype(F32)).astype(BF16)
        da_ref[...] = da
        db_ref[...] = db
        dya = _nt(da[:, 0:C], wba_ref[0])
        dys = _nt(db[:, 0:C], wbs_ref[0])
        for j in range(1, J):
            dya += _nt(da[:, j * C:(j + 1) * C], wba_ref[j])
            dys += _nt(db[:, j * C:(j + 1) * C], wbs_ref[j])
        dya_ref[...] = dya.astype(BF16)
        dys_ref[...] = dys.astype(BF16)
        _run_hook(hook, "after", h_in, h_out, h_sems)

    row = lambda n: pl.BlockSpec((tm, n), lambda i: (i, 0))
    return _hooked_call(
        body, hook, n_in=10, n_out=5, grid=(S // tm,), name="merge_bwd",
        in_specs=[row(D), _full(w_out.shape), row(D), row(D)] + _gate_specs(tm) + [_full(w_ba.shape), _full(w_bs.shape)],
        out_specs=[row(D), row(D), row(2 * D), row(W), row(W)],
        out_shape=[jax.ShapeDtypeStruct((S, D), BF16)] * 2 + [jax.ShapeDtypeStruct((S, 2 * D), BF16)]
        + [jax.ShapeDtypeStruct((S, W), BF16)] * 2,
        compiler_params=_params("arbitrary"),
    )(dx1, w_out, a, b, z, z, z, z, w_ba, w_bs, *hook.ins)


def attn_bwd(z, y, dy, lse, bias, hook=Hook()):
    S = z.shape[0]
    nt = S // ATT_TILE
    back = ATT_WIN - 1
    n_in, nh, nho = 5 + 2 * ATT_WIN, len(hook.ins), len(hook.out_shapes)

    def body(q_ref, *refs):
        k_refs, v_refs = refs[:ATT_WIN], refs[ATT_WIN:2 * ATT_WIN]
        y_ref, dy_ref, lse_ref, bias_ref = refs[2 * ATT_WIN:2 * ATT_WIN + 4]
        rest = refs[2 * ATT_WIN + 4:]
        h_in, (dqkv_ref, dbias_ref), h_out = rest[:nh], rest[nh:nh + 2], rest[nh + 2:nh + 2 + nho]
        dk_acc, dv_acc, dq_buf = rest[nh + 2 + nho:nh + 5 + nho]
        h_sems = rest[nh + 5 + nho:]
        t = pl.program_id(0)
        _run_hook(hook, "before", h_in, h_out, h_sems)

        @pl.when(t == 0)
        def _():
            dbias_ref[...] = jnp.zeros_like(dbias_ref)
            dk_acc[...] = jnp.zeros_like(dk_acc)
            dv_acc[...] = jnp.zeros_like(dv_acc)
            dq_buf[...] = jnp.zeros_like(dq_buf)

        @pl.when(t < nt)
        def _():
            valid = _att_valid(t)
            q_t = (q_ref[...] * ATT_SCALE).astype(F32).T.astype(BF16)
            do_t = dy_ref[...].astype(F32).T.astype(BF16)
            for h in range(ATT_HEADS):
                sl = slice(h * ATT_HEAD_DIM, (h + 1) * ATT_HEAD_DIM)
                k = jnp.concatenate([r[:, sl] for r in k_refs], axis=0)
                v = jnp.concatenate([r[:, sl] for r in v_refs], axis=0)
                q, do_b = q_ref[:, sl] * ATT_SCALE, dy_ref[:, sl]
                p = jnp.exp(_att_scores(q, k, bias_ref[h], valid) - lse_ref[:, h:h + 1])
                delta = jnp.sum(do_b.astype(F32) * y_ref[:, sl].astype(F32), axis=-1, keepdims=True)
                ds = p * (_nt(do_b, v) - delta)
                dbias_ref[h] += ds
                ds_b = ds.astype(BF16)
                dq_buf[t % ATT_WIN, :, sl] = (_nn(ds_b, k) * ATT_SCALE).astype(BF16)
                dk_w = _nn(q_t[sl, :], ds_b)
                dv_w = _nn(do_t[sl, :], p.astype(BF16))
                for i in range(ATT_WIN):
                    slot = (t + 1 + i) % ATT_WIN
                    cols = slice(i * ATT_TILE, (i + 1) * ATT_TILE)
                    if i == back:
                        dk_acc[slot, sl, :] = dk_w[:, cols]
                        dv_acc[slot, sl, :] = dv_w[:, cols]
                    else:
                        dk_acc[slot, sl, :] += dk_w[:, cols]
                        dv_acc[slot, sl, :] += dv_w[:, cols]

        done = (t + 1) % ATT_WIN
        dqkv_ref[:, 0:512] = dq_buf[done]
        dqkv_ref[:, 512:1024] = dk_acc[done].T.astype(BF16)
        dqkv_ref[:, 1024:1536] = dv_acc[done].T.astype(BF16)
        _run_hook(hook, "after", h_in, h_out, h_sems)

    last = nt - 1
    tile = lambda col, n=512: pl.BlockSpec((ATT_TILE, n), lambda t: (jnp.minimum(t, last), col))
    late = pl.BlockSpec((ATT_TILE, 1536), lambda t: (jnp.maximum(t - back, 0), 0))
    return _hooked_call(
        body, hook, n_in=n_in, n_out=2, grid=(nt + back,), name="attn_bwd",
        in_specs=[tile(0)] + _att_window_specs(nt, 1) + _att_window_specs(nt, 2)
        + [tile(0), tile(0), tile(0, ATT_HEADS), _full(bias.shape)],
        out_specs=[late, _acc(bias.shape)],
        out_shape=[jax.ShapeDtypeStruct((S, 1536), BF16), jax.ShapeDtypeStruct(bias.shape, F32)],
        scratch_shapes=[pltpu.VMEM((ATT_WIN, 512, ATT_TILE), F32)] * 2 + [pltpu.VMEM((ATT_WIN, ATT_TILE, 512), BF16)],
        compiler_params=_params("arbitrary"),
    )(z, *([z] * (2 * ATT_WIN)), y, dy, lse, bias, *hook.ins)


def sgu_bwd(z, dy, amean, lng, lnb, wpair, wtpair, bfull, maskpair, *, tm):
    S = z.shape[0]
    n = S // tm

    def body(zu_ref, zv_ref, dy_ref, a_ref, lng_ref, lnb_ref, wpair_ref, wtpair_ref, bfull_ref, mask_ref,
             duv_ref, dw_ref, dsgb_ref, dlng_ref, dlnb_ref, xhat_scr, rstd_scr, vn_scr, dxh_scr, vbd, dbd,
             w_acc, b_acc, g_acc, s_acc):
        i = pl.program_id(0)

        @pl.when(i == 0)
        def _():
            for r in (w_acc, b_acc, g_acc, s_acc):
                r[...] = jnp.zeros_like(r)

        xhat_scr[...], rstd_scr[...], vn_scr[...] = _sgu_norm(zv_ref, a_ref, lng_ref, lnb_ref)

        def block(b, carry):
            rows = _sgu_rows(b)
            zu = zu_ref[rows, :].astype(F32)
            u, du = _gelu(zu), _gelu_grad(zu)
            dyv = dy_ref[rows, :].astype(F32)
            for p in range(SG_PAIRS):
                lanes = slice(p * 128, (p + 1) * 128)
                _pair_diag(vn_scr[rows, lanes], vbd.at[p])
                sv = _nn(wpair_ref[p], vbd[p]) + bfull_ref[:, lanes]
                duv_ref[rows, lanes] = (dyv[:, lanes] * sv * du[:, lanes]).astype(BF16)
                dsv = dyv[:, lanes] * u[:, lanes]
                b_acc[:, lanes] += dsv
                w_acc[p] += _nt(dsv.astype(BF16), vbd[p])
                _pair_diag(dsv, dbd.at[p])
                dvn = _nn(wtpair_ref[p], dbd[p])
                g_acc[:, lanes] += dvn * xhat_scr[rows, lanes]
                s_acc[:, lanes] += dvn
                dxh_scr[rows, lanes] = dvn * lng_ref[:, lanes]
            return carry

        lax.fori_loop(0, tm // SG_BLOCK, block, 0)
        dxh, xhat = dxh_scr[...], xhat_scr[...]
        dv = rstd_scr[...] * (dxh - _group_mean(dxh, a_ref) - xhat * _group_mean(dxh * xhat, a_ref))
        duv_ref[:, 512:1024] = (dv * _gelu_grad(zv_ref[...].astype(F32))).astype(BF16)

        @pl.when(i == n - 1)
        def _():
            dw_ref[...] = w_acc[...] * mask_ref[...]
            dsgb_ref[...] = _group_mean(b_acc[...], a_ref) * float(SG_GROUP_DIM)
            dlng_ref[...] = jnp.sum(g_acc[...], axis=0, keepdims=True)
            dlnb_ref[...] = jnp.sum(s_acc[...], axis=0, keepdims=True)

    tile = lambda col: pl.BlockSpec((tm, 512), lambda i: (i, col))
    smalls = [amean, lng, lnb, wpair, wtpair, bfull, maskpair]
    pair_shape = (SG_PAIRS, SG_BLOCK, 2 * SG_BLOCK)
    return pl.pallas_call(
        body, grid=(n,), name="sgu_bwd",
        in_specs=[tile(3), tile(4), tile(0)] + [_full(a.shape) for a in smalls],
        out_specs=[pl.BlockSpec((tm, 1024), lambda i: (i, 0)), _acc(pair_shape), _acc((SG_BLOCK, 512)),
                   _acc((1, 512)), _acc((1, 512))],
        out_shape=[jax.ShapeDtypeStruct((S, 1024), BF16), jax.ShapeDtypeStruct(pair_shape, F32),
                   jax.ShapeDtypeStruct((SG_BLOCK, 512), F32), jax.ShapeDtypeStruct((1, 512), F32),
                   jax.ShapeDtypeStruct((1, 512), F32)],
        scratch_shapes=[pltpu.VMEM((tm, 512), F32)] * 4
        + [pltpu.VMEM((SG_PAIRS, 2 * SG_BLOCK, 128), BF16)] * 2
        + [pltpu.VMEM(pair_shape, F32)] + [pltpu.VMEM((SG_BLOCK, 512), F32)] * 3,
        compiler_params=_params("arbitrary"),
    )(z, z, dy, *smalls)


def bias_colsum(p):
    H, _, L = p.shape
    near_lo = (ATT_WIN - 1) * ATT_TILE - REL_CLIP + 1
    near_hi = ATT_WIN * ATT_TILE

    def body(p_ref, gw_ref, far_ref):
        k = lax.broadcasted_iota(jnp.int32, (1, L), 1)
        is_far = (k < near_lo) | (k >= near_hi)
        for h in range(H):
            g = jnp.sum(p_ref[h], axis=0, keepdims=True)
            gw_ref[h:h + 1, :] = g
            far_ref[h:h + 1, :] = jnp.zeros((1, 128), F32) + jnp.sum(jnp.where(is_far, g, 0.0))

    return pl.pallas_call(
        body, name="bias_colsum", in_specs=[_full(p.shape)], out_specs=[_acc((H, L)), _acc((H, 128))],
        out_shape=[jax.ShapeDtypeStruct((H, L), F32), jax.ShapeDtypeStruct((H, 128), F32)], grid=(1,),
        compiler_params=_params("arbitrary"),
    )(p)


def _row_tile(rows, cols):
    best = None
    for tr in range(16, rows + 1, 16):
        if rows % tr == 0 and tr * cols * 4 <= EW_BLOCK_BYTES:
            best = tr
    return best if best is not None else rows


def place_shard(w, own, name):
    R, C = w.shape
    r = R // 2
    tr = _row_tile(r, C)
    nr = r // tr

    def body(own_ref, w_ref, o_ref):
        o_ref[...] = w_ref[...].astype(BF16)

    return pl.pallas_call(
        body, name=name,
        grid_spec=pltpu.PrefetchScalarGridSpec(
            num_scalar_prefetch=1, grid=(2, nr),
            in_specs=[pl.BlockSpec((tr, C), lambda h, i, own: (h * nr + i, 0))],
            out_specs=pl.BlockSpec((None, None, tr, C), lambda h, i, own: (own[0], h, i, 0))),
        out_shape=jax.ShapeDtypeStruct((4, 2, r, C), BF16), compiler_params=_params("parallel", "parallel"),
    )(own, w)


def chip_partial(grad, recv, core, name):
    _, _, r, C = grad.shape
    tr = _row_tile(r, C)

    def body(core_ref, g_ref, r_ref, o_ref):
        o_ref[...] = (g_ref[...].astype(F32) + r_ref[...].astype(F32)).astype(BF16)

    spec = pl.BlockSpec((None, tr, C), lambda j, i, core: (j, i, 0))
    return pl.pallas_call(
        body, name=name,
        grid_spec=pltpu.PrefetchScalarGridSpec(
            num_scalar_prefetch=1, grid=(4, r // tr),
            in_specs=[pl.BlockSpec((None, None, tr, C), lambda j, i, core: (j, core[0], i, 0)), spec], out_specs=spec),
        out_shape=jax.ShapeDtypeStruct((4, r, C), BF16), compiler_params=_params("parallel", "parallel"),
    )(core, grad, recv)


def shard_sum(part, recv, own, core, name):
    _, r, C = part.shape
    tr = _row_tile(r, C)

    def body(own_ref, core_ref, p_ref, r0, r1, r2, o_ref):
        o_ref[...] = p_ref[...].astype(F32) + r0[...].astype(F32) + r1[...].astype(F32) + r2[...].astype(F32)

    return pl.pallas_call(
        body, name=name,
        grid_spec=pltpu.PrefetchScalarGridSpec(
            num_scalar_prefetch=2, grid=(r // tr,),
            in_specs=[pl.BlockSpec((None, tr, C), lambda i, own, core: (own[0], i, 0))]
            + [pl.BlockSpec((None, tr, C), lambda i, own, core, k=k: (k, i, 0)) for k in range(3)],
            out_specs=pl.BlockSpec((None, tr, C), lambda i, own, core: (core[0], i, 0))),
        out_shape=jax.ShapeDtypeStruct((2, r, C), F32), compiler_params=_params("parallel"),
    )(own, core, part, recv, recv, recv)


def adamw(w, g, m, v, name):
    R, C = w.shape
    tr = _row_tile(R, C)

    def body(w_ref, g_ref, m_ref, v_ref, d_ref, nm_ref, nv_ref):
        gv = g_ref[...]
        nm = ADAM_B1 * m_ref[...] + (1.0 - ADAM_B1) * gv
        nv = ADAM_B2 * v_ref[...] + (1.0 - ADAM_B2) * (gv * gv)
        m_hat = nm / (1.0 - ADAM_B1 ** ADAM_STEP)
        v_hat = nv / (1.0 - ADAM_B2 ** ADAM_STEP)
        d_ref[...] = -ADAM_LR * (m_hat / (jnp.sqrt(v_hat) + ADAM_EPS) + ADAM_WD * w_ref[...])
        nm_ref[...] = nm
        nv_ref[...] = nv

    spec = pl.BlockSpec((tr, C), lambda i: (i, 0))
    return pl.pallas_call(
        body, grid=(R // tr,), name=name, in_specs=[spec] * 4, out_specs=[spec] * 3,
        out_shape=[jax.ShapeDtypeStruct((R, C), F32)] * 3, compiler_params=_params("parallel"),
    )(w, g, m, v)


HBM = pl.BlockSpec(memory_space=pl.ANY)


def _place():
    x, y, c = lax.axis_index("x"), lax.axis_index("y"), lax.axis_index("c")
    chips = [(1 - x, y), (x, 1 - y), (1 - x, 1 - y)]
    return x, y, c, chips


def _gather_phases(n):
    def copies(buf, sems, kind):
        send_sems, recv_sems = sems
        x, y, c, chips = _place()

        def remote(w, k, slot, to):
            return pltpu.make_async_remote_copy(src_ref=slot, dst_ref=slot, send_sem=send_sems.at[w, k],
                                                recv_sem=recv_sems.at[w, k], device_id=to, device_id_type=MESH)

        def one(w, k, px, py):
            if kind == "mine":
                return remote(w, k, buf[w].at[2 * x + y, c], (px, py, c))
            if kind == "theirs":
                return remote(w, k, buf[w].at[2 * px + py, c], (px, py, c))
            if kind == "onward":
                return remote(w, 3 + k, buf[w].at[2 * px + py, c], (x, y, 1 - c))
            return remote(w, 3 + k, buf[w].at[2 * px + py, 1 - c], (x, y, 1 - c))

        return [one(w, k, px, py) for w in range(n) for k, (px, py) in enumerate(chips)]

    def start(_, buf, sems):
        for cp in copies(buf, sems, "mine"):
            cp.start()

    def relay(_, buf, sems):
        for theirs, onward in zip(copies(buf, sems, "theirs"), copies(buf, sems, "onward")):
            theirs.wait_recv()
            onward.start()

    def finish(_, buf, sems):
        for cp in copies(buf, sems, "relayed"):
            cp.wait_recv()
        for cp in copies(buf, sems, "mine") + copies(buf, sems, "onward"):
            cp.wait_send()

    return start, relay, finish


def _gather_sems(n):
    return (pltpu.SemaphoreType.DMA((n, 6)), pltpu.SemaphoreType.DMA((n, 6)))


def gather_weights(bufs):
    n = len(bufs)
    phases = _gather_phases(n)

    def body(*refs):
        for phase in phases:
            phase(None, refs[n:2 * n], refs[2 * n:])

    return pl.pallas_call(
        body, name="gather_weights", in_specs=[HBM] * n, out_specs=[HBM] * n,
        out_shape=[jax.ShapeDtypeStruct(b.shape, b.dtype) for b in bufs],
        input_output_aliases={i: i for i in range(n)}, scratch_shapes=list(_gather_sems(n)),
    )(*bufs)


def gather_hook(bufs, n_steps):
    start, relay, finish = _gather_phases(len(bufs))
    return Hook(ins=tuple(bufs), out_shapes=tuple(jax.ShapeDtypeStruct(b.shape, b.dtype) for b in bufs),
                alias=tuple((i, i) for i in range(len(bufs))),
                sems=_gather_sems(len(bufs)),
                steps=((0, "before", start), ((n_steps * 13) // 16, "before", relay), (n_steps - 1, "after", finish)))


def sibling_split(grads, name):
    n = len(grads)

    def body(*refs):
        src, dst = refs[:n], refs[n:2 * n]
        send_sems, recv_sems = refs[2 * n:]
        x, y, c, _ = _place()
        sends = [pltpu.make_async_remote_copy(src_ref=src[w].at[:, 1 - c], dst_ref=dst[w], send_sem=send_sems.at[w],
                                              recv_sem=recv_sems.at[w], device_id=(x, y, 1 - c), device_id_type=MESH)
                 for w in range(n)]
        for cp in sends:
            cp.start()
        for cp in sends:
            cp.wait()

    return pl.pallas_call(
        body, name=name, in_specs=[HBM] * n, out_specs=[HBM] * n,
        out_shape=[jax.ShapeDtypeStruct((4,) + g.shape[2:], g.dtype) for g in grads],
        scratch_shapes=[pltpu.SemaphoreType.DMA((n,))] * 2,
    )(*grads)


def split_hook(grads, n_steps):
    n = len(grads)

    def copies(src, dst, sems):
        send_sems, recv_sems = sems
        x, y, c, _ = _place()
        return [pltpu.make_async_remote_copy(src_ref=src[w].at[:, 1 - c], dst_ref=dst[w], send_sem=send_sems.at[w],
                                             recv_sem=recv_sems.at[w], device_id=(x, y, 1 - c), device_id_type=MESH)
                for w in range(n)]

    def start(src, dst, sems):
        for cp in copies(src, dst, sems):
            cp.start()

    def finish(src, dst, sems):
        for cp in copies(src, dst, sems):
            cp.wait()

    return Hook(ins=tuple(grads), out_shapes=tuple(jax.ShapeDtypeStruct((4,) + g.shape[2:], g.dtype) for g in grads),
                sems=(pltpu.SemaphoreType.DMA((n,)), pltpu.SemaphoreType.DMA((n,))),
                steps=((0, "before", start), (n_steps - 1, "after", finish)))


def exchange_hook(parts, n_steps):
    n = len(parts)

    def copies(src, dst, sems):
        send_sems, recv_sems = sems
        _, _, c, chips = _place()
        return [pltpu.make_async_remote_copy(
            src_ref=src[w].at[2 * px + py], dst_ref=dst[w].at[k], send_sem=send_sems.at[w, k],
            recv_sem=recv_sems.at[w, k], device_id=(px, py, c), device_id_type=MESH)
            for w in range(n) for k, (px, py) in enumerate(chips)]

    def start(src, dst, sems):
        for cp in copies(src, dst, sems):
            cp.start()

    def finish(src, dst, sems):
        for cp in copies(src, dst, sems):
            cp.wait()

    return Hook(ins=tuple(parts), out_shapes=tuple(jax.ShapeDtypeStruct((3,) + p.shape[1:], p.dtype) for p in parts),
                sems=(pltpu.SemaphoreType.DMA((n, 3)), pltpu.SemaphoreType.DMA((n, 3))),
                steps=((0, "before", start), (n_steps - 1, "after", finish)))


def sibling_join(sums):
    n = len(sums)

    def body(*refs):
        buf = refs[n:2 * n]
        send_sems, recv_sems = refs[2 * n:]
        x, y, c, _ = _place()
        sends = [pltpu.make_async_remote_copy(src_ref=buf[w].at[c], dst_ref=buf[w].at[c], send_sem=send_sems.at[w],
                                              recv_sem=recv_sems.at[w], device_id=(x, y, 1 - c), device_id_type=MESH)
                 for w in range(n)]
        for cp in sends:
            cp.start()
        for w, cp in enumerate(sends):
            cp.wait_send()
            pltpu.make_async_remote_copy(src_ref=buf[w].at[c], dst_ref=buf[w].at[1 - c], send_sem=send_sems.at[w],
                                         recv_sem=recv_sems.at[w], device_id=(x, y, 1 - c), device_id_type=MESH).wait_recv()

    return pl.pallas_call(
        body, name="sibling_join", in_specs=[HBM] * n, out_specs=[HBM] * n,
        out_shape=[jax.ShapeDtypeStruct(s.shape, s.dtype) for s in sums],
        input_output_aliases={i: i for i in range(n)},
        scratch_shapes=[pltpu.SemaphoreType.DMA((n,))] * 2,
    )(*sums)


def allreduce_small(v):
    R, C = v.shape

    def body(v_ref, out_ref, all_ref, send_sems, recv_sems):
        x, y, c, chips = _place()
        me, sibling = (x, y, c), (x, y, 1 - c)

        def slot(px, py, pc):
            return all_ref.at[4 * px + 2 * py + pc]

        def copy(k, block, to, src=None):
            return pltpu.make_async_remote_copy(src_ref=slot(*block) if src is None else src, dst_ref=slot(*block),
                                                send_sem=send_sems.at[k], recv_sem=recv_sems.at[k], device_id=to,
                                                device_id_type=MESH)

        first = [copy(0, me, sibling, src=v_ref)] + [copy(1 + j, me, (*chip, c), src=v_ref) for j, chip in enumerate(chips)]
        for cp in first:
            cp.start()
        slot(*me)[...] = v_ref[...]
        passed = [copy(4 + j, (*chip, c), sibling) for j, chip in enumerate(chips)]
        for j, chip in enumerate(chips):
            copy(1 + j, (*chip, c), me).wait_recv()
            passed[j].start()
        copy(0, sibling, me).wait_recv()
        for j, chip in enumerate(chips):
            copy(4 + j, (*chip, 1 - c), me).wait_recv()
        for cp in first + passed:
            cp.wait_send()
        acc = all_ref[0]
        for k in range(1, 8):
            acc = acc + all_ref[k]
        out_ref[...] = acc

    vmem = pl.BlockSpec(memory_space=pltpu.VMEM)
    return pl.pallas_call(
        body, name="allreduce_small", in_specs=[vmem], out_specs=vmem, out_shape=jax.ShapeDtypeStruct((R, C), F32),
        scratch_shapes=[pltpu.VMEM((8, R, C), F32), pltpu.SemaphoreType.DMA((7,)), pltpu.SemaphoreType.DMA((7,))],
        compiler_params=pltpu.CompilerParams(vmem_limit_bytes=VMEM_LIMIT_V7X),
    )(v)


def allgather_hook(v, n_steps):
    R, C = v.shape

    def copies(v_ref, out, sems, kind):
        send_sems, recv_sems, _ = sems
        x, y, c, chips = _place()
        me, sibling = (x, y, c), (x, y, 1 - c)

        def slot(px, py, pc):
            return out.at[4 * px + 2 * py + pc]

        def copy(k, block, to, src=None):
            return pltpu.make_async_remote_copy(src_ref=slot(*block) if src is None else src, dst_ref=slot(*block),
                                                send_sem=send_sems.at[k], recv_sem=recv_sems.at[k], device_id=to,
                                                device_id_type=MESH)

        if kind == "mine":
            return [copy(0, me, sibling, src=v_ref)] + [copy(1 + j, me, (*chip, c), src=v_ref) for j, chip in enumerate(chips)]
        if kind == "theirs":
            return [copy(1 + j, (*chip, c), me) for j, chip in enumerate(chips)]
        if kind == "onward":
            return [copy(4 + j, (*chip, c), sibling) for j, chip in enumerate(chips)]
        return [copy(0, sibling, me)] + [copy(4 + j, (*chip, 1 - c), me) for j, chip in enumerate(chips)]

    def own(v_ref, out, sems):
        x, y, c, _ = _place()
        return pltpu.make_async_copy(v_ref, out.at[4 * x + 2 * y + c], sems[2])

    def start(ins, outs, sems):
        own(ins[0], outs[0], sems).start()
        for cp in copies(ins[0], outs[0], sems, "mine"):
            cp.start()

    def relay(ins, outs, sems):
        for theirs, onward in zip(copies(ins[0], outs[0], sems, "theirs"), copies(ins[0], outs[0], sems, "onward")):
            theirs.wait_recv()
            onward.start()

    def finish(ins, outs, sems):
        for cp in copies(ins[0], outs[0], sems, "relayed"):
            cp.wait_recv()
        for cp in copies(ins[0], outs[0], sems, "mine") + copies(ins[0], outs[0], sems, "onward"):
            cp.wait_send()
        own(ins[0], outs[0], sems).wait()

    return Hook(ins=(v,), out_shapes=(jax.ShapeDtypeStruct((8, R, C), F32),),
                sems=(pltpu.SemaphoreType.DMA((7,)), pltpu.SemaphoreType.DMA((7,)), pltpu.SemaphoreType.DMA(())),
                steps=((0, "before", start), ((n_steps * 13) // 16, "before", relay), (n_steps - 1, "after", finish)))


def merge_hooks(a, b):
    cut = (len(a.ins), len(a.out_shapes), len(a.sems))

    def part(fn, second):
        def run(ins, outs, sems):
            if second:
                fn(ins[cut[0]:], outs[cut[1]:], sems[cut[2]:])
            else:
                fn(ins[:cut[0]], outs[:cut[1]], sems[:cut[2]])
        return run

    steps = tuple((s, w, part(fn, False)) for s, w, fn in a.steps) + tuple((s, w, part(fn, True)) for s, w, fn in b.steps)
    alias = a.alias + tuple((i + cut[0], o + cut[1]) for i, o in b.alias)
    return Hook(a.ins + b.ins, a.out_shapes + b.out_shapes, alias, a.sems + b.sems, steps)


def sum_blocks(g):
    n, R, C = g.shape

    def body(g_ref, o_ref):
        acc = g_ref[0]
        for k in range(1, n):
            acc = acc + g_ref[k]
        o_ref[...] = acc

    return pl.pallas_call(body, name="sum_blocks", grid=(1,), in_specs=[_full(g.shape)], out_specs=_acc((R, C)),
                          out_shape=jax.ShapeDtypeStruct((R, C), F32), compiler_params=_params("arbitrary"))(g)


SMALL_PAD = 1024


def _pack_small(arrs):
    parts = []
    for a in arrs:
        f = a.reshape(-1).astype(F32)
        parts.append(jnp.pad(f, (0, (-f.shape[0]) % SMALL_PAD)))
    return jnp.concatenate(parts).reshape(-1, 128)


def _unpack_small(packed, shapes):
    flat = packed.reshape(-1)
    outs, off = [], 0
    for s in shapes:
        size = int(np.prod(s))
        outs.append(flat[off:off + size].reshape(s))
        off += size + (-size) % SMALL_PAD
    return outs


ATT_KEYS = ATT_WIN * ATT_TILE
ATT_NEAR_LO = (ATT_WIN - 1) * ATT_TILE - REL_CLIP + 1
ATT_PERIOD = ATT_KEYS + ATT_TILE + 1


def _att_bias_table(rel_bias):
    far = rel_bias[:, 2 * REL_CLIP:]
    near = rel_bias[:, 2 * REL_CLIP - 1:0:-1]
    w = jnp.concatenate([jnp.broadcast_to(far, (ATT_HEADS, ATT_NEAR_LO)), near,
                         jnp.broadcast_to(far, (ATT_HEADS, ATT_PERIOD - ATT_KEYS))], axis=1)
    rows = jnp.tile(w, (1, ATT_TILE))[:, :ATT_TILE * (ATT_PERIOD - 1)].reshape(ATT_HEADS, ATT_TILE, ATT_PERIOD - 1)
    i = np.arange(ATT_TILE)[:, None]
    m = np.arange(ATT_KEYS)[None, :]
    qc, kc = i // CHUNK, m // CHUNK
    band = (kc >= qc) & (kc <= qc + N_PREV_CHUNKS)
    return jnp.where(band[None], rows[:, :, :ATT_KEYS], NEG_INF)


def _rel_bias_grad(dbias):
    p = jnp.pad(dbias, ((0, 0), (0, 0), (0, ATT_PERIOD - 1 - ATT_KEYS))).reshape(ATT_HEADS, -1)
    p = jnp.pad(p, ((0, 0), (0, ATT_TILE))).reshape(ATT_HEADS, ATT_TILE, ATT_PERIOD)
    gw, far = bias_colsum(p)
    return jnp.concatenate([jnp.zeros((ATT_HEADS, 1), F32), gw[:, ATT_KEYS - 1:ATT_NEAR_LO - 1:-1], far[:, :1]], axis=1)


def kernel(x, mem, norm_mix_g, w_in, rel_bias, sg_ln_g, sg_ln_b, sg_w, sg_b, w_branch_att, w_branch_sg, w_out, norm_xattn_g, norm_mem_g, w_xq, w_xkv, w_xo, norm_ffn_g, w_ffn_in, w_ffn_out, norm_final_g, loss_target, m_norm_mix_g, m_w_in, m_rel_bias, m_sg_ln_g, m_sg_ln_b, m_sg_w, m_sg_b, m_w_branch_att, m_w_branch_sg, m_w_out, m_norm_xattn_g, m_norm_mem_g, m_w_xq, m_w_xkv, m_w_xo, m_norm_ffn_g, m_w_ffn_in, m_w_ffn_out, m_norm_final_g, v_norm_mix_g, v_w_in, v_rel_bias, v_sg_ln_g, v_sg_ln_b, v_sg_w, v_sg_b, v_w_branch_att, v_w_branch_sg, v_w_out, v_norm_xattn_g, v_norm_mem_g, v_w_xq, v_w_xkv, v_w_xo, v_norm_ffn_g, v_w_ffn_in, v_w_ffn_out, v_norm_final_g):
    S, D = x.shape[1], x.shape[2]
    x2d, mem2d, tgt = x[0], mem[0], loss_target[0]

    big_names = ["w_in", "w_branch_att", "w_branch_sg", "w_out", "w_xq", "w_xkv", "w_xo", "w_ffn_in", "w_ffn_out"]
    col_sharded = [True, True, True, False, False, True, False, True, False]
    big_w = [a[0] for a in (w_in, w_branch_att, w_branch_sg, w_out, w_xq, w_xkv, w_xo, w_ffn_in, w_ffn_out)]
    big_m = [a[0] for a in (m_w_in, m_w_branch_att, m_w_branch_sg, m_w_out, m_w_xq, m_w_xkv, m_w_xo, m_w_ffn_in, m_w_ffn_out)]
    big_v = [a[0] for a in (v_w_in, v_w_branch_att, v_w_branch_sg, v_w_out, v_w_xq, v_w_xkv, v_w_xo, v_w_ffn_in, v_w_ffn_out)]

    own = (2 * lax.axis_index("x") + lax.axis_index("y")).astype(jnp.int32).reshape(1)
    core = lax.axis_index("c").astype(jnp.int32).reshape(1)
    placed = [place_shard(w, own, "place_" + nm) for w, nm in zip(big_w, big_names)]

    def whole(g4, i):
        R, C = big_w[i].shape
        return g4.reshape(4, R, C) if col_sharded[i] else g4.reshape(4 * R, C)

    W_in = whole(gather_weights(placed[:1])[0], 0)

    g_mix, g_xat, g_mem, g_ffn = norm_mix_g, norm_xattn_g, norm_mem_g, norm_ffn_g
    g_fin = norm_final_g.reshape(1, D)
    bias = _att_bias_table(rel_bias[0])
    tt = np.arange(SG_BLOCK)
    sg_mask = (tt[None, :] // CHUNK) <= (tt[:, None] // CHUNK)
    wm_f = jnp.where(sg_mask[None], sg_w[0], 0.0)
    pairs = wm_f.reshape(SG_PAIRS, 2, SG_BLOCK, SG_BLOCK)
    wpair = jnp.transpose(pairs, (0, 2, 1, 3)).reshape(SG_PAIRS, SG_BLOCK, 2 * SG_BLOCK).astype(BF16)
    wtpair = jnp.transpose(pairs, (0, 3, 1, 2)).reshape(SG_PAIRS, SG_BLOCK, 2 * SG_BLOCK).astype(BF16)
    maskpair = jnp.asarray(np.tile(sg_mask, (SG_PAIRS, 1, 2)), F32)
    bfull = jnp.repeat(sg_b[0].T, SG_GROUP_DIM, axis=1)
    lng, lnb = sg_ln_g[0].reshape(1, 512), sg_ln_b[0].reshape(1, 512)
    gid = np.arange(512) // SG_GROUP_DIM
    amean = jnp.asarray((gid[:, None] == gid[None, :]) / SG_GROUP_DIM, BF16)

    h1, z, *early = norm_mm(x2d, g_mix, W_in, tm=ROW_TILE, scale=None, name="norm_mm_in",
                            hook=gather_hook(placed[1:7], S // ROW_TILE))
    y_att, lse_att, *late = attn_fwd(z, bias, gather_hook(placed[7:], S // ATT_TILE))
    W_ba, W_bs, W_out, W_xq, W_xkv, W_xo, W_fi, W_fo = [whole(g4, i + 1) for i, g4 in enumerate(early + late)]
    y_sg, = sgu_fwd(z, amean, lng, lnb, wpair, bfull, tm=512)
    a_br, b_br, merged, x1 = merge_fwd(y_att, y_sg, z, x2d, W_ba, W_bs, W_out, tm=ROW_TILE)
    h2, xq = norm_mm(x1, g_xat, W_xq.reshape(1, D, D), tm=512, scale=XATT_HEAD_DIM ** -0.5, name="norm_mm_xq")
    mn, kv = norm_mm(mem2d, g_mem, W_xkv, tm=mem2d.shape[0], scale=None, name="norm_mm_kv")
    o_x, lse_x, x2 = xattn_fwd(xq, kv, W_xo, x1, tm=512)
    h3, gu = norm_mm(x2, g_ffn, W_fi, tm=ROW_TILE, scale=None, name="norm_mm_ffn")
    act, loss_vec, dx3, dg_fin = ffn_out_loss(gu, W_fo, x2, g_fin, tgt, tm=ROW_TILE)

    dgu = ffn_act_bwd(dx3, gu, W_fo, tm=ROW_TILE, nchunk=2)
    gW_fo = tn_mm(act, dx3, shards=1, tn=D, tk=DW_TOKENS, name="dw_ffn_out")
    gW_fi = tn_mm(h3, dgu, shards=4, tn=2 * W_fi.shape[2], tk=DW_TOKENS, name="dw_ffn_in")
    dx2, dg_ffn = mm_nt_norm_bwd(dgu, W_fi, x2, g_ffn, dx3, tm=ROW_TILE, name="dx_ffn")
    dq_x, dkv = xattn_bwd(dx2, xq, o_x, lse_x, kv, W_xo, tm=512)
    gW_xo = tn_mm(o_x, dx2, shards=1, tn=D, tk=DW_TOKENS, name="dw_xo")
    gW_xq = tn_mm(h2, dq_x, shards=1, tn=D, tk=DW_TOKENS, name="dw_xq")
    dx1, dg_xat = mm_nt_norm_bwd(dq_x, W_xq.reshape(1, D, D), x1, g_xat, dx2, tm=512, name="dx_xq")
    gW_xkv = tn_mm(mn, dkv, shards=4, tn=2 * D, tk=mem2d.shape[0], name="dw_xkv")
    _, dg_mem = mm_nt_norm_bwd(dkv.astype(BF16), W_xkv, mem2d, g_mem, None, tm=mem2d.shape[0], name="dx_mem")
    def canon(g, i):
        R, C = big_w[i].shape
        return g.reshape(4, 2, R // 2, C)

    def partials(gs, recv, idx):
        return [chip_partial(g, r, core, "chip_partial_" + big_names[i]) for g, r, i in zip(gs, recv, idx)]

    early_g = [canon(g, i) for g, i in zip([gW_xq, gW_xkv, gW_xo, gW_fi, gW_fo], range(4, 9))]
    da, db, dgab, dy_att, dy_sg, *early_r = merge_bwd(dx1, W_out, a_br, b_br, z, W_ba, W_bs, tm=ROW_TILE,
                                                      hook=split_hook(early_g, S // ROW_TILE))
    gW_out = tn_mm(merged, dx1, shards=1, tn=D, tk=DW_TOKENS, name="dw_out")
    gW_ba = tn_mm(y_att, da, shards=4, tn=D, tk=DW_TOKENS, name="dw_branch_att")
    gW_bs = tn_mm(y_sg, db, shards=4, tn=D, tk=DW_TOKENS, name="dw_branch_sg")
    mid_g = [canon(g, i) for g, i in zip([gW_ba, gW_bs, gW_out], range(1, 4))]
    parts_b = (partials(mid_g, sibling_split(mid_g, "sibling_split_mid"), range(1, 4))
               + partials(early_g, early_r, range(4, 9)))
    nt_bwd = S // ATT_TILE + ATT_WIN - 1
    dqkv, dbias, *from_chips_b = attn_bwd(z, y_att, dy_att, lse_att, bias, exchange_hook(parts_b, nt_bwd))
    duv, dwpair, dsgb_full, dlng, dlnb = sgu_bwd(z, dy_sg, amean, lng, lnb, wpair, wtpair, bfull, maskpair, tm=512)
    dwm = jnp.transpose(dwpair.reshape(SG_PAIRS, SG_BLOCK, 2, SG_BLOCK), (0, 2, 1, 3))
    dsgb = dsgb_full[:, ::SG_GROUP_DIM].T
    dz = [dqkv, duv, dgab]
    gW_in = tn_mm_pieces(h1, dz, shards=4, per=2, tk=DW_TOKENS, name="dw_in")
    in_g = [canon(gW_in, 0)]
    parts_a = partials(in_g, sibling_split(in_g, "sibling_split_w_in"), [0])
    d_rel = _rel_bias_grad(dbias)
    small_g = [d_rel, dlng, dlnb, dwm, dsgb, dg_xat, dg_mem, dg_ffn, dg_fin, loss_vec[0, :1]]
    n_dx = S // ROW_TILE
    dx, dg_mix, from_chips_a, small_all = mm_nt_norm_bwd(
        dz, W_in, x2d, g_mix, dx1, tm=ROW_TILE, name="dx_in",
        hook=merge_hooks(exchange_hook(parts_a, n_dx), allgather_hook(_pack_small(small_g), n_dx)))
    from_chips_a = [from_chips_a]

    parts, from_chips = parts_a + parts_b, from_chips_a + from_chips_b
    sums = [shard_sum(p, r, own, core, "shard_sum_" + nm) for p, r, nm in zip(parts, from_chips, big_names)]
    joined = sibling_join(sums)
    big_out = []
    for j, w, m_, v_, nm in zip(joined, big_w, big_m, big_v, big_names):
        g = j.reshape(w.shape)
        big_out.append((g,) + tuple(adamw(w, g, m_, v_, "adamw_" + nm)))

    small_w = [norm_mix_g, rel_bias, sg_ln_g, sg_ln_b, sg_w, sg_b, norm_xattn_g, norm_mem_g, norm_ffn_g, norm_final_g]
    small_m = [m_norm_mix_g, m_rel_bias, m_sg_ln_g, m_sg_ln_b, m_sg_w, m_sg_b, m_norm_xattn_g, m_norm_mem_g, m_norm_ffn_g, m_norm_final_g]
    small_v = [v_norm_mix_g, v_rel_bias, v_sg_ln_g, v_sg_ln_b, v_sg_w, v_sg_b, v_norm_xattn_g, v_norm_mem_g, v_norm_ffn_g, v_norm_final_g]
    shapes = [w.shape for w in small_w]
    g_sum = jnp.concatenate([allreduce_small(_pack_small([dg_mix])), sum_blocks(small_all)], axis=0)
    zero = jnp.zeros((1,), F32)
    d_s, m_s, v_s = adamw(_pack_small(small_w + [zero]), g_sum, _pack_small(small_m + [zero]), _pack_small(small_v + [zero]),
                          "adamw_small")
    sg_, sd, sm, sv_ = (_unpack_small(p, shapes + [(1,)]) for p in (g_sum, d_s, m_s, v_s))
    loss = sg_[-1][0]

    order = ["norm_mix_g", "w_in", "rel_bias", "sg_ln_g", "sg_ln_b", "sg_w", "sg_b", "w_branch_att", "w_branch_sg", "w_out",
             "norm_xattn_g", "norm_mem_g", "w_xq", "w_xkv", "w_xo", "norm_ffn_g", "w_ffn_in", "w_ffn_out", "norm_final_g"]
    small_names = ["norm_mix_g", "rel_bias", "sg_ln_g", "sg_ln_b", "sg_w", "sg_b", "norm_xattn_g", "norm_mem_g", "norm_ffn_g",
                   "norm_final_g"]
    res = {}
    for i, nm in enumerate(small_names):
        res[nm] = (sg_[i], sd[i], sm[i], sv_[i])
    for nm, outs in zip(big_names, big_out):
        res[nm] = tuple(o[None] for o in outs)
    return (loss, dx[None], *[res[nm][0] for nm in order], *[res[nm][1] for nm in order],
            *[res[nm][2] for nm in order], *[res[nm][3] for nm in order])
```

```python
import functools
from typing import NamedTuple

import numpy as np
import jax
import jax.numpy as jnp
from jax import lax
from jax.experimental import pallas as pl
from jax.experimental.pallas import tpu as pltpu

F32, BF16 = jnp.float32, jnp.bfloat16
MESH = pl.DeviceIdType.MESH

EPS = 1e-6
NEG_INF = -1e30
CHUNK = 64
N_PREV_CHUNKS = 8
ATT_HEADS, ATT_HEAD_DIM = 8, 64
REL_CLIP = 128
SG_BLOCK, SG_GROUPS, SG_GROUP_DIM = 128, 8, 64
XATT_HEADS, XATT_HEAD_DIM = 4, 256
ATT_TILE = 128
ATT_WIN = 5
ADAM_LR, ADAM_B1, ADAM_B2, ADAM_EPS, ADAM_WD, ADAM_STEP = 0.001, 0.9, 0.999, 1e-08, 0.01, 10

VMEM_LIMIT_V7X = 56 * 1024 * 1024
EW_BLOCK_BYTES = 2 << 20
ROW_TILE = 512
DW_TOKENS = 1024
HBM = pl.BlockSpec(memory_space=pl.ANY)


def _params(*sem):
    return pltpu.CompilerParams(dimension_semantics=sem, vmem_limit_bytes=VMEM_LIMIT_V7X)


def _full(shape):
    n = len(shape)
    return pl.BlockSpec(shape, lambda *_: (0,) * n, pipeline_mode=pl.Buffered(1))


def _acc(shape):
    n = len(shape)
    return pl.BlockSpec(shape, lambda *_: (0,) * n)


class Hook(NamedTuple):
    ins: tuple = ()
    out_shapes: tuple = ()
    alias: tuple = ()
    sems: tuple = ()
    steps: tuple = ()


def _run_hook(hook, pos, h_in, h_out, h_sems):
    for at, where, fn in hook.steps:
        if where == pos:
            pl.when(pl.program_id(0) == at)(functools.partial(fn, h_in, h_out, h_sems))


def _hooked_call(body, hook, *, n_in, n_out, in_specs, out_specs, out_shape, scratch_shapes=(), **kw):
    nh = len(hook.ins)
    aliases = {n_in + i: n_out + o for i, o in hook.alias}
    return pl.pallas_call(
        body, in_specs=list(in_specs) + [HBM] * nh, out_specs=list(out_specs) + [HBM] * len(hook.out_shapes),
        out_shape=list(out_shape) + list(hook.out_shapes), scratch_shapes=list(scratch_shapes) + list(hook.sems),
        input_output_aliases=aliases, **kw)


def _nt(a, b):
    return lax.dot_general(a, b, (((1,), (1,)), ((), ())), preferred_element_type=F32)


def _tn(a, b):
    return lax.dot_general(a, b, (((0,), (0,)), ((), ())), preferred_element_type=F32)


def _nn(a, b):
    return jnp.dot(a, b, preferred_element_type=F32)


def _sigmoid(x):
    return 0.5 * jnp.tanh(0.5 * x) + 0.5


_GELU_C = float(np.sqrt(2.0 / np.pi))


def _gelu(x):
    t = jnp.tanh(_GELU_C * (x + 0.044715 * (x * x * x)))
    return x * (0.5 * (1.0 + t))


def _gelu_grad(x):
    t = jnp.tanh(_GELU_C * (x + 0.044715 * (x * x * x)))
    return 0.5 * (1.0 + t) + 0.5 * x * (1.0 - t * t) * (_GELU_C * (1.0 + 3.0 * 0.044715 * x * x))


def _rms_stats(xf):
    rstd = lax.rsqrt(jnp.mean(xf * xf, axis=-1, keepdims=True) + EPS)
    return rstd, xf * rstd


def _rms_bwd(xhat, rstd, g, dh):
    dxh = dh * g
    dx = rstd * (dxh - xhat * jnp.mean(dxh * xhat, axis=-1, keepdims=True))
    return dx, dh * xhat


def norm_mm(x, g, w, *, tm, scale, name, hook=Hook()):
    S, D = x.shape
    J, _, C = w.shape
    nh, nho = len(hook.ins), len(hook.out_shapes)

    def body(x_ref, g_ref, w_ref, *rest):
        h_in, (h_ref, z_ref), h_out, h_sems = rest[:nh], rest[nh:nh + 2], rest[nh + 2:nh + 2 + nho], rest[nh + 2 + nho:]
        _run_hook(hook, "before", h_in, h_out, h_sems)
        _, xhat = _rms_stats(x_ref[...])
        h = (xhat * g_ref[...]).astype(BF16)
        h_ref[...] = h
        for j in range(J):
            acc = _nn(h, w_ref[j])
            if scale is not None:
                acc = acc * scale
            z_ref[:, j * C:(j + 1) * C] = acc.astype(BF16)
        _run_hook(hook, "after", h_in, h_out, h_sems)

    return _hooked_call(
        body, hook, n_in=3, n_out=2, grid=(S // tm,), name=name,
        in_specs=[pl.BlockSpec((tm, D), lambda i: (i, 0)), _full((1, D)), _full((J, D, C))],
        out_specs=[pl.BlockSpec((tm, D), lambda i: (i, 0)), pl.BlockSpec((tm, J * C), lambda i: (i, 0))],
        out_shape=[jax.ShapeDtypeStruct((S, D), BF16), jax.ShapeDtypeStruct((S, J * C), BF16)],
        compiler_params=_params("arbitrary"),
    )(x, g, w, *hook.ins)


def _att_window_specs(nt, col):
    return [pl.BlockSpec((ATT_TILE, 512), lambda t, j=j: (jnp.clip(t - (ATT_WIN - 1) + j, 0, nt - 1), col))
            for j in range(ATT_WIN)]


ATT_SCALE = ATT_HEAD_DIM ** -0.5
ATT_FAR = 3 * ATT_TILE


def _att_scores(q_scaled, k, bias, valid):
    return jnp.where(valid, _nt(q_scaled, k) + bias, NEG_INF)


def _att_valid(t):
    kpos = lax.broadcasted_iota(jnp.int32, (ATT_TILE, ATT_WIN * ATT_TILE), 1) + (t - (ATT_WIN - 1)) * ATT_TILE
    return kpos >= 0


def attn_fwd(z, bias, hook=Hook()):
    S = z.shape[0]
    nt = S // ATT_TILE
    n_in, nh, nho = 2 + 2 * ATT_WIN, len(hook.ins), len(hook.out_shapes)

    def body(q_ref, *refs):
        k_refs, v_refs = refs[:ATT_WIN], refs[ATT_WIN:2 * ATT_WIN]
        bias_ref = refs[2 * ATT_WIN]
        rest = refs[2 * ATT_WIN + 1:]
        h_in, (y_ref, lse_ref), h_out = rest[:nh], rest[nh:nh + 2], rest[nh + 2:nh + 2 + nho]
        s_scr, p_scr = rest[nh + 2 + nho:nh + 4 + nho]
        h_sems = rest[nh + 4 + nho:]
        _run_hook(hook, "before", h_in, h_out, h_sems)
        valid = _att_valid(pl.program_id(0))
        heads = [slice(h * ATT_HEAD_DIM, (h + 1) * ATT_HEAD_DIM) for h in range(ATT_HEADS)]
        for h, sl in enumerate(heads):
            k = jnp.concatenate([r[:, sl] for r in k_refs], axis=0)
            s_scr[h] = _att_scores(q_ref[:, sl] * ATT_SCALE, k, bias_ref[h], valid)
        stats = []
        for h in range(ATT_HEADS):
            s = s_scr[h]
            m = jnp.max(s, axis=-1, keepdims=True)
            p = jnp.exp(s - m)
            stats.append((m, jnp.sum(p, axis=-1, keepdims=True)))
            p_scr[h] = p.astype(BF16)
        for h, sl in enumerate(heads):
            m, l = stats[h]
            v = jnp.concatenate([r[:, sl] for r in v_refs], axis=0)
            y_ref[:, sl] = (_nn(p_scr[h], v) / l).astype(BF16)
            lse_ref[:, h:h + 1] = m + jnp.log(l)
        _run_hook(hook, "after", h_in, h_out, h_sems)

    tile = lambda col: pl.BlockSpec((ATT_TILE, 512), lambda t: (t, col))
    return _hooked_call(
        body, hook, n_in=n_in, n_out=2, grid=(nt,), name="attn_fwd",
        in_specs=[tile(0)] + _att_window_specs(nt, 1) + _att_window_specs(nt, 2) + [_full(bias.shape)],
        out_specs=[tile(0), pl.BlockSpec((ATT_TILE, ATT_HEADS), lambda t: (t, 0))],
        out_shape=[jax.ShapeDtypeStruct((S, 512), BF16), jax.ShapeDtypeStruct((S, ATT_HEADS), F32)],
        scratch_shapes=[pltpu.VMEM((ATT_HEADS, ATT_TILE, ATT_WIN * ATT_TILE), F32),
                        pltpu.VMEM((ATT_HEADS, ATT_TILE, ATT_WIN * ATT_TILE), BF16)],
        compiler_params=_params("arbitrary"),
    )(z, *([z] * (2 * ATT_WIN)), bias, *hook.ins)


SG_PAIRS = SG_GROUPS // 2


def _group_mean(x, a_ref):
    rows = x.shape[0]
    hi = x.astype(BF16)
    lo = (x - hi.astype(F32)).astype(BF16)
    both = _nn(jnp.concatenate([hi, lo], axis=0), a_ref[...])
    return both[:rows] + both[rows:]


def _pair_diag(x, ref):
    first = lax.broadcasted_iota(jnp.int32, x.shape, 1) < SG_GROUP_DIM
    ref[0:SG_BLOCK, :] = jnp.where(first, x, 0.0).astype(BF16)
    ref[SG_BLOCK:2 * SG_BLOCK, :] = jnp.where(first, 0.0, x).astype(BF16)


def _sgu_norm(zv_ref, a_ref, lng_ref, lnb_ref):
    v = _gelu(zv_ref[...].astype(F32))
    vc = v - _group_mean(v, a_ref)
    rstd = lax.rsqrt(_group_mean(vc * vc, a_ref) + EPS)
    xhat = vc * rstd
    return xhat, rstd, xhat * lng_ref[...] + lnb_ref[...]


def _sgu_rows(b):
    return pl.ds(pl.multiple_of(b * SG_BLOCK, SG_BLOCK), SG_BLOCK)


def sgu_fwd(z, amean, lng, lnb, wpair, bfull, *, tm, hook=Hook()):
    S = z.shape[0]
    nh, nho = len(hook.ins), len(hook.out_shapes)

    def body(zu_ref, zv_ref, a_ref, lng_ref, lnb_ref, wpair_ref, bfull_ref, *rest):
        h_in, y_ref, h_out = rest[:nh], rest[nh], rest[nh + 1:nh + 1 + nho]
        vn_scr, vbd = rest[nh + 1 + nho:nh + 3 + nho]
        h_sems = rest[nh + 3 + nho:]
        _run_hook(hook, "before", h_in, h_out, h_sems)
        vn_scr[...] = _sgu_norm(zv_ref, a_ref, lng_ref, lnb_ref)[2]

        def block(b, carry):
            rows = _sgu_rows(b)
            u = _gelu(zu_ref[rows, :].astype(F32))
            for p in range(SG_PAIRS):
                lanes = slice(p * 128, (p + 1) * 128)
                _pair_diag(vn_scr[rows, lanes], vbd.at[p])
                sv = _nn(wpair_ref[p], vbd[p]) + bfull_ref[:, lanes]
                y_ref[rows, lanes] = (u[:, lanes] * sv).astype(BF16)
            return carry

        lax.fori_loop(0, tm // SG_BLOCK, block, 0)
        _run_hook(hook, "after", h_in, h_out, h_sems)

    tile = lambda col: pl.BlockSpec((tm, 512), lambda i: (i, col))
    smalls = [amean, lng, lnb, wpair, bfull]
    return _hooked_call(
        body, hook, n_in=2 + len(smalls), n_out=1, grid=(S // tm,), name="sgu_fwd",
        in_specs=[tile(3), tile(4)] + [_full(a.shape) for a in smalls],
        out_specs=[tile(0)], out_shape=[jax.ShapeDtypeStruct((S, 512), BF16)],
        scratch_shapes=[pltpu.VMEM((tm, 512), F32), pltpu.VMEM((SG_PAIRS, 2 * SG_BLOCK, 128), BF16)],
        compiler_params=_params("arbitrary"),
    )(z, z, *smalls, *hook.ins)


def _gate_specs(tm):
    return [pl.BlockSpec((tm, 512), lambda i, c=c: (i, c)) for c in (5, 6, 7, 8)]


def merge_fwd(y_att, y_sg, z, x, w_ba, w_bs, w_out, *, tm):
    S, D = x.shape
    J, _, C = w_ba.shape

    def body(ya_ref, ys_ref, g0, g1, g2, g3, x_ref, wba_ref, wbs_ref, wo_ref, a_ref, b_ref, m_ref, x1_ref):
        ya, ys = ya_ref[...], ys_ref[...]
        a = jnp.concatenate([_nn(ya, wba_ref[j]) for j in range(J)], axis=1)
        b = jnp.concatenate([_nn(ys, wbs_ref[j]) for j in range(J)], axis=1)
        sa = _sigmoid(jnp.concatenate([g0[...], g1[...]], axis=1).astype(F32))
        sb = _sigmoid(jnp.concatenate([g2[...], g3[...]], axis=1).astype(F32))
        a_ref[...] = a.astype(BF16)
        b_ref[...] = b.astype(BF16)
        merged = (sa * a + sb * b).astype(BF16)
        m_ref[...] = merged
        x1_ref[...] = x_ref[...] + _nn(merged, wo_ref[...])

    row = lambda n: pl.BlockSpec((tm, n), lambda i: (i, 0))
    return pl.pallas_call(
        body, grid=(S // tm,), name="merge_fwd",
        in_specs=[row(512), row(512)] + _gate_specs(tm) + [row(D), _full(w_ba.shape), _full(w_bs.shape), _full(w_out.shape)],
        out_specs=[row(D)] * 4,
        out_shape=[jax.ShapeDtypeStruct((S, D), BF16)] * 3 + [jax.ShapeDtypeStruct((S, D), F32)],
        compiler_params=_params("parallel"),
    )(y_att, y_sg, z, z, z, z, x, w_ba, w_bs, w_out)


def xattn_fwd(xq, kv, w_xo, x1, *, tm):
    S, D = xq.shape
    dh = XATT_HEAD_DIM

    def body(q_ref, kv_ref, wo_ref, x1_ref, o_ref, lse_ref, x2_ref):
        outs = []
        for h in range(XATT_HEADS):
            s = _nt(q_ref[:, h * dh:(h + 1) * dh], kv_ref[:, h * dh:(h + 1) * dh])
            m = jnp.max(s, axis=-1, keepdims=True)
            p = jnp.exp(s - m)
            l = jnp.sum(p, axis=-1, keepdims=True)
            outs.append((_nn(p.astype(BF16), kv_ref[:, D + h * dh:D + (h + 1) * dh]) / l).astype(BF16))
            lse_ref[:, h:h + 1] = m + jnp.log(l)
        o = jnp.concatenate(outs, axis=1)
        o_ref[...] = o
        x2_ref[...] = x1_ref[...] + _nn(o, wo_ref[...])

    row = lambda n: pl.BlockSpec((tm, n), lambda i: (i, 0))
    return pl.pallas_call(
        body, grid=(S // tm,), name="xattn_fwd",
        in_specs=[row(D), _full(kv.shape), _full(w_xo.shape), row(D)],
        out_specs=[row(D), row(XATT_HEADS), row(D)],
        out_shape=[jax.ShapeDtypeStruct((S, D), BF16), jax.ShapeDtypeStruct((S, XATT_HEADS), F32),
                   jax.ShapeDtypeStruct((S, D), F32)],
        compiler_params=_params("parallel"),
    )(xq, kv, w_xo, x1)


def ffn_out_loss(gu, w, x2, g, target, *, tm):
    S, D = x2.shape
    F = w.shape[0]

    def body(gu_ref, w_ref, x2_ref, g_ref, t_ref, act_ref, loss_ref, dx_ref, dg_ref):
        @pl.when(pl.program_id(0) == 0)
        def _():
            loss_ref[...] = jnp.zeros_like(loss_ref)
            dg_ref[...] = jnp.zeros_like(dg_ref)

        half = gu_ref[:, :F] * 0.5
        act = (half * gu_ref[:, F:]) * (jnp.tanh(half) + 1.0)
        act_ref[...] = act
        gv = g_ref[...]
        rstd, xhat = _rms_stats(x2_ref[...] + _nn(act, w_ref[...]))
        err = xhat * gv - t_ref[...]
        loss_ref[...] += 0.5 * jnp.sum(jnp.mean(err * err, axis=-1, keepdims=True))
        dx, dgc = _rms_bwd(xhat, rstd, gv, err * (1.0 / D))
        dx_ref[...] = dx
        dg_ref[...] += jnp.sum(dgc, axis=0, keepdims=True)

    row = lambda n: pl.BlockSpec((tm, n), lambda i: (i, 0))
    return pl.pallas_call(
        body, grid=(S // tm,), name="ffn_out_loss",
        in_specs=[row(2 * F), _full(w.shape), row(D), _full((1, D)), row(D)],
        out_specs=[row(F), _acc((1, 128)), row(D), _acc((1, D))],
        out_shape=[jax.ShapeDtypeStruct((S, F), BF16), jax.ShapeDtypeStruct((1, 128), F32),
                   jax.ShapeDtypeStruct((S, D), F32), jax.ShapeDtypeStruct((1, D), F32)],
        compiler_params=_params("arbitrary"),
    )(gu, w, x2, g, target)


def ffn_act_bwd(dx3, gu, w, *, tm, nchunk):
    S, D = dx3.shape
    F = w.shape[0]
    cn = F // nchunk

    def body(dx_ref, gu_ref, w_ref, dgu_ref):
        dxb = dx_ref[...].astype(BF16)
        for j in range(nchunk):
            dact = _nt(dxb, w_ref[j * cn:(j + 1) * cn, :])
            gate = gu_ref[:, j * cn:(j + 1) * cn]
            up = gu_ref[:, F + j * cn:F + (j + 1) * cn]
            sg = _sigmoid(gate)
            silu = gate * sg
            dsilu = sg + silu * (1.0 - sg)
            dgu_ref[:, j * cn:(j + 1) * cn] = (dact * (up * dsilu).astype(F32)).astype(BF16)
            dgu_ref[:, F + j * cn:F + (j + 1) * cn] = (dact * silu.astype(F32)).astype(BF16)

    row = lambda n: pl.BlockSpec((tm, n), lambda i: (i, 0))
    return pl.pallas_call(
        body, grid=(S // tm,), name="ffn_act_bwd",
        in_specs=[row(D), row(2 * F), _full(w.shape)], out_specs=row(2 * F),
        out_shape=jax.ShapeDtypeStruct((S, 2 * F), BF16), compiler_params=_params("parallel"),
    )(dx3, gu, w)


def _overlaps(widths, lo, hi):
    out, off = [], 0
    for p, wd in enumerate(widths):
        a, b = max(lo, off), min(hi, off + wd)
        if a < b:
            out.append((p, a - off, b - off, a - lo))
        off += wd
    return out


def tn_mm_pieces(a, pieces, *, shards, per, tk, name):
    K, M = a.shape
    widths = [p.shape[1] for p in pieces]
    C = sum(widths) // shards
    tn = per * C
    nk = K // tk
    n_tiles = shards // per

    def body(a_ref, *refs):
        p_refs, o_ref, acc_ref = refs[:len(pieces)], refs[len(pieces)], refs[len(pieces) + 1]
        n, k = pl.program_id(0), pl.program_id(1)

        @pl.when(k == 0)
        def _():
            acc_ref[...] = jnp.zeros_like(acc_ref)

        av = a_ref[...]
        for tile in range(n_tiles):
            @pl.when(n == tile)
            def _(tile=tile):
                for p, c0, c1, at in _overlaps(widths, tile * tn, (tile + 1) * tn):
                    acc_ref[:, at:at + c1 - c0] += _tn(av, p_refs[p][:, c0:c1])

        @pl.when(k == nk - 1)
        def _():
            for s in range(per):
                o_ref[s] = acc_ref[:, s * C:(s + 1) * C].astype(BF16)

    return pl.pallas_call(
        body, grid=(n_tiles, nk), name=name,
        in_specs=[pl.BlockSpec((tk, M), lambda n, k: (k, 0))] + [pl.BlockSpec((tk, wd), lambda n, k: (k, 0)) for wd in widths],
        out_specs=pl.BlockSpec((per, M, C), lambda n, k: (n, 0, 0)), out_shape=jax.ShapeDtypeStruct((shards, M, C), BF16),
        scratch_shapes=[pltpu.VMEM((M, tn), F32)], compiler_params=_params("parallel", "arbitrary"),
    )(a, *pieces)


def tn_mm(a, b, *, shards, tn, tk, name):
    K, M = a.shape
    N = b.shape[1]
    C = N // shards
    per = tn // C
    nk = K // tk

    def body(a_ref, b_ref, o_ref, acc_ref):
        k = pl.program_id(1)

        @pl.when(k == 0)
        def _():
            acc_ref[...] = jnp.zeros_like(acc_ref)

        acc_ref[...] += _tn(a_ref[...].astype(BF16), b_ref[...].astype(BF16))

        @pl.when(k == nk - 1)
        def _():
            if shards > 1:
                for s in range(per):
                    o_ref[s] = acc_ref[:, s * C:(s + 1) * C].astype(BF16)
            else:
                o_ref[...] = acc_ref[...].astype(BF16)

    if shards > 1:
        out_spec = pl.BlockSpec((per, M, C), lambda n, k: (n, 0, 0))
        out_shape = jax.ShapeDtypeStruct((shards, M, C), BF16)
    else:
        out_spec = pl.BlockSpec((M, tn), lambda n, k: (0, n))
        out_shape = jax.ShapeDtypeStruct((M, N), BF16)
    return pl.pallas_call(
        body, grid=(N // tn, nk), name=name,
        in_specs=[pl.BlockSpec((tk, M), lambda n, k: (k, 0)), pl.BlockSpec((tk, tn), lambda n, k: (k, n))],
        out_specs=out_spec, out_shape=out_shape, scratch_shapes=[pltpu.VMEM((M, tn), F32)],
        compiler_params=_params("parallel", "arbitrary"),
    )(a, b)


def mm_nt_norm_bwd(dy, w, x, g, dx_in, *, tm, name, hook=Hook()):
    S, D = x.shape
    J, _, C = w.shape
    has_in = dx_in is not None
    dys = list(dy) if isinstance(dy, (list, tuple)) else [dy]
    widths = [d.shape[1] for d in dys]
    nd = len(dys)
    n_in, nh, nho = nd + 3 + has_in, len(hook.ins), len(hook.out_shapes)

    def body(*refs):
        dy_refs = refs[:nd]
        w_ref, x_ref, g_ref = refs[nd:nd + 3]
        dxin_ref = refs[nd + 3] if has_in else None
        rest = refs[n_in:]
        h_in, (dx_ref, dg_ref), h_out, h_sems = rest[:nh], rest[nh:nh + 2], rest[nh + 2:nh + 2 + nho], rest[nh + 2 + nho:]
        _run_hook(hook, "before", h_in, h_out, h_sems)

        @pl.when(pl.program_id(0) == 0)
        def _():
            dg_ref[...] = jnp.zeros_like(dg_ref)

        dh = None
        for j in range(J):
            for p, c0, c1, at in _overlaps(widths, j * C, (j + 1) * C):
                part = _nt(dy_refs[p][:, c0:c1], w_ref[j, :, at:at + c1 - c0])
                dh = part if dh is None else dh + part
        rstd, xhat = _rms_stats(x_ref[...])
        dx, dgc = _rms_bwd(xhat, rstd, g_ref[...], dh)
        dx_ref[...] = dx + dxin_ref[...] if has_in else dx
        dg_ref[...] += jnp.sum(dgc, axis=0, keepdims=True)
        _run_hook(hook, "after", h_in, h_out, h_sems)

    row = lambda n: pl.BlockSpec((tm, n), lambda i: (i, 0))
    ins = dys + [w, x, g] + ([dx_in] if has_in else [])
    return _hooked_call(
        body, hook, n_in=n_in, n_out=2, grid=(S // tm,), name=name,
        in_specs=[row(wd) for wd in widths] + [_full(w.shape), row(D), _full((1, D))] + ([row(D)] if has_in else []),
        out_specs=[row(D), _acc((1, D))],
        out_shape=[jax.ShapeDtypeStruct((S, D), F32), jax.ShapeDtypeStruct((1, D), F32)],
        compiler_params=_params("arbitrary"),
    )(*ins, *hook.ins)


def xattn_bwd(dx2, xq, o, lse, kv, w_xo, *, tm):
    S, D = xq.shape
    M = kv.shape[0]
    dh = XATT_HEAD_DIM

    def body(dx_ref, q_ref, o_ref, lse_ref, kv_ref, wo_ref, dq_ref, dkv_ref):
        @pl.when(pl.program_id(0) == 0)
        def _():
            dkv_ref[...] = jnp.zeros_like(dkv_ref)

        do = _nt(dx_ref[...].astype(BF16), wo_ref[...])
        for h in range(XATT_HEADS):
            hs = slice(h * dh, (h + 1) * dh)
            vs = slice(D + h * dh, D + (h + 1) * dh)
            q, k, v = q_ref[:, hs], kv_ref[:, hs], kv_ref[:, vs]
            do_h = do[:, hs]
            do_b = do_h.astype(BF16)
            p = jnp.exp(_nt(q, k) - lse_ref[:, h:h + 1])
            delta = jnp.sum(do_h * o_ref[:, hs].astype(F32), axis=-1, keepdims=True)
            ds = (p * (_nt(do_b, v) - delta)).astype(BF16)
            dq_ref[:, hs] = (_nn(ds, k) * (dh ** -0.5)).astype(BF16)
            dkv_ref[:, hs] += _tn(ds, q)
            dkv_ref[:, vs] += _tn(p.astype(BF16), do_b)

    row = lambda n: pl.BlockSpec((tm, n), lambda i: (i, 0))
    return pl.pallas_call(
        body, grid=(S // tm,), name="xattn_bwd",
        in_specs=[row(D), row(D), row(D), row(XATT_HEADS), _full(kv.shape), _full(w_xo.shape)],
        out_specs=[row(D), _acc((M, 2 * D))],
        out_shape=[jax.ShapeDtypeStruct((S, D), BF16), jax.ShapeDtypeStruct((M, 2 * D), F32)],
        compiler_params=_params("arbitrary"),
    )(dx2, xq, o, lse, kv, w_xo)


def merge_bwd(dx1, w_out, a, b, z, w_ba, w_bs, *, tm, hook=Hook()):
    S, D = dx1.shape
    J, W, C = w_ba.shape
    nh, nho = len(hook.ins), len(hook.out_shapes)

    def body(dx_ref, wo_ref, a_ref, b_ref, g0, g1, g2, g3, wba_ref, wbs_ref, *rest):
        h_in, (da_ref, db_ref, dg_ref, dya_ref, dys_ref) = rest[:nh], rest[nh:nh + 5]
        h_out, h_sems = rest[nh + 5:nh + 5 + nho], rest[nh + 5 + nho:]
        _run_hook(hook, "before", h_in, h_out, h_sems)
        dm = _nt(dx_ref[...].astype(BF16), wo_ref[...])
        sa = _sigmoid(jnp.concatenate([g0[...], g1[...]], axis=1))
        sb = _sigmoid(jnp.concatenate([g2[...], g3[...]], axis=1))
        dg_ref[:, :D] = (dm * (a_ref[...] * (sa * (1.0 - sa))).astype(F32)).astype(BF16)
        dg_ref[:, D:] = (dm * (b_ref[...] * (sb * (1.0 - sb))).astype(F32)).astype(BF16)
        da = (dm * sa.astype(F32)).astype(BF16)
        db = (dm * sb.astype(F32)).astype(BF16)
        da_ref[...] = da
        db_ref[...] = db
        dya = _nt(da[:, 0:C], wba_ref[0])
        dys = _nt(db[:, 0:C], wbs_ref[0])
        for j in range(1, J):
            dya += _nt(da[:, j * C:(j + 1) * C], wba_ref[j])
            dys += _nt(db[:, j * C:(j + 1) * C], wbs_ref[j])
        dya_ref[...] = dya.astype(BF16)
        dys_ref[...] = dys.astype(BF16)
        _run_hook(hook, "after", h_in, h_out, h_sems)

    row = lambda n: pl.BlockSpec((tm, n), lambda i: (i, 0))
    return _hooked_call(
        body, hook, n_in=10, n_out=5, grid=(S // tm,), name="merge_bwd",
        in_specs=[row(D), _full(w_out.shape), row(D), row(D)] + _gate_specs(tm) + [_full(w_ba.shape), _full(w_bs.shape)],
        out_specs=[row(D), row(D), row(2 * D), row(W), row(W)],
        out_shape=[jax.ShapeDtypeStruct((S, D), BF16)] * 2 + [jax.ShapeDtypeStruct((S, 2 * D), BF16)]
        + [jax.ShapeDtypeStruct((S, W), BF16)] * 2,
        compiler_params=_params("arbitrary"),
    )(dx1, w_out, a, b, z, z, z, z, w_ba, w_bs, *hook.ins)


def attn_bwd(z, y, dy, lse, bias, hook=Hook()):
    S = z.shape[0]
    nt = S // ATT_TILE
    back = ATT_WIN - 1
    n_in, nh, nho = 5 + 2 * ATT_WIN, len(hook.ins), len(hook.out_shapes)

    def body(q_ref, *refs):
        k_refs, v_refs = refs[:ATT_WIN], refs[ATT_WIN:2 * ATT_WIN]
        y_ref, dy_ref, lse_ref, bias_ref = refs[2 * ATT_WIN:2 * ATT_WIN + 4]
        rest = refs[2 * ATT_WIN + 4:]
        h_in, (dqkv_ref, dbias_ref), h_out = rest[:nh], rest[nh:nh + 2], rest[nh + 2:nh + 2 + nho]
        dk_acc, dv_acc, dq_buf = rest[nh + 2 + nho:nh + 5 + nho]
        h_sems = rest[nh + 5 + nho:]
        t = pl.program_id(0)
        _run_hook(hook, "before", h_in, h_out, h_sems)

        @pl.when(t == 0)
        def _():
            dbias_ref[...] = jnp.zeros_like(dbias_ref)
            dk_acc[...] = jnp.zeros_like(dk_acc)
            dv_acc[...] = jnp.zeros_like(dv_acc)
            dq_buf[...] = jnp.zeros_like(dq_buf)

        @pl.when(t < nt)
        def _():
            valid = _att_valid(t)
            q_t = (q_ref[...] * ATT_SCALE).astype(F32).T.astype(BF16)
            do_t = dy_ref[...].astype(F32).T.astype(BF16)
            for h in range(ATT_HEADS):
                sl = slice(h * ATT_HEAD_DIM, (h + 1) * ATT_HEAD_DIM)
                k = jnp.concatenate([r[:, sl] for r in k_refs], axis=0)
                v = jnp.concatenate([r[:, sl] for r in v_refs], axis=0)
                q, do_b = q_ref[:, sl] * ATT_SCALE, dy_ref[:, sl]
                p = jnp.exp(_att_scores(q, k, bias_ref[h], valid) - lse_ref[:, h:h + 1])
                delta = jnp.sum(do_b.astype(F32) * y_ref[:, sl].astype(F32), axis=-1, keepdims=True)
                ds = p * (_nt(do_b, v) - delta)
                dbias_ref[h, :, 0:ATT_TILE] += ds[:, 0:ATT_TILE] + ds[:, ATT_TILE:2 * ATT_TILE] + ds[:, 2 * ATT_TILE:ATT_FAR]
                dbias_ref[h, :, ATT_TILE:] += ds[:, ATT_FAR:]
                ds_b = ds.astype(BF16)
                dq_buf[t % ATT_WIN, :, sl] = (_nn(ds_b, k) * ATT_SCALE).astype(BF16)
                dk_w = _nn(q_t[sl, :], ds_b)
                dv_w = _nn(do_t[sl, :], p.astype(BF16))
                for i in range(ATT_WIN):
                    slot = (t + 1 + i) % ATT_WIN
                    cols = slice(i * ATT_TILE, (i + 1) * ATT_TILE)
                    if i == back:
                        dk_acc[slot, sl, :] = dk_w[:, cols]
                        dv_acc[slot, sl, :] = dv_w[:, cols]
                    else:
                        dk_acc[slot, sl, :] += dk_w[:, cols]
                        dv_acc[slot, sl, :] += dv_w[:, cols]

        done = (t + 1) % ATT_WIN
        dqkv_ref[:, 0:512] = dq_buf[done]
        dqkv_ref[:, 512:1024] = dk_acc[done].T.astype(BF16)
        dqkv_ref[:, 1024:1536] = dv_acc[done].T.astype(BF16)
        _run_hook(hook, "after", h_in, h_out, h_sems)

    last = nt - 1
    dbias_shape = (ATT_HEADS, ATT_TILE, ATT_TILE + ATT_WIN * ATT_TILE - ATT_FAR)
    tile = lambda col, n=512: pl.BlockSpec((ATT_TILE, n), lambda t: (jnp.minimum(t, last), col))
    late = pl.BlockSpec((ATT_TILE, 1536), lambda t: (jnp.maximum(t - back, 0), 0))
    return _hooked_call(
        body, hook, n_in=n_in, n_out=2, grid=(nt + back,), name="attn_bwd",
        in_specs=[tile(0)] + _att_window_specs(nt, 1) + _att_window_specs(nt, 2)
        + [tile(0), tile(0), tile(0, ATT_HEADS), _full(bias.shape)],
        out_specs=[late, _acc(dbias_shape)],
        out_shape=[jax.ShapeDtypeStruct((S, 1536), BF16), jax.ShapeDtypeStruct(dbias_shape, F32)],
        scratch_shapes=[pltpu.VMEM((ATT_WIN, 512, ATT_TILE), F32)] * 2 + [pltpu.VMEM((ATT_WIN, ATT_TILE, 512), BF16)],
        compiler_params=_params("arbitrary"),
    )(z, *([z] * (2 * ATT_WIN)), y, dy, lse, bias, *hook.ins)


def sgu_bwd(z, dy, amean, lng, lnb, wpair, wtpair, bfull, maskpair, *, tm, hook=Hook()):
    S = z.shape[0]
    n = S // tm
    nh, nho = len(hook.ins), len(hook.out_shapes)

    def body(zu_ref, zv_ref, dy_ref, a_ref, lng_ref, lnb_ref, wpair_ref, wtpair_ref, bfull_ref, mask_ref, *rest):
        h_in, (duv_ref, dw_ref, dsgb_ref, dlng_ref, dlnb_ref), h_out = rest[:nh], rest[nh:nh + 5], rest[nh + 5:nh + 5 + nho]
        (xhat_scr, rstd_scr, vn_scr, dxh_scr, vbd, dbd, w_acc, b_acc, g_acc, s_acc) = rest[nh + 5 + nho:nh + 15 + nho]
        h_sems = rest[nh + 15 + nho:]
        i = pl.program_id(0)
        _run_hook(hook, "before", h_in, h_out, h_sems)

        @pl.when(i == 0)
        def _():
            for r in (w_acc, b_acc, g_acc, s_acc):
                r[...] = jnp.zeros_like(r)

        xhat_scr[...], rstd_scr[...], vn_scr[...] = _sgu_norm(zv_ref, a_ref, lng_ref, lnb_ref)

        def block(b, carry):
            rows = _sgu_rows(b)
            zu = zu_ref[rows, :].astype(F32)
            u, du = _gelu(zu), _gelu_grad(zu)
            dyv = dy_ref[rows, :].astype(F32)
            for p in range(SG_PAIRS):
                lanes = slice(p * 128, (p + 1) * 128)
                _pair_diag(vn_scr[rows, lanes], vbd.at[p])
                sv = _nn(wpair_ref[p], vbd[p]) + bfull_ref[:, lanes]
                duv_ref[rows, lanes] = (dyv[:, lanes] * sv * du[:, lanes]).astype(BF16)
                dsv = dyv[:, lanes] * u[:, lanes]
                b_acc[:, lanes] += dsv
                w_acc[p] += _nt(dsv.astype(BF16), vbd[p])
                _pair_diag(dsv, dbd.at[p])
                dvn = _nn(wtpair_ref[p], dbd[p])
                g_acc[:, lanes] += dvn * xhat_scr[rows, lanes]
                s_acc[:, lanes] += dvn
                dxh_scr[rows, lanes] = dvn * lng_ref[:, lanes]
            return carry

        lax.fori_loop(0, tm // SG_BLOCK, block, 0)
        dxh, xhat = dxh_scr[...], xhat_scr[...]
        dv = rstd_scr[...] * (dxh - _group_mean(dxh, a_ref) - xhat * _group_mean(dxh * xhat, a_ref))
        duv_ref[:, 512:1024] = (dv * _gelu_grad(zv_ref[...].astype(F32))).astype(BF16)

        @pl.when(i == n - 1)
        def _():
            dw_ref[...] = w_acc[...] * mask_ref[...]
            dsgb_ref[...] = _group_mean(b_acc[...], a_ref) * float(SG_GROUP_DIM)
            dlng_ref[...] = jnp.sum(g_acc[...], axis=0, keepdims=True)
            dlnb_ref[...] = jnp.sum(s_acc[...], axis=0, keepdims=True)

        _run_hook(hook, "after", h_in, h_out, h_sems)

    tile = lambda col: pl.BlockSpec((tm, 512), lambda i: (i, col))
    smalls = [amean, lng, lnb, wpair, wtpair, bfull, maskpair]
    pair_shape = (SG_PAIRS, SG_BLOCK, 2 * SG_BLOCK)
    return _hooked_call(
        body, hook, n_in=3 + len(smalls), n_out=5, grid=(n,), name="sgu_bwd",
        in_specs=[tile(3), tile(4), tile(0)] + [_full(a.shape) for a in smalls],
        out_specs=[pl.BlockSpec((tm, 1024), lambda i: (i, 0)), _acc(pair_shape), _acc((SG_BLOCK, 512)),
                   _acc((1, 512)), _acc((1, 512))],
        out_shape=[jax.ShapeDtypeStruct((S, 1024), BF16), jax.ShapeDtypeStruct(pair_shape, F32),
                   jax.ShapeDtypeStruct((SG_BLOCK, 512), F32), jax.ShapeDtypeStruct((1, 512), F32),
                   jax.ShapeDtypeStruct((1, 512), F32)],
        scratch_shapes=[pltpu.VMEM((tm, 512), F32)] * 4
        + [pltpu.VMEM((SG_PAIRS, 2 * SG_BLOCK, 128), BF16)] * 2
        + [pltpu.VMEM(pair_shape, F32)] + [pltpu.VMEM((SG_BLOCK, 512), F32)] * 3,
        compiler_params=_params("arbitrary"),
    )(z, z, dy, *smalls, *hook.ins)


def bias_colsum(p, far_part):
    H, _, L = p.shape
    near_lo = (ATT_WIN - 1) * ATT_TILE - REL_CLIP + 1
    near_hi = ATT_WIN * ATT_TILE

    def body(p_ref, f_ref, gw_ref, far_ref):
        k = lax.broadcasted_iota(jnp.int32, (1, L), 1)
        is_far = (k < near_lo) | (k >= near_hi)
        for h in range(H):
            g = jnp.sum(p_ref[h], axis=0, keepdims=True)
            gw_ref[h:h + 1, :] = g
            far_ref[h:h + 1, :] = jnp.zeros((1, 128), F32) + (jnp.sum(jnp.where(is_far, g, 0.0)) + jnp.sum(f_ref[h]))

    return pl.pallas_call(
        body, name="bias_colsum", in_specs=[_full(p.shape), _full(far_part.shape)],
        out_specs=[_acc((H, L)), _acc((H, 128))],
        out_shape=[jax.ShapeDtypeStruct((H, L), F32), jax.ShapeDtypeStruct((H, 128), F32)], grid=(1,),
        compiler_params=_params("arbitrary"),
    )(p, far_part)


def _row_tile(rows, cols):
    best = None
    for tr in range(16, rows + 1, 16):
        if rows % tr == 0 and tr * cols * 4 <= EW_BLOCK_BYTES:
            best = tr
    return best if best is not None else rows


def place_shard(w, own, name):
    R, C = w.shape
    r = R // 2
    tr = _row_tile(r, C)
    nr = r // tr

    def body(own_ref, w_ref, o_ref):
        o_ref[...] = w_ref[...].astype(BF16)

    return pl.pallas_call(
        body, name=name,
        grid_spec=pltpu.PrefetchScalarGridSpec(
            num_scalar_prefetch=1, grid=(2, nr),
            in_specs=[pl.BlockSpec((tr, C), lambda h, i, own: (h * nr + i, 0))],
            out_specs=pl.BlockSpec((None, None, tr, C), lambda h, i, own: (own[0], h, i, 0))),
        out_shape=jax.ShapeDtypeStruct((4, 2, r, C), BF16), compiler_params=_params("parallel", "parallel"),
    )(own, w)


def chip_partial(grad, recv, core, name):
    _, _, r, C = grad.shape
    tr = _row_tile(r, C)

    def body(core_ref, g_ref, r_ref, o_ref):
        o_ref[...] = (g_ref[...].astype(F32) + r_ref[...].astype(F32)).astype(BF16)

    spec = pl.BlockSpec((None, tr, C), lambda j, i, core: (j, i, 0))
    return pl.pallas_call(
        body, name=name,
        grid_spec=pltpu.PrefetchScalarGridSpec(
            num_scalar_prefetch=1, grid=(4, r // tr),
            in_specs=[pl.BlockSpec((None, None, tr, C), lambda j, i, core: (j, core[0], i, 0)), spec], out_specs=spec),
        out_shape=jax.ShapeDtypeStruct((4, r, C), BF16), compiler_params=_params("parallel", "parallel"),
    )(core, grad, recv)


def shard_sum(part, recv, own, core, name):
    _, r, C = part.shape
    tr = _row_tile(r, C)

    def body(own_ref, core_ref, p_ref, r0, r1, r2, o_ref):
        o_ref[...] = p_ref[...].astype(F32) + r0[...].astype(F32) + r1[...].astype(F32) + r2[...].astype(F32)

    return pl.pallas_call(
        body, name=name,
        grid_spec=pltpu.PrefetchScalarGridSpec(
            num_scalar_prefetch=2, grid=(r // tr,),
            in_specs=[pl.BlockSpec((None, tr, C), lambda i, own, core: (own[0], i, 0))]
            + [pl.BlockSpec((None, tr, C), lambda i, own, core, k=k: (k, i, 0)) for k in range(3)],
            out_specs=pl.BlockSpec((None, tr, C), lambda i, own, core: (core[0], i, 0))),
        out_shape=jax.ShapeDtypeStruct((2, r, C), F32), compiler_params=_params("parallel"),
    )(own, core, part, recv, recv, recv)


def adamw(w, g, m, v, name):
    R, C = w.shape
    tr = _row_tile(R, C)

    def body(w_ref, g_ref, m_ref, v_ref, d_ref, nm_ref, nv_ref):
        gv = g_ref[...]
        nm = ADAM_B1 * m_ref[...] + (1.0 - ADAM_B1) * gv
        nv = ADAM_B2 * v_ref[...] + (1.0 - ADAM_B2) * (gv * gv)
        m_hat = nm / (1.0 - ADAM_B1 ** ADAM_STEP)
        v_hat = nv / (1.0 - ADAM_B2 ** ADAM_STEP)
        d_ref[...] = -ADAM_LR * (m_hat / (jnp.sqrt(v_hat) + ADAM_EPS) + ADAM_WD * w_ref[...])
        nm_ref[...] = nm
        nv_ref[...] = nv

    spec = pl.BlockSpec((tr, C), lambda i: (i, 0))
    return pl.pallas_call(
        body, grid=(R // tr,), name=name, in_specs=[spec] * 4, out_specs=[spec] * 3,
        out_shape=[jax.ShapeDtypeStruct((R, C), F32)] * 3, compiler_params=_params("parallel"),
    )(w, g, m, v)


HBM = pl.BlockSpec(memory_space=pl.ANY)


def _place():
    x, y, c = lax.axis_index("x"), lax.axis_index("y"), lax.axis_index("c")
    chips = [(1 - x, y), (x, 1 - y), (1 - x, 1 - y)]
    return x, y, c, chips


def _gather_phases(n):
    def copies(buf, sems, kind):
        send_sems, recv_sems = sems
        x, y, c, chips = _place()

        def remote(w, k, slot, to):
            return pltpu.make_async_remote_copy(src_ref=slot, dst_ref=slot, send_sem=send_sems.at[w, k],
                                                recv_sem=recv_sems.at[w, k], device_id=to, device_id_type=MESH)

        def one(w, k, px, py):
            if kind == "mine":
                return remote(w, k, buf[w].at[2 * x + y, c], (px, py, c))
            if kind == "theirs":
                return remote(w, k, buf[w].at[2 * px + py, c], (px, py, c))
            if kind == "onward":
                return remote(w, 3 + k, buf[w].at[2 * px + py, c], (x, y, 1 - c))
            return remote(w, 3 + k, buf[w].at[2 * px + py, 1 - c], (x, y, 1 - c))

        return [one(w, k, px, py) for w in range(n) for k, (px, py) in enumerate(chips)]

    def start(_, buf, sems):
        for cp in copies(buf, sems, "mine"):
            cp.start()

    def relay(_, buf, sems):
        for theirs, onward in zip(copies(buf, sems, "theirs"), copies(buf, sems, "onward")):
            theirs.wait_recv()
            onward.start()

    def finish(_, buf, sems):
        for cp in copies(buf, sems, "relayed"):
            cp.wait_recv()
        for cp in copies(buf, sems, "mine") + copies(buf, sems, "onward"):
            cp.wait_send()

    return start, relay, finish


def _gather_sems(n):
    return (pltpu.SemaphoreType.DMA((n, 6)), pltpu.SemaphoreType.DMA((n, 6)))


def gather_weights(bufs):
    n = len(bufs)
    phases = _gather_phases(n)

    def body(*refs):
        for phase in phases:
            phase(None, refs[n:2 * n], refs[2 * n:])

    return pl.pallas_call(
        body, name="gather_weights", in_specs=[HBM] * n, out_specs=[HBM] * n,
        out_shape=[jax.ShapeDtypeStruct(b.shape, b.dtype) for b in bufs],
        input_output_aliases={i: i for i in range(n)}, scratch_shapes=list(_gather_sems(n)),
    )(*bufs)


def gather_hook(bufs, n_steps):
    start, relay, finish = _gather_phases(len(bufs))
    return Hook(ins=tuple(bufs), out_shapes=tuple(jax.ShapeDtypeStruct(b.shape, b.dtype) for b in bufs),
                alias=tuple((i, i) for i in range(len(bufs))),
                sems=_gather_sems(len(bufs)),
                steps=((0, "before", start), ((n_steps * 13) // 16, "before", relay), (n_steps - 1, "after", finish)))


def sibling_split(grads, name):
    n = len(grads)

    def body(*refs):
        src, dst = refs[:n], refs[n:2 * n]
        send_sems, recv_sems = refs[2 * n:]
        x, y, c, _ = _place()
        sends = [pltpu.make_async_remote_copy(src_ref=src[w].at[:, 1 - c], dst_ref=dst[w], send_sem=send_sems.at[w],
                                              recv_sem=recv_sems.at[w], device_id=(x, y, 1 - c), device_id_type=MESH)
                 for w in range(n)]
        for cp in sends:
            cp.start()
        for cp in sends:
            cp.wait()

    return pl.pallas_call(
        body, name=name, in_specs=[HBM] * n, out_specs=[HBM] * n,
        out_shape=[jax.ShapeDtypeStruct((4,) + g.shape[2:], g.dtype) for g in grads],
        scratch_shapes=[pltpu.SemaphoreType.DMA((n,))] * 2,
    )(*grads)


def split_hook(grads, n_steps):
    n = len(grads)

    def copies(src, dst, sems):
        send_sems, recv_sems = sems
        x, y, c, _ = _place()
        return [pltpu.make_async_remote_copy(src_ref=src[w].at[:, 1 - c], dst_ref=dst[w], send_sem=send_sems.at[w],
                                             recv_sem=recv_sems.at[w], device_id=(x, y, 1 - c), device_id_type=MESH)
                for w in range(n)]

    def start(src, dst, sems):
        for cp in copies(src, dst, sems):
            cp.start()

    def finish(src, dst, sems):
        for cp in copies(src, dst, sems):
            cp.wait()

    return Hook(ins=tuple(grads), out_shapes=tuple(jax.ShapeDtypeStruct((4,) + g.shape[2:], g.dtype) for g in grads),
                sems=(pltpu.SemaphoreType.DMA((n,)), pltpu.SemaphoreType.DMA((n,))),
                steps=((0, "before", start), (n_steps - 1, "after", finish)))


def exchange_hook(parts, n_steps):
    n = len(parts)

    def copies(src, dst, sems):
        send_sems, recv_sems = sems
        _, _, c, chips = _place()
        return [pltpu.make_async_remote_copy(
            src_ref=src[w].at[2 * px + py], dst_ref=dst[w].at[k], send_sem=send_sems.at[w, k],
            recv_sem=recv_sems.at[w, k], device_id=(px, py, c), device_id_type=MESH)
            for w in range(n) for k, (px, py) in enumerate(chips)]

    def start(src, dst, sems):
        for cp in copies(src, dst, sems):
            cp.start()

    def finish(src, dst, sems):
        for cp in copies(src, dst, sems):
            cp.wait()

    return Hook(ins=tuple(parts), out_shapes=tuple(jax.ShapeDtypeStruct((3,) + p.shape[1:], p.dtype) for p in parts),
                sems=(pltpu.SemaphoreType.DMA((n, 3)), pltpu.SemaphoreType.DMA((n, 3))),
                steps=((0, "before", start), (n_steps - 1, "after", finish)))


def sibling_join(sums):
    n = len(sums)

    def body(*refs):
        buf = refs[n:2 * n]
        send_sems, recv_sems = refs[2 * n:]
        x, y, c, _ = _place()
        sends = [pltpu.make_async_remote_copy(src_ref=buf[w].at[c], dst_ref=buf[w].at[c], send_sem=send_sems.at[w],
                                              recv_sem=recv_sems.at[w], device_id=(x, y, 1 - c), device_id_type=MESH)
                 for w in range(n)]
        for cp in sends:
            cp.start()
        for w, cp in enumerate(sends):
            cp.wait_send()
            pltpu.make_async_remote_copy(src_ref=buf[w].at[c], dst_ref=buf[w].at[1 - c], send_sem=send_sems.at[w],
                                         recv_sem=recv_sems.at[w], device_id=(x, y, 1 - c), device_id_type=MESH).wait_recv()

    return pl.pallas_call(
        body, name="sibling_join", in_specs=[HBM] * n, out_specs=[HBM] * n,
        out_shape=[jax.ShapeDtypeStruct(s.shape, s.dtype) for s in sums],
        input_output_aliases={i: i for i in range(n)},
        scratch_shapes=[pltpu.SemaphoreType.DMA((n,))] * 2,
    )(*sums)


def join_hook(sums, n_steps):
    n = len(sums)

    def copies(buf, sems, recv):
        send_sems, recv_sems = sems
        x, y, c, _ = _place()
        half = (1 - c) if recv else c
        return [pltpu.make_async_remote_copy(src_ref=buf[w].at[half], dst_ref=buf[w].at[half], send_sem=send_sems.at[w],
                                             recv_sem=recv_sems.at[w], device_id=(x, y, 1 - c), device_id_type=MESH)
                for w in range(n)]

    def start(_, buf, sems):
        for cp in copies(buf, sems, False):
            cp.start()

    def finish(_, buf, sems):
        for cp in copies(buf, sems, False):
            cp.wait_send()
        for cp in copies(buf, sems, True):
            cp.wait_recv()

    return Hook(ins=tuple(sums), out_shapes=tuple(jax.ShapeDtypeStruct(s.shape, s.dtype) for s in sums),
                alias=tuple((i, i) for i in range(n)),
                sems=(pltpu.SemaphoreType.DMA((n,)), pltpu.SemaphoreType.DMA((n,))),
                steps=((0, "before", start), (n_steps - 1, "after", finish)))


def allreduce_small(v):
    R, C = v.shape

    def body(v_ref, out_ref, all_ref, send_sems, recv_sems):
        x, y, c, chips = _place()
        me, sibling = (x, y, c), (x, y, 1 - c)

        def slot(px, py, pc):
            return all_ref.at[4 * px + 2 * py + pc]

        def copy(k, block, to, src=None):
            return pltpu.make_async_remote_copy(src_ref=slot(*block) if src is None else src, dst_ref=slot(*block),
                                                send_sem=send_sems.at[k], recv_sem=recv_sems.at[k], device_id=to,
                                                device_id_type=MESH)

        first = [copy(0, me, sibling, src=v_ref)] + [copy(1 + j, me, (*chip, c), src=v_ref) for j, chip in enumerate(chips)]
        for cp in first:
            cp.start()
        slot(*me)[...] = v_ref[...]
        passed = [copy(4 + j, (*chip, c), sibling) for j, chip in enumerate(chips)]
        for j, chip in enumerate(chips):
            copy(1 + j, (*chip, c), me).wait_recv()
            passed[j].start()
        copy(0, sibling, me).wait_recv()
        for j, chip in enumerate(chips):
            copy(4 + j, (*chip, 1 - c), me).wait_recv()
        for cp in first + passed:
            cp.wait_send()
        acc = all_ref[0]
        for k in range(1, 8):
            acc = acc + all_ref[k]
        out_ref[...] = acc

    vmem = pl.BlockSpec(memory_space=pltpu.VMEM)
    return pl.pallas_call(
        body, name="allreduce_small", in_specs=[vmem], out_specs=vmem, out_shape=jax.ShapeDtypeStruct((R, C), F32),
        scratch_shapes=[pltpu.VMEM((8, R, C), F32), pltpu.SemaphoreType.DMA((7,)), pltpu.SemaphoreType.DMA((7,))],
        compiler_params=pltpu.CompilerParams(vmem_limit_bytes=VMEM_LIMIT_V7X),
    )(v)


def allgather_hook(v, n_steps):
    R, C = v.shape

    def copies(v_ref, out, sems, kind):
        send_sems, recv_sems, _ = sems
        x, y, c, chips = _place()
        me, sibling = (x, y, c), (x, y, 1 - c)

        def slot(px, py, pc):
            return out.at[4 * px + 2 * py + pc]

        def copy(k, block, to, src=None):
            return pltpu.make_async_remote_copy(src_ref=slot(*block) if src is None else src, dst_ref=slot(*block),
                                                send_sem=send_sems.at[k], recv_sem=recv_sems.at[k], device_id=to,
                                                device_id_type=MESH)

        if kind == "mine":
            return [copy(0, me, sibling, src=v_ref)] + [copy(1 + j, me, (*chip, c), src=v_ref) for j, chip in enumerate(chips)]
        if kind == "theirs":
            return [copy(1 + j, (*chip, c), me) for j, chip in enumerate(chips)]
        if kind == "onward":
            return [copy(4 + j, (*chip, c), sibling) for j, chip in enumerate(chips)]
        return [copy(0, sibling, me)] + [copy(4 + j, (*chip, 1 - c), me) for j, chip in enumerate(chips)]

    def own(v_ref, out, sems):
        x, y, c, _ = _place()
        return pltpu.make_async_copy(v_ref, out.at[4 * x + 2 * y + c], sems[2])

    def start(ins, outs, sems):
        own(ins[0], outs[0], sems).start()
        for cp in copies(ins[0], outs[0], sems, "mine"):
            cp.start()

    def relay(ins, outs, sems):
        for theirs, onward in zip(copies(ins[0], outs[0], sems, "theirs"), copies(ins[0], outs[0], sems, "onward")):
            theirs.wait_recv()
            onward.start()

    def finish(ins, outs, sems):
        for cp in copies(ins[0], outs[0], sems, "relayed"):
            cp.wait_recv()
        for cp in copies(ins[0], outs[0], sems, "mine") + copies(ins[0], outs[0], sems, "onward"):
            cp.wait_send()
        own(ins[0], outs[0], sems).wait()

    return Hook(ins=(v,), out_shapes=(jax.ShapeDtypeStruct((8, R, C), F32),),
                sems=(pltpu.SemaphoreType.DMA((7,)), pltpu.SemaphoreType.DMA((7,)), pltpu.SemaphoreType.DMA(())),
                steps=((0, "before", start), ((n_steps * 13) // 16, "before", relay), (n_steps - 1, "after", finish)))


def merge_hooks(a, b):
    cut = (len(a.ins), len(a.out_shapes), len(a.sems))

    def part(fn, second):
        def run(ins, outs, sems):
            if second:
                fn(ins[cut[0]:], outs[cut[1]:], sems[cut[2]:])
            else:
                fn(ins[:cut[0]], outs[:cut[1]], sems[:cut[2]])
        return run

    steps = tuple((s, w, part(fn, False)) for s, w, fn in a.steps) + tuple((s, w, part(fn, True)) for s, w, fn in b.steps)
    alias = a.alias + tuple((i + cut[0], o + cut[1]) for i, o in b.alias)
    return Hook(a.ins + b.ins, a.out_shapes + b.out_shapes, alias, a.sems + b.sems, steps)


def sum_blocks(g):
    n, R, C = g.shape

    def body(g_ref, o_ref):
        acc = g_ref[0]
        for k in range(1, n):
            acc = acc + g_ref[k]
        o_ref[...] = acc

    return pl.pallas_call(body, name="sum_blocks", grid=(1,), in_specs=[_full(g.shape)], out_specs=_acc((R, C)),
                          out_shape=jax.ShapeDtypeStruct((R, C), F32), compiler_params=_params("arbitrary"))(g)


SMALL_PAD = 1024


def _pack_small(arrs):
    parts = []
    for a in arrs:
        f = a.reshape(-1).astype(F32)
        parts.append(jnp.pad(f, (0, (-f.shape[0]) % SMALL_PAD)))
    return jnp.concatenate(parts).reshape(-1, 128)


def _unpack_small(packed, shapes):
    flat = packed.reshape(-1)
    outs, off = [], 0
    for s in shapes:
        size = int(np.prod(s))
        outs.append(flat[off:off + size].reshape(s))
        off += size + (-size) % SMALL_PAD
    return outs


ATT_KEYS = ATT_WIN * ATT_TILE
ATT_NEAR_LO = (ATT_WIN - 1) * ATT_TILE - REL_CLIP + 1
ATT_PERIOD = ATT_KEYS + ATT_TILE + 1


def _att_bias_table(rel_bias):
    far = rel_bias[:, 2 * REL_CLIP:]
    near = rel_bias[:, 2 * REL_CLIP - 1:0:-1]
    w = jnp.concatenate([jnp.broadcast_to(far, (ATT_HEADS, ATT_NEAR_LO)), near,
                         jnp.broadcast_to(far, (ATT_HEADS, ATT_PERIOD - ATT_KEYS))], axis=1)
    rows = jnp.tile(w, (1, ATT_TILE))[:, :ATT_TILE * (ATT_PERIOD - 1)].reshape(ATT_HEADS, ATT_TILE, ATT_PERIOD - 1)
    i = np.arange(ATT_TILE)[:, None]
    m = np.arange(ATT_KEYS)[None, :]
    qc, kc = i // CHUNK, m // CHUNK
    band = (kc >= qc) & (kc <= qc + N_PREV_CHUNKS)
    return jnp.where(band[None], rows[:, :, :ATT_KEYS], NEG_INF)


def _rel_bias_grad(dbias):
    p = jnp.pad(dbias[:, :, ATT_TILE:], ((0, 0), (0, 0), (ATT_FAR, ATT_PERIOD - 1 - ATT_KEYS))).reshape(ATT_HEADS, -1)
    p = jnp.pad(p, ((0, 0), (0, ATT_TILE))).reshape(ATT_HEADS, ATT_TILE, ATT_PERIOD)
    gw, far = bias_colsum(p, dbias[:, :, :ATT_TILE])
    return jnp.concatenate([jnp.zeros((ATT_HEADS, 1), F32), gw[:, ATT_KEYS - 1:ATT_NEAR_LO - 1:-1], far[:, :1]], axis=1)


def kernel(x, mem, norm_mix_g, w_in, rel_bias, sg_ln_g, sg_ln_b, sg_w, sg_b, w_branch_att, w_branch_sg, w_out, norm_xattn_g, norm_mem_g, w_xq, w_xkv, w_xo, norm_ffn_g, w_ffn_in, w_ffn_out, norm_final_g, loss_target, m_norm_mix_g, m_w_in, m_rel_bias, m_sg_ln_g, m_sg_ln_b, m_sg_w, m_sg_b, m_w_branch_att, m_w_branch_sg, m_w_out, m_norm_xattn_g, m_norm_mem_g, m_w_xq, m_w_xkv, m_w_xo, m_norm_ffn_g, m_w_ffn_in, m_w_ffn_out, m_norm_final_g, v_norm_mix_g, v_w_in, v_rel_bias, v_sg_ln_g, v_sg_ln_b, v_sg_w, v_sg_b, v_w_branch_att, v_w_branch_sg, v_w_out, v_norm_xattn_g, v_norm_mem_g, v_w_xq, v_w_xkv, v_w_xo, v_norm_ffn_g, v_w_ffn_in, v_w_ffn_out, v_norm_final_g):
    S, D = x.shape[1], x.shape[2]
    x2d, mem2d, tgt = x[0], mem[0], loss_target[0]

    big_names = ["w_in", "w_branch_att", "w_branch_sg", "w_out", "w_xq", "w_xkv", "w_xo", "w_ffn_in", "w_ffn_out"]
    col_sharded = [True, True, True, False, False, True, False, True, False]
    big_w = [a[0] for a in (w_in, w_branch_att, w_branch_sg, w_out, w_xq, w_xkv, w_xo, w_ffn_in, w_ffn_out)]
    big_m = [a[0] for a in (m_w_in, m_w_branch_att, m_w_branch_sg, m_w_out, m_w_xq, m_w_xkv, m_w_xo, m_w_ffn_in, m_w_ffn_out)]
    big_v = [a[0] for a in (v_w_in, v_w_branch_att, v_w_branch_sg, v_w_out, v_w_xq, v_w_xkv, v_w_xo, v_w_ffn_in, v_w_ffn_out)]

    own = (2 * lax.axis_index("x") + lax.axis_index("y")).astype(jnp.int32).reshape(1)
    core = lax.axis_index("c").astype(jnp.int32).reshape(1)
    placed = [place_shard(w, own, "place_" + nm) for w, nm in zip(big_w, big_names)]

    def whole(g4, i):
        R, C = big_w[i].shape
        return g4.reshape(4, R, C) if col_sharded[i] else g4.reshape(4 * R, C)

    W_in = whole(gather_weights(placed[:1])[0], 0)

    g_mix, g_xat, g_mem, g_ffn = norm_mix_g, norm_xattn_g, norm_mem_g, norm_ffn_g
    g_fin = norm_final_g.reshape(1, D)
    bias = _att_bias_table(rel_bias[0])
    tt = np.arange(SG_BLOCK)
    sg_mask = (tt[None, :] // CHUNK) <= (tt[:, None] // CHUNK)
    wm_f = jnp.where(sg_mask[None], sg_w[0], 0.0)
    pairs = wm_f.reshape(SG_PAIRS, 2, SG_BLOCK, SG_BLOCK)
    wpair = jnp.transpose(pairs, (0, 2, 1, 3)).reshape(SG_PAIRS, SG_BLOCK, 2 * SG_BLOCK).astype(BF16)
    wtpair = jnp.transpose(pairs, (0, 3, 1, 2)).reshape(SG_PAIRS, SG_BLOCK, 2 * SG_BLOCK).astype(BF16)
    maskpair = jnp.asarray(np.tile(sg_mask, (SG_PAIRS, 1, 2)), F32)
    bfull = jnp.repeat(sg_b[0].T, SG_GROUP_DIM, axis=1)
    lng, lnb = sg_ln_g[0].reshape(1, 512), sg_ln_b[0].reshape(1, 512)
    gid = np.arange(512) // SG_GROUP_DIM
    amean = jnp.asarray((gid[:, None] == gid[None, :]) / SG_GROUP_DIM, BF16)

    h1, z, *early = norm_mm(x2d, g_mix, W_in, tm=ROW_TILE, scale=None, name="norm_mm_in",
                            hook=gather_hook(placed[1:7], S // ROW_TILE))
    y_att, lse_att, *late = attn_fwd(z, bias, gather_hook(placed[7:], S // ATT_TILE))
    W_ba, W_bs, W_out, W_xq, W_xkv, W_xo, W_fi, W_fo = [whole(g4, i + 1) for i, g4 in enumerate(early + late)]
    y_sg, = sgu_fwd(z, amean, lng, lnb, wpair, bfull, tm=512)
    a_br, b_br, merged, x1 = merge_fwd(y_att, y_sg, z, x2d, W_ba, W_bs, W_out, tm=ROW_TILE)
    h2, xq = norm_mm(x1, g_xat, W_xq.reshape(1, D, D), tm=512, scale=XATT_HEAD_DIM ** -0.5, name="norm_mm_xq")
    mn, kv = norm_mm(mem2d, g_mem, W_xkv, tm=mem2d.shape[0], scale=None, name="norm_mm_kv")
    o_x, lse_x, x2 = xattn_fwd(xq, kv, W_xo, x1, tm=512)
    h3, gu = norm_mm(x2, g_ffn, W_fi, tm=ROW_TILE, scale=None, name="norm_mm_ffn")
    act, loss_vec, dx3, dg_fin = ffn_out_loss(gu, W_fo, x2, g_fin, tgt, tm=ROW_TILE)

    dgu = ffn_act_bwd(dx3, gu, W_fo, tm=ROW_TILE, nchunk=2)
    gW_fo = tn_mm(act, dx3, shards=1, tn=D, tk=DW_TOKENS, name="dw_ffn_out")
    gW_fi = tn_mm(h3, dgu, shards=4, tn=2 * W_fi.shape[2], tk=DW_TOKENS, name="dw_ffn_in")
    dx2, dg_ffn = mm_nt_norm_bwd(dgu, W_fi, x2, g_ffn, dx3, tm=ROW_TILE, name="dx_ffn")
    dq_x, dkv = xattn_bwd(dx2, xq, o_x, lse_x, kv, W_xo, tm=512)
    gW_xo = tn_mm(o_x, dx2, shards=1, tn=D, tk=DW_TOKENS, name="dw_xo")
    gW_xq = tn_mm(h2, dq_x, shards=1, tn=D, tk=DW_TOKENS, name="dw_xq")
    dx1, dg_xat = mm_nt_norm_bwd(dq_x, W_xq.reshape(1, D, D), x1, g_xat, dx2, tm=512, name="dx_xq")
    gW_xkv = tn_mm(mn, dkv, shards=4, tn=2 * D, tk=mem2d.shape[0], name="dw_xkv")
    _, dg_mem = mm_nt_norm_bwd(dkv.astype(BF16), W_xkv, mem2d, g_mem, None, tm=mem2d.shape[0], name="dx_mem")
    def canon(g, i):
        R, C = big_w[i].shape
        return g.reshape(4, 2, R // 2, C)

    def partials(gs, recv, idx):
        return [chip_partial(g, r, core, "chip_partial_" + big_names[i]) for g, r, i in zip(gs, recv, idx)]

    early_g = [canon(g, i) for g, i in zip([gW_xq, gW_xkv, gW_xo, gW_fi, gW_fo], range(4, 9))]
    da, db, dgab, dy_att, dy_sg, *early_r = merge_bwd(dx1, W_out, a_br, b_br, z, W_ba, W_bs, tm=ROW_TILE,
                                                      hook=split_hook(early_g, S // ROW_TILE))
    gW_out = tn_mm(merged, dx1, shards=1, tn=D, tk=DW_TOKENS, name="dw_out")
    gW_ba = tn_mm(y_att, da, shards=4, tn=D, tk=DW_TOKENS, name="dw_branch_att")
    gW_bs = tn_mm(y_sg, db, shards=4, tn=D, tk=DW_TOKENS, name="dw_branch_sg")
    mid_g = [canon(g, i) for g, i in zip([gW_ba, gW_bs, gW_out], range(1, 4))]
    parts_b = (partials(mid_g, sibling_split(mid_g, "sibling_split_mid"), range(1, 4))
               + partials(early_g, early_r, range(4, 9)))
    nt_bwd = S // ATT_TILE + ATT_WIN - 1
    dqkv, dbias, *from_chips_b = attn_bwd(z, y_att, dy_att, lse_att, bias, exchange_hook(parts_b, nt_bwd))
    sums_b = [shard_sum(p, r, own, core, "shard_sum_" + nm) for p, r, nm in zip(parts_b, from_chips_b, big_names[1:])]
    duv, dwpair, dsgb_full, dlng, dlnb, *joined_b = sgu_bwd(z, dy_sg, amean, lng, lnb, wpair, wtpair, bfull, maskpair,
                                                         tm=512, hook=join_hook(sums_b, S // 512))
    dwm = jnp.transpose(dwpair.reshape(SG_PAIRS, SG_BLOCK, 2, SG_BLOCK), (0, 2, 1, 3))
    dsgb = dsgb_full[:, ::SG_GROUP_DIM].T
    dz = [dqkv, duv, dgab]
    gW_in = tn_mm_pieces(h1, dz, shards=4, per=2, tk=DW_TOKENS, name="dw_in")
    in_g = [canon(gW_in, 0)]
    parts_a = partials(in_g, sibling_split(in_g, "sibling_split_w_in"), [0])
    d_rel = _rel_bias_grad(dbias)
    small_g = [d_rel, dlng, dlnb, dwm, dsgb, dg_xat, dg_mem, dg_ffn, dg_fin, loss_vec[0, :1]]
    n_dx = S // ROW_TILE
    dx, dg_mix, from_chips_a, small_all = mm_nt_norm_bwd(
        dz, W_in, x2d, g_mix, dx1, tm=ROW_TILE, name="dx_in",
        hook=merge_hooks(exchange_hook(parts_a, n_dx), allgather_hook(_pack_small(small_g), n_dx)))

    joined = list(sibling_join([shard_sum(parts_a[0], from_chips_a, own, core, "shard_sum_w_in")])) + list(joined_b)
    big_out = []
    for j, w, m_, v_, nm in zip(joined, big_w, big_m, big_v, big_names):
        g = j.reshape(w.shape)
        big_out.append((g,) + tuple(adamw(w, g, m_, v_, "adamw_" + nm)))

    small_w = [norm_mix_g, rel_bias, sg_ln_g, sg_ln_b, sg_w, sg_b, norm_xattn_g, norm_mem_g, norm_ffn_g, norm_final_g]
    small_m = [m_norm_mix_g, m_rel_bias, m_sg_ln_g, m_sg_ln_b, m_sg_w, m_sg_b, m_norm_xattn_g, m_norm_mem_g, m_norm_ffn_g, m_norm_final_g]
    small_v = [v_norm_mix_g, v_rel_bias, v_sg_ln_g, v_sg_ln_b, v_sg_w, v_sg_b, v_norm_xattn_g, v_norm_mem_g, v_norm_ffn_g, v_norm_final_g]
    shapes = [w.shape for w in small_w]
    g_sum = jnp.concatenate([allreduce_small(_pack_small([dg_mix])), sum_blocks(small_all)], axis=0)
    zero = jnp.zeros((1,), F32)
    d_s, m_s, v_s = adamw(_pack_small(small_w + [zero]), g_sum, _pack_small(small_m + [zero]), _pack_small(small_v + [zero]),
                          "adamw_small")
    sg_, sd, sm, sv_ = (_unpack_small(p, shapes + [(1,)]) for p in (g_sum, d_s, m_s, v_s))
    loss = sg_[-1][0]

    order = ["norm_mix_g", "w_in", "rel_bias", "sg_ln_g", "sg_ln_b", "sg_w", "sg_b", "w_branch_att", "w_branch_sg", "w_out",
             "norm_xattn_g", "norm_mem_g", "w_xq", "w_xkv", "w_xo", "norm_ffn_g", "w_ffn_in", "w_ffn_out", "norm_final_g"]
    small_names = ["norm_mix_g", "rel_bias", "sg_ln_g", "sg_ln_b", "sg_w", "sg_b", "norm_xattn_g", "norm_mem_g", "norm_ffn_g",
                   "norm_final_g"]
    res = {}
    for i, nm in enumerate(small_names):
        res[nm] = (sg_[i], sd[i], sm[i], sv_[i])
    for nm, outs in zip(big_names, big_out):
        res[nm] = tuple(o[None] for o in outs)
    return (loss, dx[None], *[res[nm][0] for nm in order], *[res[nm][1] for nm in order],
            *[res[nm][2] for nm in order], *[res[nm][3] for nm in order])
```

```python
import functools
from typing import NamedTuple

import numpy as np
import jax
import jax.numpy as jnp
from jax import lax
from jax.experimental import pallas as pl
from jax.experimental.pallas import tpu as pltpu

F32, BF16 = jnp.float32, jnp.bfloat16
MESH = pl.DeviceIdType.MESH

EPS = 1e-6
NEG_INF = -1e30
CHUNK = 64
N_PREV_CHUNKS = 8
ATT_HEADS, ATT_HEAD_DIM = 8, 64
REL_CLIP = 128
SG_BLOCK, SG_GROUPS, SG_GROUP_DIM = 128, 8, 64
XATT_HEADS, XATT_HEAD_DIM = 4, 256
ATT_TILE = 128
ATT_WIN = 5
ADAM_LR, ADAM_B1, ADAM_B2, ADAM_EPS, ADAM_WD, ADAM_STEP = 0.001, 0.9, 0.999, 1e-08, 0.01, 10

VMEM_LIMIT_V7X = 56 * 1024 * 1024
EW_BLOCK_BYTES = 2 << 20
ROW_TILE = 512
DW_TOKENS = 1024
XATT_TILE = 1024
SGU_TILE = 1024
HBM = pl.BlockSpec(memory_space=pl.ANY)


def _params(*sem):
    return pltpu.CompilerParams(dimension_semantics=sem, vmem_limit_bytes=VMEM_LIMIT_V7X)


def _full(shape):
    n = len(shape)
    return pl.BlockSpec(shape, lambda *_: (0,) * n, pipeline_mode=pl.Buffered(1))


def _acc(shape):
    n = len(shape)
    return pl.BlockSpec(shape, lambda *_: (0,) * n)


class Hook(NamedTuple):
    ins: tuple = ()
    out_shapes: tuple = ()
    alias: tuple = ()
    sems: tuple = ()
    steps: tuple = ()


def _run_hook(hook, pos, h_in, h_out, h_sems):
    for at, where, fn in hook.steps:
        if where == pos:
            pl.when(pl.program_id(0) == at)(functools.partial(fn, h_in, h_out, h_sems))


def _hooked_call(body, hook, *, n_in, n_out, in_specs, out_specs, out_shape, scratch_shapes=(), **kw):
    nh = len(hook.ins)
    aliases = {n_in + i: n_out + o for i, o in hook.alias}
    return pl.pallas_call(
        body, in_specs=list(in_specs) + [HBM] * nh, out_specs=list(out_specs) + [HBM] * len(hook.out_shapes),
        out_shape=list(out_shape) + list(hook.out_shapes), scratch_shapes=list(scratch_shapes) + list(hook.sems),
        input_output_aliases=aliases, **kw)


def _nt(a, b):
    return lax.dot_general(a, b, (((1,), (1,)), ((), ())), preferred_element_type=F32)


def _tn(a, b):
    return lax.dot_general(a, b, (((0,), (0,)), ((), ())), preferred_element_type=F32)


def _nn(a, b):
    return jnp.dot(a, b, preferred_element_type=F32)


def _sigmoid(x):
    return 0.5 * jnp.tanh(0.5 * x) + 0.5


_GELU_C = float(np.sqrt(2.0 / np.pi))


def _gelu(x):
    t = jnp.tanh(_GELU_C * (x + 0.044715 * (x * x * x)))
    return x * (0.5 * (1.0 + t))


def _gelu_grad(x):
    t = jnp.tanh(_GELU_C * (x + 0.044715 * (x * x * x)))
    return 0.5 * (1.0 + t) + 0.5 * x * (1.0 - t * t) * (_GELU_C * (1.0 + 3.0 * 0.044715 * x * x))


def _rms_stats(xf):
    rstd = lax.rsqrt(jnp.mean(xf * xf, axis=-1, keepdims=True) + EPS)
    return rstd, xf * rstd


def _rms_bwd(xhat, rstd, g, dh):
    dxh = dh * g
    dx = rstd * (dxh - xhat * jnp.mean(dxh * xhat, axis=-1, keepdims=True))
    return dx, dh * xhat


def norm_mm(x, g, w, *, tm, scale, name, hook=Hook()):
    S, D = x.shape
    J, _, C = w.shape
    nh, nho = len(hook.ins), len(hook.out_shapes)

    def body(x_ref, g_ref, w_ref, *rest):
        h_in, (h_ref, z_ref), h_out, h_sems = rest[:nh], rest[nh:nh + 2], rest[nh + 2:nh + 2 + nho], rest[nh + 2 + nho:]
        _run_hook(hook, "before", h_in, h_out, h_sems)
        _, xhat = _rms_stats(x_ref[...])
        h = (xhat * g_ref[...]).astype(BF16)
        h_ref[...] = h
        for j in range(J):
            acc = _nn(h, w_ref[j])
            if scale is not None:
                acc = acc * scale
            z_ref[:, j * C:(j + 1) * C] = acc.astype(BF16)
        _run_hook(hook, "after", h_in, h_out, h_sems)

    return _hooked_call(
        body, hook, n_in=3, n_out=2, grid=(S // tm,), name=name,
        in_specs=[pl.BlockSpec((tm, D), lambda i: (i, 0)), _full((1, D)), _full((J, D, C))],
        out_specs=[pl.BlockSpec((tm, D), lambda i: (i, 0)), pl.BlockSpec((tm, J * C), lambda i: (i, 0))],
        out_shape=[jax.ShapeDtypeStruct((S, D), BF16), jax.ShapeDtypeStruct((S, J * C), BF16)],
        compiler_params=_params("arbitrary"),
    )(x, g, w, *hook.ins)


def _att_window_specs(nt, col):
    return [pl.BlockSpec((ATT_TILE, 512), lambda t, j=j: (jnp.clip(t - (ATT_WIN - 1) + j, 0, nt - 1), col))
            for j in range(ATT_WIN)]


ATT_SCALE = ATT_HEAD_DIM ** -0.5
ATT_FAR = 3 * ATT_TILE


def _att_scores(q_scaled, k, bias, valid):
    return jnp.where(valid, _nt(q_scaled, k) + bias, NEG_INF)


def _att_valid(t):
    kpos = lax.broadcasted_iota(jnp.int32, (ATT_TILE, ATT_WIN * ATT_TILE), 1) + (t - (ATT_WIN - 1)) * ATT_TILE
    return kpos >= 0


def attn_fwd(z, bias, hook=Hook()):
    S = z.shape[0]
    nt = S // ATT_TILE
    n_in, nh, nho = 2 + 2 * ATT_WIN, len(hook.ins), len(hook.out_shapes)

    def body(q_ref, *refs):
        k_refs, v_refs = refs[:ATT_WIN], refs[ATT_WIN:2 * ATT_WIN]
        bias_ref = refs[2 * ATT_WIN]
        rest = refs[2 * ATT_WIN + 1:]
        h_in, (y_ref, lse_ref), h_out = rest[:nh], rest[nh:nh + 2], rest[nh + 2:nh + 2 + nho]
        s_scr, p_scr = rest[nh + 2 + nho:nh + 4 + nho]
        h_sems = rest[nh + 4 + nho:]
        _run_hook(hook, "before", h_in, h_out, h_sems)
        valid = _att_valid(pl.program_id(0))
        heads = [slice(h * ATT_HEAD_DIM, (h + 1) * ATT_HEAD_DIM) for h in range(ATT_HEADS)]
        for h, sl in enumerate(heads):
            k = jnp.concatenate([r[:, sl] for r in k_refs], axis=0)
            s_scr[h] = _att_scores(q_ref[:, sl] * ATT_SCALE, k, bias_ref[h], valid)
        stats = []
        for h in range(ATT_HEADS):
            s = s_scr[h]
            m = jnp.max(s, axis=-1, keepdims=True)
            p = jnp.exp(s - m)
            stats.append((m, jnp.sum(p, axis=-1, keepdims=True)))
            p_scr[h] = p.astype(BF16)
        for h, sl in enumerate(heads):
            m, l = stats[h]
            v = jnp.concatenate([r[:, sl] for r in v_refs], axis=0)
            y_ref[:, sl] = (_nn(p_scr[h], v) / l).astype(BF16)
            lse_ref[:, h:h + 1] = m + jnp.log(l)
        _run_hook(hook, "after", h_in, h_out, h_sems)

    tile = lambda col: pl.BlockSpec((ATT_TILE, 512), lambda t: (t, col))
    return _hooked_call(
        body, hook, n_in=n_in, n_out=2, grid=(nt,), name="attn_fwd",
        in_specs=[tile(0)] + _att_window_specs(nt, 1) + _att_window_specs(nt, 2) + [_full(bias.shape)],
        out_specs=[tile(0), pl.BlockSpec((ATT_TILE, ATT_HEADS), lambda t: (t, 0))],
        out_shape=[jax.ShapeDtypeStruct((S, 512), BF16), jax.ShapeDtypeStruct((S, ATT_HEADS), F32)],
        scratch_shapes=[pltpu.VMEM((ATT_HEADS, ATT_TILE, ATT_WIN * ATT_TILE), F32),
                        pltpu.VMEM((ATT_HEADS, ATT_TILE, ATT_WIN * ATT_TILE), BF16)],
        compiler_params=_params("arbitrary"),
    )(z, *([z] * (2 * ATT_WIN)), bias, *hook.ins)


SG_PAIRS = SG_GROUPS // 2


def _group_mean(x, a_ref):
    rows = x.shape[0]
    hi = x.astype(BF16)
    lo = (x - hi.astype(F32)).astype(BF16)
    both = _nn(jnp.concatenate([hi, lo], axis=0), a_ref[...])
    return both[:rows] + both[rows:]


def _pair_diag(x, ref):
    first = lax.broadcasted_iota(jnp.int32, x.shape, 1) < SG_GROUP_DIM
    ref[0:SG_BLOCK, :] = jnp.where(first, x, 0.0).astype(BF16)
    ref[SG_BLOCK:2 * SG_BLOCK, :] = jnp.where(first, 0.0, x).astype(BF16)


def _sgu_norm(zv_ref, a_ref, lng_ref, lnb_ref):
    v = _gelu(zv_ref[...].astype(F32))
    vc = v - _group_mean(v, a_ref)
    rstd = lax.rsqrt(_group_mean(vc * vc, a_ref) + EPS)
    xhat = vc * rstd
    return xhat, rstd, xhat * lng_ref[...] + lnb_ref[...]


def _sgu_rows(b):
    return pl.ds(pl.multiple_of(b * SG_BLOCK, SG_BLOCK), SG_BLOCK)


def sgu_fwd(z, amean, lng, lnb, wpair, bfull, *, tm, hook=Hook()):
    S = z.shape[0]
    nh, nho = len(hook.ins), len(hook.out_shapes)

    def body(zu_ref, zv_ref, a_ref, lng_ref, lnb_ref, wpair_ref, bfull_ref, *rest):
        h_in, y_ref, h_out = rest[:nh], rest[nh], rest[nh + 1:nh + 1 + nho]
        vn_scr, vbd = rest[nh + 1 + nho:nh + 3 + nho]
        h_sems = rest[nh + 3 + nho:]
        _run_hook(hook, "before", h_in, h_out, h_sems)
        vn_scr[...] = _sgu_norm(zv_ref, a_ref, lng_ref, lnb_ref)[2]

        def block(b, carry):
            rows = _sgu_rows(b)
            u = _gelu(zu_ref[rows, :].astype(F32))
            for p in range(SG_PAIRS):
                lanes = slice(p * 128, (p + 1) * 128)
                _pair_diag(vn_scr[rows, lanes], vbd.at[p])
                sv = _nn(wpair_ref[p], vbd[p]) + bfull_ref[:, lanes]
                y_ref[rows, lanes] = (u[:, lanes] * sv).astype(BF16)
            return carry

        lax.fori_loop(0, tm // SG_BLOCK, block, 0)
        _run_hook(hook, "after", h_in, h_out, h_sems)

    tile = lambda col: pl.BlockSpec((tm, 512), lambda i: (i, col))
    smalls = [amean, lng, lnb, wpair, bfull]
    return _hooked_call(
        body, hook, n_in=2 + len(smalls), n_out=1, grid=(S // tm,), name="sgu_fwd",
        in_specs=[tile(3), tile(4)] + [_full(a.shape) for a in smalls],
        out_specs=[tile(0)], out_shape=[jax.ShapeDtypeStruct((S, 512), BF16)],
        scratch_shapes=[pltpu.VMEM((tm, 512), F32), pltpu.VMEM((SG_PAIRS, 2 * SG_BLOCK, 128), BF16)],
        compiler_params=_params("arbitrary"),
    )(z, z, *smalls, *hook.ins)


def _gate_specs(tm):
    return [pl.BlockSpec((tm, 512), lambda i, c=c: (i, c)) for c in (5, 6, 7, 8)]


def merge_fwd(y_att, y_sg, z, x, w_ba, w_bs, w_out, *, tm):
    S, D = x.shape
    J, _, C = w_ba.shape

    def body(ya_ref, ys_ref, g0, g1, g2, g3, x_ref, wba_ref, wbs_ref, wo_ref, a_ref, b_ref, m_ref, x1_ref):
        ya, ys = ya_ref[...], ys_ref[...]
        a = jnp.concatenate([_nn(ya, wba_ref[j]) for j in range(J)], axis=1)
        b = jnp.concatenate([_nn(ys, wbs_ref[j]) for j in range(J)], axis=1)
        sa = _sigmoid(jnp.concatenate([g0[...], g1[...]], axis=1).astype(F32))
        sb = _sigmoid(jnp.concatenate([g2[...], g3[...]], axis=1).astype(F32))
        a_ref[...] = a.astype(BF16)
        b_ref[...] = b.astype(BF16)
        merged = (sa * a + sb * b).astype(BF16)
        m_ref[...] = merged
        x1_ref[...] = x_ref[...] + _nn(merged, wo_ref[...])

    row = lambda n: pl.BlockSpec((tm, n), lambda i: (i, 0))
    return pl.pallas_call(
        body, grid=(S // tm,), name="merge_fwd",
        in_specs=[row(512), row(512)] + _gate_specs(tm) + [row(D), _full(w_ba.shape), _full(w_bs.shape), _full(w_out.shape)],
        out_specs=[row(D)] * 4,
        out_shape=[jax.ShapeDtypeStruct((S, D), BF16)] * 3 + [jax.ShapeDtypeStruct((S, D), F32)],
        compiler_params=_params("parallel"),
    )(y_att, y_sg, z, z, z, z, x, w_ba, w_bs, w_out)


def xattn_fwd(xq, kv, w_xo, x1, *, tm):
    S, D = xq.shape
    dh = XATT_HEAD_DIM

    def body(q_ref, kv_ref, wo_ref, x1_ref, o_ref, lse_ref, x2_ref):
        outs = []
        for h in range(XATT_HEADS):
            s = _nt(q_ref[:, h * dh:(h + 1) * dh], kv_ref[:, h * dh:(h + 1) * dh])
            m = jnp.max(s, axis=-1, keepdims=True)
            p = jnp.exp(s - m)
            l = jnp.sum(p, axis=-1, keepdims=True)
            outs.append((_nn(p.astype(BF16), kv_ref[:, D + h * dh:D + (h + 1) * dh]) / l).astype(BF16))
            lse_ref[:, h:h + 1] = m + jnp.log(l)
        o = jnp.concatenate(outs, axis=1)
        o_ref[...] = o
        x2_ref[...] = x1_ref[...] + _nn(o, wo_ref[...])

    row = lambda n: pl.BlockSpec((tm, n), lambda i: (i, 0))
    return pl.pallas_call(
        body, grid=(S // tm,), name="xattn_fwd",
        in_specs=[row(D), _full(kv.shape), _full(w_xo.shape), row(D)],
        out_specs=[row(D), row(XATT_HEADS), row(D)],
        out_shape=[jax.ShapeDtypeStruct((S, D), BF16), jax.ShapeDtypeStruct((S, XATT_HEADS), F32),
                   jax.ShapeDtypeStruct((S, D), F32)],
        compiler_params=_params("parallel"),
    )(xq, kv, w_xo, x1)


def ffn_out_loss(gu, w, x2, g, target, *, tm):
    S, D = x2.shape
    F = w.shape[0]

    def body(gu_ref, w_ref, x2_ref, g_ref, t_ref, act_ref, loss_ref, dx_ref, dg_ref):
        @pl.when(pl.program_id(0) == 0)
        def _():
            loss_ref[...] = jnp.zeros_like(loss_ref)
            dg_ref[...] = jnp.zeros_like(dg_ref)

        half = gu_ref[:, :F] * 0.5
        act = (half * gu_ref[:, F:]) * (jnp.tanh(half) + 1.0)
        act_ref[...] = act
        gv = g_ref[...]
        rstd, xhat = _rms_stats(x2_ref[...] + _nn(act, w_ref[...]))
        err = xhat * gv - t_ref[...]
        loss_ref[...] += 0.5 * jnp.sum(jnp.mean(err * err, axis=-1, keepdims=True))
        dx, dgc = _rms_bwd(xhat, rstd, gv, err * (1.0 / D))
        dx_ref[...] = dx
        dg_ref[...] += jnp.sum(dgc, axis=0, keepdims=True)

    row = lambda n: pl.BlockSpec((tm, n), lambda i: (i, 0))
    return pl.pallas_call(
        body, grid=(S // tm,), name="ffn_out_loss",
        in_specs=[row(2 * F), _full(w.shape), row(D), _full((1, D)), row(D)],
        out_specs=[row(F), _acc((1, 128)), row(D), _acc((1, D))],
        out_shape=[jax.ShapeDtypeStruct((S, F), BF16), jax.ShapeDtypeStruct((1, 128), F32),
                   jax.ShapeDtypeStruct((S, D), F32), jax.ShapeDtypeStruct((1, D), F32)],
        compiler_params=_params("arbitrary"),
    )(gu, w, x2, g, target)


def ffn_act_bwd(dx3, gu, w, *, tm, nchunk):
    S, D = dx3.shape
    F = w.shape[0]
    cn = F // nchunk

    def body(dx_ref, gu_ref, w_ref, dgu_ref):
        dxb = dx_ref[...].astype(BF16)
        for j in range(nchunk):
            dact = _nt(dxb, w_ref[j * cn:(j + 1) * cn, :])
            gate = gu_ref[:, j * cn:(j + 1) * cn]
            up = gu_ref[:, F + j * cn:F + (j + 1) * cn]
            sg = _sigmoid(gate)
            silu = gate * sg
            dsilu = sg + silu * (1.0 - sg)
            dgu_ref[:, j * cn:(j + 1) * cn] = (dact * (up * dsilu).astype(F32)).astype(BF16)
            dgu_ref[:, F + j * cn:F + (j + 1) * cn] = (dact * silu.astype(F32)).astype(BF16)

    row = lambda n: pl.BlockSpec((tm, n), lambda i: (i, 0))
    return pl.pallas_call(
        body, grid=(S // tm,), name="ffn_act_bwd",
        in_specs=[row(D), row(2 * F), _full(w.shape)], out_specs=row(2 * F),
        out_shape=jax.ShapeDtypeStruct((S, 2 * F), BF16), compiler_params=_params("parallel"),
    )(dx3, gu, w)


def _overlaps(widths, lo, hi):
    out, off = [], 0
    for p, wd in enumerate(widths):
        a, b = max(lo, off), min(hi, off + wd)
        if a < b:
            out.append((p, a - off, b - off, a - lo))
        off += wd
    return out


def tn_mm_pieces(a, pieces, *, shards, per, tk, name):
    K, M = a.shape
    widths = [p.shape[1] for p in pieces]
    C = sum(widths) // shards
    tn = per * C
    nk = K // tk
    n_tiles = shards // per

    def body(a_ref, *refs):
        p_refs, o_ref, acc_ref = refs[:len(pieces)], refs[len(pieces)], refs[len(pieces) + 1]
        n, k = pl.program_id(0), pl.program_id(1)

        @pl.when(k == 0)
        def _():
            acc_ref[...] = jnp.zeros_like(acc_ref)

        av = a_ref[...]
        for tile in range(n_tiles):
            @pl.when(n == tile)
            def _(tile=tile):
                for p, c0, c1, at in _overlaps(widths, tile * tn, (tile + 1) * tn):
                    acc_ref[:, at:at + c1 - c0] += _tn(av, p_refs[p][:, c0:c1])

        @pl.when(k == nk - 1)
        def _():
            for s in range(per):
                o_ref[s] = acc_ref[:, s * C:(s + 1) * C].astype(BF16)

    return pl.pallas_call(
        body, grid=(n_tiles, nk), name=name,
        in_specs=[pl.BlockSpec((tk, M), lambda n, k: (k, 0))] + [pl.BlockSpec((tk, wd), lambda n, k: (k, 0)) for wd in widths],
        out_specs=pl.BlockSpec((per, M, C), lambda n, k: (n, 0, 0)), out_shape=jax.ShapeDtypeStruct((shards, M, C), BF16),
        scratch_shapes=[pltpu.VMEM((M, tn), F32)], compiler_params=_params("parallel", "arbitrary"),
    )(a, *pieces)


def tn_mm(a, b, *, shards, tn, tk, name):
    K, M = a.shape
    N = b.shape[1]
    C = N // shards
    per = tn // C
    nk = K // tk

    def body(a_ref, b_ref, o_ref, acc_ref):
        k = pl.program_id(1)

        @pl.when(k == 0)
        def _():
            acc_ref[...] = jnp.zeros_like(acc_ref)

        acc_ref[...] += _tn(a_ref[...].astype(BF16), b_ref[...].astype(BF16))

        @pl.when(k == nk - 1)
        def _():
            if shards > 1:
                for s in range(per):
                    o_ref[s] = acc_ref[:, s * C:(s + 1) * C].astype(BF16)
            else:
                o_ref[...] = acc_ref[...].astype(BF16)

    if shards > 1:
        out_spec = pl.BlockSpec((per, M, C), lambda n, k: (n, 0, 0))
        out_shape = jax.ShapeDtypeStruct((shards, M, C), BF16)
    else:
        out_spec = pl.BlockSpec((M, tn), lambda n, k: (0, n))
        out_shape = jax.ShapeDtypeStruct((M, N), BF16)
    return pl.pallas_call(
        body, grid=(N // tn, nk), name=name,
        in_specs=[pl.BlockSpec((tk, M), lambda n, k: (k, 0)), pl.BlockSpec((tk, tn), lambda n, k: (k, n))],
        out_specs=out_spec, out_shape=out_shape, scratch_shapes=[pltpu.VMEM((M, tn), F32)],
        compiler_params=_params("parallel", "arbitrary"),
    )(a, b)


def mm_nt_norm_bwd(dy, w, x, g, dx_in, *, tm, name, hook=Hook()):
    S, D = x.shape
    J, _, C = w.shape
    has_in = dx_in is not None
    dys = list(dy) if isinstance(dy, (list, tuple)) else [dy]
    widths = [d.shape[1] for d in dys]
    nd = len(dys)
    n_in, nh, nho = nd + 3 + has_in, len(hook.ins), len(hook.out_shapes)

    def body(*refs):
        dy_refs = refs[:nd]
        w_ref, x_ref, g_ref = refs[nd:nd + 3]
        dxin_ref = refs[nd + 3] if has_in else None
        rest = refs[n_in:]
        h_in, (dx_ref, dg_ref), h_out, h_sems = rest[:nh], rest[nh:nh + 2], rest[nh + 2:nh + 2 + nho], rest[nh + 2 + nho:]
        _run_hook(hook, "before", h_in, h_out, h_sems)

        @pl.when(pl.program_id(0) == 0)
        def _():
            dg_ref[...] = jnp.zeros_like(dg_ref)

        dh = None
        for j in range(J):
            for p, c0, c1, at in _overlaps(widths, j * C, (j + 1) * C):
                part = _nt(dy_refs[p][:, c0:c1], w_ref[j, :, at:at + c1 - c0])
                dh = part if dh is None else dh + part
        rstd, xhat = _rms_stats(x_ref[...])
        dx, dgc = _rms_bwd(xhat, rstd, g_ref[...], dh)
        dx_ref[...] = dx + dxin_ref[...] if has_in else dx
        dg_ref[...] += jnp.sum(dgc, axis=0, keepdims=True)
        _run_hook(hook, "after", h_in, h_out, h_sems)

    row = lambda n: pl.BlockSpec((tm, n), lambda i: (i, 0))
    ins = dys + [w, x, g] + ([dx_in] if has_in else [])
    return _hooked_call(
        body, hook, n_in=n_in, n_out=2, grid=(S // tm,), name=name,
        in_specs=[row(wd) for wd in widths] + [_full(w.shape), row(D), _full((1, D))] + ([row(D)] if has_in else []),
        out_specs=[row(D), _acc((1, D))],
        out_shape=[jax.ShapeDtypeStruct((S, D), F32), jax.ShapeDtypeStruct((1, D), F32)],
        compiler_params=_params("arbitrary"),
    )(*ins, *hook.ins)


def xattn_bwd(dx2, xq, o, lse, kv, w_xo, *, tm):
    S, D = xq.shape
    M = kv.shape[0]
    dh = XATT_HEAD_DIM

    def body(dx_ref, q_ref, o_ref, lse_ref, kv_ref, wo_ref, dq_ref, dkv_ref):
        @pl.when(pl.program_id(0) == 0)
        def _():
            dkv_ref[...] = jnp.zeros_like(dkv_ref)

        do = _nt(dx_ref[...].astype(BF16), wo_ref[...])
        for h in range(XATT_HEADS):
            hs = slice(h * dh, (h + 1) * dh)
            vs = slice(D + h * dh, D + (h + 1) * dh)
            q, k, v = q_ref[:, hs], kv_ref[:, hs], kv_ref[:, vs]
            do_h = do[:, hs]
            do_b = do_h.astype(BF16)
            p = jnp.exp(_nt(q, k) - lse_ref[:, h:h + 1])
            delta = jnp.sum(do_h * o_ref[:, hs].astype(F32), axis=-1, keepdims=True)
            ds = (p * (_nt(do_b, v) - delta)).astype(BF16)
            dq_ref[:, hs] = (_nn(ds, k) * (dh ** -0.5)).astype(BF16)
            dkv_ref[:, hs] += _tn(ds, q)
            dkv_ref[:, vs] += _tn(p.astype(BF16), do_b)

    row = lambda n: pl.BlockSpec((tm, n), lambda i: (i, 0))
    return pl.pallas_call(
        body, grid=(S // tm,), name="xattn_bwd",
        in_specs=[row(D), row(D), row(D), row(XATT_HEADS), _full(kv.shape), _full(w_xo.shape)],
        out_specs=[row(D), _acc((M, 2 * D))],
        out_shape=[jax.ShapeDtypeStruct((S, D), BF16), jax.ShapeDtypeStruct((M, 2 * D), F32)],
        compiler_params=_params("arbitrary"),
    )(dx2, xq, o, lse, kv, w_xo)


def merge_bwd(dx1, w_out, a, b, z, w_ba, w_bs, *, tm, hook=Hook()):
    S, D = dx1.shape
    J, W, C = w_ba.shape
    nh, nho = len(hook.ins), len(hook.out_shapes)

    def body(dx_ref, wo_ref, a_ref, b_ref, g0, g1, g2, g3, wba_ref, wbs_ref, *rest):
        h_in, (da_ref, db_ref, dg_ref, dya_ref, dys_ref) = rest[:nh], rest[nh:nh + 5]
        h_out, h_sems = rest[nh + 5:nh + 5 + nho], rest[nh + 5 + nho:]
        _run_hook(hook, "before", h_in, h_out, h_sems)
        dm = _nt(dx_ref[...].astype(BF16), wo_ref[...])
        sa = _sigmoid(jnp.concatenate([g0[...], g1[...]], axis=1))
        sb = _sigmoid(jnp.concatenate([g2[...], g3[...]], axis=1))
        dg_ref[:, :D] = (dm * (a_ref[...] * (sa * (1.0 - sa))).astype(F32)).astype(BF16)
        dg_ref[:, D:] = (dm * (b_ref[...] * (sb * (1.0 - sb))).astype(F32)).astype(BF16)
        da = (dm * sa.astype(F32)).astype(BF16)
        db = (dm * sb.astype(F32)).astype(BF16)
        da_ref[...] = da
        db_ref[...] = db
        dya = _nt(da[:, 0:C], wba_ref[0])
        dys = _nt(db[:, 0:C], wbs_ref[0])
        for j in range(1, J):
            dya += _nt(da[:, j * C:(j + 1) * C], wba_ref[j])
            dys += _nt(db[:, j * C:(j + 1) * C], wbs_ref[j])
        dya_ref[...] = dya.astype(BF16)
        dys_ref[...] = dys.astype(BF16)
        _run_hook(hook, "after", h_in, h_out, h_sems)

    row = lambda n: pl.BlockSpec((tm, n), lambda i: (i, 0))
    return _hooked_call(
        body, hook, n_in=10, n_out=5, grid=(S // tm,), name="merge_bwd",
        in_specs=[row(D), _full(w_out.shape), row(D), row(D)] + _gate_specs(tm) + [_full(w_ba.shape), _full(w_bs.shape)],
        out_specs=[row(D), row(D), row(2 * D), row(W), row(W)],
        out_shape=[jax.ShapeDtypeStruct((S, D), BF16)] * 2 + [jax.ShapeDtypeStruct((S, 2 * D), BF16)]
        + [jax.ShapeDtypeStruct((S, W), BF16)] * 2,
        compiler_params=_params("arbitrary"),
    )(dx1, w_out, a, b, z, z, z, z, w_ba, w_bs, *hook.ins)


def attn_bwd(z, y, dy, lse, bias, hook=Hook()):
    S = z.shape[0]
    nt = S // ATT_TILE
    back = ATT_WIN - 1
    n_in, nh, nho = 5 + 2 * ATT_WIN, len(hook.ins), len(hook.out_shapes)

    def body(q_ref, *refs):
        k_refs, v_refs = refs[:ATT_WIN], refs[ATT_WIN:2 * ATT_WIN]
        y_ref, dy_ref, lse_ref, bias_ref = refs[2 * ATT_WIN:2 * ATT_WIN + 4]
        rest = refs[2 * ATT_WIN + 4:]
        h_in, (dqkv_ref, dbias_ref), h_out = rest[:nh], rest[nh:nh + 2], rest[nh + 2:nh + 2 + nho]
        dk_acc, dv_acc, dq_buf = rest[nh + 2 + nho:nh + 5 + nho]
        h_sems = rest[nh + 5 + nho:]
        t = pl.program_id(0)
        _run_hook(hook, "before", h_in, h_out, h_sems)

        @pl.when(t == 0)
        def _():
            dbias_ref[...] = jnp.zeros_like(dbias_ref)
            dk_acc[...] = jnp.zeros_like(dk_acc)
            dv_acc[...] = jnp.zeros_like(dv_acc)
            dq_buf[...] = jnp.zeros_like(dq_buf)

        @pl.when(t < nt)
        def _():
            valid = _att_valid(t)
            q_t = (q_ref[...] * ATT_SCALE).astype(F32).T.astype(BF16)
            do_t = dy_ref[...].astype(F32).T.astype(BF16)
            for h in range(ATT_HEADS):
                sl = slice(h * ATT_HEAD_DIM, (h + 1) * ATT_HEAD_DIM)
                k = jnp.concatenate([r[:, sl] for r in k_refs], axis=0)
                v = jnp.concatenate([r[:, sl] for r in v_refs], axis=0)
                q, do_b = q_ref[:, sl] * ATT_SCALE, dy_ref[:, sl]
                p = jnp.exp(_att_scores(q, k, bias_ref[h], valid) - lse_ref[:, h:h + 1])
                delta = jnp.sum(do_b.astype(F32) * y_ref[:, sl].astype(F32), axis=-1, keepdims=True)
                ds = p * (_nt(do_b, v) - delta)
                dbias_ref[h, :, 0:ATT_TILE] += ds[:, 0:ATT_TILE] + ds[:, ATT_TILE:2 * ATT_TILE] + ds[:, 2 * ATT_TILE:ATT_FAR]
                dbias_ref[h, :, ATT_TILE:] += ds[:, ATT_FAR:]
                ds_b = ds.astype(BF16)
                dq_buf[t % ATT_WIN, :, sl] = (_nn(ds_b, k) * ATT_SCALE).astype(BF16)
                dk_w = _nn(q_t[sl, :], ds_b)
                dv_w = _nn(do_t[sl, :], p.astype(BF16))
                for i in range(ATT_WIN):
                    slot = (t + 1 + i) % ATT_WIN
                    cols = slice(i * ATT_TILE, (i + 1) * ATT_TILE)
                    if i == back:
                        dk_acc[slot, sl, :] = dk_w[:, cols]
                        dv_acc[slot, sl, :] = dv_w[:, cols]
                    else:
                        dk_acc[slot, sl, :] += dk_w[:, cols]
                        dv_acc[slot, sl, :] += dv_w[:, cols]

        done = (t + 1) % ATT_WIN
        dqkv_ref[:, 0:512] = dq_buf[done]
        dqkv_ref[:, 512:1024] = dk_acc[done].T.astype(BF16)
        dqkv_ref[:, 1024:1536] = dv_acc[done].T.astype(BF16)
        _run_hook(hook, "after", h_in, h_out, h_sems)

    last = nt - 1
    dbias_shape = (ATT_HEADS, ATT_TILE, ATT_TILE + ATT_WIN * ATT_TILE - ATT_FAR)
    tile = lambda col, n=512: pl.BlockSpec((ATT_TILE, n), lambda t: (jnp.minimum(t, last), col))
    late = pl.BlockSpec((ATT_TILE, 1536), lambda t: (jnp.maximum(t - back, 0), 0))
    return _hooked_call(
        body, hook, n_in=n_in, n_out=2, grid=(nt + back,), name="attn_bwd",
        in_specs=[tile(0)] + _att_window_specs(nt, 1) + _att_window_specs(nt, 2)
        + [tile(0), tile(0), tile(0, ATT_HEADS), _full(bias.shape)],
        out_specs=[late, _acc(dbias_shape)],
        out_shape=[jax.ShapeDtypeStruct((S, 1536), BF16), jax.ShapeDtypeStruct(dbias_shape, F32)],
        scratch_shapes=[pltpu.VMEM((ATT_WIN, 512, ATT_TILE), F32)] * 2 + [pltpu.VMEM((ATT_WIN, ATT_TILE, 512), BF16)],
        compiler_params=_params("arbitrary"),
    )(z, *([z] * (2 * ATT_WIN)), y, dy, lse, bias, *hook.ins)


def sgu_bwd(z, dy, amean, lng, lnb, wpair, wtpair, bfull, maskpair, *, tm, hook=Hook()):
    S = z.shape[0]
    n = S // tm
    nh, nho = len(hook.ins), len(hook.out_shapes)

    def body(zu_ref, zv_ref, dy_ref, a_ref, lng_ref, lnb_ref, wpair_ref, wtpair_ref, bfull_ref, mask_ref, *rest):
        h_in, (duv_ref, dw_ref, dsgb_ref, dlng_ref, dlnb_ref), h_out = rest[:nh], rest[nh:nh + 5], rest[nh + 5:nh + 5 + nho]
        (xhat_scr, rstd_scr, vn_scr, dxh_scr, vbd, dbd, w_acc, b_acc, g_acc, s_acc) = rest[nh + 5 + nho:nh + 15 + nho]
        h_sems = rest[nh + 15 + nho:]
        i = pl.program_id(0)
        _run_hook(hook, "before", h_in, h_out, h_sems)

        @pl.when(i == 0)
        def _():
            for r in (w_acc, b_acc, g_acc, s_acc):
                r[...] = jnp.zeros_like(r)

        xhat_scr[...], rstd_scr[...], vn_scr[...] = _sgu_norm(zv_ref, a_ref, lng_ref, lnb_ref)

        def block(b, carry):
            rows = _sgu_rows(b)
            zu = zu_ref[rows, :].astype(F32)
            u, du = _gelu(zu), _gelu_grad(zu)
            dyv = dy_ref[rows, :].astype(F32)
            for p in range(SG_PAIRS):
                lanes = slice(p * 128, (p + 1) * 128)
                _pair_diag(vn_scr[rows, lanes], vbd.at[p])
                sv = _nn(wpair_ref[p], vbd[p]) + bfull_ref[:, lanes]
                duv_ref[rows, lanes] = (dyv[:, lanes] * sv * du[:, lanes]).astype(BF16)
                dsv = dyv[:, lanes] * u[:, lanes]
                b_acc[:, lanes] += dsv
                w_acc[p] += _nt(dsv.astype(BF16), vbd[p])
                _pair_diag(dsv, dbd.at[p])
                dvn = _nn(wtpair_ref[p], dbd[p])
                g_acc[:, lanes] += dvn * xhat_scr[rows, lanes]
                s_acc[:, lanes] += dvn
                dxh_scr[rows, lanes] = dvn * lng_ref[:, lanes]
            return carry

        lax.fori_loop(0, tm // SG_BLOCK, block, 0)
        dxh, xhat = dxh_scr[...], xhat_scr[...]
        dv = rstd_scr[...] * (dxh - _group_mean(dxh, a_ref) - xhat * _group_mean(dxh * xhat, a_ref))
        duv_ref[:, 512:1024] = (dv * _gelu_grad(zv_ref[...].astype(F32))).astype(BF16)

        @pl.when(i == n - 1)
        def _():
            dw_ref[...] = w_acc[...] * mask_ref[...]
            dsgb_ref[...] = _group_mean(b_acc[...], a_ref) * float(SG_GROUP_DIM)
            dlng_ref[...] = jnp.sum(g_acc[...], axis=0, keepdims=True)
            dlnb_ref[...] = jnp.sum(s_acc[...], axis=0, keepdims=True)

        _run_hook(hook, "after", h_in, h_out, h_sems)

    tile = lambda col: pl.BlockSpec((tm, 512), lambda i: (i, col))
    smalls = [amean, lng, lnb, wpair, wtpair, bfull, maskpair]
    pair_shape = (SG_PAIRS, SG_BLOCK, 2 * SG_BLOCK)
    return _hooked_call(
        body, hook, n_in=3 + len(smalls), n_out=5, grid=(n,), name="sgu_bwd",
        in_specs=[tile(3), tile(4), tile(0)] + [_full(a.shape) for a in smalls],
        out_specs=[pl.BlockSpec((tm, 1024), lambda i: (i, 0)), _acc(pair_shape), _acc((SG_BLOCK, 512)),
                   _acc((1, 512)), _acc((1, 512))],
        out_shape=[jax.ShapeDtypeStruct((S, 1024), BF16), jax.ShapeDtypeStruct(pair_shape, F32),
                   jax.ShapeDtypeStruct((SG_BLOCK, 512), F32), jax.ShapeDtypeStruct((1, 512), F32),
                   jax.ShapeDtypeStruct((1, 512), F32)],
        scratch_shapes=[pltpu.VMEM((tm, 512), F32)] * 4
        + [pltpu.VMEM((SG_PAIRS, 2 * SG_BLOCK, 128), BF16)] * 2
        + [pltpu.VMEM(pair_shape, F32)] + [pltpu.VMEM((SG_BLOCK, 512), F32)] * 3,
        compiler_params=_params("arbitrary"),
    )(z, z, dy, *smalls, *hook.ins)


def bias_colsum(p, far_part):
    H, _, L = p.shape
    near_lo = (ATT_WIN - 1) * ATT_TILE - REL_CLIP + 1
    near_hi = ATT_WIN * ATT_TILE

    def body(p_ref, f_ref, gw_ref, far_ref):
        k = lax.broadcasted_iota(jnp.int32, (1, L), 1)
        is_far = (k < near_lo) | (k >= near_hi)
        for h in range(H):
            g = jnp.sum(p_ref[h], axis=0, keepdims=True)
            gw_ref[h:h + 1, :] = g
            far_ref[h:h + 1, :] = jnp.zeros((1, 128), F32) + (jnp.sum(jnp.where(is_far, g, 0.0)) + jnp.sum(f_ref[h]))

    return pl.pallas_call(
        body, name="bias_colsum", in_specs=[_full(p.shape), _full(far_part.shape)],
        out_specs=[_acc((H, L)), _acc((H, 128))],
        out_shape=[jax.ShapeDtypeStruct((H, L), F32), jax.ShapeDtypeStruct((H, 128), F32)], grid=(1,),
        compiler_params=_params("arbitrary"),
    )(p, far_part)


def _row_tile(rows, cols):
    best = None
    for tr in range(16, rows + 1, 16):
        if rows % tr == 0 and tr * cols * 4 <= EW_BLOCK_BYTES:
            best = tr
    return best if best is not None else rows


def place_shard(w, own, name):
    R, C = w.shape
    r = R // 2
    tr = _row_tile(r, C)
    nr = r // tr

    def body(own_ref, w_ref, o_ref):
        o_ref[...] = w_ref[...].astype(BF16)

    return pl.pallas_call(
        body, name=name,
        grid_spec=pltpu.PrefetchScalarGridSpec(
            num_scalar_prefetch=1, grid=(2, nr),
            in_specs=[pl.BlockSpec((tr, C), lambda h, i, own: (h * nr + i, 0))],
            out_specs=pl.BlockSpec((None, None, tr, C), lambda h, i, own: (own[0], h, i, 0))),
        out_shape=jax.ShapeDtypeStruct((4, 2, r, C), BF16), compiler_params=_params("parallel", "parallel"),
    )(own, w)


def chip_partial(grad, recv, core, name):
    _, _, r, C = grad.shape
    tr = _row_tile(r, C)

    def body(core_ref, g_ref, r_ref, o_ref):
        o_ref[...] = (g_ref[...].astype(F32) + r_ref[...].astype(F32)).astype(BF16)

    spec = pl.BlockSpec((None, tr, C), lambda j, i, core: (j, i, 0))
    return pl.pallas_call(
        body, name=name,
        grid_spec=pltpu.PrefetchScalarGridSpec(
            num_scalar_prefetch=1, grid=(4, r // tr),
            in_specs=[pl.BlockSpec((None, None, tr, C), lambda j, i, core: (j, core[0], i, 0)), spec], out_specs=spec),
        out_shape=jax.ShapeDtypeStruct((4, r, C), BF16), compiler_params=_params("parallel", "parallel"),
    )(core, grad, recv)


def shard_sum(part, recv, own, core, name):
    _, r, C = part.shape
    tr = _row_tile(r, C)

    def body(own_ref, core_ref, p_ref, r0, r1, r2, o_ref):
        o_ref[...] = p_ref[...].astype(F32) + r0[...].astype(F32) + r1[...].astype(F32) + r2[...].astype(F32)

    return pl.pallas_call(
        body, name=name,
        grid_spec=pltpu.PrefetchScalarGridSpec(
            num_scalar_prefetch=2, grid=(r // tr,),
            in_specs=[pl.BlockSpec((None, tr, C), lambda i, own, core: (own[0], i, 0))]
            + [pl.BlockSpec((None, tr, C), lambda i, own, core, k=k: (k, i, 0)) for k in range(3)],
            out_specs=pl.BlockSpec((None, tr, C), lambda i, own, core: (core[0], i, 0))),
        out_shape=jax.ShapeDtypeStruct((2, r, C), F32), compiler_params=_params("parallel"),
    )(own, core, part, recv, recv, recv)


def adamw(w, g, m, v, name):
    R, C = w.shape
    tr = _row_tile(R, C)

    def body(w_ref, g_ref, m_ref, v_ref, d_ref, nm_ref, nv_ref):
        gv = g_ref[...]
        nm = ADAM_B1 * m_ref[...] + (1.0 - ADAM_B1) * gv
        nv = ADAM_B2 * v_ref[...] + (1.0 - ADAM_B2) * (gv * gv)
        m_hat = nm / (1.0 - ADAM_B1 ** ADAM_STEP)
        v_hat = nv / (1.0 - ADAM_B2 ** ADAM_STEP)
        d_ref[...] = -ADAM_LR * (m_hat / (jnp.sqrt(v_hat) + ADAM_EPS) + ADAM_WD * w_ref[...])
        nm_ref[...] = nm
        nv_ref[...] = nv

    spec = pl.BlockSpec((tr, C), lambda i: (i, 0))
    return pl.pallas_call(
        body, grid=(R // tr,), name=name, in_specs=[spec] * 4, out_specs=[spec] * 3,
        out_shape=[jax.ShapeDtypeStruct((R, C), F32)] * 3, compiler_params=_params("parallel"),
    )(w, g, m, v)


HBM = pl.BlockSpec(memory_space=pl.ANY)


def _place():
    x, y, c = lax.axis_index("x"), lax.axis_index("y"), lax.axis_index("c")
    chips = [(1 - x, y), (x, 1 - y), (1 - x, 1 - y)]
    return x, y, c, chips


def _gather_phases(n):
    def copies(buf, sems, kind):
        send_sems, recv_sems = sems
        x, y, c, chips = _place()

        def remote(w, k, slot, to):
            return pltpu.make_async_remote_copy(src_ref=slot, dst_ref=slot, send_sem=send_sems.at[w, k],
                                                recv_sem=recv_sems.at[w, k], device_id=to, device_id_type=MESH)

        def one(w, k, px, py):
            if kind == "mine":
                return remote(w, k, buf[w].at[2 * x + y, c], (px, py, c))
            if kind == "theirs":
                return remote(w, k, buf[w].at[2 * px + py, c], (px, py, c))
            if kind == "onward":
                return remote(w, 3 + k, buf[w].at[2 * px + py, c], (x, y, 1 - c))
            return remote(w, 3 + k, buf[w].at[2 * px + py, 1 - c], (x, y, 1 - c))

        return [one(w, k, px, py) for w in range(n) for k, (px, py) in enumerate(chips)]

    def start(_, buf, sems):
        for cp in copies(buf, sems, "mine"):
            cp.start()

    def relay(_, buf, sems):
        for theirs, onward in zip(copies(buf, sems, "theirs"), copies(buf, sems, "onward")):
            theirs.wait_recv()
            onward.start()

    def finish(_, buf, sems):
        for cp in copies(buf, sems, "relayed"):
            cp.wait_recv()
        for cp in copies(buf, sems, "mine") + copies(buf, sems, "onward"):
            cp.wait_send()

    return start, relay, finish


def _gather_sems(n):
    return (pltpu.SemaphoreType.DMA((n, 6)), pltpu.SemaphoreType.DMA((n, 6)))


def gather_weights(bufs):
    n = len(bufs)
    phases = _gather_phases(n)

    def body(*refs):
        for phase in phases:
            phase(None, refs[n:2 * n], refs[2 * n:])

    return pl.pallas_call(
        body, name="gather_weights", in_specs=[HBM] * n, out_specs=[HBM] * n,
        out_shape=[jax.ShapeDtypeStruct(b.shape, b.dtype) for b in bufs],
        input_output_aliases={i: i for i in range(n)}, scratch_shapes=list(_gather_sems(n)),
    )(*bufs)


def gather_hook(bufs, n_steps):
    start, relay, finish = _gather_phases(len(bufs))
    return Hook(ins=tuple(bufs), out_shapes=tuple(jax.ShapeDtypeStruct(b.shape, b.dtype) for b in bufs),
                alias=tuple((i, i) for i in range(len(bufs))),
                sems=_gather_sems(len(bufs)),
                steps=((0, "before", start), ((n_steps * 13) // 16, "before", relay), (n_steps - 1, "after", finish)))


def sibling_split(grads, name):
    n = len(grads)

    def body(*refs):
        src, dst = refs[:n], refs[n:2 * n]
        send_sems, recv_sems = refs[2 * n:]
        x, y, c, _ = _place()
        sends = [pltpu.make_async_remote_copy(src_ref=src[w].at[:, 1 - c], dst_ref=dst[w], send_sem=send_sems.at[w],
                                              recv_sem=recv_sems.at[w], device_id=(x, y, 1 - c), device_id_type=MESH)
                 for w in range(n)]
        for cp in sends:
            cp.start()
        for cp in sends:
            cp.wait()

    return pl.pallas_call(
        body, name=name, in_specs=[HBM] * n, out_specs=[HBM] * n,
        out_shape=[jax.ShapeDtypeStruct((4,) + g.shape[2:], g.dtype) for g in grads],
        scratch_shapes=[pltpu.SemaphoreType.DMA((n,))] * 2,
    )(*grads)


def split_hook(grads, n_steps):
    n = len(grads)

    def copies(src, dst, sems):
        send_sems, recv_sems = sems
        x, y, c, _ = _place()
        return [pltpu.make_async_remote_copy(src_ref=src[w].at[:, 1 - c], dst_ref=dst[w], send_sem=send_sems.at[w],
                                             recv_sem=recv_sems.at[w], device_id=(x, y, 1 - c), device_id_type=MESH)
                for w in range(n)]

    def start(src, dst, sems):
        for cp in copies(src, dst, sems):
            cp.start()

    def finish(src, dst, sems):
        for cp in copies(src, dst, sems):
            cp.wait()

    return Hook(ins=tuple(grads), out_shapes=tuple(jax.ShapeDtypeStruct((4,) + g.shape[2:], g.dtype) for g in grads),
                sems=(pltpu.SemaphoreType.DMA((n,)), pltpu.SemaphoreType.DMA((n,))),
                steps=((0, "before", start), (n_steps - 1, "after", finish)))


def exchange_hook(parts, n_steps):
    n = len(parts)

    def copies(src, dst, sems):
        send_sems, recv_sems = sems
        _, _, c, chips = _place()
        return [pltpu.make_async_remote_copy(
            src_ref=src[w].at[2 * px + py], dst_ref=dst[w].at[k], send_sem=send_sems.at[w, k],
            recv_sem=recv_sems.at[w, k], device_id=(px, py, c), device_id_type=MESH)
            for w in range(n) for k, (px, py) in enumerate(chips)]

    def start(src, dst, sems):
        for cp in copies(src, dst, sems):
            cp.start()

    def finish(src, dst, sems):
        for cp in copies(src, dst, sems):
            cp.wait()

    return Hook(ins=tuple(parts), out_shapes=tuple(jax.ShapeDtypeStruct((3,) + p.shape[1:], p.dtype) for p in parts),
                sems=(pltpu.SemaphoreType.DMA((n, 3)), pltpu.SemaphoreType.DMA((n, 3))),
                steps=((0, "before", start), (n_steps - 1, "after", finish)))


def sibling_join(sums):
    n = len(sums)

    def body(*refs):
        buf = refs[n:2 * n]
        send_sems, recv_sems = refs[2 * n:]
        x, y, c, _ = _place()
        sends = [pltpu.make_async_remote_copy(src_ref=buf[w].at[c], dst_ref=buf[w].at[c], send_sem=send_sems.at[w],
                                              recv_sem=recv_sems.at[w], device_id=(x, y, 1 - c), device_id_type=MESH)
                 for w in range(n)]
        for cp in sends:
            cp.start()
        for w, cp in enumerate(sends):
            cp.wait_send()
            pltpu.make_async_remote_copy(src_ref=buf[w].at[c], dst_ref=buf[w].at[1 - c], send_sem=send_sems.at[w],
                                         recv_sem=recv_sems.at[w], device_id=(x, y, 1 - c), device_id_type=MESH).wait_recv()

    return pl.pallas_call(
        body, name="sibling_join", in_specs=[HBM] * n, out_specs=[HBM] * n,
        out_shape=[jax.ShapeDtypeStruct(s.shape, s.dtype) for s in sums],
        input_output_aliases={i: i for i in range(n)},
        scratch_shapes=[pltpu.SemaphoreType.DMA((n,))] * 2,
    )(*sums)


def join_hook(sums, n_steps):
    n = len(sums)

    def copies(buf, sems, recv):
        send_sems, recv_sems = sems
        x, y, c, _ = _place()
        half = (1 - c) if recv else c
        return [pltpu.make_async_remote_copy(src_ref=buf[w].at[half], dst_ref=buf[w].at[half], send_sem=send_sems.at[w],
                                             recv_sem=recv_sems.at[w], device_id=(x, y, 1 - c), device_id_type=MESH)
                for w in range(n)]

    def start(_, buf, sems):
        for cp in copies(buf, sems, False):
            cp.start()

    def finish(_, buf, sems):
        for cp in copies(buf, sems, False):
            cp.wait_send()
        for cp in copies(buf, sems, True):
            cp.wait_recv()

    return Hook(ins=tuple(sums), out_shapes=tuple(jax.ShapeDtypeStruct(s.shape, s.dtype) for s in sums),
                alias=tuple((i, i) for i in range(n)),
                sems=(pltpu.SemaphoreType.DMA((n,)), pltpu.SemaphoreType.DMA((n,))),
                steps=((0, "before", start), (n_steps - 1, "after", finish)))


def allreduce_small(v):
    R, C = v.shape

    def body(v_ref, out_ref, all_ref, send_sems, recv_sems):
        x, y, c, chips = _place()
        me, sibling = (x, y, c), (x, y, 1 - c)

        def slot(px, py, pc):
            return all_ref.at[4 * px + 2 * py + pc]

        def copy(k, block, to, src=None):
            return pltpu.make_async_remote_copy(src_ref=slot(*block) if src is None else src, dst_ref=slot(*block),
                                                send_sem=send_sems.at[k], recv_sem=recv_sems.at[k], device_id=to,
                                                device_id_type=MESH)

        first = [copy(0, me, sibling, src=v_ref)] + [copy(1 + j, me, (*chip, c), src=v_ref) for j, chip in enumerate(chips)]
        for cp in first:
            cp.start()
        slot(*me)[...] = v_ref[...]
        passed = [copy(4 + j, (*chip, c), sibling) for j, chip in enumerate(chips)]
        for j, chip in enumerate(chips):
            copy(1 + j, (*chip, c), me).wait_recv()
            passed[j].start()
        copy(0, sibling, me).wait_recv()
        for j, chip in enumerate(chips):
            copy(4 + j, (*chip, 1 - c), me).wait_recv()
        for cp in first + passed:
            cp.wait_send()
        acc = all_ref[0]
        for k in range(1, 8):
            acc = acc + all_ref[k]
        out_ref[...] = acc

    vmem = pl.BlockSpec(memory_space=pltpu.VMEM)
    return pl.pallas_call(
        body, name="allreduce_small", in_specs=[vmem], out_specs=vmem, out_shape=jax.ShapeDtypeStruct((R, C), F32),
        scratch_shapes=[pltpu.VMEM((8, R, C), F32), pltpu.SemaphoreType.DMA((7,)), pltpu.SemaphoreType.DMA((7,))],
        compiler_params=pltpu.CompilerParams(vmem_limit_bytes=VMEM_LIMIT_V7X),
    )(v)


def allgather_hook(v, n_steps):
    R, C = v.shape

    def copies(v_ref, out, sems, kind):
        send_sems, recv_sems, _ = sems
        x, y, c, chips = _place()
        me, sibling = (x, y, c), (x, y, 1 - c)

        def slot(px, py, pc):
            return out.at[4 * px + 2 * py + pc]

        def copy(k, block, to, src=None):
            return pltpu.make_async_remote_copy(src_ref=slot(*block) if src is None else src, dst_ref=slot(*block),
                                                send_sem=send_sems.at[k], recv_sem=recv_sems.at[k], device_id=to,
                                                device_id_type=MESH)

        if kind == "mine":
            return [copy(0, me, sibling, src=v_ref)] + [copy(1 + j, me, (*chip, c), src=v_ref) for j, chip in enumerate(chips)]
        if kind == "theirs":
            return [copy(1 + j, (*chip, c), me) for j, chip in enumerate(chips)]
        if kind == "onward":
            return [copy(4 + j, (*chip, c), sibling) for j, chip in enumerate(chips)]
        return [copy(0, sibling, me)] + [copy(4 + j, (*chip, 1 - c), me) for j, chip in enumerate(chips)]

    def own(v_ref, out, sems):
        x, y, c, _ = _place()
        return pltpu.make_async_copy(v_ref, out.at[4 * x + 2 * y + c], sems[2])

    def start(ins, outs, sems):
        own(ins[0], outs[0], sems).start()
        for cp in copies(ins[0], outs[0], sems, "mine"):
            cp.start()

    def relay(ins, outs, sems):
        for theirs, onward in zip(copies(ins[0], outs[0], sems, "theirs"), copies(ins[0], outs[0], sems, "onward")):
            theirs.wait_recv()
            onward.start()

    def finish(ins, outs, sems):
        for cp in copies(ins[0], outs[0], sems, "relayed"):
            cp.wait_recv()
        for cp in copies(ins[0], outs[0], sems, "mine") + copies(ins[0], outs[0], sems, "onward"):
            cp.wait_send()
        own(ins[0], outs[0], sems).wait()

    return Hook(ins=(v,), out_shapes=(jax.ShapeDtypeStruct((8, R, C), F32),),
                sems=(pltpu.SemaphoreType.DMA((7,)), pltpu.SemaphoreType.DMA((7,)), pltpu.SemaphoreType.DMA(())),
                steps=((0, "before", start), ((n_steps * 13) // 16, "before", relay), (n_steps - 1, "after", finish)))


def merge_hooks(a, b):
    cut = (len(a.ins), len(a.out_shapes), len(a.sems))

    def part(fn, second):
        def run(ins, outs, sems):
            if second:
                fn(ins[cut[0]:], outs[cut[1]:], sems[cut[2]:])
            else:
                fn(ins[:cut[0]], outs[:cut[1]], sems[:cut[2]])
        return run

    steps = tuple((s, w, part(fn, False)) for s, w, fn in a.steps) + tuple((s, w, part(fn, True)) for s, w, fn in b.steps)
    alias = a.alias + tuple((i + cut[0], o + cut[1]) for i, o in b.alias)
    return Hook(a.ins + b.ins, a.out_shapes + b.out_shapes, alias, a.sems + b.sems, steps)


def sum_blocks(g):
    n, R, C = g.shape

    def body(g_ref, o_ref):
        acc = g_ref[0]
        for k in range(1, n):
            acc = acc + g_ref[k]
        o_ref[...] = acc

    return pl.pallas_call(body, name="sum_blocks", grid=(1,), in_specs=[_full(g.shape)], out_specs=_acc((R, C)),
                          out_shape=jax.ShapeDtypeStruct((R, C), F32), compiler_params=_params("arbitrary"))(g)


SMALL_PAD = 1024


def _pack_small(arrs):
    parts = []
    for a in arrs:
        f = a.reshape(-1).astype(F32)
        parts.append(jnp.pad(f, (0, (-f.shape[0]) % SMALL_PAD)))
    return jnp.concatenate(parts).reshape(-1, 128)


def _unpack_small(packed, shapes):
    flat = packed.reshape(-1)
    outs, off = [], 0
    for s in shapes:
        size = int(np.prod(s))
        outs.append(flat[off:off + size].reshape(s))
        off += size + (-size) % SMALL_PAD
    return outs


ATT_KEYS = ATT_WIN * ATT_TILE
ATT_NEAR_LO = (ATT_WIN - 1) * ATT_TILE - REL_CLIP + 1
ATT_PERIOD = ATT_KEYS + ATT_TILE + 1


def _att_bias_table(rel_bias):
    far = rel_bias[:, 2 * REL_CLIP:]
    near = rel_bias[:, 2 * REL_CLIP - 1:0:-1]
    w = jnp.concatenate([jnp.broadcast_to(far, (ATT_HEADS, ATT_NEAR_LO)), near,
                         jnp.broadcast_to(far, (ATT_HEADS, ATT_PERIOD - ATT_KEYS))], axis=1)
    rows = jnp.tile(w, (1, ATT_TILE))[:, :ATT_TILE * (ATT_PERIOD - 1)].reshape(ATT_HEADS, ATT_TILE, ATT_PERIOD - 1)
    i = np.arange(ATT_TILE)[:, None]
    m = np.arange(ATT_KEYS)[None, :]
    qc, kc = i // CHUNK, m // CHUNK
    band = (kc >= qc) & (kc <= qc + N_PREV_CHUNKS)
    return jnp.where(band[None], rows[:, :, :ATT_KEYS], NEG_INF)


def _rel_bias_grad(dbias):
    p = jnp.pad(dbias[:, :, ATT_TILE:], ((0, 0), (0, 0), (ATT_FAR, ATT_PERIOD - 1 - ATT_KEYS))).reshape(ATT_HEADS, -1)
    p = jnp.pad(p, ((0, 0), (0, ATT_TILE))).reshape(ATT_HEADS, ATT_TILE, ATT_PERIOD)
    gw, far = bias_colsum(p, dbias[:, :, :ATT_TILE])
    return jnp.concatenate([jnp.zeros((ATT_HEADS, 1), F32), gw[:, ATT_KEYS - 1:ATT_NEAR_LO - 1:-1], far[:, :1]], axis=1)


def kernel(x, mem, norm_mix_g, w_in, rel_bias, sg_ln_g, sg_ln_b, sg_w, sg_b, w_branch_att, w_branch_sg, w_out, norm_xattn_g, norm_mem_g, w_xq, w_xkv, w_xo, norm_ffn_g, w_ffn_in, w_ffn_out, norm_final_g, loss_target, m_norm_mix_g, m_w_in, m_rel_bias, m_sg_ln_g, m_sg_ln_b, m_sg_w, m_sg_b, m_w_branch_att, m_w_branch_sg, m_w_out, m_norm_xattn_g, m_norm_mem_g, m_w_xq, m_w_xkv, m_w_xo, m_norm_ffn_g, m_w_ffn_in, m_w_ffn_out, m_norm_final_g, v_norm_mix_g, v_w_in, v_rel_bias, v_sg_ln_g, v_sg_ln_b, v_sg_w, v_sg_b, v_w_branch_att, v_w_branch_sg, v_w_out, v_norm_xattn_g, v_norm_mem_g, v_w_xq, v_w_xkv, v_w_xo, v_norm_ffn_g, v_w_ffn_in, v_w_ffn_out, v_norm_final_g):
    S, D = x.shape[1], x.shape[2]
    x2d, mem2d, tgt = x[0], mem[0], loss_target[0]

    big_names = ["w_in", "w_branch_att", "w_branch_sg", "w_out", "w_xq", "w_xkv", "w_xo", "w_ffn_in", "w_ffn_out"]
    col_sharded = [True, True, True, False, False, True, False, True, False]
    big_w = [a[0] for a in (w_in, w_branch_att, w_branch_sg, w_out, w_xq, w_xkv, w_xo, w_ffn_in, w_ffn_out)]
    big_m = [a[0] for a in (m_w_in, m_w_branch_att, m_w_branch_sg, m_w_out, m_w_xq, m_w_xkv, m_w_xo, m_w_ffn_in, m_w_ffn_out)]
    big_v = [a[0] for a in (v_w_in, v_w_branch_att, v_w_branch_sg, v_w_out, v_w_xq, v_w_xkv, v_w_xo, v_w_ffn_in, v_w_ffn_out)]

    own = (2 * lax.axis_index("x") + lax.axis_index("y")).astype(jnp.int32).reshape(1)
    core = lax.axis_index("c").astype(jnp.int32).reshape(1)
    placed = [place_shard(w, own, "place_" + nm) for w, nm in zip(big_w, big_names)]

    def whole(g4, i):
        R, C = big_w[i].shape
        return g4.reshape(4, R, C) if col_sharded[i] else g4.reshape(4 * R, C)

    W_in = whole(gather_weights(placed[:1])[0], 0)

    g_mix, g_xat, g_mem, g_ffn = norm_mix_g, norm_xattn_g, norm_mem_g, norm_ffn_g
    g_fin = norm_final_g.reshape(1, D)
    bias = _att_bias_table(rel_bias[0])
    tt = np.arange(SG_BLOCK)
    sg_mask = (tt[None, :] // CHUNK) <= (tt[:, None] // CHUNK)
    wm_f = jnp.where(sg_mask[None], sg_w[0], 0.0)
    pairs = wm_f.reshape(SG_PAIRS, 2, SG_BLOCK, SG_BLOCK)
    wpair = jnp.transpose(pairs, (0, 2, 1, 3)).reshape(SG_PAIRS, SG_BLOCK, 2 * SG_BLOCK).astype(BF16)
    wtpair = jnp.transpose(pairs, (0, 3, 1, 2)).reshape(SG_PAIRS, SG_BLOCK, 2 * SG_BLOCK).astype(BF16)
    maskpair = jnp.asarray(np.tile(sg_mask, (SG_PAIRS, 1, 2)), F32)
    bfull = jnp.repeat(sg_b[0].T, SG_GROUP_DIM, axis=1)
    lng, lnb = sg_ln_g[0].reshape(1, 512), sg_ln_b[0].reshape(1, 512)
    gid = np.arange(512) // SG_GROUP_DIM
    amean = jnp.asarray((gid[:, None] == gid[None, :]) / SG_GROUP_DIM, BF16)

    h1, z, *early = norm_mm(x2d, g_mix, W_in, tm=ROW_TILE, scale=None, name="norm_mm_in",
                            hook=gather_hook(placed[1:7], S // ROW_TILE))
    y_att, lse_att, *late = attn_fwd(z, bias, gather_hook(placed[7:], S // ATT_TILE))
    W_ba, W_bs, W_out, W_xq, W_xkv, W_xo, W_fi, W_fo = [whole(g4, i + 1) for i, g4 in enumerate(early + late)]
    y_sg, = sgu_fwd(z, amean, lng, lnb, wpair, bfull, tm=SGU_TILE)
    a_br, b_br, merged, x1 = merge_fwd(y_att, y_sg, z, x2d, W_ba, W_bs, W_out, tm=ROW_TILE)
    h2, xq = norm_mm(x1, g_xat, W_xq.reshape(1, D, D), tm=XATT_TILE, scale=XATT_HEAD_DIM ** -0.5, name="norm_mm_xq")
    mn, kv = norm_mm(mem2d, g_mem, W_xkv, tm=mem2d.shape[0], scale=None, name="norm_mm_kv")
    o_x, lse_x, x2 = xattn_fwd(xq, kv, W_xo, x1, tm=XATT_TILE)
    h3, gu = norm_mm(x2, g_ffn, W_fi, tm=ROW_TILE, scale=None, name="norm_mm_ffn")
    act, loss_vec, dx3, dg_fin = ffn_out_loss(gu, W_fo, x2, g_fin, tgt, tm=ROW_TILE)

    dgu = ffn_act_bwd(dx3, gu, W_fo, tm=ROW_TILE, nchunk=2)
    gW_fo = tn_mm(act, dx3, shards=1, tn=D, tk=DW_TOKENS, name="dw_ffn_out")
    gW_fi = tn_mm(h3, dgu, shards=4, tn=2 * W_fi.shape[2], tk=DW_TOKENS, name="dw_ffn_in")
    dx2, dg_ffn = mm_nt_norm_bwd(dgu, W_fi, x2, g_ffn, dx3, tm=ROW_TILE, name="dx_ffn")
    dq_x, dkv = xattn_bwd(dx2, xq, o_x, lse_x, kv, W_xo, tm=XATT_TILE)
    gW_xo = tn_mm(o_x, dx2, shards=1, tn=D, tk=DW_TOKENS, name="dw_xo")
    gW_xq = tn_mm(h2, dq_x, shards=1, tn=D, tk=DW_TOKENS, name="dw_xq")
    dx1, dg_xat = mm_nt_norm_bwd(dq_x, W_xq.reshape(1, D, D), x1, g_xat, dx2, tm=XATT_TILE, name="dx_xq")
    gW_xkv = tn_mm(mn, dkv, shards=4, tn=2 * D, tk=mem2d.shape[0], name="dw_xkv")
    _, dg_mem = mm_nt_norm_bwd(dkv.astype(BF16), W_xkv, mem2d, g_mem, None, tm=mem2d.shape[0], name="dx_mem")
    def canon(g, i):
        R, C = big_w[i].shape
        return g.reshape(4, 2, R // 2, C)

    def partials(gs, recv, idx):
        return [chip_partial(g, r, core, "chip_partial_" + big_names[i]) for g, r, i in zip(gs, recv, idx)]

    early_g = [canon(g, i) for g, i in zip([gW_xq, gW_xkv, gW_xo, gW_fi, gW_fo], range(4, 9))]
    da, db, dgab, dy_att, dy_sg, *early_r = merge_bwd(dx1, W_out, a_br, b_br, z, W_ba, W_bs, tm=ROW_TILE,
                                                      hook=split_hook(early_g, S // ROW_TILE))
    gW_out = tn_mm(merged, dx1, shards=1, tn=D, tk=DW_TOKENS, name="dw_out")
    gW_ba = tn_mm(y_att, da, shards=4, tn=D, tk=DW_TOKENS, name="dw_branch_att")
    gW_bs = tn_mm(y_sg, db, shards=4, tn=D, tk=DW_TOKENS, name="dw_branch_sg")
    mid_g = [canon(g, i) for g, i in zip([gW_ba, gW_bs, gW_out], range(1, 4))]
    parts_b = (partials(mid_g, sibling_split(mid_g, "sibling_split_mid"), range(1, 4))
               + partials(early_g, early_r, range(4, 9)))
    nt_bwd = S // ATT_TILE + ATT_WIN - 1
    dqkv, dbias, *from_chips_b = attn_bwd(z, y_att, dy_att, lse_att, bias, exchange_hook(parts_b, nt_bwd))
    sums_b = [shard_sum(p, r, own, core, "shard_sum_" + nm) for p, r, nm in zip(parts_b, from_chips_b, big_names[1:])]
    duv, dwpair, dsgb_full, dlng, dlnb, *joined_b = sgu_bwd(z, dy_sg, amean, lng, lnb, wpair, wtpair, bfull, maskpair,
                                                         tm=SGU_TILE, hook=join_hook(sums_b, S // SGU_TILE))
    dwm = jnp.transpose(dwpair.reshape(SG_PAIRS, SG_BLOCK, 2, SG_BLOCK), (0, 2, 1, 3))
    dsgb = dsgb_full[:, ::SG_GROUP_DIM].T
    dz = [dqkv, duv, dgab]
    gW_in = tn_mm_pieces(h1, dz, shards=4, per=2, tk=DW_TOKENS, name="dw_in")
    in_g = [canon(gW_in, 0)]
    parts_a = partials(in_g, sibling_split(in_g, "sibling_split_w_in"), [0])
    d_rel = _rel_bias_grad(dbias)
    small_g = [d_rel, dlng, dlnb, dwm, dsgb, dg_xat, dg_mem, dg_ffn, dg_fin, loss_vec[0, :1]]
    n_dx = S // ROW_TILE
    dx, dg_mix, from_chips_a, small_all = mm_nt_norm_bwd(
        dz, W_in, x2d, g_mix, dx1, tm=ROW_TILE, name="dx_in",
        hook=merge_hooks(exchange_hook(parts_a, n_dx), allgather_hook(_pack_small(small_g), n_dx)))

    joined = list(sibling_join([shard_sum(parts_a[0], from_chips_a, own, core, "shard_sum_w_in")])) + list(joined_b)
    big_out = []
    for j, w, m_, v_, nm in zip(joined, big_w, big_m, big_v, big_names):
        g = j.reshape(w.shape)
        big_out.append((g,) + tuple(adamw(w, g, m_, v_, "adamw_" + nm)))

    small_w = [norm_mix_g, rel_bias, sg_ln_g, sg_ln_b, sg_w, sg_b, norm_xattn_g, norm_mem_g, norm_ffn_g, norm_final_g]
    small_m = [m_norm_mix_g, m_rel_bias, m_sg_ln_g, m_sg_ln_b, m_sg_w, m_sg_b, m_norm_xattn_g, m_norm_mem_g, m_norm_ffn_g, m_norm_final_g]
    small_v = [v_norm_mix_g, v_rel_bias, v_sg_ln_g, v_sg_ln_b, v_sg_w, v_sg_b, v_norm_xattn_g, v_norm_mem_g, v_norm_ffn_g, v_norm_final_g]
    shapes = [w.shape for w in small_w]
    g_sum = jnp.concatenate([allreduce_small(_pack_small([dg_mix])), sum_blocks(small_all)], axis=0)
    zero = jnp.zeros((1,), F32)
    d_s, m_s, v_s = adamw(_pack_small(small_w + [zero]), g_sum, _pack_small(small_m + [zero]), _pack_small(small_v + [zero]),
                          "adamw_small")
    sg_, sd, sm, sv_ = (_unpack_small(p, shapes + [(1,)]) for p in (g_sum, d_s, m_s, v_s))
    loss = sg_[-1][0]

    order = ["norm_mix_g", "w_in", "rel_bias", "sg_ln_g", "sg_ln_b", "sg_w", "sg_b", "w_branch_att", "w_branch_sg", "w_out",
             "norm_xattn_g", "norm_mem_g", "w_xq", "w_xkv", "w_xo", "norm_ffn_g", "w_ffn_in", "w_ffn_out", "norm_final_g"]
    small_names = ["norm_mix_g", "rel_bias", "sg_ln_g", "sg_ln_b", "sg_w", "sg_b", "norm_xattn_g", "norm_mem_g", "norm_ffn_g",
                   "norm_final_g"]
    res = {}
    for i, nm in enumerate(small_names):
        res[nm] = (sg_[i], sd[i], sm[i], sv_[i])
    for nm, outs in zip(big_names, big_out):
        res[nm] = tuple(o[None] for o in outs)
    return (loss, dx[None], *[res[nm][0] for nm in order], *[res[nm][1] for nm in order],
            *[res[nm][2] for nm in order], *[res[nm][3] for nm in order])
```

```python
import functools
from typing import NamedTuple

import numpy as np
import jax
import jax.numpy as jnp
from jax import lax
from jax.experimental import pallas as pl
from jax.experimental.pallas import tpu as pltpu

F32, BF16 = jnp.float32, jnp.bfloat16
MESH = pl.DeviceIdType.MESH

EPS = 1e-6
NEG_INF = -1e30
CHUNK = 64
N_PREV_CHUNKS = 8
ATT_HEADS, ATT_HEAD_DIM = 8, 64
REL_CLIP = 128
SG_BLOCK, SG_GROUPS, SG_GROUP_DIM = 128, 8, 64
XATT_HEADS, XATT_HEAD_DIM = 4, 256
ATT_TILE = 128
ATT_WIN = 5
ADAM_LR, ADAM_B1, ADAM_B2, ADAM_EPS, ADAM_WD, ADAM_STEP = 0.001, 0.9, 0.999, 1e-08, 0.01, 10

VMEM_LIMIT_V7X = 56 * 1024 * 1024
EW_BLOCK_BYTES = 2 << 20
ROW_TILE = 512
DW_TOKENS = 1024
XATT_TILE = 1024
SGU_TILE = 1024
IN_TILE = 1024
HBM = pl.BlockSpec(memory_space=pl.ANY)


def _params(*sem):
    return pltpu.CompilerParams(dimension_semantics=sem, vmem_limit_bytes=VMEM_LIMIT_V7X)


def _full(shape):
    n = len(shape)
    return pl.BlockSpec(shape, lambda *_: (0,) * n, pipeline_mode=pl.Buffered(1))


def _acc(shape):
    n = len(shape)
    return pl.BlockSpec(shape, lambda *_: (0,) * n)


class Hook(NamedTuple):
    ins: tuple = ()
    out_shapes: tuple = ()
    alias: tuple = ()
    sems: tuple = ()
    steps: tuple = ()


def _run_hook(hook, pos, h_in, h_out, h_sems):
    for at, where, fn in hook.steps:
        if where == pos:
            pl.when(pl.program_id(0) == at)(functools.partial(fn, h_in, h_out, h_sems))


def _hooked_call(body, hook, *, n_in, n_out, in_specs, out_specs, out_shape, scratch_shapes=(), **kw):
    nh = len(hook.ins)
    aliases = {n_in + i: n_out + o for i, o in hook.alias}
    return pl.pallas_call(
        body, in_specs=list(in_specs) + [HBM] * nh, out_specs=list(out_specs) + [HBM] * len(hook.out_shapes),
        out_shape=list(out_shape) + list(hook.out_shapes), scratch_shapes=list(scratch_shapes) + list(hook.sems),
        input_output_aliases=aliases, **kw)


def _nt(a, b):
    return lax.dot_general(a, b, (((1,), (1,)), ((), ())), preferred_element_type=F32)


def _tn(a, b):
    return lax.dot_general(a, b, (((0,), (0,)), ((), ())), preferred_element_type=F32)


def _nn(a, b):
    return jnp.dot(a, b, preferred_element_type=F32)


def _sigmoid(x):
    return 0.5 * jnp.tanh(0.5 * x) + 0.5


_GELU_C = float(np.sqrt(2.0 / np.pi))


def _gelu(x):
    t = jnp.tanh(_GELU_C * (x + 0.044715 * (x * x * x)))
    return x * (0.5 * (1.0 + t))


def _gelu_grad(x):
    t = jnp.tanh(_GELU_C * (x + 0.044715 * (x * x * x)))
    return 0.5 * (1.0 + t) + 0.5 * x * (1.0 - t * t) * (_GELU_C * (1.0 + 3.0 * 0.044715 * x * x))


def _rms_stats(xf):
    rstd = lax.rsqrt(jnp.mean(xf * xf, axis=-1, keepdims=True) + EPS)
    return rstd, xf * rstd


def _rms_bwd(xhat, rstd, g, dh):
    dxh = dh * g
    dx = rstd * (dxh - xhat * jnp.mean(dxh * xhat, axis=-1, keepdims=True))
    return dx, dh * xhat


def norm_mm(x, g, w, *, tm, scale, name, hook=Hook()):
    S, D = x.shape
    J, _, C = w.shape
    nh, nho = len(hook.ins), len(hook.out_shapes)

    def body(x_ref, g_ref, w_ref, *rest):
        h_in, (h_ref, z_ref), h_out, h_sems = rest[:nh], rest[nh:nh + 2], rest[nh + 2:nh + 2 + nho], rest[nh + 2 + nho:]
        _run_hook(hook, "before", h_in, h_out, h_sems)
        _, xhat = _rms_stats(x_ref[...])
        h = (xhat * g_ref[...]).astype(BF16)
        h_ref[...] = h
        for j in range(J):
            acc = _nn(h, w_ref[j])
            if scale is not None:
                acc = acc * scale
            z_ref[:, j * C:(j + 1) * C] = acc.astype(BF16)
        _run_hook(hook, "after", h_in, h_out, h_sems)

    return _hooked_call(
        body, hook, n_in=3, n_out=2, grid=(S // tm,), name=name,
        in_specs=[pl.BlockSpec((tm, D), lambda i: (i, 0)), _full((1, D)), _full((J, D, C))],
        out_specs=[pl.BlockSpec((tm, D), lambda i: (i, 0)), pl.BlockSpec((tm, J * C), lambda i: (i, 0))],
        out_shape=[jax.ShapeDtypeStruct((S, D), BF16), jax.ShapeDtypeStruct((S, J * C), BF16)],
        compiler_params=_params("arbitrary"),
    )(x, g, w, *hook.ins)


def _att_window_specs(nt, col):
    return [pl.BlockSpec((ATT_TILE, 512), lambda t, j=j: (jnp.clip(t - (ATT_WIN - 1) + j, 0, nt - 1), col))
            for j in range(ATT_WIN)]


ATT_SCALE = ATT_HEAD_DIM ** -0.5
ATT_FAR = 3 * ATT_TILE


def _att_scores(q_scaled, k, bias, valid):
    return jnp.where(valid, _nt(q_scaled, k) + bias, NEG_INF)


def _att_valid(t):
    kpos = lax.broadcasted_iota(jnp.int32, (ATT_TILE, ATT_WIN * ATT_TILE), 1) + (t - (ATT_WIN - 1)) * ATT_TILE
    return kpos >= 0


def attn_fwd(z, bias, hook=Hook()):
    S = z.shape[0]
    nt = S // ATT_TILE
    n_in, nh, nho = 2 + 2 * ATT_WIN, len(hook.ins), len(hook.out_shapes)

    def body(q_ref, *refs):
        k_refs, v_refs = refs[:ATT_WIN], refs[ATT_WIN:2 * ATT_WIN]
        bias_ref = refs[2 * ATT_WIN]
        rest = refs[2 * ATT_WIN + 1:]
        h_in, (y_ref, lse_ref), h_out = rest[:nh], rest[nh:nh + 2], rest[nh + 2:nh + 2 + nho]
        s_scr, p_scr = rest[nh + 2 + nho:nh + 4 + nho]
        h_sems = rest[nh + 4 + nho:]
        _run_hook(hook, "before", h_in, h_out, h_sems)
        valid = _att_valid(pl.program_id(0))
        heads = [slice(h * ATT_HEAD_DIM, (h + 1) * ATT_HEAD_DIM) for h in range(ATT_HEADS)]
        for h, sl in enumerate(heads):
            k = jnp.concatenate([r[:, sl] for r in k_refs], axis=0)
            s_scr[h] = _att_scores(q_ref[:, sl] * ATT_SCALE, k, bias_ref[h], valid)
        stats = []
        for h in range(ATT_HEADS):
            s = s_scr[h]
            m = jnp.max(s, axis=-1, keepdims=True)
            p = jnp.exp(s - m)
            stats.append((m, jnp.sum(p, axis=-1, keepdims=True)))
            p_scr[h] = p.astype(BF16)
        for h, sl in enumerate(heads):
            m, l = stats[h]
            v = jnp.concatenate([r[:, sl] for r in v_refs], axis=0)
            y_ref[:, sl] = (_nn(p_scr[h], v) / l).astype(BF16)
            lse_ref[:, h:h + 1] = m + jnp.log(l)
        _run_hook(hook, "after", h_in, h_out, h_sems)

    tile = lambda col: pl.BlockSpec((ATT_TILE, 512), lambda t: (t, col))
    return _hooked_call(
        body, hook, n_in=n_in, n_out=2, grid=(nt,), name="attn_fwd",
        in_specs=[tile(0)] + _att_window_specs(nt, 1) + _att_window_specs(nt, 2) + [_full(bias.shape)],
        out_specs=[tile(0), pl.BlockSpec((ATT_TILE, ATT_HEADS), lambda t: (t, 0))],
        out_shape=[jax.ShapeDtypeStruct((S, 512), BF16), jax.ShapeDtypeStruct((S, ATT_HEADS), F32)],
        scratch_shapes=[pltpu.VMEM((ATT_HEADS, ATT_TILE, ATT_WIN * ATT_TILE), F32),
                        pltpu.VMEM((ATT_HEADS, ATT_TILE, ATT_WIN * ATT_TILE), BF16)],
        compiler_params=_params("arbitrary"),
    )(z, *([z] * (2 * ATT_WIN)), bias, *hook.ins)


SG_PAIRS = SG_GROUPS // 2


def _group_mean(x, a_ref):
    rows = x.shape[0]
    hi = x.astype(BF16)
    lo = (x - hi.astype(F32)).astype(BF16)
    both = _nn(jnp.concatenate([hi, lo], axis=0), a_ref[...])
    return both[:rows] + both[rows:]


def _pair_diag(x, ref):
    first = lax.broadcasted_iota(jnp.int32, x.shape, 1) < SG_GROUP_DIM
    ref[0:SG_BLOCK, :] = jnp.where(first, x, 0.0).astype(BF16)
    ref[SG_BLOCK:2 * SG_BLOCK, :] = jnp.where(first, 0.0, x).astype(BF16)


def _sgu_norm(zv_ref, a_ref, lng_ref, lnb_ref):
    v = _gelu(zv_ref[...].astype(F32))
    vc = v - _group_mean(v, a_ref)
    rstd = lax.rsqrt(_group_mean(vc * vc, a_ref) + EPS)
    xhat = vc * rstd
    return xhat, rstd, xhat * lng_ref[...] + lnb_ref[...]


def _sgu_rows(b):
    return pl.ds(pl.multiple_of(b * SG_BLOCK, SG_BLOCK), SG_BLOCK)


def sgu_fwd(z, amean, lng, lnb, wpair, bfull, *, tm, hook=Hook()):
    S = z.shape[0]
    nh, nho = len(hook.ins), len(hook.out_shapes)

    def body(zu_ref, zv_ref, a_ref, lng_ref, lnb_ref, wpair_ref, bfull_ref, *rest):
        h_in, y_ref, h_out = rest[:nh], rest[nh], rest[nh + 1:nh + 1 + nho]
        vn_scr, vbd = rest[nh + 1 + nho:nh + 3 + nho]
        h_sems = rest[nh + 3 + nho:]
        _run_hook(hook, "before", h_in, h_out, h_sems)
        vn_scr[...] = _sgu_norm(zv_ref, a_ref, lng_ref, lnb_ref)[2]

        def block(b, carry):
            rows = _sgu_rows(b)
            u = _gelu(zu_ref[rows, :].astype(F32))
            for p in range(SG_PAIRS):
                lanes = slice(p * 128, (p + 1) * 128)
                _pair_diag(vn_scr[rows, lanes], vbd.at[p])
                sv = _nn(wpair_ref[p], vbd[p]) + bfull_ref[:, lanes]
                y_ref[rows, lanes] = (u[:, lanes] * sv).astype(BF16)
            return carry

        lax.fori_loop(0, tm // SG_BLOCK, block, 0)
        _run_hook(hook, "after", h_in, h_out, h_sems)

    tile = lambda col: pl.BlockSpec((tm, 512), lambda i: (i, col))
    smalls = [amean, lng, lnb, wpair, bfull]
    return _hooked_call(
        body, hook, n_in=2 + len(smalls), n_out=1, grid=(S // tm,), name="sgu_fwd",
        in_specs=[tile(3), tile(4)] + [_full(a.shape) for a in smalls],
        out_specs=[tile(0)], out_shape=[jax.ShapeDtypeStruct((S, 512), BF16)],
        scratch_shapes=[pltpu.VMEM((tm, 512), F32), pltpu.VMEM((SG_PAIRS, 2 * SG_BLOCK, 128), BF16)],
        compiler_params=_params("arbitrary"),
    )(z, z, *smalls, *hook.ins)


def _gate_specs(tm):
    return [pl.BlockSpec((tm, 512), lambda i, c=c: (i, c)) for c in (5, 6, 7, 8)]


def merge_fwd(y_att, y_sg, z, x, w_ba, w_bs, w_out, *, tm):
    S, D = x.shape
    J, _, C = w_ba.shape

    def body(ya_ref, ys_ref, g0, g1, g2, g3, x_ref, wba_ref, wbs_ref, wo_ref, a_ref, b_ref, m_ref, x1_ref):
        ya, ys = ya_ref[...], ys_ref[...]
        a = jnp.concatenate([_nn(ya, wba_ref[j]) for j in range(J)], axis=1)
        b = jnp.concatenate([_nn(ys, wbs_ref[j]) for j in range(J)], axis=1)
        sa = _sigmoid(jnp.concatenate([g0[...], g1[...]], axis=1).astype(F32))
        sb = _sigmoid(jnp.concatenate([g2[...], g3[...]], axis=1).astype(F32))
        a_ref[...] = a.astype(BF16)
        b_ref[...] = b.astype(BF16)
        merged = (sa * a + sb * b).astype(BF16)
        m_ref[...] = merged
        x1_ref[...] = x_ref[...] + _nn(merged, wo_ref[...])

    row = lambda n: pl.BlockSpec((tm, n), lambda i: (i, 0))
    return pl.pallas_call(
        body, grid=(S // tm,), name="merge_fwd",
        in_specs=[row(512), row(512)] + _gate_specs(tm) + [row(D), _full(w_ba.shape), _full(w_bs.shape), _full(w_out.shape)],
        out_specs=[row(D)] * 4,
        out_shape=[jax.ShapeDtypeStruct((S, D), BF16)] * 3 + [jax.ShapeDtypeStruct((S, D), F32)],
        compiler_params=_params("parallel"),
    )(y_att, y_sg, z, z, z, z, x, w_ba, w_bs, w_out)


def xattn_fwd(xq, kv, w_xo, x1, *, tm):
    S, D = xq.shape
    dh = XATT_HEAD_DIM

    def body(q_ref, kv_ref, wo_ref, x1_ref, o_ref, lse_ref, x2_ref):
        outs = []
        for h in range(XATT_HEADS):
            s = _nt(q_ref[:, h * dh:(h + 1) * dh], kv_ref[:, h * dh:(h + 1) * dh])
            m = jnp.max(s, axis=-1, keepdims=True)
            p = jnp.exp(s - m)
            l = jnp.sum(p, axis=-1, keepdims=True)
            outs.append((_nn(p.astype(BF16), kv_ref[:, D + h * dh:D + (h + 1) * dh]) / l).astype(BF16))
            lse_ref[:, h:h + 1] = m + jnp.log(l)
        o = jnp.concatenate(outs, axis=1)
        o_ref[...] = o
        x2_ref[...] = x1_ref[...] + _nn(o, wo_ref[...])

    row = lambda n: pl.BlockSpec((tm, n), lambda i: (i, 0))
    return pl.pallas_call(
        body, grid=(S // tm,), name="xattn_fwd",
        in_specs=[row(D), _full(kv.shape), _full(w_xo.shape), row(D)],
        out_specs=[row(D), row(XATT_HEADS), row(D)],
        out_shape=[jax.ShapeDtypeStruct((S, D), BF16), jax.ShapeDtypeStruct((S, XATT_HEADS), F32),
                   jax.ShapeDtypeStruct((S, D), F32)],
        compiler_params=_params("parallel"),
    )(xq, kv, w_xo, x1)


def ffn_out_loss(gu, w, x2, g, target, *, tm):
    S, D = x2.shape
    F = w.shape[0]

    def body(gu_ref, w_ref, x2_ref, g_ref, t_ref, act_ref, loss_ref, dx_ref, dg_ref):
        @pl.when(pl.program_id(0) == 0)
        def _():
            loss_ref[...] = jnp.zeros_like(loss_ref)
            dg_ref[...] = jnp.zeros_like(dg_ref)

        half = gu_ref[:, :F] * 0.5
        act = (half * gu_ref[:, F:]) * (jnp.tanh(half) + 1.0)
        act_ref[...] = act
        gv = g_ref[...]
        rstd, xhat = _rms_stats(x2_ref[...] + _nn(act, w_ref[...]))
        err = xhat * gv - t_ref[...]
        loss_ref[...] += 0.5 * jnp.sum(jnp.mean(err * err, axis=-1, keepdims=True))
        dx, dgc = _rms_bwd(xhat, rstd, gv, err * (1.0 / D))
        dx_ref[...] = dx
        dg_ref[...] += jnp.sum(dgc, axis=0, keepdims=True)

    row = lambda n: pl.BlockSpec((tm, n), lambda i: (i, 0))
    return pl.pallas_call(
        body, grid=(S // tm,), name="ffn_out_loss",
        in_specs=[row(2 * F), _full(w.shape), row(D), _full((1, D)), row(D)],
        out_specs=[row(F), _acc((1, 128)), row(D), _acc((1, D))],
        out_shape=[jax.ShapeDtypeStruct((S, F), BF16), jax.ShapeDtypeStruct((1, 128), F32),
                   jax.ShapeDtypeStruct((S, D), F32), jax.ShapeDtypeStruct((1, D), F32)],
        compiler_params=_params("arbitrary"),
    )(gu, w, x2, g, target)


def ffn_act_bwd(dx3, gu, w, *, tm, nchunk):
    S, D = dx3.shape
    F = w.shape[0]
    cn = F // nchunk

    def body(dx_ref, gu_ref, w_ref, dgu_ref):
        dxb = dx_ref[...].astype(BF16)
        for j in range(nchunk):
            dact = _nt(dxb, w_ref[j * cn:(j + 1) * cn, :])
            gate = gu_ref[:, j * cn:(j + 1) * cn]
            up = gu_ref[:, F + j * cn:F + (j + 1) * cn]
            sg = _sigmoid(gate)
            silu = gate * sg
            dsilu = sg + silu * (1.0 - sg)
            dgu_ref[:, j * cn:(j + 1) * cn] = (dact * (up * dsilu).astype(F32)).astype(BF16)
            dgu_ref[:, F + j * cn:F + (j + 1) * cn] = (dact * silu.astype(F32)).astype(BF16)

    row = lambda n: pl.BlockSpec((tm, n), lambda i: (i, 0))
    return pl.pallas_call(
        body, grid=(S // tm,), name="ffn_act_bwd",
        in_specs=[row(D), row(2 * F), _full(w.shape)], out_specs=row(2 * F),
        out_shape=jax.ShapeDtypeStruct((S, 2 * F), BF16), compiler_params=_params("parallel"),
    )(dx3, gu, w)


def _overlaps(widths, lo, hi):
    out, off = [], 0
    for p, wd in enumerate(widths):
        a, b = max(lo, off), min(hi, off + wd)
        if a < b:
            out.append((p, a - off, b - off, a - lo))
        off += wd
    return out


def tn_mm_pieces(a, pieces, *, shards, per, tk, name):
    K, M = a.shape
    widths = [p.shape[1] for p in pieces]
    C = sum(widths) // shards
    tn = per * C
    nk = K // tk
    n_tiles = shards // per

    def body(a_ref, *refs):
        p_refs, o_ref, acc_ref = refs[:len(pieces)], refs[len(pieces)], refs[len(pieces) + 1]
        n, k = pl.program_id(0), pl.program_id(1)

        @pl.when(k == 0)
        def _():
            acc_ref[...] = jnp.zeros_like(acc_ref)

        av = a_ref[...]
        for tile in range(n_tiles):
            @pl.when(n == tile)
            def _(tile=tile):
                for p, c0, c1, at in _overlaps(widths, tile * tn, (tile + 1) * tn):
                    acc_ref[:, at:at + c1 - c0] += _tn(av, p_refs[p][:, c0:c1])

        @pl.when(k == nk - 1)
        def _():
            for s in range(per):
                o_ref[s] = acc_ref[:, s * C:(s + 1) * C].astype(BF16)

    return pl.pallas_call(
        body, grid=(n_tiles, nk), name=name,
        in_specs=[pl.BlockSpec((tk, M), lambda n, k: (k, 0))] + [pl.BlockSpec((tk, wd), lambda n, k: (k, 0)) for wd in widths],
        out_specs=pl.BlockSpec((per, M, C), lambda n, k: (n, 0, 0)), out_shape=jax.ShapeDtypeStruct((shards, M, C), BF16),
        scratch_shapes=[pltpu.VMEM((M, tn), F32)], compiler_params=_params("parallel", "arbitrary"),
    )(a, *pieces)


def tn_mm(a, b, *, shards, tn, tk, name):
    K, M = a.shape
    N = b.shape[1]
    C = N // shards
    per = tn // C
    nk = K // tk

    def body(a_ref, b_ref, o_ref, acc_ref):
        k = pl.program_id(1)

        @pl.when(k == 0)
        def _():
            acc_ref[...] = jnp.zeros_like(acc_ref)

        acc_ref[...] += _tn(a_ref[...].astype(BF16), b_ref[...].astype(BF16))

        @pl.when(k == nk - 1)
        def _():
            if shards > 1:
                for s in range(per):
                    o_ref[s] = acc_ref[:, s * C:(s + 1) * C].astype(BF16)
            else:
                o_ref[...] = acc_ref[...].astype(BF16)

    if shards > 1:
        out_spec = pl.BlockSpec((per, M, C), lambda n, k: (n, 0, 0))
        out_shape = jax.ShapeDtypeStruct((shards, M, C), BF16)
    else:
        out_spec = pl.BlockSpec((M, tn), lambda n, k: (0, n))
        out_shape = jax.ShapeDtypeStruct((M, N), BF16)
    return pl.pallas_call(
        body, grid=(N // tn, nk), name=name,
        in_specs=[pl.BlockSpec((tk, M), lambda n, k: (k, 0)), pl.BlockSpec((tk, tn), lambda n, k: (k, n))],
        out_specs=out_spec, out_shape=out_shape, scratch_shapes=[pltpu.VMEM((M, tn), F32)],
        compiler_params=_params("parallel", "arbitrary"),
    )(a, b)


def mm_nt_norm_bwd(dy, w, x, g, dx_in, *, tm, name, hook=Hook()):
    S, D = x.shape
    J, _, C = w.shape
    has_in = dx_in is not None
    dys = list(dy) if isinstance(dy, (list, tuple)) else [dy]
    widths = [d.shape[1] for d in dys]
    nd = len(dys)
    n_in, nh, nho = nd + 3 + has_in, len(hook.ins), len(hook.out_shapes)

    def body(*refs):
        dy_refs = refs[:nd]
        w_ref, x_ref, g_ref = refs[nd:nd + 3]
        dxin_ref = refs[nd + 3] if has_in else None
        rest = refs[n_in:]
        h_in, (dx_ref, dg_ref), h_out, h_sems = rest[:nh], rest[nh:nh + 2], rest[nh + 2:nh + 2 + nho], rest[nh + 2 + nho:]
        _run_hook(hook, "before", h_in, h_out, h_sems)

        @pl.when(pl.program_id(0) == 0)
        def _():
            dg_ref[...] = jnp.zeros_like(dg_ref)

        dh = None
        for j in range(J):
            for p, c0, c1, at in _overlaps(widths, j * C, (j + 1) * C):
                part = _nt(dy_refs[p][:, c0:c1], w_ref[j, :, at:at + c1 - c0])
                dh = part if dh is None else dh + part
        rstd, xhat = _rms_stats(x_ref[...])
        dx, dgc = _rms_bwd(xhat, rstd, g_ref[...], dh)
        dx_ref[...] = dx + dxin_ref[...] if has_in else dx
        dg_ref[...] += jnp.sum(dgc, axis=0, keepdims=True)
        _run_hook(hook, "after", h_in, h_out, h_sems)

    row = lambda n: pl.BlockSpec((tm, n), lambda i: (i, 0))
    ins = dys + [w, x, g] + ([dx_in] if has_in else [])
    return _hooked_call(
        body, hook, n_in=n_in, n_out=2, grid=(S // tm,), name=name,
        in_specs=[row(wd) for wd in widths] + [_full(w.shape), row(D), _full((1, D))] + ([row(D)] if has_in else []),
        out_specs=[row(D), _acc((1, D))],
        out_shape=[jax.ShapeDtypeStruct((S, D), F32), jax.ShapeDtypeStruct((1, D), F32)],
        compiler_params=_params("arbitrary"),
    )(*ins, *hook.ins)


def xattn_bwd(dx2, xq, o, lse, kv, w_xo, *, tm):
    S, D = xq.shape
    M = kv.shape[0]
    dh = XATT_HEAD_DIM

    def body(dx_ref, q_ref, o_ref, lse_ref, kv_ref, wo_ref, dq_ref, dkv_ref):
        @pl.when(pl.program_id(0) == 0)
        def _():
            dkv_ref[...] = jnp.zeros_like(dkv_ref)

        do = _nt(dx_ref[...].astype(BF16), wo_ref[...])
        for h in range(XATT_HEADS):
            hs = slice(h * dh, (h + 1) * dh)
            vs = slice(D + h * dh, D + (h + 1) * dh)
            q, k, v = q_ref[:, hs], kv_ref[:, hs], kv_ref[:, vs]
            do_h = do[:, hs]
            do_b = do_h.astype(BF16)
            p = jnp.exp(_nt(q, k) - lse_ref[:, h:h + 1])
            delta = jnp.sum(do_h * o_ref[:, hs].astype(F32), axis=-1, keepdims=True)
            ds = (p * (_nt(do_b, v) - delta)).astype(BF16)
            dq_ref[:, hs] = (_nn(ds, k) * (dh ** -0.5)).astype(BF16)
            dkv_ref[:, hs] += _tn(ds, q)
            dkv_ref[:, vs] += _tn(p.astype(BF16), do_b)

    row = lambda n: pl.BlockSpec((tm, n), lambda i: (i, 0))
    return pl.pallas_call(
        body, grid=(S // tm,), name="xattn_bwd",
        in_specs=[row(D), row(D), row(D), row(XATT_HEADS), _full(kv.shape), _full(w_xo.shape)],
        out_specs=[row(D), _acc((M, 2 * D))],
        out_shape=[jax.ShapeDtypeStruct((S, D), BF16), jax.ShapeDtypeStruct((M, 2 * D), F32)],
        compiler_params=_params("arbitrary"),
    )(dx2, xq, o, lse, kv, w_xo)


def merge_bwd(dx1, w_out, a, b, z, w_ba, w_bs, *, tm, hook=Hook()):
    S, D = dx1.shape
    J, W, C = w_ba.shape
    nh, nho = len(hook.ins), len(hook.out_shapes)

    def body(dx_ref, wo_ref, a_ref, b_ref, g0, g1, g2, g3, wba_ref, wbs_ref, *rest):
        h_in, (da_ref, db_ref, dg_ref, dya_ref, dys_ref) = rest[:nh], rest[nh:nh + 5]
        h_out, h_sems = rest[nh + 5:nh + 5 + nho], rest[nh + 5 + nho:]
        _run_hook(hook, "before", h_in, h_out, h_sems)
        dm = _nt(dx_ref[...].astype(BF16), wo_ref[...])
        sa = _sigmoid(jnp.concatenate([g0[...], g1[...]], axis=1))
        sb = _sigmoid(jnp.concatenate([g2[...], g3[...]], axis=1))
        dg_ref[:, :D] = (dm * (a_ref[...] * (sa * (1.0 - sa))).astype(F32)).astype(BF16)
        dg_ref[:, D:] = (dm * (b_ref[...] * (sb * (1.0 - sb))).astype(F32)).astype(BF16)
        da = (dm * sa.astype(F32)).astype(BF16)
        db = (dm * sb.astype(F32)).astype(BF16)
        da_ref[...] = da
        db_ref[...] = db
        dya = _nt(da[:, 0:C], wba_ref[0])
        dys = _nt(db[:, 0:C], wbs_ref[0])
        for j in range(1, J):
            dya += _nt(da[:, j * C:(j + 1) * C], wba_ref[j])
            dys += _nt(db[:, j * C:(j + 1) * C], wbs_ref[j])
        dya_ref[...] = dya.astype(BF16)
        dys_ref[...] = dys.astype(BF16)
        _run_hook(hook, "after", h_in, h_out, h_sems)

    row = lambda n: pl.BlockSpec((tm, n), lambda i: (i, 0))
    return _hooked_call(
        body, hook, n_in=10, n_out=5, grid=(S // tm,), name="merge_bwd",
        in_specs=[row(D), _full(w_out.shape), row(D), row(D)] + _gate_specs(tm) + [_full(w_ba.shape), _full(w_bs.shape)],
        out_specs=[row(D), row(D), row(2 * D), row(W), row(W)],
        out_shape=[jax.ShapeDtypeStruct((S, D), BF16)] * 2 + [jax.ShapeDtypeStruct((S, 2 * D), BF16)]
        + [jax.ShapeDtypeStruct((S, W), BF16)] * 2,
        compiler_params=_params("arbitrary"),
    )(dx1, w_out, a, b, z, z, z, z, w_ba, w_bs, *hook.ins)


def attn_bwd(z, y, dy, lse, bias, hook=Hook()):
    S = z.shape[0]
    nt = S // ATT_TILE
    back = ATT_WIN - 1
    n_in, nh, nho = 5 + 2 * ATT_WIN, len(hook.ins), len(hook.out_shapes)

    def body(q_ref, *refs):
        k_refs, v_refs = refs[:ATT_WIN], refs[ATT_WIN:2 * ATT_WIN]
        y_ref, dy_ref, lse_ref, bias_ref = refs[2 * ATT_WIN:2 * ATT_WIN + 4]
        rest = refs[2 * ATT_WIN + 4:]
        h_in, (dqkv_ref, dbias_ref), h_out = rest[:nh], rest[nh:nh + 2], rest[nh + 2:nh + 2 + nho]
        dk_acc, dv_acc, dq_buf = rest[nh + 2 + nho:nh + 5 + nho]
        h_sems = rest[nh + 5 + nho:]
        t = pl.program_id(0)
        _run_hook(hook, "before", h_in, h_out, h_sems)

        @pl.when(t == 0)
        def _():
            dbias_ref[...] = jnp.zeros_like(dbias_ref)
            dk_acc[...] = jnp.zeros_like(dk_acc)
            dv_acc[...] = jnp.zeros_like(dv_acc)
            dq_buf[...] = jnp.zeros_like(dq_buf)

        @pl.when(t < nt)
        def _():
            valid = _att_valid(t)
            q_t = (q_ref[...] * ATT_SCALE).astype(F32).T.astype(BF16)
            do_t = dy_ref[...].astype(F32).T.astype(BF16)
            for h in range(ATT_HEADS):
                sl = slice(h * ATT_HEAD_DIM, (h + 1) * ATT_HEAD_DIM)
                k = jnp.concatenate([r[:, sl] for r in k_refs], axis=0)
                v = jnp.concatenate([r[:, sl] for r in v_refs], axis=0)
                q, do_b = q_ref[:, sl] * ATT_SCALE, dy_ref[:, sl]
                p = jnp.exp(_att_scores(q, k, bias_ref[h], valid) - lse_ref[:, h:h + 1])
                delta = jnp.sum(do_b.astype(F32) * y_ref[:, sl].astype(F32), axis=-1, keepdims=True)
                ds = p * (_nt(do_b, v) - delta)
                dbias_ref[h, :, 0:ATT_TILE] += ds[:, 0:ATT_TILE] + ds[:, ATT_TILE:2 * ATT_TILE] + ds[:, 2 * ATT_TILE:ATT_FAR]
                dbias_ref[h, :, ATT_TILE:] += ds[:, ATT_FAR:]
                ds_b = ds.astype(BF16)
                dq_buf[t % ATT_WIN, :, sl] = (_nn(ds_b, k) * ATT_SCALE).astype(BF16)
                dk_w = _nn(q_t[sl, :], ds_b)
                dv_w = _nn(do_t[sl, :], p.astype(BF16))
                for i in range(ATT_WIN):
                    slot = (t + 1 + i) % ATT_WIN
                    cols = slice(i * ATT_TILE, (i + 1) * ATT_TILE)
                    if i == back:
                        dk_acc[slot, sl, :] = dk_w[:, cols]
                        dv_acc[slot, sl, :] = dv_w[:, cols]
                    else:
                        dk_acc[slot, sl, :] += dk_w[:, cols]
                        dv_acc[slot, sl, :] += dv_w[:, cols]

        done = (t + 1) % ATT_WIN
        dqkv_ref[:, 0:512] = dq_buf[done]
        dqkv_ref[:, 512:1024] = dk_acc[done].T.astype(BF16)
        dqkv_ref[:, 1024:1536] = dv_acc[done].T.astype(BF16)
        _run_hook(hook, "after", h_in, h_out, h_sems)

    last = nt - 1
    dbias_shape = (ATT_HEADS, ATT_TILE, ATT_TILE + ATT_WIN * ATT_TILE - ATT_FAR)
    tile = lambda col, n=512: pl.BlockSpec((ATT_TILE, n), lambda t: (jnp.minimum(t, last), col))
    late = pl.BlockSpec((ATT_TILE, 1536), lambda t: (jnp.maximum(t - back, 0), 0))
    return _hooked_call(
        body, hook, n_in=n_in, n_out=2, grid=(nt + back,), name="attn_bwd",
        in_specs=[tile(0)] + _att_window_specs(nt, 1) + _att_window_specs(nt, 2)
        + [tile(0), tile(0), tile(0, ATT_HEADS), _full(bias.shape)],
        out_specs=[late, _acc(dbias_shape)],
        out_shape=[jax.ShapeDtypeStruct((S, 1536), BF16), jax.ShapeDtypeStruct(dbias_shape, F32)],
        scratch_shapes=[pltpu.VMEM((ATT_WIN, 512, ATT_TILE), F32)] * 2 + [pltpu.VMEM((ATT_WIN, ATT_TILE, 512), BF16)],
        compiler_params=_params("arbitrary"),
    )(z, *([z] * (2 * ATT_WIN)), y, dy, lse, bias, *hook.ins)


def sgu_bwd(z, dy, amean, lng, lnb, wpair, wtpair, bfull, maskpair, *, tm, hook=Hook()):
    S = z.shape[0]
    n = S // tm
    nh, nho = len(hook.ins), len(hook.out_shapes)

    def body(zu_ref, zv_ref, dy_ref, a_ref, lng_ref, lnb_ref, wpair_ref, wtpair_ref, bfull_ref, mask_ref, *rest):
        h_in, (duv_ref, dw_ref, dsgb_ref, dlng_ref, dlnb_ref), h_out = rest[:nh], rest[nh:nh + 5], rest[nh + 5:nh + 5 + nho]
        (xhat_scr, rstd_scr, vn_scr, dxh_scr, vbd, dbd, w_acc, b_acc, g_acc, s_acc) = rest[nh + 5 + nho:nh + 15 + nho]
        h_sems = rest[nh + 15 + nho:]
        i = pl.program_id(0)
        _run_hook(hook, "before", h_in, h_out, h_sems)

        @pl.when(i == 0)
        def _():
            for r in (w_acc, b_acc, g_acc, s_acc):
                r[...] = jnp.zeros_like(r)

        xhat_scr[...], rstd_scr[...], vn_scr[...] = _sgu_norm(zv_ref, a_ref, lng_ref, lnb_ref)

        def block(b, carry):
            rows = _sgu_rows(b)
            zu = zu_ref[rows, :].astype(F32)
            u, du = _gelu(zu), _gelu_grad(zu)
            dyv = dy_ref[rows, :].astype(F32)
            for p in range(SG_PAIRS):
                lanes = slice(p * 128, (p + 1) * 128)
                _pair_diag(vn_scr[rows, lanes], vbd.at[p])
                sv = _nn(wpair_ref[p], vbd[p]) + bfull_ref[:, lanes]
                duv_ref[rows, lanes] = (dyv[:, lanes] * sv * du[:, lanes]).astype(BF16)
                dsv = dyv[:, lanes] * u[:, lanes]
                b_acc[:, lanes] += dsv
                w_acc[p] += _nt(dsv.astype(BF16), vbd[p])
                _pair_diag(dsv, dbd.at[p])
                dvn = _nn(wtpair_ref[p], dbd[p])
                g_acc[:, lanes] += dvn * xhat_scr[rows, lanes]
                s_acc[:, lanes] += dvn
                dxh_scr[rows, lanes] = dvn * lng_ref[:, lanes]
            return carry

        lax.fori_loop(0, tm // SG_BLOCK, block, 0)
        dxh, xhat = dxh_scr[...], xhat_scr[...]
        dv = rstd_scr[...] * (dxh - _group_mean(dxh, a_ref) - xhat * _group_mean(dxh * xhat, a_ref))
        duv_ref[:, 512:1024] = (dv * _gelu_grad(zv_ref[...].astype(F32))).astype(BF16)

        @pl.when(i == n - 1)
        def _():
            dw_ref[...] = w_acc[...] * mask_ref[...]
            dsgb_ref[...] = _group_mean(b_acc[...], a_ref) * float(SG_GROUP_DIM)
            dlng_ref[...] = jnp.sum(g_acc[...], axis=0, keepdims=True)
            dlnb_ref[...] = jnp.sum(s_acc[...], axis=0, keepdims=True)

        _run_hook(hook, "after", h_in, h_out, h_sems)

    tile = lambda col: pl.BlockSpec((tm, 512), lambda i: (i, col))
    smalls = [amean, lng, lnb, wpair, wtpair, bfull, maskpair]
    pair_shape = (SG_PAIRS, SG_BLOCK, 2 * SG_BLOCK)
    return _hooked_call(
        body, hook, n_in=3 + len(smalls), n_out=5, grid=(n,), name="sgu_bwd",
        in_specs=[tile(3), tile(4), tile(0)] + [_full(a.shape) for a in smalls],
        out_specs=[pl.BlockSpec((tm, 1024), lambda i: (i, 0)), _acc(pair_shape), _acc((SG_BLOCK, 512)),
                   _acc((1, 512)), _acc((1, 512))],
        out_shape=[jax.ShapeDtypeStruct((S, 1024), BF16), jax.ShapeDtypeStruct(pair_shape, F32),
                   jax.ShapeDtypeStruct((SG_BLOCK, 512), F32), jax.ShapeDtypeStruct((1, 512), F32),
                   jax.ShapeDtypeStruct((1, 512), F32)],
        scratch_shapes=[pltpu.VMEM((tm, 512), F32)] * 4
        + [pltpu.VMEM((SG_PAIRS, 2 * SG_BLOCK, 128), BF16)] * 2
        + [pltpu.VMEM(pair_shape, F32)] + [pltpu.VMEM((SG_BLOCK, 512), F32)] * 3,
        compiler_params=_params("arbitrary"),
    )(z, z, dy, *smalls, *hook.ins)


def bias_colsum(p, far_part):
    H, _, L = p.shape
    near_lo = (ATT_WIN - 1) * ATT_TILE - REL_CLIP + 1
    near_hi = ATT_WIN * ATT_TILE

    def body(p_ref, f_ref, gw_ref, far_ref):
        k = lax.broadcasted_iota(jnp.int32, (1, L), 1)
        is_far = (k < near_lo) | (k >= near_hi)
        for h in range(H):
            g = jnp.sum(p_ref[h], axis=0, keepdims=True)
            gw_ref[h:h + 1, :] = g
            far_ref[h:h + 1, :] = jnp.zeros((1, 128), F32) + (jnp.sum(jnp.where(is_far, g, 0.0)) + jnp.sum(f_ref[h]))

    return pl.pallas_call(
        body, name="bias_colsum", in_specs=[_full(p.shape), _full(far_part.shape)],
        out_specs=[_acc((H, L)), _acc((H, 128))],
        out_shape=[jax.ShapeDtypeStruct((H, L), F32), jax.ShapeDtypeStruct((H, 128), F32)], grid=(1,),
        compiler_params=_params("arbitrary"),
    )(p, far_part)


def _row_tile(rows, cols):
    best = None
    for tr in range(16, rows + 1, 16):
        if rows % tr == 0 and tr * cols * 4 <= EW_BLOCK_BYTES:
            best = tr
    return best if best is not None else rows


def place_shard(w, own, name):
    R, C = w.shape
    r = R // 2
    tr = _row_tile(r, C)
    nr = r // tr

    def body(own_ref, w_ref, o_ref):
        o_ref[...] = w_ref[...].astype(BF16)

    return pl.pallas_call(
        body, name=name,
        grid_spec=pltpu.PrefetchScalarGridSpec(
            num_scalar_prefetch=1, grid=(2, nr),
            in_specs=[pl.BlockSpec((tr, C), lambda h, i, own: (h * nr + i, 0))],
            out_specs=pl.BlockSpec((None, None, tr, C), lambda h, i, own: (own[0], h, i, 0))),
        out_shape=jax.ShapeDtypeStruct((4, 2, r, C), BF16), compiler_params=_params("parallel", "parallel"),
    )(own, w)


def chip_partial(grad, recv, core, name):
    _, _, r, C = grad.shape
    tr = _row_tile(r, C)

    def body(core_ref, g_ref, r_ref, o_ref):
        o_ref[...] = (g_ref[...].astype(F32) + r_ref[...].astype(F32)).astype(BF16)

    spec = pl.BlockSpec((None, tr, C), lambda j, i, core: (j, i, 0))
    return pl.pallas_call(
        body, name=name,
        grid_spec=pltpu.PrefetchScalarGridSpec(
            num_scalar_prefetch=1, grid=(4, r // tr),
            in_specs=[pl.BlockSpec((None, None, tr, C), lambda j, i, core: (j, core[0], i, 0)), spec], out_specs=spec),
        out_shape=jax.ShapeDtypeStruct((4, r, C), BF16), compiler_params=_params("parallel", "parallel"),
    )(core, grad, recv)


def shard_sum(part, recv, own, core, name):
    _, r, C = part.shape
    tr = _row_tile(r, C)

    def body(own_ref, core_ref, p_ref, r0, r1, r2, o_ref):
        o_ref[...] = p_ref[...].astype(F32) + r0[...].astype(F32) + r1[...].astype(F32) + r2[...].astype(F32)

    return pl.pallas_call(
        body, name=name,
        grid_spec=pltpu.PrefetchScalarGridSpec(
            num_scalar_prefetch=2, grid=(r // tr,),
            in_specs=[pl.BlockSpec((None, tr, C), lambda i, own, core: (own[0], i, 0))]
            + [pl.BlockSpec((None, tr, C), lambda i, own, core, k=k: (k, i, 0)) for k in range(3)],
            out_specs=pl.BlockSpec((None, tr, C), lambda i, own, core: (core[0], i, 0))),
        out_shape=jax.ShapeDtypeStruct((2, r, C), F32), compiler_params=_params("parallel"),
    )(own, core, part, recv, recv, recv)


def adamw(w, g, m, v, name):
    R, C = w.shape
    tr = _row_tile(R, C)

    def body(w_ref, g_ref, m_ref, v_ref, d_ref, nm_ref, nv_ref):
        gv = g_ref[...]
        nm = ADAM_B1 * m_ref[...] + (1.0 - ADAM_B1) * gv
        nv = ADAM_B2 * v_ref[...] + (1.0 - ADAM_B2) * (gv * gv)
        m_hat = nm / (1.0 - ADAM_B1 ** ADAM_STEP)
        v_hat = nv / (1.0 - ADAM_B2 ** ADAM_STEP)
        d_ref[...] = -ADAM_LR * (m_hat / (jnp.sqrt(v_hat) + ADAM_EPS) + ADAM_WD * w_ref[...])
        nm_ref[...] = nm
        nv_ref[...] = nv

    spec = pl.BlockSpec((tr, C), lambda i: (i, 0))
    return pl.pallas_call(
        body, grid=(R // tr,), name=name, in_specs=[spec] * 4, out_specs=[spec] * 3,
        out_shape=[jax.ShapeDtypeStruct((R, C), F32)] * 3, compiler_params=_params("parallel"),
    )(w, g, m, v)


HBM = pl.BlockSpec(memory_space=pl.ANY)


def _place():
    x, y, c = lax.axis_index("x"), lax.axis_index("y"), lax.axis_index("c")
    chips = [(1 - x, y), (x, 1 - y), (1 - x, 1 - y)]
    return x, y, c, chips


def _gather_phases(n):
    def copies(buf, sems, kind):
        send_sems, recv_sems = sems
        x, y, c, chips = _place()

        def remote(w, k, slot, to):
            return pltpu.make_async_remote_copy(src_ref=slot, dst_ref=slot, send_sem=send_sems.at[w, k],
                                                recv_sem=recv_sems.at[w, k], device_id=to, device_id_type=MESH)

        def one(w, k, px, py):
            if kind == "mine":
                return remote(w, k, buf[w].at[2 * x + y, c], (px, py, c))
            if kind == "theirs":
                return remote(w, k, buf[w].at[2 * px + py, c], (px, py, c))
            if kind == "onward":
                return remote(w, 3 + k, buf[w].at[2 * px + py, c], (x, y, 1 - c))
            return remote(w, 3 + k, buf[w].at[2 * px + py, 1 - c], (x, y, 1 - c))

        return [one(w, k, px, py) for w in range(n) for k, (px, py) in enumerate(chips)]

    def start(_, buf, sems):
        for cp in copies(buf, sems, "mine"):
            cp.start()

    def relay(_, buf, sems):
        for theirs, onward in zip(copies(buf, sems, "theirs"), copies(buf, sems, "onward")):
            theirs.wait_recv()
            onward.start()

    def finish(_, buf, sems):
        for cp in copies(buf, sems, "relayed"):
            cp.wait_recv()
        for cp in copies(buf, sems, "mine") + copies(buf, sems, "onward"):
            cp.wait_send()

    return start, relay, finish


def _gather_sems(n):
    return (pltpu.SemaphoreType.DMA((n, 6)), pltpu.SemaphoreType.DMA((n, 6)))


def gather_weights(bufs):
    n = len(bufs)
    phases = _gather_phases(n)

    def body(*refs):
        for phase in phases:
            phase(None, refs[n:2 * n], refs[2 * n:])

    return pl.pallas_call(
        body, name="gather_weights", in_specs=[HBM] * n, out_specs=[HBM] * n,
        out_shape=[jax.ShapeDtypeStruct(b.shape, b.dtype) for b in bufs],
        input_output_aliases={i: i for i in range(n)}, scratch_shapes=list(_gather_sems(n)),
    )(*bufs)


def gather_hook(bufs, n_steps):
    start, relay, finish = _gather_phases(len(bufs))
    return Hook(ins=tuple(bufs), out_shapes=tuple(jax.ShapeDtypeStruct(b.shape, b.dtype) for b in bufs),
                alias=tuple((i, i) for i in range(len(bufs))),
                sems=_gather_sems(len(bufs)),
                steps=((0, "before", start), ((n_steps * 13) // 16, "before", relay), (n_steps - 1, "after", finish)))


def sibling_split(grads, name):
    n = len(grads)

    def body(*refs):
        src, dst = refs[:n], refs[n:2 * n]
        send_sems, recv_sems = refs[2 * n:]
        x, y, c, _ = _place()
        sends = [pltpu.make_async_remote_copy(src_ref=src[w].at[:, 1 - c], dst_ref=dst[w], send_sem=send_sems.at[w],
                                              recv_sem=recv_sems.at[w], device_id=(x, y, 1 - c), device_id_type=MESH)
                 for w in range(n)]
        for cp in sends:
            cp.start()
        for cp in sends:
            cp.wait()

    return pl.pallas_call(
        body, name=name, in_specs=[HBM] * n, out_specs=[HBM] * n,
        out_shape=[jax.ShapeDtypeStruct((4,) + g.shape[2:], g.dtype) for g in grads],
        scratch_shapes=[pltpu.SemaphoreType.DMA((n,))] * 2,
    )(*grads)


def split_hook(grads, n_steps):
    n = len(grads)

    def copies(src, dst, sems):
        send_sems, recv_sems = sems
        x, y, c, _ = _place()
        return [pltpu.make_async_remote_copy(src_ref=src[w].at[:, 1 - c], dst_ref=dst[w], send_sem=send_sems.at[w],
                                             recv_sem=recv_sems.at[w], device_id=(x, y, 1 - c), device_id_type=MESH)
                for w in range(n)]

    def start(src, dst, sems):
        for cp in copies(src, dst, sems):
            cp.start()

    def finish(src, dst, sems):
        for cp in copies(src, dst, sems):
            cp.wait()

    return Hook(ins=tuple(grads), out_shapes=tuple(jax.ShapeDtypeStruct((4,) + g.shape[2:], g.dtype) for g in grads),
                sems=(pltpu.SemaphoreType.DMA((n,)), pltpu.SemaphoreType.DMA((n,))),
                steps=((0, "before", start), (n_steps - 1, "after", finish)))


def exchange_hook(parts, n_steps):
    n = len(parts)

    def copies(src, dst, sems):
        send_sems, recv_sems = sems
        _, _, c, chips = _place()
        return [pltpu.make_async_remote_copy(
            src_ref=src[w].at[2 * px + py], dst_ref=dst[w].at[k], send_sem=send_sems.at[w, k],
            recv_sem=recv_sems.at[w, k], device_id=(px, py, c), device_id_type=MESH)
            for w in range(n) for k, (px, py) in enumerate(chips)]

    def start(src, dst, sems):
        for cp in copies(src, dst, sems):
            cp.start()

    def finish(src, dst, sems):
        for cp in copies(src, dst, sems):
            cp.wait()

    return Hook(ins=tuple(parts), out_shapes=tuple(jax.ShapeDtypeStruct((3,) + p.shape[1:], p.dtype) for p in parts),
                sems=(pltpu.SemaphoreType.DMA((n, 3)), pltpu.SemaphoreType.DMA((n, 3))),
                steps=((0, "before", start), (n_steps - 1, "after", finish)))


def sibling_join(sums):
    n = len(sums)

    def body(*refs):
        buf = refs[n:2 * n]
        send_sems, recv_sems = refs[2 * n:]
        x, y, c, _ = _place()
        sends = [pltpu.make_async_remote_copy(src_ref=buf[w].at[c], dst_ref=buf[w].at[c], send_sem=send_sems.at[w],
                                              recv_sem=recv_sems.at[w], device_id=(x, y, 1 - c), device_id_type=MESH)
                 for w in range(n)]
        for cp in sends:
            cp.start()
        for w, cp in enumerate(sends):
            cp.wait_send()
            pltpu.make_async_remote_copy(src_ref=buf[w].at[c], dst_ref=buf[w].at[1 - c], send_sem=send_sems.at[w],
                                         recv_sem=recv_sems.at[w], device_id=(x, y, 1 - c), device_id_type=MESH).wait_recv()

    return pl.pallas_call(
        body, name="sibling_join", in_specs=[HBM] * n, out_specs=[HBM] * n,
        out_shape=[jax.ShapeDtypeStruct(s.shape, s.dtype) for s in sums],
        input_output_aliases={i: i for i in range(n)},
        scratch_shapes=[pltpu.SemaphoreType.DMA((n,))] * 2,
    )(*sums)


def join_hook(sums, n_steps):
    n = len(sums)

    def copies(buf, sems, recv):
        send_sems, recv_sems = sems
        x, y, c, _ = _place()
        half = (1 - c) if recv else c
        return [pltpu.make_async_remote_copy(src_ref=buf[w].at[half], dst_ref=buf[w].at[half], send_sem=send_sems.at[w],
                                             recv_sem=recv_sems.at[w], device_id=(x, y, 1 - c), device_id_type=MESH)
                for w in range(n)]

    def start(_, buf, sems):
        for cp in copies(buf, sems, False):
            cp.start()

    def finish(_, buf, sems):
        for cp in copies(buf, sems, False):
            cp.wait_send()
        for cp in copies(buf, sems, True):
            cp.wait_recv()

    return Hook(ins=tuple(sums), out_shapes=tuple(jax.ShapeDtypeStruct(s.shape, s.dtype) for s in sums),
                alias=tuple((i, i) for i in range(n)),
                sems=(pltpu.SemaphoreType.DMA((n,)), pltpu.SemaphoreType.DMA((n,))),
                steps=((0, "before", start), (n_steps - 1, "after", finish)))


def allreduce_small(v):
    R, C = v.shape

    def body(v_ref, out_ref, all_ref, send_sems, recv_sems):
        x, y, c, chips = _place()
        me, sibling = (x, y, c), (x, y, 1 - c)

        def slot(px, py, pc):
            return all_ref.at[4 * px + 2 * py + pc]

        def copy(k, block, to, src=None):
            return pltpu.make_async_remote_copy(src_ref=slot(*block) if src is None else src, dst_ref=slot(*block),
                                                send_sem=send_sems.at[k], recv_sem=recv_sems.at[k], device_id=to,
                                                device_id_type=MESH)

        first = [copy(0, me, sibling, src=v_ref)] + [copy(1 + j, me, (*chip, c), src=v_ref) for j, chip in enumerate(chips)]
        for cp in first:
            cp.start()
        slot(*me)[...] = v_ref[...]
        passed = [copy(4 + j, (*chip, c), sibling) for j, chip in enumerate(chips)]
        for j, chip in enumerate(chips):
            copy(1 + j, (*chip, c), me).wait_recv()
            passed[j].start()
        copy(0, sibling, me).wait_recv()
        for j, chip in enumerate(chips):
            copy(4 + j, (*chip, 1 - c), me).wait_recv()
        for cp in first + passed:
            cp.wait_send()
        acc = all_ref[0]
        for k in range(1, 8):
            acc = acc + all_ref[k]
        out_ref[...] = acc

    vmem = pl.BlockSpec(memory_space=pltpu.VMEM)
    return pl.pallas_call(
        body, name="allreduce_small", in_specs=[vmem], out_specs=vmem, out_shape=jax.ShapeDtypeStruct((R, C), F32),
        scratch_shapes=[pltpu.VMEM((8, R, C), F32), pltpu.SemaphoreType.DMA((7,)), pltpu.SemaphoreType.DMA((7,))],
        compiler_params=pltpu.CompilerParams(vmem_limit_bytes=VMEM_LIMIT_V7X),
    )(v)


def allgather_hook(v, n_steps):
    R, C = v.shape

    def copies(v_ref, out, sems, kind):
        send_sems, recv_sems, _ = sems
        x, y, c, chips = _place()
        me, sibling = (x, y, c), (x, y, 1 - c)

        def slot(px, py, pc):
            return out.at[4 * px + 2 * py + pc]

        def copy(k, block, to, src=None):
            return pltpu.make_async_remote_copy(src_ref=slot(*block) if src is None else src, dst_ref=slot(*block),
                                                send_sem=send_sems.at[k], recv_sem=recv_sems.at[k], device_id=to,
                                                device_id_type=MESH)

        if kind == "mine":
            return [copy(0, me, sibling, src=v_ref)] + [copy(1 + j, me, (*chip, c), src=v_ref) for j, chip in enumerate(chips)]
        if kind == "theirs":
            return [copy(1 + j, (*chip, c), me) for j, chip in enumerate(chips)]
        if kind == "onward":
            return [copy(4 + j, (*chip, c), sibling) for j, chip in enumerate(chips)]
        return [copy(0, sibling, me)] + [copy(4 + j, (*chip, 1 - c), me) for j, chip in enumerate(chips)]

    def own(v_ref, out, sems):
        x, y, c, _ = _place()
        return pltpu.make_async_copy(v_ref, out.at[4 * x + 2 * y + c], sems[2])

    def start(ins, outs, sems):
        own(ins[0], outs[0], sems).start()
        for cp in copies(ins[0], outs[0], sems, "mine"):
            cp.start()

    def relay(ins, outs, sems):
        for theirs, onward in zip(copies(ins[0], outs[0], sems, "theirs"), copies(ins[0], outs[0], sems, "onward")):
            theirs.wait_recv()
            onward.start()

    def finish(ins, outs, sems):
        for cp in copies(ins[0], outs[0], sems, "relayed"):
            cp.wait_recv()
        for cp in copies(ins[0], outs[0], sems, "mine") + copies(ins[0], outs[0], sems, "onward"):
            cp.wait_send()
        own(ins[0], outs[0], sems).wait()

    return Hook(ins=(v,), out_shapes=(jax.ShapeDtypeStruct((8, R, C), F32),),
                sems=(pltpu.SemaphoreType.DMA((7,)), pltpu.SemaphoreType.DMA((7,)), pltpu.SemaphoreType.DMA(())),
                steps=((0, "before", start), ((n_steps * 13) // 16, "before", relay), (n_steps - 1, "after", finish)))


def merge_hooks(a, b):
    cut = (len(a.ins), len(a.out_shapes), len(a.sems))

    def part(fn, second):
        def run(ins, outs, sems):
            if second:
                fn(ins[cut[0]:], outs[cut[1]:], sems[cut[2]:])
            else:
                fn(ins[:cut[0]], outs[:cut[1]], sems[:cut[2]])
        return run

    steps = tuple((s, w, part(fn, False)) for s, w, fn in a.steps) + tuple((s, w, part(fn, True)) for s, w, fn in b.steps)
    alias = a.alias + tuple((i + cut[0], o + cut[1]) for i, o in b.alias)
    return Hook(a.ins + b.ins, a.out_shapes + b.out_shapes, alias, a.sems + b.sems, steps)


def sum_blocks(g):
    n, R, C = g.shape

    def body(g_ref, o_ref):
        acc = g_ref[0]
        for k in range(1, n):
            acc = acc + g_ref[k]
        o_ref[...] = acc

    return pl.pallas_call(body, name="sum_blocks", grid=(1,), in_specs=[_full(g.shape)], out_specs=_acc((R, C)),
                          out_shape=jax.ShapeDtypeStruct((R, C), F32), compiler_params=_params("arbitrary"))(g)


SMALL_PAD = 1024


def _pack_small(arrs):
    parts = []
    for a in arrs:
        f = a.reshape(-1).astype(F32)
        parts.append(jnp.pad(f, (0, (-f.shape[0]) % SMALL_PAD)))
    return jnp.concatenate(parts).reshape(-1, 128)


def _unpack_small(packed, shapes):
    flat = packed.reshape(-1)
    outs, off = [], 0
    for s in shapes:
        size = int(np.prod(s))
        outs.append(flat[off:off + size].reshape(s))
        off += size + (-size) % SMALL_PAD
    return outs


ATT_KEYS = ATT_WIN * ATT_TILE
ATT_NEAR_LO = (ATT_WIN - 1) * ATT_TILE - REL_CLIP + 1
ATT_PERIOD = ATT_KEYS + ATT_TILE + 1


def _att_bias_table(rel_bias):
    far = rel_bias[:, 2 * REL_CLIP:]
    near = rel_bias[:, 2 * REL_CLIP - 1:0:-1]
    w = jnp.concatenate([jnp.broadcast_to(far, (ATT_HEADS, ATT_NEAR_LO)), near,
                         jnp.broadcast_to(far, (ATT_HEADS, ATT_PERIOD - ATT_KEYS))], axis=1)
    rows = jnp.tile(w, (1, ATT_TILE))[:, :ATT_TILE * (ATT_PERIOD - 1)].reshape(ATT_HEADS, ATT_TILE, ATT_PERIOD - 1)
    i = np.arange(ATT_TILE)[:, None]
    m = np.arange(ATT_KEYS)[None, :]
    qc, kc = i // CHUNK, m // CHUNK
    band = (kc >= qc) & (kc <= qc + N_PREV_CHUNKS)
    return jnp.where(band[None], rows[:, :, :ATT_KEYS], NEG_INF)


def _rel_bias_grad(dbias):
    p = jnp.pad(dbias[:, :, ATT_TILE:], ((0, 0), (0, 0), (ATT_FAR, ATT_PERIOD - 1 - ATT_KEYS))).reshape(ATT_HEADS, -1)
    p = jnp.pad(p, ((0, 0), (0, ATT_TILE))).reshape(ATT_HEADS, ATT_TILE, ATT_PERIOD)
    gw, far = bias_colsum(p, dbias[:, :, :ATT_TILE])
    return jnp.concatenate([jnp.zeros((ATT_HEADS, 1), F32), gw[:, ATT_KEYS - 1:ATT_NEAR_LO - 1:-1], far[:, :1]], axis=1)


def kernel(x, mem, norm_mix_g, w_in, rel_bias, sg_ln_g, sg_ln_b, sg_w, sg_b, w_branch_att, w_branch_sg, w_out, norm_xattn_g, norm_mem_g, w_xq, w_xkv, w_xo, norm_ffn_g, w_ffn_in, w_ffn_out, norm_final_g, loss_target, m_norm_mix_g, m_w_in, m_rel_bias, m_sg_ln_g, m_sg_ln_b, m_sg_w, m_sg_b, m_w_branch_att, m_w_branch_sg, m_w_out, m_norm_xattn_g, m_norm_mem_g, m_w_xq, m_w_xkv, m_w_xo, m_norm_ffn_g, m_w_ffn_in, m_w_ffn_out, m_norm_final_g, v_norm_mix_g, v_w_in, v_rel_bias, v_sg_ln_g, v_sg_ln_b, v_sg_w, v_sg_b, v_w_branch_att, v_w_branch_sg, v_w_out, v_norm_xattn_g, v_norm_mem_g, v_w_xq, v_w_xkv, v_w_xo, v_norm_ffn_g, v_w_ffn_in, v_w_ffn_out, v_norm_final_g):
    S, D = x.shape[1], x.shape[2]
    x2d, mem2d, tgt = x[0], mem[0], loss_target[0]

    big_names = ["w_in", "w_branch_att", "w_branch_sg", "w_out", "w_xq", "w_xkv", "w_xo", "w_ffn_in", "w_ffn_out"]
    col_sharded = [True, True, True, False, False, True, False, True, False]
    big_w = [a[0] for a in (w_in, w_branch_att, w_branch_sg, w_out, w_xq, w_xkv, w_xo, w_ffn_in, w_ffn_out)]
    big_m = [a[0] for a in (m_w_in, m_w_branch_att, m_w_branch_sg, m_w_out, m_w_xq, m_w_xkv, m_w_xo, m_w_ffn_in, m_w_ffn_out)]
    big_v = [a[0] for a in (v_w_in, v_w_branch_att, v_w_branch_sg, v_w_out, v_w_xq, v_w_xkv, v_w_xo, v_w_ffn_in, v_w_ffn_out)]

    own = (2 * lax.axis_index("x") + lax.axis_index("y")).astype(jnp.int32).reshape(1)
    core = lax.axis_index("c").astype(jnp.int32).reshape(1)
    placed = [place_shard(w, own, "place_" + nm) for w, nm in zip(big_w, big_names)]

    def whole(g4, i):
        R, C = big_w[i].shape
        return g4.reshape(4, R, C) if col_sharded[i] else g4.reshape(4 * R, C)

    W_in = whole(gather_weights(placed[:1])[0], 0)

    g_mix, g_xat, g_mem, g_ffn = norm_mix_g, norm_xattn_g, norm_mem_g, norm_ffn_g
    g_fin = norm_final_g.reshape(1, D)
    bias = _att_bias_table(rel_bias[0])
    tt = np.arange(SG_BLOCK)
    sg_mask = (tt[None, :] // CHUNK) <= (tt[:, None] // CHUNK)
    wm_f = jnp.where(sg_mask[None], sg_w[0], 0.0)
    pairs = wm_f.reshape(SG_PAIRS, 2, SG_BLOCK, SG_BLOCK)
    wpair = jnp.transpose(pairs, (0, 2, 1, 3)).reshape(SG_PAIRS, SG_BLOCK, 2 * SG_BLOCK).astype(BF16)
    wtpair = jnp.transpose(pairs, (0, 3, 1, 2)).reshape(SG_PAIRS, SG_BLOCK, 2 * SG_BLOCK).astype(BF16)
    maskpair = jnp.asarray(np.tile(sg_mask, (SG_PAIRS, 1, 2)), F32)
    bfull = jnp.repeat(sg_b[0].T, SG_GROUP_DIM, axis=1)
    lng, lnb = sg_ln_g[0].reshape(1, 512), sg_ln_b[0].reshape(1, 512)
    gid = np.arange(512) // SG_GROUP_DIM
    amean = jnp.asarray((gid[:, None] == gid[None, :]) / SG_GROUP_DIM, BF16)

    h1, z, *early = norm_mm(x2d, g_mix, W_in, tm=IN_TILE, scale=None, name="norm_mm_in",
                            hook=gather_hook(placed[1:7], S // IN_TILE))
    y_att, lse_att, *late = attn_fwd(z, bias, gather_hook(placed[7:], S // ATT_TILE))
    W_ba, W_bs, W_out, W_xq, W_xkv, W_xo, W_fi, W_fo = [whole(g4, i + 1) for i, g4 in enumerate(early + late)]
    y_sg, = sgu_fwd(z, amean, lng, lnb, wpair, bfull, tm=SGU_TILE)
    a_br, b_br, merged, x1 = merge_fwd(y_att, y_sg, z, x2d, W_ba, W_bs, W_out, tm=IN_TILE)
    h2, xq = norm_mm(x1, g_xat, W_xq.reshape(1, D, D), tm=XATT_TILE, scale=XATT_HEAD_DIM ** -0.5, name="norm_mm_xq")
    mn, kv = norm_mm(mem2d, g_mem, W_xkv, tm=mem2d.shape[0], scale=None, name="norm_mm_kv")
    o_x, lse_x, x2 = xattn_fwd(xq, kv, W_xo, x1, tm=XATT_TILE)
    h3, gu = norm_mm(x2, g_ffn, W_fi, tm=IN_TILE, scale=None, name="norm_mm_ffn")
    act, loss_vec, dx3, dg_fin = ffn_out_loss(gu, W_fo, x2, g_fin, tgt, tm=ROW_TILE)

    dgu = ffn_act_bwd(dx3, gu, W_fo, tm=ROW_TILE, nchunk=2)
    gW_fo = tn_mm(act, dx3, shards=1, tn=D, tk=DW_TOKENS, name="dw_ffn_out")
    gW_fi = tn_mm(h3, dgu, shards=4, tn=2 * W_fi.shape[2], tk=DW_TOKENS, name="dw_ffn_in")
    dx2, dg_ffn = mm_nt_norm_bwd(dgu, W_fi, x2, g_ffn, dx3, tm=ROW_TILE, name="dx_ffn")
    dq_x, dkv = xattn_bwd(dx2, xq, o_x, lse_x, kv, W_xo, tm=XATT_TILE)
    gW_xo = tn_mm(o_x, dx2, shards=1, tn=D, tk=DW_TOKENS, name="dw_xo")
    gW_xq = tn_mm(h2, dq_x, shards=1, tn=D, tk=DW_TOKENS, name="dw_xq")
    dx1, dg_xat = mm_nt_norm_bwd(dq_x, W_xq.reshape(1, D, D), x1, g_xat, dx2, tm=XATT_TILE, name="dx_xq")
    gW_xkv = tn_mm(mn, dkv, shards=4, tn=2 * D, tk=mem2d.shape[0], name="dw_xkv")
    _, dg_mem = mm_nt_norm_bwd(dkv.astype(BF16), W_xkv, mem2d, g_mem, None, tm=mem2d.shape[0], name="dx_mem")
    def canon(g, i):
        R, C = big_w[i].shape
        return g.reshape(4, 2, R // 2, C)

    def partials(gs, recv, idx):
        return [chip_partial(g, r, core, "chip_partial_" + big_names[i]) for g, r, i in zip(gs, recv, idx)]

    early_g = [canon(g, i) for g, i in zip([gW_xq, gW_xkv, gW_xo, gW_fi, gW_fo], range(4, 9))]
    da, db, dgab, dy_att, dy_sg, *early_r = merge_bwd(dx1, W_out, a_br, b_br, z, W_ba, W_bs, tm=ROW_TILE,
                                                      hook=split_hook(early_g, S // ROW_TILE))
    gW_out = tn_mm(merged, dx1, shards=1, tn=D, tk=DW_TOKENS, name="dw_out")
    gW_ba = tn_mm(y_att, da, shards=4, tn=D, tk=DW_TOKENS, name="dw_branch_att")
    gW_bs = tn_mm(y_sg, db, shards=4, tn=D, tk=DW_TOKENS, name="dw_branch_sg")
    mid_g = [canon(g, i) for g, i in zip([gW_ba, gW_bs, gW_out], range(1, 4))]
    parts_b = (partials(mid_g, sibling_split(mid_g, "sibling_split_mid"), range(1, 4))
               + partials(early_g, early_r, range(4, 9)))
    nt_bwd = S // ATT_TILE + ATT_WIN - 1
    dqkv, dbias, *from_chips_b = attn_bwd(z, y_att, dy_att, lse_att, bias, exchange_hook(parts_b, nt_bwd))
    sums_b = [shard_sum(p, r, own, core, "shard_sum_" + nm) for p, r, nm in zip(parts_b, from_chips_b, big_names[1:])]
    duv, dwpair, dsgb_full, dlng, dlnb, *joined_b = sgu_bwd(z, dy_sg, amean, lng, lnb, wpair, wtpair, bfull, maskpair,
                                                         tm=SGU_TILE, hook=join_hook(sums_b, S // SGU_TILE))
    dwm = jnp.transpose(dwpair.reshape(SG_PAIRS, SG_BLOCK, 2, SG_BLOCK), (0, 2, 1, 3))
    dsgb = dsgb_full[:, ::SG_GROUP_DIM].T
    dz = [dqkv, duv, dgab]
    gW_in = tn_mm_pieces(h1, dz, shards=4, per=2, tk=DW_TOKENS, name="dw_in")
    in_g = [canon(gW_in, 0)]
    parts_a = partials(in_g, sibling_split(in_g, "sibling_split_w_in"), [0])
    d_rel = _rel_bias_grad(dbias)
    small_g = [d_rel, dlng, dlnb, dwm, dsgb, dg_xat, dg_mem, dg_ffn, dg_fin, loss_vec[0, :1]]
    n_dx = S // ROW_TILE
    dx, dg_mix, from_chips_a, small_all = mm_nt_norm_bwd(
        dz, W_in, x2d, g_mix, dx1, tm=ROW_TILE, name="dx_in",
        hook=merge_hooks(exchange_hook(parts_a, n_dx), allgather_hook(_pack_small(small_g), n_dx)))

    joined = list(sibling_join([shard_sum(parts_a[0], from_chips_a, own, core, "shard_sum_w_in")])) + list(joined_b)
    big_out = []
    for j, w, m_, v_, nm in zip(joined, big_w, big_m, big_v, big_names):
        g = j.reshape(w.shape)
        big_out.append((g,) + tuple(adamw(w, g, m_, v_, "adamw_" + nm)))

    small_w = [norm_mix_g, rel_bias, sg_ln_g, sg_ln_b, sg_w, sg_b, norm_xattn_g, norm_mem_g, norm_ffn_g, norm_final_g]
    small_m = [m_norm_mix_g, m_rel_bias, m_sg_ln_g, m_sg_ln_b, m_sg_w, m_sg_b, m_norm_xattn_g, m_norm_mem_g, m_norm_ffn_g, m_norm_final_g]
    small_v = [v_norm_mix_g, v_rel_bias, v_sg_ln_g, v_sg_ln_b, v_sg_w, v_sg_b, v_norm_xattn_g, v_norm_mem_g, v_norm_ffn_g, v_norm_final_g]
    shapes = [w.shape for w in small_w]
    g_sum = jnp.concatenate([allreduce_small(_pack_small([dg_mix])), sum_blocks(small_all)], axis=0)
    zero = jnp.zeros((1,), F32)
    d_s, m_s, v_s = adamw(_pack_small(small_w + [zero]), g_sum, _pack_small(small_m + [zero]), _pack_small(small_v + [zero]),
                          "adamw_small")
    sg_, sd, sm, sv_ = (_unpack_small(p, shapes + [(1,)]) for p in (g_sum, d_s, m_s, v_s))
    loss = sg_[-1][0]

    order = ["norm_mix_g", "w_in", "rel_bias", "sg_ln_g", "sg_ln_b", "sg_w", "sg_b", "w_branch_att", "w_branch_sg", "w_out",
             "norm_xattn_g", "norm_mem_g", "w_xq", "w_xkv", "w_xo", "norm_ffn_g", "w_ffn_in", "w_ffn_out", "norm_final_g"]
    small_names = ["norm_mix_g", "rel_bias", "sg_ln_g", "sg_ln_b", "sg_w", "sg_b", "norm_xattn_g", "norm_mem_g", "norm_ffn_g",
                   "norm_final_g"]
    res = {}
    for i, nm in enumerate(small_names):
        res[nm] = (sg_[i], sd[i], sm[i], sv_[i])
    for nm, outs in zip(big_names, big_out):
        res[nm] = tuple(o[None] for o in outs)
    return (loss, dx[None], *[res[nm][0] for nm in order], *[res[nm][1] for nm in order],
            *[res[nm][2] for nm in order], *[res[nm][3] for nm in order])
```

```python
import functools
from typing import NamedTuple

import numpy as np
import jax
import jax.numpy as jnp
from jax import lax
from jax.experimental import pallas as pl
from jax.experimental.pallas import tpu as pltpu

F32, BF16 = jnp.float32, jnp.bfloat16
MESH = pl.DeviceIdType.MESH

EPS = 1e-6
NEG_INF = -1e30
CHUNK = 64
N_PREV_CHUNKS = 8
ATT_HEADS, ATT_HEAD_DIM = 8, 64
REL_CLIP = 128
SG_BLOCK, SG_GROUPS, SG_GROUP_DIM = 128, 8, 64
XATT_HEADS, XATT_HEAD_DIM = 4, 256
ATT_TILE = 128
ATT_WIN = 5
ADAM_LR, ADAM_B1, ADAM_B2, ADAM_EPS, ADAM_WD, ADAM_STEP = 0.001, 0.9, 0.999, 1e-08, 0.01, 10

VMEM_LIMIT_V7X = 56 * 1024 * 1024
EW_BLOCK_BYTES = 2 << 20
ROW_TILE = 512
DW_TOKENS = 1024
XATT_TILE = 1024
SGU_TILE = 1024
HBM = pl.BlockSpec(memory_space=pl.ANY)


def _params(*sem):
    return pltpu.CompilerParams(dimension_semantics=sem, vmem_limit_bytes=VMEM_LIMIT_V7X)


def _full(shape):
    n = len(shape)
    return pl.BlockSpec(shape, lambda *_: (0,) * n, pipeline_mode=pl.Buffered(1))


def _acc(shape):
    n = len(shape)
    return pl.BlockSpec(shape, lambda *_: (0,) * n)


class Hook(NamedTuple):
    ins: tuple = ()
    out_shapes: tuple = ()
    alias: tuple = ()
    sems: tuple = ()
    steps: tuple = ()


def _run_hook(hook, pos, h_in, h_out, h_sems):
    for at, where, fn in hook.steps:
        if where == pos:
            pl.when(pl.program_id(0) == at)(functools.partial(fn, h_in, h_out, h_sems))


def _hooked_call(body, hook, *, n_in, n_out, in_specs, out_specs, out_shape, scratch_shapes=(), **kw):
    nh = len(hook.ins)
    aliases = {n_in + i: n_out + o for i, o in hook.alias}
    return pl.pallas_call(
        body, in_specs=list(in_specs) + [HBM] * nh, out_specs=list(out_specs) + [HBM] * len(hook.out_shapes),
        out_shape=list(out_shape) + list(hook.out_shapes), scratch_shapes=list(scratch_shapes) + list(hook.sems),
        input_output_aliases=aliases, **kw)


def _nt(a, b):
    return lax.dot_general(a, b, (((1,), (1,)), ((), ())), preferred_element_type=F32)


def _tn(a, b):
    return lax.dot_general(a, b, (((0,), (0,)), ((), ())), preferred_element_type=F32)


def _nn(a, b):
    return jnp.dot(a, b, preferred_element_type=F32)


def _sigmoid(x):
    return 0.5 * jnp.tanh(0.5 * x) + 0.5


_GELU_C = float(np.sqrt(2.0 / np.pi))


def _gelu(x):
    t = jnp.tanh(_GELU_C * (x + 0.044715 * (x * x * x)))
    return x * (0.5 * (1.0 + t))


def _gelu_grad(x):
    t = jnp.tanh(_GELU_C * (x + 0.044715 * (x * x * x)))
    return 0.5 * (1.0 + t) + 0.5 * x * (1.0 - t * t) * (_GELU_C * (1.0 + 3.0 * 0.044715 * x * x))


def _rms_stats(xf):
    rstd = lax.rsqrt(jnp.mean(xf * xf, axis=-1, keepdims=True) + EPS)
    return rstd, xf * rstd


def _rms_bwd(xhat, rstd, g, dh):
    dxh = dh * g
    dx = rstd * (dxh - xhat * jnp.mean(dxh * xhat, axis=-1, keepdims=True))
    return dx, dh * xhat


def norm_mm(x, g, w, *, tm, scale, name, hook=Hook()):
    S, D = x.shape
    J, _, C = w.shape
    nh, nho = len(hook.ins), len(hook.out_shapes)

    def body(x_ref, g_ref, w_ref, *rest):
        h_in, (h_ref, z_ref), h_out, h_sems = rest[:nh], rest[nh:nh + 2], rest[nh + 2:nh + 2 + nho], rest[nh + 2 + nho:]
        _run_hook(hook, "before", h_in, h_out, h_sems)
        _, xhat = _rms_stats(x_ref[...])
        h = (xhat * g_ref[...]).astype(BF16)
        h_ref[...] = h
        for j in range(J):
            acc = _nn(h, w_ref[j])
            if scale is not None:
                acc = acc * scale
            z_ref[:, j * C:(j + 1) * C] = acc.astype(BF16)
        _run_hook(hook, "after", h_in, h_out, h_sems)

    return _hooked_call(
        body, hook, n_in=3, n_out=2, grid=(S // tm,), name=name,
        in_specs=[pl.BlockSpec((tm, D), lambda i: (i, 0)), _full((1, D)), _full((J, D, C))],
        out_specs=[pl.BlockSpec((tm, D), lambda i: (i, 0)), pl.BlockSpec((tm, J * C), lambda i: (i, 0))],
        out_shape=[jax.ShapeDtypeStruct((S, D), BF16), jax.ShapeDtypeStruct((S, J * C), BF16)],
        compiler_params=_params("arbitrary"),
    )(x, g, w, *hook.ins)


def _att_window_specs(nt, col):
    return [pl.BlockSpec((ATT_TILE, 512), lambda t, j=j: (jnp.clip(t - (ATT_WIN - 1) + j, 0, nt - 1), col))
            for j in range(ATT_WIN)]


ATT_SCALE = ATT_HEAD_DIM ** -0.5
ATT_FAR = 3 * ATT_TILE


def _att_scores(q_scaled, k, bias, valid):
    return jnp.where(valid, _nt(q_scaled, k) + bias, NEG_INF)


def _att_valid(t):
    kpos = lax.broadcasted_iota(jnp.int32, (ATT_TILE, ATT_WIN * ATT_TILE), 1) + (t - (ATT_WIN - 1)) * ATT_TILE
    return kpos >= 0


def attn_fwd(z, bias, hook=Hook()):
    S = z.shape[0]
    nt = S // ATT_TILE
    n_in, nh, nho = 2 + 2 * ATT_WIN, len(hook.ins), len(hook.out_shapes)

    def body(q_ref, *refs):
        k_refs, v_refs = refs[:ATT_WIN], refs[ATT_WIN:2 * ATT_WIN]
        bias_ref = refs[2 * ATT_WIN]
        rest = refs[2 * ATT_WIN + 1:]
        h_in, (y_ref, lse_ref), h_out = rest[:nh], rest[nh:nh + 2], rest[nh + 2:nh + 2 + nho]
        s_scr, p_scr = rest[nh + 2 + nho:nh + 4 + nho]
        h_sems = rest[nh + 4 + nho:]
        _run_hook(hook, "before", h_in, h_out, h_sems)
        valid = _att_valid(pl.program_id(0))
        heads = [slice(h * ATT_HEAD_DIM, (h + 1) * ATT_HEAD_DIM) for h in range(ATT_HEADS)]
        for h, sl in enumerate(heads):
            k = jnp.concatenate([r[:, sl] for r in k_refs], axis=0)
            s_scr[h] = _att_scores(q_ref[:, sl] * ATT_SCALE, k, bias_ref[h], valid)
        stats = []
        for h in range(ATT_HEADS):
            s = s_scr[h]
            m = jnp.max(s, axis=-1, keepdims=True)
            p = jnp.exp(s - m)
            stats.append((m, jnp.sum(p, axis=-1, keepdims=True)))
            p_scr[h] = p.astype(BF16)
        for h, sl in enumerate(heads):
            m, l = stats[h]
            v = jnp.concatenate([r[:, sl] for r in v_refs], axis=0)
            y_ref[:, sl] = (_nn(p_scr[h], v) / l).astype(BF16)
            lse_ref[:, h:h + 1] = m + jnp.log(l)
        _run_hook(hook, "after", h_in, h_out, h_sems)

    tile = lambda col: pl.BlockSpec((ATT_TILE, 512), lambda t: (t, col))
    return _hooked_call(
        body, hook, n_in=n_in, n_out=2, grid=(nt,), name="attn_fwd",
        in_specs=[tile(0)] + _att_window_specs(nt, 1) + _att_window_specs(nt, 2) + [_full(bias.shape)],
        out_specs=[tile(0), pl.BlockSpec((ATT_TILE, ATT_HEADS), lambda t: (t, 0))],
        out_shape=[jax.ShapeDtypeStruct((S, 512), BF16), jax.ShapeDtypeStruct((S, ATT_HEADS), F32)],
        scratch_shapes=[pltpu.VMEM((ATT_HEADS, ATT_TILE, ATT_WIN * ATT_TILE), F32),
                        pltpu.VMEM((ATT_HEADS, ATT_TILE, ATT_WIN * ATT_TILE), BF16)],
        compiler_params=_params("arbitrary"),
    )(z, *([z] * (2 * ATT_WIN)), bias, *hook.ins)


SG_PAIRS = SG_GROUPS // 2


def _group_mean(x, a_ref):
    rows = x.shape[0]
    hi = x.astype(BF16)
    lo = (x - hi.astype(F32)).astype(BF16)
    both = _nn(jnp.concatenate([hi, lo], axis=0), a_ref[...])
    return both[:rows] + both[rows:]


def _pair_diag(x, ref):
    first = lax.broadcasted_iota(jnp.int32, x.shape, 1) < SG_GROUP_DIM
    ref[0:SG_BLOCK, :] = jnp.where(first, x, 0.0).astype(BF16)
    ref[SG_BLOCK:2 * SG_BLOCK, :] = jnp.where(first, 0.0, x).astype(BF16)


def _sgu_norm(zv_ref, a_ref, lng_ref, lnb_ref):
    v = _gelu(zv_ref[...].astype(F32))
    vc = v - _group_mean(v, a_ref)
    rstd = lax.rsqrt(_group_mean(vc * vc, a_ref) + EPS)
    xhat = vc * rstd
    return xhat, rstd, xhat * lng_ref[...] + lnb_ref[...]


def _sgu_rows(b):
    return pl.ds(pl.multiple_of(b * SG_BLOCK, SG_BLOCK), SG_BLOCK)


def sgu_fwd(z, amean, lng, lnb, wpair, bfull, *, tm, hook=Hook()):
    S = z.shape[0]
    nh, nho = len(hook.ins), len(hook.out_shapes)

    def body(zu_ref, zv_ref, a_ref, lng_ref, lnb_ref, wpair_ref, bfull_ref, *rest):
        h_in, y_ref, h_out = rest[:nh], rest[nh], rest[nh + 1:nh + 1 + nho]
        vn_scr, vbd = rest[nh + 1 + nho:nh + 3 + nho]
        h_sems = rest[nh + 3 + nho:]
        _run_hook(hook, "before", h_in, h_out, h_sems)
        vn_scr[...] = _sgu_norm(zv_ref, a_ref, lng_ref, lnb_ref)[2]

        def block(b, carry):
            rows = _sgu_rows(b)
            u = _gelu(zu_ref[rows, :].astype(F32))
            for p in range(SG_PAIRS):
                lanes = slice(p * 128, (p + 1) * 128)
                _pair_diag(vn_scr[rows, lanes], vbd.at[p])
                sv = _nn(wpair_ref[p], vbd[p]) + bfull_ref[:, lanes]
                y_ref[rows, lanes] = (u[:, lanes] * sv).astype(BF16)
            return carry

        lax.fori_loop(0, tm // SG_BLOCK, block, 0)
        _run_hook(hook, "after", h_in, h_out, h_sems)

    tile = lambda col: pl.BlockSpec((tm, 512), lambda i: (i, col))
    smalls = [amean, lng, lnb, wpair, bfull]
    return _hooked_call(
        body, hook, n_in=2 + len(smalls), n_out=1, grid=(S // tm,), name="sgu_fwd",
        in_specs=[tile(3), tile(4)] + [_full(a.shape) for a in smalls],
        out_specs=[tile(0)], out_shape=[jax.ShapeDtypeStruct((S, 512), BF16)],
        scratch_shapes=[pltpu.VMEM((tm, 512), F32), pltpu.VMEM((SG_PAIRS, 2 * SG_BLOCK, 128), BF16)],
        compiler_params=_params("arbitrary"),
    )(z, z, *smalls, *hook.ins)


def _gate_specs(tm):
    return [pl.BlockSpec((tm, 512), lambda i, c=c: (i, c)) for c in (5, 6, 7, 8)]


def merge_fwd(y_att, y_sg, z, x, w_ba, w_bs, w_out, *, tm):
    S, D = x.shape
    J, _, C = w_ba.shape

    def body(ya_ref, ys_ref, g0, g1, g2, g3, x_ref, wba_ref, wbs_ref, wo_ref, a_ref, b_ref, m_ref, x1_ref):
        ya, ys = ya_ref[...], ys_ref[...]
        a = jnp.concatenate([_nn(ya, wba_ref[j]) for j in range(J)], axis=1)
        b = jnp.concatenate([_nn(ys, wbs_ref[j]) for j in range(J)], axis=1)
        sa = _sigmoid(jnp.concatenate([g0[...], g1[...]], axis=1).astype(F32))
        sb = _sigmoid(jnp.concatenate([g2[...], g3[...]], axis=1).astype(F32))
        a_ref[...] = a.astype(BF16)
        b_ref[...] = b.astype(BF16)
        merged = (sa * a + sb * b).astype(BF16)
        m_ref[...] = merged
        x1_ref[...] = x_ref[...] + _nn(merged, wo_ref[...])

    row = lambda n: pl.BlockSpec((tm, n), lambda i: (i, 0))
    return pl.pallas_call(
        body, grid=(S // tm,), name="merge_fwd",
        in_specs=[row(512), row(512)] + _gate_specs(tm) + [row(D), _full(w_ba.shape), _full(w_bs.shape), _full(w_out.shape)],
        out_specs=[row(D)] * 4,
        out_shape=[jax.ShapeDtypeStruct((S, D), BF16)] * 3 + [jax.ShapeDtypeStruct((S, D), F32)],
        compiler_params=_params("parallel"),
    )(y_att, y_sg, z, z, z, z, x, w_ba, w_bs, w_out)


def xattn_fwd(xq, kv, w_xo, x1, *, tm):
    S, D = xq.shape
    dh = XATT_HEAD_DIM

    def body(q_ref, kv_ref, wo_ref, x1_ref, o_ref, lse_ref, x2_ref):
        outs = []
        for h in range(XATT_HEADS):
            s = _nt(q_ref[:, h * dh:(h + 1) * dh], kv_ref[:, h * dh:(h + 1) * dh])
            m = jnp.max(s, axis=-1, keepdims=True)
            p = jnp.exp(s - m)
            l = jnp.sum(p, axis=-1, keepdims=True)
            outs.append((_nn(p.astype(BF16), kv_ref[:, D + h * dh:D + (h + 1) * dh]) / l).astype(BF16))
            lse_ref[:, h:h + 1] = m + jnp.log(l)
        o = jnp.concatenate(outs, axis=1)
        o_ref[...] = o
        x2_ref[...] = x1_ref[...] + _nn(o, wo_ref[...])

    row = lambda n: pl.BlockSpec((tm, n), lambda i: (i, 0))
    return pl.pallas_call(
        body, grid=(S // tm,), name="xattn_fwd",
        in_specs=[row(D), _full(kv.shape), _full(w_xo.shape), row(D)],
        out_specs=[row(D), row(XATT_HEADS), row(D)],
        out_shape=[jax.ShapeDtypeStruct((S, D), BF16), jax.ShapeDtypeStruct((S, XATT_HEADS), F32),
                   jax.ShapeDtypeStruct((S, D), F32)],
        compiler_params=_params("parallel"),
    )(xq, kv, w_xo, x1)


def ffn_out_loss(gu, w, x2, g, target, *, tm):
    S, D = x2.shape
    F = w.shape[0]

    def body(gu_ref, w_ref, x2_ref, g_ref, t_ref, act_ref, loss_ref, dx_ref, dg_ref):
        @pl.when(pl.program_id(0) == 0)
        def _():
            loss_ref[...] = jnp.zeros_like(loss_ref)
            dg_ref[...] = jnp.zeros_like(dg_ref)

        half = gu_ref[:, :F] * 0.5
        act = (half * gu_ref[:, F:]) * (jnp.tanh(half) + 1.0)
        act_ref[...] = act
        gv = g_ref[...]
        rstd, xhat = _rms_stats(x2_ref[...] + _nn(act, w_ref[...]))
        err = xhat * gv - t_ref[...]
        loss_ref[...] += 0.5 * jnp.sum(jnp.mean(err * err, axis=-1, keepdims=True))
        dx, dgc = _rms_bwd(xhat, rstd, gv, err * (1.0 / D))
        dx_ref[...] = dx
        dg_ref[...] += jnp.sum(dgc, axis=0, keepdims=True)

    row = lambda n: pl.BlockSpec((tm, n), lambda i: (i, 0))
    return pl.pallas_call(
        body, grid=(S // tm,), name="ffn_out_loss",
        in_specs=[row(2 * F), _full(w.shape), row(D), _full((1, D)), row(D)],
        out_specs=[row(F), _acc((1, 128)), row(D), _acc((1, D))],
        out_shape=[jax.ShapeDtypeStruct((S, F), BF16), jax.ShapeDtypeStruct((1, 128), F32),
                   jax.ShapeDtypeStruct((S, D), F32), jax.ShapeDtypeStruct((1, D), F32)],
        compiler_params=_params("arbitrary"),
    )(gu, w, x2, g, target)


def ffn_act_bwd(dx3, gu, w, *, tm, nchunk):
    S, D = dx3.shape
    F = w.shape[0]
    cn = F // nchunk

    def body(dx_ref, gu_ref, w_ref, dgu_ref):
        dxb = dx_ref[...].astype(BF16)
        for j in range(nchunk):
            dact = _nt(dxb, w_ref[j * cn:(j + 1) * cn, :])
            gate = gu_ref[:, j * cn:(j + 1) * cn]
            up = gu_ref[:, F + j * cn:F + (j + 1) * cn]
            sg = _sigmoid(gate)
            silu = gate * sg
            dsilu = sg + silu * (1.0 - sg)
            dgu_ref[:, j * cn:(j + 1) * cn] = (dact * (up * dsilu).astype(F32)).astype(BF16)
            dgu_ref[:, F + j * cn:F + (j + 1) * cn] = (dact * silu.astype(F32)).astype(BF16)

    row = lambda n: pl.BlockSpec((tm, n), lambda i: (i, 0))
    return pl.pallas_call(
        body, grid=(S // tm,), name="ffn_act_bwd",
        in_specs=[row(D), row(2 * F), _full(w.shape)], out_specs=row(2 * F),
        out_shape=jax.ShapeDtypeStruct((S, 2 * F), BF16), compiler_params=_params("parallel"),
    )(dx3, gu, w)


def _overlaps(widths, lo, hi):
    out, off = [], 0
    for p, wd in enumerate(widths):
        a, b = max(lo, off), min(hi, off + wd)
        if a < b:
            out.append((p, a - off, b - off, a - lo))
        off += wd
    return out


def tn_mm_pieces(a, pieces, *, shards, per, tk, name):
    K, M = a.shape
    widths = [p.shape[1] for p in pieces]
    C = sum(widths) // shards
    tn = per * C
    nk = K // tk
    n_tiles = shards // per

    def body(a_ref, *refs):
        p_refs, o_ref, acc_ref = refs[:len(pieces)], refs[len(pieces)], refs[len(pieces) + 1]
        n, k = pl.program_id(0), pl.program_id(1)

        @pl.when(k == 0)
        def _():
            acc_ref[...] = jnp.zeros_like(acc_ref)

        av = a_ref[...]
        for tile in range(n_tiles):
            @pl.when(n == tile)
            def _(tile=tile):
                for p, c0, c1, at in _overlaps(widths, tile * tn, (tile + 1) * tn):
                    acc_ref[:, at:at + c1 - c0] += _tn(av, p_refs[p][:, c0:c1])

        @pl.when(k == nk - 1)
        def _():
            for s in range(per):
                o_ref[s] = acc_ref[:, s * C:(s + 1) * C].astype(BF16)

    return pl.pallas_call(
        body, grid=(n_tiles, nk), name=name,
        in_specs=[pl.BlockSpec((tk, M), lambda n, k: (k, 0))] + [pl.BlockSpec((tk, wd), lambda n, k: (k, 0)) for wd in widths],
        out_specs=pl.BlockSpec((per, M, C), lambda n, k: (n, 0, 0)), out_shape=jax.ShapeDtypeStruct((shards, M, C), BF16),
        scratch_shapes=[pltpu.VMEM((M, tn), F32)], compiler_params=_params("parallel", "arbitrary"),
    )(a, *pieces)


def tn_mm(a, b, *, shards, tn, tk, name):
    K, M = a.shape
    N = b.shape[1]
    C = N // shards
    per = tn // C
    nk = K // tk

    def body(a_ref, b_ref, o_ref, acc_ref):
        k = pl.program_id(1)

        @pl.when(k == 0)
        def _():
            acc_ref[...] = jnp.zeros_like(acc_ref)

        acc_ref[...] += _tn(a_ref[...].astype(BF16), b_ref[...].astype(BF16))

        @pl.when(k == nk - 1)
        def _():
            if shards > 1:
                for s in range(per):
                    o_ref[s] = acc_ref[:, s * C:(s + 1) * C].astype(BF16)
            else:
                o_ref[...] = acc_ref[...].astype(BF16)

    if shards > 1:
        out_spec = pl.BlockSpec((per, M, C), lambda n, k: (n, 0, 0))
        out_shape = jax.ShapeDtypeStruct((shards, M, C), BF16)
    else:
        out_spec = pl.BlockSpec((M, tn), lambda n, k: (0, n))
        out_shape = jax.ShapeDtypeStruct((M, N), BF16)
    return pl.pallas_call(
        body, grid=(N // tn, nk), name=name,
        in_specs=[pl.BlockSpec((tk, M), lambda n, k: (k, 0)), pl.BlockSpec((tk, tn), lambda n, k: (k, n))],
        out_specs=out_spec, out_shape=out_shape, scratch_shapes=[pltpu.VMEM((M, tn), F32)],
        compiler_params=_params("parallel", "arbitrary"),
    )(a, b)


def mm_nt_norm_bwd(dy, w, x, g, dx_in, *, tm, name, hook=Hook()):
    S, D = x.shape
    J, _, C = w.shape
    has_in = dx_in is not None
    dys = list(dy) if isinstance(dy, (list, tuple)) else [dy]
    widths = [d.shape[1] for d in dys]
    nd = len(dys)
    n_in, nh, nho = nd + 3 + has_in, len(hook.ins), len(hook.out_shapes)

    def body(*refs):
        dy_refs = refs[:nd]
        w_ref, x_ref, g_ref = refs[nd:nd + 3]
        dxin_ref = refs[nd + 3] if has_in else None
        rest = refs[n_in:]
        h_in, (dx_ref, dg_ref), h_out, h_sems = rest[:nh], rest[nh:nh + 2], rest[nh + 2:nh + 2 + nho], rest[nh + 2 + nho:]
        _run_hook(hook, "before", h_in, h_out, h_sems)

        @pl.when(pl.program_id(0) == 0)
        def _():
            dg_ref[...] = jnp.zeros_like(dg_ref)

        dh = None
        for j in range(J):
            for p, c0, c1, at in _overlaps(widths, j * C, (j + 1) * C):
                part = _nt(dy_refs[p][:, c0:c1], w_ref[j, :, at:at + c1 - c0])
                dh = part if dh is None else dh + part
        rstd, xhat = _rms_stats(x_ref[...])
        dx, dgc = _rms_bwd(xhat, rstd, g_ref[...], dh)
        dx_ref[...] = dx + dxin_ref[...] if has_in else dx
        dg_ref[...] += jnp.sum(dgc, axis=0, keepdims=True)
        _run_hook(hook, "after", h_in, h_out, h_sems)

    row = lambda n: pl.BlockSpec((tm, n), lambda i: (i, 0))
    ins = dys + [w, x, g] + ([dx_in] if has_in else [])
    return _hooked_call(
        body, hook, n_in=n_in, n_out=2, grid=(S // tm,), name=name,
        in_specs=[row(wd) for wd in widths] + [_full(w.shape), row(D), _full((1, D))] + ([row(D)] if has_in else []),
        out_specs=[row(D), _acc((1, D))],
        out_shape=[jax.ShapeDtypeStruct((S, D), F32), jax.ShapeDtypeStruct((1, D), F32)],
        compiler_params=_params("arbitrary"),
    )(*ins, *hook.ins)


def xattn_bwd(dx2, xq, o, lse, kv, w_xo, *, tm):
    S, D = xq.shape
    M = kv.shape[0]
    dh = XATT_HEAD_DIM

    def body(dx_ref, q_ref, o_ref, lse_ref, kv_ref, wo_ref, dq_ref, dkv_ref):
        @pl.when(pl.program_id(0) == 0)
        def _():
            dkv_ref[...] = jnp.zeros_like(dkv_ref)

        do = _nt(dx_ref[...].astype(BF16), wo_ref[...])
        for h in range(XATT_HEADS):
            hs = slice(h * dh, (h + 1) * dh)
            vs = slice(D + h * dh, D + (h + 1) * dh)
            q, k, v = q_ref[:, hs], kv_ref[:, hs], kv_ref[:, vs]
            do_h = do[:, hs]
            do_b = do_h.astype(BF16)
            p = jnp.exp(_nt(q, k) - lse_ref[:, h:h + 1])
            delta = jnp.sum(do_h * o_ref[:, hs].astype(F32), axis=-1, keepdims=True)
            ds = (p * (_nt(do_b, v) - delta)).astype(BF16)
            dq_ref[:, hs] = (_nn(ds, k) * (dh ** -0.5)).astype(BF16)
            dkv_ref[:, hs] += _tn(ds, q)
            dkv_ref[:, vs] += _tn(p.astype(BF16), do_b)

    row = lambda n: pl.BlockSpec((tm, n), lambda i: (i, 0))
    return pl.pallas_call(
        body, grid=(S // tm,), name="xattn_bwd",
        in_specs=[row(D), row(D), row(D), row(XATT_HEADS), _full(kv.shape), _full(w_xo.shape)],
        out_specs=[row(D), _acc((M, 2 * D))],
        out_shape=[jax.ShapeDtypeStruct((S, D), BF16), jax.ShapeDtypeStruct((M, 2 * D), F32)],
        compiler_params=_params("arbitrary"),
    )(dx2, xq, o, lse, kv, w_xo)


def merge_bwd(dx1, w_out, a, b, z, w_ba, w_bs, *, tm, hook=Hook()):
    S, D = dx1.shape
    J, W, C = w_ba.shape
    nh, nho = len(hook.ins), len(hook.out_shapes)

    def body(dx_ref, wo_ref, a_ref, b_ref, g0, g1, g2, g3, wba_ref, wbs_ref, *rest):
        h_in, (da_ref, db_ref, dg_ref, dya_ref, dys_ref) = rest[:nh], rest[nh:nh + 5]
        h_out, h_sems = rest[nh + 5:nh + 5 + nho], rest[nh + 5 + nho:]
        _run_hook(hook, "before", h_in, h_out, h_sems)
        dm = _nt(dx_ref[...].astype(BF16), wo_ref[...])
        sa = _sigmoid(jnp.concatenate([g0[...], g1[...]], axis=1))
        sb = _sigmoid(jnp.concatenate([g2[...], g3[...]], axis=1))
        dg_ref[:, :D] = (dm * (a_ref[...] * (sa * (1.0 - sa))).astype(F32)).astype(BF16)
        dg_ref[:, D:] = (dm * (b_ref[...] * (sb * (1.0 - sb))).astype(F32)).astype(BF16)
        da = (dm * sa.astype(F32)).astype(BF16)
        db = (dm * sb.astype(F32)).astype(BF16)
        da_ref[...] = da
        db_ref[...] = db
        dya = _nt(da[:, 0:C], wba_ref[0])
        dys = _nt(db[:, 0:C], wbs_ref[0])
        for j in range(1, J):
            dya += _nt(da[:, j * C:(j + 1) * C], wba_ref[j])
            dys += _nt(db[:, j * C:(j + 1) * C], wbs_ref[j])
        dya_ref[...] = dya.astype(BF16)
        dys_ref[...] = dys.astype(BF16)
        _run_hook(hook, "after", h_in, h_out, h_sems)

    row = lambda n: pl.BlockSpec((tm, n), lambda i: (i, 0))
    return _hooked_call(
        body, hook, n_in=10, n_out=5, grid=(S // tm,), name="merge_bwd",
        in_specs=[row(D), _full(w_out.shape), row(D), row(D)] + _gate_specs(tm) + [_full(w_ba.shape), _full(w_bs.shape)],
        out_specs=[row(D), row(D), row(2 * D), row(W), row(W)],
        out_shape=[jax.ShapeDtypeStruct((S, D), BF16)] * 2 + [jax.ShapeDtypeStruct((S, 2 * D), BF16)]
        + [jax.ShapeDtypeStruct((S, W), BF16)] * 2,
        compiler_params=_params("arbitrary"),
    )(dx1, w_out, a, b, z, z, z, z, w_ba, w_bs, *hook.ins)


def attn_bwd(z, y, dy, lse, bias, hook=Hook()):
    S = z.shape[0]
    nt = S // ATT_TILE
    back = ATT_WIN - 1
    n_in, nh, nho = 5 + 2 * ATT_WIN, len(hook.ins), len(hook.out_shapes)

    def body(q_ref, *refs):
        k_refs, v_refs = refs[:ATT_WIN], refs[ATT_WIN:2 * ATT_WIN]
        y_ref, dy_ref, lse_ref, bias_ref = refs[2 * ATT_WIN:2 * ATT_WIN + 4]
        rest = refs[2 * ATT_WIN + 4:]
        h_in, (dqkv_ref, dbias_ref), h_out = rest[:nh], rest[nh:nh + 2], rest[nh + 2:nh + 2 + nho]
        dk_acc, dv_acc, dq_buf = rest[nh + 2 + nho:nh + 5 + nho]
        h_sems = rest[nh + 5 + nho:]
        t = pl.program_id(0)
        _run_hook(hook, "before", h_in, h_out, h_sems)

        @pl.when(t == 0)
        def _():
            dbias_ref[...] = jnp.zeros_like(dbias_ref)
            dk_acc[...] = jnp.zeros_like(dk_acc)
            dv_acc[...] = jnp.zeros_like(dv_acc)
            dq_buf[...] = jnp.zeros_like(dq_buf)

        @pl.when(t < nt)
        def _():
            valid = _att_valid(t)
            q_t = (q_ref[...] * ATT_SCALE).astype(F32).T.astype(BF16)
            do_t = dy_ref[...].astype(F32).T.astype(BF16)
            for h in range(ATT_HEADS):
                sl = slice(h * ATT_HEAD_DIM, (h + 1) * ATT_HEAD_DIM)
                k = jnp.concatenate([r[:, sl] for r in k_refs], axis=0)
                v = jnp.concatenate([r[:, sl] for r in v_refs], axis=0)
                q, do_b = q_ref[:, sl] * ATT_SCALE, dy_ref[:, sl]
                p = jnp.exp(_att_scores(q, k, bias_ref[h], valid) - lse_ref[:, h:h + 1])
                delta = jnp.sum(do_b.astype(F32) * y_ref[:, sl].astype(F32), axis=-1, keepdims=True)
                ds = p * (_nt(do_b, v) - delta)
                dbias_ref[h, :, 0:ATT_TILE] += ds[:, 0:ATT_TILE] + ds[:, ATT_TILE:2 * ATT_TILE] + ds[:, 2 * ATT_TILE:ATT_FAR]
                dbias_ref[h, :, ATT_TILE:] += ds[:, ATT_FAR:]
                ds_b = ds.astype(BF16)
                dq_buf[t % ATT_WIN, :, sl] = (_nn(ds_b, k) * ATT_SCALE).astype(BF16)
                dk_w = _nn(q_t[sl, :], ds_b)
                dv_w = _nn(do_t[sl, :], p.astype(BF16))
                for i in range(ATT_WIN):
                    slot = (t + 1 + i) % ATT_WIN
                    cols = slice(i * ATT_TILE, (i + 1) * ATT_TILE)
                    if i == back:
                        dk_acc[slot, sl, :] = dk_w[:, cols]
                        dv_acc[slot, sl, :] = dv_w[:, cols]
                    else:
                        dk_acc[slot, sl, :] += dk_w[:, cols]
                        dv_acc[slot, sl, :] += dv_w[:, cols]

        done = (t + 1) % ATT_WIN
        dqkv_ref[:, 0:512] = dq_buf[done]
        dqkv_ref[:, 512:1024] = dk_acc[done].T.astype(BF16)
        dqkv_ref[:, 1024:1536] = dv_acc[done].T.astype(BF16)
        _run_hook(hook, "after", h_in, h_out, h_sems)

    last = nt - 1
    dbias_shape = (ATT_HEADS, ATT_TILE, ATT_TILE + ATT_WIN * ATT_TILE - ATT_FAR)
    tile = lambda col, n=512: pl.BlockSpec((ATT_TILE, n), lambda t: (jnp.minimum(t, last), col))
    late = pl.BlockSpec((ATT_TILE, 1536), lambda t: (jnp.maximum(t - back, 0), 0))
    return _hooked_call(
        body, hook, n_in=n_in, n_out=2, grid=(nt + back,), name="attn_bwd",
        in_specs=[tile(0)] + _att_window_specs(nt, 1) + _att_window_specs(nt, 2)
        + [tile(0), tile(0), tile(0, ATT_HEADS), _full(bias.shape)],
        out_specs=[late, _acc(dbias_shape)],
        out_shape=[jax.ShapeDtypeStruct((S, 1536), BF16), jax.ShapeDtypeStruct(dbias_shape, F32)],
        scratch_shapes=[pltpu.VMEM((ATT_WIN, 512, ATT_TILE), F32)] * 2 + [pltpu.VMEM((ATT_WIN, ATT_TILE, 512), BF16)],
        compiler_params=_params("arbitrary"),
    )(z, *([z] * (2 * ATT_WIN)), y, dy, lse, bias, *hook.ins)


def sgu_bwd(z, dy, amean, lng, lnb, wpair, wtpair, bfull, maskpair, *, tm, hook=Hook()):
    S = z.shape[0]
    n = S // tm
    nh, nho = len(hook.ins), len(hook.out_shapes)

    def body(zu_ref, zv_ref, dy_ref, a_ref, lng_ref, lnb_ref, wpair_ref, wtpair_ref, bfull_ref, mask_ref, *rest):
        h_in, (duv_ref, dw_ref, dsgb_ref, dlng_ref, dlnb_ref), h_out = rest[:nh], rest[nh:nh + 5], rest[nh + 5:nh + 5 + nho]
        (xhat_scr, rstd_scr, vn_scr, dxh_scr, vbd, dbd, w_acc, b_acc, g_acc, s_acc) = rest[nh + 5 + nho:nh + 15 + nho]
        h_sems = rest[nh + 15 + nho:]
        i = pl.program_id(0)
        _run_hook(hook, "before", h_in, h_out, h_sems)

        @pl.when(i == 0)
        def _():
            for r in (w_acc, b_acc, g_acc, s_acc):
                r[...] = jnp.zeros_like(r)

        xhat_scr[...], rstd_scr[...], vn_scr[...] = _sgu_norm(zv_ref, a_ref, lng_ref, lnb_ref)

        def block(b, carry):
            rows = _sgu_rows(b)
            zu = zu_ref[rows, :].astype(F32)
            u, du = _gelu(zu), _gelu_grad(zu)
            dyv = dy_ref[rows, :].astype(F32)
            for p in range(SG_PAIRS):
                lanes = slice(p * 128, (p + 1) * 128)
                _pair_diag(vn_scr[rows, lanes], vbd.at[p])
                sv = _nn(wpair_ref[p], vbd[p]) + bfull_ref[:, lanes]
                duv_ref[rows, lanes] = (dyv[:, lanes] * sv * du[:, lanes]).astype(BF16)
                dsv = dyv[:, lanes] * u[:, lanes]
                b_acc[:, lanes] += dsv
                w_acc[p] += _nt(dsv.astype(BF16), vbd[p])
                _pair_diag(dsv, dbd.at[p])
                dvn = _nn(wtpair_ref[p], dbd[p])
                g_acc[:, lanes] += dvn * xhat_scr[rows, lanes]
                s_acc[:, lanes] += dvn
                dxh_scr[rows, lanes] = dvn * lng_ref[:, lanes]
            return carry

        lax.fori_loop(0, tm // SG_BLOCK, block, 0)
        dxh, xhat = dxh_scr[...], xhat_scr[...]
        dv = rstd_scr[...] * (dxh - _group_mean(dxh, a_ref) - xhat * _group_mean(dxh * xhat, a_ref))
        duv_ref[:, 512:1024] = (dv * _gelu_grad(zv_ref[...].astype(F32))).astype(BF16)

        @pl.when(i == n - 1)
        def _():
            dw_ref[...] = w_acc[...] * mask_ref[...]
            dsgb_ref[...] = _group_mean(b_acc[...], a_ref) * float(SG_GROUP_DIM)
            dlng_ref[...] = jnp.sum(g_acc[...], axis=0, keepdims=True)
            dlnb_ref[...] = jnp.sum(s_acc[...], axis=0, keepdims=True)

        _run_hook(hook, "after", h_in, h_out, h_sems)

    tile = lambda col: pl.BlockSpec((tm, 512), lambda i: (i, col))
    smalls = [amean, lng, lnb, wpair, wtpair, bfull, maskpair]
    pair_shape = (SG_PAIRS, SG_BLOCK, 2 * SG_BLOCK)
    return _hooked_call(
        body, hook, n_in=3 + len(smalls), n_out=5, grid=(n,), name="sgu_bwd",
        in_specs=[tile(3), tile(4), tile(0)] + [_full(a.shape) for a in smalls],
        out_specs=[pl.BlockSpec((tm, 1024), lambda i: (i, 0)), _acc(pair_shape), _acc((SG_BLOCK, 512)),
                   _acc((1, 512)), _acc((1, 512))],
        out_shape=[jax.ShapeDtypeStruct((S, 1024), BF16), jax.ShapeDtypeStruct(pair_shape, F32),
                   jax.ShapeDtypeStruct((SG_BLOCK, 512), F32), jax.ShapeDtypeStruct((1, 512), F32),
                   jax.ShapeDtypeStruct((1, 512), F32)],
        scratch_shapes=[pltpu.VMEM((tm, 512), F32)] * 4
        + [pltpu.VMEM((SG_PAIRS, 2 * SG_BLOCK, 128), BF16)] * 2
        + [pltpu.VMEM(pair_shape, F32)] + [pltpu.VMEM((SG_BLOCK, 512), F32)] * 3,
        compiler_params=_params("arbitrary"),
    )(z, z, dy, *smalls, *hook.ins)


def bias_colsum(p, far_part):
    H, _, L = p.shape
    near_lo = (ATT_WIN - 1) * ATT_TILE - REL_CLIP + 1
    near_hi = ATT_WIN * ATT_TILE

    def body(p_ref, f_ref, gw_ref, far_ref):
        k = lax.broadcasted_iota(jnp.int32, (1, L), 1)
        is_far = (k < near_lo) | (k >= near_hi)
        for h in range(H):
            g = jnp.sum(p_ref[h], axis=0, keepdims=True)
            gw_ref[h:h + 1, :] = g
            far_ref[h:h + 1, :] = jnp.zeros((1, 128), F32) + (jnp.sum(jnp.where(is_far, g, 0.0)) + jnp.sum(f_ref[h]))

    return pl.pallas_call(
        body, name="bias_colsum", in_specs=[_full(p.shape), _full(far_part.shape)],
        out_specs=[_acc((H, L)), _acc((H, 128))],
        out_shape=[jax.ShapeDtypeStruct((H, L), F32), jax.ShapeDtypeStruct((H, 128), F32)], grid=(1,),
        compiler_params=_params("arbitrary"),
    )(p, far_part)


def _row_tile(rows, cols):
    best = None
    for tr in range(16, rows + 1, 16):
        if rows % tr == 0 and tr * cols * 4 <= EW_BLOCK_BYTES:
            best = tr
    return best if best is not None else rows


def place_shard(w, own, name):
    R, C = w.shape
    r = R // 2
    tr = _row_tile(r, C)
    nr = r // tr

    def body(own_ref, w_ref, o_ref):
        o_ref[...] = w_ref[...].astype(BF16)

    return pl.pallas_call(
        body, name=name,
        grid_spec=pltpu.PrefetchScalarGridSpec(
            num_scalar_prefetch=1, grid=(2, nr),
            in_specs=[pl.BlockSpec((tr, C), lambda h, i, own: (h * nr + i, 0))],
            out_specs=pl.BlockSpec((None, None, tr, C), lambda h, i, own: (own[0], h, i, 0))),
        out_shape=jax.ShapeDtypeStruct((4, 2, r, C), BF16), compiler_params=_params("parallel", "parallel"),
    )(own, w)


def chip_partial(grad, recv, core, name):
    _, _, r, C = grad.shape
    tr = _row_tile(r, C)

    def body(core_ref, g_ref, r_ref, o_ref):
        o_ref[...] = (g_ref[...].astype(F32) + r_ref[...].astype(F32)).astype(BF16)

    spec = pl.BlockSpec((None, tr, C), lambda j, i, core: (j, i, 0))
    return pl.pallas_call(
        body, name=name,
        grid_spec=pltpu.PrefetchScalarGridSpec(
            num_scalar_prefetch=1, grid=(4, r // tr),
            in_specs=[pl.BlockSpec((None, None, tr, C), lambda j, i, core: (j, core[0], i, 0)), spec], out_specs=spec),
        out_shape=jax.ShapeDtypeStruct((4, r, C), BF16), compiler_params=_params("parallel", "parallel"),
    )(core, grad, recv)


def shard_sum(part, recv, own, core, name):
    _, r, C = part.shape
    tr = _row_tile(r, C)

    def body(own_ref, core_ref, p_ref, r0, r1, r2, o_ref):
        o_ref[...] = p_ref[...].astype(F32) + r0[...].astype(F32) + r1[...].astype(F32) + r2[...].astype(F32)

    return pl.pallas_call(
        body, name=name,
        grid_spec=pltpu.PrefetchScalarGridSpec(
            num_scalar_prefetch=2, grid=(r // tr,),
            in_specs=[pl.BlockSpec((None, tr, C), lambda i, own, core: (own[0], i, 0))]
            + [pl.BlockSpec((None, tr, C), lambda i, own, core, k=k: (k, i, 0)) for k in range(3)],
            out_specs=pl.BlockSpec((None, tr, C), lambda i, own, core: (core[0], i, 0))),
        out_shape=jax.ShapeDtypeStruct((2, r, C), F32), compiler_params=_params("parallel"),
    )(own, core, part, recv, recv, recv)


def adamw(w, g, m, v, name):
    R, C = w.shape
    tr = _row_tile(R, C)

    def body(w_ref, g_ref, m_ref, v_ref, d_ref, nm_ref, nv_ref):
        gv = g_ref[...]
        nm = ADAM_B1 * m_ref[...] + (1.0 - ADAM_B1) * gv
        nv = ADAM_B2 * v_ref[...] + (1.0 - ADAM_B2) * (gv * gv)
        m_hat = nm / (1.0 - ADAM_B1 ** ADAM_STEP)
        v_hat = nv / (1.0 - ADAM_B2 ** ADAM_STEP)
        d_ref[...] = -ADAM_LR * (m_hat / (jnp.sqrt(v_hat) + ADAM_EPS) + ADAM_WD * w_ref[...])
        nm_ref[...] = nm
        nv_ref[...] = nv

    spec = pl.BlockSpec((tr, C), lambda i: (i, 0))
    return pl.pallas_call(
        body, grid=(R // tr,), name=name, in_specs=[spec] * 4, out_specs=[spec] * 3,
        out_shape=[jax.ShapeDtypeStruct((R, C), F32)] * 3, compiler_params=_params("parallel"),
    )(w, g, m, v)


HBM = pl.BlockSpec(memory_space=pl.ANY)


def _place():
    x, y, c = lax.axis_index("x"), lax.axis_index("y"), lax.axis_index("c")
    chips = [(1 - x, y), (x, 1 - y), (1 - x, 1 - y)]
    return x, y, c, chips


def _gather_phases(n):
    def copies(buf, sems, kind):
        send_sems, recv_sems = sems
        x, y, c, chips = _place()

        def remote(w, k, slot, to):
            return pltpu.make_async_remote_copy(src_ref=slot, dst_ref=slot, send_sem=send_sems.at[w, k],
                                                recv_sem=recv_sems.at[w, k], device_id=to, device_id_type=MESH)

        def one(w, k, px, py):
            if kind == "mine":
                return remote(w, k, buf[w].at[2 * x + y, c], (px, py, c))
            if kind == "theirs":
                return remote(w, k, buf[w].at[2 * px + py, c], (px, py, c))
            if kind == "onward":
                return remote(w, 3 + k, buf[w].at[2 * px + py, c], (x, y, 1 - c))
            return remote(w, 3 + k, buf[w].at[2 * px + py, 1 - c], (x, y, 1 - c))

        return [one(w, k, px, py) for w in range(n) for k, (px, py) in enumerate(chips)]

    def start(_, buf, sems):
        for cp in copies(buf, sems, "mine"):
            cp.start()

    def relay(_, buf, sems):
        for theirs, onward in zip(copies(buf, sems, "theirs"), copies(buf, sems, "onward")):
            theirs.wait_recv()
            onward.start()

    def finish(_, buf, sems):
        for cp in copies(buf, sems, "relayed"):
            cp.wait_recv()
        for cp in copies(buf, sems, "mine") + copies(buf, sems, "onward"):
            cp.wait_send()

    return start, relay, finish


def _gather_sems(n):
    return (pltpu.SemaphoreType.DMA((n, 6)), pltpu.SemaphoreType.DMA((n, 6)))


def gather_weights(bufs):
    n = len(bufs)
    phases = _gather_phases(n)

    def body(*refs):
        for phase in phases:
            phase(None, refs[n:2 * n], refs[2 * n:])

    return pl.pallas_call(
        body, name="gather_weights", in_specs=[HBM] * n, out_specs=[HBM] * n,
        out_shape=[jax.ShapeDtypeStruct(b.shape, b.dtype) for b in bufs],
        input_output_aliases={i: i for i in range(n)}, scratch_shapes=list(_gather_sems(n)),
    )(*bufs)


def gather_hook(bufs, n_steps):
    start, relay, finish = _gather_phases(len(bufs))
    return Hook(ins=tuple(bufs), out_shapes=tuple(jax.ShapeDtypeStruct(b.shape, b.dtype) for b in bufs),
                alias=tuple((i, i) for i in range(len(bufs))),
                sems=_gather_sems(len(bufs)),
                steps=((0, "before", start), ((n_steps * 13) // 16, "before", relay), (n_steps - 1, "after", finish)))


def sibling_split(grads, name):
    n = len(grads)

    def body(*refs):
        src, dst = refs[:n], refs[n:2 * n]
        send_sems, recv_sems = refs[2 * n:]
        x, y, c, _ = _place()
        sends = [pltpu.make_async_remote_copy(src_ref=src[w].at[:, 1 - c], dst_ref=dst[w], send_sem=send_sems.at[w],
                                              recv_sem=recv_sems.at[w], device_id=(x, y, 1 - c), device_id_type=MESH)
                 for w in range(n)]
        for cp in sends:
            cp.start()
        for cp in sends:
            cp.wait()

    return pl.pallas_call(
        body, name=name, in_specs=[HBM] * n, out_specs=[HBM] * n,
        out_shape=[jax.ShapeDtypeStruct((4,) + g.shape[2:], g.dtype) for g in grads],
        scratch_shapes=[pltpu.SemaphoreType.DMA((n,))] * 2,
    )(*grads)


def split_hook(grads, n_steps):
    n = len(grads)

    def copies(src, dst, sems):
        send_sems, recv_sems = sems
        x, y, c, _ = _place()
        return [pltpu.make_async_remote_copy(src_ref=src[w].at[:, 1 - c], dst_ref=dst[w], send_sem=send_sems.at[w],
                                             recv_sem=recv_sems.at[w], device_id=(x, y, 1 - c), device_id_type=MESH)
                for w in range(n)]

    def start(src, dst, sems):
        for cp in copies(src, dst, sems):
            cp.start()

    def finish(src, dst, sems):
        for cp in copies(src, dst, sems):
            cp.wait()

    return Hook(ins=tuple(grads), out_shapes=tuple(jax.ShapeDtypeStruct((4,) + g.shape[2:], g.dtype) for g in grads),
                sems=(pltpu.SemaphoreType.DMA((n,)), pltpu.SemaphoreType.DMA((n,))),
                steps=((0, "before", start), (n_steps - 1, "after", finish)))


def exchange_hook(parts, n_steps):
    n = len(parts)

    def copies(src, dst, sems):
        send_sems, recv_sems = sems
        _, _, c, chips = _place()
        return [pltpu.make_async_remote_copy(
            src_ref=src[w].at[2 * px + py], dst_ref=dst[w].at[k], send_sem=send_sems.at[w, k],
            recv_sem=recv_sems.at[w, k], device_id=(px, py, c), device_id_type=MESH)
            for w in range(n) for k, (px, py) in enumerate(chips)]

    def start(src, dst, sems):
        for cp in copies(src, dst, sems):
            cp.start()

    def finish(src, dst, sems):
        for cp in copies(src, dst, sems):
            cp.wait()

    return Hook(ins=tuple(parts), out_shapes=tuple(jax.ShapeDtypeStruct((3,) + p.shape[1:], p.dtype) for p in parts),
                sems=(pltpu.SemaphoreType.DMA((n, 3)), pltpu.SemaphoreType.DMA((n, 3))),
                steps=((0, "before", start), (n_steps - 1, "after", finish)))


def sibling_join(sums):
    n = len(sums)

    def body(*refs):
        buf = refs[n:2 * n]
        send_sems, recv_sems = refs[2 * n:]
        x, y, c, _ = _place()
        sends = [pltpu.make_async_remote_copy(src_ref=buf[w].at[c], dst_ref=buf[w].at[c], send_sem=send_sems.at[w],
                                              recv_sem=recv_sems.at[w], device_id=(x, y, 1 - c), device_id_type=MESH)
                 for w in range(n)]
        for cp in sends:
            cp.start()
        for w, cp in enumerate(sends):
            cp.wait_send()
            pltpu.make_async_remote_copy(src_ref=buf[w].at[c], dst_ref=buf[w].at[1 - c], send_sem=send_sems.at[w],
                                         recv_sem=recv_sems.at[w], device_id=(x, y, 1 - c), device_id_type=MESH).wait_recv()

    return pl.pallas_call(
        body, name="sibling_join", in_specs=[HBM] * n, out_specs=[HBM] * n,
        out_shape=[jax.ShapeDtypeStruct(s.shape, s.dtype) for s in sums],
        input_output_aliases={i: i for i in range(n)},
        scratch_shapes=[pltpu.SemaphoreType.DMA((n,))] * 2,
    )(*sums)


def join_hook(sums, n_steps):
    n = len(sums)

    def copies(buf, sems, recv):
        send_sems, recv_sems = sems
        x, y, c, _ = _place()
        half = (1 - c) if recv else c
        return [pltpu.make_async_remote_copy(src_ref=buf[w].at[half], dst_ref=buf[w].at[half], send_sem=send_sems.at[w],
                                             recv_sem=recv_sems.at[w], device_id=(x, y, 1 - c), device_id_type=MESH)
                for w in range(n)]

    def start(_, buf, sems):
        for cp in copies(buf, sems, False):
            cp.start()

    def finish(_, buf, sems):
        for cp in copies(buf, sems, False):
            cp.wait_send()
        for cp in copies(buf, sems, True):
            cp.wait_recv()

    return Hook(ins=tuple(sums), out_shapes=tuple(jax.ShapeDtypeStruct(s.shape, s.dtype) for s in sums),
                alias=tuple((i, i) for i in range(n)),
                sems=(pltpu.SemaphoreType.DMA((n,)), pltpu.SemaphoreType.DMA((n,))),
                steps=((0, "before", start), (n_steps - 1, "after", finish)))


def allreduce_small(v):
    R, C = v.shape

    def body(v_ref, out_ref, all_ref, send_sems, recv_sems):
        x, y, c, chips = _place()
        me, sibling = (x, y, c), (x, y, 1 - c)

        def slot(px, py, pc):
            return all_ref.at[4 * px + 2 * py + pc]

        def copy(k, block, to, src=None):
            return pltpu.make_async_remote_copy(src_ref=slot(*block) if src is None else src, dst_ref=slot(*block),
                                                send_sem=send_sems.at[k], recv_sem=recv_sems.at[k], device_id=to,
                                                device_id_type=MESH)

        first = [copy(0, me, sibling, src=v_ref)] + [copy(1 + j, me, (*chip, c), src=v_ref) for j, chip in enumerate(chips)]
        for cp in first:
            cp.start()
        slot(*me)[...] = v_ref[...]
        passed = [copy(4 + j, (*chip, c), sibling) for j, chip in enumerate(chips)]
        for j, chip in enumerate(chips):
            copy(1 + j, (*chip, c), me).wait_recv()
            passed[j].start()
        copy(0, sibling, me).wait_recv()
        for j, chip in enumerate(chips):
            copy(4 + j, (*chip, 1 - c), me).wait_recv()
        for cp in first + passed:
            cp.wait_send()
        acc = all_ref[0]
        for k in range(1, 8):
            acc = acc + all_ref[k]
        out_ref[...] = acc

    vmem = pl.BlockSpec(memory_space=pltpu.VMEM)
    return pl.pallas_call(
        body, name="allreduce_small", in_specs=[vmem], out_specs=vmem, out_shape=jax.ShapeDtypeStruct((R, C), F32),
        scratch_shapes=[pltpu.VMEM((8, R, C), F32), pltpu.SemaphoreType.DMA((7,)), pltpu.SemaphoreType.DMA((7,))],
        compiler_params=pltpu.CompilerParams(vmem_limit_bytes=VMEM_LIMIT_V7X),
    )(v)


def allgather_hook(v, n_steps):
    R, C = v.shape

    def copies(v_ref, out, sems, kind):
        send_sems, recv_sems, _ = sems
        x, y, c, chips = _place()
        me, sibling = (x, y, c), (x, y, 1 - c)

        def slot(px, py, pc):
            return out.at[4 * px + 2 * py + pc]

        def copy(k, block, to, src=None):
            return pltpu.make_async_remote_copy(src_ref=slot(*block) if src is None else src, dst_ref=slot(*block),
                                                send_sem=send_sems.at[k], recv_sem=recv_sems.at[k], device_id=to,
                                                device_id_type=MESH)

        if kind == "mine":
            return [copy(0, me, sibling, src=v_ref)] + [copy(1 + j, me, (*chip, c), src=v_ref) for j, chip in enumerate(chips)]
        if kind == "theirs":
            return [copy(1 + j, (*chip, c), me) for j, chip in enumerate(chips)]
        if kind == "onward":
            return [copy(4 + j, (*chip, c), sibling) for j, chip in enumerate(chips)]
        return [copy(0, sibling, me)] + [copy(4 + j, (*chip, 1 - c), me) for j, chip in enumerate(chips)]

    def own(v_ref, out, sems):
        x, y, c, _ = _place()
        return pltpu.make_async_copy(v_ref, out.at[4 * x + 2 * y + c], sems[2])

    def start(ins, outs, sems):
        own(ins[0], outs[0], sems).start()
        for cp in copies(ins[0], outs[0], sems, "mine"):
            cp.start()

    def relay(ins, outs, sems):
        for theirs, onward in zip(copies(ins[0], outs[0], sems, "theirs"), copies(ins[0], outs[0], sems, "onward")):
            theirs.wait_recv()
            onward.start()

    def finish(ins, outs, sems):
        for cp in copies(ins[0], outs[0], sems, "relayed"):
            cp.wait_recv()
        for cp in copies(ins[0], outs[0], sems, "mine") + copies(ins[0], outs[0], sems, "onward"):
            cp.wait_send()
        own(ins[0], outs[0], sems).wait()

    return Hook(ins=(v,), out_shapes=(jax.ShapeDtypeStruct((8, R, C), F32),),
                sems=(pltpu.SemaphoreType.DMA((7,)), pltpu.SemaphoreType.DMA((7,)), pltpu.SemaphoreType.DMA(())),
                steps=((0, "before", start), ((n_steps * 13) // 16, "before", relay), (n_steps - 1, "after", finish)))


def merge_hooks(a, b):
    cut = (len(a.ins), len(a.out_shapes), len(a.sems))

    def part(fn, second):
        def run(ins, outs, sems):
            if second:
                fn(ins[cut[0]:], outs[cut[1]:], sems[cut[2]:])
            else:
                fn(ins[:cut[0]], outs[:cut[1]], sems[:cut[2]])
        return run

    steps = tuple((s, w, part(fn, False)) for s, w, fn in a.steps) + tuple((s, w, part(fn, True)) for s, w, fn in b.steps)
    alias = a.alias + tuple((i + cut[0], o + cut[1]) for i, o in b.alias)
    return Hook(a.ins + b.ins, a.out_shapes + b.out_shapes, alias, a.sems + b.sems, steps)


def sum_blocks(g):
    n, R, C = g.shape

    def body(g_ref, o_ref):
        acc = g_ref[0]
        for k in range(1, n):
            acc = acc + g_ref[k]
        o_ref[...] = acc

    return pl.pallas_call(body, name="sum_blocks", grid=(1,), in_specs=[_full(g.shape)], out_specs=_acc((R, C)),
                          out_shape=jax.ShapeDtypeStruct((R, C), F32), compiler_params=_params("arbitrary"))(g)


SMALL_PAD = 1024


def _pack_small(arrs):
    parts = []
    for a in arrs:
        f = a.reshape(-1).astype(F32)
        parts.append(jnp.pad(f, (0, (-f.shape[0]) % SMALL_PAD)))
    return jnp.concatenate(parts).reshape(-1, 128)


def _unpack_small(packed, shapes):
    flat = packed.reshape(-1)
    outs, off = [], 0
    for s in shapes:
        size = int(np.prod(s))
        outs.append(flat[off:off + size].reshape(s))
        off += size + (-size) % SMALL_PAD
    return outs


ATT_KEYS = ATT_WIN * ATT_TILE
ATT_NEAR_LO = (ATT_WIN - 1) * ATT_TILE - REL_CLIP + 1
ATT_PERIOD = ATT_KEYS + ATT_TILE + 1


def _att_bias_table(rel_bias):
    far = rel_bias[:, 2 * REL_CLIP:]
    near = rel_bias[:, 2 * REL_CLIP - 1:0:-1]
    w = jnp.concatenate([jnp.broadcast_to(far, (ATT_HEADS, ATT_NEAR_LO)), near,
                         jnp.broadcast_to(far, (ATT_HEADS, ATT_PERIOD - ATT_KEYS))], axis=1)
    rows = jnp.tile(w, (1, ATT_TILE))[:, :ATT_TILE * (ATT_PERIOD - 1)].reshape(ATT_HEADS, ATT_TILE, ATT_PERIOD - 1)
    i = np.arange(ATT_TILE)[:, None]
    m = np.arange(ATT_KEYS)[None, :]
    qc, kc = i // CHUNK, m // CHUNK
    band = (kc >= qc) & (kc <= qc + N_PREV_CHUNKS)
    return jnp.where(band[None], rows[:, :, :ATT_KEYS], NEG_INF)


def _rel_bias_grad(dbias):
    p = jnp.pad(dbias[:, :, ATT_TILE:], ((0, 0), (0, 0), (ATT_FAR, ATT_PERIOD - 1 - ATT_KEYS))).reshape(ATT_HEADS, -1)
    p = jnp.pad(p, ((0, 0), (0, ATT_TILE))).reshape(ATT_HEADS, ATT_TILE, ATT_PERIOD)
    gw, far = bias_colsum(p, dbias[:, :, :ATT_TILE])
    return jnp.concatenate([jnp.zeros((ATT_HEADS, 1), F32), gw[:, ATT_KEYS - 1:ATT_NEAR_LO - 1:-1], far[:, :1]], axis=1)


def kernel(x, mem, norm_mix_g, w_in, rel_bias, sg_ln_g, sg_ln_b, sg_w, sg_b, w_branch_att, w_branch_sg, w_out, norm_xattn_g, norm_mem_g, w_xq, w_xkv, w_xo, norm_ffn_g, w_ffn_in, w_ffn_out, norm_final_g, loss_target, m_norm_mix_g, m_w_in, m_rel_bias, m_sg_ln_g, m_sg_ln_b, m_sg_w, m_sg_b, m_w_branch_att, m_w_branch_sg, m_w_out, m_norm_xattn_g, m_norm_mem_g, m_w_xq, m_w_xkv, m_w_xo, m_norm_ffn_g, m_w_ffn_in, m_w_ffn_out, m_norm_final_g, v_norm_mix_g, v_w_in, v_rel_bias, v_sg_ln_g, v_sg_ln_b, v_sg_w, v_sg_b, v_w_branch_att, v_w_branch_sg, v_w_out, v_norm_xattn_g, v_norm_mem_g, v_w_xq, v_w_xkv, v_w_xo, v_norm_ffn_g, v_w_ffn_in, v_w_ffn_out, v_norm_final_g):
    S, D = x.shape[1], x.shape[2]
    x2d, mem2d, tgt = x[0], mem[0], loss_target[0]

    big_names = ["w_in", "w_branch_att", "w_branch_sg", "w_out", "w_xq", "w_xkv", "w_xo", "w_ffn_in", "w_ffn_out"]
    col_sharded = [True, True, True, False, False, True, False, True, False]
    big_w = [a[0] for a in (w_in, w_branch_att, w_branch_sg, w_out, w_xq, w_xkv, w_xo, w_ffn_in, w_ffn_out)]
    big_m = [a[0] for a in (m_w_in, m_w_branch_att, m_w_branch_sg, m_w_out, m_w_xq, m_w_xkv, m_w_xo, m_w_ffn_in, m_w_ffn_out)]
    big_v = [a[0] for a in (v_w_in, v_w_branch_att, v_w_branch_sg, v_w_out, v_w_xq, v_w_xkv, v_w_xo, v_w_ffn_in, v_w_ffn_out)]

    own = (2 * lax.axis_index("x") + lax.axis_index("y")).astype(jnp.int32).reshape(1)
    core = lax.axis_index("c").astype(jnp.int32).reshape(1)
    placed = [place_shard(w, own, "place_" + nm) for w, nm in zip(big_w, big_names)]

    def whole(g4, i):
        R, C = big_w[i].shape
        return g4.reshape(4, R, C) if col_sharded[i] else g4.reshape(4 * R, C)

    W_in = whole(gather_weights(placed[:1])[0], 0)

    g_mix, g_xat, g_mem, g_ffn = norm_mix_g, norm_xattn_g, norm_mem_g, norm_ffn_g
    g_fin = norm_final_g.reshape(1, D)
    bias = _att_bias_table(rel_bias[0])
    tt = np.arange(SG_BLOCK)
    sg_mask = (tt[None, :] // CHUNK) <= (tt[:, None] // CHUNK)
    wm_f = jnp.where(sg_mask[None], sg_w[0], 0.0)
    pairs = wm_f.reshape(SG_PAIRS, 2, SG_BLOCK, SG_BLOCK)
    wpair = jnp.transpose(pairs, (0, 2, 1, 3)).reshape(SG_PAIRS, SG_BLOCK, 2 * SG_BLOCK).astype(BF16)
    wtpair = jnp.transpose(pairs, (0, 3, 1, 2)).reshape(SG_PAIRS, SG_BLOCK, 2 * SG_BLOCK).astype(BF16)
    maskpair = jnp.asarray(np.tile(sg_mask, (SG_PAIRS, 1, 2)), F32)
    bfull = jnp.repeat(sg_b[0].T, SG_GROUP_DIM, axis=1)
    lng, lnb = sg_ln_g[0].reshape(1, 512), sg_ln_b[0].reshape(1, 512)
    gid = np.arange(512) // SG_GROUP_DIM
    amean = jnp.asarray((gid[:, None] == gid[None, :]) / SG_GROUP_DIM, BF16)

    h1, z, *early = norm_mm(x2d, g_mix, W_in, tm=ROW_TILE, scale=None, name="norm_mm_in",
                            hook=gather_hook(placed[1:7], S // ROW_TILE))
    y_att, lse_att, *late = attn_fwd(z, bias, gather_hook(placed[7:], S // ATT_TILE))
    W_ba, W_bs, W_out, W_xq, W_xkv, W_xo, W_fi, W_fo = [whole(g4, i + 1) for i, g4 in enumerate(early + late)]
    y_sg, = sgu_fwd(z, amean, lng, lnb, wpair, bfull, tm=SGU_TILE)
    a_br, b_br, merged, x1 = merge_fwd(y_att, y_sg, z, x2d, W_ba, W_bs, W_out, tm=ROW_TILE)
    h2, xq = norm_mm(x1, g_xat, W_xq.reshape(1, D, D), tm=XATT_TILE, scale=XATT_HEAD_DIM ** -0.5, name="norm_mm_xq")
    mn, kv = norm_mm(mem2d, g_mem, W_xkv, tm=mem2d.shape[0], scale=None, name="norm_mm_kv")
    o_x, lse_x, x2 = xattn_fwd(xq, kv, W_xo, x1, tm=XATT_TILE)
    h3, gu = norm_mm(x2, g_ffn, W_fi, tm=ROW_TILE, scale=None, name="norm_mm_ffn")
    act, loss_vec, dx3, dg_fin = ffn_out_loss(gu, W_fo, x2, g_fin, tgt, tm=ROW_TILE)

    dgu = ffn_act_bwd(dx3, gu, W_fo, tm=ROW_TILE, nchunk=2)
    gW_fo = tn_mm(act, dx3, shards=1, tn=D, tk=DW_TOKENS, name="dw_ffn_out")
    gW_fi = tn_mm(h3, dgu, shards=4, tn=2 * W_fi.shape[2], tk=DW_TOKENS, name="dw_ffn_in")
    dx2, dg_ffn = mm_nt_norm_bwd(dgu, W_fi, x2, g_ffn, dx3, tm=ROW_TILE, name="dx_ffn")
    dq_x, dkv = xattn_bwd(dx2, xq, o_x, lse_x, kv, W_xo, tm=XATT_TILE)
    tk_wide = min(2 * DW_TOKENS, S)
    gW_xo = tn_mm(o_x, dx2, shards=1, tn=D, tk=tk_wide, name="dw_xo")
    gW_xq = tn_mm(h2, dq_x, shards=1, tn=D, tk=tk_wide, name="dw_xq")
    dx1, dg_xat = mm_nt_norm_bwd(dq_x, W_xq.reshape(1, D, D), x1, g_xat, dx2, tm=XATT_TILE, name="dx_xq")
    gW_xkv = tn_mm(mn, dkv, shards=4, tn=2 * D, tk=mem2d.shape[0], name="dw_xkv")
    _, dg_mem = mm_nt_norm_bwd(dkv.astype(BF16), W_xkv, mem2d, g_mem, None, tm=mem2d.shape[0], name="dx_mem")
    def canon(g, i):
        R, C = big_w[i].shape
        return g.reshape(4, 2, R // 2, C)

    def partials(gs, recv, idx):
        return [chip_partial(g, r, core, "chip_partial_" + big_names[i]) for g, r, i in zip(gs, recv, idx)]

    early_g = [canon(g, i) for g, i in zip([gW_xq, gW_xkv, gW_xo, gW_fi, gW_fo], range(4, 9))]
    da, db, dgab, dy_att, dy_sg, *early_r = merge_bwd(dx1, W_out, a_br, b_br, z, W_ba, W_bs, tm=ROW_TILE,
                                                      hook=split_hook(early_g, S // ROW_TILE))
    gW_out = tn_mm(merged, dx1, shards=1, tn=D, tk=tk_wide, name="dw_out")
    gW_ba = tn_mm(y_att, da, shards=4, tn=D, tk=tk_wide, name="dw_branch_att")
    gW_bs = tn_mm(y_sg, db, shards=4, tn=D, tk=tk_wide, name="dw_branch_sg")
    mid_g = [canon(g, i) for g, i in zip([gW_ba, gW_bs, gW_out], range(1, 4))]
    parts_b = (partials(mid_g, sibling_split(mid_g, "sibling_split_mid"), range(1, 4))
               + partials(early_g, early_r, range(4, 9)))
    nt_bwd = S // ATT_TILE + ATT_WIN - 1
    dqkv, dbias, *from_chips_b = attn_bwd(z, y_att, dy_att, lse_att, bias, exchange_hook(parts_b, nt_bwd))
    sums_b = [shard_sum(p, r, own, core, "shard_sum_" + nm) for p, r, nm in zip(parts_b, from_chips_b, big_names[1:])]
    duv, dwpair, dsgb_full, dlng, dlnb, *joined_b = sgu_bwd(z, dy_sg, amean, lng, lnb, wpair, wtpair, bfull, maskpair,
                                                         tm=SGU_TILE, hook=join_hook(sums_b, S // SGU_TILE))
    dwm = jnp.transpose(dwpair.reshape(SG_PAIRS, SG_BLOCK, 2, SG_BLOCK), (0, 2, 1, 3))
    dsgb = dsgb_full[:, ::SG_GROUP_DIM].T
    dz = [dqkv, duv, dgab]
    gW_in = tn_mm_pieces(h1, dz, shards=4, per=2, tk=DW_TOKENS, name="dw_in")
    in_g = [canon(gW_in, 0)]
    parts_a = partials(in_g, sibling_split(in_g, "sibling_split_w_in"), [0])
    d_rel = _rel_bias_grad(dbias)
    small_g = [d_rel, dlng, dlnb, dwm, dsgb, dg_xat, dg_mem, dg_ffn, dg_fin, loss_vec[0, :1]]
    n_dx = S // ROW_TILE
    dx, dg_mix, from_chips_a, small_all = mm_nt_norm_bwd(
        dz, W_in, x2d, g_mix, dx1, tm=ROW_TILE, name="dx_in",
        hook=merge_hooks(exchange_hook(parts_a, n_dx), allgather_hook(_pack_small(small_g), n_dx)))

    joined = list(sibling_join([shard_sum(parts_a[0], from_chips_a, own, core, "shard_sum_w_in")])) + list(joined_b)
    big_out = []
    for j, w, m_, v_, nm in zip(joined, big_w, big_m, big_v, big_names):
        g = j.reshape(w.shape)
        big_out.append((g,) + tuple(adamw(w, g, m_, v_, "adamw_" + nm)))

    small_w = [norm_mix_g, rel_bias, sg_ln_g, sg_ln_b, sg_w, sg_b, norm_xattn_g, norm_mem_g, norm_ffn_g, norm_final_g]
    small_m = [m_norm_mix_g, m_rel_bias, m_sg_ln_g, m_sg_ln_b, m_sg_w, m_sg_b, m_norm_xattn_g, m_norm_mem_g, m_norm_ffn_g, m_norm_final_g]
    small_v = [v_norm_mix_g, v_rel_bias, v_sg_ln_g, v_sg_ln_b, v_sg_w, v_sg_b, v_norm_xattn_g, v_norm_mem_g, v_norm_ffn_g, v_norm_final_g]
    shapes = [w.shape for w in small_w]
    g_sum = jnp.concatenate([allreduce_small(_pack_small([dg_mix])), sum_blocks(small_all)], axis=0)
    zero = jnp.zeros((1,), F32)
    d_s, m_s, v_s = adamw(_pack_small(small_w + [zero]), g_sum, _pack_small(small_m + [zero]), _pack_small(small_v + [zero]),
                          "adamw_small")
    sg_, sd, sm, sv_ = (_unpack_small(p, shapes + [(1,)]) for p in (g_sum, d_s, m_s, v_s))
    loss = sg_[-1][0]

    order = ["norm_mix_g", "w_in", "rel_bias", "sg_ln_g", "sg_ln_b", "sg_w", "sg_b", "w_branch_att", "w_branch_sg", "w_out",
             "norm_xattn_g", "norm_mem_g", "w_xq", "w_xkv", "w_xo", "norm_ffn_g", "w_ffn_in", "w_ffn_out", "norm_final_g"]
    small_names = ["norm_mix_g", "rel_bias", "sg_ln_g", "sg_ln_b", "sg_w", "sg_b", "norm_xattn_g", "norm_mem_g", "norm_ffn_g",
                   "norm_final_g"]
    res = {}
    for i, nm in enumerate(small_names):
        res[nm] = (sg_[i], sd[i], sm[i], sv_[i])
    for nm, outs in zip(big_names, big_out):
        res[nm] = tuple(o[None] for o in outs)
    return (loss, dx[None], *[res[nm][0] for nm in order], *[res[nm][1] for nm in order],
            *[res[nm][2] for nm in order], *[res[nm][3] for nm in order])
```

```python
import functools
from typing import NamedTuple

import numpy as np
import jax
import jax.numpy as jnp
from jax import lax
from jax.experimental import pallas as pl
from jax.experimental.pallas import tpu as pltpu

F32, BF16 = jnp.float32, jnp.bfloat16
MESH = pl.DeviceIdType.MESH

EPS = 1e-6
NEG_INF = -1e30
CHUNK = 64
N_PREV_CHUNKS = 8
ATT_HEADS, ATT_HEAD_DIM = 8, 64
REL_CLIP = 128
SG_BLOCK, SG_GROUPS, SG_GROUP_DIM = 128, 8, 64
XATT_HEADS, XATT_HEAD_DIM = 4, 256
ATT_TILE = 128
ATT_WIN = 5
ADAM_LR, ADAM_B1, ADAM_B2, ADAM_EPS, ADAM_WD, ADAM_STEP = 0.001, 0.9, 0.999, 1e-08, 0.01, 10

VMEM_LIMIT_V7X = 56 * 1024 * 1024
EW_BLOCK_BYTES = 2 << 20
ROW_TILE = 512
DW_TOKENS = 1024
XATT_TILE = 1024
SGU_TILE = 1024
HBM = pl.BlockSpec(memory_space=pl.ANY)


def _params(*sem):
    return pltpu.CompilerParams(dimension_semantics=sem, vmem_limit_bytes=VMEM_LIMIT_V7X)


def _full(shape):
    n = len(shape)
    return pl.BlockSpec(shape, lambda *_: (0,) * n, pipeline_mode=pl.Buffered(1))


def _acc(shape):
    n = len(shape)
    return pl.BlockSpec(shape, lambda *_: (0,) * n)


class Hook(NamedTuple):
    ins: tuple = ()
    out_shapes: tuple = ()
    alias: tuple = ()
    sems: tuple = ()
    steps: tuple = ()


def _run_hook(hook, pos, h_in, h_out, h_sems):
    for at, where, fn in hook.steps:
        if where == pos:
            pl.when(pl.program_id(0) == at)(functools.partial(fn, h_in, h_out, h_sems))


def _hooked_call(body, hook, *, n_in, n_out, in_specs, out_specs, out_shape, scratch_shapes=(), **kw):
    nh = len(hook.ins)
    aliases = {n_in + i: n_out + o for i, o in hook.alias}
    return pl.pallas_call(
        body, in_specs=list(in_specs) + [HBM] * nh, out_specs=list(out_specs) + [HBM] * len(hook.out_shapes),
        out_shape=list(out_shape) + list(hook.out_shapes), scratch_shapes=list(scratch_shapes) + list(hook.sems),
        input_output_aliases=aliases, **kw)


def _nt(a, b):
    return lax.dot_general(a, b, (((1,), (1,)), ((), ())), preferred_element_type=F32)


def _tn(a, b):
    return lax.dot_general(a, b, (((0,), (0,)), ((), ())), preferred_element_type=F32)


def _nn(a, b):
    return jnp.dot(a, b, preferred_element_type=F32)


def _sigmoid(x):
    return 0.5 * jnp.tanh(0.5 * x) + 0.5


_GELU_C = float(np.sqrt(2.0 / np.pi))


def _gelu(x):
    t = jnp.tanh(_GELU_C * (x + 0.044715 * (x * x * x)))
    return x * (0.5 * (1.0 + t))


def _gelu_grad(x):
    t = jnp.tanh(_GELU_C * (x + 0.044715 * (x * x * x)))
    return 0.5 * (1.0 + t) + 0.5 * x * (1.0 - t * t) * (_GELU_C * (1.0 + 3.0 * 0.044715 * x * x))


def _rms_stats(xf):
    rstd = lax.rsqrt(jnp.mean(xf * xf, axis=-1, keepdims=True) + EPS)
    return rstd, xf * rstd


def _rms_bwd(xhat, rstd, g, dh):
    dxh = dh * g
    dx = rstd * (dxh - xhat * jnp.mean(dxh * xhat, axis=-1, keepdims=True))
    return dx, dh * xhat


def norm_mm(x, g, w, *, tm, scale, name, hook=Hook()):
    S, D = x.shape
    J, _, C = w.shape
    nh, nho = len(hook.ins), len(hook.out_shapes)

    def body(x_ref, g_ref, w_ref, *rest):
        h_in, (h_ref, z_ref), h_out, h_sems = rest[:nh], rest[nh:nh + 2], rest[nh + 2:nh + 2 + nho], rest[nh + 2 + nho:]
        _run_hook(hook, "before", h_in, h_out, h_sems)
        _, xhat = _rms_stats(x_ref[...])
        h = (xhat * g_ref[...]).astype(BF16)
        h_ref[...] = h
        for j in range(J):
            acc = _nn(h, w_ref[j])
            if scale is not None:
                acc = acc * scale
            z_ref[:, j * C:(j + 1) * C] = acc.astype(BF16)
        _run_hook(hook, "after", h_in, h_out, h_sems)

    return _hooked_call(
        body, hook, n_in=3, n_out=2, grid=(S // tm,), name=name,
        in_specs=[pl.BlockSpec((tm, D), lambda i: (i, 0)), _full((1, D)), _full((J, D, C))],
        out_specs=[pl.BlockSpec((tm, D), lambda i: (i, 0)), pl.BlockSpec((tm, J * C), lambda i: (i, 0))],
        out_shape=[jax.ShapeDtypeStruct((S, D), BF16), jax.ShapeDtypeStruct((S, J * C), BF16)],
        compiler_params=_params("arbitrary"),
    )(x, g, w, *hook.ins)


def _att_window_specs(nt, col):
    return [pl.BlockSpec((ATT_TILE, 512), lambda t, j=j: (jnp.clip(t - (ATT_WIN - 1) + j, 0, nt - 1), col))
            for j in range(ATT_WIN)]


ATT_SCALE = ATT_HEAD_DIM ** -0.5
ATT_FAR = 3 * ATT_TILE


def _att_scores(q_scaled, k, bias, valid):
    return jnp.where(valid, _nt(q_scaled, k) + bias, NEG_INF)


def _att_valid(t):
    kpos = lax.broadcasted_iota(jnp.int32, (ATT_TILE, ATT_WIN * ATT_TILE), 1) + (t - (ATT_WIN - 1)) * ATT_TILE
    return kpos >= 0


def attn_fwd(z, bias, hook=Hook()):
    S = z.shape[0]
    nt = S // ATT_TILE
    n_in, nh, nho = 2 + 2 * ATT_WIN, len(hook.ins), len(hook.out_shapes)

    def body(q_ref, *refs):
        k_refs, v_refs = refs[:ATT_WIN], refs[ATT_WIN:2 * ATT_WIN]
        bias_ref = refs[2 * ATT_WIN]
        rest = refs[2 * ATT_WIN + 1:]
        h_in, (y_ref, lse_ref), h_out = rest[:nh], rest[nh:nh + 2], rest[nh + 2:nh + 2 + nho]
        s_scr, p_scr = rest[nh + 2 + nho:nh + 4 + nho]
        h_sems = rest[nh + 4 + nho:]
        _run_hook(hook, "before", h_in, h_out, h_sems)
        t = pl.program_id(0)
        heads = [slice(h * ATT_HEAD_DIM, (h + 1) * ATT_HEAD_DIM) for h in range(ATT_HEADS)]
        for h, sl in enumerate(heads):
            k = jnp.concatenate([r[:, sl] for r in k_refs], axis=0)
            s_scr[h] = _nt(q_ref[:, sl] * ATT_SCALE, k) + bias_ref[h]

        @pl.when(t < ATT_WIN - 1)
        def _():
            valid = _att_valid(t)
            for h in range(ATT_HEADS):
                s_scr[h] = jnp.where(valid, s_scr[h], NEG_INF)

        stats = []
        for h in range(ATT_HEADS):
            s = s_scr[h]
            m = jnp.max(s, axis=-1, keepdims=True)
            p = jnp.exp(s - m)
            stats.append((m, jnp.sum(p, axis=-1, keepdims=True)))
            p_scr[h] = p.astype(BF16)
        for h, sl in enumerate(heads):
            m, l = stats[h]
            v = jnp.concatenate([r[:, sl] for r in v_refs], axis=0)
            y_ref[:, sl] = (_nn(p_scr[h], v) / l).astype(BF16)
            lse_ref[:, h:h + 1] = m + jnp.log(l)
        _run_hook(hook, "after", h_in, h_out, h_sems)

    tile = lambda col: pl.BlockSpec((ATT_TILE, 512), lambda t: (t, col))
    return _hooked_call(
        body, hook, n_in=n_in, n_out=2, grid=(nt,), name="attn_fwd",
        in_specs=[tile(0)] + _att_window_specs(nt, 1) + _att_window_specs(nt, 2) + [_full(bias.shape)],
        out_specs=[tile(0), pl.BlockSpec((ATT_TILE, ATT_HEADS), lambda t: (t, 0))],
        out_shape=[jax.ShapeDtypeStruct((S, 512), BF16), jax.ShapeDtypeStruct((S, ATT_HEADS), F32)],
        scratch_shapes=[pltpu.VMEM((ATT_HEADS, ATT_TILE, ATT_WIN * ATT_TILE), F32),
                        pltpu.VMEM((ATT_HEADS, ATT_TILE, ATT_WIN * ATT_TILE), BF16)],
        compiler_params=_params("arbitrary"),
    )(z, *([z] * (2 * ATT_WIN)), bias, *hook.ins)


SG_PAIRS = SG_GROUPS // 2


def _group_mean(x, a_ref):
    rows = x.shape[0]
    hi = x.astype(BF16)
    lo = (x - hi.astype(F32)).astype(BF16)
    both = _nn(jnp.concatenate([hi, lo], axis=0), a_ref[...])
    return both[:rows] + both[rows:]


def _pair_diag(x, ref):
    first = lax.broadcasted_iota(jnp.int32, x.shape, 1) < SG_GROUP_DIM
    ref[0:SG_BLOCK, :] = jnp.where(first, x, 0.0).astype(BF16)
    ref[SG_BLOCK:2 * SG_BLOCK, :] = jnp.where(first, 0.0, x).astype(BF16)


def _sgu_norm(zv_ref, a_ref, lng_ref, lnb_ref):
    v = _gelu(zv_ref[...].astype(F32))
    vc = v - _group_mean(v, a_ref)
    rstd = lax.rsqrt(_group_mean(vc * vc, a_ref) + EPS)
    xhat = vc * rstd
    return xhat, rstd, xhat * lng_ref[...] + lnb_ref[...]


def _sgu_rows(b):
    return pl.ds(pl.multiple_of(b * SG_BLOCK, SG_BLOCK), SG_BLOCK)


def sgu_fwd(z, amean, lng, lnb, wpair, bfull, *, tm, hook=Hook()):
    S = z.shape[0]
    nh, nho = len(hook.ins), len(hook.out_shapes)

    def body(zu_ref, zv_ref, a_ref, lng_ref, lnb_ref, wpair_ref, bfull_ref, *rest):
        h_in, y_ref, h_out = rest[:nh], rest[nh], rest[nh + 1:nh + 1 + nho]
        vn_scr, vbd = rest[nh + 1 + nho:nh + 3 + nho]
        h_sems = rest[nh + 3 + nho:]
        _run_hook(hook, "before", h_in, h_out, h_sems)
        vn_scr[...] = _sgu_norm(zv_ref, a_ref, lng_ref, lnb_ref)[2]

        def block(b, carry):
            rows = _sgu_rows(b)
            u = _gelu(zu_ref[rows, :].astype(F32))
            for p in range(SG_PAIRS):
                lanes = slice(p * 128, (p + 1) * 128)
                _pair_diag(vn_scr[rows, lanes], vbd.at[p])
                sv = _nn(wpair_ref[p], vbd[p]) + bfull_ref[:, lanes]
                y_ref[rows, lanes] = (u[:, lanes] * sv).astype(BF16)
            return carry

        lax.fori_loop(0, tm // SG_BLOCK, block, 0)
        _run_hook(hook, "after", h_in, h_out, h_sems)

    tile = lambda col: pl.BlockSpec((tm, 512), lambda i: (i, col))
    smalls = [amean, lng, lnb, wpair, bfull]
    return _hooked_call(
        body, hook, n_in=2 + len(smalls), n_out=1, grid=(S // tm,), name="sgu_fwd",
        in_specs=[tile(3), tile(4)] + [_full(a.shape) for a in smalls],
        out_specs=[tile(0)], out_shape=[jax.ShapeDtypeStruct((S, 512), BF16)],
        scratch_shapes=[pltpu.VMEM((tm, 512), F32), pltpu.VMEM((SG_PAIRS, 2 * SG_BLOCK, 128), BF16)],
        compiler_params=_params("arbitrary"),
    )(z, z, *smalls, *hook.ins)


def _gate_specs(tm):
    return [pl.BlockSpec((tm, 512), lambda i, c=c: (i, c)) for c in (5, 6, 7, 8)]


def merge_fwd(y_att, y_sg, z, x, w_ba, w_bs, w_out, *, tm):
    S, D = x.shape
    J, _, C = w_ba.shape

    def body(ya_ref, ys_ref, g0, g1, g2, g3, x_ref, wba_ref, wbs_ref, wo_ref, a_ref, b_ref, m_ref, x1_ref):
        ya, ys = ya_ref[...], ys_ref[...]
        a = jnp.concatenate([_nn(ya, wba_ref[j]) for j in range(J)], axis=1)
        b = jnp.concatenate([_nn(ys, wbs_ref[j]) for j in range(J)], axis=1)
        sa = _sigmoid(jnp.concatenate([g0[...], g1[...]], axis=1).astype(F32))
        sb = _sigmoid(jnp.concatenate([g2[...], g3[...]], axis=1).astype(F32))
        a_ref[...] = a.astype(BF16)
        b_ref[...] = b.astype(BF16)
        merged = (sa * a + sb * b).astype(BF16)
        m_ref[...] = merged
        x1_ref[...] = x_ref[...] + _nn(merged, wo_ref[...])

    row = lambda n: pl.BlockSpec((tm, n), lambda i: (i, 0))
    return pl.pallas_call(
        body, grid=(S // tm,), name="merge_fwd",
        in_specs=[row(512), row(512)] + _gate_specs(tm) + [row(D), _full(w_ba.shape), _full(w_bs.shape), _full(w_out.shape)],
        out_specs=[row(D)] * 4,
        out_shape=[jax.ShapeDtypeStruct((S, D), BF16)] * 3 + [jax.ShapeDtypeStruct((S, D), F32)],
        compiler_params=_params("parallel"),
    )(y_att, y_sg, z, z, z, z, x, w_ba, w_bs, w_out)


def xattn_fwd(xq, kv, w_xo, x1, *, tm):
    S, D = xq.shape
    dh = XATT_HEAD_DIM

    def body(q_ref, kv_ref, wo_ref, x1_ref, o_ref, lse_ref, x2_ref):
        outs = []
        for h in range(XATT_HEADS):
            s = _nt(q_ref[:, h * dh:(h + 1) * dh], kv_ref[:, h * dh:(h + 1) * dh])
            m = jnp.max(s, axis=-1, keepdims=True)
            p = jnp.exp(s - m)
            l = jnp.sum(p, axis=-1, keepdims=True)
            outs.append((_nn(p.astype(BF16), kv_ref[:, D + h * dh:D + (h + 1) * dh]) / l).astype(BF16))
            lse_ref[:, h:h + 1] = m + jnp.log(l)
        o = jnp.concatenate(outs, axis=1)
        o_ref[...] = o
        x2_ref[...] = x1_ref[...] + _nn(o, wo_ref[...])

    row = lambda n: pl.BlockSpec((tm, n), lambda i: (i, 0))
    return pl.pallas_call(
        body, grid=(S // tm,), name="xattn_fwd",
        in_specs=[row(D), _full(kv.shape), _full(w_xo.shape), row(D)],
        out_specs=[row(D), row(XATT_HEADS), row(D)],
        out_shape=[jax.ShapeDtypeStruct((S, D), BF16), jax.ShapeDtypeStruct((S, XATT_HEADS), F32),
                   jax.ShapeDtypeStruct((S, D), F32)],
        compiler_params=_params("parallel"),
    )(xq, kv, w_xo, x1)


def ffn_out_loss(gu, w, x2, g, target, *, tm):
    S, D = x2.shape
    F = w.shape[0]

    def body(gu_ref, w_ref, x2_ref, g_ref, t_ref, act_ref, loss_ref, dx_ref, dg_ref):
        @pl.when(pl.program_id(0) == 0)
        def _():
            loss_ref[...] = jnp.zeros_like(loss_ref)
            dg_ref[...] = jnp.zeros_like(dg_ref)

        half = gu_ref[:, :F] * 0.5
        act = (half * gu_ref[:, F:]) * (jnp.tanh(half) + 1.0)
        act_ref[...] = act
        gv = g_ref[...]
        rstd, xhat = _rms_stats(x2_ref[...] + _nn(act, w_ref[...]))
        err = xhat * gv - t_ref[...]
        loss_ref[...] += 0.5 * jnp.sum(jnp.mean(err * err, axis=-1, keepdims=True))
        dx, dgc = _rms_bwd(xhat, rstd, gv, err * (1.0 / D))
        dx_ref[...] = dx
        dg_ref[...] += jnp.sum(dgc, axis=0, keepdims=True)

    row = lambda n: pl.BlockSpec((tm, n), lambda i: (i, 0))
    return pl.pallas_call(
        body, grid=(S // tm,), name="ffn_out_loss",
        in_specs=[row(2 * F), _full(w.shape), row(D), _full((1, D)), row(D)],
        out_specs=[row(F), _acc((1, 128)), row(D), _acc((1, D))],
        out_shape=[jax.ShapeDtypeStruct((S, F), BF16), jax.ShapeDtypeStruct((1, 128), F32),
                   jax.ShapeDtypeStruct((S, D), F32), jax.ShapeDtypeStruct((1, D), F32)],
        compiler_params=_params("arbitrary"),
    )(gu, w, x2, g, target)


def ffn_act_bwd(dx3, gu, w, *, tm, nchunk):
    S, D = dx3.shape
    F = w.shape[0]
    cn = F // nchunk

    def body(dx_ref, gu_ref, w_ref, dgu_ref):
        dxb = dx_ref[...].astype(BF16)
        for j in range(nchunk):
            dact = _nt(dxb, w_ref[j * cn:(j + 1) * cn, :])
            gate = gu_ref[:, j * cn:(j + 1) * cn]
            up = gu_ref[:, F + j * cn:F + (j + 1) * cn]
            sg = _sigmoid(gate)
            silu = gate * sg
            dsilu = sg + silu * (1.0 - sg)
            dgu_ref[:, j * cn:(j + 1) * cn] = (dact * (up * dsilu).astype(F32)).astype(BF16)
            dgu_ref[:, F + j * cn:F + (j + 1) * cn] = (dact * silu.astype(F32)).astype(BF16)

    row = lambda n: pl.BlockSpec((tm, n), lambda i: (i, 0))
    return pl.pallas_call(
        body, grid=(S // tm,), name="ffn_act_bwd",
        in_specs=[row(D), row(2 * F), _full(w.shape)], out_specs=row(2 * F),
        out_shape=jax.ShapeDtypeStruct((S, 2 * F), BF16), compiler_params=_params("parallel"),
    )(dx3, gu, w)


def _overlaps(widths, lo, hi):
    out, off = [], 0
    for p, wd in enumerate(widths):
        a, b = max(lo, off), min(hi, off + wd)
        if a < b:
            out.append((p, a - off, b - off, a - lo))
        off += wd
    return out


def tn_mm_pieces(a, pieces, *, shards, per, tk, name):
    K, M = a.shape
    widths = [p.shape[1] for p in pieces]
    C = sum(widths) // shards
    tn = per * C
    nk = K // tk
    n_tiles = shards // per

    def body(a_ref, *refs):
        p_refs, o_ref, acc_ref = refs[:len(pieces)], refs[len(pieces)], refs[len(pieces) + 1]
        n, k = pl.program_id(0), pl.program_id(1)

        @pl.when(k == 0)
        def _():
            acc_ref[...] = jnp.zeros_like(acc_ref)

        av = a_ref[...]
        for tile in range(n_tiles):
            @pl.when(n == tile)
            def _(tile=tile):
                for p, c0, c1, at in _overlaps(widths, tile * tn, (tile + 1) * tn):
                    acc_ref[:, at:at + c1 - c0] += _tn(av, p_refs[p][:, c0:c1])

        @pl.when(k == nk - 1)
        def _():
            for s in range(per):
                o_ref[s] = acc_ref[:, s * C:(s + 1) * C].astype(BF16)

    return pl.pallas_call(
        body, grid=(n_tiles, nk), name=name,
        in_specs=[pl.BlockSpec((tk, M), lambda n, k: (k, 0))] + [pl.BlockSpec((tk, wd), lambda n, k: (k, 0)) for wd in widths],
        out_specs=pl.BlockSpec((per, M, C), lambda n, k: (n, 0, 0)), out_shape=jax.ShapeDtypeStruct((shards, M, C), BF16),
        scratch_shapes=[pltpu.VMEM((M, tn), F32)], compiler_params=_params("parallel", "arbitrary"),
    )(a, *pieces)


def tn_mm(a, b, *, shards, tn, tk, name):
    K, M = a.shape
    N = b.shape[1]
    C = N // shards
    per = tn // C
    nk = K // tk

    def body(a_ref, b_ref, o_ref, acc_ref):
        k = pl.program_id(1)

        @pl.when(k == 0)
        def _():
            acc_ref[...] = jnp.zeros_like(acc_ref)

        acc_ref[...] += _tn(a_ref[...].astype(BF16), b_ref[...].astype(BF16))

        @pl.when(k == nk - 1)
        def _():
            if shards > 1:
                for s in range(per):
                    o_ref[s] = acc_ref[:, s * C:(s + 1) * C].astype(BF16)
            else:
                o_ref[...] = acc_ref[...].astype(BF16)

    if shards > 1:
        out_spec = pl.BlockSpec((per, M, C), lambda n, k: (n, 0, 0))
        out_shape = jax.ShapeDtypeStruct((shards, M, C), BF16)
    else:
        out_spec = pl.BlockSpec((M, tn), lambda n, k: (0, n))
        out_shape = jax.ShapeDtypeStruct((M, N), BF16)
    return pl.pallas_call(
        body, grid=(N // tn, nk), name=name,
        in_specs=[pl.BlockSpec((tk, M), lambda n, k: (k, 0)), pl.BlockSpec((tk, tn), lambda n, k: (k, n))],
        out_specs=out_spec, out_shape=out_shape, scratch_shapes=[pltpu.VMEM((M, tn), F32)],
        compiler_params=_params("parallel", "arbitrary"),
    )(a, b)


def mm_nt_norm_bwd(dy, w, x, g, dx_in, *, tm, name, hook=Hook()):
    S, D = x.shape
    J, _, C = w.shape
    has_in = dx_in is not None
    dys = list(dy) if isinstance(dy, (list, tuple)) else [dy]
    widths = [d.shape[1] for d in dys]
    nd = len(dys)
    n_in, nh, nho = nd + 3 + has_in, len(hook.ins), len(hook.out_shapes)

    def body(*refs):
        dy_refs = refs[:nd]
        w_ref, x_ref, g_ref = refs[nd:nd + 3]
        dxin_ref = refs[nd + 3] if has_in else None
        rest = refs[n_in:]
        h_in, (dx_ref, dg_ref), h_out, h_sems = rest[:nh], rest[nh:nh + 2], rest[nh + 2:nh + 2 + nho], rest[nh + 2 + nho:]
        _run_hook(hook, "before", h_in, h_out, h_sems)

        @pl.when(pl.program_id(0) == 0)
        def _():
            dg_ref[...] = jnp.zeros_like(dg_ref)

        dh = None
        for j in range(J):
            for p, c0, c1, at in _overlaps(widths, j * C, (j + 1) * C):
                part = _nt(dy_refs[p][:, c0:c1], w_ref[j, :, at:at + c1 - c0])
                dh = part if dh is None else dh + part
        rstd, xhat = _rms_stats(x_ref[...])
        dx, dgc = _rms_bwd(xhat, rstd, g_ref[...], dh)
        dx_ref[...] = dx + dxin_ref[...] if has_in else dx
        dg_ref[...] += jnp.sum(dgc, axis=0, keepdims=True)
        _run_hook(hook, "after", h_in, h_out, h_sems)

    row = lambda n: pl.BlockSpec((tm, n), lambda i: (i, 0))
    ins = dys + [w, x, g] + ([dx_in] if has_in else [])
    return _hooked_call(
        body, hook, n_in=n_in, n_out=2, grid=(S // tm,), name=name,
        in_specs=[row(wd) for wd in widths] + [_full(w.shape), row(D), _full((1, D))] + ([row(D)] if has_in else []),
        out_specs=[row(D), _acc((1, D))],
        out_shape=[jax.ShapeDtypeStruct((S, D), F32), jax.ShapeDtypeStruct((1, D), F32)],
        compiler_params=_params("arbitrary"),
    )(*ins, *hook.ins)


def xattn_bwd(dx2, xq, o, lse, kv, w_xo, *, tm):
    S, D = xq.shape
    M = kv.shape[0]
    dh = XATT_HEAD_DIM

    def body(dx_ref, q_ref, o_ref, lse_ref, kv_ref, wo_ref, dq_ref, dkv_ref):
        @pl.when(pl.program_id(0) == 0)
        def _():
            dkv_ref[...] = jnp.zeros_like(dkv_ref)

        do = _nt(dx_ref[...].astype(BF16), wo_ref[...])
        for h in range(XATT_HEADS):
            hs = slice(h * dh, (h + 1) * dh)
            vs = slice(D + h * dh, D + (h + 1) * dh)
            q, k, v = q_ref[:, hs], kv_ref[:, hs], kv_ref[:, vs]
            do_h = do[:, hs]
            do_b = do_h.astype(BF16)
            p = jnp.exp(_nt(q, k) - lse_ref[:, h:h + 1])
            delta = jnp.sum(do_h * o_ref[:, hs].astype(F32), axis=-1, keepdims=True)
            ds = (p * (_nt(do_b, v) - delta)).astype(BF16)
            dq_ref[:, hs] = (_nn(ds, k) * (dh ** -0.5)).astype(BF16)
            dkv_ref[:, hs] += _tn(ds, q)
            dkv_ref[:, vs] += _tn(p.astype(BF16), do_b)

    row = lambda n: pl.BlockSpec((tm, n), lambda i: (i, 0))
    return pl.pallas_call(
        body, grid=(S // tm,), name="xattn_bwd",
        in_specs=[row(D), row(D), row(D), row(XATT_HEADS), _full(kv.shape), _full(w_xo.shape)],
        out_specs=[row(D), _acc((M, 2 * D))],
        out_shape=[jax.ShapeDtypeStruct((S, D), BF16), jax.ShapeDtypeStruct((M, 2 * D), F32)],
        compiler_params=_params("arbitrary"),
    )(dx2, xq, o, lse, kv, w_xo)


def merge_bwd(dx1, w_out, a, b, z, w_ba, w_bs, *, tm, hook=Hook()):
    S, D = dx1.shape
    J, W, C = w_ba.shape
    nh, nho = len(hook.ins), len(hook.out_shapes)

    def body(dx_ref, wo_ref, a_ref, b_ref, g0, g1, g2, g3, wba_ref, wbs_ref, *rest):
        h_in, (da_ref, db_ref, dg_ref, dya_ref, dys_ref) = rest[:nh], rest[nh:nh + 5]
        h_out, h_sems = rest[nh + 5:nh + 5 + nho], rest[nh + 5 + nho:]
        _run_hook(hook, "before", h_in, h_out, h_sems)
        dm = _nt(dx_ref[...].astype(BF16), wo_ref[...])
        sa = _sigmoid(jnp.concatenate([g0[...], g1[...]], axis=1))
        sb = _sigmoid(jnp.concatenate([g2[...], g3[...]], axis=1))
        dg_ref[:, :D] = (dm * (a_ref[...] * (sa * (1.0 - sa))).astype(F32)).astype(BF16)
        dg_ref[:, D:] = (dm * (b_ref[...] * (sb * (1.0 - sb))).astype(F32)).astype(BF16)
        da = (dm * sa.astype(F32)).astype(BF16)
        db = (dm * sb.astype(F32)).astype(BF16)
        da_ref[...] = da
        db_ref[...] = db
        dya = _nt(da[:, 0:C], wba_ref[0])
        dys = _nt(db[:, 0:C], wbs_ref[0])
        for j in range(1, J):
            dya += _nt(da[:, j * C:(j + 1) * C], wba_ref[j])
            dys += _nt(db[:, j * C:(j + 1) * C], wbs_ref[j])
        dya_ref[...] = dya.astype(BF16)
        dys_ref[...] = dys.astype(BF16)
        _run_hook(hook, "after", h_in, h_out, h_sems)

    row = lambda n: pl.BlockSpec((tm, n), lambda i: (i, 0))
    return _hooked_call(
        body, hook, n_in=10, n_out=5, grid=(S // tm,), name="merge_bwd",
        in_specs=[row(D), _full(w_out.shape), row(D), row(D)] + _gate_specs(tm) + [_full(w_ba.shape), _full(w_bs.shape)],
        out_specs=[row(D), row(D), row(2 * D), row(W), row(W)],
        out_shape=[jax.ShapeDtypeStruct((S, D), BF16)] * 2 + [jax.ShapeDtypeStruct((S, 2 * D), BF16)]
        + [jax.ShapeDtypeStruct((S, W), BF16)] * 2,
        compiler_params=_params("arbitrary"),
    )(dx1, w_out, a, b, z, z, z, z, w_ba, w_bs, *hook.ins)


def attn_bwd(z, y, dy, lse, bias, hook=Hook()):
    S = z.shape[0]
    nt = S // ATT_TILE
    back = ATT_WIN - 1
    n_in, nh, nho = 5 + 2 * ATT_WIN, len(hook.ins), len(hook.out_shapes)

    def body(q_ref, *refs):
        k_refs, v_refs = refs[:ATT_WIN], refs[ATT_WIN:2 * ATT_WIN]
        y_ref, dy_ref, lse_ref, bias_ref = refs[2 * ATT_WIN:2 * ATT_WIN + 4]
        rest = refs[2 * ATT_WIN + 4:]
        h_in, (dqkv_ref, dbias_ref), h_out = rest[:nh], rest[nh:nh + 2], rest[nh + 2:nh + 2 + nho]
        dk_acc, dv_acc, dq_buf = rest[nh + 2 + nho:nh + 5 + nho]
        h_sems = rest[nh + 5 + nho:]
        t = pl.program_id(0)
        _run_hook(hook, "before", h_in, h_out, h_sems)

        @pl.when(t == 0)
        def _():
            dbias_ref[...] = jnp.zeros_like(dbias_ref)
            dk_acc[...] = jnp.zeros_like(dk_acc)
            dv_acc[...] = jnp.zeros_like(dv_acc)
            dq_buf[...] = jnp.zeros_like(dq_buf)

        @pl.when(t < nt)
        def _():
            valid = _att_valid(t)
            q_t = (q_ref[...] * ATT_SCALE).astype(F32).T.astype(BF16)
            do_t = dy_ref[...].astype(F32).T.astype(BF16)
            for h in range(ATT_HEADS):
                sl = slice(h * ATT_HEAD_DIM, (h + 1) * ATT_HEAD_DIM)
                k = jnp.concatenate([r[:, sl] for r in k_refs], axis=0)
                v = jnp.concatenate([r[:, sl] for r in v_refs], axis=0)
                q, do_b = q_ref[:, sl] * ATT_SCALE, dy_ref[:, sl]
                p = jnp.exp(_att_scores(q, k, bias_ref[h], valid) - lse_ref[:, h:h + 1])
                delta = jnp.sum(do_b.astype(F32) * y_ref[:, sl].astype(F32), axis=-1, keepdims=True)
                ds = p * (_nt(do_b, v) - delta)
                dbias_ref[h, :, 0:ATT_TILE] += ds[:, 0:ATT_TILE] + ds[:, ATT_TILE:2 * ATT_TILE] + ds[:, 2 * ATT_TILE:ATT_FAR]
                dbias_ref[h, :, ATT_TILE:] += ds[:, ATT_FAR:]
                ds_b = ds.astype(BF16)
                dq_buf[t % ATT_WIN, :, sl] = (_nn(ds_b, k) * ATT_SCALE).astype(BF16)
                dk_w = _nn(q_t[sl, :], ds_b)
                dv_w = _nn(do_t[sl, :], p.astype(BF16))
                for i in range(ATT_WIN):
                    slot = (t + 1 + i) % ATT_WIN
                    cols = slice(i * ATT_TILE, (i + 1) * ATT_TILE)
                    if i == back:
                        dk_acc[slot, sl, :] = dk_w[:, cols]
                        dv_acc[slot, sl, :] = dv_w[:, cols]
                    else:
                        dk_acc[slot, sl, :] += dk_w[:, cols]
                        dv_acc[slot, sl, :] += dv_w[:, cols]

        done = (t + 1) % ATT_WIN
        dqkv_ref[:, 0:512] = dq_buf[done]
        dqkv_ref[:, 512:1024] = dk_acc[done].T.astype(BF16)
        dqkv_ref[:, 1024:1536] = dv_acc[done].T.astype(BF16)
        _run_hook(hook, "after", h_in, h_out, h_sems)

    last = nt - 1
    dbias_shape = (ATT_HEADS, ATT_TILE, ATT_TILE + ATT_WIN * ATT_TILE - ATT_FAR)
    tile = lambda col, n=512: pl.BlockSpec((ATT_TILE, n), lambda t: (jnp.minimum(t, last), col))
    late = pl.BlockSpec((ATT_TILE, 1536), lambda t: (jnp.maximum(t - back, 0), 0))
    return _hooked_call(
        body, hook, n_in=n_in, n_out=2, grid=(nt + back,), name="attn_bwd",
        in_specs=[tile(0)] + _att_window_specs(nt, 1) + _att_window_specs(nt, 2)
        + [tile(0), tile(0), tile(0, ATT_HEADS), _full(bias.shape)],
        out_specs=[late, _acc(dbias_shape)],
        out_shape=[jax.ShapeDtypeStruct((S, 1536), BF16), jax.ShapeDtypeStruct(dbias_shape, F32)],
        scratch_shapes=[pltpu.VMEM((ATT_WIN, 512, ATT_TILE), F32)] * 2 + [pltpu.VMEM((ATT_WIN, ATT_TILE, 512), BF16)],
        compiler_params=_params("arbitrary"),
    )(z, *([z] * (2 * ATT_WIN)), y, dy, lse, bias, *hook.ins)


def sgu_bwd(z, dy, amean, lng, lnb, wpair, wtpair, bfull, maskpair, *, tm, hook=Hook()):
    S = z.shape[0]
    n = S // tm
    nh, nho = len(hook.ins), len(hook.out_shapes)

    def body(zu_ref, zv_ref, dy_ref, a_ref, lng_ref, lnb_ref, wpair_ref, wtpair_ref, bfull_ref, mask_ref, *rest):
        h_in, (duv_ref, dw_ref, dsgb_ref, dlng_ref, dlnb_ref), h_out = rest[:nh], rest[nh:nh + 5], rest[nh + 5:nh + 5 + nho]
        (xhat_scr, rstd_scr, vn_scr, dxh_scr, vbd, dbd, w_acc, b_acc, g_acc, s_acc) = rest[nh + 5 + nho:nh + 15 + nho]
        h_sems = rest[nh + 15 + nho:]
        i = pl.program_id(0)
        _run_hook(hook, "before", h_in, h_out, h_sems)

        @pl.when(i == 0)
        def _():
            for r in (w_acc, b_acc, g_acc, s_acc):
                r[...] = jnp.zeros_like(r)

        xhat_scr[...], rstd_scr[...], vn_scr[...] = _sgu_norm(zv_ref, a_ref, lng_ref, lnb_ref)

        def block(b, carry):
            rows = _sgu_rows(b)
            zu = zu_ref[rows, :].astype(F32)
            u, du = _gelu(zu), _gelu_grad(zu)
            dyv = dy_ref[rows, :].astype(F32)
            for p in range(SG_PAIRS):
                lanes = slice(p * 128, (p + 1) * 128)
                _pair_diag(vn_scr[rows, lanes], vbd.at[p])
                sv = _nn(wpair_ref[p], vbd[p]) + bfull_ref[:, lanes]
                duv_ref[rows, lanes] = (dyv[:, lanes] * sv * du[:, lanes]).astype(BF16)
                dsv = dyv[:, lanes] * u[:, lanes]
                b_acc[:, lanes] += dsv
                w_acc[p] += _nt(dsv.astype(BF16), vbd[p])
                _pair_diag(dsv, dbd.at[p])
                dvn = _nn(wtpair_ref[p], dbd[p])
                g_acc[:, lanes] += dvn * xhat_scr[rows, lanes]
                s_acc[:, lanes] += dvn
                dxh_scr[rows, lanes] = dvn * lng_ref[:, lanes]
            return carry

        lax.fori_loop(0, tm // SG_BLOCK, block, 0)
        dxh, xhat = dxh_scr[...], xhat_scr[...]
        dv = rstd_scr[...] * (dxh - _group_mean(dxh, a_ref) - xhat * _group_mean(dxh * xhat, a_ref))
        duv_ref[:, 512:1024] = (dv * _gelu_grad(zv_ref[...].astype(F32))).astype(BF16)

        @pl.when(i == n - 1)
        def _():
            dw_ref[...] = w_acc[...] * mask_ref[...]
            dsgb_ref[...] = _group_mean(b_acc[...], a_ref) * float(SG_GROUP_DIM)
            dlng_ref[...] = jnp.sum(g_acc[...], axis=0, keepdims=True)
            dlnb_ref[...] = jnp.sum(s_acc[...], axis=0, keepdims=True)

        _run_hook(hook, "after", h_in, h_out, h_sems)

    tile = lambda col: pl.BlockSpec((tm, 512), lambda i: (i, col))
    smalls = [amean, lng, lnb, wpair, wtpair, bfull, maskpair]
    pair_shape = (SG_PAIRS, SG_BLOCK, 2 * SG_BLOCK)
    return _hooked_call(
        body, hook, n_in=3 + len(smalls), n_out=5, grid=(n,), name="sgu_bwd",
        in_specs=[tile(3), tile(4), tile(0)] + [_full(a.shape) for a in smalls],
        out_specs=[pl.BlockSpec((tm, 1024), lambda i: (i, 0)), _acc(pair_shape), _acc((SG_BLOCK, 512)),
                   _acc((1, 512)), _acc((1, 512))],
        out_shape=[jax.ShapeDtypeStruct((S, 1024), BF16), jax.ShapeDtypeStruct(pair_shape, F32),
                   jax.ShapeDtypeStruct((SG_BLOCK, 512), F32), jax.ShapeDtypeStruct((1, 512), F32),
                   jax.ShapeDtypeStruct((1, 512), F32)],
        scratch_shapes=[pltpu.VMEM((tm, 512), F32)] * 4
        + [pltpu.VMEM((SG_PAIRS, 2 * SG_BLOCK, 128), BF16)] * 2
        + [pltpu.VMEM(pair_shape, F32)] + [pltpu.VMEM((SG_BLOCK, 512), F32)] * 3,
        compiler_params=_params("arbitrary"),
    )(z, z, dy, *smalls, *hook.ins)


def bias_colsum(p, far_part):
    H, _, L = p.shape
    near_lo = (ATT_WIN - 1) * ATT_TILE - REL_CLIP + 1
    near_hi = ATT_WIN * ATT_TILE

    def body(p_ref, f_ref, gw_ref, far_ref):
        k = lax.broadcasted_iota(jnp.int32, (1, L), 1)
        is_far = (k < near_lo) | (k >= near_hi)
        for h in range(H):
            g = jnp.sum(p_ref[h], axis=0, keepdims=True)
            gw_ref[h:h + 1, :] = g
            far_ref[h:h + 1, :] = jnp.zeros((1, 128), F32) + (jnp.sum(jnp.where(is_far, g, 0.0)) + jnp.sum(f_ref[h]))

    return pl.pallas_call(
        body, name="bias_colsum", in_specs=[_full(p.shape), _full(far_part.shape)],
        out_specs=[_acc((H, L)), _acc((H, 128))],
        out_shape=[jax.ShapeDtypeStruct((H, L), F32), jax.ShapeDtypeStruct((H, 128), F32)], grid=(1,),
        compiler_params=_params("arbitrary"),
    )(p, far_part)


def _row_tile(rows, cols):
    best = None
    for tr in range(16, rows + 1, 16):
        if rows % tr == 0 and tr * cols * 4 <= EW_BLOCK_BYTES:
            best = tr
    return best if best is not None else rows


def place_shard(w, own, name):
    R, C = w.shape
    r = R // 2
    tr = _row_tile(r, C)
    nr = r // tr

    def body(own_ref, w_ref, o_ref):
        o_ref[...] = w_ref[...].astype(BF16)

    return pl.pallas_call(
        body, name=name,
        grid_spec=pltpu.PrefetchScalarGridSpec(
            num_scalar_prefetch=1, grid=(2, nr),
            in_specs=[pl.BlockSpec((tr, C), lambda h, i, own: (h * nr + i, 0))],
            out_specs=pl.BlockSpec((None, None, tr, C), lambda h, i, own: (own[0], h, i, 0))),
        out_shape=jax.ShapeDtypeStruct((4, 2, r, C), BF16), compiler_params=_params("parallel", "parallel"),
    )(own, w)


def chip_partial(grad, recv, core, name):
    _, _, r, C = grad.shape
    tr = _row_tile(r, C)

    def body(core_ref, g_ref, r_ref, o_ref):
        o_ref[...] = (g_ref[...].astype(F32) + r_ref[...].astype(F32)).astype(BF16)

    spec = pl.BlockSpec((None, tr, C), lambda j, i, core: (j, i, 0))
    return pl.pallas_call(
        body, name=name,
        grid_spec=pltpu.PrefetchScalarGridSpec(
            num_scalar_prefetch=1, grid=(4, r // tr),
            in_specs=[pl.BlockSpec((None, None, tr, C), lambda j, i, core: (j, core[0], i, 0)), spec], out_specs=spec),
        out_shape=jax.ShapeDtypeStruct((4, r, C), BF16), compiler_params=_params("parallel", "parallel"),
    )(core, grad, recv)


def shard_sum(part, recv, own, core, name):
    _, r, C = part.shape
    tr = _row_tile(r, C)

    def body(own_ref, core_ref, p_ref, r0, r1, r2, o_ref):
        o_ref[...] = p_ref[...].astype(F32) + r0[...].astype(F32) + r1[...].astype(F32) + r2[...].astype(F32)

    return pl.pallas_call(
        body, name=name,
        grid_spec=pltpu.PrefetchScalarGridSpec(
            num_scalar_prefetch=2, grid=(r // tr,),
            in_specs=[pl.BlockSpec((None, tr, C), lambda i, own, core: (own[0], i, 0))]
            + [pl.BlockSpec((None, tr, C), lambda i, own, core, k=k: (k, i, 0)) for k in range(3)],
            out_specs=pl.BlockSpec((None, tr, C), lambda i, own, core: (core[0], i, 0))),
        out_shape=jax.ShapeDtypeStruct((2, r, C), F32), compiler_params=_params("parallel"),
    )(own, core, part, recv, recv, recv)


def adamw(w, g, m, v, name):
    R, C = w.shape
    tr = _row_tile(R, C)

    def body(w_ref, g_ref, m_ref, v_ref, d_ref, nm_ref, nv_ref):
        gv = g_ref[...]
        nm = ADAM_B1 * m_ref[...] + (1.0 - ADAM_B1) * gv
        nv = ADAM_B2 * v_ref[...] + (1.0 - ADAM_B2) * (gv * gv)
        m_hat = nm / (1.0 - ADAM_B1 ** ADAM_STEP)
        v_hat = nv / (1.0 - ADAM_B2 ** ADAM_STEP)
        d_ref[...] = -ADAM_LR * (m_hat / (jnp.sqrt(v_hat) + ADAM_EPS) + ADAM_WD * w_ref[...])
        nm_ref[...] = nm
        nv_ref[...] = nv

    spec = pl.BlockSpec((tr, C), lambda i: (i, 0))
    return pl.pallas_call(
        body, grid=(R // tr,), name=name, in_specs=[spec] * 4, out_specs=[spec] * 3,
        out_shape=[jax.ShapeDtypeStruct((R, C), F32)] * 3, compiler_params=_params("parallel"),
    )(w, g, m, v)


HBM = pl.BlockSpec(memory_space=pl.ANY)


def _place():
    x, y, c = lax.axis_index("x"), lax.axis_index("y"), lax.axis_index("c")
    chips = [(1 - x, y), (x, 1 - y), (1 - x, 1 - y)]
    return x, y, c, chips


def _gather_phases(n):
    def copies(buf, sems, kind):
        send_sems, recv_sems = sems
        x, y, c, chips = _place()

        def remote(w, k, slot, to):
            return pltpu.make_async_remote_copy(src_ref=slot, dst_ref=slot, send_sem=send_sems.at[w, k],
                                                recv_sem=recv_sems.at[w, k], device_id=to, device_id_type=MESH)

        def one(w, k, px, py):
            if kind == "mine":
                return remote(w, k, buf[w].at[2 * x + y, c], (px, py, c))
            if kind == "theirs":
                return remote(w, k, buf[w].at[2 * px + py, c], (px, py, c))
            if kind == "onward":
                return remote(w, 3 + k, buf[w].at[2 * px + py, c], (x, y, 1 - c))
            return remote(w, 3 + k, buf[w].at[2 * px + py, 1 - c], (x, y, 1 - c))

        return [one(w, k, px, py) for w in range(n) for k, (px, py) in enumerate(chips)]

    def start(_, buf, sems):
        for cp in copies(buf, sems, "mine"):
            cp.start()

    def relay(_, buf, sems):
        for theirs, onward in zip(copies(buf, sems, "theirs"), copies(buf, sems, "onward")):
            theirs.wait_recv()
            onward.start()

    def finish(_, buf, sems):
        for cp in copies(buf, sems, "relayed"):
            cp.wait_recv()
        for cp in copies(buf, sems, "mine") + copies(buf, sems, "onward"):
            cp.wait_send()

    return start, relay, finish


def _gather_sems(n):
    return (pltpu.SemaphoreType.DMA((n, 6)), pltpu.SemaphoreType.DMA((n, 6)))


def gather_weights(bufs):
    n = len(bufs)
    phases = _gather_phases(n)

    def body(*refs):
        for phase in phases:
            phase(None, refs[n:2 * n], refs[2 * n:])

    return pl.pallas_call(
        body, name="gather_weights", in_specs=[HBM] * n, out_specs=[HBM] * n,
        out_shape=[jax.ShapeDtypeStruct(b.shape, b.dtype) for b in bufs],
        input_output_aliases={i: i for i in range(n)}, scratch_shapes=list(_gather_sems(n)),
    )(*bufs)


def gather_hook(bufs, n_steps):
    start, relay, finish = _gather_phases(len(bufs))
    return Hook(ins=tuple(bufs), out_shapes=tuple(jax.ShapeDtypeStruct(b.shape, b.dtype) for b in bufs),
                alias=tuple((i, i) for i in range(len(bufs))),
                sems=_gather_sems(len(bufs)),
                steps=((0, "before", start), ((n_steps * 13) // 16, "before", relay), (n_steps - 1, "after", finish)))


def sibling_split(grads, name):
    n = len(grads)

    def body(*refs):
        src, dst = refs[:n], refs[n:2 * n]
        send_sems, recv_sems = refs[2 * n:]
        x, y, c, _ = _place()
        sends = [pltpu.make_async_remote_copy(src_ref=src[w].at[:, 1 - c], dst_ref=dst[w], send_sem=send_sems.at[w],
                                              recv_sem=recv_sems.at[w], device_id=(x, y, 1 - c), device_id_type=MESH)
                 for w in range(n)]
        for cp in sends:
            cp.start()
        for cp in sends:
            cp.wait()

    return pl.pallas_call(
        body, name=name, in_specs=[HBM] * n, out_specs=[HBM] * n,
        out_shape=[jax.ShapeDtypeStruct((4,) + g.shape[2:], g.dtype) for g in grads],
        scratch_shapes=[pltpu.SemaphoreType.DMA((n,))] * 2,
    )(*grads)


def split_hook(grads, n_steps):
    n = len(grads)

    def copies(src, dst, sems):
        send_sems, recv_sems = sems
        x, y, c, _ = _place()
        return [pltpu.make_async_remote_copy(src_ref=src[w].at[:, 1 - c], dst_ref=dst[w], send_sem=send_sems.at[w],
                                             recv_sem=recv_sems.at[w], device_id=(x, y, 1 - c), device_id_type=MESH)
                for w in range(n)]

    def start(src, dst, sems):
        for cp in copies(src, dst, sems):
            cp.start()

    def finish(src, dst, sems):
        for cp in copies(src, dst, sems):
            cp.wait()

    return Hook(ins=tuple(grads), out_shapes=tuple(jax.ShapeDtypeStruct((4,) + g.shape[2:], g.dtype) for g in grads),
                sems=(pltpu.SemaphoreType.DMA((n,)), pltpu.SemaphoreType.DMA((n,))),
                steps=((0, "before", start), (n_steps - 1, "after", finish)))


def exchange_hook(parts, n_steps):
    n = len(parts)

    def copies(src, dst, sems):
        send_sems, recv_sems = sems
        _, _, c, chips = _place()
        return [pltpu.make_async_remote_copy(
            src_ref=src[w].at[2 * px + py], dst_ref=dst[w].at[k], send_sem=send_sems.at[w, k],
            recv_sem=recv_sems.at[w, k], device_id=(px, py, c), device_id_type=MESH)
            for w in range(n) for k, (px, py) in enumerate(chips)]

    def start(src, dst, sems):
        for cp in copies(src, dst, sems):
            cp.start()

    def finish(src, dst, sems):
        for cp in copies(src, dst, sems):
            cp.wait()

    return Hook(ins=tuple(parts), out_shapes=tuple(jax.ShapeDtypeStruct((3,) + p.shape[1:], p.dtype) for p in parts),
                sems=(pltpu.SemaphoreType.DMA((n, 3)), pltpu.SemaphoreType.DMA((n, 3))),
                steps=((0, "before", start), (n_steps - 1, "after", finish)))


def sibling_join(sums):
    n = len(sums)

    def body(*refs):
        buf = refs[n:2 * n]
        send_sems, recv_sems = refs[2 * n:]
        x, y, c, _ = _place()
        sends = [pltpu.make_async_remote_copy(src_ref=buf[w].at[c], dst_ref=buf[w].at[c], send_sem=send_sems.at[w],
                                              recv_sem=recv_sems.at[w], device_id=(x, y, 1 - c), device_id_type=MESH)
                 for w in range(n)]
        for cp in sends:
            cp.start()
        for w, cp in enumerate(sends):
            cp.wait_send()
            pltpu.make_async_remote_copy(src_ref=buf[w].at[c], dst_ref=buf[w].at[1 - c], send_sem=send_sems.at[w],
                                         recv_sem=recv_sems.at[w], device_id=(x, y, 1 - c), device_id_type=MESH).wait_recv()

    return pl.pallas_call(
        body, name="sibling_join", in_specs=[HBM] * n, out_specs=[HBM] * n,
        out_shape=[jax.ShapeDtypeStruct(s.shape, s.dtype) for s in sums],
        input_output_aliases={i: i for i in range(n)},
        scratch_shapes=[pltpu.SemaphoreType.DMA((n,))] * 2,
    )(*sums)


def join_hook(sums, n_steps):
    n = len(sums)

    def copies(buf, sems, recv):
        send_sems, recv_sems = sems
        x, y, c, _ = _place()
        half = (1 - c) if recv else c
        return [pltpu.make_async_remote_copy(src_ref=buf[w].at[half], dst_ref=buf[w].at[half], send_sem=send_sems.at[w],
                                             recv_sem=recv_sems.at[w], device_id=(x, y, 1 - c), device_id_type=MESH)
                for w in range(n)]

    def start(_, buf, sems):
        for cp in copies(buf, sems, False):
            cp.start()

    def finish(_, buf, sems):
        for cp in copies(buf, sems, False):
            cp.wait_send()
        for cp in copies(buf, sems, True):
            cp.wait_recv()

    return Hook(ins=tuple(sums), out_shapes=tuple(jax.ShapeDtypeStruct(s.shape, s.dtype) for s in sums),
                alias=tuple((i, i) for i in range(n)),
                sems=(pltpu.SemaphoreType.DMA((n,)), pltpu.SemaphoreType.DMA((n,))),
                steps=((0, "before", start), (n_steps - 1, "after", finish)))


def allreduce_small(v):
    R, C = v.shape

    def body(v_ref, out_ref, all_ref, send_sems, recv_sems):
        x, y, c, chips = _place()
        me, sibling = (x, y, c), (x, y, 1 - c)

        def slot(px, py, pc):
            return all_ref.at[4 * px + 2 * py + pc]

        def copy(k, block, to, src=None):
            return pltpu.make_async_remote_copy(src_ref=slot(*block) if src is None else src, dst_ref=slot(*block),
                                                send_sem=send_sems.at[k], recv_sem=recv_sems.at[k], device_id=to,
                                                device_id_type=MESH)

        first = [copy(0, me, sibling, src=v_ref)] + [copy(1 + j, me, (*chip, c), src=v_ref) for j, chip in enumerate(chips)]
        for cp in first:
            cp.start()
        slot(*me)[...] = v_ref[...]
        passed = [copy(4 + j, (*chip, c), sibling) for j, chip in enumerate(chips)]
        for j, chip in enumerate(chips):
            copy(1 + j, (*chip, c), me).wait_recv()
            passed[j].start()
        copy(0, sibling, me).wait_recv()
        for j, chip in enumerate(chips):
            copy(4 + j, (*chip, 1 - c), me).wait_recv()
        for cp in first + passed:
            cp.wait_send()
        acc = all_ref[0]
        for k in range(1, 8):
            acc = acc + all_ref[k]
        out_ref[...] = acc

    vmem = pl.BlockSpec(memory_space=pltpu.VMEM)
    return pl.pallas_call(
        body, name="allreduce_small", in_specs=[vmem], out_specs=vmem, out_shape=jax.ShapeDtypeStruct((R, C), F32),
        scratch_shapes=[pltpu.VMEM((8, R, C), F32), pltpu.SemaphoreType.DMA((7,)), pltpu.SemaphoreType.DMA((7,))],
        compiler_params=pltpu.CompilerParams(vmem_limit_bytes=VMEM_LIMIT_V7X),
    )(v)


def allgather_hook(v, n_steps):
    R, C = v.shape

    def copies(v_ref, out, sems, kind):
        send_sems, recv_sems, _ = sems
        x, y, c, chips = _place()
        me, sibling = (x, y, c), (x, y, 1 - c)

        def slot(px, py, pc):
            return out.at[4 * px + 2 * py + pc]

        def copy(k, block, to, src=None):
            return pltpu.make_async_remote_copy(src_ref=slot(*block) if src is None else src, dst_ref=slot(*block),
                                                send_sem=send_sems.at[k], recv_sem=recv_sems.at[k], device_id=to,
                                                device_id_type=MESH)

        if kind == "mine":
            return [copy(0, me, sibling, src=v_ref)] + [copy(1 + j, me, (*chip, c), src=v_ref) for j, chip in enumerate(chips)]
        if kind == "theirs":
            return [copy(1 + j, (*chip, c), me) for j, chip in enumerate(chips)]
        if kind == "onward":
            return [copy(4 + j, (*chip, c), sibling) for j, chip in enumerate(chips)]
        return [copy(0, sibling, me)] + [copy(4 + j, (*chip, 1 - c), me) for j, chip in enumerate(chips)]

    def own(v_ref, out, sems):
        x, y, c, _ = _place()
        return pltpu.make_async_copy(v_ref, out.at[4 * x + 2 * y + c], sems[2])

    def start(ins, outs, sems):
        own(ins[0], outs[0], sems).start()
        for cp in copies(ins[0], outs[0], sems, "mine"):
            cp.start()

    def relay(ins, outs, sems):
        for theirs, onward in zip(copies(ins[0], outs[0], sems, "theirs"), copies(ins[0], outs[0], sems, "onward")):
            theirs.wait_recv()
            onward.start()

    def finish(ins, outs, sems):
        for cp in copies(ins[0], outs[0], sems, "relayed"):
            cp.wait_recv()
        for cp in copies(ins[0], outs[0], sems, "mine") + copies(ins[0], outs[0], sems, "onward"):
            cp.wait_send()
        own(ins[0], outs[0], sems).wait()

    return Hook(ins=(v,), out_shapes=(jax.ShapeDtypeStruct((8, R, C), F32),),
                sems=(pltpu.SemaphoreType.DMA((7,)), pltpu.SemaphoreType.DMA((7,)), pltpu.SemaphoreType.DMA(())),
                steps=((0, "before", start), ((n_steps * 13) // 16, "before", relay), (n_steps - 1, "after", finish)))


def merge_hooks(a, b):
    cut = (len(a.ins), len(a.out_shapes), len(a.sems))

    def part(fn, second):
        def run(ins, outs, sems):
            if second:
                fn(ins[cut[0]:], outs[cut[1]:], sems[cut[2]:])
            else:
                fn(ins[:cut[0]], outs[:cut[1]], sems[:cut[2]])
        return run

    steps = tuple((s, w, part(fn, False)) for s, w, fn in a.steps) + tuple((s, w, part(fn, True)) for s, w, fn in b.steps)
    alias = a.alias + tuple((i + cut[0], o + cut[1]) for i, o in b.alias)
    return Hook(a.ins + b.ins, a.out_shapes + b.out_shapes, alias, a.sems + b.sems, steps)


def sum_blocks(g):
    n, R, C = g.shape

    def body(g_ref, o_ref):
        acc = g_ref[0]
        for k in range(1, n):
            acc = acc + g_ref[k]
        o_ref[...] = acc

    return pl.pallas_call(body, name="sum_blocks", grid=(1,), in_specs=[_full(g.shape)], out_specs=_acc((R, C)),
                          out_shape=jax.ShapeDtypeStruct((R, C), F32), compiler_params=_params("arbitrary"))(g)


SMALL_PAD = 1024


def _pack_small(arrs):
    parts = []
    for a in arrs:
        f = a.reshape(-1).astype(F32)
        parts.append(jnp.pad(f, (0, (-f.shape[0]) % SMALL_PAD)))
    return jnp.concatenate(parts).reshape(-1, 128)


def _unpack_small(packed, shapes):
    flat = packed.reshape(-1)
    outs, off = [], 0
    for s in shapes:
        size = int(np.prod(s))
        outs.append(flat[off:off + size].reshape(s))
        off += size + (-size) % SMALL_PAD
    return outs


ATT_KEYS = ATT_WIN * ATT_TILE
ATT_NEAR_LO = (ATT_WIN - 1) * ATT_TILE - REL_CLIP + 1
ATT_PERIOD = ATT_KEYS + ATT_TILE + 1


def _att_bias_table(rel_bias):
    far = rel_bias[:, 2 * REL_CLIP:]
    near = rel_bias[:, 2 * REL_CLIP - 1:0:-1]
    w = jnp.concatenate([jnp.broadcast_to(far, (ATT_HEADS, ATT_NEAR_LO)), near,
                         jnp.broadcast_to(far, (ATT_HEADS, ATT_PERIOD - ATT_KEYS))], axis=1)
    rows = jnp.tile(w, (1, ATT_TILE))[:, :ATT_TILE * (ATT_PERIOD - 1)].reshape(ATT_HEADS, ATT_TILE, ATT_PERIOD - 1)
    i = np.arange(ATT_TILE)[:, None]
    m = np.arange(ATT_KEYS)[None, :]
    qc, kc = i // CHUNK, m // CHUNK
    band = (kc >= qc) & (kc <= qc + N_PREV_CHUNKS)
    return jnp.where(band[None], rows[:, :, :ATT_KEYS], NEG_INF)


def _rel_bias_grad(dbias):
    p = jnp.pad(dbias[:, :, ATT_TILE:], ((0, 0), (0, 0), (ATT_FAR, ATT_PERIOD - 1 - ATT_KEYS))).reshape(ATT_HEADS, -1)
    p = jnp.pad(p, ((0, 0), (0, ATT_TILE))).reshape(ATT_HEADS, ATT_TILE, ATT_PERIOD)
    gw, far = bias_colsum(p, dbias[:, :, :ATT_TILE])
    return jnp.concatenate([jnp.zeros((ATT_HEADS, 1), F32), gw[:, ATT_KEYS - 1:ATT_NEAR_LO - 1:-1], far[:, :1]], axis=1)


def kernel(x, mem, norm_mix_g, w_in, rel_bias, sg_ln_g, sg_ln_b, sg_w, sg_b, w_branch_att, w_branch_sg, w_out, norm_xattn_g, norm_mem_g, w_xq, w_xkv, w_xo, norm_ffn_g, w_ffn_in, w_ffn_out, norm_final_g, loss_target, m_norm_mix_g, m_w_in, m_rel_bias, m_sg_ln_g, m_sg_ln_b, m_sg_w, m_sg_b, m_w_branch_att, m_w_branch_sg, m_w_out, m_norm_xattn_g, m_norm_mem_g, m_w_xq, m_w_xkv, m_w_xo, m_norm_ffn_g, m_w_ffn_in, m_w_ffn_out, m_norm_final_g, v_norm_mix_g, v_w_in, v_rel_bias, v_sg_ln_g, v_sg_ln_b, v_sg_w, v_sg_b, v_w_branch_att, v_w_branch_sg, v_w_out, v_norm_xattn_g, v_norm_mem_g, v_w_xq, v_w_xkv, v_w_xo, v_norm_ffn_g, v_w_ffn_in, v_w_ffn_out, v_norm_final_g):
    S, D = x.shape[1], x.shape[2]
    x2d, mem2d, tgt = x[0], mem[0], loss_target[0]

    big_names = ["w_in", "w_branch_att", "w_branch_sg", "w_out", "w_xq", "w_xkv", "w_xo", "w_ffn_in", "w_ffn_out"]
    col_sharded = [True, True, True, False, False, True, False, True, False]
    big_w = [a[0] for a in (w_in, w_branch_att, w_branch_sg, w_out, w_xq, w_xkv, w_xo, w_ffn_in, w_ffn_out)]
    big_m = [a[0] for a in (m_w_in, m_w_branch_att, m_w_branch_sg, m_w_out, m_w_xq, m_w_xkv, m_w_xo, m_w_ffn_in, m_w_ffn_out)]
    big_v = [a[0] for a in (v_w_in, v_w_branch_att, v_w_branch_sg, v_w_out, v_w_xq, v_w_xkv, v_w_xo, v_w_ffn_in, v_w_ffn_out)]

    own = (2 * lax.axis_index("x") + lax.axis_index("y")).astype(jnp.int32).reshape(1)
    core = lax.axis_index("c").astype(jnp.int32).reshape(1)
    placed = [place_shard(w, own, "place_" + nm) for w, nm in zip(big_w, big_names)]

    def whole(g4, i):
        R, C = big_w[i].shape
        return g4.reshape(4, R, C) if col_sharded[i] else g4.reshape(4 * R, C)

    W_in = whole(gather_weights(placed[:1])[0], 0)

    g_mix, g_xat, g_mem, g_ffn = norm_mix_g, norm_xattn_g, norm_mem_g, norm_ffn_g
    g_fin = norm_final_g.reshape(1, D)
    bias = _att_bias_table(rel_bias[0])
    tt = np.arange(SG_BLOCK)
    sg_mask = (tt[None, :] // CHUNK) <= (tt[:, None] // CHUNK)
    wm_f = jnp.where(sg_mask[None], sg_w[0], 0.0)
    pairs = wm_f.reshape(SG_PAIRS, 2, SG_BLOCK, SG_BLOCK)
    wpair = jnp.transpose(pairs, (0, 2, 1, 3)).reshape(SG_PAIRS, SG_BLOCK, 2 * SG_BLOCK).astype(BF16)
    wtpair = jnp.transpose(pairs, (0, 3, 1, 2)).reshape(SG_PAIRS, SG_BLOCK, 2 * SG_BLOCK).astype(BF16)
    maskpair = jnp.asarray(np.tile(sg_mask, (SG_PAIRS, 1, 2)), F32)
    bfull = jnp.repeat(sg_b[0].T, SG_GROUP_DIM, axis=1)
    lng, lnb = sg_ln_g[0].reshape(1, 512), sg_ln_b[0].reshape(1, 512)
    gid = np.arange(512) // SG_GROUP_DIM
    amean = jnp.asarray((gid[:, None] == gid[None, :]) / SG_GROUP_DIM, BF16)

    h1, z, *early = norm_mm(x2d, g_mix, W_in, tm=ROW_TILE, scale=None, name="norm_mm_in",
                            hook=gather_hook(placed[1:7], S // ROW_TILE))
    y_att, lse_att, *late = attn_fwd(z, bias, gather_hook(placed[7:], S // ATT_TILE))
    W_ba, W_bs, W_out, W_xq, W_xkv, W_xo, W_fi, W_fo = [whole(g4, i + 1) for i, g4 in enumerate(early + late)]
    y_sg, = sgu_fwd(z, amean, lng, lnb, wpair, bfull, tm=SGU_TILE)
    a_br, b_br, merged, x1 = merge_fwd(y_att, y_sg, z, x2d, W_ba, W_bs, W_out, tm=ROW_TILE)
    h2, xq = norm_mm(x1, g_xat, W_xq.reshape(1, D, D), tm=XATT_TILE, scale=XATT_HEAD_DIM ** -0.5, name="norm_mm_xq")
    mn, kv = norm_mm(mem2d, g_mem, W_xkv, tm=mem2d.shape[0], scale=None, name="norm_mm_kv")
    o_x, lse_x, x2 = xattn_fwd(xq, kv, W_xo, x1, tm=XATT_TILE)
    h3, gu = norm_mm(x2, g_ffn, W_fi, tm=ROW_TILE, scale=None, name="norm_mm_ffn")
    act, loss_vec, dx3, dg_fin = ffn_out_loss(gu, W_fo, x2, g_fin, tgt, tm=ROW_TILE)

    dgu = ffn_act_bwd(dx3, gu, W_fo, tm=ROW_TILE, nchunk=2)
    gW_fo = tn_mm(act, dx3, shards=1, tn=D, tk=DW_TOKENS, name="dw_ffn_out")
    gW_fi = tn_mm(h3, dgu, shards=4, tn=2 * W_fi.shape[2], tk=DW_TOKENS, name="dw_ffn_in")
    dx2, dg_ffn = mm_nt_norm_bwd(dgu, W_fi, x2, g_ffn, dx3, tm=ROW_TILE, name="dx_ffn")
    dq_x, dkv = xattn_bwd(dx2, xq, o_x, lse_x, kv, W_xo, tm=XATT_TILE)
    tk_wide = min(2 * DW_TOKENS, S)
    gW_xo = tn_mm(o_x, dx2, shards=1, tn=D, tk=tk_wide, name="dw_xo")
    gW_xq = tn_mm(h2, dq_x, shards=1, tn=D, tk=tk_wide, name="dw_xq")
    dx1, dg_xat = mm_nt_norm_bwd(dq_x, W_xq.reshape(1, D, D), x1, g_xat, dx2, tm=XATT_TILE, name="dx_xq")
    gW_xkv = tn_mm(mn, dkv, shards=4, tn=2 * D, tk=mem2d.shape[0], name="dw_xkv")
    _, dg_mem = mm_nt_norm_bwd(dkv.astype(BF16), W_xkv, mem2d, g_mem, None, tm=mem2d.shape[0], name="dx_mem")
    def canon(g, i):
        R, C = big_w[i].shape
        return g.reshape(4, 2, R // 2, C)

    def partials(gs, recv, idx):
        return [chip_partial(g, r, core, "chip_partial_" + big_names[i]) for g, r, i in zip(gs, recv, idx)]

    early_g = [canon(g, i) for g, i in zip([gW_xq, gW_xkv, gW_xo, gW_fi, gW_fo], range(4, 9))]
    da, db, dgab, dy_att, dy_sg, *early_r = merge_bwd(dx1, W_out, a_br, b_br, z, W_ba, W_bs, tm=ROW_TILE,
                                                      hook=split_hook(early_g, S // ROW_TILE))
    gW_out = tn_mm(merged, dx1, shards=1, tn=D, tk=tk_wide, name="dw_out")
    gW_ba = tn_mm(y_att, da, shards=4, tn=D, tk=tk_wide, name="dw_branch_att")
    gW_bs = tn_mm(y_sg, db, shards=4, tn=D, tk=tk_wide, name="dw_branch_sg")
    mid_g = [canon(g, i) for g, i in zip([gW_ba, gW_bs, gW_out], range(1, 4))]
    parts_b = (partials(mid_g, sibling_split(mid_g, "sibling_split_mid"), range(1, 4))
               + partials(early_g, early_r, range(4, 9)))
    nt_bwd = S // ATT_TILE + ATT_WIN - 1
    dqkv, dbias, *from_chips_b = attn_bwd(z, y_att, dy_att, lse_att, bias, exchange_hook(parts_b, nt_bwd))
    sums_b = [shard_sum(p, r, own, core, "shard_sum_" + nm) for p, r, nm in zip(parts_b, from_chips_b, big_names[1:])]
    duv, dwpair, dsgb_full, dlng, dlnb, *joined_b = sgu_bwd(z, dy_sg, amean, lng, lnb, wpair, wtpair, bfull, maskpair,
                                                         tm=SGU_TILE, hook=join_hook(sums_b, S // SGU_TILE))
    dwm = jnp.transpose(dwpair.reshape(SG_PAIRS, SG_BLOCK, 2, SG_BLOCK), (0, 2, 1, 3))
    dsgb = dsgb_full[:, ::SG_GROUP_DIM].T
    dz = [dqkv, duv, dgab]
    gW_in = tn_mm_pieces(h1, dz, shards=4, per=2, tk=DW_TOKENS, name="dw_in")
    in_g = [canon(gW_in, 0)]
    parts_a = partials(in_g, sibling_split(in_g, "sibling_split_w_in"), [0])
    d_rel = _rel_bias_grad(dbias)
    small_g = [d_rel, dlng, dlnb, dwm, dsgb, dg_xat, dg_mem, dg_ffn, dg_fin, loss_vec[0, :1]]
    n_dx = S // ROW_TILE
    dx, dg_mix, from_chips_a, small_all = mm_nt_norm_bwd(
        dz, W_in, x2d, g_mix, dx1, tm=ROW_TILE, name="dx_in",
        hook=merge_hooks(exchange_hook(parts_a, n_dx), allgather_hook(_pack_small(small_g), n_dx)))

    joined = list(sibling_join([shard_sum(parts_a[0], from_chips_a, own, core, "shard_sum_w_in")])) + list(joined_b)
    big_out = []
    for j, w, m_, v_, nm in zip(joined, big_w, big_m, big_v, big_names):
        g = j.reshape(w.shape)
        big_out.append((g,) + tuple(adamw(w, g, m_, v_, "adamw_" + nm)))

    small_w = [norm_mix_g, rel_bias, sg_ln_g, sg_ln_b, sg_w, sg_b, norm_xattn_g, norm_mem_g, norm_ffn_g, norm_final_g]
    small_m = [m_norm_mix_g, m_rel_bias, m_sg_ln_g, m_sg_ln_b, m_sg_w, m_sg_b, m_norm_xattn_g, m_norm_mem_g, m_norm_ffn_g, m_norm_final_g]
    small_v = [v_norm_mix_g, v_rel_bias, v_sg_ln_g, v_sg_ln_b, v_sg_w, v_sg_b, v_norm_xattn_g, v_norm_mem_g, v_norm_ffn_g, v_norm_final_g]
    shapes = [w.shape for w in small_w]
    g_sum = jnp.concatenate([allreduce_small(_pack_small([dg_mix])), sum_blocks(small_all)], axis=0)
    zero = jnp.zeros((1,), F32)
    d_s, m_s, v_s = adamw(_pack_small(small_w + [zero]), g_sum, _pack_small(small_m + [zero]), _pack_small(small_v + [zero]),
                          "adamw_small")
    sg_, sd, sm, sv_ = (_unpack_small(p, shapes + [(1,)]) for p in (g_sum, d_s, m_s, v_s))
    loss = sg_[-1][0]

    order = ["norm_mix_g", "w_in", "rel_bias", "sg_ln_g", "sg_ln_b", "sg_w", "sg_b", "w_branch_att", "w_branch_sg", "w_out",
             "norm_xattn_g", "norm_mem_g", "w_xq", "w_xkv", "w_xo", "norm_ffn_g", "w_ffn_in", "w_ffn_out", "norm_final_g"]
    small_names = ["norm_mix_g", "rel_bias", "sg_ln_g", "sg_ln_b", "sg_w", "sg_b", "norm_xattn_g", "norm_mem_g", "norm_ffn_g",
                   "norm_final_g"]
    res = {}
    for i, nm in enumerate(small_names):
        res[nm] = (sg_[i], sd[i], sm[i], sv_[i])
    for nm, outs in zip(big_names, big_out):
        res[nm] = tuple(o[None] for o in outs)
    return (loss, dx[None], *[res[nm][0] for nm in order], *[res[nm][1] for nm in order],
            *[res[nm][2] for nm in order], *[res[nm][3] for nm in order])
```

```python
import functools
from typing import NamedTuple

import numpy as np
import jax
import jax.numpy as jnp
from jax import lax
from jax.experimental import pallas as pl
from jax.experimental.pallas import tpu as pltpu

F32, BF16 = jnp.float32, jnp.bfloat16
MESH = pl.DeviceIdType.MESH

EPS = 1e-6
NEG_INF = -1e30
CHUNK = 64
N_PREV_CHUNKS = 8
ATT_HEADS, ATT_HEAD_DIM = 8, 64
REL_CLIP = 128
SG_BLOCK, SG_GROUPS, SG_GROUP_DIM = 128, 8, 64
XATT_HEADS, XATT_HEAD_DIM = 4, 256
ATT_TILE = 128
ATT_WIN = 5
ADAM_LR, ADAM_B1, ADAM_B2, ADAM_EPS, ADAM_WD, ADAM_STEP = 0.001, 0.9, 0.999, 1e-08, 0.01, 10

VMEM_LIMIT_V7X = 56 * 1024 * 1024
EW_BLOCK_BYTES = 2 << 20
ROW_TILE = 512
DW_TOKENS = 1024
XATT_TILE = 1024
SGU_TILE = 1024
HBM = pl.BlockSpec(memory_space=pl.ANY)


def _params(*sem):
    return pltpu.CompilerParams(dimension_semantics=sem, vmem_limit_bytes=VMEM_LIMIT_V7X)


def _full(shape):
    n = len(shape)
    return pl.BlockSpec(shape, lambda *_: (0,) * n, pipeline_mode=pl.Buffered(1))


def _acc(shape):
    n = len(shape)
    return pl.BlockSpec(shape, lambda *_: (0,) * n)


class Hook(NamedTuple):
    ins: tuple = ()
    out_shapes: tuple = ()
    alias: tuple = ()
    sems: tuple = ()
    steps: tuple = ()


def _run_hook(hook, pos, h_in, h_out, h_sems):
    for at, where, fn in hook.steps:
        if where == pos:
            pl.when(pl.program_id(0) == at)(functools.partial(fn, h_in, h_out, h_sems))


def _hooked_call(body, hook, *, n_in, n_out, in_specs, out_specs, out_shape, scratch_shapes=(), **kw):
    nh = len(hook.ins)
    aliases = {n_in + i: n_out + o for i, o in hook.alias}
    return pl.pallas_call(
        body, in_specs=list(in_specs) + [HBM] * nh, out_specs=list(out_specs) + [HBM] * len(hook.out_shapes),
        out_shape=list(out_shape) + list(hook.out_shapes), scratch_shapes=list(scratch_shapes) + list(hook.sems),
        input_output_aliases=aliases, **kw)


def _nt(a, b):
    return lax.dot_general(a, b, (((1,), (1,)), ((), ())), preferred_element_type=F32)


def _tn(a, b):
    return lax.dot_general(a, b, (((0,), (0,)), ((), ())), preferred_element_type=F32)


def _nn(a, b):
    return jnp.dot(a, b, preferred_element_type=F32)


def _sigmoid(x):
    return 0.5 * jnp.tanh(0.5 * x) + 0.5


_GELU_C = float(np.sqrt(2.0 / np.pi))


def _gelu(x):
    t = jnp.tanh(_GELU_C * (x + 0.044715 * (x * x * x)))
    return x * (0.5 * (1.0 + t))


def _gelu_grad(x):
    t = jnp.tanh(_GELU_C * (x + 0.044715 * (x * x * x)))
    return 0.5 * (1.0 + t) + 0.5 * x * (1.0 - t * t) * (_GELU_C * (1.0 + 3.0 * 0.044715 * x * x))


def _rms_stats(xf):
    rstd = lax.rsqrt(jnp.mean(xf * xf, axis=-1, keepdims=True) + EPS)
    return rstd, xf * rstd


def _rms_bwd(xhat, rstd, g, dh):
    dxh = dh * g
    dx = rstd * (dxh - xhat * jnp.mean(dxh * xhat, axis=-1, keepdims=True))
    return dx, dh * xhat


def norm_mm(x, g, w, *, tm, scale, name, hook=Hook()):
    S, D = x.shape
    J, _, C = w.shape
    nh, nho = len(hook.ins), len(hook.out_shapes)

    def body(x_ref, g_ref, w_ref, *rest):
        h_in, (h_ref, z_ref), h_out, h_sems = rest[:nh], rest[nh:nh + 2], rest[nh + 2:nh + 2 + nho], rest[nh + 2 + nho:]
        _run_hook(hook, "before", h_in, h_out, h_sems)
        _, xhat = _rms_stats(x_ref[...])
        h = (xhat * g_ref[...]).astype(BF16)
        h_ref[...] = h
        for j in range(J):
            acc = _nn(h, w_ref[j])
            if scale is not None:
                acc = acc * scale
            z_ref[:, j * C:(j + 1) * C] = acc.astype(BF16)
        _run_hook(hook, "after", h_in, h_out, h_sems)

    return _hooked_call(
        body, hook, n_in=3, n_out=2, grid=(S // tm,), name=name,
        in_specs=[pl.BlockSpec((tm, D), lambda i: (i, 0)), _full((1, D)), _full((J, D, C))],
        out_specs=[pl.BlockSpec((tm, D), lambda i: (i, 0)), pl.BlockSpec((tm, J * C), lambda i: (i, 0))],
        out_shape=[jax.ShapeDtypeStruct((S, D), BF16), jax.ShapeDtypeStruct((S, J * C), BF16)],
        compiler_params=_params("arbitrary"),
    )(x, g, w, *hook.ins)


def _att_window_specs(nt, col):
    return [pl.BlockSpec((ATT_TILE, 512), lambda t, j=j: (jnp.clip(t - (ATT_WIN - 1) + j, 0, nt - 1), col))
            for j in range(ATT_WIN)]


ATT_SCALE = ATT_HEAD_DIM ** -0.5
ATT_FAR = 3 * ATT_TILE


def _att_scores(q_scaled, k, bias, valid):
    return jnp.where(valid, _nt(q_scaled, k) + bias, NEG_INF)


def _att_valid(t):
    kpos = lax.broadcasted_iota(jnp.int32, (ATT_TILE, ATT_WIN * ATT_TILE), 1) + (t - (ATT_WIN - 1)) * ATT_TILE
    return kpos >= 0


def attn_fwd(z, bias, hook=Hook()):
    S = z.shape[0]
    nt = S // ATT_TILE
    n_in, nh, nho = 2 + 2 * ATT_WIN, len(hook.ins), len(hook.out_shapes)

    def body(q_ref, *refs):
        k_refs, v_refs = refs[:ATT_WIN], refs[ATT_WIN:2 * ATT_WIN]
        bias_ref = refs[2 * ATT_WIN]
        rest = refs[2 * ATT_WIN + 1:]
        h_in, (y_ref, lse_ref), h_out = rest[:nh], rest[nh:nh + 2], rest[nh + 2:nh + 2 + nho]
        s_scr, p_scr = rest[nh + 2 + nho:nh + 4 + nho]
        h_sems = rest[nh + 4 + nho:]
        _run_hook(hook, "before", h_in, h_out, h_sems)
        valid = _att_valid(pl.program_id(0))
        heads = [slice(h * ATT_HEAD_DIM, (h + 1) * ATT_HEAD_DIM) for h in range(ATT_HEADS)]
        for h, sl in enumerate(heads):
            k = jnp.concatenate([r[:, sl] for r in k_refs], axis=0)
            s_scr[h] = _att_scores(q_ref[:, sl] * ATT_SCALE, k, bias_ref[h], valid)
        stats = []
        for h in range(ATT_HEADS):
            s = s_scr[h]
            m = jnp.max(s, axis=-1, keepdims=True)
            p = jnp.exp(s - m)
            stats.append((m, jnp.sum(p, axis=-1, keepdims=True)))
            p_scr[h] = p.astype(BF16)
        for h, sl in enumerate(heads):
            m, l = stats[h]
            v = jnp.concatenate([r[:, sl] for r in v_refs], axis=0)
            y_ref[:, sl] = (_nn(p_scr[h], v) / l).astype(BF16)
            lse_ref[:, h:h + 1] = m + jnp.log(l)
        _run_hook(hook, "after", h_in, h_out, h_sems)

    tile = lambda col: pl.BlockSpec((ATT_TILE, 512), lambda t: (t, col))
    return _hooked_call(
        body, hook, n_in=n_in, n_out=2, grid=(nt,), name="attn_fwd",
        in_specs=[tile(0)] + _att_window_specs(nt, 1) + _att_window_specs(nt, 2) + [_full(bias.shape)],
        out_specs=[tile(0), pl.BlockSpec((ATT_TILE, ATT_HEADS), lambda t: (t, 0))],
        out_shape=[jax.ShapeDtypeStruct((S, 512), BF16), jax.ShapeDtypeStruct((S, ATT_HEADS), F32)],
        scratch_shapes=[pltpu.VMEM((ATT_HEADS, ATT_TILE, ATT_WIN * ATT_TILE), F32),
                        pltpu.VMEM((ATT_HEADS, ATT_TILE, ATT_WIN * ATT_TILE), BF16)],
        compiler_params=_params("arbitrary"),
    )(z, *([z] * (2 * ATT_WIN)), bias, *hook.ins)


SG_PAIRS = SG_GROUPS // 2


def _group_mean(x, a_ref):
    rows = x.shape[0]
    hi = x.astype(BF16)
    lo = (x - hi.astype(F32)).astype(BF16)
    both = _nn(jnp.concatenate([hi, lo], axis=0), a_ref[...])
    return both[:rows] + both[rows:]


def _pair_diag(x, ref):
    first = lax.broadcasted_iota(jnp.int32, x.shape, 1) < SG_GROUP_DIM
    ref[0:SG_BLOCK, :] = jnp.where(first, x, 0.0).astype(BF16)
    ref[SG_BLOCK:2 * SG_BLOCK, :] = jnp.where(first, 0.0, x).astype(BF16)


def _sgu_norm(zv_ref, a_ref, lng_ref, lnb_ref):
    v = _gelu(zv_ref[...].astype(F32))
    vc = v - _group_mean(v, a_ref)
    rstd = lax.rsqrt(_group_mean(vc * vc, a_ref) + EPS)
    xhat = vc * rstd
    return xhat, rstd, xhat * lng_ref[...] + lnb_ref[...]


def _sgu_rows(b):
    return pl.ds(pl.multiple_of(b * SG_BLOCK, SG_BLOCK), SG_BLOCK)


def sgu_fwd(z, amean, lng, lnb, wpair, bfull, *, tm, hook=Hook()):
    S = z.shape[0]
    nh, nho = len(hook.ins), len(hook.out_shapes)

    def body(zu_ref, zv_ref, a_ref, lng_ref, lnb_ref, wpair_ref, bfull_ref, *rest):
        h_in, y_ref, h_out = rest[:nh], rest[nh], rest[nh + 1:nh + 1 + nho]
        vn_scr, vbd = rest[nh + 1 + nho:nh + 3 + nho]
        h_sems = rest[nh + 3 + nho:]
        _run_hook(hook, "before", h_in, h_out, h_sems)
        vn_scr[...] = _sgu_norm(zv_ref, a_ref, lng_ref, lnb_ref)[2]

        def block(b, carry):
            rows = _sgu_rows(b)
            u = _gelu(zu_ref[rows, :].astype(F32))
            for p in range(SG_PAIRS):
                lanes = slice(p * 128, (p + 1) * 128)
                _pair_diag(vn_scr[rows, lanes], vbd.at[p])
                sv = _nn(wpair_ref[p], vbd[p]) + bfull_ref[:, lanes]
                y_ref[rows, lanes] = (u[:, lanes] * sv).astype(BF16)
            return carry

        lax.fori_loop(0, tm // SG_BLOCK, block, 0)
        _run_hook(hook, "after", h_in, h_out, h_sems)

    tile = lambda col: pl.BlockSpec((tm, 512), lambda i: (i, col))
    smalls = [amean, lng, lnb, wpair, bfull]
    return _hooked_call(
        body, hook, n_in=2 + len(smalls), n_out=1, grid=(S // tm,), name="sgu_fwd",
        in_specs=[tile(3), tile(4)] + [_full(a.shape) for a in smalls],
        out_specs=[tile(0)], out_shape=[jax.ShapeDtypeStruct((S, 512), BF16)],
        scratch_shapes=[pltpu.VMEM((tm, 512), F32), pltpu.VMEM((SG_PAIRS, 2 * SG_BLOCK, 128), BF16)],
        compiler_params=_params("arbitrary"),
    )(z, z, *smalls, *hook.ins)


def _gate_specs(tm):
    return [pl.BlockSpec((tm, 512), lambda i, c=c: (i, c)) for c in (5, 6, 7, 8)]


def merge_fwd(y_att, y_sg, z, x, w_ba, w_bs, w_out, *, tm):
    S, D = x.shape
    J, _, C = w_ba.shape

    def body(ya_ref, ys_ref, g0, g1, g2, g3, x_ref, wba_ref, wbs_ref, wo_ref, a_ref, b_ref, m_ref, x1_ref):
        ya, ys = ya_ref[...], ys_ref[...]
        a = jnp.concatenate([_nn(ya, wba_ref[j]) for j in range(J)], axis=1)
        b = jnp.concatenate([_nn(ys, wbs_ref[j]) for j in range(J)], axis=1)
        sa = _sigmoid(jnp.concatenate([g0[...], g1[...]], axis=1).astype(F32))
        sb = _sigmoid(jnp.concatenate([g2[...], g3[...]], axis=1).astype(F32))
        a_ref[...] = a.astype(BF16)
        b_ref[...] = b.astype(BF16)
        merged = (sa * a + sb * b).astype(BF16)
        m_ref[...] = merged
        x1_ref[...] = x_ref[...] + _nn(merged, wo_ref[...])

    row = lambda n: pl.BlockSpec((tm, n), lambda i: (i, 0))
    return pl.pallas_call(
        body, grid=(S // tm,), name="merge_fwd",
        in_specs=[row(512), row(512)] + _gate_specs(tm) + [row(D), _full(w_ba.shape), _full(w_bs.shape), _full(w_out.shape)],
        out_specs=[row(D)] * 4,
        out_shape=[jax.ShapeDtypeStruct((S, D), BF16)] * 3 + [jax.ShapeDtypeStruct((S, D), F32)],
        compiler_params=_params("parallel"),
    )(y_att, y_sg, z, z, z, z, x, w_ba, w_bs, w_out)


def xattn_fwd(xq, kv, w_xo, x1, *, tm):
    S, D = xq.shape
    dh = XATT_HEAD_DIM

    def body(q_ref, kv_ref, wo_ref, x1_ref, o_ref, lse_ref, x2_ref):
        outs = []
        for h in range(XATT_HEADS):
            s = _nt(q_ref[:, h * dh:(h + 1) * dh], kv_ref[:, h * dh:(h + 1) * dh])
            m = jnp.max(s, axis=-1, keepdims=True)
            p = jnp.exp(s - m)
            l = jnp.sum(p, axis=-1, keepdims=True)
            outs.append((_nn(p.astype(BF16), kv_ref[:, D + h * dh:D + (h + 1) * dh]) / l).astype(BF16))
            lse_ref[:, h:h + 1] = m + jnp.log(l)
        o = jnp.concatenate(outs, axis=1)
        o_ref[...] = o
        x2_ref[...] = x1_ref[...] + _nn(o, wo_ref[...])

    row = lambda n: pl.BlockSpec((tm, n), lambda i: (i, 0))
    return pl.pallas_call(
        body, grid=(S // tm,), name="xattn_fwd",
        in_specs=[row(D), _full(kv.shape), _full(w_xo.shape), row(D)],
        out_specs=[row(D), row(XATT_HEADS), row(D)],
        out_shape=[jax.ShapeDtypeStruct((S, D), BF16), jax.ShapeDtypeStruct((S, XATT_HEADS), F32),
                   jax.ShapeDtypeStruct((S, D), F32)],
        compiler_params=_params("parallel"),
    )(xq, kv, w_xo, x1)


def ffn_out_loss(gu, w, x2, g, target, *, tm):
    S, D = x2.shape
    F = w.shape[0]

    def body(gu_ref, w_ref, x2_ref, g_ref, t_ref, act_ref, loss_ref, dx_ref, dg_ref):
        @pl.when(pl.program_id(0) == 0)
        def _():
            loss_ref[...] = jnp.zeros_like(loss_ref)
            dg_ref[...] = jnp.zeros_like(dg_ref)

        half = gu_ref[:, :F] * 0.5
        act = (half * gu_ref[:, F:]) * (jnp.tanh(half) + 1.0)
        act_ref[...] = act
        gv = g_ref[...]
        rstd, xhat = _rms_stats(x2_ref[...] + _nn(act, w_ref[...]))
        err = xhat * gv - t_ref[...]
        loss_ref[...] += 0.5 * jnp.sum(jnp.mean(err * err, axis=-1, keepdims=True))
        dx, dgc = _rms_bwd(xhat, rstd, gv, err * (1.0 / D))
        dx_ref[...] = dx
        dg_ref[...] += jnp.sum(dgc, axis=0, keepdims=True)

    row = lambda n: pl.BlockSpec((tm, n), lambda i: (i, 0))
    return pl.pallas_call(
        body, grid=(S // tm,), name="ffn_out_loss",
        in_specs=[row(2 * F), _full(w.shape), row(D), _full((1, D)), row(D)],
        out_specs=[row(F), _acc((1, 128)), row(D), _acc((1, D))],
        out_shape=[jax.ShapeDtypeStruct((S, F), BF16), jax.ShapeDtypeStruct((1, 128), F32),
                   jax.ShapeDtypeStruct((S, D), F32), jax.ShapeDtypeStruct((1, D), F32)],
        compiler_params=_params("arbitrary"),
    )(gu, w, x2, g, target)


def ffn_act_bwd(dx3, gu, w, *, tm, nchunk):
    S, D = dx3.shape
    F = w.shape[0]
    cn = F // nchunk

    def body(dx_ref, gu_ref, w_ref, dgu_ref):
        dxb = dx_ref[...].astype(BF16)
        for j in range(nchunk):
            dact = _nt(dxb, w_ref[j * cn:(j + 1) * cn, :])
            gate = gu_ref[:, j * cn:(j + 1) * cn]
            up = gu_ref[:, F + j * cn:F + (j + 1) * cn]
            sg = _sigmoid(gate)
            silu = gate * sg
            dsilu = sg + silu * (1.0 - sg)
            dgu_ref[:, j * cn:(j + 1) * cn] = (dact * (up * dsilu).astype(F32)).astype(BF16)
            dgu_ref[:, F + j * cn:F + (j + 1) * cn] = (dact * silu.astype(F32)).astype(BF16)

    row = lambda n: pl.BlockSpec((tm, n), lambda i: (i, 0))
    return pl.pallas_call(
        body, grid=(S // tm,), name="ffn_act_bwd",
        in_specs=[row(D), row(2 * F), _full(w.shape)], out_specs=row(2 * F),
        out_shape=jax.ShapeDtypeStruct((S, 2 * F), BF16), compiler_params=_params("parallel"),
    )(dx3, gu, w)


def _overlaps(widths, lo, hi):
    out, off = [], 0
    for p, wd in enumerate(widths):
        a, b = max(lo, off), min(hi, off + wd)
        if a < b:
            out.append((p, a - off, b - off, a - lo))
        off += wd
    return out


def tn_mm_pieces(a, pieces, *, shards, per, tk, name):
    K, M = a.shape
    widths = [p.shape[1] for p in pieces]
    C = sum(widths) // shards
    tn = per * C
    nk = K // tk
    n_tiles = shards // per

    def body(a_ref, *refs):
        p_refs, o_ref, acc_ref = refs[:len(pieces)], refs[len(pieces)], refs[len(pieces) + 1]
        n, k = pl.program_id(0), pl.program_id(1)

        @pl.when(k == 0)
        def _():
            acc_ref[...] = jnp.zeros_like(acc_ref)

        av = a_ref[...]
        for tile in range(n_tiles):
            @pl.when(n == tile)
            def _(tile=tile):
                for p, c0, c1, at in _overlaps(widths, tile * tn, (tile + 1) * tn):
                    acc_ref[:, at:at + c1 - c0] += _tn(av, p_refs[p][:, c0:c1])

        @pl.when(k == nk - 1)
        def _():
            for s in range(per):
                o_ref[s] = acc_ref[:, s * C:(s + 1) * C].astype(BF16)

    return pl.pallas_call(
        body, grid=(n_tiles, nk), name=name,
        in_specs=[pl.BlockSpec((tk, M), lambda n, k: (k, 0))] + [pl.BlockSpec((tk, wd), lambda n, k: (k, 0)) for wd in widths],
        out_specs=pl.BlockSpec((per, M, C), lambda n, k: (n, 0, 0)), out_shape=jax.ShapeDtypeStruct((shards, M, C), BF16),
        scratch_shapes=[pltpu.VMEM((M, tn), F32)], compiler_params=_params("parallel", "arbitrary"),
    )(a, *pieces)


def tn_mm(a, b, *, shards, tn, tk, name):
    K, M = a.shape
    N = b.shape[1]
    C = N // shards
    per = tn // C
    nk = K // tk

    def body(a_ref, b_ref, o_ref, acc_ref):
        k = pl.program_id(1)

        @pl.when(k == 0)
        def _():
            acc_ref[...] = jnp.zeros_like(acc_ref)

        acc_ref[...] += _tn(a_ref[...].astype(BF16), b_ref[...].astype(BF16))

        @pl.when(k == nk - 1)
        def _():
            if shards > 1:
                for s in range(per):
                    o_ref[s] = acc_ref[:, s * C:(s + 1) * C].astype(BF16)
            else:
                o_ref[...] = acc_ref[...].astype(BF16)

    if shards > 1:
        out_spec = pl.BlockSpec((per, M, C), lambda n, k: (n, 0, 0))
        out_shape = jax.ShapeDtypeStruct((shards, M, C), BF16)
    else:
        out_spec = pl.BlockSpec((M, tn), lambda n, k: (0, n))
        out_shape = jax.ShapeDtypeStruct((M, N), BF16)
    return pl.pallas_call(
        body, grid=(N // tn, nk), name=name,
        in_specs=[pl.BlockSpec((tk, M), lambda n, k: (k, 0)), pl.BlockSpec((tk, tn), lambda n, k: (k, n))],
        out_specs=out_spec, out_shape=out_shape, scratch_shapes=[pltpu.VMEM((M, tn), F32)],
        compiler_params=_params("parallel", "arbitrary"),
    )(a, b)


def mm_nt_norm_bwd(dy, w, x, g, dx_in, *, tm, name, hook=Hook()):
    S, D = x.shape
    J, _, C = w.shape
    has_in = dx_in is not None
    dys = list(dy) if isinstance(dy, (list, tuple)) else [dy]
    widths = [d.shape[1] for d in dys]
    nd = len(dys)
    n_in, nh, nho = nd + 3 + has_in, len(hook.ins), len(hook.out_shapes)

    def body(*refs):
        dy_refs = refs[:nd]
        w_ref, x_ref, g_ref = refs[nd:nd + 3]
        dxin_ref = refs[nd + 3] if has_in else None
        rest = refs[n_in:]
        h_in, (dx_ref, dg_ref), h_out, h_sems = rest[:nh], rest[nh:nh + 2], rest[nh + 2:nh + 2 + nho], rest[nh + 2 + nho:]
        _run_hook(hook, "before", h_in, h_out, h_sems)

        @pl.when(pl.program_id(0) == 0)
        def _():
            dg_ref[...] = jnp.zeros_like(dg_ref)

        dh = None
        for j in range(J):
            for p, c0, c1, at in _overlaps(widths, j * C, (j + 1) * C):
                part = _nt(dy_refs[p][:, c0:c1], w_ref[j, :, at:at + c1 - c0])
                dh = part if dh is None else dh + part
        rstd, xhat = _rms_stats(x_ref[...])
        dx, dgc = _rms_bwd(xhat, rstd, g_ref[...], dh)
        dx_ref[...] = dx + dxin_ref[...] if has_in else dx
        dg_ref[...] += jnp.sum(dgc, axis=0, keepdims=True)
        _run_hook(hook, "after", h_in, h_out, h_sems)

    row = lambda n: pl.BlockSpec((tm, n), lambda i: (i, 0))
    ins = dys + [w, x, g] + ([dx_in] if has_in else [])
    return _hooked_call(
        body, hook, n_in=n_in, n_out=2, grid=(S // tm,), name=name,
        in_specs=[row(wd) for wd in widths] + [_full(w.shape), row(D), _full((1, D))] + ([row(D)] if has_in else []),
        out_specs=[row(D), _acc((1, D))],
        out_shape=[jax.ShapeDtypeStruct((S, D), F32), jax.ShapeDtypeStruct((1, D), F32)],
        compiler_params=_params("arbitrary"),
    )(*ins, *hook.ins)


def xattn_bwd(dx2, xq, o, lse, kv, w_xo, *, tm):
    S, D = xq.shape
    M = kv.shape[0]
    dh = XATT_HEAD_DIM

    def body(dx_ref, q_ref, o_ref, lse_ref, kv_ref, wo_ref, dq_ref, dkv_ref):
        @pl.when(pl.program_id(0) == 0)
        def _():
            dkv_ref[...] = jnp.zeros_like(dkv_ref)

        do = _nt(dx_ref[...].astype(BF16), wo_ref[...])
        for h in range(XATT_HEADS):
            hs = slice(h * dh, (h + 1) * dh)
            vs = slice(D + h * dh, D + (h + 1) * dh)
            q, k, v = q_ref[:, hs], kv_ref[:, hs], kv_ref[:, vs]
            do_h = do[:, hs]
            do_b = do_h.astype(BF16)
            p = jnp.exp(_nt(q, k) - lse_ref[:, h:h + 1])
            delta = jnp.sum(do_h * o_ref[:, hs].astype(F32), axis=-1, keepdims=True)
            ds = (p * (_nt(do_b, v) - delta)).astype(BF16)
            dq_ref[:, hs] = (_nn(ds, k) * (dh ** -0.5)).astype(BF16)
            dkv_ref[:, hs] += _tn(ds, q)
            dkv_ref[:, vs] += _tn(p.astype(BF16), do_b)

    row = lambda n: pl.BlockSpec((tm, n), lambda i: (i, 0))
    return pl.pallas_call(
        body, grid=(S // tm,), name="xattn_bwd",
        in_specs=[row(D), row(D), row(D), row(XATT_HEADS), _full(kv.shape), _full(w_xo.shape)],
        out_specs=[row(D), _acc((M, 2 * D))],
        out_shape=[jax.ShapeDtypeStruct((S, D), BF16), jax.ShapeDtypeStruct((M, 2 * D), F32)],
        compiler_params=_params("arbitrary"),
    )(dx2, xq, o, lse, kv, w_xo)


def merge_bwd(dx1, w_out, a, b, z, w_ba, w_bs, *, tm, hook=Hook()):
    S, D = dx1.shape
    J, W, C = w_ba.shape
    nh, nho = len(hook.ins), len(hook.out_shapes)

    def body(dx_ref, wo_ref, a_ref, b_ref, g0, g1, g2, g3, wba_ref, wbs_ref, *rest):
        h_in, (da_ref, db_ref, dg_ref, dya_ref, dys_ref) = rest[:nh], rest[nh:nh + 5]
        h_out, h_sems = rest[nh + 5:nh + 5 + nho], rest[nh + 5 + nho:]
        _run_hook(hook, "before", h_in, h_out, h_sems)
        dm = _nt(dx_ref[...].astype(BF16), wo_ref[...])
        sa = _sigmoid(jnp.concatenate([g0[...], g1[...]], axis=1))
        sb = _sigmoid(jnp.concatenate([g2[...], g3[...]], axis=1))
        dg_ref[:, :D] = (dm * (a_ref[...] * (sa * (1.0 - sa))).astype(F32)).astype(BF16)
        dg_ref[:, D:] = (dm * (b_ref[...] * (sb * (1.0 - sb))).astype(F32)).astype(BF16)
        da = (dm * sa.astype(F32)).astype(BF16)
        db = (dm * sb.astype(F32)).astype(BF16)
        da_ref[...] = da
        db_ref[...] = db
        dya = _nt(da[:, 0:C], wba_ref[0])
        dys = _nt(db[:, 0:C], wbs_ref[0])
        for j in range(1, J):
            dya += _nt(da[:, j * C:(j + 1) * C], wba_ref[j])
            dys += _nt(db[:, j * C:(j + 1) * C], wbs_ref[j])
        dya_ref[...] = dya.astype(BF16)
        dys_ref[...] = dys.astype(BF16)
        _run_hook(hook, "after", h_in, h_out, h_sems)

    row = lambda n: pl.BlockSpec((tm, n), lambda i: (i, 0))
    return _hooked_call(
        body, hook, n_in=10, n_out=5, grid=(S // tm,), name="merge_bwd",
        in_specs=[row(D), _full(w_out.shape), row(D), row(D)] + _gate_specs(tm) + [_full(w_ba.shape), _full(w_bs.shape)],
        out_specs=[row(D), row(D), row(2 * D), row(W), row(W)],
        out_shape=[jax.ShapeDtypeStruct((S, D), BF16)] * 2 + [jax.ShapeDtypeStruct((S, 2 * D), BF16)]
        + [jax.ShapeDtypeStruct((S, W), BF16)] * 2,
        compiler_params=_params("arbitrary"),
    )(dx1, w_out, a, b, z, z, z, z, w_ba, w_bs, *hook.ins)


def attn_bwd(z, y, dy, lse, bias, hook=Hook()):
    S = z.shape[0]
    nt = S // ATT_TILE
    back = ATT_WIN - 1
    n_in, nh, nho = 5 + 2 * ATT_WIN, len(hook.ins), len(hook.out_shapes)

    def body(q_ref, *refs):
        k_refs, v_refs = refs[:ATT_WIN], refs[ATT_WIN:2 * ATT_WIN]
        y_ref, dy_ref, lse_ref, bias_ref = refs[2 * ATT_WIN:2 * ATT_WIN + 4]
        rest = refs[2 * ATT_WIN + 4:]
        h_in, (dqkv_ref, dbias_ref), h_out = rest[:nh], rest[nh:nh + 2], rest[nh + 2:nh + 2 + nho]
        dk_acc, dv_acc, dq_buf = rest[nh + 2 + nho:nh + 5 + nho]
        h_sems = rest[nh + 5 + nho:]
        t = pl.program_id(0)
        _run_hook(hook, "before", h_in, h_out, h_sems)

        @pl.when(t == 0)
        def _():
            dbias_ref[...] = jnp.zeros_like(dbias_ref)
            dk_acc[...] = jnp.zeros_like(dk_acc)
            dv_acc[...] = jnp.zeros_like(dv_acc)
            dq_buf[...] = jnp.zeros_like(dq_buf)

        def emit():
            done = (t + 1) % ATT_WIN
            dqkv_ref[:, 0:512] = dq_buf[done]
            dqkv_ref[:, 512:1024] = dk_acc[done].T.astype(BF16)
            dqkv_ref[:, 1024:1536] = dv_acc[done].T.astype(BF16)

        @pl.when(t < nt)
        def _():
            valid = _att_valid(t)
            q_t = (q_ref[...] * ATT_SCALE).astype(F32).T.astype(BF16)
            do_t = dy_ref[...].astype(F32).T.astype(BF16)
            for h in range(ATT_HEADS):
                sl = slice(h * ATT_HEAD_DIM, (h + 1) * ATT_HEAD_DIM)
                k = jnp.concatenate([r[:, sl] for r in k_refs], axis=0)
                v = jnp.concatenate([r[:, sl] for r in v_refs], axis=0)
                q, do_b = q_ref[:, sl] * ATT_SCALE, dy_ref[:, sl]
                p = jnp.exp(_att_scores(q, k, bias_ref[h], valid) - lse_ref[:, h:h + 1])
                delta = jnp.sum(do_b.astype(F32) * y_ref[:, sl].astype(F32), axis=-1, keepdims=True)
                ds = p * (_nt(do_b, v) - delta)
                dbias_ref[h, :, 0:ATT_TILE] += ds[:, 0:ATT_TILE] + ds[:, ATT_TILE:2 * ATT_TILE] + ds[:, 2 * ATT_TILE:ATT_FAR]
                dbias_ref[h, :, ATT_TILE:] += ds[:, ATT_FAR:]
                ds_b = ds.astype(BF16)
                dq_buf[t % ATT_WIN, :, sl] = (_nn(ds_b, k) * ATT_SCALE).astype(BF16)
                dk_w = _nn(q_t[sl, :], ds_b)
                dv_w = _nn(do_t[sl, :], p.astype(BF16))
                for i in range(ATT_WIN):
                    slot = (t + 1 + i) % ATT_WIN
                    cols = slice(i * ATT_TILE, (i + 1) * ATT_TILE)
                    if i == back:
                        dk_acc[slot, sl, :] = dk_w[:, cols]
                        dv_acc[slot, sl, :] = dv_w[:, cols]
                    else:
                        dk_acc[slot, sl, :] += dk_w[:, cols]
                        dv_acc[slot, sl, :] += dv_w[:, cols]
            emit()

        pl.when(t >= nt)(emit)
        _run_hook(hook, "after", h_in, h_out, h_sems)

    last = nt - 1
    dbias_shape = (ATT_HEADS, ATT_TILE, ATT_TILE + ATT_WIN * ATT_TILE - ATT_FAR)
    tile = lambda col, n=512: pl.BlockSpec((ATT_TILE, n), lambda t: (jnp.minimum(t, last), col))
    late = pl.BlockSpec((ATT_TILE, 1536), lambda t: (jnp.maximum(t - back, 0), 0))
    return _hooked_call(
        body, hook, n_in=n_in, n_out=2, grid=(nt + back,), name="attn_bwd",
        in_specs=[tile(0)] + _att_window_specs(nt, 1) + _att_window_specs(nt, 2)
        + [tile(0), tile(0), tile(0, ATT_HEADS), _full(bias.shape)],
        out_specs=[late, _acc(dbias_shape)],
        out_shape=[jax.ShapeDtypeStruct((S, 1536), BF16), jax.ShapeDtypeStruct(dbias_shape, F32)],
        scratch_shapes=[pltpu.VMEM((ATT_WIN, 512, ATT_TILE), F32)] * 2 + [pltpu.VMEM((ATT_WIN, ATT_TILE, 512), BF16)],
        compiler_params=_params("arbitrary"),
    )(z, *([z] * (2 * ATT_WIN)), y, dy, lse, bias, *hook.ins)


def sgu_bwd(z, dy, amean, lng, lnb, wpair, wtpair, bfull, maskpair, *, tm, hook=Hook()):
    S = z.shape[0]
    n = S // tm
    nh, nho = len(hook.ins), len(hook.out_shapes)

    def body(zu_ref, zv_ref, dy_ref, a_ref, lng_ref, lnb_ref, wpair_ref, wtpair_ref, bfull_ref, mask_ref, *rest):
        h_in, (duv_ref, dw_ref, dsgb_ref, dlng_ref, dlnb_ref), h_out = rest[:nh], rest[nh:nh + 5], rest[nh + 5:nh + 5 + nho]
        (xhat_scr, rstd_scr, vn_scr, dxh_scr, vbd, dbd, w_acc, b_acc, g_acc, s_acc) = rest[nh + 5 + nho:nh + 15 + nho]
        h_sems = rest[nh + 15 + nho:]
        i = pl.program_id(0)
        _run_hook(hook, "before", h_in, h_out, h_sems)

        @pl.when(i == 0)
        def _():
            for r in (w_acc, b_acc, g_acc, s_acc):
                r[...] = jnp.zeros_like(r)

        xhat_scr[...], rstd_scr[...], vn_scr[...] = _sgu_norm(zv_ref, a_ref, lng_ref, lnb_ref)

        def block(b, carry):
            rows = _sgu_rows(b)
            zu = zu_ref[rows, :].astype(F32)
            u, du = _gelu(zu), _gelu_grad(zu)
            dyv = dy_ref[rows, :].astype(F32)
            for p in range(SG_PAIRS):
                lanes = slice(p * 128, (p + 1) * 128)
                _pair_diag(vn_scr[rows, lanes], vbd.at[p])
                sv = _nn(wpair_ref[p], vbd[p]) + bfull_ref[:, lanes]
                duv_ref[rows, lanes] = (dyv[:, lanes] * sv * du[:, lanes]).astype(BF16)
                dsv = dyv[:, lanes] * u[:, lanes]
                b_acc[:, lanes] += dsv
                w_acc[p] += _nt(dsv.astype(BF16), vbd[p])
                _pair_diag(dsv, dbd.at[p])
                dvn = _nn(wtpair_ref[p], dbd[p])
                g_acc[:, lanes] += dvn * xhat_scr[rows, lanes]
                s_acc[:, lanes] += dvn
                dxh_scr[rows, lanes] = dvn * lng_ref[:, lanes]
            return carry

        lax.fori_loop(0, tm // SG_BLOCK, block, 0)
        dxh, xhat = dxh_scr[...], xhat_scr[...]
        dv = rstd_scr[...] * (dxh - _group_mean(dxh, a_ref) - xhat * _group_mean(dxh * xhat, a_ref))
        duv_ref[:, 512:1024] = (dv * _gelu_grad(zv_ref[...].astype(F32))).astype(BF16)

        @pl.when(i == n - 1)
        def _():
            dw_ref[...] = w_acc[...] * mask_ref[...]
            dsgb_ref[...] = _group_mean(b_acc[...], a_ref) * float(SG_GROUP_DIM)
            dlng_ref[...] = jnp.sum(g_acc[...], axis=0, keepdims=True)
            dlnb_ref[...] = jnp.sum(s_acc[...], axis=0, keepdims=True)

        _run_hook(hook, "after", h_in, h_out, h_sems)

    tile = lambda col: pl.BlockSpec((tm, 512), lambda i: (i, col))
    smalls = [amean, lng, lnb, wpair, wtpair, bfull, maskpair]
    pair_shape = (SG_PAIRS, SG_BLOCK, 2 * SG_BLOCK)
    return _hooked_call(
        body, hook, n_in=3 + len(smalls), n_out=5, grid=(n,), name="sgu_bwd",
        in_specs=[tile(3), tile(4), tile(0)] + [_full(a.shape) for a in smalls],
        out_specs=[pl.BlockSpec((tm, 1024), lambda i: (i, 0)), _acc(pair_shape), _acc((SG_BLOCK, 512)),
                   _acc((1, 512)), _acc((1, 512))],
        out_shape=[jax.ShapeDtypeStruct((S, 1024), BF16), jax.ShapeDtypeStruct(pair_shape, F32),
                   jax.ShapeDtypeStruct((SG_BLOCK, 512), F32), jax.ShapeDtypeStruct((1, 512), F32),
                   jax.ShapeDtypeStruct((1, 512), F32)],
        scratch_shapes=[pltpu.VMEM((tm, 512), F32)] * 4
        + [pltpu.VMEM((SG_PAIRS, 2 * SG_BLOCK, 128), BF16)] * 2
        + [pltpu.VMEM(pair_shape, F32)] + [pltpu.VMEM((SG_BLOCK, 512), F32)] * 3,
        compiler_params=_params("arbitrary"),
    )(z, z, dy, *smalls, *hook.ins)


def bias_colsum(p, far_part):
    H, _, L = p.shape
    near_lo = (ATT_WIN - 1) * ATT_TILE - REL_CLIP + 1
    near_hi = ATT_WIN * ATT_TILE

    def body(p_ref, f_ref, gw_ref, far_ref):
        k = lax.broadcasted_iota(jnp.int32, (1, L), 1)
        is_far = (k < near_lo) | (k >= near_hi)
        for h in range(H):
            g = jnp.sum(p_ref[h], axis=0, keepdims=True)
            gw_ref[h:h + 1, :] = g
            far_ref[h:h + 1, :] = jnp.zeros((1, 128), F32) + (jnp.sum(jnp.where(is_far, g, 0.0)) + jnp.sum(f_ref[h]))

    return pl.pallas_call(
        body, name="bias_colsum", in_specs=[_full(p.shape), _full(far_part.shape)],
        out_specs=[_acc((H, L)), _acc((H, 128))],
        out_shape=[jax.ShapeDtypeStruct((H, L), F32), jax.ShapeDtypeStruct((H, 128), F32)], grid=(1,),
        compiler_params=_params("arbitrary"),
    )(p, far_part)


def _row_tile(rows, cols):
    best = None
    for tr in range(16, rows + 1, 16):
        if rows % tr == 0 and tr * cols * 4 <= EW_BLOCK_BYTES:
            best = tr
    return best if best is not None else rows


def place_shard(w, own, name):
    R, C = w.shape
    r = R // 2
    tr = _row_tile(r, C)
    nr = r // tr

    def body(own_ref, w_ref, o_ref):
        o_ref[...] = w_ref[...].astype(BF16)

    return pl.pallas_call(
        body, name=name,
        grid_spec=pltpu.PrefetchScalarGridSpec(
            num_scalar_prefetch=1, grid=(2, nr),
            in_specs=[pl.BlockSpec((tr, C), lambda h, i, own: (h * nr + i, 0))],
            out_specs=pl.BlockSpec((None, None, tr, C), lambda h, i, own: (own[0], h, i, 0))),
        out_shape=jax.ShapeDtypeStruct((4, 2, r, C), BF16), compiler_params=_params("parallel", "parallel"),
    )(own, w)


def chip_partial(grad, recv, core, name):
    _, _, r, C = grad.shape
    tr = _row_tile(r, C)

    def body(core_ref, g_ref, r_ref, o_ref):
        o_ref[...] = (g_ref[...].astype(F32) + r_ref[...].astype(F32)).astype(BF16)

    spec = pl.BlockSpec((None, tr, C), lambda j, i, core: (j, i, 0))
    return pl.pallas_call(
        body, name=name,
        grid_spec=pltpu.PrefetchScalarGridSpec(
            num_scalar_prefetch=1, grid=(4, r // tr),
            in_specs=[pl.BlockSpec((None, None, tr, C), lambda j, i, core: (j, core[0], i, 0)), spec], out_specs=spec),
        out_shape=jax.ShapeDtypeStruct((4, r, C), BF16), compiler_params=_params("parallel", "parallel"),
    )(core, grad, recv)


def shard_sum(part, recv, own, core, name):
    _, r, C = part.shape
    tr = _row_tile(r, C)

    def body(own_ref, core_ref, p_ref, r0, r1, r2, o_ref):
        o_ref[...] = p_ref[...].astype(F32) + r0[...].astype(F32) + r1[...].astype(F32) + r2[...].astype(F32)

    return pl.pallas_call(
        body, name=name,
        grid_spec=pltpu.PrefetchScalarGridSpec(
            num_scalar_prefetch=2, grid=(r // tr,),
            in_specs=[pl.BlockSpec((None, tr, C), lambda i, own, core: (own[0], i, 0))]
            + [pl.BlockSpec((None, tr, C), lambda i, own, core, k=k: (k, i, 0)) for k in range(3)],
            out_specs=pl.BlockSpec((None, tr, C), lambda i, own, core: (core[0], i, 0))),
        out_shape=jax.ShapeDtypeStruct((2, r, C), F32), compiler_params=_params("parallel"),
    )(own, core, part, recv, recv, recv)


def adamw(w, g, m, v, name):
    R, C = w.shape
    tr = _row_tile(R, C)

    def body(w_ref, g_ref, m_ref, v_ref, d_ref, nm_ref, nv_ref):
        gv = g_ref[...]
        nm = ADAM_B1 * m_ref[...] + (1.0 - ADAM_B1) * gv
        nv = ADAM_B2 * v_ref[...] + (1.0 - ADAM_B2) * (gv * gv)
        m_hat = nm / (1.0 - ADAM_B1 ** ADAM_STEP)
        v_hat = nv / (1.0 - ADAM_B2 ** ADAM_STEP)
        d_ref[...] = -ADAM_LR * (m_hat / (jnp.sqrt(v_hat) + ADAM_EPS) + ADAM_WD * w_ref[...])
        nm_ref[...] = nm
        nv_ref[...] = nv

    spec = pl.BlockSpec((tr, C), lambda i: (i, 0))
    return pl.pallas_call(
        body, grid=(R // tr,), name=name, in_specs=[spec] * 4, out_specs=[spec] * 3,
        out_shape=[jax.ShapeDtypeStruct((R, C), F32)] * 3, compiler_params=_params("parallel"),
    )(w, g, m, v)


HBM = pl.BlockSpec(memory_space=pl.ANY)


def _place():
    x, y, c = lax.axis_index("x"), lax.axis_index("y"), lax.axis_index("c")
    chips = [(1 - x, y), (x, 1 - y), (1 - x, 1 - y)]
    return x, y, c, chips


def _gather_phases(n):
    def copies(buf, sems, kind):
        send_sems, recv_sems = sems
        x, y, c, chips = _place()

        def remote(w, k, slot, to):
            return pltpu.make_async_remote_copy(src_ref=slot, dst_ref=slot, send_sem=send_sems.at[w, k],
                                                recv_sem=recv_sems.at[w, k], device_id=to, device_id_type=MESH)

        def one(w, k, px, py):
            if kind == "mine":
                return remote(w, k, buf[w].at[2 * x + y, c], (px, py, c))
            if kind == "theirs":
                return remote(w, k, buf[w].at[2 * px + py, c], (px, py, c))
            if kind == "onward":
                return remote(w, 3 + k, buf[w].at[2 * px + py, c], (x, y, 1 - c))
            return remote(w, 3 + k, buf[w].at[2 * px + py, 1 - c], (x, y, 1 - c))

        return [one(w, k, px, py) for w in range(n) for k, (px, py) in enumerate(chips)]

    def start(_, buf, sems):
        for cp in copies(buf, sems, "mine"):
            cp.start()

    def relay(_, buf, sems):
        for theirs, onward in zip(copies(buf, sems, "theirs"), copies(buf, sems, "onward")):
            theirs.wait_recv()
            onward.start()

    def finish(_, buf, sems):
        for cp in copies(buf, sems, "relayed"):
            cp.wait_recv()
        for cp in copies(buf, sems, "mine") + copies(buf, sems, "onward"):
            cp.wait_send()

    return start, relay, finish


def _gather_sems(n):
    return (pltpu.SemaphoreType.DMA((n, 6)), pltpu.SemaphoreType.DMA((n, 6)))


def gather_weights(bufs):
    n = len(bufs)
    phases = _gather_phases(n)

    def body(*refs):
        for phase in phases:
            phase(None, refs[n:2 * n], refs[2 * n:])

    return pl.pallas_call(
        body, name="gather_weights", in_specs=[HBM] * n, out_specs=[HBM] * n,
        out_shape=[jax.ShapeDtypeStruct(b.shape, b.dtype) for b in bufs],
        input_output_aliases={i: i for i in range(n)}, scratch_shapes=list(_gather_sems(n)),
    )(*bufs)


def gather_hook(bufs, n_steps):
    start, relay, finish = _gather_phases(len(bufs))
    return Hook(ins=tuple(bufs), out_shapes=tuple(jax.ShapeDtypeStruct(b.shape, b.dtype) for b in bufs),
                alias=tuple((i, i) for i in range(len(bufs))),
                sems=_gather_sems(len(bufs)),
                steps=((0, "before", start), ((n_steps * 13) // 16, "before", relay), (n_steps - 1, "after", finish)))


def sibling_split(grads, name):
    n = len(grads)

    def body(*refs):
        src, dst = refs[:n], refs[n:2 * n]
        send_sems, recv_sems = refs[2 * n:]
        x, y, c, _ = _place()
        sends = [pltpu.make_async_remote_copy(src_ref=src[w].at[:, 1 - c], dst_ref=dst[w], send_sem=send_sems.at[w],
                                              recv_sem=recv_sems.at[w], device_id=(x, y, 1 - c), device_id_type=MESH)
                 for w in range(n)]
        for cp in sends:
            cp.start()
        for cp in sends:
            cp.wait()

    return pl.pallas_call(
        body, name=name, in_specs=[HBM] * n, out_specs=[HBM] * n,
        out_shape=[jax.ShapeDtypeStruct((4,) + g.shape[2:], g.dtype) for g in grads],
        scratch_shapes=[pltpu.SemaphoreType.DMA((n,))] * 2,
    )(*grads)


def split_hook(grads, n_steps):
    n = len(grads)

    def copies(src, dst, sems):
        send_sems, recv_sems = sems
        x, y, c, _ = _place()
        return [pltpu.make_async_remote_copy(src_ref=src[w].at[:, 1 - c], dst_ref=dst[w], send_sem=send_sems.at[w],
                                             recv_sem=recv_sems.at[w], device_id=(x, y, 1 - c), device_id_type=MESH)
                for w in range(n)]

    def start(src, dst, sems):
        for cp in copies(src, dst, sems):
            cp.start()

    def finish(src, dst, sems):
        for cp in copies(src, dst, sems):
            cp.wait()

    return Hook(ins=tuple(grads), out_shapes=tuple(jax.ShapeDtypeStruct((4,) + g.shape[2:], g.dtype) for g in grads),
                sems=(pltpu.SemaphoreType.DMA((n,)), pltpu.SemaphoreType.DMA((n,))),
                steps=((0, "before", start), (n_steps - 1, "after", finish)))


def exchange_hook(parts, n_steps):
    n = len(parts)

    def copies(src, dst, sems):
        send_sems, recv_sems = sems
        _, _, c, chips = _place()
        return [pltpu.make_async_remote_copy(
            src_ref=src[w].at[2 * px + py], dst_ref=dst[w].at[k], send_sem=send_sems.at[w, k],
            recv_sem=recv_sems.at[w, k], device_id=(px, py, c), device_id_type=MESH)
            for w in range(n) for k, (px, py) in enumerate(chips)]

    def start(src, dst, sems):
        for cp in copies(src, dst, sems):
            cp.start()

    def finish(src, dst, sems):
        for cp in copies(src, dst, sems):
            cp.wait()

    return Hook(ins=tuple(parts), out_shapes=tuple(jax.ShapeDtypeStruct((3,) + p.shape[1:], p.dtype) for p in parts),
                sems=(pltpu.SemaphoreType.DMA((n, 3)), pltpu.SemaphoreType.DMA((n, 3))),
                steps=((0, "before", start), (n_steps - 1, "after", finish)))


def sibling_join(sums):
    n = len(sums)

    def body(*refs):
        buf = refs[n:2 * n]
        send_sems, recv_sems = refs[2 * n:]
        x, y, c, _ = _place()
        sends = [pltpu.make_async_remote_copy(src_ref=buf[w].at[c], dst_ref=buf[w].at[c], send_sem=send_sems.at[w],
                                              recv_sem=recv_sems.at[w], device_id=(x, y, 1 - c), device_id_type=MESH)
                 for w in range(n)]
        for cp in sends:
            cp.start()
        for w, cp in enumerate(sends):
            cp.wait_send()
            pltpu.make_async_remote_copy(src_ref=buf[w].at[c], dst_ref=buf[w].at[1 - c], send_sem=send_sems.at[w],
                                         recv_sem=recv_sems.at[w], device_id=(x, y, 1 - c), device_id_type=MESH).wait_recv()

    return pl.pallas_call(
        body, name="sibling_join", in_specs=[HBM] * n, out_specs=[HBM] * n,
        out_shape=[jax.ShapeDtypeStruct(s.shape, s.dtype) for s in sums],
        input_output_aliases={i: i for i in range(n)},
        scratch_shapes=[pltpu.SemaphoreType.DMA((n,))] * 2,
    )(*sums)


def join_hook(sums, n_steps):
    n = len(sums)

    def copies(buf, sems, recv):
        send_sems, recv_sems = sems
        x, y, c, _ = _place()
        half = (1 - c) if recv else c
        return [pltpu.make_async_remote_copy(src_ref=buf[w].at[half], dst_ref=buf[w].at[half], send_sem=send_sems.at[w],
                                             recv_sem=recv_sems.at[w], device_id=(x, y, 1 - c), device_id_type=MESH)
                for w in range(n)]

    def start(_, buf, sems):
        for cp in copies(buf, sems, False):
            cp.start()

    def finish(_, buf, sems):
        for cp in copies(buf, sems, False):
            cp.wait_send()
        for cp in copies(buf, sems, True):
            cp.wait_recv()

    return Hook(ins=tuple(sums), out_shapes=tuple(jax.ShapeDtypeStruct(s.shape, s.dtype) for s in sums),
                alias=tuple((i, i) for i in range(n)),
                sems=(pltpu.SemaphoreType.DMA((n,)), pltpu.SemaphoreType.DMA((n,))),
                steps=((0, "before", start), (n_steps - 1, "after", finish)))


def allreduce_small(v):
    R, C = v.shape

    def body(v_ref, out_ref, all_ref, send_sems, recv_sems):
        x, y, c, chips = _place()
        me, sibling = (x, y, c), (x, y, 1 - c)

        def slot(px, py, pc):
            return all_ref.at[4 * px + 2 * py + pc]

        def copy(k, block, to, src=None):
            return pltpu.make_async_remote_copy(src_ref=slot(*block) if src is None else src, dst_ref=slot(*block),
                                                send_sem=send_sems.at[k], recv_sem=recv_sems.at[k], device_id=to,
                                                device_id_type=MESH)

        first = [copy(0, me, sibling, src=v_ref)] + [copy(1 + j, me, (*chip, c), src=v_ref) for j, chip in enumerate(chips)]
        for cp in first:
            cp.start()
        slot(*me)[...] = v_ref[...]
        passed = [copy(4 + j, (*chip, c), sibling) for j, chip in enumerate(chips)]
        for j, chip in enumerate(chips):
            copy(1 + j, (*chip, c), me).wait_recv()
            passed[j].start()
        copy(0, sibling, me).wait_recv()
        for j, chip in enumerate(chips):
            copy(4 + j, (*chip, 1 - c), me).wait_recv()
        for cp in first + passed:
            cp.wait_send()
        acc = all_ref[0]
        for k in range(1, 8):
            acc = acc + all_ref[k]
        out_ref[...] = acc

    vmem = pl.BlockSpec(memory_space=pltpu.VMEM)
    return pl.pallas_call(
        body, name="allreduce_small", in_specs=[vmem], out_specs=vmem, out_shape=jax.ShapeDtypeStruct((R, C), F32),
        scratch_shapes=[pltpu.VMEM((8, R, C), F32), pltpu.SemaphoreType.DMA((7,)), pltpu.SemaphoreType.DMA((7,))],
        compiler_params=pltpu.CompilerParams(vmem_limit_bytes=VMEM_LIMIT_V7X),
    )(v)


def allgather_hook(v, n_steps):
    R, C = v.shape

    def copies(v_ref, out, sems, kind):
        send_sems, recv_sems, _ = sems
        x, y, c, chips = _place()
        me, sibling = (x, y, c), (x, y, 1 - c)

        def slot(px, py, pc):
            return out.at[4 * px + 2 * py + pc]

        def copy(k, block, to, src=None):
            return pltpu.make_async_remote_copy(src_ref=slot(*block) if src is None else src, dst_ref=slot(*block),
                                                send_sem=send_sems.at[k], recv_sem=recv_sems.at[k], device_id=to,
                                                device_id_type=MESH)

        if kind == "mine":
            return [copy(0, me, sibling, src=v_ref)] + [copy(1 + j, me, (*chip, c), src=v_ref) for j, chip in enumerate(chips)]
        if kind == "theirs":
            return [copy(1 + j, (*chip, c), me) for j, chip in enumerate(chips)]
        if kind == "onward":
            return [copy(4 + j, (*chip, c), sibling) for j, chip in enumerate(chips)]
        return [copy(0, sibling, me)] + [copy(4 + j, (*chip, 1 - c), me) for j, chip in enumerate(chips)]

    def own(v_ref, out, sems):
        x, y, c, _ = _place()
        return pltpu.make_async_copy(v_ref, out.at[4 * x + 2 * y + c], sems[2])

    def start(ins, outs, sems):
        own(ins[0], outs[0], sems).start()
        for cp in copies(ins[0], outs[0], sems, "mine"):
            cp.start()

    def relay(ins, outs, sems):
        for theirs, onward in zip(copies(ins[0], outs[0], sems, "theirs"), copies(ins[0], outs[0], sems, "onward")):
            theirs.wait_recv()
            onward.start()

    def finish(ins, outs, sems):
        for cp in copies(ins[0], outs[0], sems, "relayed"):
            cp.wait_recv()
        for cp in copies(ins[0], outs[0], sems, "mine") + copies(ins[0], outs[0], sems, "onward"):
            cp.wait_send()
        own(ins[0], outs[0], sems).wait()

    return Hook(ins=(v,), out_shapes=(jax.ShapeDtypeStruct((8, R, C), F32),),
                sems=(pltpu.SemaphoreType.DMA((7,)), pltpu.SemaphoreType.DMA((7,)), pltpu.SemaphoreType.DMA(())),
                steps=((0, "before", start), ((n_steps * 13) // 16, "before", relay), (n_steps - 1, "after", finish)))


def merge_hooks(a, b):
    cut = (len(a.ins), len(a.out_shapes), len(a.sems))

    def part(fn, second):
        def run(ins, outs, sems):
            if second:
                fn(ins[cut[0]:], outs[cut[1]:], sems[cut[2]:])
            else:
                fn(ins[:cut[0]], outs[:cut[1]], sems[:cut[2]])
        return run

    steps = tuple((s, w, part(fn, False)) for s, w, fn in a.steps) + tuple((s, w, part(fn, True)) for s, w, fn in b.steps)
    alias = a.alias + tuple((i + cut[0], o + cut[1]) for i, o in b.alias)
    return Hook(a.ins + b.ins, a.out_shapes + b.out_shapes, alias, a.sems + b.sems, steps)


def sum_blocks(g):
    n, R, C = g.shape

    def body(g_ref, o_ref):
        acc = g_ref[0]
        for k in range(1, n):
            acc = acc + g_ref[k]
        o_ref[...] = acc

    return pl.pallas_call(body, name="sum_blocks", grid=(1,), in_specs=[_full(g.shape)], out_specs=_acc((R, C)),
                          out_shape=jax.ShapeDtypeStruct((R, C), F32), compiler_params=_params("arbitrary"))(g)


SMALL_PAD = 1024


def _pack_small(arrs):
    parts = []
    for a in arrs:
        f = a.reshape(-1).astype(F32)
        parts.append(jnp.pad(f, (0, (-f.shape[0]) % SMALL_PAD)))
    return jnp.concatenate(parts).reshape(-1, 128)


def _unpack_small(packed, shapes):
    flat = packed.reshape(-1)
    outs, off = [], 0
    for s in shapes:
        size = int(np.prod(s))
        outs.append(flat[off:off + size].reshape(s))
        off += size + (-size) % SMALL_PAD
    return outs


ATT_KEYS = ATT_WIN * ATT_TILE
ATT_NEAR_LO = (ATT_WIN - 1) * ATT_TILE - REL_CLIP + 1
ATT_PERIOD = ATT_KEYS + ATT_TILE + 1


def _att_bias_table(rel_bias):
    far = rel_bias[:, 2 * REL_CLIP:]
    near = rel_bias[:, 2 * REL_CLIP - 1:0:-1]
    w = jnp.concatenate([jnp.broadcast_to(far, (ATT_HEADS, ATT_NEAR_LO)), near,
                         jnp.broadcast_to(far, (ATT_HEADS, ATT_PERIOD - ATT_KEYS))], axis=1)
    rows = jnp.tile(w, (1, ATT_TILE))[:, :ATT_TILE * (ATT_PERIOD - 1)].reshape(ATT_HEADS, ATT_TILE, ATT_PERIOD - 1)
    i = np.arange(ATT_TILE)[:, None]
    m = np.arange(ATT_KEYS)[None, :]
    qc, kc = i // CHUNK, m // CHUNK
    band = (kc >= qc) & (kc <= qc + N_PREV_CHUNKS)
    return jnp.where(band[None], rows[:, :, :ATT_KEYS], NEG_INF)


def _rel_bias_grad(dbias):
    p = jnp.pad(dbias[:, :, ATT_TILE:], ((0, 0), (0, 0), (ATT_FAR, ATT_PERIOD - 1 - ATT_KEYS))).reshape(ATT_HEADS, -1)
    p = jnp.pad(p, ((0, 0), (0, ATT_TILE))).reshape(ATT_HEADS, ATT_TILE, ATT_PERIOD)
    gw, far = bias_colsum(p, dbias[:, :, :ATT_TILE])
    return jnp.concatenate([jnp.zeros((ATT_HEADS, 1), F32), gw[:, ATT_KEYS - 1:ATT_NEAR_LO - 1:-1], far[:, :1]], axis=1)


def kernel(x, mem, norm_mix_g, w_in, rel_bias, sg_ln_g, sg_ln_b, sg_w, sg_b, w_branch_att, w_branch_sg, w_out, norm_xattn_g, norm_mem_g, w_xq, w_xkv, w_xo, norm_ffn_g, w_ffn_in, w_ffn_out, norm_final_g, loss_target, m_norm_mix_g, m_w_in, m_rel_bias, m_sg_ln_g, m_sg_ln_b, m_sg_w, m_sg_b, m_w_branch_att, m_w_branch_sg, m_w_out, m_norm_xattn_g, m_norm_mem_g, m_w_xq, m_w_xkv, m_w_xo, m_norm_ffn_g, m_w_ffn_in, m_w_ffn_out, m_norm_final_g, v_norm_mix_g, v_w_in, v_rel_bias, v_sg_ln_g, v_sg_ln_b, v_sg_w, v_sg_b, v_w_branch_att, v_w_branch_sg, v_w_out, v_norm_xattn_g, v_norm_mem_g, v_w_xq, v_w_xkv, v_w_xo, v_norm_ffn_g, v_w_ffn_in, v_w_ffn_out, v_norm_final_g):
    S, D = x.shape[1], x.shape[2]
    x2d, mem2d, tgt = x[0], mem[0], loss_target[0]

    big_names = ["w_in", "w_branch_att", "w_branch_sg", "w_out", "w_xq", "w_xkv", "w_xo", "w_ffn_in", "w_ffn_out"]
    col_sharded = [True, True, True, False, False, True, False, True, False]
    big_w = [a[0] for a in (w_in, w_branch_att, w_branch_sg, w_out, w_xq, w_xkv, w_xo, w_ffn_in, w_ffn_out)]
    big_m = [a[0] for a in (m_w_in, m_w_branch_att, m_w_branch_sg, m_w_out, m_w_xq, m_w_xkv, m_w_xo, m_w_ffn_in, m_w_ffn_out)]
    big_v = [a[0] for a in (v_w_in, v_w_branch_att, v_w_branch_sg, v_w_out, v_w_xq, v_w_xkv, v_w_xo, v_w_ffn_in, v_w_ffn_out)]

    own = (2 * lax.axis_index("x") + lax.axis_index("y")).astype(jnp.int32).reshape(1)
    core = lax.axis_index("c").astype(jnp.int32).reshape(1)
    placed = [place_shard(w, own, "place_" + nm) for w, nm in zip(big_w, big_names)]

    def whole(g4, i):
        R, C = big_w[i].shape
        return g4.reshape(4, R, C) if col_sharded[i] else g4.reshape(4 * R, C)

    W_in = whole(gather_weights(placed[:1])[0], 0)

    g_mix, g_xat, g_mem, g_ffn = norm_mix_g, norm_xattn_g, norm_mem_g, norm_ffn_g
    g_fin = norm_final_g.reshape(1, D)
    bias = _att_bias_table(rel_bias[0])
    tt = np.arange(SG_BLOCK)
    sg_mask = (tt[None, :] // CHUNK) <= (tt[:, None] // CHUNK)
    wm_f = jnp.where(sg_mask[None], sg_w[0], 0.0)
    pairs = wm_f.reshape(SG_PAIRS, 2, SG_BLOCK, SG_BLOCK)
    wpair = jnp.transpose(pairs, (0, 2, 1, 3)).reshape(SG_PAIRS, SG_BLOCK, 2 * SG_BLOCK).astype(BF16)
    wtpair = jnp.transpose(pairs, (0, 3, 1, 2)).reshape(SG_PAIRS, SG_BLOCK, 2 * SG_BLOCK).astype(BF16)
    maskpair = jnp.asarray(np.tile(sg_mask, (SG_PAIRS, 1, 2)), F32)
    bfull = jnp.repeat(sg_b[0].T, SG_GROUP_DIM, axis=1)
    lng, lnb = sg_ln_g[0].reshape(1, 512), sg_ln_b[0].reshape(1, 512)
    gid = np.arange(512) // SG_GROUP_DIM
    amean = jnp.asarray((gid[:, None] == gid[None, :]) / SG_GROUP_DIM, BF16)

    h1, z, *early = norm_mm(x2d, g_mix, W_in, tm=ROW_TILE, scale=None, name="norm_mm_in",
                            hook=gather_hook(placed[1:7], S // ROW_TILE))
    y_att, lse_att, *late = attn_fwd(z, bias, gather_hook(placed[7:], S // ATT_TILE))
    W_ba, W_bs, W_out, W_xq, W_xkv, W_xo, W_fi, W_fo = [whole(g4, i + 1) for i, g4 in enumerate(early + late)]
    y_sg, = sgu_fwd(z, amean, lng, lnb, wpair, bfull, tm=SGU_TILE)
    a_br, b_br, merged, x1 = merge_fwd(y_att, y_sg, z, x2d, W_ba, W_bs, W_out, tm=ROW_TILE)
    h2, xq = norm_mm(x1, g_xat, W_xq.reshape(1, D, D), tm=XATT_TILE, scale=XATT_HEAD_DIM ** -0.5, name="norm_mm_xq")
    mn, kv = norm_mm(mem2d, g_mem, W_xkv, tm=mem2d.shape[0], scale=None, name="norm_mm_kv")
    o_x, lse_x, x2 = xattn_fwd(xq, kv, W_xo, x1, tm=XATT_TILE)
    h3, gu = norm_mm(x2, g_ffn, W_fi, tm=ROW_TILE, scale=None, name="norm_mm_ffn")
    act, loss_vec, dx3, dg_fin = ffn_out_loss(gu, W_fo, x2, g_fin, tgt, tm=ROW_TILE)

    dgu = ffn_act_bwd(dx3, gu, W_fo, tm=ROW_TILE, nchunk=2)
    gW_fo = tn_mm(act, dx3, shards=1, tn=D, tk=DW_TOKENS, name="dw_ffn_out")
    gW_fi = tn_mm(h3, dgu, shards=4, tn=2 * W_fi.shape[2], tk=DW_TOKENS, name="dw_ffn_in")
    dx2, dg_ffn = mm_nt_norm_bwd(dgu, W_fi, x2, g_ffn, dx3, tm=ROW_TILE, name="dx_ffn")
    dq_x, dkv = xattn_bwd(dx2, xq, o_x, lse_x, kv, W_xo, tm=XATT_TILE)
    tk_wide = min(2 * DW_TOKENS, S)
    gW_xo = tn_mm(o_x, dx2, shards=1, tn=D, tk=tk_wide, name="dw_xo")
    gW_xq = tn_mm(h2, dq_x, shards=1, tn=D, tk=tk_wide, name="dw_xq")
    dx1, dg_xat = mm_nt_norm_bwd(dq_x, W_xq.reshape(1, D, D), x1, g_xat, dx2, tm=XATT_TILE, name="dx_xq")
    gW_xkv = tn_mm(mn, dkv, shards=4, tn=2 * D, tk=mem2d.shape[0], name="dw_xkv")
    _, dg_mem = mm_nt_norm_bwd(dkv.astype(BF16), W_xkv, mem2d, g_mem, None, tm=mem2d.shape[0], name="dx_mem")
    def canon(g, i):
        R, C = big_w[i].shape
        return g.reshape(4, 2, R // 2, C)

    def partials(gs, recv, idx):
        return [chip_partial(g, r, core, "chip_partial_" + big_names[i]) for g, r, i in zip(gs, recv, idx)]

    early_g = [canon(g, i) for g, i in zip([gW_xq, gW_xkv, gW_xo, gW_fi, gW_fo], range(4, 9))]
    da, db, dgab, dy_att, dy_sg, *early_r = merge_bwd(dx1, W_out, a_br, b_br, z, W_ba, W_bs, tm=ROW_TILE,
                                                      hook=split_hook(early_g, S // ROW_TILE))
    gW_out = tn_mm(merged, dx1, shards=1, tn=D, tk=tk_wide, name="dw_out")
    gW_ba = tn_mm(y_att, da, shards=4, tn=D, tk=tk_wide, name="dw_branch_att")
    gW_bs = tn_mm(y_sg, db, shards=4, tn=D, tk=tk_wide, name="dw_branch_sg")
    mid_g = [canon(g, i) for g, i in zip([gW_ba, gW_bs, gW_out], range(1, 4))]
    parts_b = (partials(mid_g, sibling_split(mid_g, "sibling_split_mid"), range(1, 4))
               + partials(early_g, early_r, range(4, 9)))
    nt_bwd = S // ATT_TILE + ATT_WIN - 1
    dqkv, dbias, *from_chips_b = attn_bwd(z, y_att, dy_att, lse_att, bias, exchange_hook(parts_b, nt_bwd))
    sums_b = [shard_sum(p, r, own, core, "shard_sum_" + nm) for p, r, nm in zip(parts_b, from_chips_b, big_names[1:])]
    duv, dwpair, dsgb_full, dlng, dlnb, *joined_b = sgu_bwd(z, dy_sg, amean, lng, lnb, wpair, wtpair, bfull, maskpair,
                                                         tm=SGU_TILE, hook=join_hook(sums_b, S // SGU_TILE))
    dwm = jnp.transpose(dwpair.reshape(SG_PAIRS, SG_BLOCK, 2, SG_BLOCK), (0, 2, 1, 3))
    dsgb = dsgb_full[:, ::SG_GROUP_DIM].T
    dz = [dqkv, duv, dgab]
    gW_in = tn_mm_pieces(h1, dz, shards=4, per=2, tk=DW_TOKENS, name="dw_in")
    in_g = [canon(gW_in, 0)]
    parts_a = partials(in_g, sibling_split(in_g, "sibling_split_w_in"), [0])
    d_rel = _rel_bias_grad(dbias)
    small_g = [d_rel, dlng, dlnb, dwm, dsgb, dg_xat, dg_mem, dg_ffn, dg_fin, loss_vec[0, :1]]
    n_dx = S // ROW_TILE
    dx, dg_mix, from_chips_a, small_all = mm_nt_norm_bwd(
        dz, W_in, x2d, g_mix, dx1, tm=ROW_TILE, name="dx_in",
        hook=merge_hooks(exchange_hook(parts_a, n_dx), allgather_hook(_pack_small(small_g), n_dx)))

    joined = list(sibling_join([shard_sum(parts_a[0], from_chips_a, own, core, "shard_sum_w_in")])) + list(joined_b)
    big_out = []
    for j, w, m_, v_, nm in zip(joined, big_w, big_m, big_v, big_names):
        g = j.reshape(w.shape)
        big_out.append((g,) + tuple(adamw(w, g, m_, v_, "adamw_" + nm)))

    small_w = [norm_mix_g, rel_bias, sg_ln_g, sg_ln_b, sg_w, sg_b, norm_xattn_g, norm_mem_g, norm_ffn_g, norm_final_g]
    small_m = [m_norm_mix_g, m_rel_bias, m_sg_ln_g, m_sg_ln_b, m_sg_w, m_sg_b, m_norm_xattn_g, m_norm_mem_g, m_norm_ffn_g, m_norm_final_g]
    small_v = [v_norm_mix_g, v_rel_bias, v_sg_ln_g, v_sg_ln_b, v_sg_w, v_sg_b, v_norm_xattn_g, v_norm_mem_g, v_norm_ffn_g, v_norm_final_g]
    shapes = [w.shape for w in small_w]
    g_sum = jnp.concatenate([allreduce_small(_pack_small([dg_mix])), sum_blocks(small_all)], axis=0)
    zero = jnp.zeros((1,), F32)
    d_s, m_s, v_s = adamw(_pack_small(small_w + [zero]), g_sum, _pack_small(small_m + [zero]), _pack_small(small_v + [zero]),
                          "adamw_small")
    sg_, sd, sm, sv_ = (_unpack_small(p, shapes + [(1,)]) for p in (g_sum, d_s, m_s, v_s))
    loss = sg_[-1][0]

    order = ["norm_mix_g", "w_in", "rel_bias", "sg_ln_g", "sg_ln_b", "sg_w", "sg_b", "w_branch_att", "w_branch_sg", "w_out",
             "norm_xattn_g", "norm_mem_g", "w_xq", "w_xkv", "w_xo", "norm_ffn_g", "w_ffn_in", "w_ffn_out", "norm_final_g"]
    small_names = ["norm_mix_g", "rel_bias", "sg_ln_g", "sg_ln_b", "sg_w", "sg_b", "norm_xattn_g", "norm_mem_g", "norm_ffn_g",
                   "norm_final_g"]
    res = {}
    for i, nm in enumerate(small_names):
        res[nm] = (sg_[i], sd[i], sm[i], sv_[i])
    for nm, outs in zip(big_names, big_out):
        res[nm] = tuple(o[None] for o in outs)
    return (loss, dx[None], *[res[nm][0] for nm in order], *[res[nm][1] for nm in order],
            *[res[nm][2] for nm in order], *[res[nm][3] for nm in order])
```

```python
import functools
from typing import NamedTuple

import numpy as np
import jax
import jax.numpy as jnp
from jax import lax
from jax.experimental import pallas as pl
from jax.experimental.pallas import tpu as pltpu

F32, BF16 = jnp.float32, jnp.bfloat16
MESH = pl.DeviceIdType.MESH

EPS = 1e-6
NEG_INF = -1e30
CHUNK = 64
N_PREV_CHUNKS = 8
ATT_HEADS, ATT_HEAD_DIM = 8, 64
REL_CLIP = 128
SG_BLOCK, SG_GROUPS, SG_GROUP_DIM = 128, 8, 64
XATT_HEADS, XATT_HEAD_DIM = 4, 256
ATT_TILE = 128
ATT_WIN = 5
ADAM_LR, ADAM_B1, ADAM_B2, ADAM_EPS, ADAM_WD, ADAM_STEP = 0.001, 0.9, 0.999, 1e-08, 0.01, 10

VMEM_LIMIT_V7X = 56 * 1024 * 1024
EW_BLOCK_BYTES = 2 << 20
ROW_TILE = 512
DW_TOKENS = 1024
XATT_TILE = 1024
SGU_TILE = 1024
HBM = pl.BlockSpec(memory_space=pl.ANY)


def _params(*sem):
    return pltpu.CompilerParams(dimension_semantics=sem, vmem_limit_bytes=VMEM_LIMIT_V7X)


def _full(shape):
    n = len(shape)
    return pl.BlockSpec(shape, lambda *_: (0,) * n, pipeline_mode=pl.Buffered(1))


def _acc(shape):
    n = len(shape)
    return pl.BlockSpec(shape, lambda *_: (0,) * n)


class Hook(NamedTuple):
    ins: tuple = ()
    out_shapes: tuple = ()
    alias: tuple = ()
    sems: tuple = ()
    steps: tuple = ()


def _run_hook(hook, pos, h_in, h_out, h_sems):
    for at, where, fn in hook.steps:
        if where == pos:
            pl.when(pl.program_id(0) == at)(functools.partial(fn, h_in, h_out, h_sems))


def _hooked_call(body, hook, *, n_in, n_out, in_specs, out_specs, out_shape, scratch_shapes=(), **kw):
    nh = len(hook.ins)
    aliases = {n_in + i: n_out + o for i, o in hook.alias}
    return pl.pallas_call(
        body, in_specs=list(in_specs) + [HBM] * nh, out_specs=list(out_specs) + [HBM] * len(hook.out_shapes),
        out_shape=list(out_shape) + list(hook.out_shapes), scratch_shapes=list(scratch_shapes) + list(hook.sems),
        input_output_aliases=aliases, **kw)


def _nt(a, b):
    return lax.dot_general(a, b, (((1,), (1,)), ((), ())), preferred_element_type=F32)


def _tn(a, b):
    return lax.dot_general(a, b, (((0,), (0,)), ((), ())), preferred_element_type=F32)


def _nn(a, b):
    return jnp.dot(a, b, preferred_element_type=F32)


def _sigmoid(x):
    return 0.5 * jnp.tanh(0.5 * x) + 0.5


_GELU_C = float(np.sqrt(2.0 / np.pi))


def _gelu(x):
    t = jnp.tanh(_GELU_C * (x + 0.044715 * (x * x * x)))
    return x * (0.5 * (1.0 + t))


def _gelu_grad(x):
    t = jnp.tanh(_GELU_C * (x + 0.044715 * (x * x * x)))
    return 0.5 * (1.0 + t) + 0.5 * x * (1.0 - t * t) * (_GELU_C * (1.0 + 3.0 * 0.044715 * x * x))


def _rms_stats(xf):
    rstd = lax.rsqrt(jnp.mean(xf * xf, axis=-1, keepdims=True) + EPS)
    return rstd, xf * rstd


def _rms_bwd(xhat, rstd, g, dh):
    dxh = dh * g
    dx = rstd * (dxh - xhat * jnp.mean(dxh * xhat, axis=-1, keepdims=True))
    return dx, dh * xhat


def norm_mm(x, g, w, *, tm, scale, name, hook=Hook()):
    S, D = x.shape
    J, _, C = w.shape
    nh, nho = len(hook.ins), len(hook.out_shapes)

    def body(x_ref, g_ref, w_ref, *rest):
        h_in, (h_ref, z_ref), h_out, h_sems = rest[:nh], rest[nh:nh + 2], rest[nh + 2:nh + 2 + nho], rest[nh + 2 + nho:]
        _run_hook(hook, "before", h_in, h_out, h_sems)
        _, xhat = _rms_stats(x_ref[...])
        h = (xhat * g_ref[...]).astype(BF16)
        h_ref[...] = h
        for j in range(J):
            acc = _nn(h, w_ref[j])
            if scale is not None:
                acc = acc * scale
            z_ref[:, j * C:(j + 1) * C] = acc.astype(BF16)
        _run_hook(hook, "after", h_in, h_out, h_sems)

    return _hooked_call(
        body, hook, n_in=3, n_out=2, grid=(S // tm,), name=name,
        in_specs=[pl.BlockSpec((tm, D), lambda i: (i, 0)), _full((1, D)), _full((J, D, C))],
        out_specs=[pl.BlockSpec((tm, D), lambda i: (i, 0)), pl.BlockSpec((tm, J * C), lambda i: (i, 0))],
        out_shape=[jax.ShapeDtypeStruct((S, D), BF16), jax.ShapeDtypeStruct((S, J * C), BF16)],
        compiler_params=_params("arbitrary"),
    )(x, g, w, *hook.ins)


def _att_window_specs(nt, col):
    return [pl.BlockSpec((ATT_TILE, 512), lambda t, j=j: (jnp.clip(t - (ATT_WIN - 1) + j, 0, nt - 1), col))
            for j in range(ATT_WIN)]


ATT_SCALE = ATT_HEAD_DIM ** -0.5
ATT_FAR = 3 * ATT_TILE


def _att_scores(q_scaled, k, bias, valid):
    return jnp.where(valid, _nt(q_scaled, k) + bias, NEG_INF)


def _att_valid(t):
    kpos = lax.broadcasted_iota(jnp.int32, (ATT_TILE, ATT_WIN * ATT_TILE), 1) + (t - (ATT_WIN - 1)) * ATT_TILE
    return kpos >= 0


def attn_fwd(z, bias, hook=Hook()):
    S = z.shape[0]
    nt = S // ATT_TILE
    n_in, nh, nho = 2 + 2 * ATT_WIN, len(hook.ins), len(hook.out_shapes)

    def body(q_ref, *refs):
        k_refs, v_refs = refs[:ATT_WIN], refs[ATT_WIN:2 * ATT_WIN]
        bias_ref = refs[2 * ATT_WIN]
        rest = refs[2 * ATT_WIN + 1:]
        h_in, (y_ref, lse_ref), h_out = rest[:nh], rest[nh:nh + 2], rest[nh + 2:nh + 2 + nho]
        s_scr, p_scr = rest[nh + 2 + nho:nh + 4 + nho]
        h_sems = rest[nh + 4 + nho:]
        _run_hook(hook, "before", h_in, h_out, h_sems)
        valid = _att_valid(pl.program_id(0))
        heads = [slice(h * ATT_HEAD_DIM, (h + 1) * ATT_HEAD_DIM) for h in range(ATT_HEADS)]
        for h, sl in enumerate(heads):
            k = jnp.concatenate([r[:, sl] for r in k_refs], axis=0)
            s_scr[h] = _att_scores(q_ref[:, sl] * ATT_SCALE, k, bias_ref[h], valid)
        stats = []
        for h in range(ATT_HEADS):
            s = s_scr[h]
            m = jnp.max(s, axis=-1, keepdims=True)
            p = jnp.exp(s - m)
            stats.append((m, jnp.sum(p, axis=-1, keepdims=True)))
            p_scr[h] = p.astype(BF16)
        for h, sl in enumerate(heads):
            m, l = stats[h]
            v = jnp.concatenate([r[:, sl] for r in v_refs], axis=0)
            y_ref[:, sl] = (_nn(p_scr[h], v) / l).astype(BF16)
            lse_ref[:, h:h + 1] = m + jnp.log(l)
        _run_hook(hook, "after", h_in, h_out, h_sems)

    tile = lambda col: pl.BlockSpec((ATT_TILE, 512), lambda t: (t, col))
    return _hooked_call(
        body, hook, n_in=n_in, n_out=2, grid=(nt,), name="attn_fwd",
        in_specs=[tile(0)] + _att_window_specs(nt, 1) + _att_window_specs(nt, 2) + [_full(bias.shape)],
        out_specs=[tile(0), pl.BlockSpec((ATT_TILE, ATT_HEADS), lambda t: (t, 0))],
        out_shape=[jax.ShapeDtypeStruct((S, 512), BF16), jax.ShapeDtypeStruct((S, ATT_HEADS), F32)],
        scratch_shapes=[pltpu.VMEM((ATT_HEADS, ATT_TILE, ATT_WIN * ATT_TILE), F32),
                        pltpu.VMEM((ATT_HEADS, ATT_TILE, ATT_WIN * ATT_TILE), BF16)],
        compiler_params=_params("arbitrary"),
    )(z, *([z] * (2 * ATT_WIN)), bias, *hook.ins)


SG_PAIRS = SG_GROUPS // 2


def _group_mean(x, a_ref):
    rows = x.shape[0]
    hi = x.astype(BF16)
    lo = (x - hi.astype(F32)).astype(BF16)
    both = _nn(jnp.concatenate([hi, lo], axis=0), a_ref[...])
    return both[:rows] + both[rows:]


def _pair_diag(x, ref):
    first = lax.broadcasted_iota(jnp.int32, x.shape, 1) < SG_GROUP_DIM
    ref[0:SG_BLOCK, :] = jnp.where(first, x, 0.0).astype(BF16)
    ref[SG_BLOCK:2 * SG_BLOCK, :] = jnp.where(first, 0.0, x).astype(BF16)


def _sgu_norm(zv_ref, a_ref, lng_ref, lnb_ref):
    v = _gelu(zv_ref[...].astype(F32))
    vc = v - _group_mean(v, a_ref)
    rstd = lax.rsqrt(_group_mean(vc * vc, a_ref) + EPS)
    xhat = vc * rstd
    return xhat, rstd, xhat * lng_ref[...] + lnb_ref[...]


def _sgu_rows(b):
    return pl.ds(pl.multiple_of(b * SG_BLOCK, SG_BLOCK), SG_BLOCK)


def sgu_fwd(z, amean, lng, lnb, wpair, bfull, *, tm, hook=Hook()):
    S = z.shape[0]
    nh, nho = len(hook.ins), len(hook.out_shapes)

    def body(zu_ref, zv_ref, a_ref, lng_ref, lnb_ref, wpair_ref, bfull_ref, *rest):
        h_in, y_ref, h_out = rest[:nh], rest[nh], rest[nh + 1:nh + 1 + nho]
        vn_scr, vbd = rest[nh + 1 + nho:nh + 3 + nho]
        h_sems = rest[nh + 3 + nho:]
        _run_hook(hook, "before", h_in, h_out, h_sems)
        vn_scr[...] = _sgu_norm(zv_ref, a_ref, lng_ref, lnb_ref)[2]

        def block(b, carry):
            rows = _sgu_rows(b)
            u = _gelu(zu_ref[rows, :].astype(F32))
            for p in range(SG_PAIRS):
                lanes = slice(p * 128, (p + 1) * 128)
                _pair_diag(vn_scr[rows, lanes], vbd.at[p])
                sv = _nn(wpair_ref[p], vbd[p]) + bfull_ref[:, lanes]
                y_ref[rows, lanes] = (u[:, lanes] * sv).astype(BF16)
            return carry

        lax.fori_loop(0, tm // SG_BLOCK, block, 0, unroll=2)
        _run_hook(hook, "after", h_in, h_out, h_sems)

    tile = lambda col: pl.BlockSpec((tm, 512), lambda i: (i, col))
    smalls = [amean, lng, lnb, wpair, bfull]
    return _hooked_call(
        body, hook, n_in=2 + len(smalls), n_out=1, grid=(S // tm,), name="sgu_fwd",
        in_specs=[tile(3), tile(4)] + [_full(a.shape) for a in smalls],
        out_specs=[tile(0)], out_shape=[jax.ShapeDtypeStruct((S, 512), BF16)],
        scratch_shapes=[pltpu.VMEM((tm, 512), F32), pltpu.VMEM((SG_PAIRS, 2 * SG_BLOCK, 128), BF16)],
        compiler_params=_params("arbitrary"),
    )(z, z, *smalls, *hook.ins)


def _gate_specs(tm):
    return [pl.BlockSpec((tm, 512), lambda i, c=c: (i, c)) for c in (5, 6, 7, 8)]


def merge_fwd(y_att, y_sg, z, x, w_ba, w_bs, w_out, *, tm):
    S, D = x.shape
    J, _, C = w_ba.shape

    def body(ya_ref, ys_ref, g0, g1, g2, g3, x_ref, wba_ref, wbs_ref, wo_ref, a_ref, b_ref, m_ref, x1_ref):
        ya, ys = ya_ref[...], ys_ref[...]
        a = jnp.concatenate([_nn(ya, wba_ref[j]) for j in range(J)], axis=1)
        b = jnp.concatenate([_nn(ys, wbs_ref[j]) for j in range(J)], axis=1)
        sa = _sigmoid(jnp.concatenate([g0[...], g1[...]], axis=1).astype(F32))
        sb = _sigmoid(jnp.concatenate([g2[...], g3[...]], axis=1).astype(F32))
        a_ref[...] = a.astype(BF16)
        b_ref[...] = b.astype(BF16)
        merged = (sa * a + sb * b).astype(BF16)
        m_ref[...] = merged
        x1_ref[...] = x_ref[...] + _nn(merged, wo_ref[...])

    row = lambda n: pl.BlockSpec((tm, n), lambda i: (i, 0))
    return pl.pallas_call(
        body, grid=(S // tm,), name="merge_fwd",
        in_specs=[row(512), row(512)] + _gate_specs(tm) + [row(D), _full(w_ba.shape), _full(w_bs.shape), _full(w_out.shape)],
        out_specs=[row(D)] * 4,
        out_shape=[jax.ShapeDtypeStruct((S, D), BF16)] * 3 + [jax.ShapeDtypeStruct((S, D), F32)],
        compiler_params=_params("parallel"),
    )(y_att, y_sg, z, z, z, z, x, w_ba, w_bs, w_out)


def xattn_fwd(xq, kv, w_xo, x1, *, tm):
    S, D = xq.shape
    dh = XATT_HEAD_DIM

    def body(q_ref, kv_ref, wo_ref, x1_ref, o_ref, lse_ref, x2_ref):
        outs = []
        for h in range(XATT_HEADS):
            s = _nt(q_ref[:, h * dh:(h + 1) * dh], kv_ref[:, h * dh:(h + 1) * dh])
            m = jnp.max(s, axis=-1, keepdims=True)
            p = jnp.exp(s - m)
            l = jnp.sum(p, axis=-1, keepdims=True)
            outs.append((_nn(p.astype(BF16), kv_ref[:, D + h * dh:D + (h + 1) * dh]) / l).astype(BF16))
            lse_ref[:, h:h + 1] = m + jnp.log(l)
        o = jnp.concatenate(outs, axis=1)
        o_ref[...] = o
        x2_ref[...] = x1_ref[...] + _nn(o, wo_ref[...])

    row = lambda n: pl.BlockSpec((tm, n), lambda i: (i, 0))
    return pl.pallas_call(
        body, grid=(S // tm,), name="xattn_fwd",
        in_specs=[row(D), _full(kv.shape), _full(w_xo.shape), row(D)],
        out_specs=[row(D), row(XATT_HEADS), row(D)],
        out_shape=[jax.ShapeDtypeStruct((S, D), BF16), jax.ShapeDtypeStruct((S, XATT_HEADS), F32),
                   jax.ShapeDtypeStruct((S, D), F32)],
        compiler_params=_params("parallel"),
    )(xq, kv, w_xo, x1)


def ffn_out_loss(gu, w, x2, g, target, *, tm):
    S, D = x2.shape
    F = w.shape[0]

    def body(gu_ref, w_ref, x2_ref, g_ref, t_ref, act_ref, loss_ref, dx_ref, dg_ref):
        @pl.when(pl.program_id(0) == 0)
        def _():
            loss_ref[...] = jnp.zeros_like(loss_ref)
            dg_ref[...] = jnp.zeros_like(dg_ref)

        half = gu_ref[:, :F] * 0.5
        act = (half * gu_ref[:, F:]) * (jnp.tanh(half) + 1.0)
        act_ref[...] = act
        gv = g_ref[...]
        rstd, xhat = _rms_stats(x2_ref[...] + _nn(act, w_ref[...]))
        err = xhat * gv - t_ref[...]
        loss_ref[...] += 0.5 * jnp.sum(jnp.mean(err * err, axis=-1, keepdims=True))
        dx, dgc = _rms_bwd(xhat, rstd, gv, err * (1.0 / D))
        dx_ref[...] = dx
        dg_ref[...] += jnp.sum(dgc, axis=0, keepdims=True)

    row = lambda n: pl.BlockSpec((tm, n), lambda i: (i, 0))
    return pl.pallas_call(
        body, grid=(S // tm,), name="ffn_out_loss",
        in_specs=[row(2 * F), _full(w.shape), row(D), _full((1, D)), row(D)],
        out_specs=[row(F), _acc((1, 128)), row(D), _acc((1, D))],
        out_shape=[jax.ShapeDtypeStruct((S, F), BF16), jax.ShapeDtypeStruct((1, 128), F32),
                   jax.ShapeDtypeStruct((S, D), F32), jax.ShapeDtypeStruct((1, D), F32)],
        compiler_params=_params("arbitrary"),
    )(gu, w, x2, g, target)


def ffn_act_bwd(dx3, gu, w, *, tm, nchunk):
    S, D = dx3.shape
    F = w.shape[0]
    cn = F // nchunk

    def body(dx_ref, gu_ref, w_ref, dgu_ref):
        dxb = dx_ref[...].astype(BF16)
        for j in range(nchunk):
            dact = _nt(dxb, w_ref[j * cn:(j + 1) * cn, :])
            gate = gu_ref[:, j * cn:(j + 1) * cn]
            up = gu_ref[:, F + j * cn:F + (j + 1) * cn]
            sg = _sigmoid(gate)
            silu = gate * sg
            dsilu = sg + silu * (1.0 - sg)
            dgu_ref[:, j * cn:(j + 1) * cn] = (dact * (up * dsilu).astype(F32)).astype(BF16)
            dgu_ref[:, F + j * cn:F + (j + 1) * cn] = (dact * silu.astype(F32)).astype(BF16)

    row = lambda n: pl.BlockSpec((tm, n), lambda i: (i, 0))
    return pl.pallas_call(
        body, grid=(S // tm,), name="ffn_act_bwd",
        in_specs=[row(D), row(2 * F), _full(w.shape)], out_specs=row(2 * F),
        out_shape=jax.ShapeDtypeStruct((S, 2 * F), BF16), compiler_params=_params("parallel"),
    )(dx3, gu, w)


def _overlaps(widths, lo, hi):
    out, off = [], 0
    for p, wd in enumerate(widths):
        a, b = max(lo, off), min(hi, off + wd)
        if a < b:
            out.append((p, a - off, b - off, a - lo))
        off += wd
    return out


def tn_mm_pieces(a, pieces, *, shards, per, tk, name):
    K, M = a.shape
    widths = [p.shape[1] for p in pieces]
    C = sum(widths) // shards
    tn = per * C
    nk = K // tk
    n_tiles = shards // per

    def body(a_ref, *refs):
        p_refs, o_ref, acc_ref = refs[:len(pieces)], refs[len(pieces)], refs[len(pieces) + 1]
        n, k = pl.program_id(0), pl.program_id(1)

        @pl.when(k == 0)
        def _():
            acc_ref[...] = jnp.zeros_like(acc_ref)

        av = a_ref[...]
        for tile in range(n_tiles):
            @pl.when(n == tile)
            def _(tile=tile):
                for p, c0, c1, at in _overlaps(widths, tile * tn, (tile + 1) * tn):
                    acc_ref[:, at:at + c1 - c0] += _tn(av, p_refs[p][:, c0:c1])

        @pl.when(k == nk - 1)
        def _():
            for s in range(per):
                o_ref[s] = acc_ref[:, s * C:(s + 1) * C].astype(BF16)

    return pl.pallas_call(
        body, grid=(n_tiles, nk), name=name,
        in_specs=[pl.BlockSpec((tk, M), lambda n, k: (k, 0))] + [pl.BlockSpec((tk, wd), lambda n, k: (k, 0)) for wd in widths],
        out_specs=pl.BlockSpec((per, M, C), lambda n, k: (n, 0, 0)), out_shape=jax.ShapeDtypeStruct((shards, M, C), BF16),
        scratch_shapes=[pltpu.VMEM((M, tn), F32)], compiler_params=_params("parallel", "arbitrary"),
    )(a, *pieces)


def tn_mm(a, b, *, shards, tn, tk, name):
    K, M = a.shape
    N = b.shape[1]
    C = N // shards
    per = tn // C
    nk = K // tk

    def body(a_ref, b_ref, o_ref, acc_ref):
        k = pl.program_id(1)

        @pl.when(k == 0)
        def _():
            acc_ref[...] = jnp.zeros_like(acc_ref)

        acc_ref[...] += _tn(a_ref[...].astype(BF16), b_ref[...].astype(BF16))

        @pl.when(k == nk - 1)
        def _():
            if shards > 1:
                for s in range(per):
                    o_ref[s] = acc_ref[:, s * C:(s + 1) * C].astype(BF16)
            else:
                o_ref[...] = acc_ref[...].astype(BF16)

    if shards > 1:
        out_spec = pl.BlockSpec((per, M, C), lambda n, k: (n, 0, 0))
        out_shape = jax.ShapeDtypeStruct((shards, M, C), BF16)
    else:
        out_spec = pl.BlockSpec((M, tn), lambda n, k: (0, n))
        out_shape = jax.ShapeDtypeStruct((M, N), BF16)
    return pl.pallas_call(
        body, grid=(N // tn, nk), name=name,
        in_specs=[pl.BlockSpec((tk, M), lambda n, k: (k, 0)), pl.BlockSpec((tk, tn), lambda n, k: (k, n))],
        out_specs=out_spec, out_shape=out_shape, scratch_shapes=[pltpu.VMEM((M, tn), F32)],
        compiler_params=_params("parallel", "arbitrary"),
    )(a, b)


def mm_nt_norm_bwd(dy, w, x, g, dx_in, *, tm, name, hook=Hook()):
    S, D = x.shape
    J, _, C = w.shape
    has_in = dx_in is not None
    dys = list(dy) if isinstance(dy, (list, tuple)) else [dy]
    widths = [d.shape[1] for d in dys]
    nd = len(dys)
    n_in, nh, nho = nd + 3 + has_in, len(hook.ins), len(hook.out_shapes)

    def body(*refs):
        dy_refs = refs[:nd]
        w_ref, x_ref, g_ref = refs[nd:nd + 3]
        dxin_ref = refs[nd + 3] if has_in else None
        rest = refs[n_in:]
        h_in, (dx_ref, dg_ref), h_out, h_sems = rest[:nh], rest[nh:nh + 2], rest[nh + 2:nh + 2 + nho], rest[nh + 2 + nho:]
        _run_hook(hook, "before", h_in, h_out, h_sems)

        @pl.when(pl.program_id(0) == 0)
        def _():
            dg_ref[...] = jnp.zeros_like(dg_ref)

        dh = None
        for j in range(J):
            for p, c0, c1, at in _overlaps(widths, j * C, (j + 1) * C):
                part = _nt(dy_refs[p][:, c0:c1], w_ref[j, :, at:at + c1 - c0])
                dh = part if dh is None else dh + part
        rstd, xhat = _rms_stats(x_ref[...])
        dx, dgc = _rms_bwd(xhat, rstd, g_ref[...], dh)
        dx_ref[...] = dx + dxin_ref[...] if has_in else dx
        dg_ref[...] += jnp.sum(dgc, axis=0, keepdims=True)
        _run_hook(hook, "after", h_in, h_out, h_sems)

    row = lambda n: pl.BlockSpec((tm, n), lambda i: (i, 0))
    ins = dys + [w, x, g] + ([dx_in] if has_in else [])
    return _hooked_call(
        body, hook, n_in=n_in, n_out=2, grid=(S // tm,), name=name,
        in_specs=[row(wd) for wd in widths] + [_full(w.shape), row(D), _full((1, D))] + ([row(D)] if has_in else []),
        out_specs=[row(D), _acc((1, D))],
        out_shape=[jax.ShapeDtypeStruct((S, D), F32), jax.ShapeDtypeStruct((1, D), F32)],
        compiler_params=_params("arbitrary"),
    )(*ins, *hook.ins)


def xattn_bwd(dx2, xq, o, lse, kv, w_xo, *, tm):
    S, D = xq.shape
    M = kv.shape[0]
    dh = XATT_HEAD_DIM

    def body(dx_ref, q_ref, o_ref, lse_ref, kv_ref, wo_ref, dq_ref, dkv_ref):
        @pl.when(pl.program_id(0) == 0)
        def _():
            dkv_ref[...] = jnp.zeros_like(dkv_ref)

        do = _nt(dx_ref[...].astype(BF16), wo_ref[...])
        for h in range(XATT_HEADS):
            hs = slice(h * dh, (h + 1) * dh)
            vs = slice(D + h * dh, D + (h + 1) * dh)
            q, k, v = q_ref[:, hs], kv_ref[:, hs], kv_ref[:, vs]
            do_h = do[:, hs]
            do_b = do_h.astype(BF16)
            p = jnp.exp(_nt(q, k) - lse_ref[:, h:h + 1])
            delta = jnp.sum(do_h * o_ref[:, hs].astype(F32), axis=-1, keepdims=True)
            ds = (p * (_nt(do_b, v) - delta)).astype(BF16)
            dq_ref[:, hs] = (_nn(ds, k) * (dh ** -0.5)).astype(BF16)
            dkv_ref[:, hs] += _tn(ds, q)
            dkv_ref[:, vs] += _tn(p.astype(BF16), do_b)

    row = lambda n: pl.BlockSpec((tm, n), lambda i: (i, 0))
    return pl.pallas_call(
        body, grid=(S // tm,), name="xattn_bwd",
        in_specs=[row(D), row(D), row(D), row(XATT_HEADS), _full(kv.shape), _full(w_xo.shape)],
        out_specs=[row(D), _acc((M, 2 * D))],
        out_shape=[jax.ShapeDtypeStruct((S, D), BF16), jax.ShapeDtypeStruct((M, 2 * D), F32)],
        compiler_params=_params("arbitrary"),
    )(dx2, xq, o, lse, kv, w_xo)


def merge_bwd(dx1, w_out, a, b, z, w_ba, w_bs, *, tm, hook=Hook()):
    S, D = dx1.shape
    J, W, C = w_ba.shape
    nh, nho = len(hook.ins), len(hook.out_shapes)

    def body(dx_ref, wo_ref, a_ref, b_ref, g0, g1, g2, g3, wba_ref, wbs_ref, *rest):
        h_in, (da_ref, db_ref, dg_ref, dya_ref, dys_ref) = rest[:nh], rest[nh:nh + 5]
        h_out, h_sems = rest[nh + 5:nh + 5 + nho], rest[nh + 5 + nho:]
        _run_hook(hook, "before", h_in, h_out, h_sems)
        dm = _nt(dx_ref[...].astype(BF16), wo_ref[...])
        sa = _sigmoid(jnp.concatenate([g0[...], g1[...]], axis=1))
        sb = _sigmoid(jnp.concatenate([g2[...], g3[...]], axis=1))
        dg_ref[:, :D] = (dm * (a_ref[...] * (sa * (1.0 - sa))).astype(F32)).astype(BF16)
        dg_ref[:, D:] = (dm * (b_ref[...] * (sb * (1.0 - sb))).astype(F32)).astype(BF16)
        da = (dm * sa.astype(F32)).astype(BF16)
        db = (dm * sb.astype(F32)).astype(BF16)
        da_ref[...] = da
        db_ref[...] = db
        dya = _nt(da[:, 0:C], wba_ref[0])
        dys = _nt(db[:, 0:C], wbs_ref[0])
        for j in range(1, J):
            dya += _nt(da[:, j * C:(j + 1) * C], wba_ref[j])
            dys += _nt(db[:, j * C:(j + 1) * C], wbs_ref[j])
        dya_ref[...] = dya.astype(BF16)
        dys_ref[...] = dys.astype(BF16)
        _run_hook(hook, "after", h_in, h_out, h_sems)

    row = lambda n: pl.BlockSpec((tm, n), lambda i: (i, 0))
    return _hooked_call(
        body, hook, n_in=10, n_out=5, grid=(S // tm,), name="merge_bwd",
        in_specs=[row(D), _full(w_out.shape), row(D), row(D)] + _gate_specs(tm) + [_full(w_ba.shape), _full(w_bs.shape)],
        out_specs=[row(D), row(D), row(2 * D), row(W), row(W)],
        out_shape=[jax.ShapeDtypeStruct((S, D), BF16)] * 2 + [jax.ShapeDtypeStruct((S, 2 * D), BF16)]
        + [jax.ShapeDtypeStruct((S, W), BF16)] * 2,
        compiler_params=_params("arbitrary"),
    )(dx1, w_out, a, b, z, z, z, z, w_ba, w_bs, *hook.ins)


def attn_bwd(z, y, dy, lse, bias, hook=Hook()):
    S = z.shape[0]
    nt = S // ATT_TILE
    back = ATT_WIN - 1
    n_in, nh, nho = 5 + 2 * ATT_WIN, len(hook.ins), len(hook.out_shapes)

    def body(q_ref, *refs):
        k_refs, v_refs = refs[:ATT_WIN], refs[ATT_WIN:2 * ATT_WIN]
        y_ref, dy_ref, lse_ref, bias_ref = refs[2 * ATT_WIN:2 * ATT_WIN + 4]
        rest = refs[2 * ATT_WIN + 4:]
        h_in, (dqkv_ref, dbias_ref), h_out = rest[:nh], rest[nh:nh + 2], rest[nh + 2:nh + 2 + nho]
        dk_acc, dv_acc, dq_buf = rest[nh + 2 + nho:nh + 5 + nho]
        h_sems = rest[nh + 5 + nho:]
        t = pl.program_id(0)
        _run_hook(hook, "before", h_in, h_out, h_sems)

        @pl.when(t == 0)
        def _():
            dbias_ref[...] = jnp.zeros_like(dbias_ref)
            dk_acc[...] = jnp.zeros_like(dk_acc)
            dv_acc[...] = jnp.zeros_like(dv_acc)
            dq_buf[...] = jnp.zeros_like(dq_buf)

        def emit():
            done = (t + 1) % ATT_WIN
            dqkv_ref[:, 0:512] = dq_buf[done]
            dqkv_ref[:, 512:1024] = dk_acc[done].T.astype(BF16)
            dqkv_ref[:, 1024:1536] = dv_acc[done].T.astype(BF16)

        @pl.when(t < nt)
        def _():
            valid = _att_valid(t)
            q_t = (q_ref[...] * ATT_SCALE).astype(F32).T.astype(BF16)
            do_t = dy_ref[...].astype(F32).T.astype(BF16)
            for h in range(ATT_HEADS):
                sl = slice(h * ATT_HEAD_DIM, (h + 1) * ATT_HEAD_DIM)
                k = jnp.concatenate([r[:, sl] for r in k_refs], axis=0)
                v = jnp.concatenate([r[:, sl] for r in v_refs], axis=0)
                q, do_b = q_ref[:, sl] * ATT_SCALE, dy_ref[:, sl]
                p = jnp.exp(_att_scores(q, k, bias_ref[h], valid) - lse_ref[:, h:h + 1])
                delta = jnp.sum(do_b.astype(F32) * y_ref[:, sl].astype(F32), axis=-1, keepdims=True)
                ds = p * (_nt(do_b, v) - delta)
                dbias_ref[h, :, 0:ATT_TILE] += ds[:, 0:ATT_TILE] + ds[:, ATT_TILE:2 * ATT_TILE] + ds[:, 2 * ATT_TILE:ATT_FAR]
                dbias_ref[h, :, ATT_TILE:] += ds[:, ATT_FAR:]
                ds_b = ds.astype(BF16)
                dq_buf[t % ATT_WIN, :, sl] = (_nn(ds_b, k) * ATT_SCALE).astype(BF16)
                dk_w = _nn(q_t[sl, :], ds_b)
                dv_w = _nn(do_t[sl, :], p.astype(BF16))
                for i in range(ATT_WIN):
                    slot = (t + 1 + i) % ATT_WIN
                    cols = slice(i * ATT_TILE, (i + 1) * ATT_TILE)
                    if i == back:
                        dk_acc[slot, sl, :] = dk_w[:, cols]
                        dv_acc[slot, sl, :] = dv_w[:, cols]
                    else:
                        dk_acc[slot, sl, :] += dk_w[:, cols]
                        dv_acc[slot, sl, :] += dv_w[:, cols]
            emit()

        pl.when(t >= nt)(emit)
        _run_hook(hook, "after", h_in, h_out, h_sems)

    last = nt - 1
    dbias_shape = (ATT_HEADS, ATT_TILE, ATT_TILE + ATT_WIN * ATT_TILE - ATT_FAR)
    tile = lambda col, n=512: pl.BlockSpec((ATT_TILE, n), lambda t: (jnp.minimum(t, last), col))
    late = pl.BlockSpec((ATT_TILE, 1536), lambda t: (jnp.maximum(t - back, 0), 0))
    return _hooked_call(
        body, hook, n_in=n_in, n_out=2, grid=(nt + back,), name="attn_bwd",
        in_specs=[tile(0)] + _att_window_specs(nt, 1) + _att_window_specs(nt, 2)
        + [tile(0), tile(0), tile(0, ATT_HEADS), _full(bias.shape)],
        out_specs=[late, _acc(dbias_shape)],
        out_shape=[jax.ShapeDtypeStruct((S, 1536), BF16), jax.ShapeDtypeStruct(dbias_shape, F32)],
        scratch_shapes=[pltpu.VMEM((ATT_WIN, 512, ATT_TILE), F32)] * 2 + [pltpu.VMEM((ATT_WIN, ATT_TILE, 512), BF16)],
        compiler_params=_params("arbitrary"),
    )(z, *([z] * (2 * ATT_WIN)), y, dy, lse, bias, *hook.ins)


def sgu_bwd(z, dy, amean, lng, lnb, wpair, wtpair, bfull, maskpair, *, tm, hook=Hook()):
    S = z.shape[0]
    n = S // tm
    nh, nho = len(hook.ins), len(hook.out_shapes)

    def body(zu_ref, zv_ref, dy_ref, a_ref, lng_ref, lnb_ref, wpair_ref, wtpair_ref, bfull_ref, mask_ref, *rest):
        h_in, (duv_ref, dw_ref, dsgb_ref, dlng_ref, dlnb_ref), h_out = rest[:nh], rest[nh:nh + 5], rest[nh + 5:nh + 5 + nho]
        (xhat_scr, rstd_scr, vn_scr, dxh_scr, vbd, dbd, w_acc, b_acc, g_acc, s_acc) = rest[nh + 5 + nho:nh + 15 + nho]
        h_sems = rest[nh + 15 + nho:]
        i = pl.program_id(0)
        _run_hook(hook, "before", h_in, h_out, h_sems)

        @pl.when(i == 0)
        def _():
            for r in (w_acc, b_acc, g_acc, s_acc):
                r[...] = jnp.zeros_like(r)

        xhat_scr[...], rstd_scr[...], vn_scr[...] = _sgu_norm(zv_ref, a_ref, lng_ref, lnb_ref)

        def block(b, carry):
            rows = _sgu_rows(b)
            zu = zu_ref[rows, :].astype(F32)
            u, du = _gelu(zu), _gelu_grad(zu)
            dyv = dy_ref[rows, :].astype(F32)
            for p in range(SG_PAIRS):
                lanes = slice(p * 128, (p + 1) * 128)
                _pair_diag(vn_scr[rows, lanes], vbd.at[p])
                sv = _nn(wpair_ref[p], vbd[p]) + bfull_ref[:, lanes]
                duv_ref[rows, lanes] = (dyv[:, lanes] * sv * du[:, lanes]).astype(BF16)
                dsv = dyv[:, lanes] * u[:, lanes]
                b_acc[:, lanes] += dsv
                w_acc[p] += _nt(dsv.astype(BF16), vbd[p])
                _pair_diag(dsv, dbd.at[p])
                dvn = _nn(wtpair_ref[p], dbd[p])
                g_acc[:, lanes] += dvn * xhat_scr[rows, lanes]
                s_acc[:, lanes] += dvn
                dxh_scr[rows, lanes] = dvn * lng_ref[:, lanes]
            return carry

        lax.fori_loop(0, tm // SG_BLOCK, block, 0, unroll=2)
        dxh, xhat = dxh_scr[...], xhat_scr[...]
        dv = rstd_scr[...] * (dxh - _group_mean(dxh, a_ref) - xhat * _group_mean(dxh * xhat, a_ref))
        duv_ref[:, 512:1024] = (dv * _gelu_grad(zv_ref[...].astype(F32))).astype(BF16)

        @pl.when(i == n - 1)
        def _():
            dw_ref[...] = w_acc[...] * mask_ref[...]
            dsgb_ref[...] = _group_mean(b_acc[...], a_ref) * float(SG_GROUP_DIM)
            dlng_ref[...] = jnp.sum(g_acc[...], axis=0, keepdims=True)
            dlnb_ref[...] = jnp.sum(s_acc[...], axis=0, keepdims=True)

        _run_hook(hook, "after", h_in, h_out, h_sems)

    tile = lambda col: pl.BlockSpec((tm, 512), lambda i: (i, col))
    smalls = [amean, lng, lnb, wpair, wtpair, bfull, maskpair]
    pair_shape = (SG_PAIRS, SG_BLOCK, 2 * SG_BLOCK)
    return _hooked_call(
        body, hook, n_in=3 + len(smalls), n_out=5, grid=(n,), name="sgu_bwd",
        in_specs=[tile(3), tile(4), tile(0)] + [_full(a.shape) for a in smalls],
        out_specs=[pl.BlockSpec((tm, 1024), lambda i: (i, 0)), _acc(pair_shape), _acc((SG_BLOCK, 512)),
                   _acc((1, 512)), _acc((1, 512))],
        out_shape=[jax.ShapeDtypeStruct((S, 1024), BF16), jax.ShapeDtypeStruct(pair_shape, F32),
                   jax.ShapeDtypeStruct((SG_BLOCK, 512), F32), jax.ShapeDtypeStruct((1, 512), F32),
                   jax.ShapeDtypeStruct((1, 512), F32)],
        scratch_shapes=[pltpu.VMEM((tm, 512), F32)] * 4
        + [pltpu.VMEM((SG_PAIRS, 2 * SG_BLOCK, 128), BF16)] * 2
        + [pltpu.VMEM(pair_shape, F32)] + [pltpu.VMEM((SG_BLOCK, 512), F32)] * 3,
        compiler_params=_params("arbitrary"),
    )(z, z, dy, *smalls, *hook.ins)


def bias_colsum(p, far_part):
    H, _, L = p.shape
    near_lo = (ATT_WIN - 1) * ATT_TILE - REL_CLIP + 1
    near_hi = ATT_WIN * ATT_TILE

    def body(p_ref, f_ref, gw_ref, far_ref):
        k = lax.broadcasted_iota(jnp.int32, (1, L), 1)
        is_far = (k < near_lo) | (k >= near_hi)
        for h in range(H):
            g = jnp.sum(p_ref[h], axis=0, keepdims=True)
            gw_ref[h:h + 1, :] = g
            far_ref[h:h + 1, :] = jnp.zeros((1, 128), F32) + (jnp.sum(jnp.where(is_far, g, 0.0)) + jnp.sum(f_ref[h]))

    return pl.pallas_call(
        body, name="bias_colsum", in_specs=[_full(p.shape), _full(far_part.shape)],
        out_specs=[_acc((H, L)), _acc((H, 128))],
        out_shape=[jax.ShapeDtypeStruct((H, L), F32), jax.ShapeDtypeStruct((H, 128), F32)], grid=(1,),
        compiler_params=_params("arbitrary"),
    )(p, far_part)


def _row_tile(rows, cols):
    best = None
    for tr in range(16, rows + 1, 16):
        if rows % tr == 0 and tr * cols * 4 <= EW_BLOCK_BYTES:
            best = tr
    return best if best is not None else rows


def place_shard(w, own, name):
    R, C = w.shape
    r = R // 2
    tr = _row_tile(r, C)
    nr = r // tr

    def body(own_ref, w_ref, o_ref):
        o_ref[...] = w_ref[...].astype(BF16)

    return pl.pallas_call(
        body, name=name,
        grid_spec=pltpu.PrefetchScalarGridSpec(
            num_scalar_prefetch=1, grid=(2, nr),
            in_specs=[pl.BlockSpec((tr, C), lambda h, i, own: (h * nr + i, 0))],
            out_specs=pl.BlockSpec((None, None, tr, C), lambda h, i, own: (own[0], h, i, 0))),
        out_shape=jax.ShapeDtypeStruct((4, 2, r, C), BF16), compiler_params=_params("parallel", "parallel"),
    )(own, w)


def chip_partial(grad, recv, core, name):
    _, _, r, C = grad.shape
    tr = _row_tile(r, C)

    def body(core_ref, g_ref, r_ref, o_ref):
        o_ref[...] = (g_ref[...].astype(F32) + r_ref[...].astype(F32)).astype(BF16)

    spec = pl.BlockSpec((None, tr, C), lambda j, i, core: (j, i, 0))
    return pl.pallas_call(
        body, name=name,
        grid_spec=pltpu.PrefetchScalarGridSpec(
            num_scalar_prefetch=1, grid=(4, r // tr),
            in_specs=[pl.BlockSpec((None, None, tr, C), lambda j, i, core: (j, core[0], i, 0)), spec], out_specs=spec),
        out_shape=jax.ShapeDtypeStruct((4, r, C), BF16), compiler_params=_params("parallel", "parallel"),
    )(core, grad, recv)


def shard_sum(part, recv, own, core, name):
    _, r, C = part.shape
    tr = _row_tile(r, C)

    def body(own_ref, core_ref, p_ref, r0, r1, r2, o_ref):
        o_ref[...] = p_ref[...].astype(F32) + r0[...].astype(F32) + r1[...].astype(F32) + r2[...].astype(F32)

    return pl.pallas_call(
        body, name=name,
        grid_spec=pltpu.PrefetchScalarGridSpec(
            num_scalar_prefetch=2, grid=(r // tr,),
            in_specs=[pl.BlockSpec((None, tr, C), lambda i, own, core: (own[0], i, 0))]
            + [pl.BlockSpec((None, tr, C), lambda i, own, core, k=k: (k, i, 0)) for k in range(3)],
            out_specs=pl.BlockSpec((None, tr, C), lambda i, own, core: (core[0], i, 0))),
        out_shape=jax.ShapeDtypeStruct((2, r, C), F32), compiler_params=_params("parallel"),
    )(own, core, part, recv, recv, recv)


def adamw(w, g, m, v, name):
    R, C = w.shape
    tr = _row_tile(R, C)

    def body(w_ref, g_ref, m_ref, v_ref, d_ref, nm_ref, nv_ref):
        gv = g_ref[...]
        nm = ADAM_B1 * m_ref[...] + (1.0 - ADAM_B1) * gv
        nv = ADAM_B2 * v_ref[...] + (1.0 - ADAM_B2) * (gv * gv)
        m_hat = nm / (1.0 - ADAM_B1 ** ADAM_STEP)
        v_hat = nv / (1.0 - ADAM_B2 ** ADAM_STEP)
        d_ref[...] = -ADAM_LR * (m_hat / (jnp.sqrt(v_hat) + ADAM_EPS) + ADAM_WD * w_ref[...])
        nm_ref[...] = nm
        nv_ref[...] = nv

    spec = pl.BlockSpec((tr, C), lambda i: (i, 0))
    return pl.pallas_call(
        body, grid=(R // tr,), name=name, in_specs=[spec] * 4, out_specs=[spec] * 3,
        out_shape=[jax.ShapeDtypeStruct((R, C), F32)] * 3, compiler_params=_params("parallel"),
    )(w, g, m, v)


HBM = pl.BlockSpec(memory_space=pl.ANY)


def _place():
    x, y, c = lax.axis_index("x"), lax.axis_index("y"), lax.axis_index("c")
    chips = [(1 - x, y), (x, 1 - y), (1 - x, 1 - y)]
    return x, y, c, chips


def _gather_phases(n):
    def copies(buf, sems, kind):
        send_sems, recv_sems = sems
        x, y, c, chips = _place()

        def remote(w, k, slot, to):
            return pltpu.make_async_remote_copy(src_ref=slot, dst_ref=slot, send_sem=send_sems.at[w, k],
                                                recv_sem=recv_sems.at[w, k], device_id=to, device_id_type=MESH)

        def one(w, k, px, py):
            if kind == "mine":
                return remote(w, k, buf[w].at[2 * x + y, c], (px, py, c))
            if kind == "theirs":
                return remote(w, k, buf[w].at[2 * px + py, c], (px, py, c))
            if kind == "onward":
                return remote(w, 3 + k, buf[w].at[2 * px + py, c], (x, y, 1 - c))
            return remote(w, 3 + k, buf[w].at[2 * px + py, 1 - c], (x, y, 1 - c))

        return [one(w, k, px, py) for w in range(n) for k, (px, py) in enumerate(chips)]

    def start(_, buf, sems):
        for cp in copies(buf, sems, "mine"):
            cp.start()

    def relay(_, buf, sems):
        for theirs, onward in zip(copies(buf, sems, "theirs"), copies(buf, sems, "onward")):
            theirs.wait_recv()
            onward.start()

    def finish(_, buf, sems):
        for cp in copies(buf, sems, "relayed"):
            cp.wait_recv()
        for cp in copies(buf, sems, "mine") + copies(buf, sems, "onward"):
            cp.wait_send()

    return start, relay, finish


def _gather_sems(n):
    return (pltpu.SemaphoreType.DMA((n, 6)), pltpu.SemaphoreType.DMA((n, 6)))


def gather_weights(bufs):
    n = len(bufs)
    phases = _gather_phases(n)

    def body(*refs):
        for phase in phases:
            phase(None, refs[n:2 * n], refs[2 * n:])

    return pl.pallas_call(
        body, name="gather_weights", in_specs=[HBM] * n, out_specs=[HBM] * n,
        out_shape=[jax.ShapeDtypeStruct(b.shape, b.dtype) for b in bufs],
        input_output_aliases={i: i for i in range(n)}, scratch_shapes=list(_gather_sems(n)),
    )(*bufs)


def gather_hook(bufs, n_steps):
    start, relay, finish = _gather_phases(len(bufs))
    return Hook(ins=tuple(bufs), out_shapes=tuple(jax.ShapeDtypeStruct(b.shape, b.dtype) for b in bufs),
                alias=tuple((i, i) for i in range(len(bufs))),
                sems=_gather_sems(len(bufs)),
                steps=((0, "before", start), ((n_steps * 13) // 16, "before", relay), (n_steps - 1, "after", finish)))


def sibling_split(grads, name):
    n = len(grads)

    def body(*refs):
        src, dst = refs[:n], refs[n:2 * n]
        send_sems, recv_sems = refs[2 * n:]
        x, y, c, _ = _place()
        sends = [pltpu.make_async_remote_copy(src_ref=src[w].at[:, 1 - c], dst_ref=dst[w], send_sem=send_sems.at[w],
                                              recv_sem=recv_sems.at[w], device_id=(x, y, 1 - c), device_id_type=MESH)
                 for w in range(n)]
        for cp in sends:
            cp.start()
        for cp in sends:
            cp.wait()

    return pl.pallas_call(
        body, name=name, in_specs=[HBM] * n, out_specs=[HBM] * n,
        out_shape=[jax.ShapeDtypeStruct((4,) + g.shape[2:], g.dtype) for g in grads],
        scratch_shapes=[pltpu.SemaphoreType.DMA((n,))] * 2,
    )(*grads)


def split_hook(grads, n_steps):
    n = len(grads)

    def copies(src, dst, sems):
        send_sems, recv_sems = sems
        x, y, c, _ = _place()
        return [pltpu.make_async_remote_copy(src_ref=src[w].at[:, 1 - c], dst_ref=dst[w], send_sem=send_sems.at[w],
                                             recv_sem=recv_sems.at[w], device_id=(x, y, 1 - c), device_id_type=MESH)
                for w in range(n)]

    def start(src, dst, sems):
        for cp in copies(src, dst, sems):
            cp.start()

    def finish(src, dst, sems):
        for cp in copies(src, dst, sems):
            cp.wait()

    return Hook(ins=tuple(grads), out_shapes=tuple(jax.ShapeDtypeStruct((4,) + g.shape[2:], g.dtype) for g in grads),
                sems=(pltpu.SemaphoreType.DMA((n,)), pltpu.SemaphoreType.DMA((n,))),
                steps=((0, "before", start), (n_steps - 1, "after", finish)))


def exchange_hook(parts, n_steps):
    n = len(parts)

    def copies(src, dst, sems):
        send_sems, recv_sems = sems
        _, _, c, chips = _place()
        return [pltpu.make_async_remote_copy(
            src_ref=src[w].at[2 * px + py], dst_ref=dst[w].at[k], send_sem=send_sems.at[w, k],
            recv_sem=recv_sems.at[w, k], device_id=(px, py, c), device_id_type=MESH)
            for w in range(n) for k, (px, py) in enumerate(chips)]

    def start(src, dst, sems):
        for cp in copies(src, dst, sems):
            cp.start()

    def finish(src, dst, sems):
        for cp in copies(src, dst, sems):
            cp.wait()

    return Hook(ins=tuple(parts), out_shapes=tuple(jax.ShapeDtypeStruct((3,) + p.shape[1:], p.dtype) for p in parts),
                sems=(pltpu.SemaphoreType.DMA((n, 3)), pltpu.SemaphoreType.DMA((n, 3))),
                steps=((0, "before", start), (n_steps - 1, "after", finish)))


def sibling_join(sums):
    n = len(sums)

    def body(*refs):
        buf = refs[n:2 * n]
        send_sems, recv_sems = refs[2 * n:]
        x, y, c, _ = _place()
        sends = [pltpu.make_async_remote_copy(src_ref=buf[w].at[c], dst_ref=buf[w].at[c], send_sem=send_sems.at[w],
                                              recv_sem=recv_sems.at[w], device_id=(x, y, 1 - c), device_id_type=MESH)
                 for w in range(n)]
        for cp in sends:
            cp.start()
        for w, cp in enumerate(sends):
            cp.wait_send()
            pltpu.make_async_remote_copy(src_ref=buf[w].at[c], dst_ref=buf[w].at[1 - c], send_sem=send_sems.at[w],
                                         recv_sem=recv_sems.at[w], device_id=(x, y, 1 - c), device_id_type=MESH).wait_recv()

    return pl.pallas_call(
        body, name="sibling_join", in_specs=[HBM] * n, out_specs=[HBM] * n,
        out_shape=[jax.ShapeDtypeStruct(s.shape, s.dtype) for s in sums],
        input_output_aliases={i: i for i in range(n)},
        scratch_shapes=[pltpu.SemaphoreType.DMA((n,))] * 2,
    )(*sums)


def join_hook(sums, n_steps):
    n = len(sums)

    def copies(buf, sems, recv):
        send_sems, recv_sems = sems
        x, y, c, _ = _place()
        half = (1 - c) if recv else c
        return [pltpu.make_async_remote_copy(src_ref=buf[w].at[half], dst_ref=buf[w].at[half], send_sem=send_sems.at[w],
                                             recv_sem=recv_sems.at[w], device_id=(x, y, 1 - c), device_id_type=MESH)
                for w in range(n)]

    def start(_, buf, sems):
        for cp in copies(buf, sems, False):
            cp.start()

    def finish(_, buf, sems):
        for cp in copies(buf, sems, False):
            cp.wait_send()
        for cp in copies(buf, sems, True):
            cp.wait_recv()

    return Hook(ins=tuple(sums), out_shapes=tuple(jax.ShapeDtypeStruct(s.shape, s.dtype) for s in sums),
                alias=tuple((i, i) for i in range(n)),
                sems=(pltpu.SemaphoreType.DMA((n,)), pltpu.SemaphoreType.DMA((n,))),
                steps=((0, "before", start), (n_steps - 1, "after", finish)))


def allreduce_small(v):
    R, C = v.shape

    def body(v_ref, out_ref, all_ref, send_sems, recv_sems):
        x, y, c, chips = _place()
        me, sibling = (x, y, c), (x, y, 1 - c)

        def slot(px, py, pc):
            return all_ref.at[4 * px + 2 * py + pc]

        def copy(k, block, to, src=None):
            return pltpu.make_async_remote_copy(src_ref=slot(*block) if src is None else src, dst_ref=slot(*block),
                                                send_sem=send_sems.at[k], recv_sem=recv_sems.at[k], device_id=to,
                                                device_id_type=MESH)

        first = [copy(0, me, sibling, src=v_ref)] + [copy(1 + j, me, (*chip, c), src=v_ref) for j, chip in enumerate(chips)]
        for cp in first:
            cp.start()
        slot(*me)[...] = v_ref[...]
        passed = [copy(4 + j, (*chip, c), sibling) for j, chip in enumerate(chips)]
        for j, chip in enumerate(chips):
            copy(1 + j, (*chip, c), me).wait_recv()
            passed[j].start()
        copy(0, sibling, me).wait_recv()
        for j, chip in enumerate(chips):
            copy(4 + j, (*chip, 1 - c), me).wait_recv()
        for cp in first + passed:
            cp.wait_send()
        acc = all_ref[0]
        for k in range(1, 8):
            acc = acc + all_ref[k]
        out_ref[...] = acc

    vmem = pl.BlockSpec(memory_space=pltpu.VMEM)
    return pl.pallas_call(
        body, name="allreduce_small", in_specs=[vmem], out_specs=vmem, out_shape=jax.ShapeDtypeStruct((R, C), F32),
        scratch_shapes=[pltpu.VMEM((8, R, C), F32), pltpu.SemaphoreType.DMA((7,)), pltpu.SemaphoreType.DMA((7,))],
        compiler_params=pltpu.CompilerParams(vmem_limit_bytes=VMEM_LIMIT_V7X),
    )(v)


def allgather_hook(v, n_steps):
    R, C = v.shape

    def copies(v_ref, out, sems, kind):
        send_sems, recv_sems, _ = sems
        x, y, c, chips = _place()
        me, sibling = (x, y, c), (x, y, 1 - c)

        def slot(px, py, pc):
            return out.at[4 * px + 2 * py + pc]

        def copy(k, block, to, src=None):
            return pltpu.make_async_remote_copy(src_ref=slot(*block) if src is None else src, dst_ref=slot(*block),
                                                send_sem=send_sems.at[k], recv_sem=recv_sems.at[k], device_id=to,
                                                device_id_type=MESH)

        if kind == "mine":
            return [copy(0, me, sibling, src=v_ref)] + [copy(1 + j, me, (*chip, c), src=v_ref) for j, chip in enumerate(chips)]
        if kind == "theirs":
            return [copy(1 + j, (*chip, c), me) for j, chip in enumerate(chips)]
        if kind == "onward":
            return [copy(4 + j, (*chip, c), sibling) for j, chip in enumerate(chips)]
        return [copy(0, sibling, me)] + [copy(4 + j, (*chip, 1 - c), me) for j, chip in enumerate(chips)]

    def own(v_ref, out, sems):
        x, y, c, _ = _place()
        return pltpu.make_async_copy(v_ref, out.at[4 * x + 2 * y + c], sems[2])

    def start(ins, outs, sems):
        own(ins[0], outs[0], sems).start()
        for cp in copies(ins[0], outs[0], sems, "mine"):
            cp.start()

    def relay(ins, outs, sems):
        for theirs, onward in zip(copies(ins[0], outs[0], sems, "theirs"), copies(ins[0], outs[0], sems, "onward")):
            theirs.wait_recv()
            onward.start()

    def finish(ins, outs, sems):
        for cp in copies(ins[0], outs[0], sems, "relayed"):
            cp.wait_recv()
        for cp in copies(ins[0], outs[0], sems, "mine") + copies(ins[0], outs[0], sems, "onward"):
            cp.wait_send()
        own(ins[0], outs[0], sems).wait()

    return Hook(ins=(v,), out_shapes=(jax.ShapeDtypeStruct((8, R, C), F32),),
                sems=(pltpu.SemaphoreType.DMA((7,)), pltpu.SemaphoreType.DMA((7,)), pltpu.SemaphoreType.DMA(())),
                steps=((0, "before", start), ((n_steps * 13) // 16, "before", relay), (n_steps - 1, "after", finish)))


def merge_hooks(a, b):
    cut = (len(a.ins), len(a.out_shapes), len(a.sems))

    def part(fn, second):
        def run(ins, outs, sems):
            if second:
                fn(ins[cut[0]:], outs[cut[1]:], sems[cut[2]:])
            else:
                fn(ins[:cut[0]], outs[:cut[1]], sems[:cut[2]])
        return run

    steps = tuple((s, w, part(fn, False)) for s, w, fn in a.steps) + tuple((s, w, part(fn, True)) for s, w, fn in b.steps)
    alias = a.alias + tuple((i + cut[0], o + cut[1]) for i, o in b.alias)
    return Hook(a.ins + b.ins, a.out_shapes + b.out_shapes, alias, a.sems + b.sems, steps)


def sum_blocks(g):
    n, R, C = g.shape

    def body(g_ref, o_ref):
        acc = g_ref[0]
        for k in range(1, n):
            acc = acc + g_ref[k]
        o_ref[...] = acc

    return pl.pallas_call(body, name="sum_blocks", grid=(1,), in_specs=[_full(g.shape)], out_specs=_acc((R, C)),
                          out_shape=jax.ShapeDtypeStruct((R, C), F32), compiler_params=_params("arbitrary"))(g)


SMALL_PAD = 1024


def _pack_small(arrs):
    parts = []
    for a in arrs:
        f = a.reshape(-1).astype(F32)
        parts.append(jnp.pad(f, (0, (-f.shape[0]) % SMALL_PAD)))
    return jnp.concatenate(parts).reshape(-1, 128)


def _unpack_small(packed, shapes):
    flat = packed.reshape(-1)
    outs, off = [], 0
    for s in shapes:
        size = int(np.prod(s))
        outs.append(flat[off:off + size].reshape(s))
        off += size + (-size) % SMALL_PAD
    return outs


ATT_KEYS = ATT_WIN * ATT_TILE
ATT_NEAR_LO = (ATT_WIN - 1) * ATT_TILE - REL_CLIP + 1
ATT_PERIOD = ATT_KEYS + ATT_TILE + 1


def _att_bias_table(rel_bias):
    far = rel_bias[:, 2 * REL_CLIP:]
    near = rel_bias[:, 2 * REL_CLIP - 1:0:-1]
    w = jnp.concatenate([jnp.broadcast_to(far, (ATT_HEADS, ATT_NEAR_LO)), near,
                         jnp.broadcast_to(far, (ATT_HEADS, ATT_PERIOD - ATT_KEYS))], axis=1)
    rows = jnp.tile(w, (1, ATT_TILE))[:, :ATT_TILE * (ATT_PERIOD - 1)].reshape(ATT_HEADS, ATT_TILE, ATT_PERIOD - 1)
    i = np.arange(ATT_TILE)[:, None]
    m = np.arange(ATT_KEYS)[None, :]
    qc, kc = i // CHUNK, m // CHUNK
    band = (kc >= qc) & (kc <= qc + N_PREV_CHUNKS)
    return jnp.where(band[None], rows[:, :, :ATT_KEYS], NEG_INF)


def _rel_bias_grad(dbias):
    p = jnp.pad(dbias[:, :, ATT_TILE:], ((0, 0), (0, 0), (ATT_FAR, ATT_PERIOD - 1 - ATT_KEYS))).reshape(ATT_HEADS, -1)
    p = jnp.pad(p, ((0, 0), (0, ATT_TILE))).reshape(ATT_HEADS, ATT_TILE, ATT_PERIOD)
    gw, far = bias_colsum(p, dbias[:, :, :ATT_TILE])
    return jnp.concatenate([jnp.zeros((ATT_HEADS, 1), F32), gw[:, ATT_KEYS - 1:ATT_NEAR_LO - 1:-1], far[:, :1]], axis=1)


def kernel(x, mem, norm_mix_g, w_in, rel_bias, sg_ln_g, sg_ln_b, sg_w, sg_b, w_branch_att, w_branch_sg, w_out, norm_xattn_g, norm_mem_g, w_xq, w_xkv, w_xo, norm_ffn_g, w_ffn_in, w_ffn_out, norm_final_g, loss_target, m_norm_mix_g, m_w_in, m_rel_bias, m_sg_ln_g, m_sg_ln_b, m_sg_w, m_sg_b, m_w_branch_att, m_w_branch_sg, m_w_out, m_norm_xattn_g, m_norm_mem_g, m_w_xq, m_w_xkv, m_w_xo, m_norm_ffn_g, m_w_ffn_in, m_w_ffn_out, m_norm_final_g, v_norm_mix_g, v_w_in, v_rel_bias, v_sg_ln_g, v_sg_ln_b, v_sg_w, v_sg_b, v_w_branch_att, v_w_branch_sg, v_w_out, v_norm_xattn_g, v_norm_mem_g, v_w_xq, v_w_xkv, v_w_xo, v_norm_ffn_g, v_w_ffn_in, v_w_ffn_out, v_norm_final_g):
    S, D = x.shape[1], x.shape[2]
    x2d, mem2d, tgt = x[0], mem[0], loss_target[0]

    big_names = ["w_in", "w_branch_att", "w_branch_sg", "w_out", "w_xq", "w_xkv", "w_xo", "w_ffn_in", "w_ffn_out"]
    col_sharded = [True, True, True, False, False, True, False, True, False]
    big_w = [a[0] for a in (w_in, w_branch_att, w_branch_sg, w_out, w_xq, w_xkv, w_xo, w_ffn_in, w_ffn_out)]
    big_m = [a[0] for a in (m_w_in, m_w_branch_att, m_w_branch_sg, m_w_out, m_w_xq, m_w_xkv, m_w_xo, m_w_ffn_in, m_w_ffn_out)]
    big_v = [a[0] for a in (v_w_in, v_w_branch_att, v_w_branch_sg, v_w_out, v_w_xq, v_w_xkv, v_w_xo, v_w_ffn_in, v_w_ffn_out)]

    own = (2 * lax.axis_index("x") + lax.axis_index("y")).astype(jnp.int32).reshape(1)
    core = lax.axis_index("c").astype(jnp.int32).reshape(1)
    placed = [place_shard(w, own, "place_" + nm) for w, nm in zip(big_w, big_names)]

    def whole(g4, i):
        R, C = big_w[i].shape
        return g4.reshape(4, R, C) if col_sharded[i] else g4.reshape(4 * R, C)

    W_in = whole(gather_weights(placed[:1])[0], 0)

    g_mix, g_xat, g_mem, g_ffn = norm_mix_g, norm_xattn_g, norm_mem_g, norm_ffn_g
    g_fin = norm_final_g.reshape(1, D)
    bias = _att_bias_table(rel_bias[0])
    tt = np.arange(SG_BLOCK)
    sg_mask = (tt[None, :] // CHUNK) <= (tt[:, None] // CHUNK)
    wm_f = jnp.where(sg_mask[None], sg_w[0], 0.0)
    pairs = wm_f.reshape(SG_PAIRS, 2, SG_BLOCK, SG_BLOCK)
    wpair = jnp.transpose(pairs, (0, 2, 1, 3)).reshape(SG_PAIRS, SG_BLOCK, 2 * SG_BLOCK).astype(BF16)
    wtpair = jnp.transpose(pairs, (0, 3, 1, 2)).reshape(SG_PAIRS, SG_BLOCK, 2 * SG_BLOCK).astype(BF16)
    maskpair = jnp.asarray(np.tile(sg_mask, (SG_PAIRS, 1, 2)), F32)
    bfull = jnp.repeat(sg_b[0].T, SG_GROUP_DIM, axis=1)
    lng, lnb = sg_ln_g[0].reshape(1, 512), sg_ln_b[0].reshape(1, 512)
    gid = np.arange(512) // SG_GROUP_DIM
    amean = jnp.asarray((gid[:, None] == gid[None, :]) / SG_GROUP_DIM, BF16)

    h1, z, *early = norm_mm(x2d, g_mix, W_in, tm=ROW_TILE, scale=None, name="norm_mm_in",
                            hook=gather_hook(placed[1:7], S // ROW_TILE))
    y_att, lse_att, *late = attn_fwd(z, bias, gather_hook(placed[7:], S // ATT_TILE))
    W_ba, W_bs, W_out, W_xq, W_xkv, W_xo, W_fi, W_fo = [whole(g4, i + 1) for i, g4 in enumerate(early + late)]
    y_sg, = sgu_fwd(z, amean, lng, lnb, wpair, bfull, tm=SGU_TILE)
    a_br, b_br, merged, x1 = merge_fwd(y_att, y_sg, z, x2d, W_ba, W_bs, W_out, tm=ROW_TILE)
    h2, xq = norm_mm(x1, g_xat, W_xq.reshape(1, D, D), tm=XATT_TILE, scale=XATT_HEAD_DIM ** -0.5, name="norm_mm_xq")
    mn, kv = norm_mm(mem2d, g_mem, W_xkv, tm=mem2d.shape[0], scale=None, name="norm_mm_kv")
    o_x, lse_x, x2 = xattn_fwd(xq, kv, W_xo, x1, tm=XATT_TILE)
    h3, gu = norm_mm(x2, g_ffn, W_fi, tm=ROW_TILE, scale=None, name="norm_mm_ffn")
    act, loss_vec, dx3, dg_fin = ffn_out_loss(gu, W_fo, x2, g_fin, tgt, tm=ROW_TILE)

    dgu = ffn_act_bwd(dx3, gu, W_fo, tm=ROW_TILE, nchunk=2)
    gW_fo = tn_mm(act, dx3, shards=1, tn=D, tk=DW_TOKENS, name="dw_ffn_out")
    gW_fi = tn_mm(h3, dgu, shards=4, tn=2 * W_fi.shape[2], tk=DW_TOKENS, name="dw_ffn_in")
    dx2, dg_ffn = mm_nt_norm_bwd(dgu, W_fi, x2, g_ffn, dx3, tm=ROW_TILE, name="dx_ffn")
    dq_x, dkv = xattn_bwd(dx2, xq, o_x, lse_x, kv, W_xo, tm=XATT_TILE)
    tk_wide = min(2 * DW_TOKENS, S)
    gW_xo = tn_mm(o_x, dx2, shards=1, tn=D, tk=tk_wide, name="dw_xo")
    gW_xq = tn_mm(h2, dq_x, shards=1, tn=D, tk=tk_wide, name="dw_xq")
    dx1, dg_xat = mm_nt_norm_bwd(dq_x, W_xq.reshape(1, D, D), x1, g_xat, dx2, tm=XATT_TILE, name="dx_xq")
    gW_xkv = tn_mm(mn, dkv, shards=4, tn=2 * D, tk=mem2d.shape[0], name="dw_xkv")
    _, dg_mem = mm_nt_norm_bwd(dkv.astype(BF16), W_xkv, mem2d, g_mem, None, tm=mem2d.shape[0], name="dx_mem")
    def canon(g, i):
        R, C = big_w[i].shape
        return g.reshape(4, 2, R // 2, C)

    def partials(gs, recv, idx):
        return [chip_partial(g, r, core, "chip_partial_" + big_names[i]) for g, r, i in zip(gs, recv, idx)]

    early_g = [canon(g, i) for g, i in zip([gW_xq, gW_xkv, gW_xo, gW_fi, gW_fo], range(4, 9))]
    da, db, dgab, dy_att, dy_sg, *early_r = merge_bwd(dx1, W_out, a_br, b_br, z, W_ba, W_bs, tm=ROW_TILE,
                                                      hook=split_hook(early_g, S // ROW_TILE))
    gW_out = tn_mm(merged, dx1, shards=1, tn=D, tk=tk_wide, name="dw_out")
    gW_ba = tn_mm(y_att, da, shards=4, tn=D, tk=tk_wide, name="dw_branch_att")
    gW_bs = tn_mm(y_sg, db, shards=4, tn=D, tk=tk_wide, name="dw_branch_sg")
    mid_g = [canon(g, i) for g, i in zip([gW_ba, gW_bs, gW_out], range(1, 4))]
    parts_b = (partials(mid_g, sibling_split(mid_g, "sibling_split_mid"), range(1, 4))
               + partials(early_g, early_r, range(4, 9)))
    nt_bwd = S // ATT_TILE + ATT_WIN - 1
    dqkv, dbias, *from_chips_b = attn_bwd(z, y_att, dy_att, lse_att, bias, exchange_hook(parts_b, nt_bwd))
    sums_b = [shard_sum(p, r, own, core, "shard_sum_" + nm) for p, r, nm in zip(parts_b, from_chips_b, big_names[1:])]
    duv, dwpair, dsgb_full, dlng, dlnb, *joined_b = sgu_bwd(z, dy_sg, amean, lng, lnb, wpair, wtpair, bfull, maskpair,
                                                         tm=SGU_TILE, hook=join_hook(sums_b, S // SGU_TILE))
    dwm = jnp.transpose(dwpair.reshape(SG_PAIRS, SG_BLOCK, 2, SG_BLOCK), (0, 2, 1, 3))
    dsgb = dsgb_full[:, ::SG_GROUP_DIM].T
    dz = [dqkv, duv, dgab]
    gW_in = tn_mm_pieces(h1, dz, shards=4, per=2, tk=DW_TOKENS, name="dw_in")
    in_g = [canon(gW_in, 0)]
    parts_a = partials(in_g, sibling_split(in_g, "sibling_split_w_in"), [0])
    d_rel = _rel_bias_grad(dbias)
    small_g = [d_rel, dlng, dlnb, dwm, dsgb, dg_xat, dg_mem, dg_ffn, dg_fin, loss_vec[0, :1]]
    n_dx = S // ROW_TILE
    dx, dg_mix, from_chips_a, small_all = mm_nt_norm_bwd(
        dz, W_in, x2d, g_mix, dx1, tm=ROW_TILE, name="dx_in",
        hook=merge_hooks(exchange_hook(parts_a, n_dx), allgather_hook(_pack_small(small_g), n_dx)))

    joined = list(sibling_join([shard_sum(parts_a[0], from_chips_a, own, core, "shard_sum_w_in")])) + list(joined_b)
    big_out = []
    for j, w, m_, v_, nm in zip(joined, big_w, big_m, big_v, big_names):
        g = j.reshape(w.shape)
        big_out.append((g,) + tuple(adamw(w, g, m_, v_, "adamw_" + nm)))

    small_w = [norm_mix_g, rel_bias, sg_ln_g, sg_ln_b, sg_w, sg_b, norm_xattn_g, norm_mem_g, norm_ffn_g, norm_final_g]
    small_m = [m_norm_mix_g, m_rel_bias, m_sg_ln_g, m_sg_ln_b, m_sg_w, m_sg_b, m_norm_xattn_g, m_norm_mem_g, m_norm_ffn_g, m_norm_final_g]
    small_v = [v_norm_mix_g, v_rel_bias, v_sg_ln_g, v_sg_ln_b, v_sg_w, v_sg_b, v_norm_xattn_g, v_norm_mem_g, v_norm_ffn_g, v_norm_final_g]
    shapes = [w.shape for w in small_w]
    g_sum = jnp.concatenate([allreduce_small(_pack_small([dg_mix])), sum_blocks(small_all)], axis=0)
    zero = jnp.zeros((1,), F32)
    d_s, m_s, v_s = adamw(_pack_small(small_w + [zero]), g_sum, _pack_small(small_m + [zero]), _pack_small(small_v + [zero]),
                          "adamw_small")
    sg_, sd, sm, sv_ = (_unpack_small(p, shapes + [(1,)]) for p in (g_sum, d_s, m_s, v_s))
    loss = sg_[-1][0]

    order = ["norm_mix_g", "w_in", "rel_bias", "sg_ln_g", "sg_ln_b", "sg_w", "sg_b", "w_branch_att", "w_branch_sg", "w_out",
             "norm_xattn_g", "norm_mem_g", "w_xq", "w_xkv", "w_xo", "norm_ffn_g", "w_ffn_in", "w_ffn_out", "norm_final_g"]
    small_names = ["norm_mix_g", "rel_bias", "sg_ln_g", "sg_ln_b", "sg_w", "sg_b", "norm_xattn_g", "norm_mem_g", "norm_ffn_g",
                   "norm_final_g"]
    res = {}
    for i, nm in enumerate(small_names):
        res[nm] = (sg_[i], sd[i], sm[i], sv_[i])
    for nm, outs in zip(big_names, big_out):
        res[nm] = tuple(o[None] for o in outs)
    return (loss, dx[None], *[res[nm][0] for nm in order], *[res[nm][1] for nm in order],
            *[res[nm][2] for nm in order], *[res[nm][3] for nm in order])
```

```python
import functools
from typing import NamedTuple

import numpy as np
import jax
import jax.numpy as jnp
from jax import lax
from jax.experimental import pallas as pl
from jax.experimental.pallas import tpu as pltpu

F32, BF16 = jnp.float32, jnp.bfloat16
MESH = pl.DeviceIdType.MESH

EPS = 1e-6
NEG_INF = -1e30
CHUNK = 64
N_PREV_CHUNKS = 8
ATT_HEADS, ATT_HEAD_DIM = 8, 64
REL_CLIP = 128
SG_BLOCK, SG_GROUPS, SG_GROUP_DIM = 128, 8, 64
XATT_HEADS, XATT_HEAD_DIM = 4, 256
ATT_TILE = 128
ATT_WIN = 5
ADAM_LR, ADAM_B1, ADAM_B2, ADAM_EPS, ADAM_WD, ADAM_STEP = 0.001, 0.9, 0.999, 1e-08, 0.01, 10

VMEM_LIMIT_V7X = 56 * 1024 * 1024
EW_BLOCK_BYTES = 2 << 20
ROW_TILE = 512
DW_TOKENS = 1024
XATT_TILE = 1024
SGU_TILE = 1024
HBM = pl.BlockSpec(memory_space=pl.ANY)


def _params(*sem):
    return pltpu.CompilerParams(dimension_semantics=sem, vmem_limit_bytes=VMEM_LIMIT_V7X)


def _full(shape):
    n = len(shape)
    return pl.BlockSpec(shape, lambda *_: (0,) * n, pipeline_mode=pl.Buffered(1))


def _acc(shape):
    n = len(shape)
    return pl.BlockSpec(shape, lambda *_: (0,) * n)


class Hook(NamedTuple):
    ins: tuple = ()
    out_shapes: tuple = ()
    alias: tuple = ()
    sems: tuple = ()
    steps: tuple = ()


def _run_hook(hook, pos, h_in, h_out, h_sems):
    for at, where, fn in hook.steps:
        if where == pos:
            pl.when(pl.program_id(0) == at)(functools.partial(fn, h_in, h_out, h_sems))


def _hooked_call(body, hook, *, n_in, n_out, in_specs, out_specs, out_shape, scratch_shapes=(), **kw):
    nh = len(hook.ins)
    aliases = {n_in + i: n_out + o for i, o in hook.alias}
    return pl.pallas_call(
        body, in_specs=list(in_specs) + [HBM] * nh, out_specs=list(out_specs) + [HBM] * len(hook.out_shapes),
        out_shape=list(out_shape) + list(hook.out_shapes), scratch_shapes=list(scratch_shapes) + list(hook.sems),
        input_output_aliases=aliases, **kw)


def _nt(a, b):
    return lax.dot_general(a, b, (((1,), (1,)), ((), ())), preferred_element_type=F32)


def _tn(a, b):
    return lax.dot_general(a, b, (((0,), (0,)), ((), ())), preferred_element_type=F32)


def _nn(a, b):
    return jnp.dot(a, b, preferred_element_type=F32)


def _sigmoid(x):
    return 0.5 * jnp.tanh(0.5 * x) + 0.5


_GELU_C = float(np.sqrt(2.0 / np.pi))


def _gelu(x):
    t = jnp.tanh(_GELU_C * (x + 0.044715 * (x * x * x)))
    return x * (0.5 * (1.0 + t))


def _gelu_grad(x):
    t = jnp.tanh(_GELU_C * (x + 0.044715 * (x * x * x)))
    return 0.5 * (1.0 + t) + 0.5 * x * (1.0 - t * t) * (_GELU_C * (1.0 + 3.0 * 0.044715 * x * x))


def _rms_stats(xf):
    rstd = lax.rsqrt(jnp.mean(xf * xf, axis=-1, keepdims=True) + EPS)
    return rstd, xf * rstd


def _rms_bwd(xhat, rstd, g, dh):
    dxh = dh * g
    dx = rstd * (dxh - xhat * jnp.mean(dxh * xhat, axis=-1, keepdims=True))
    return dx, dh * xhat


def norm_mm(x, g, w, *, tm, scale, name, hook=Hook()):
    S, D = x.shape
    J, _, C = w.shape
    nh, nho = len(hook.ins), len(hook.out_shapes)

    def body(x_ref, g_ref, w_ref, *rest):
        h_in, (h_ref, z_ref), h_out, h_sems = rest[:nh], rest[nh:nh + 2], rest[nh + 2:nh + 2 + nho], rest[nh + 2 + nho:]
        _run_hook(hook, "before", h_in, h_out, h_sems)
        _, xhat = _rms_stats(x_ref[...])
        h = (xhat * g_ref[...]).astype(BF16)
        h_ref[...] = h
        for j in range(J):
            acc = _nn(h, w_ref[j])
            if scale is not None:
                acc = acc * scale
            z_ref[:, j * C:(j + 1) * C] = acc.astype(BF16)
        _run_hook(hook, "after", h_in, h_out, h_sems)

    return _hooked_call(
        body, hook, n_in=3, n_out=2, grid=(S // tm,), name=name,
        in_specs=[pl.BlockSpec((tm, D), lambda i: (i, 0)), _full((1, D)), _full((J, D, C))],
        out_specs=[pl.BlockSpec((tm, D), lambda i: (i, 0)), pl.BlockSpec((tm, J * C), lambda i: (i, 0))],
        out_shape=[jax.ShapeDtypeStruct((S, D), BF16), jax.ShapeDtypeStruct((S, J * C), BF16)],
        compiler_params=_params("arbitrary"),
    )(x, g, w, *hook.ins)


def _att_window_specs(nt, col):
    return [pl.BlockSpec((ATT_TILE, 512), lambda t, j=j: (jnp.clip(t - (ATT_WIN - 1) + j, 0, nt - 1), col))
            for j in range(ATT_WIN)]


ATT_SCALE = ATT_HEAD_DIM ** -0.5
ATT_FAR = 3 * ATT_TILE


def _att_scores(q_scaled, k, bias, valid):
    return jnp.where(valid, _nt(q_scaled, k) + bias, NEG_INF)


def _att_valid(t):
    kpos = lax.broadcasted_iota(jnp.int32, (ATT_TILE, ATT_WIN * ATT_TILE), 1) + (t - (ATT_WIN - 1)) * ATT_TILE
    return kpos >= 0


def attn_fwd(z, bias, hook=Hook()):
    S = z.shape[0]
    nt = S // ATT_TILE
    n_in, nh, nho = 2 + 2 * ATT_WIN, len(hook.ins), len(hook.out_shapes)

    def body(q_ref, *refs):
        k_refs, v_refs = refs[:ATT_WIN], refs[ATT_WIN:2 * ATT_WIN]
        bias_ref = refs[2 * ATT_WIN]
        rest = refs[2 * ATT_WIN + 1:]
        h_in, (y_ref, lse_ref), h_out = rest[:nh], rest[nh:nh + 2], rest[nh + 2:nh + 2 + nho]
        s_scr, p_scr = rest[nh + 2 + nho:nh + 4 + nho]
        h_sems = rest[nh + 4 + nho:]
        _run_hook(hook, "before", h_in, h_out, h_sems)
        valid = _att_valid(pl.program_id(0))
        heads = [slice(h * ATT_HEAD_DIM, (h + 1) * ATT_HEAD_DIM) for h in range(ATT_HEADS)]
        for h, sl in enumerate(heads):
            k = jnp.concatenate([r[:, sl] for r in k_refs], axis=0)
            s_scr[h] = _att_scores(q_ref[:, sl] * ATT_SCALE, k, bias_ref[h], valid)
        stats = []
        for h in range(ATT_HEADS):
            s = s_scr[h]
            m = jnp.max(s, axis=-1, keepdims=True)
            p = jnp.exp(s - m)
            stats.append((m, jnp.sum(p, axis=-1, keepdims=True)))
            p_scr[h] = p.astype(BF16)
        for h, sl in enumerate(heads):
            m, l = stats[h]
            v = jnp.concatenate([r[:, sl] for r in v_refs], axis=0)
            y_ref[:, sl] = (_nn(p_scr[h], v) / l).astype(BF16)
            lse_ref[:, h:h + 1] = m + jnp.log(l)
        _run_hook(hook, "after", h_in, h_out, h_sems)

    tile = lambda col: pl.BlockSpec((ATT_TILE, 512), lambda t: (t, col))
    return _hooked_call(
        body, hook, n_in=n_in, n_out=2, grid=(nt,), name="attn_fwd",
        in_specs=[tile(0)] + _att_window_specs(nt, 1) + _att_window_specs(nt, 2) + [_full(bias.shape)],
        out_specs=[tile(0), pl.BlockSpec((ATT_TILE, ATT_HEADS), lambda t: (t, 0))],
        out_shape=[jax.ShapeDtypeStruct((S, 512), BF16), jax.ShapeDtypeStruct((S, ATT_HEADS), F32)],
        scratch_shapes=[pltpu.VMEM((ATT_HEADS, ATT_TILE, ATT_WIN * ATT_TILE), F32),
                        pltpu.VMEM((ATT_HEADS, ATT_TILE, ATT_WIN * ATT_TILE), BF16)],
        compiler_params=_params("arbitrary"),
    )(z, *([z] * (2 * ATT_WIN)), bias, *hook.ins)


SG_PAIRS = SG_GROUPS // 2


def _group_mean(x, a_ref):
    rows = x.shape[0]
    hi = x.astype(BF16)
    lo = (x - hi.astype(F32)).astype(BF16)
    both = _nn(jnp.concatenate([hi, lo], axis=0), a_ref[...])
    return both[:rows] + both[rows:]


def _pair_diag(x, ref):
    first = lax.broadcasted_iota(jnp.int32, x.shape, 1) < SG_GROUP_DIM
    ref[0:SG_BLOCK, :] = jnp.where(first, x, 0.0).astype(BF16)
    ref[SG_BLOCK:2 * SG_BLOCK, :] = jnp.where(first, 0.0, x).astype(BF16)


def _sgu_norm(zv_ref, a_ref, lng_ref, lnb_ref):
    v = _gelu(zv_ref[...].astype(F32))
    vc = v - _group_mean(v, a_ref)
    rstd = lax.rsqrt(_group_mean(vc * vc, a_ref) + EPS)
    xhat = vc * rstd
    return xhat, rstd, xhat * lng_ref[...] + lnb_ref[...]


def _sgu_rows(b):
    return pl.ds(pl.multiple_of(b * SG_BLOCK, SG_BLOCK), SG_BLOCK)


def sgu_fwd(z, amean, lng, lnb, wpair, bfull, *, tm, hook=Hook()):
    S = z.shape[0]
    nh, nho = len(hook.ins), len(hook.out_shapes)

    def body(zu_ref, zv_ref, a_ref, lng_ref, lnb_ref, wpair_ref, bfull_ref, *rest):
        h_in, y_ref, h_out = rest[:nh], rest[nh], rest[nh + 1:nh + 1 + nho]
        vn_scr, vbd = rest[nh + 1 + nho:nh + 3 + nho]
        h_sems = rest[nh + 3 + nho:]
        _run_hook(hook, "before", h_in, h_out, h_sems)
        vn_scr[...] = _sgu_norm(zv_ref, a_ref, lng_ref, lnb_ref)[2]

        def block(b, carry):
            rows = _sgu_rows(b)
            u = _gelu(zu_ref[rows, :].astype(F32))
            for p in range(SG_PAIRS):
                lanes = slice(p * 128, (p + 1) * 128)
                _pair_diag(vn_scr[rows, lanes], vbd.at[p])
                sv = _nn(wpair_ref[p], vbd[p]) + bfull_ref[:, lanes]
                y_ref[rows, lanes] = (u[:, lanes] * sv).astype(BF16)
            return carry

        lax.fori_loop(0, tm // SG_BLOCK, block, 0, unroll=4)
        _run_hook(hook, "after", h_in, h_out, h_sems)

    tile = lambda col: pl.BlockSpec((tm, 512), lambda i: (i, col))
    smalls = [amean, lng, lnb, wpair, bfull]
    return _hooked_call(
        body, hook, n_in=2 + len(smalls), n_out=1, grid=(S // tm,), name="sgu_fwd",
        in_specs=[tile(3), tile(4)] + [_full(a.shape) for a in smalls],
        out_specs=[tile(0)], out_shape=[jax.ShapeDtypeStruct((S, 512), BF16)],
        scratch_shapes=[pltpu.VMEM((tm, 512), F32), pltpu.VMEM((SG_PAIRS, 2 * SG_BLOCK, 128), BF16)],
        compiler_params=_params("arbitrary"),
    )(z, z, *smalls, *hook.ins)


def _gate_specs(tm):
    return [pl.BlockSpec((tm, 512), lambda i, c=c: (i, c)) for c in (5, 6, 7, 8)]


def merge_fwd(y_att, y_sg, z, x, w_ba, w_bs, w_out, *, tm):
    S, D = x.shape
    J, _, C = w_ba.shape

    def body(ya_ref, ys_ref, g0, g1, g2, g3, x_ref, wba_ref, wbs_ref, wo_ref, a_ref, b_ref, m_ref, x1_ref):
        ya, ys = ya_ref[...], ys_ref[...]
        a = jnp.concatenate([_nn(ya, wba_ref[j]) for j in range(J)], axis=1)
        b = jnp.concatenate([_nn(ys, wbs_ref[j]) for j in range(J)], axis=1)
        sa = _sigmoid(jnp.concatenate([g0[...], g1[...]], axis=1).astype(F32))
        sb = _sigmoid(jnp.concatenate([g2[...], g3[...]], axis=1).astype(F32))
        a_ref[...] = a.astype(BF16)
        b_ref[...] = b.astype(BF16)
        merged = (sa * a + sb * b).astype(BF16)
        m_ref[...] = merged
        x1_ref[...] = x_ref[...] + _nn(merged, wo_ref[...])

    row = lambda n: pl.BlockSpec((tm, n), lambda i: (i, 0))
    return pl.pallas_call(
        body, grid=(S // tm,), name="merge_fwd",
        in_specs=[row(512), row(512)] + _gate_specs(tm) + [row(D), _full(w_ba.shape), _full(w_bs.shape), _full(w_out.shape)],
        out_specs=[row(D)] * 4,
        out_shape=[jax.ShapeDtypeStruct((S, D), BF16)] * 3 + [jax.ShapeDtypeStruct((S, D), F32)],
        compiler_params=_params("parallel"),
    )(y_att, y_sg, z, z, z, z, x, w_ba, w_bs, w_out)


def xattn_fwd(xq, kv, w_xo, x1, *, tm):
    S, D = xq.shape
    dh = XATT_HEAD_DIM

    def body(q_ref, kv_ref, wo_ref, x1_ref, o_ref, lse_ref, x2_ref):
        outs = []
        for h in range(XATT_HEADS):
            s = _nt(q_ref[:, h * dh:(h + 1) * dh], kv_ref[:, h * dh:(h + 1) * dh])
            m = jnp.max(s, axis=-1, keepdims=True)
            p = jnp.exp(s - m)
            l = jnp.sum(p, axis=-1, keepdims=True)
            outs.append((_nn(p.astype(BF16), kv_ref[:, D + h * dh:D + (h + 1) * dh]) / l).astype(BF16))
            lse_ref[:, h:h + 1] = m + jnp.log(l)
        o = jnp.concatenate(outs, axis=1)
        o_ref[...] = o
        x2_ref[...] = x1_ref[...] + _nn(o, wo_ref[...])

    row = lambda n: pl.BlockSpec((tm, n), lambda i: (i, 0))
    return pl.pallas_call(
        body, grid=(S // tm,), name="xattn_fwd",
        in_specs=[row(D), _full(kv.shape), _full(w_xo.shape), row(D)],
        out_specs=[row(D), row(XATT_HEADS), row(D)],
        out_shape=[jax.ShapeDtypeStruct((S, D), BF16), jax.ShapeDtypeStruct((S, XATT_HEADS), F32),
                   jax.ShapeDtypeStruct((S, D), F32)],
        compiler_params=_params("parallel"),
    )(xq, kv, w_xo, x1)


def ffn_out_loss(gu, w, x2, g, target, *, tm):
    S, D = x2.shape
    F = w.shape[0]

    def body(gu_ref, w_ref, x2_ref, g_ref, t_ref, act_ref, loss_ref, dx_ref, dg_ref):
        @pl.when(pl.program_id(0) == 0)
        def _():
            loss_ref[...] = jnp.zeros_like(loss_ref)
            dg_ref[...] = jnp.zeros_like(dg_ref)

        half = gu_ref[:, :F] * 0.5
        act = (half * gu_ref[:, F:]) * (jnp.tanh(half) + 1.0)
        act_ref[...] = act
        gv = g_ref[...]
        rstd, xhat = _rms_stats(x2_ref[...] + _nn(act, w_ref[...]))
        err = xhat * gv - t_ref[...]
        loss_ref[...] += 0.5 * jnp.sum(jnp.mean(err * err, axis=-1, keepdims=True))
        dx, dgc = _rms_bwd(xhat, rstd, gv, err * (1.0 / D))
        dx_ref[...] = dx
        dg_ref[...] += jnp.sum(dgc, axis=0, keepdims=True)

    row = lambda n: pl.BlockSpec((tm, n), lambda i: (i, 0))
    return pl.pallas_call(
        body, grid=(S // tm,), name="ffn_out_loss",
        in_specs=[row(2 * F), _full(w.shape), row(D), _full((1, D)), row(D)],
        out_specs=[row(F), _acc((1, 128)), row(D), _acc((1, D))],
        out_shape=[jax.ShapeDtypeStruct((S, F), BF16), jax.ShapeDtypeStruct((1, 128), F32),
                   jax.ShapeDtypeStruct((S, D), F32), jax.ShapeDtypeStruct((1, D), F32)],
        compiler_params=_params("arbitrary"),
    )(gu, w, x2, g, target)


def ffn_act_bwd(dx3, gu, w, *, tm, nchunk):
    S, D = dx3.shape
    F = w.shape[0]
    cn = F // nchunk

    def body(dx_ref, gu_ref, w_ref, dgu_ref):
        dxb = dx_ref[...].astype(BF16)
        for j in range(nchunk):
            dact = _nt(dxb, w_ref[j * cn:(j + 1) * cn, :])
            gate = gu_ref[:, j * cn:(j + 1) * cn]
            up = gu_ref[:, F + j * cn:F + (j + 1) * cn]
            sg = _sigmoid(gate)
            silu = gate * sg
            dsilu = sg + silu * (1.0 - sg)
            dgu_ref[:, j * cn:(j + 1) * cn] = (dact * (up * dsilu).astype(F32)).astype(BF16)
            dgu_ref[:, F + j * cn:F + (j + 1) * cn] = (dact * silu.astype(F32)).astype(BF16)

    row = lambda n: pl.BlockSpec((tm, n), lambda i: (i, 0))
    return pl.pallas_call(
        body, grid=(S // tm,), name="ffn_act_bwd",
        in_specs=[row(D), row(2 * F), _full(w.shape)], out_specs=row(2 * F),
        out_shape=jax.ShapeDtypeStruct((S, 2 * F), BF16), compiler_params=_params("parallel"),
    )(dx3, gu, w)


def _overlaps(widths, lo, hi):
    out, off = [], 0
    for p, wd in enumerate(widths):
        a, b = max(lo, off), min(hi, off + wd)
        if a < b:
            out.append((p, a - off, b - off, a - lo))
        off += wd
    return out


def tn_mm_pieces(a, pieces, *, shards, per, tk, name):
    K, M = a.shape
    widths = [p.shape[1] for p in pieces]
    C = sum(widths) // shards
    tn = per * C
    nk = K // tk
    n_tiles = shards // per

    def body(a_ref, *refs):
        p_refs, o_ref, acc_ref = refs[:len(pieces)], refs[len(pieces)], refs[len(pieces) + 1]
        n, k = pl.program_id(0), pl.program_id(1)

        @pl.when(k == 0)
        def _():
            acc_ref[...] = jnp.zeros_like(acc_ref)

        av = a_ref[...]
        for tile in range(n_tiles):
            @pl.when(n == tile)
            def _(tile=tile):
                for p, c0, c1, at in _overlaps(widths, tile * tn, (tile + 1) * tn):
                    acc_ref[:, at:at + c1 - c0] += _tn(av, p_refs[p][:, c0:c1])

        @pl.when(k == nk - 1)
        def _():
            for s in range(per):
                o_ref[s] = acc_ref[:, s * C:(s + 1) * C].astype(BF16)

    return pl.pallas_call(
        body, grid=(n_tiles, nk), name=name,
        in_specs=[pl.BlockSpec((tk, M), lambda n, k: (k, 0))] + [pl.BlockSpec((tk, wd), lambda n, k: (k, 0)) for wd in widths],
        out_specs=pl.BlockSpec((per, M, C), lambda n, k: (n, 0, 0)), out_shape=jax.ShapeDtypeStruct((shards, M, C), BF16),
        scratch_shapes=[pltpu.VMEM((M, tn), F32)], compiler_params=_params("parallel", "arbitrary"),
    )(a, *pieces)


def tn_mm(a, b, *, shards, tn, tk, name):
    K, M = a.shape
    N = b.shape[1]
    C = N // shards
    per = tn // C
    nk = K // tk

    def body(a_ref, b_ref, o_ref, acc_ref):
        k = pl.program_id(1)

        @pl.when(k == 0)
        def _():
            acc_ref[...] = jnp.zeros_like(acc_ref)

        acc_ref[...] += _tn(a_ref[...].astype(BF16), b_ref[...].astype(BF16))

        @pl.when(k == nk - 1)
        def _():
            if shards > 1:
                for s in range(per):
                    o_ref[s] = acc_ref[:, s * C:(s + 1) * C].astype(BF16)
            else:
                o_ref[...] = acc_ref[...].astype(BF16)

    if shards > 1:
        out_spec = pl.BlockSpec((per, M, C), lambda n, k: (n, 0, 0))
        out_shape = jax.ShapeDtypeStruct((shards, M, C), BF16)
    else:
        out_spec = pl.BlockSpec((M, tn), lambda n, k: (0, n))
        out_shape = jax.ShapeDtypeStruct((M, N), BF16)
    return pl.pallas_call(
        body, grid=(N // tn, nk), name=name,
        in_specs=[pl.BlockSpec((tk, M), lambda n, k: (k, 0)), pl.BlockSpec((tk, tn), lambda n, k: (k, n))],
        out_specs=out_spec, out_shape=out_shape, scratch_shapes=[pltpu.VMEM((M, tn), F32)],
        compiler_params=_params("parallel", "arbitrary"),
    )(a, b)


def mm_nt_norm_bwd(dy, w, x, g, dx_in, *, tm, name, hook=Hook()):
    S, D = x.shape
    J, _, C = w.shape
    has_in = dx_in is not None
    dys = list(dy) if isinstance(dy, (list, tuple)) else [dy]
    widths = [d.shape[1] for d in dys]
    nd = len(dys)
    n_in, nh, nho = nd + 3 + has_in, len(hook.ins), len(hook.out_shapes)

    def body(*refs):
        dy_refs = refs[:nd]
        w_ref, x_ref, g_ref = refs[nd:nd + 3]
        dxin_ref = refs[nd + 3] if has_in else None
        rest = refs[n_in:]
        h_in, (dx_ref, dg_ref), h_out, h_sems = rest[:nh], rest[nh:nh + 2], rest[nh + 2:nh + 2 + nho], rest[nh + 2 + nho:]
        _run_hook(hook, "before", h_in, h_out, h_sems)

        @pl.when(pl.program_id(0) == 0)
        def _():
            dg_ref[...] = jnp.zeros_like(dg_ref)

        dh = None
        for j in range(J):
            for p, c0, c1, at in _overlaps(widths, j * C, (j + 1) * C):
                part = _nt(dy_refs[p][:, c0:c1], w_ref[j, :, at:at + c1 - c0])
                dh = part if dh is None else dh + part
        rstd, xhat = _rms_stats(x_ref[...])
        dx, dgc = _rms_bwd(xhat, rstd, g_ref[...], dh)
        dx_ref[...] = dx + dxin_ref[...] if has_in else dx
        dg_ref[...] += jnp.sum(dgc, axis=0, keepdims=True)
        _run_hook(hook, "after", h_in, h_out, h_sems)

    row = lambda n: pl.BlockSpec((tm, n), lambda i: (i, 0))
    ins = dys + [w, x, g] + ([dx_in] if has_in else [])
    return _hooked_call(
        body, hook, n_in=n_in, n_out=2, grid=(S // tm,), name=name,
        in_specs=[row(wd) for wd in widths] + [_full(w.shape), row(D), _full((1, D))] + ([row(D)] if has_in else []),
        out_specs=[row(D), _acc((1, D))],
        out_shape=[jax.ShapeDtypeStruct((S, D), F32), jax.ShapeDtypeStruct((1, D), F32)],
        compiler_params=_params("arbitrary"),
    )(*ins, *hook.ins)


def xattn_bwd(dx2, xq, o, lse, kv, w_xo, *, tm):
    S, D = xq.shape
    M = kv.shape[0]
    dh = XATT_HEAD_DIM

    def body(dx_ref, q_ref, o_ref, lse_ref, kv_ref, wo_ref, dq_ref, dkv_ref):
        @pl.when(pl.program_id(0) == 0)
        def _():
            dkv_ref[...] = jnp.zeros_like(dkv_ref)

        do = _nt(dx_ref[...].astype(BF16), wo_ref[...])
        for h in range(XATT_HEADS):
            hs = slice(h * dh, (h + 1) * dh)
            vs = slice(D + h * dh, D + (h + 1) * dh)
            q, k, v = q_ref[:, hs], kv_ref[:, hs], kv_ref[:, vs]
            do_h = do[:, hs]
            do_b = do_h.astype(BF16)
            p = jnp.exp(_nt(q, k) - lse_ref[:, h:h + 1])
            delta = jnp.sum(do_h * o_ref[:, hs].astype(F32), axis=-1, keepdims=True)
            ds = (p * (_nt(do_b, v) - delta)).astype(BF16)
            dq_ref[:, hs] = (_nn(ds, k) * (dh ** -0.5)).astype(BF16)
            dkv_ref[:, hs] += _tn(ds, q)
            dkv_ref[:, vs] += _tn(p.astype(BF16), do_b)

    row = lambda n: pl.BlockSpec((tm, n), lambda i: (i, 0))
    return pl.pallas_call(
        body, grid=(S // tm,), name="xattn_bwd",
        in_specs=[row(D), row(D), row(D), row(XATT_HEADS), _full(kv.shape), _full(w_xo.shape)],
        out_specs=[row(D), _acc((M, 2 * D))],
        out_shape=[jax.ShapeDtypeStruct((S, D), BF16), jax.ShapeDtypeStruct((M, 2 * D), F32)],
        compiler_params=_params("arbitrary"),
    )(dx2, xq, o, lse, kv, w_xo)


def merge_bwd(dx1, w_out, a, b, z, w_ba, w_bs, *, tm, hook=Hook()):
    S, D = dx1.shape
    J, W, C = w_ba.shape
    nh, nho = len(hook.ins), len(hook.out_shapes)

    def body(dx_ref, wo_ref, a_ref, b_ref, g0, g1, g2, g3, wba_ref, wbs_ref, *rest):
        h_in, (da_ref, db_ref, dg_ref, dya_ref, dys_ref) = rest[:nh], rest[nh:nh + 5]
        h_out, h_sems = rest[nh + 5:nh + 5 + nho], rest[nh + 5 + nho:]
        _run_hook(hook, "before", h_in, h_out, h_sems)
        dm = _nt(dx_ref[...].astype(BF16), wo_ref[...])
        sa = _sigmoid(jnp.concatenate([g0[...], g1[...]], axis=1))
        sb = _sigmoid(jnp.concatenate([g2[...], g3[...]], axis=1))
        dg_ref[:, :D] = (dm * (a_ref[...] * (sa * (1.0 - sa))).astype(F32)).astype(BF16)
        dg_ref[:, D:] = (dm * (b_ref[...] * (sb * (1.0 - sb))).astype(F32)).astype(BF16)
        da = (dm * sa.astype(F32)).astype(BF16)
        db = (dm * sb.astype(F32)).astype(BF16)
        da_ref[...] = da
        db_ref[...] = db
        dya = _nt(da[:, 0:C], wba_ref[0])
        dys = _nt(db[:, 0:C], wbs_ref[0])
        for j in range(1, J):
            dya += _nt(da[:, j * C:(j + 1) * C], wba_ref[j])
            dys += _nt(db[:, j * C:(j + 1) * C], wbs_ref[j])
        dya_ref[...] = dya.astype(BF16)
        dys_ref[...] = dys.astype(BF16)
        _run_hook(hook, "after", h_in, h_out, h_sems)

    row = lambda n: pl.BlockSpec((tm, n), lambda i: (i, 0))
    return _hooked_call(
        body, hook, n_in=10, n_out=5, grid=(S // tm,), name="merge_bwd",
        in_specs=[row(D), _full(w_out.shape), row(D), row(D)] + _gate_specs(tm) + [_full(w_ba.shape), _full(w_bs.shape)],
        out_specs=[row(D), row(D), row(2 * D), row(W), row(W)],
        out_shape=[jax.ShapeDtypeStruct((S, D), BF16)] * 2 + [jax.ShapeDtypeStruct((S, 2 * D), BF16)]
        + [jax.ShapeDtypeStruct((S, W), BF16)] * 2,
        compiler_params=_params("arbitrary"),
    )(dx1, w_out, a, b, z, z, z, z, w_ba, w_bs, *hook.ins)


def attn_bwd(z, y, dy, lse, bias, hook=Hook()):
    S = z.shape[0]
    nt = S // ATT_TILE
    back = ATT_WIN - 1
    n_in, nh, nho = 5 + 2 * ATT_WIN, len(hook.ins), len(hook.out_shapes)

    def body(q_ref, *refs):
        k_refs, v_refs = refs[:ATT_WIN], refs[ATT_WIN:2 * ATT_WIN]
        y_ref, dy_ref, lse_ref, bias_ref = refs[2 * ATT_WIN:2 * ATT_WIN + 4]
        rest = refs[2 * ATT_WIN + 4:]
        h_in, (dqkv_ref, dbias_ref), h_out = rest[:nh], rest[nh:nh + 2], rest[nh + 2:nh + 2 + nho]
        dk_acc, dv_acc, dq_buf = rest[nh + 2 + nho:nh + 5 + nho]
        h_sems = rest[nh + 5 + nho:]
        t = pl.program_id(0)
        _run_hook(hook, "before", h_in, h_out, h_sems)

        @pl.when(t == 0)
        def _():
            dbias_ref[...] = jnp.zeros_like(dbias_ref)
            dk_acc[...] = jnp.zeros_like(dk_acc)
            dv_acc[...] = jnp.zeros_like(dv_acc)
            dq_buf[...] = jnp.zeros_like(dq_buf)

        def emit():
            done = (t + 1) % ATT_WIN
            dqkv_ref[:, 0:512] = dq_buf[done]
            dqkv_ref[:, 512:1024] = dk_acc[done].T.astype(BF16)
            dqkv_ref[:, 1024:1536] = dv_acc[done].T.astype(BF16)

        @pl.when(t < nt)
        def _():
            valid = _att_valid(t)
            q_t = (q_ref[...] * ATT_SCALE).astype(F32).T.astype(BF16)
            do_t = dy_ref[...].astype(F32).T.astype(BF16)
            for h in range(ATT_HEADS):
                sl = slice(h * ATT_HEAD_DIM, (h + 1) * ATT_HEAD_DIM)
                k = jnp.concatenate([r[:, sl] for r in k_refs], axis=0)
                v = jnp.concatenate([r[:, sl] for r in v_refs], axis=0)
                q, do_b = q_ref[:, sl] * ATT_SCALE, dy_ref[:, sl]
                p = jnp.exp(_att_scores(q, k, bias_ref[h], valid) - lse_ref[:, h:h + 1])
                delta = jnp.sum(do_b.astype(F32) * y_ref[:, sl].astype(F32), axis=-1, keepdims=True)
                ds = p * (_nt(do_b, v) - delta)
                dbias_ref[h, :, 0:ATT_TILE] += ds[:, 0:ATT_TILE] + ds[:, ATT_TILE:2 * ATT_TILE] + ds[:, 2 * ATT_TILE:ATT_FAR]
                dbias_ref[h, :, ATT_TILE:] += ds[:, ATT_FAR:]
                ds_b = ds.astype(BF16)
                dq_buf[t % ATT_WIN, :, sl] = (_nn(ds_b, k) * ATT_SCALE).astype(BF16)
                dk_w = _nn(q_t[sl, :], ds_b)
                dv_w = _nn(do_t[sl, :], p.astype(BF16))
                for i in range(ATT_WIN):
                    slot = (t + 1 + i) % ATT_WIN
                    cols = slice(i * ATT_TILE, (i + 1) * ATT_TILE)
                    if i == back:
                        dk_acc[slot, sl, :] = dk_w[:, cols]
                        dv_acc[slot, sl, :] = dv_w[:, cols]
                    else:
                        dk_acc[slot, sl, :] += dk_w[:, cols]
                        dv_acc[slot, sl, :] += dv_w[:, cols]
            emit()

        pl.when(t >= nt)(emit)
        _run_hook(hook, "after", h_in, h_out, h_sems)

    last = nt - 1
    dbias_shape = (ATT_HEADS, ATT_TILE, ATT_TILE + ATT_WIN * ATT_TILE - ATT_FAR)
    tile = lambda col, n=512: pl.BlockSpec((ATT_TILE, n), lambda t: (jnp.minimum(t, last), col))
    late = pl.BlockSpec((ATT_TILE, 1536), lambda t: (jnp.maximum(t - back, 0), 0))
    return _hooked_call(
        body, hook, n_in=n_in, n_out=2, grid=(nt + back,), name="attn_bwd",
        in_specs=[tile(0)] + _att_window_specs(nt, 1) + _att_window_specs(nt, 2)
        + [tile(0), tile(0), tile(0, ATT_HEADS), _full(bias.shape)],
        out_specs=[late, _acc(dbias_shape)],
        out_shape=[jax.ShapeDtypeStruct((S, 1536), BF16), jax.ShapeDtypeStruct(dbias_shape, F32)],
        scratch_shapes=[pltpu.VMEM((ATT_WIN, 512, ATT_TILE), F32)] * 2 + [pltpu.VMEM((ATT_WIN, ATT_TILE, 512), BF16)],
        compiler_params=_params("arbitrary"),
    )(z, *([z] * (2 * ATT_WIN)), y, dy, lse, bias, *hook.ins)


def sgu_bwd(z, dy, amean, lng, lnb, wpair, wtpair, bfull, maskpair, *, tm, hook=Hook()):
    S = z.shape[0]
    n = S // tm
    nh, nho = len(hook.ins), len(hook.out_shapes)

    def body(zu_ref, zv_ref, dy_ref, a_ref, lng_ref, lnb_ref, wpair_ref, wtpair_ref, bfull_ref, mask_ref, *rest):
        h_in, (duv_ref, dw_ref, dsgb_ref, dlng_ref, dlnb_ref), h_out = rest[:nh], rest[nh:nh + 5], rest[nh + 5:nh + 5 + nho]
        (xhat_scr, rstd_scr, vn_scr, dxh_scr, vbd, dbd, w_acc, b_acc, g_acc, s_acc) = rest[nh + 5 + nho:nh + 15 + nho]
        h_sems = rest[nh + 15 + nho:]
        i = pl.program_id(0)
        _run_hook(hook, "before", h_in, h_out, h_sems)

        @pl.when(i == 0)
        def _():
            for r in (w_acc, b_acc, g_acc, s_acc):
                r[...] = jnp.zeros_like(r)

        xhat_scr[...], rstd_scr[...], vn_scr[...] = _sgu_norm(zv_ref, a_ref, lng_ref, lnb_ref)

        def block(b, carry):
            rows = _sgu_rows(b)
            zu = zu_ref[rows, :].astype(F32)
            u, du = _gelu(zu), _gelu_grad(zu)
            dyv = dy_ref[rows, :].astype(F32)
            for p in range(SG_PAIRS):
                lanes = slice(p * 128, (p + 1) * 128)
                _pair_diag(vn_scr[rows, lanes], vbd.at[p])
                sv = _nn(wpair_ref[p], vbd[p]) + bfull_ref[:, lanes]
                duv_ref[rows, lanes] = (dyv[:, lanes] * sv * du[:, lanes]).astype(BF16)
                dsv = dyv[:, lanes] * u[:, lanes]
                b_acc[:, lanes] += dsv
                w_acc[p] += _nt(dsv.astype(BF16), vbd[p])
                _pair_diag(dsv, dbd.at[p])
                dvn = _nn(wtpair_ref[p], dbd[p])
                g_acc[:, lanes] += dvn * xhat_scr[rows, lanes]
                s_acc[:, lanes] += dvn
                dxh_scr[rows, lanes] = dvn * lng_ref[:, lanes]
            return carry

        lax.fori_loop(0, tm // SG_BLOCK, block, 0, unroll=4)
        dxh, xhat = dxh_scr[...], xhat_scr[...]
        dv = rstd_scr[...] * (dxh - _group_mean(dxh, a_ref) - xhat * _group_mean(dxh * xhat, a_ref))
        duv_ref[:, 512:1024] = (dv * _gelu_grad(zv_ref[...].astype(F32))).astype(BF16)

        @pl.when(i == n - 1)
        def _():
            dw_ref[...] = w_acc[...] * mask_ref[...]
            dsgb_ref[...] = _group_mean(b_acc[...], a_ref) * float(SG_GROUP_DIM)
            dlng_ref[...] = jnp.sum(g_acc[...], axis=0, keepdims=True)
            dlnb_ref[...] = jnp.sum(s_acc[...], axis=0, keepdims=True)

        _run_hook(hook, "after", h_in, h_out, h_sems)

    tile = lambda col: pl.BlockSpec((tm, 512), lambda i: (i, col))
    smalls = [amean, lng, lnb, wpair, wtpair, bfull, maskpair]
    pair_shape = (SG_PAIRS, SG_BLOCK, 2 * SG_BLOCK)
    return _hooked_call(
        body, hook, n_in=3 + len(smalls), n_out=5, grid=(n,), name="sgu_bwd",
        in_specs=[tile(3), tile(4), tile(0)] + [_full(a.shape) for a in smalls],
        out_specs=[pl.BlockSpec((tm, 1024), lambda i: (i, 0)), _acc(pair_shape), _acc((SG_BLOCK, 512)),
                   _acc((1, 512)), _acc((1, 512))],
        out_shape=[jax.ShapeDtypeStruct((S, 1024), BF16), jax.ShapeDtypeStruct(pair_shape, F32),
                   jax.ShapeDtypeStruct((SG_BLOCK, 512), F32), jax.ShapeDtypeStruct((1, 512), F32),
                   jax.ShapeDtypeStruct((1, 512), F32)],
        scratch_shapes=[pltpu.VMEM((tm, 512), F32)] * 4
        + [pltpu.VMEM((SG_PAIRS, 2 * SG_BLOCK, 128), BF16)] * 2
        + [pltpu.VMEM(pair_shape, F32)] + [pltpu.VMEM((SG_BLOCK, 512), F32)] * 3,
        compiler_params=_params("arbitrary"),
    )(z, z, dy, *smalls, *hook.ins)


def bias_colsum(p, far_part):
    H, _, L = p.shape
    near_lo = (ATT_WIN - 1) * ATT_TILE - REL_CLIP + 1
    near_hi = ATT_WIN * ATT_TILE

    def body(p_ref, f_ref, gw_ref, far_ref):
        k = lax.broadcasted_iota(jnp.int32, (1, L), 1)
        is_far = (k < near_lo) | (k >= near_hi)
        for h in range(H):
            g = jnp.sum(p_ref[h], axis=0, keepdims=True)
            gw_ref[h:h + 1, :] = g
            far_ref[h:h + 1, :] = jnp.zeros((1, 128), F32) + (jnp.sum(jnp.where(is_far, g, 0.0)) + jnp.sum(f_ref[h]))

    return pl.pallas_call(
        body, name="bias_colsum", in_specs=[_full(p.shape), _full(far_part.shape)],
        out_specs=[_acc((H, L)), _acc((H, 128))],
        out_shape=[jax.ShapeDtypeStruct((H, L), F32), jax.ShapeDtypeStruct((H, 128), F32)], grid=(1,),
        compiler_params=_params("arbitrary"),
    )(p, far_part)


def _row_tile(rows, cols):
    best = None
    for tr in range(16, rows + 1, 16):
        if rows % tr == 0 and tr * cols * 4 <= EW_BLOCK_BYTES:
            best = tr
    return best if best is not None else rows


def place_shard(w, own, name):
    R, C = w.shape
    r = R // 2
    tr = _row_tile(r, C)
    nr = r // tr

    def body(own_ref, w_ref, o_ref):
        o_ref[...] = w_ref[...].astype(BF16)

    return pl.pallas_call(
        body, name=name,
        grid_spec=pltpu.PrefetchScalarGridSpec(
            num_scalar_prefetch=1, grid=(2, nr),
            in_specs=[pl.BlockSpec((tr, C), lambda h, i, own: (h * nr + i, 0))],
            out_specs=pl.BlockSpec((None, None, tr, C), lambda h, i, own: (own[0], h, i, 0))),
        out_shape=jax.ShapeDtypeStruct((4, 2, r, C), BF16), compiler_params=_params("parallel", "parallel"),
    )(own, w)


def chip_partial(grad, recv, core, name):
    _, _, r, C = grad.shape
    tr = _row_tile(r, C)

    def body(core_ref, g_ref, r_ref, o_ref):
        o_ref[...] = (g_ref[...].astype(F32) + r_ref[...].astype(F32)).astype(BF16)

    spec = pl.BlockSpec((None, tr, C), lambda j, i, core: (j, i, 0))
    return pl.pallas_call(
        body, name=name,
        grid_spec=pltpu.PrefetchScalarGridSpec(
            num_scalar_prefetch=1, grid=(4, r // tr),
            in_specs=[pl.BlockSpec((None, None, tr, C), lambda j, i, core: (j, core[0], i, 0)), spec], out_specs=spec),
        out_shape=jax.ShapeDtypeStruct((4, r, C), BF16), compiler_params=_params("parallel", "parallel"),
    )(core, grad, recv)


def shard_sum(part, recv, own, core, name):
    _, r, C = part.shape
    tr = _row_tile(r, C)

    def body(own_ref, core_ref, p_ref, r0, r1, r2, o_ref):
        o_ref[...] = p_ref[...].astype(F32) + r0[...].astype(F32) + r1[...].astype(F32) + r2[...].astype(F32)

    return pl.pallas_call(
        body, name=name,
        grid_spec=pltpu.PrefetchScalarGridSpec(
            num_scalar_prefetch=2, grid=(r // tr,),
            in_specs=[pl.BlockSpec((None, tr, C), lambda i, own, core: (own[0], i, 0))]
            + [pl.BlockSpec((None, tr, C), lambda i, own, core, k=k: (k, i, 0)) for k in range(3)],
            out_specs=pl.BlockSpec((None, tr, C), lambda i, own, core: (core[0], i, 0))),
        out_shape=jax.ShapeDtypeStruct((2, r, C), F32), compiler_params=_params("parallel"),
    )(own, core, part, recv, recv, recv)


def adamw(w, g, m, v, name):
    R, C = w.shape
    tr = _row_tile(R, C)

    def body(w_ref, g_ref, m_ref, v_ref, d_ref, nm_ref, nv_ref):
        gv = g_ref[...]
        nm = ADAM_B1 * m_ref[...] + (1.0 - ADAM_B1) * gv
        nv = ADAM_B2 * v_ref[...] + (1.0 - ADAM_B2) * (gv * gv)
        m_hat = nm / (1.0 - ADAM_B1 ** ADAM_STEP)
        v_hat = nv / (1.0 - ADAM_B2 ** ADAM_STEP)
        d_ref[...] = -ADAM_LR * (m_hat / (jnp.sqrt(v_hat) + ADAM_EPS) + ADAM_WD * w_ref[...])
        nm_ref[...] = nm
        nv_ref[...] = nv

    spec = pl.BlockSpec((tr, C), lambda i: (i, 0))
    return pl.pallas_call(
        body, grid=(R // tr,), name=name, in_specs=[spec] * 4, out_specs=[spec] * 3,
        out_shape=[jax.ShapeDtypeStruct((R, C), F32)] * 3, compiler_params=_params("parallel"),
    )(w, g, m, v)


HBM = pl.BlockSpec(memory_space=pl.ANY)


def _place():
    x, y, c = lax.axis_index("x"), lax.axis_index("y"), lax.axis_index("c")
    chips = [(1 - x, y), (x, 1 - y), (1 - x, 1 - y)]
    return x, y, c, chips


def _gather_phases(n):
    def copies(buf, sems, kind):
        send_sems, recv_sems = sems
        x, y, c, chips = _place()

        def remote(w, k, slot, to):
            return pltpu.make_async_remote_copy(src_ref=slot, dst_ref=slot, send_sem=send_sems.at[w, k],
                                                recv_sem=recv_sems.at[w, k], device_id=to, device_id_type=MESH)

        def one(w, k, px, py):
            if kind == "mine":
                return remote(w, k, buf[w].at[2 * x + y, c], (px, py, c))
            if kind == "theirs":
                return remote(w, k, buf[w].at[2 * px + py, c], (px, py, c))
            if kind == "onward":
                return remote(w, 3 + k, buf[w].at[2 * px + py, c], (x, y, 1 - c))
            return remote(w, 3 + k, buf[w].at[2 * px + py, 1 - c], (x, y, 1 - c))

        return [one(w, k, px, py) for w in range(n) for k, (px, py) in enumerate(chips)]

    def start(_, buf, sems):
        for cp in copies(buf, sems, "mine"):
            cp.start()

    def relay(_, buf, sems):
        for theirs, onward in zip(copies(buf, sems, "theirs"), copies(buf, sems, "onward")):
            theirs.wait_recv()
            onward.start()

    def finish(_, buf, sems):
        for cp in copies(buf, sems, "relayed"):
            cp.wait_recv()
        for cp in copies(buf, sems, "mine") + copies(buf, sems, "onward"):
            cp.wait_send()

    return start, relay, finish


def _gather_sems(n):
    return (pltpu.SemaphoreType.DMA((n, 6)), pltpu.SemaphoreType.DMA((n, 6)))


def gather_weights(bufs):
    n = len(bufs)
    phases = _gather_phases(n)

    def body(*refs):
        for phase in phases:
            phase(None, refs[n:2 * n], refs[2 * n:])

    return pl.pallas_call(
        body, name="gather_weights", in_specs=[HBM] * n, out_specs=[HBM] * n,
        out_shape=[jax.ShapeDtypeStruct(b.shape, b.dtype) for b in bufs],
        input_output_aliases={i: i for i in range(n)}, scratch_shapes=list(_gather_sems(n)),
    )(*bufs)


def gather_hook(bufs, n_steps):
    start, relay, finish = _gather_phases(len(bufs))
    return Hook(ins=tuple(bufs), out_shapes=tuple(jax.ShapeDtypeStruct(b.shape, b.dtype) for b in bufs),
                alias=tuple((i, i) for i in range(len(bufs))),
                sems=_gather_sems(len(bufs)),
                steps=((0, "before", start), ((n_steps * 13) // 16, "before", relay), (n_steps - 1, "after", finish)))


def sibling_split(grads, name):
    n = len(grads)

    def body(*refs):
        src, dst = refs[:n], refs[n:2 * n]
        send_sems, recv_sems = refs[2 * n:]
        x, y, c, _ = _place()
        sends = [pltpu.make_async_remote_copy(src_ref=src[w].at[:, 1 - c], dst_ref=dst[w], send_sem=send_sems.at[w],
                                              recv_sem=recv_sems.at[w], device_id=(x, y, 1 - c), device_id_type=MESH)
                 for w in range(n)]
        for cp in sends:
            cp.start()
        for cp in sends:
            cp.wait()

    return pl.pallas_call(
        body, name=name, in_specs=[HBM] * n, out_specs=[HBM] * n,
        out_shape=[jax.ShapeDtypeStruct((4,) + g.shape[2:], g.dtype) for g in grads],
        scratch_shapes=[pltpu.SemaphoreType.DMA((n,))] * 2,
    )(*grads)


def split_hook(grads, n_steps):
    n = len(grads)

    def copies(src, dst, sems):
        send_sems, recv_sems = sems
        x, y, c, _ = _place()
        return [pltpu.make_async_remote_copy(src_ref=src[w].at[:, 1 - c], dst_ref=dst[w], send_sem=send_sems.at[w],
                                             recv_sem=recv_sems.at[w], device_id=(x, y, 1 - c), device_id_type=MESH)
                for w in range(n)]

    def start(src, dst, sems):
        for cp in copies(src, dst, sems):
            cp.start()

    def finish(src, dst, sems):
        for cp in copies(src, dst, sems):
            cp.wait()

    return Hook(ins=tuple(grads), out_shapes=tuple(jax.ShapeDtypeStruct((4,) + g.shape[2:], g.dtype) for g in grads),
                sems=(pltpu.SemaphoreType.DMA((n,)), pltpu.SemaphoreType.DMA((n,))),
                steps=((0, "before", start), (n_steps - 1, "after", finish)))


def exchange_hook(parts, n_steps):
    n = len(parts)

    def copies(src, dst, sems):
        send_sems, recv_sems = sems
        _, _, c, chips = _place()
        return [pltpu.make_async_remote_copy(
            src_ref=src[w].at[2 * px + py], dst_ref=dst[w].at[k], send_sem=send_sems.at[w, k],
            recv_sem=recv_sems.at[w, k], device_id=(px, py, c), device_id_type=MESH)
            for w in range(n) for k, (px, py) in enumerate(chips)]

    def start(src, dst, sems):
        for cp in copies(src, dst, sems):
            cp.start()

    def finish(src, dst, sems):
        for cp in copies(src, dst, sems):
            cp.wait()

    return Hook(ins=tuple(parts), out_shapes=tuple(jax.ShapeDtypeStruct((3,) + p.shape[1:], p.dtype) for p in parts),
                sems=(pltpu.SemaphoreType.DMA((n, 3)), pltpu.SemaphoreType.DMA((n, 3))),
                steps=((0, "before", start), (n_steps - 1, "after", finish)))


def sibling_join(sums):
    n = len(sums)

    def body(*refs):
        buf = refs[n:2 * n]
        send_sems, recv_sems = refs[2 * n:]
        x, y, c, _ = _place()
        sends = [pltpu.make_async_remote_copy(src_ref=buf[w].at[c], dst_ref=buf[w].at[c], send_sem=send_sems.at[w],
                                              recv_sem=recv_sems.at[w], device_id=(x, y, 1 - c), device_id_type=MESH)
                 for w in range(n)]
        for cp in sends:
            cp.start()
        for w, cp in enumerate(sends):
            cp.wait_send()
            pltpu.make_async_remote_copy(src_ref=buf[w].at[c], dst_ref=buf[w].at[1 - c], send_sem=send_sems.at[w],
                                         recv_sem=recv_sems.at[w], device_id=(x, y, 1 - c), device_id_type=MESH).wait_recv()

    return pl.pallas_call(
        body, name="sibling_join", in_specs=[HBM] * n, out_specs=[HBM] * n,
        out_shape=[jax.ShapeDtypeStruct(s.shape, s.dtype) for s in sums],
        input_output_aliases={i: i for i in range(n)},
        scratch_shapes=[pltpu.SemaphoreType.DMA((n,))] * 2,
    )(*sums)


def join_hook(sums, n_steps):
    n = len(sums)

    def copies(buf, sems, recv):
        send_sems, recv_sems = sems
        x, y, c, _ = _place()
        half = (1 - c) if recv else c
        return [pltpu.make_async_remote_copy(src_ref=buf[w].at[half], dst_ref=buf[w].at[half], send_sem=send_sems.at[w],
                                             recv_sem=recv_sems.at[w], device_id=(x, y, 1 - c), device_id_type=MESH)
                for w in range(n)]

    def start(_, buf, sems):
        for cp in copies(buf, sems, False):
            cp.start()

    def finish(_, buf, sems):
        for cp in copies(buf, sems, False):
            cp.wait_send()
        for cp in copies(buf, sems, True):
            cp.wait_recv()

    return Hook(ins=tuple(sums), out_shapes=tuple(jax.ShapeDtypeStruct(s.shape, s.dtype) for s in sums),
                alias=tuple((i, i) for i in range(n)),
                sems=(pltpu.SemaphoreType.DMA((n,)), pltpu.SemaphoreType.DMA((n,))),
                steps=((0, "before", start), (n_steps - 1, "after", finish)))


def allreduce_small(v):
    R, C = v.shape

    def body(v_ref, out_ref, all_ref, send_sems, recv_sems):
        x, y, c, chips = _place()
        me, sibling = (x, y, c), (x, y, 1 - c)

        def slot(px, py, pc):
            return all_ref.at[4 * px + 2 * py + pc]

        def copy(k, block, to, src=None):
            return pltpu.make_async_remote_copy(src_ref=slot(*block) if src is None else src, dst_ref=slot(*block),
                                                send_sem=send_sems.at[k], recv_sem=recv_sems.at[k], device_id=to,
                                                device_id_type=MESH)

        first = [copy(0, me, sibling, src=v_ref)] + [copy(1 + j, me, (*chip, c), src=v_ref) for j, chip in enumerate(chips)]
        for cp in first:
            cp.start()
        slot(*me)[...] = v_ref[...]
        passed = [copy(4 + j, (*chip, c), sibling) for j, chip in enumerate(chips)]
        for j, chip in enumerate(chips):
            copy(1 + j, (*chip, c), me).wait_recv()
            passed[j].start()
        copy(0, sibling, me).wait_recv()
        for j, chip in enumerate(chips):
            copy(4 + j, (*chip, 1 - c), me).wait_recv()
        for cp in first + passed:
            cp.wait_send()
        acc = all_ref[0]
        for k in range(1, 8):
            acc = acc + all_ref[k]
        out_ref[...] = acc

    vmem = pl.BlockSpec(memory_space=pltpu.VMEM)
    return pl.pallas_call(
        body, name="allreduce_small", in_specs=[vmem], out_specs=vmem, out_shape=jax.ShapeDtypeStruct((R, C), F32),
        scratch_shapes=[pltpu.VMEM((8, R, C), F32), pltpu.SemaphoreType.DMA((7,)), pltpu.SemaphoreType.DMA((7,))],
        compiler_params=pltpu.CompilerParams(vmem_limit_bytes=VMEM_LIMIT_V7X),
    )(v)


def allgather_hook(v, n_steps):
    R, C = v.shape

    def copies(v_ref, out, sems, kind):
        send_sems, recv_sems, _ = sems
        x, y, c, chips = _place()
        me, sibling = (x, y, c), (x, y, 1 - c)

        def slot(px, py, pc):
            return out.at[4 * px + 2 * py + pc]

        def copy(k, block, to, src=None):
            return pltpu.make_async_remote_copy(src_ref=slot(*block) if src is None else src, dst_ref=slot(*block),
                                                send_sem=send_sems.at[k], recv_sem=recv_sems.at[k], device_id=to,
                                                device_id_type=MESH)

        if kind == "mine":
            return [copy(0, me, sibling, src=v_ref)] + [copy(1 + j, me, (*chip, c), src=v_ref) for j, chip in enumerate(chips)]
        if kind == "theirs":
            return [copy(1 + j, (*chip, c), me) for j, chip in enumerate(chips)]
        if kind == "onward":
            return [copy(4 + j, (*chip, c), sibling) for j, chip in enumerate(chips)]
        return [copy(0, sibling, me)] + [copy(4 + j, (*chip, 1 - c), me) for j, chip in enumerate(chips)]

    def own(v_ref, out, sems):
        x, y, c, _ = _place()
        return pltpu.make_async_copy(v_ref, out.at[4 * x + 2 * y + c], sems[2])

    def start(ins, outs, sems):
        own(ins[0], outs[0], sems).start()
        for cp in copies(ins[0], outs[0], sems, "mine"):
            cp.start()

    def relay(ins, outs, sems):
        for theirs, onward in zip(copies(ins[0], outs[0], sems, "theirs"), copies(ins[0], outs[0], sems, "onward")):
            theirs.wait_recv()
            onward.start()

    def finish(ins, outs, sems):
        for cp in copies(ins[0], outs[0], sems, "relayed"):
            cp.wait_recv()
        for cp in copies(ins[0], outs[0], sems, "mine") + copies(ins[0], outs[0], sems, "onward"):
            cp.wait_send()
        own(ins[0], outs[0], sems).wait()

    return Hook(ins=(v,), out_shapes=(jax.ShapeDtypeStruct((8, R, C), F32),),
                sems=(pltpu.SemaphoreType.DMA((7,)), pltpu.SemaphoreType.DMA((7,)), pltpu.SemaphoreType.DMA(())),
                steps=((0, "before", start), ((n_steps * 13) // 16, "before", relay), (n_steps - 1, "after", finish)))


def merge_hooks(a, b):
    cut = (len(a.ins), len(a.out_shapes), len(a.sems))

    def part(fn, second):
        def run(ins, outs, sems):
            if second:
                fn(ins[cut[0]:], outs[cut[1]:], sems[cut[2]:])
            else:
                fn(ins[:cut[0]], outs[:cut[1]], sems[:cut[2]])
        return run

    steps = tuple((s, w, part(fn, False)) for s, w, fn in a.steps) + tuple((s, w, part(fn, True)) for s, w, fn in b.steps)
    alias = a.alias + tuple((i + cut[0], o + cut[1]) for i, o in b.alias)
    return Hook(a.ins + b.ins, a.out_shapes + b.out_shapes, alias, a.sems + b.sems, steps)


def sum_blocks(g):
    n, R, C = g.shape

    def body(g_ref, o_ref):
        acc = g_ref[0]
        for k in range(1, n):
            acc = acc + g_ref[k]
        o_ref[...] = acc

    return pl.pallas_call(body, name="sum_blocks", grid=(1,), in_specs=[_full(g.shape)], out_specs=_acc((R, C)),
                          out_shape=jax.ShapeDtypeStruct((R, C), F32), compiler_params=_params("arbitrary"))(g)


SMALL_PAD = 1024


def _pack_small(arrs):
    parts = []
    for a in arrs:
        f = a.reshape(-1).astype(F32)
        parts.append(jnp.pad(f, (0, (-f.shape[0]) % SMALL_PAD)))
    return jnp.concatenate(parts).reshape(-1, 128)


def _unpack_small(packed, shapes):
    flat = packed.reshape(-1)
    outs, off = [], 0
    for s in shapes:
        size = int(np.prod(s))
        outs.append(flat[off:off + size].reshape(s))
        off += size + (-size) % SMALL_PAD
    return outs


ATT_KEYS = ATT_WIN * ATT_TILE
ATT_NEAR_LO = (ATT_WIN - 1) * ATT_TILE - REL_CLIP + 1
ATT_PERIOD = ATT_KEYS + ATT_TILE + 1


def _att_bias_table(rel_bias):
    far = rel_bias[:, 2 * REL_CLIP:]
    near = rel_bias[:, 2 * REL_CLIP - 1:0:-1]
    w = jnp.concatenate([jnp.broadcast_to(far, (ATT_HEADS, ATT_NEAR_LO)), near,
                         jnp.broadcast_to(far, (ATT_HEADS, ATT_PERIOD - ATT_KEYS))], axis=1)
    rows = jnp.tile(w, (1, ATT_TILE))[:, :ATT_TILE * (ATT_PERIOD - 1)].reshape(ATT_HEADS, ATT_TILE, ATT_PERIOD - 1)
    i = np.arange(ATT_TILE)[:, None]
    m = np.arange(ATT_KEYS)[None, :]
    qc, kc = i // CHUNK, m // CHUNK
    band = (kc >= qc) & (kc <= qc + N_PREV_CHUNKS)
    return jnp.where(band[None], rows[:, :, :ATT_KEYS], NEG_INF)


def _rel_bias_grad(dbias):
    p = jnp.pad(dbias[:, :, ATT_TILE:], ((0, 0), (0, 0), (ATT_FAR, ATT_PERIOD - 1 - ATT_KEYS))).reshape(ATT_HEADS, -1)
    p = jnp.pad(p, ((0, 0), (0, ATT_TILE))).reshape(ATT_HEADS, ATT_TILE, ATT_PERIOD)
    gw, far = bias_colsum(p, dbias[:, :, :ATT_TILE])
    return jnp.concatenate([jnp.zeros((ATT_HEADS, 1), F32), gw[:, ATT_KEYS - 1:ATT_NEAR_LO - 1:-1], far[:, :1]], axis=1)


def kernel(x, mem, norm_mix_g, w_in, rel_bias, sg_ln_g, sg_ln_b, sg_w, sg_b, w_branch_att, w_branch_sg, w_out, norm_xattn_g, norm_mem_g, w_xq, w_xkv, w_xo, norm_ffn_g, w_ffn_in, w_ffn_out, norm_final_g, loss_target, m_norm_mix_g, m_w_in, m_rel_bias, m_sg_ln_g, m_sg_ln_b, m_sg_w, m_sg_b, m_w_branch_att, m_w_branch_sg, m_w_out, m_norm_xattn_g, m_norm_mem_g, m_w_xq, m_w_xkv, m_w_xo, m_norm_ffn_g, m_w_ffn_in, m_w_ffn_out, m_norm_final_g, v_norm_mix_g, v_w_in, v_rel_bias, v_sg_ln_g, v_sg_ln_b, v_sg_w, v_sg_b, v_w_branch_att, v_w_branch_sg, v_w_out, v_norm_xattn_g, v_norm_mem_g, v_w_xq, v_w_xkv, v_w_xo, v_norm_ffn_g, v_w_ffn_in, v_w_ffn_out, v_norm_final_g):
    S, D = x.shape[1], x.shape[2]
    x2d, mem2d, tgt = x[0], mem[0], loss_target[0]

    big_names = ["w_in", "w_branch_att", "w_branch_sg", "w_out", "w_xq", "w_xkv", "w_xo", "w_ffn_in", "w_ffn_out"]
    col_sharded = [True, True, True, False, False, True, False, True, False]
    big_w = [a[0] for a in (w_in, w_branch_att, w_branch_sg, w_out, w_xq, w_xkv, w_xo, w_ffn_in, w_ffn_out)]
    big_m = [a[0] for a in (m_w_in, m_w_branch_att, m_w_branch_sg, m_w_out, m_w_xq, m_w_xkv, m_w_xo, m_w_ffn_in, m_w_ffn_out)]
    big_v = [a[0] for a in (v_w_in, v_w_branch_att, v_w_branch_sg, v_w_out, v_w_xq, v_w_xkv, v_w_xo, v_w_ffn_in, v_w_ffn_out)]

    own = (2 * lax.axis_index("x") + lax.axis_index("y")).astype(jnp.int32).reshape(1)
    core = lax.axis_index("c").astype(jnp.int32).reshape(1)
    placed = [place_shard(w, own, "place_" + nm) for w, nm in zip(big_w, big_names)]

    def whole(g4, i):
        R, C = big_w[i].shape
        return g4.reshape(4, R, C) if col_sharded[i] else g4.reshape(4 * R, C)

    W_in = whole(gather_weights(placed[:1])[0], 0)

    g_mix, g_xat, g_mem, g_ffn = norm_mix_g, norm_xattn_g, norm_mem_g, norm_ffn_g
    g_fin = norm_final_g.reshape(1, D)
    bias = _att_bias_table(rel_bias[0])
    tt = np.arange(SG_BLOCK)
    sg_mask = (tt[None, :] // CHUNK) <= (tt[:, None] // CHUNK)
    wm_f = jnp.where(sg_mask[None], sg_w[0], 0.0)
    pairs = wm_f.reshape(SG_PAIRS, 2, SG_BLOCK, SG_BLOCK)
    wpair = jnp.transpose(pairs, (0, 2, 1, 3)).reshape(SG_PAIRS, SG_BLOCK, 2 * SG_BLOCK).astype(BF16)
    wtpair = jnp.transpose(pairs, (0, 3, 1, 2)).reshape(SG_PAIRS, SG_BLOCK, 2 * SG_BLOCK).astype(BF16)
    maskpair = jnp.asarray(np.tile(sg_mask, (SG_PAIRS, 1, 2)), F32)
    bfull = jnp.repeat(sg_b[0].T, SG_GROUP_DIM, axis=1)
    lng, lnb = sg_ln_g[0].reshape(1, 512), sg_ln_b[0].reshape(1, 512)
    gid = np.arange(512) // SG_GROUP_DIM
    amean = jnp.asarray((gid[:, None] == gid[None, :]) / SG_GROUP_DIM, BF16)

    h1, z, *early = norm_mm(x2d, g_mix, W_in, tm=ROW_TILE, scale=None, name="norm_mm_in",
                            hook=gather_hook(placed[1:7], S // ROW_TILE))
    y_att, lse_att, *late = attn_fwd(z, bias, gather_hook(placed[7:], S // ATT_TILE))
    W_ba, W_bs, W_out, W_xq, W_xkv, W_xo, W_fi, W_fo = [whole(g4, i + 1) for i, g4 in enumerate(early + late)]
    y_sg, = sgu_fwd(z, amean, lng, lnb, wpair, bfull, tm=SGU_TILE)
    a_br, b_br, merged, x1 = merge_fwd(y_att, y_sg, z, x2d, W_ba, W_bs, W_out, tm=ROW_TILE)
    h2, xq = norm_mm(x1, g_xat, W_xq.reshape(1, D, D), tm=XATT_TILE, scale=XATT_HEAD_DIM ** -0.5, name="norm_mm_xq")
    mn, kv = norm_mm(mem2d, g_mem, W_xkv, tm=mem2d.shape[0], scale=None, name="norm_mm_kv")
    o_x, lse_x, x2 = xattn_fwd(xq, kv, W_xo, x1, tm=XATT_TILE)
    h3, gu = norm_mm(x2, g_ffn, W_fi, tm=ROW_TILE, scale=None, name="norm_mm_ffn")
    act, loss_vec, dx3, dg_fin = ffn_out_loss(gu, W_fo, x2, g_fin, tgt, tm=ROW_TILE)

    dgu = ffn_act_bwd(dx3, gu, W_fo, tm=ROW_TILE, nchunk=2)
    gW_fo = tn_mm(act, dx3, shards=1, tn=D, tk=DW_TOKENS, name="dw_ffn_out")
    gW_fi = tn_mm(h3, dgu, shards=4, tn=2 * W_fi.shape[2], tk=DW_TOKENS, name="dw_ffn_in")
    dx2, dg_ffn = mm_nt_norm_bwd(dgu, W_fi, x2, g_ffn, dx3, tm=ROW_TILE, name="dx_ffn")
    dq_x, dkv = xattn_bwd(dx2, xq, o_x, lse_x, kv, W_xo, tm=XATT_TILE)
    tk_wide = min(2 * DW_TOKENS, S)
    gW_xo = tn_mm(o_x, dx2, shards=1, tn=D, tk=tk_wide, name="dw_xo")
    gW_xq = tn_mm(h2, dq_x, shards=1, tn=D, tk=tk_wide, name="dw_xq")
    dx1, dg_xat = mm_nt_norm_bwd(dq_x, W_xq.reshape(1, D, D), x1, g_xat, dx2, tm=XATT_TILE, name="dx_xq")
    gW_xkv = tn_mm(mn, dkv, shards=4, tn=2 * D, tk=mem2d.shape[0], name="dw_xkv")
    _, dg_mem = mm_nt_norm_bwd(dkv.astype(BF16), W_xkv, mem2d, g_mem, None, tm=mem2d.shape[0], name="dx_mem")
    def canon(g, i):
        R, C = big_w[i].shape
        return g.reshape(4, 2, R // 2, C)

    def partials(gs, recv, idx):
        return [chip_partial(g, r, core, "chip_partial_" + big_names[i]) for g, r, i in zip(gs, recv, idx)]

    early_g = [canon(g, i) for g, i in zip([gW_xq, gW_xkv, gW_xo, gW_fi, gW_fo], range(4, 9))]
    da, db, dgab, dy_att, dy_sg, *early_r = merge_bwd(dx1, W_out, a_br, b_br, z, W_ba, W_bs, tm=ROW_TILE,
                                                      hook=split_hook(early_g, S // ROW_TILE))
    gW_out = tn_mm(merged, dx1, shards=1, tn=D, tk=tk_wide, name="dw_out")
    gW_ba = tn_mm(y_att, da, shards=4, tn=D, tk=tk_wide, name="dw_branch_att")
    gW_bs = tn_mm(y_sg, db, shards=4, tn=D, tk=tk_wide, name="dw_branch_sg")
    mid_g = [canon(g, i) for g, i in zip([gW_ba, gW_bs, gW_out], range(1, 4))]
    parts_b = (partials(mid_g, sibling_split(mid_g, "sibling_split_mid"), range(1, 4))
               + partials(early_g, early_r, range(4, 9)))
    nt_bwd = S // ATT_TILE + ATT_WIN - 1
    dqkv, dbias, *from_chips_b = attn_bwd(z, y_att, dy_att, lse_att, bias, exchange_hook(parts_b, nt_bwd))
    sums_b = [shard_sum(p, r, own, core, "shard_sum_" + nm) for p, r, nm in zip(parts_b, from_chips_b, big_names[1:])]
    duv, dwpair, dsgb_full, dlng, dlnb, *joined_b = sgu_bwd(z, dy_sg, amean, lng, lnb, wpair, wtpair, bfull, maskpair,
                                                         tm=SGU_TILE, hook=join_hook(sums_b, S // SGU_TILE))
    dwm = jnp.transpose(dwpair.reshape(SG_PAIRS, SG_BLOCK, 2, SG_BLOCK), (0, 2, 1, 3))
    dsgb = dsgb_full[:, ::SG_GROUP_DIM].T
    dz = [dqkv, duv, dgab]
    gW_in = tn_mm_pieces(h1, dz, shards=4, per=2, tk=DW_TOKENS, name="dw_in")
    in_g = [canon(gW_in, 0)]
    parts_a = partials(in_g, sibling_split(in_g, "sibling_split_w_in"), [0])
    d_rel = _rel_bias_grad(dbias)
    small_g = [d_rel, dlng, dlnb, dwm, dsgb, dg_xat, dg_mem, dg_ffn, dg_fin, loss_vec[0, :1]]
    n_dx = S // ROW_TILE
    dx, dg_mix, from_chips_a, small_all = mm_nt_norm_bwd(
        dz, W_in, x2d, g_mix, dx1, tm=ROW_TILE, name="dx_in",
        hook=merge_hooks(exchange_hook(parts_a, n_dx), allgather_hook(_pack_small(small_g), n_dx)))

    joined = list(sibling_join([shard_sum(parts_a[0], from_chips_a, own, core, "shard_sum_w_in")])) + list(joined_b)
    big_out = []
    for j, w, m_, v_, nm in zip(joined, big_w, big_m, big_v, big_names):
        g = j.reshape(w.shape)
        big_out.append((g,) + tuple(adamw(w, g, m_, v_, "adamw_" + nm)))

    small_w = [norm_mix_g, rel_bias, sg_ln_g, sg_ln_b, sg_w, sg_b, norm_xattn_g, norm_mem_g, norm_ffn_g, norm_final_g]
    small_m = [m_norm_mix_g, m_rel_bias, m_sg_ln_g, m_sg_ln_b, m_sg_w, m_sg_b, m_norm_xattn_g, m_norm_mem_g, m_norm_ffn_g, m_norm_final_g]
    small_v = [v_norm_mix_g, v_rel_bias, v_sg_ln_g, v_sg_ln_b, v_sg_w, v_sg_b, v_norm_xattn_g, v_norm_mem_g, v_norm_ffn_g, v_norm_final_g]
    shapes = [w.shape for w in small_w]
    g_sum = jnp.concatenate([allreduce_small(_pack_small([dg_mix])), sum_blocks(small_all)], axis=0)
    zero = jnp.zeros((1,), F32)
    d_s, m_s, v_s = adamw(_pack_small(small_w + [zero]), g_sum, _pack_small(small_m + [zero]), _pack_small(small_v + [zero]),
                          "adamw_small")
    sg_, sd, sm, sv_ = (_unpack_small(p, shapes + [(1,)]) for p in (g_sum, d_s, m_s, v_s))
    loss = sg_[-1][0]

    order = ["norm_mix_g", "w_in", "rel_bias", "sg_ln_g", "sg_ln_b", "sg_w", "sg_b", "w_branch_att", "w_branch_sg", "w_out",
             "norm_xattn_g", "norm_mem_g", "w_xq", "w_xkv", "w_xo", "norm_ffn_g", "w_ffn_in", "w_ffn_out", "norm_final_g"]
    small_names = ["norm_mix_g", "rel_bias", "sg_ln_g", "sg_ln_b", "sg_w", "sg_b", "norm_xattn_g", "norm_mem_g", "norm_ffn_g",
                   "norm_final_g"]
    res = {}
    for i, nm in enumerate(small_names):
        res[nm] = (sg_[i], sd[i], sm[i], sv_[i])
    for nm, outs in zip(big_names, big_out):
        res[nm] = tuple(o[None] for o in outs)
    return (loss, dx[None], *[res[nm][0] for nm in order], *[res[nm][1] for nm in order],
            *[res[nm][2] for nm in order], *[res[nm][3] for nm in order])
```
